```python
import math
import jax, jax.numpy as jnp
from jax import lax
import numpy as np

D_MODEL = 2048
BATCH = 8
SEQ = 4096
DEPTH = 1

D_INNER_A = 3 * D_MODEL // 2
HEAD_DIM_A = 64
N_HEADS_A = D_INNER_A // HEAD_DIM_A
N_GROUPS_A = 8
HEADS_PER_GROUP = N_HEADS_A // N_GROUPS_A
D_STATE_A = 128
CONV_A = 4
CHUNK = 256
CONV_DIM_A = D_INNER_A + 2 * N_GROUPS_A * D_STATE_A
D_S5 = D_MODEL // 2
S5_GROUP = 16
N_GROUPS_S5 = D_S5 // S5_GROUP
S5_STATE = 64
D_FF = 256 * ((8 * D_MODEL // 3 + 255) // 256)
CONV_FFN = 3
EPS = 1e-6
DT_MIN = 1e-3
DT_MAX = 1e-1
EIG_MAX = -1e-4
IN_COLS = D_INNER_A + CONV_DIM_A + N_HEADS_A + D_S5 + 2 * D_MODEL

kernel_name = "hybrid_ssd_s5_gated_convffn"


def rmsnorm(x, w):
    xf = x.astype(jnp.float32)
    xf = xf * lax.rsqrt(jnp.mean(xf * xf, axis=-1, keepdims=True) + EPS)
    return xf.astype(x.dtype) * w


def causal_dwconv(x, w, b):
    k = w.shape[0]
    y = lax.conv_general_dilated(
        x, w[:, None, :].astype(x.dtype), window_strides=(1,), padding=[(k - 1, 0)],
        dimension_numbers=("NWC", "WIO", "NWC"), feature_group_count=x.shape[-1])
    return y + b


def segsum(a):
    t = a.shape[-1]
    ar = jnp.broadcast_to(a[..., :, None], a.shape + (t,))
    strict = jnp.tril(jnp.ones((t, t), dtype=bool), k=-1)
    cs = jnp.cumsum(jnp.where(strict, ar, 0.0), axis=-2)
    incl = jnp.tril(jnp.ones((t, t), dtype=bool), k=0)
    return jnp.where(incl, cs, -jnp.inf)


def ssd_chunked(xh, da, bm, cm):
    b, seqlen = xh.shape[:2]
    nc = -(-seqlen // CHUNK)
    pad = nc * CHUNK - seqlen
    if pad:
        xh = jnp.pad(xh, ((0, 0), (0, pad), (0, 0), (0, 0), (0, 0)))
        da = jnp.pad(da, ((0, 0), (0, pad), (0, 0), (0, 0)))
        bm = jnp.pad(bm, ((0, 0), (0, pad), (0, 0), (0, 0)))
        cm = jnp.pad(cm, ((0, 0), (0, pad), (0, 0), (0, 0)))
    X = xh.reshape(b, nc, CHUNK, N_GROUPS_A, HEADS_PER_GROUP, HEAD_DIM_A)
    A = da.reshape(b, nc, CHUNK, N_GROUPS_A, HEADS_PER_GROUP).transpose(0, 3, 4, 1, 2)
    Bc = bm.reshape(b, nc, CHUNK, N_GROUPS_A, D_STATE_A)
    Cc = cm.reshape(b, nc, CHUNK, N_GROUPS_A, D_STATE_A)
    a_cs = jnp.cumsum(A, axis=-1)
    decay_in = jnp.exp(segsum(A))
    cb = jnp.einsum("bclgn,bcsgn->bgcls", Cc, Bc)
    y_diag = jnp.einsum("bgcls,bgrcls,bcsgrp->bclgrp", cb, decay_in, X)
    decay_states = jnp.exp(a_cs[..., -1:] - a_cs)
    states = jnp.einsum("bclgn,bgrcl,bclgrp->bcgrpn", Bc, decay_states, X)
    states = jnp.concatenate([jnp.zeros_like(states[:, :1]), states], axis=1)
    chunk_decay = jnp.exp(segsum(jnp.pad(a_cs[..., -1], ((0, 0), (0, 0), (0, 0), (1, 0)))))
    new_states = jnp.einsum("bgrzc,bcgrpn->bzgrpn", chunk_decay, states)
    states = new_states[:, :-1]
    y_off = jnp.einsum("bclgn,bcgrpn,bgrcl->bclgrp", Cc, states, jnp.exp(a_cs))
    y = (y_diag + y_off).reshape(b, nc * CHUNK, N_GROUPS_A, HEADS_PER_GROUP, HEAD_DIM_A)
    return y[:, :seqlen]


def mamba2_mixer(z, xbc, dt_raw, conv_w, conv_b, dt_bias, a_log, d_skip, norm_w, w_proj):
    b, seqlen, _ = z.shape
    f32 = jnp.float32
    xbc = jax.nn.silu(causal_dwconv(xbc, conv_w, conv_b))
    gn = N_GROUPS_A * D_STATE_A
    xs, bm, cm = jnp.split(xbc, [D_INNER_A, D_INNER_A + gn], axis=-1)
    dt = jax.nn.softplus(dt_raw.astype(f32) + dt_bias.astype(f32))
    a = -jnp.exp(a_log.astype(f32))
    dt_g = dt.reshape(b, seqlen, N_GROUPS_A, HEADS_PER_GROUP)
    xh = xs.astype(f32).reshape(b, seqlen, N_GROUPS_A, HEADS_PER_GROUP, HEAD_DIM_A)
    y = ssd_chunked(xh * dt_g[..., None],
                    dt_g * a.reshape(N_GROUPS_A, HEADS_PER_GROUP),
                    bm.astype(f32).reshape(b, seqlen, N_GROUPS_A, D_STATE_A),
                    cm.astype(f32).reshape(b, seqlen, N_GROUPS_A, D_STATE_A))
    y = y + d_skip.astype(f32).reshape(N_GROUPS_A, HEADS_PER_GROUP)[:, :, None] * xh
    y = y.reshape(b, seqlen, D_INNER_A) * jax.nn.silu(z.astype(f32))
    yg = y.reshape(b, seqlen, N_GROUPS_A, D_INNER_A // N_GROUPS_A)
    yg = yg * lax.rsqrt(jnp.mean(yg * yg, axis=-1, keepdims=True) + EPS)
    y = yg.reshape(b, seqlen, D_INNER_A).astype(z.dtype) * norm_w
    return y @ w_proj


def _lin_rec(e1, e2):
    a1, b1 = e1
    a2, b2 = e2
    return a1 * a2, a2 * b1 + b2


def s5_mixer(u, lam_re, lam_im, log_dt, b_re, b_im, c_re, c_im, d_skip, w_glu):
    b, seqlen, _ = u.shape
    f32 = jnp.float32
    ug = u.astype(f32).reshape(b, seqlen, N_GROUPS_S5, S5_GROUP)
    lam = lax.complex(jnp.minimum(lam_re.astype(f32), EIG_MAX), lam_im.astype(f32))
    dt = jnp.exp(log_dt.astype(f32))[:, None]
    lam_bar = jnp.exp(lam * dt)
    bmat = lax.complex(b_re.astype(f32), b_im.astype(f32))
    cmat = lax.complex(c_re.astype(f32), c_im.astype(f32))
    b_bar = ((lam_bar - 1.0) / lam)[..., None] * bmat
    bu = jnp.einsum("gpc,blgc->blgp", b_bar, ug.astype(jnp.complex64))
    a_el = jnp.broadcast_to(lam_bar, bu.shape)
    _, states = lax.associative_scan(_lin_rec, (a_el, bu), axis=1)
    y = jnp.einsum("gcp,blgp->blgc", cmat, states).real + d_skip.astype(f32).reshape(N_GROUPS_S5, S5_GROUP) * ug
    y = jax.nn.gelu(y.reshape(b, seqlen, D_S5)).astype(u.dtype)
    val, gate = jnp.split(y @ w_glu, 2, axis=-1)
    return val * jax.nn.sigmoid(gate)


def conv_glu_ffn(hn, w_up, conv_w, conv_b, w_down):
    up = causal_dwconv(hn @ w_up, conv_w, conv_b)
    gate, val = jnp.split(up, 2, axis=-1)
    return (jax.nn.silu(gate) * val) @ w_down


def _fwd_setup_inputs(seed: int = 0) -> dict:
    key = jax.random.key(seed)
    ks = jax.random.split(key, 28)
    f32 = jnp.float32
    L = DEPTH

    def nrm(k, shape, scale):
        return jax.random.normal(k, shape, f32) * scale

    log_lo, log_hi = math.log(DT_MIN), math.log(DT_MAX)
    dt0 = jnp.exp(jax.random.uniform(ks[6], (L, N_HEADS_A), f32, log_lo, log_hi))
    dt_bias = dt0 + jnp.log(-jnp.expm1(-dt0))
    lam_im0 = math.pi * jnp.arange(S5_STATE, dtype=f32)
    return {
        "x": nrm(ks[0], (BATCH, SEQ, D_MODEL), 1.0),
        "norm_mix_w": 1.0 + nrm(ks[1], (L, D_MODEL), 0.02),
        "w_in": nrm(ks[2], (L, D_MODEL, IN_COLS), D_MODEL ** -0.5),
        "conv_a_w": nrm(ks[3], (L, CONV_A, CONV_DIM_A), CONV_A ** -0.5),
        "conv_a_b": nrm(ks[4], (L, CONV_DIM_A), 0.01),
        "dt_bias": dt_bias,
        "a_log": jnp.log(jax.random.uniform(ks[7], (L, N_HEADS_A), f32, 1.0, 16.0)),
        "d_a": 1.0 + nrm(ks[8], (L, N_HEADS_A), 0.02),
        "norm_a_w": 1.0 + nrm(ks[9], (L, D_INNER_A), 0.02),
        "w_proj_a": nrm(ks[10], (L, D_INNER_A, D_MODEL), D_INNER_A ** -0.5),
        "s5_lam_re": -0.5 + nrm(ks[11], (L, N_GROUPS_S5, S5_STATE), 0.01),
        "s5_lam_im": lam_im0 + nrm(ks[12], (L, N_GROUPS_S5, S5_STATE), 0.01),
        "s5_log_dt": jax.random.uniform(ks[13], (L, N_GROUPS_S5), f32, log_lo, log_hi),
        "s5_b_re": nrm(ks[14], (L, N_GROUPS_S5, S5_STATE, S5_GROUP), (2 * S5_GROUP) ** -0.5),
        "s5_b_im": nrm(ks[15], (L, N_GROUPS_S5, S5_STATE, S5_GROUP), (2 * S5_GROUP) ** -0.5),
        "s5_c_re": nrm(ks[16], (L, N_GROUPS_S5, S5_GROUP, S5_STATE), S5_STATE ** -0.5),
        "s5_c_im": nrm(ks[17], (L, N_GROUPS_S5, S5_GROUP, S5_STATE), S5_STATE ** -0.5),
        "s5_d": nrm(ks[18], (L, D_S5), 1.0),
        "w_s5_glu": nrm(ks[19], (L, D_S5, 2 * D_MODEL), D_S5 ** -0.5),
        "w_out": nrm(ks[20], (L, D_MODEL, D_MODEL), D_MODEL ** -0.5),
        "norm_ffn_w": 1.0 + nrm(ks[21], (L, D_MODEL), 0.02),
        "w_up": nrm(ks[22], (L, D_MODEL, 2 * D_FF), D_MODEL ** -0.5),
        "conv_ffn_w": nrm(ks[23], (L, CONV_FFN, 2 * D_FF), CONV_FFN ** -0.5),
        "conv_ffn_b": nrm(ks[24], (L, 2 * D_FF), 0.01),
        "w_down": nrm(ks[25], (L, D_FF, D_MODEL), D_FF ** -0.5),
        "norm_final_w": 1.0 + nrm(ks[26], (D_MODEL,), 0.02),
    }


def _fwd_reference(x, norm_mix_w, w_in, conv_a_w, conv_a_b, dt_bias, a_log, d_a, norm_a_w, w_proj_a,
              s5_lam_re, s5_lam_im, s5_log_dt, s5_b_re, s5_b_im, s5_c_re, s5_c_im, s5_d, w_s5_glu,
              w_out, norm_ffn_w, w_up, conv_ffn_w, conv_ffn_b, w_down, norm_final_w):
    sizes = [D_INNER_A, CONV_DIM_A, N_HEADS_A, D_S5, D_MODEL, D_MODEL]
    splits = [int(s) for s in np.cumsum(sizes)[:-1]]
    h = x
    for i in range(DEPTH):
        hn = rmsnorm(h, norm_mix_w[i])
        z, xbc, dt_raw, u, g_a, g_b = jnp.split(hn @ w_in[i], splits, axis=-1)
        y_a = mamba2_mixer(z, xbc, dt_raw, conv_a_w[i], conv_a_b[i], dt_bias[i], a_log[i], d_a[i],
                           norm_a_w[i], w_proj_a[i])
        y_b = s5_mixer(u, s5_lam_re[i], s5_lam_im[i], s5_log_dt[i], s5_b_re[i], s5_b_im[i],
                       s5_c_re[i], s5_c_im[i], s5_d[i], w_s5_glu[i])
        merged = jax.nn.sigmoid(g_a) * y_a + jax.nn.sigmoid(g_b) * y_b
        h = h + merged @ w_out[i]
        hn = rmsnorm(h, norm_ffn_w[i])
        h = h + conv_glu_ffn(hn, w_up[i], conv_ffn_w[i], conv_ffn_b[i], w_down[i])
    return rmsnorm(h, norm_final_w)


import jax as _jax
import jax.numpy as _jnp

TWIN_FORMAT = 'train_step'
FWD_PARAMS = ['x', 'norm_mix_w', 'w_in', 'conv_a_w', 'conv_a_b', 'dt_bias', 'a_log', 'd_a', 'norm_a_w', 'w_proj_a', 's5_lam_re', 's5_lam_im', 's5_log_dt', 's5_b_re', 's5_b_im', 's5_c_re', 's5_c_im', 's5_d', 'w_s5_glu', 'w_out', 'norm_ffn_w', 'w_up', 'conv_ffn_w', 'conv_ffn_b', 'w_down', 'norm_final_w']
TWIN_WEIGHTS = ['norm_mix_w', 'w_in', 'conv_a_w', 'conv_a_b', 'dt_bias', 'a_log', 'd_a', 'norm_a_w', 'w_proj_a', 's5_lam_re', 's5_lam_im', 's5_log_dt', 's5_b_re', 's5_b_im', 's5_c_re', 's5_c_im', 's5_d', 'w_s5_glu', 'w_out', 'norm_ffn_w', 'w_up', 'conv_ffn_w', 'conv_ffn_b', 'w_down', 'norm_final_w']
TWIN_DIFF_INPUT = 'x'
TWIN_INPUTS = ['x', 'norm_mix_w', 'w_in', 'conv_a_w', 'conv_a_b', 'dt_bias', 'a_log', 'd_a', 'norm_a_w', 'w_proj_a', 's5_lam_re', 's5_lam_im', 's5_log_dt', 's5_b_re', 's5_b_im', 's5_c_re', 's5_c_im', 's5_d', 'w_s5_glu', 'w_out', 'norm_ffn_w', 'w_up', 'conv_ffn_w', 'conv_ffn_b', 'w_down', 'norm_final_w', 'loss_target', 'm_norm_mix_w', 'm_w_in', 'm_conv_a_w', 'm_conv_a_b', 'm_dt_bias', 'm_a_log', 'm_d_a', 'm_norm_a_w', 'm_w_proj_a', 'm_s5_lam_re', 'm_s5_lam_im', 'm_s5_log_dt', 'm_s5_b_re', 'm_s5_b_im', 'm_s5_c_re', 'm_s5_c_im', 'm_s5_d', 'm_w_s5_glu', 'm_w_out', 'm_norm_ffn_w', 'm_w_up', 'm_conv_ffn_w', 'm_conv_ffn_b', 'm_w_down', 'm_norm_final_w', 'v_norm_mix_w', 'v_w_in', 'v_conv_a_w', 'v_conv_a_b', 'v_dt_bias', 'v_a_log', 'v_d_a', 'v_norm_a_w', 'v_w_proj_a', 'v_s5_lam_re', 'v_s5_lam_im', 'v_s5_log_dt', 'v_s5_b_re', 'v_s5_b_im', 'v_s5_c_re', 'v_s5_c_im', 'v_s5_d', 'v_w_s5_glu', 'v_w_out', 'v_norm_ffn_w', 'v_w_up', 'v_conv_ffn_w', 'v_conv_ffn_b', 'v_w_down', 'v_norm_final_w']
TWIN_OUTPUTS = ['loss', 'grad_x', 'grad_norm_mix_w', 'grad_w_in', 'grad_conv_a_w', 'grad_conv_a_b', 'grad_dt_bias', 'grad_a_log', 'grad_d_a', 'grad_norm_a_w', 'grad_w_proj_a', 'grad_s5_lam_re', 'grad_s5_lam_im', 'grad_s5_log_dt', 'grad_s5_b_re', 'grad_s5_b_im', 'grad_s5_c_re', 'grad_s5_c_im', 'grad_s5_d', 'grad_w_s5_glu', 'grad_w_out', 'grad_norm_ffn_w', 'grad_w_up', 'grad_conv_ffn_w', 'grad_conv_ffn_b', 'grad_w_down', 'grad_norm_final_w', 'delta_norm_mix_w', 'delta_w_in', 'delta_conv_a_w', 'delta_conv_a_b', 'delta_dt_bias', 'delta_a_log', 'delta_d_a', 'delta_norm_a_w', 'delta_w_proj_a', 'delta_s5_lam_re', 'delta_s5_lam_im', 'delta_s5_log_dt', 'delta_s5_b_re', 'delta_s5_b_im', 'delta_s5_c_re', 'delta_s5_c_im', 'delta_s5_d', 'delta_w_s5_glu', 'delta_w_out', 'delta_norm_ffn_w', 'delta_w_up', 'delta_conv_ffn_w', 'delta_conv_ffn_b', 'delta_w_down', 'delta_norm_final_w', 'new_m_norm_mix_w', 'new_m_w_in', 'new_m_conv_a_w', 'new_m_conv_a_b', 'new_m_dt_bias', 'new_m_a_log', 'new_m_d_a', 'new_m_norm_a_w', 'new_m_w_proj_a', 'new_m_s5_lam_re', 'new_m_s5_lam_im', 'new_m_s5_log_dt', 'new_m_s5_b_re', 'new_m_s5_b_im', 'new_m_s5_c_re', 'new_m_s5_c_im', 'new_m_s5_d', 'new_m_w_s5_glu', 'new_m_w_out', 'new_m_norm_ffn_w', 'new_m_w_up', 'new_m_conv_ffn_w', 'new_m_conv_ffn_b', 'new_m_w_down', 'new_m_norm_final_w', 'new_v_norm_mix_w', 'new_v_w_in', 'new_v_conv_a_w', 'new_v_conv_a_b', 'new_v_dt_bias', 'new_v_a_log', 'new_v_d_a', 'new_v_norm_a_w', 'new_v_w_proj_a', 'new_v_s5_lam_re', 'new_v_s5_lam_im', 'new_v_s5_log_dt', 'new_v_s5_b_re', 'new_v_s5_b_im', 'new_v_s5_c_re', 'new_v_s5_c_im', 'new_v_s5_d', 'new_v_w_s5_glu', 'new_v_w_out', 'new_v_norm_ffn_w', 'new_v_w_up', 'new_v_conv_ffn_w', 'new_v_conv_ffn_b', 'new_v_w_down', 'new_v_norm_final_w']
TWIN_LEAF_KINDS = {'loss': 'loss', 'grad_x': 'grad_x', 'grad_norm_mix_w': 'grad_w', 'grad_w_in': 'grad_w', 'grad_conv_a_w': 'grad_w', 'grad_conv_a_b': 'grad_w', 'grad_dt_bias': 'grad_w', 'grad_a_log': 'grad_w', 'grad_d_a': 'grad_w', 'grad_norm_a_w': 'grad_w', 'grad_w_proj_a': 'grad_w', 'grad_s5_lam_re': 'grad_w', 'grad_s5_lam_im': 'grad_w', 'grad_s5_log_dt': 'grad_w', 'grad_s5_b_re': 'grad_w', 'grad_s5_b_im': 'grad_w', 'grad_s5_c_re': 'grad_w', 'grad_s5_c_im': 'grad_w', 'grad_s5_d': 'grad_w', 'grad_w_s5_glu': 'grad_w', 'grad_w_out': 'grad_w', 'grad_norm_ffn_w': 'grad_w', 'grad_w_up': 'grad_w', 'grad_conv_ffn_w': 'grad_w', 'grad_conv_ffn_b': 'grad_w', 'grad_w_down': 'grad_w', 'grad_norm_final_w': 'grad_w', 'delta_norm_mix_w': 'delta_w', 'delta_w_in': 'delta_w', 'delta_conv_a_w': 'delta_w', 'delta_conv_a_b': 'delta_w', 'delta_dt_bias': 'delta_w', 'delta_a_log': 'delta_w', 'delta_d_a': 'delta_w', 'delta_norm_a_w': 'delta_w', 'delta_w_proj_a': 'delta_w', 'delta_s5_lam_re': 'delta_w', 'delta_s5_lam_im': 'delta_w', 'delta_s5_log_dt': 'delta_w', 'delta_s5_b_re': 'delta_w', 'delta_s5_b_im': 'delta_w', 'delta_s5_c_re': 'delta_w', 'delta_s5_c_im': 'delta_w', 'delta_s5_d': 'delta_w', 'delta_w_s5_glu': 'delta_w', 'delta_w_out': 'delta_w', 'delta_norm_ffn_w': 'delta_w', 'delta_w_up': 'delta_w', 'delta_conv_ffn_w': 'delta_w', 'delta_conv_ffn_b': 'delta_w', 'delta_w_down': 'delta_w', 'delta_norm_final_w': 'delta_w', 'new_m_norm_mix_w': 'new_m', 'new_m_w_in': 'new_m', 'new_m_conv_a_w': 'new_m', 'new_m_conv_a_b': 'new_m', 'new_m_dt_bias': 'new_m', 'new_m_a_log': 'new_m', 'new_m_d_a': 'new_m', 'new_m_norm_a_w': 'new_m', 'new_m_w_proj_a': 'new_m', 'new_m_s5_lam_re': 'new_m', 'new_m_s5_lam_im': 'new_m', 'new_m_s5_log_dt': 'new_m', 'new_m_s5_b_re': 'new_m', 'new_m_s5_b_im': 'new_m', 'new_m_s5_c_re': 'new_m', 'new_m_s5_c_im': 'new_m', 'new_m_s5_d': 'new_m', 'new_m_w_s5_glu': 'new_m', 'new_m_w_out': 'new_m', 'new_m_norm_ffn_w': 'new_m', 'new_m_w_up': 'new_m', 'new_m_conv_ffn_w': 'new_m', 'new_m_conv_ffn_b': 'new_m', 'new_m_w_down': 'new_m', 'new_m_norm_final_w': 'new_m', 'new_v_norm_mix_w': 'new_v', 'new_v_w_in': 'new_v', 'new_v_conv_a_w': 'new_v', 'new_v_conv_a_b': 'new_v', 'new_v_dt_bias': 'new_v', 'new_v_a_log': 'new_v', 'new_v_d_a': 'new_v', 'new_v_norm_a_w': 'new_v', 'new_v_w_proj_a': 'new_v', 'new_v_s5_lam_re': 'new_v', 'new_v_s5_lam_im': 'new_v', 'new_v_s5_log_dt': 'new_v', 'new_v_s5_b_re': 'new_v', 'new_v_s5_b_im': 'new_v', 'new_v_s5_c_re': 'new_v', 'new_v_s5_c_im': 'new_v', 'new_v_s5_d': 'new_v', 'new_v_w_s5_glu': 'new_v', 'new_v_w_out': 'new_v', 'new_v_norm_ffn_w': 'new_v', 'new_v_w_up': 'new_v', 'new_v_conv_ffn_w': 'new_v', 'new_v_conv_ffn_b': 'new_v', 'new_v_w_down': 'new_v', 'new_v_norm_final_w': 'new_v'}


def _forward(args):
    return _fwd_reference(*[args[k] for k in FWD_PARAMS])


def _output_shape():
    def fwd():
        inp = _fwd_setup_inputs(0)
        return _fwd_reference(*[inp[k] for k in FWD_PARAMS])
    out = _jax.eval_shape(fwd)
    return out.shape, out.dtype

N_MICROBATCH = 1
ADAM_LR = 0.001
ADAM_B1 = 0.9
ADAM_B2 = 0.999
ADAM_EPS = 1e-08
ADAM_WD = 0.01
ADAM_STEP = 10
PER_EXAMPLE_BATCH_AXIS = {'x': 0, 'loss_target': 0}
SHARED_INPUTS = []
_WEIGHT_DTYPES = {'norm_mix_w': _jnp.float32, 'w_in': _jnp.float32, 'conv_a_w': _jnp.float32, 'conv_a_b': _jnp.float32, 'dt_bias': _jnp.float32, 'a_log': _jnp.float32, 'd_a': _jnp.float32, 'norm_a_w': _jnp.float32, 'w_proj_a': _jnp.float32, 's5_lam_re': _jnp.float32, 's5_lam_im': _jnp.float32, 's5_log_dt': _jnp.float32, 's5_b_re': _jnp.float32, 's5_b_im': _jnp.float32, 's5_c_re': _jnp.float32, 's5_c_im': _jnp.float32, 's5_d': _jnp.float32, 'w_s5_glu': _jnp.float32, 'w_out': _jnp.float32, 'norm_ffn_w': _jnp.float32, 'w_up': _jnp.float32, 'conv_ffn_w': _jnp.float32, 'conv_ffn_b': _jnp.float32, 'w_down': _jnp.float32, 'norm_final_w': _jnp.float32}
MOMENT_SCALE = {'norm_mix_w': 7.537330e-02, 'w_in': 2.906691e-02, 'conv_a_w': 3.174459e-02, 'conv_a_b': 4.359518e-02, 'dt_bias': 8.132294e-02, 'a_log': 9.780840e-02, 'd_a': 1.939308e-01, 'norm_a_w': 3.859061e-02, 'w_proj_a': 4.644204e-02, 's5_lam_re': 1.769303e-03, 's5_lam_im': 1.805441e-03, 's5_log_dt': 1.218291e+00, 's5_b_re': 1.166789e-03, 's5_b_im': 1.142427e-03, 's5_c_re': 1.644470e-03, 's5_c_im': 1.611235e-03, 's5_d': 2.409343e-02, 'w_s5_glu': 1.214887e-02, 'w_out': 4.928101e-02, 'norm_ffn_w': 6.109304e-02, 'w_up': 2.560116e-02, 'conv_ffn_w': 2.561757e-02, 'conv_ffn_b': 2.573333e-02, 'w_down': 4.178833e-02, 'norm_final_w': 1.600315e+01}


def _to_microbatches(a, axis):
    t = _jnp.moveaxis(a, axis, 0)
    t = t.reshape((N_MICROBATCH, t.shape[0] // N_MICROBATCH) + t.shape[1:])
    return _jnp.moveaxis(t, 1, axis + 1)


def setup_inputs(seed: int = 0) -> dict:
    inp = _fwd_setup_inputs(seed)
    key = _jax.random.fold_in(_jax.random.key(seed), 7919)
    shape, _ = _output_shape()
    out = dict(inp)
    out["loss_target"] = _jax.random.normal(_jax.random.fold_in(key, 0), shape, _jnp.float32)
    for i, name in enumerate(TWIN_WEIGHTS):
        w = inp[name].astype(_jnp.float32)
        if MOMENT_SCALE is None:
            s = _jnp.sqrt(_jnp.mean(_jnp.square(w)) + 1e-30)
        else:
            s = MOMENT_SCALE[name]
        km, kv = _jax.random.split(_jax.random.fold_in(key, i + 1))
        out[name] = w
        out["m_" + name] = s * _jax.random.normal(km, w.shape, _jnp.float32)
        out["v_" + name] = (s * s) * _jax.random.uniform(kv, w.shape, _jnp.float32, 0.5, 1.5)
    if N_MICROBATCH > 1:
        for name, axis in PER_EXAMPLE_BATCH_AXIS.items():
            out[name] = _to_microbatches(out[name], axis)
    return {'x': out['x'], 'norm_mix_w': out['norm_mix_w'], 'w_in': out['w_in'], 'conv_a_w': out['conv_a_w'], 'conv_a_b': out['conv_a_b'], 'dt_bias': out['dt_bias'], 'a_log': out['a_log'], 'd_a': out['d_a'], 'norm_a_w': out['norm_a_w'], 'w_proj_a': out['w_proj_a'], 's5_lam_re': out['s5_lam_re'], 's5_lam_im': out['s5_lam_im'], 's5_log_dt': out['s5_log_dt'], 's5_b_re': out['s5_b_re'], 's5_b_im': out['s5_b_im'], 's5_c_re': out['s5_c_re'], 's5_c_im': out['s5_c_im'], 's5_d': out['s5_d'], 'w_s5_glu': out['w_s5_glu'], 'w_out': out['w_out'], 'norm_ffn_w': out['norm_ffn_w'], 'w_up': out['w_up'], 'conv_ffn_w': out['conv_ffn_w'], 'conv_ffn_b': out['conv_ffn_b'], 'w_down': out['w_down'], 'norm_final_w': out['norm_final_w'], 'loss_target': out['loss_target'], 'm_norm_mix_w': out['m_norm_mix_w'], 'm_w_in': out['m_w_in'], 'm_conv_a_w': out['m_conv_a_w'], 'm_conv_a_b': out['m_conv_a_b'], 'm_dt_bias': out['m_dt_bias'], 'm_a_log': out['m_a_log'], 'm_d_a': out['m_d_a'], 'm_norm_a_w': out['m_norm_a_w'], 'm_w_proj_a': out['m_w_proj_a'], 'm_s5_lam_re': out['m_s5_lam_re'], 'm_s5_lam_im': out['m_s5_lam_im'], 'm_s5_log_dt': out['m_s5_log_dt'], 'm_s5_b_re': out['m_s5_b_re'], 'm_s5_b_im': out['m_s5_b_im'], 'm_s5_c_re': out['m_s5_c_re'], 'm_s5_c_im': out['m_s5_c_im'], 'm_s5_d': out['m_s5_d'], 'm_w_s5_glu': out['m_w_s5_glu'], 'm_w_out': out['m_w_out'], 'm_norm_ffn_w': out['m_norm_ffn_w'], 'm_w_up': out['m_w_up'], 'm_conv_ffn_w': out['m_conv_ffn_w'], 'm_conv_ffn_b': out['m_conv_ffn_b'], 'm_w_down': out['m_w_down'], 'm_norm_final_w': out['m_norm_final_w'], 'v_norm_mix_w': out['v_norm_mix_w'], 'v_w_in': out['v_w_in'], 'v_conv_a_w': out['v_conv_a_w'], 'v_conv_a_b': out['v_conv_a_b'], 'v_dt_bias': out['v_dt_bias'], 'v_a_log': out['v_a_log'], 'v_d_a': out['v_d_a'], 'v_norm_a_w': out['v_norm_a_w'], 'v_w_proj_a': out['v_w_proj_a'], 'v_s5_lam_re': out['v_s5_lam_re'], 'v_s5_lam_im': out['v_s5_lam_im'], 'v_s5_log_dt': out['v_s5_log_dt'], 'v_s5_b_re': out['v_s5_b_re'], 'v_s5_b_im': out['v_s5_b_im'], 'v_s5_c_re': out['v_s5_c_re'], 'v_s5_c_im': out['v_s5_c_im'], 'v_s5_d': out['v_s5_d'], 'v_w_s5_glu': out['v_w_s5_glu'], 'v_w_out': out['v_w_out'], 'v_norm_ffn_w': out['v_norm_ffn_w'], 'v_w_up': out['v_w_up'], 'v_conv_ffn_w': out['v_conv_ffn_w'], 'v_conv_ffn_b': out['v_conv_ffn_b'], 'v_w_down': out['v_w_down'], 'v_norm_final_w': out['v_norm_final_w']}


def _loss(weights, diff, rest, loss_target):
    with _jax.named_scope("forward"):
        args = {**rest, TWIN_DIFF_INPUT: diff, **{k: w.astype(_WEIGHT_DTYPES[k]) for k, w in weights.items()}}
        y = _forward(args)
    with _jax.named_scope("loss_head"):
        err = _jnp.square(y.astype(_jnp.float32) - loss_target)
        return 0.5 * _jnp.sum(_jnp.mean(err, axis=-1)) if err.ndim else 0.5 * err


def _adamw(w, g, m, v):
    m = ADAM_B1 * m + (1.0 - ADAM_B1) * g
    v = ADAM_B2 * v + (1.0 - ADAM_B2) * _jnp.square(g)
    m_hat = m / (1.0 - ADAM_B1 ** ADAM_STEP)
    v_hat = v / (1.0 - ADAM_B2 ** ADAM_STEP)
    delta = -ADAM_LR * (m_hat / (_jnp.sqrt(v_hat) + ADAM_EPS) + ADAM_WD * w)
    return delta, m, v


def reference(x, norm_mix_w, w_in, conv_a_w, conv_a_b, dt_bias, a_log, d_a, norm_a_w, w_proj_a, s5_lam_re, s5_lam_im, s5_log_dt, s5_b_re, s5_b_im, s5_c_re, s5_c_im, s5_d, w_s5_glu, w_out, norm_ffn_w, w_up, conv_ffn_w, conv_ffn_b, w_down, norm_final_w, loss_target, m_norm_mix_w, m_w_in, m_conv_a_w, m_conv_a_b, m_dt_bias, m_a_log, m_d_a, m_norm_a_w, m_w_proj_a, m_s5_lam_re, m_s5_lam_im, m_s5_log_dt, m_s5_b_re, m_s5_b_im, m_s5_c_re, m_s5_c_im, m_s5_d, m_w_s5_glu, m_w_out, m_norm_ffn_w, m_w_up, m_conv_ffn_w, m_conv_ffn_b, m_w_down, m_norm_final_w, v_norm_mix_w, v_w_in, v_conv_a_w, v_conv_a_b, v_dt_bias, v_a_log, v_d_a, v_norm_a_w, v_w_proj_a, v_s5_lam_re, v_s5_lam_im, v_s5_log_dt, v_s5_b_re, v_s5_b_im, v_s5_c_re, v_s5_c_im, v_s5_d, v_w_s5_glu, v_w_out, v_norm_ffn_w, v_w_up, v_conv_ffn_w, v_conv_ffn_b, v_w_down, v_norm_final_w):
    given = dict(x=x, norm_mix_w=norm_mix_w, w_in=w_in, conv_a_w=conv_a_w, conv_a_b=conv_a_b, dt_bias=dt_bias, a_log=a_log, d_a=d_a, norm_a_w=norm_a_w, w_proj_a=w_proj_a, s5_lam_re=s5_lam_re, s5_lam_im=s5_lam_im, s5_log_dt=s5_log_dt, s5_b_re=s5_b_re, s5_b_im=s5_b_im, s5_c_re=s5_c_re, s5_c_im=s5_c_im, s5_d=s5_d, w_s5_glu=w_s5_glu, w_out=w_out, norm_ffn_w=norm_ffn_w, w_up=w_up, conv_ffn_w=conv_ffn_w, conv_ffn_b=conv_ffn_b, w_down=w_down, norm_final_w=norm_final_w, loss_target=loss_target, m_norm_mix_w=m_norm_mix_w, m_w_in=m_w_in, m_conv_a_w=m_conv_a_w, m_conv_a_b=m_conv_a_b, m_dt_bias=m_dt_bias, m_a_log=m_a_log, m_d_a=m_d_a, m_norm_a_w=m_norm_a_w, m_w_proj_a=m_w_proj_a, m_s5_lam_re=m_s5_lam_re, m_s5_lam_im=m_s5_lam_im, m_s5_log_dt=m_s5_log_dt, m_s5_b_re=m_s5_b_re, m_s5_b_im=m_s5_b_im, m_s5_c_re=m_s5_c_re, m_s5_c_im=m_s5_c_im, m_s5_d=m_s5_d, m_w_s5_glu=m_w_s5_glu, m_w_out=m_w_out, m_norm_ffn_w=m_norm_ffn_w, m_w_up=m_w_up, m_conv_ffn_w=m_conv_ffn_w, m_conv_ffn_b=m_conv_ffn_b, m_w_down=m_w_down, m_norm_final_w=m_norm_final_w, v_norm_mix_w=v_norm_mix_w, v_w_in=v_w_in, v_conv_a_w=v_conv_a_w, v_conv_a_b=v_conv_a_b, v_dt_bias=v_dt_bias, v_a_log=v_a_log, v_d_a=v_d_a, v_norm_a_w=v_norm_a_w, v_w_proj_a=v_w_proj_a, v_s5_lam_re=v_s5_lam_re, v_s5_lam_im=v_s5_lam_im, v_s5_log_dt=v_s5_log_dt, v_s5_b_re=v_s5_b_re, v_s5_b_im=v_s5_b_im, v_s5_c_re=v_s5_c_re, v_s5_c_im=v_s5_c_im, v_s5_d=v_s5_d, v_w_s5_glu=v_w_s5_glu, v_w_out=v_w_out, v_norm_ffn_w=v_norm_ffn_w, v_w_up=v_w_up, v_conv_ffn_w=v_conv_ffn_w, v_conv_ffn_b=v_conv_ffn_b, v_w_down=v_w_down, v_norm_final_w=v_norm_final_w)
    weights = {n: given[n] for n in TWIN_WEIGHTS}
    shared = {n: given[n] for n in SHARED_INPUTS}
    per_example = {n: given[n] for n in ['x']}
    grad_fn = _jax.value_and_grad(_loss, argnums=(0, 1))

    def one_microbatch(ex, loss_target):
        ex = dict(ex)
        diff = ex.pop(TWIN_DIFF_INPUT)
        return grad_fn(weights, diff, {**shared, **ex}, loss_target)

    if N_MICROBATCH == 1:
        loss, (grad_w, grad_x) = one_microbatch(per_example, given["loss_target"])
    else:
        def body(carry, xs):
            loss_sum, grad_sum = carry
            l_k, (gw_k, gx_k) = one_microbatch(xs[0], xs[1])
            with _jax.named_scope("update"):
                return (loss_sum + l_k, _jax.tree.map(_jnp.add, grad_sum, gw_k)), gx_k

        init = (_jnp.zeros((), _jnp.float32), _jax.tree.map(_jnp.zeros_like, weights))
        (loss, grad_w), grad_x = _jax.lax.scan(body, init, (per_example, given["loss_target"]))
    with _jax.named_scope("update"):
        delta_w, new_m, new_v = {}, {}, {}
        for n in TWIN_WEIGHTS:
            delta_w[n], new_m[n], new_v[n] = _adamw(weights[n], grad_w[n], given["m_" + n], given["v_" + n])
    return (loss, grad_x, *[grad_w[n] for n in TWIN_WEIGHTS], *[delta_w[n] for n in TWIN_WEIGHTS],
            *[new_m[n] for n in TWIN_WEIGHTS], *[new_v[n] for n in TWIN_WEIGHTS])
```

```python
import functools
import math

import jax
import jax.numpy as jnp
from jax import lax
from jax.experimental import pallas as pl
from jax.experimental.pallas import tpu as pltpu

F32 = jnp.float32
BF16 = jnp.bfloat16
HI = lax.Precision.HIGHEST
MESH = pl.DeviceIdType.MESH
ANY = pl.BlockSpec(memory_space=pl.ANY)

D = 2048
DI = 3072
HD = 64
NG = 8
HPG = 6
GW = HPG * HD
NS = 128
KA = 4
Q = 256
CONVD = DI + 2 * NG * NS
DS5 = 1024
NCH = 4096
DFF = 5632
KF = 3
EPS = 1e-6
EIG_MAX = -1e-4
NMAIN = 13312
OFF_XBC, OFF_U, OFF_GA, OFF_GB = 3072, 8192, 9216, 11264
VMEM_LIMIT = 56 * 1024 * 1024

LR, B1, B2, AEPS, WD, STEP = 0.001, 0.9, 0.999, 1e-08, 0.01, 10


def _cp(*sem):
    return pltpu.CompilerParams(dimension_semantics=sem, vmem_limit_bytes=VMEM_LIMIT)


def _sig(x):
    return jax.nn.sigmoid(x)


def _silu(x):
    return x * _sig(x)


def _dsilu(x):
    s = _sig(x)
    return s * (1.0 + x * (1.0 - s))


def _softplus(x):
    return jnp.maximum(x, 0.0) + jnp.log(1.0 + jnp.exp(-jnp.abs(x)))


_GC = math.sqrt(2.0 / math.pi)


def _gelu(x):
    return 0.5 * x * (1.0 + jnp.tanh(_GC * (x + 0.044715 * x * x * x)))


def _dgelu(x):
    t = jnp.tanh(_GC * (x + 0.044715 * x * x * x))
    return 0.5 * (1.0 + t) + 0.5 * x * (1.0 - t * t) * _GC * (1.0 + 3.0 * 0.044715 * x * x)


def _dot(a, b, dims=((1,), (0,)), prec=None):
    return lax.dot_general(a, b, (dims, ((), ())), precision=prec, preferred_element_type=F32)


NT = ((1,), (1,))
TN = ((0,), (0,))


def _pick(n, t):
    t = min(n, t)
    while n % t:
        t //= 2
    return t


def _matmul(a, b, mode, name, out_dtype=F32, tm=512, tn=1024, tk=2048, residual=None):
    if mode == "nn":
        (m, k), (k2, n) = a.shape, b.shape
    elif mode == "nt":
        (m, k), (n, k2) = a.shape, b.shape
    else:
        (k, m), (k2, n) = a.shape, b.shape
    assert k == k2
    tm, tn, tk = _pick(m, tm), _pick(n, tn), _pick(k, tk)
    nk = k // tk
    dims = {"nn": ((1,), (0,)), "nt": NT, "tn": TN}[mode]
    has_res = residual is not None

    def body(*refs):
        a_ref, b_ref = refs[0], refs[1]
        r_ref = refs[2] if has_res else None
        o_ref = refs[3] if has_res else refs[2]
        p = _dot(a_ref[...], b_ref[...], dims)

        def finish(r):
            if has_res:
                r = r + r_ref[...]
            o_ref[...] = r.astype(out_dtype)

        if nk == 1:
            finish(p)
        else:
            acc = refs[-1]
            kk = pl.program_id(2)

            @pl.when(kk == 0)
            def _():
                acc[...] = p

            @pl.when(kk > 0)
            def _():
                acc[...] += p

            @pl.when(kk == nk - 1)
            def _():
                finish(acc[...])

    if mode == "tn":
        a_spec = pl.BlockSpec((tk, tm), lambda i, j, kk: (kk, i))
    else:
        a_spec = pl.BlockSpec((tm, tk), lambda i, j, kk: (i, kk))
    if mode == "nt":
        b_spec = pl.BlockSpec((tn, tk), lambda i, j, kk: (j, kk))
    else:
        b_spec = pl.BlockSpec((tk, tn), lambda i, j, kk: (kk, j))
    o_spec = pl.BlockSpec((tm, tn), lambda i, j, kk: (i, j))
    in_specs, args = [a_spec, b_spec], [a, b]
    if has_res:
        in_specs.append(o_spec)
        args.append(residual)
    return pl.pallas_call(
        body, name=name, grid=(m // tm, n // tn, nk),
        in_specs=in_specs, out_specs=o_spec,
        out_shape=jax.ShapeDtypeStruct((m, n), out_dtype),
        scratch_shapes=[pltpu.VMEM((tm, tn), F32)] if nk > 1 else [],
        compiler_params=_cp("parallel", "parallel", "arbitrary"),
    )(*args)


TL = 256


def _rms_fwd(x, w, name):
    n, d = x.shape

    def body(x_ref, w_ref, o_ref):
        xv = x_ref[...]
        r = lax.rsqrt(jnp.mean(xv * xv, axis=-1, keepdims=True) + EPS)
        o_ref[...] = (xv * r * w_ref[...]).astype(BF16)

    return pl.pallas_call(
        body, name=name, grid=(n // TL,),
        in_specs=[pl.BlockSpec((TL, d), lambda i: (i, 0)), pl.BlockSpec((1, d), lambda i: (0, 0))],
        out_specs=pl.BlockSpec((TL, d), lambda i: (i, 0)),
        out_shape=jax.ShapeDtypeStruct((n, d), BF16), compiler_params=_cp("parallel"),
    )(x, w)


def _rms_bwd(dhn, x, w, dres, name):
    n, d = x.shape

    def body(g_ref, x_ref, w_ref, r_ref, dx_ref, dxb_ref, gw_ref):
        xv = x_ref[...]
        r = lax.rsqrt(jnp.mean(xv * xv, axis=-1, keepdims=True) + EPS)
        xh = xv * r
        gv = g_ref[...]
        g = gv * w_ref[...]
        dx = r_ref[...] + r * (g - xh * jnp.mean(g * xh, axis=-1, keepdims=True))
        dx_ref[...] = dx
        dxb_ref[...] = dx.astype(BF16)

        @pl.when(pl.program_id(0) == 0)
        def _():
            gw_ref[...] = jnp.zeros_like(gw_ref)

        gw_ref[...] += jnp.sum(gv * xh, axis=0, keepdims=True)

    row = pl.BlockSpec((TL, d), lambda i: (i, 0))
    vec = pl.BlockSpec((1, d), lambda i: (0, 0))
    return pl.pallas_call(
        body, name=name, grid=(n // TL,),
        in_specs=[row, row, vec, row], out_specs=[row, row, vec],
        out_shape=[jax.ShapeDtypeStruct((n, d), F32), jax.ShapeDtypeStruct((n, d), BF16),
                   jax.ShapeDtypeStruct((1, d), F32)],
        compiler_params=_cp("arbitrary"),
    )(dhn, x, w, dres)


def _final(h2, w, target):
    n, d = h2.shape

    def body(x_ref, w_ref, t_ref, dx_ref, dxb_ref, gw_ref, loss_ref):
        xv = x_ref[...]
        r = lax.rsqrt(jnp.mean(xv * xv, axis=-1, keepdims=True) + EPS)
        xh = xv * r
        diff = xh * w_ref[...] - t_ref[...]
        gv = diff * (1.0 / d)
        g = gv * w_ref[...]
        dx = r * (g - xh * jnp.mean(g * xh, axis=-1, keepdims=True))
        dx_ref[...] = dx
        dxb_ref[...] = dx.astype(BF16)

        @pl.when(pl.program_id(0) == 0)
        def _():
            gw_ref[...] = jnp.zeros_like(gw_ref)
            loss_ref[...] = jnp.zeros_like(loss_ref)

        gw_ref[...] += jnp.sum(gv * xh, axis=0, keepdims=True)
        part = 0.5 * jnp.sum(jnp.mean(diff * diff, axis=-1, keepdims=True), axis=0, keepdims=True)
        loss_ref[...] += jnp.broadcast_to(part, loss_ref.shape)

    row = pl.BlockSpec((TL, d), lambda i: (i, 0))
    vec = pl.BlockSpec((1, d), lambda i: (0, 0))
    return pl.pallas_call(
        body, name="final_loss", grid=(n // TL,),
        in_specs=[row, vec, row], out_specs=[row, row, vec, pl.BlockSpec((8, 128), lambda i: (0, 0))],
        out_shape=[jax.ShapeDtypeStruct((n, d), F32), jax.ShapeDtypeStruct((n, d), BF16),
                   jax.ShapeDtypeStruct((1, d), F32), jax.ShapeDtypeStruct((8, 128), F32)],
        compiler_params=_cp("arbitrary"),
    )(h2, w, target)


CT = 512
CL = 512


def _shift_rows(x, p8, s):
    if s == 0:
        return x
    body = pltpu.roll(x, s, 0)
    head = pltpu.roll(jnp.concatenate([p8, x[0:8]], axis=0), s, 0)[8:16]
    return jnp.concatenate([head, body[8:]], axis=0)


def _shift_up(x, u, n):
    if u == 0:
        return x[0:n]
    return pltpu.roll(x, x.shape[0] - u, 0)[0:n]


def _conv_pre(x, p8, w_ref, b_ref, taps):
    pre = b_ref[...]
    for k in range(taps):
        pre = pre + w_ref[k:k + 1, :] * _shift_rows(x, p8, taps - 1 - k)
    return pre


def _halo_specs(n, col_of):
    per = CL // 8
    cur = pl.BlockSpec((CL, CT), lambda j, i: (i, col_of(j)))
    prev = pl.BlockSpec((8, CT), lambda j, i: (jnp.maximum(i * per - 1, 0), col_of(j)))
    nxt = pl.BlockSpec((8, CT), lambda j, i: (jnp.minimum((i + 1) * per, n // 8 - 1), col_of(j)))
    return prev, cur, nxt


def _conv_a_fwd(proj, w, b):
    n = proj.shape[0]
    off = OFF_XBC // CT

    def body(p_ref, x_ref, w_ref, b_ref, o_ref):
        p8 = jnp.where(pl.program_id(1) > 0, p_ref[...], 0.0)
        o_ref[...] = _silu(_conv_pre(x_ref[...], p8, w_ref, b_ref, KA))

    prev, cur, _ = _halo_specs(n, lambda j: j + off)
    return pl.pallas_call(
        body, name="conv_a_fwd", grid=(CONVD // CT, n // CL),
        in_specs=[prev, cur, pl.BlockSpec((KA, CT), lambda j, i: (0, j)), pl.BlockSpec((1, CT), lambda j, i: (0, j))],
        out_specs=pl.BlockSpec((CL, CT), lambda j, i: (i, j)),
        out_shape=jax.ShapeDtypeStruct((n, CONVD), F32), compiler_params=_cp("parallel", "parallel"),
    )(proj, proj, w, b)


def _conv_a_bwd(proj, dout, w, b, col0, name):
    n, width = dout.shape
    off = (OFF_XBC + col0) // CT
    woff = col0 // CT
    nl = n // CL

    def body(p_ref, x_ref, n_ref, d_ref, dn_ref, w_ref, b_ref, dx_ref, dw_ref, db_ref):
        i = pl.program_id(1)
        p8 = jnp.where(i > 0, p_ref[...], 0.0)
        xe = jnp.concatenate([x_ref[...], n_ref[...]], axis=0)
        de = jnp.concatenate([d_ref[...], jnp.where(i < nl - 1, dn_ref[...], 0.0)], axis=0)
        se = de * _dsilu(_conv_pre(xe, p8, w_ref, b_ref, KA))
        dx = jnp.zeros((CL, CT), F32)
        for k in range(KA):
            dx = dx + w_ref[k:k + 1, :] * _shift_up(se, KA - 1 - k, CL)
        dx_ref[...] = dx.astype(BF16)

        @pl.when(i == 0)
        def _():
            dw_ref[...] = jnp.zeros_like(dw_ref)
            db_ref[...] = jnp.zeros_like(db_ref)

        sc = se[0:CL]
        xc = x_ref[...]
        for k in range(KA):
            dw_ref[k:k + 1, :] += jnp.sum(sc * _shift_rows(xc, p8, KA - 1 - k), axis=0, keepdims=True)
        db_ref[...] += jnp.sum(sc, axis=0, keepdims=True)

    prev, cur, nxt = _halo_specs(n, lambda j: j + off)
    _, dcur, dnxt = _halo_specs(n, lambda j: j)
    wspec = pl.BlockSpec((KA, CT), lambda j, i: (0, j + woff))
    bspec = pl.BlockSpec((1, CT), lambda j, i: (0, j + woff))
    return pl.pallas_call(
        body, name=name, grid=(width // CT, nl),
        in_specs=[prev, cur, nxt, dcur, dnxt, wspec, bspec],
        out_specs=[pl.BlockSpec((CL, CT), lambda j, i: (i, j)), pl.BlockSpec((KA, CT), lambda j, i: (0, j)),
                   pl.BlockSpec((1, CT), lambda j, i: (0, j))],
        out_shape=[jax.ShapeDtypeStruct((n, width), BF16), jax.ShapeDtypeStruct((KA, width), F32),
                   jax.ShapeDtypeStruct((1, width), F32)],
        compiler_params=_cp("parallel", "arbitrary"),
    )(proj, proj, proj, dout, dout, w, b)


def _conv_ffn_fwd(up, w, b):
    n = up.shape[0]
    nb = DFF // CT

    def body(pg_ref, g_ref, pv_ref, v_ref, wg_ref, bg_ref, wv_ref, bv_ref, o_ref):
        first = pl.program_id(1) > 0
        gc = _conv_pre(g_ref[...], jnp.where(first, pg_ref[...], 0.0), wg_ref, bg_ref, KF)
        vc = _conv_pre(v_ref[...], jnp.where(first, pv_ref[...], 0.0), wv_ref, bv_ref, KF)
        o_ref[...] = (_silu(gc) * vc).astype(BF16)

    gp, gcur, _ = _halo_specs(n, lambda j: j)
    vp, vcur, _ = _halo_specs(n, lambda j: j + nb)
    return pl.pallas_call(
        body, name="conv_ffn_fwd", grid=(nb, n // CL),
        in_specs=[gp, gcur, vp, vcur,
                  pl.BlockSpec((KF, CT), lambda j, i: (0, j)), pl.BlockSpec((1, CT), lambda j, i: (0, j)),
                  pl.BlockSpec((KF, CT), lambda j, i: (0, j + nb)), pl.BlockSpec((1, CT), lambda j, i: (0, j + nb))],
        out_specs=pl.BlockSpec((CL, CT), lambda j, i: (i, j)),
        out_shape=jax.ShapeDtypeStruct((n, DFF), BF16), compiler_params=_cp("parallel", "parallel"),
    )(up, up, up, up, w, b, w, b)


def _conv_ffn_bwd(up, dact, w, b):
    n = up.shape[0]
    nb = DFF // CT
    nl = n // CL

    def body(ps_ref, s_ref, ns_ref, po_ref, o_ref, no_ref, d_ref, dn_ref, ws_ref, bs_ref, wo_ref, bo_ref,
             dx_ref, dw_ref, db_ref):
        j, i = pl.program_id(0), pl.program_id(1)
        ps8 = jnp.where(i > 0, ps_ref[...], 0.0)
        po8 = jnp.where(i > 0, po_ref[...], 0.0)
        se = jnp.concatenate([s_ref[...], ns_ref[...]], axis=0)
        oe = jnp.concatenate([o_ref[...], no_ref[...]], axis=0)
        de = jnp.concatenate([d_ref[...], jnp.where(i < nl - 1, dn_ref[...], 0.0)], axis=0).astype(F32)
        sc = _conv_pre(se, ps8, ws_ref, bs_ref, KF)
        oc = _conv_pre(oe, po8, wo_ref, bo_ref, KF)
        ge = jnp.where(j < nb, de * oc * _dsilu(sc), de * _silu(oc))
        dx = jnp.zeros((CL, CT), F32)
        for k in range(KF):
            dx = dx + ws_ref[k:k + 1, :] * _shift_up(ge, KF - 1 - k, CL)
        dx_ref[...] = dx.astype(BF16)

        @pl.when(i == 0)
        def _():
            dw_ref[...] = jnp.zeros_like(dw_ref)
            db_ref[...] = jnp.zeros_like(db_ref)

        gcur = ge[0:CL]
        xc = s_ref[...]
        for k in range(KF):
            dw_ref[k:k + 1, :] += jnp.sum(gcur * _shift_rows(xc, ps8, KF - 1 - k), axis=0, keepdims=True)
        db_ref[...] += jnp.sum(gcur, axis=0, keepdims=True)

    sp, scur, snx = _halo_specs(n, lambda j: j)
    op, ocur, onx = _halo_specs(n, lambda j: (j + nb) % (2 * nb))
    _, dcur, dnx = _halo_specs(n, lambda j: j % nb)
    wcol = lambda f: (pl.BlockSpec((KF, CT), lambda j, i: (0, f(j))), pl.BlockSpec((1, CT), lambda j, i: (0, f(j))))
    ws, bs = wcol(lambda j: j)
    wo, bo = wcol(lambda j: (j + nb) % (2 * nb))
    return pl.pallas_call(
        body, name="conv_ffn_bwd", grid=(2 * nb, nl),
        in_specs=[sp, scur, snx, op, ocur, onx, dcur, dnx, ws, bs, wo, bo],
        out_specs=[pl.BlockSpec((CL, CT), lambda j, i: (i, j)), pl.BlockSpec((KF, CT), lambda j, i: (0, j)),
                   pl.BlockSpec((1, CT), lambda j, i: (0, j))],
        out_shape=[jax.ShapeDtypeStruct((n, 2 * DFF), BF16), jax.ShapeDtypeStruct((KF, 2 * DFF), F32),
                   jax.ShapeDtypeStruct((1, 2 * DFF), F32)],
        compiler_params=_cp("parallel", "arbitrary"),
    )(up, up, up, up, up, up, dact, dact, w, b, w, b)


def _ssd_common(dtc_ref, dtr_ref, hpc_ref, hpr_ref, e8_ref):
    row = lax.broadcasted_iota(jnp.int32, (Q, Q), 0)
    col = lax.broadcasted_iota(jnp.int32, (Q, Q), 1)
    lower = row >= col
    upper = row <= col
    hpc, hpr = hpc_ref[...], hpr_ref[...]
    pre_c = dtc_ref[...] + hpc[0:1, :]
    dt_c = _softplus(pre_c)
    a_c = -jnp.exp(hpc[1:2, :])
    s_c = _dot(lower.astype(F32), dt_c * a_c, prec=HI)
    dt_r = _softplus(dtr_ref[...] + hpr[:, 0:1])
    s_r = _dot(dt_r * (-jnp.exp(hpr[:, 1:2])), upper.astype(F32), prec=HI)
    e8 = e8_ref[...]
    dt_e = _dot(dt_c, e8, prec=HI)
    s_e = _dot(s_c, e8, prec=HI)
    return lower, upper, pre_c, dt_c, a_c, s_c, s_r, dt_e, s_e


def _ssd_specs(nc, rev):
    cc = (lambda c: nc - 1 - c) if rev else (lambda c: c)
    return [
        pl.BlockSpec((Q, GW), lambda g, c: (cc(c), g)),
        pl.BlockSpec((Q, NS), lambda g, c: (cc(c), DI // NS + g)),
        pl.BlockSpec((Q, NS), lambda g, c: (cc(c), (DI + NG * NS) // NS + g)),
        pl.BlockSpec((None, Q, 8), lambda g, c: (g, cc(c), 0)),
        pl.BlockSpec((None, 8, Q), lambda g, c: (g, 0, cc(c))),
        pl.BlockSpec((None, 8, 8), lambda g, c: (g, 0, 0)),
        pl.BlockSpec((None, 8, 128), lambda g, c: (g, 0, 0)),
        pl.BlockSpec((1, GW), lambda g, c: (0, g)),
        pl.BlockSpec((8, GW), lambda g, c: (0, 0)),
    ]


def _ssd_fwd(xbc, dtc, dtr, hpc, hpr, dexp, e8):
    n = xbc.shape[0]
    nc = n // Q

    def body(xs_ref, b_ref, c_ref, dtc_ref, dtr_ref, hpc_ref, hpr_ref, dexp_ref, e8_ref, y_ref, sp_ref, st):
        @pl.when(pl.program_id(1) == 0)
        def _():
            st[...] = jnp.zeros_like(st)

        lower, _, _, _, _, s_c, s_r, dt_e, s_e = _ssd_common(dtc_ref, dtr_ref, hpc_ref, hpr_ref, e8_ref)
        xs = xs_ref[...]
        x = xs * dt_e
        xb = x.astype(BF16)
        bb, cb = b_ref[...].astype(BF16), c_ref[...].astype(BF16)
        cbm = _dot(cb, bb, NT)
        st_e = s_e[Q - 1:Q, :]
        sprev = st[...]
        sp_ref[...] = sprev
        yoff = _dot(cb, sprev.astype(BF16)) * jnp.exp(s_e) + dexp_ref[...] * xs
        for h in range(HPG):
            sl = slice(h * HD, (h + 1) * HD)
            lm = jnp.where(lower, jnp.exp(jnp.minimum(s_c[:, h:h + 1] - s_r[h:h + 1, :], 0.0)), 0.0)
            y_ref[:, sl] = _dot((cbm * lm).astype(BF16), xb[:, sl]) + yoff[:, sl]
        w = (x * jnp.exp(st_e - s_e)).astype(BF16)
        st[...] = jnp.exp(st_e) * sprev + _dot(bb, w, TN)

    return pl.pallas_call(
        body, name="ssd_fwd", grid=(NG, nc), in_specs=_ssd_specs(nc, False),
        out_specs=[pl.BlockSpec((Q, GW), lambda g, c: (c, g)),
                   pl.BlockSpec((None, None, NS, GW), lambda g, c: (c, g, 0, 0))],
        out_shape=[jax.ShapeDtypeStruct((n, DI), F32), jax.ShapeDtypeStruct((nc, NG, NS, GW), F32)],
        scratch_shapes=[pltpu.VMEM((NS, GW), F32)],
        compiler_params=_cp("parallel", "arbitrary"),
    )(xbc, xbc, xbc, dtc, dtr, hpc, hpr, dexp, e8)


def _ssd_bwd(xbc, dtc, dtr, hpc, hpr, dexp, e8, e8t, sprev_all, dy):
    n = xbc.shape[0]
    nc = n // Q
    rc = lambda c: nc - 1 - c

    def body(xs_ref, b_ref, c_ref, dtc_ref, dtr_ref, hpc_ref, hpr_ref, dexp_ref, e8_ref, e8t_ref, sp_ref, dy_ref,
             dxs_ref, db_ref, dc_ref, draw_ref, pd_ref, ps_ref, dst, dxbuf):
        @pl.when(pl.program_id(1) == 0)
        def _():
            dst[...] = jnp.zeros_like(dst)
            pd_ref[...] = jnp.zeros_like(pd_ref)
            ps_ref[...] = jnp.zeros_like(ps_ref)

        lower, upper, pre_c, dt_c, a_c, s_c, s_r, dt_e, s_e = _ssd_common(dtc_ref, dtr_ref, hpc_ref, hpr_ref, e8_ref)
        e8t = e8t_ref[...]
        xs = xs_ref[...]
        x = xs * dt_e
        xb = x.astype(BF16)
        bb, cb = b_ref[...].astype(BF16), c_ref[...].astype(BF16)
        cbm = _dot(cb, bb, NT)
        cbt = _dot(bb, cb, NT)
        st_e = s_e[Q - 1:Q, :]
        dec_out, dec_st, e_t = jnp.exp(s_e), jnp.exp(st_e - s_e), jnp.exp(st_e)
        dyv = dy_ref[...]
        dyb = dyv.astype(BF16)
        sprev = sp_ref[...]
        sb = sprev.astype(BF16)
        ds_in = dst[...]
        dsb = ds_in.astype(BF16)

        cs = _dot(cb, sb)
        dcs = (dyv * dec_out).astype(BF16)
        d_c = _dot(dcs, sb, NT)
        wf = x * dec_st
        d_w = _dot(bb, dsb)
        d_b = _dot(wf.astype(BF16), dsb, NT)
        tw = d_w * wf
        ds_c = _dot(dyv * cs * dec_out - tw, e8t, prec=HI)
        dcb = jnp.zeros((Q, Q), F32)
        dcbt = jnp.zeros((Q, Q), F32)
        lane8 = lax.broadcasted_iota(jnp.int32, (1, 8), 1)
        for h in range(HPG):
            sl = slice(h * HD, (h + 1) * HD)
            sc_h, sr_h = s_c[:, h:h + 1], s_r[h:h + 1, :]
            lm = jnp.where(lower, jnp.exp(jnp.minimum(sc_h - sr_h, 0.0)), 0.0)
            lmt = jnp.where(upper, jnp.exp(jnp.minimum(sr_h - sc_h, 0.0)), 0.0)
            mt = cbt * lmt
            dm = _dot(dyb[:, sl], xb[:, sl], NT)
            dmt = _dot(xb[:, sl], dyb[:, sl], NT)
            dxbuf[:, sl] = _dot(mt.astype(BF16), dyb[:, sl])
            dml = dm * lm
            dmlt = dmt * lmt
            dcb = dcb + dml
            dcbt = dcbt + dmlt
            dsh = jnp.sum(dml * cbm, axis=1, keepdims=True) - jnp.sum(dmlt * cbt, axis=1, keepdims=True)
            ds_c = ds_c + dsh * (lane8 == h).astype(F32)
        d_c = d_c + _dot(dcb.astype(BF16), bb)
        d_b = d_b + _dot(dcbt.astype(BF16), cb)
        dx = d_w * dec_st + dxbuf[...]
        tsum = jnp.sum(tw, axis=0, keepdims=True) + jnp.sum(ds_in * sprev, axis=0, keepdims=True) * e_t
        ds_t = _dot(jnp.broadcast_to(tsum, (8, GW)), e8t, prec=HI)[0:1, :]
        rows = lax.broadcasted_iota(jnp.int32, (Q, 8), 0)
        ds_c = ds_c + jnp.where(rows == Q - 1, ds_t, 0.0)
        d_a = _dot(upper.astype(F32), ds_c, prec=HI)
        ddt = _dot(dx * xs, e8t, prec=HI) + d_a * a_c
        draw = ddt * _sig(pre_c)
        draw_ref[...] = draw
        ps_ref[0:1, :] += jnp.sum(draw, axis=0, keepdims=True)
        ps_ref[1:2, :] += jnp.sum(d_a * dt_c, axis=0, keepdims=True) * a_c
        pd_ref[...] += jnp.sum(dyv * xs, axis=0, keepdims=True)
        dxs_ref[...] = dx * dt_e + dyv * dexp_ref[...]
        db_ref[...] = d_b
        dc_ref[...] = d_c
        dst[...] = e_t * ds_in + _dot(cb, dcs, TN)

    in_specs = _ssd_specs(nc, True) + [
        pl.BlockSpec((GW, 8), lambda g, c: (0, 0)),
        pl.BlockSpec((None, None, NS, GW), lambda g, c: (rc(c), g, 0, 0)),
        pl.BlockSpec((Q, GW), lambda g, c: (rc(c), g)),
    ]
    return pl.pallas_call(
        body, name="ssd_bwd", grid=(NG, nc), in_specs=in_specs,
        out_specs=[pl.BlockSpec((Q, GW), lambda g, c: (rc(c), g)),
                   pl.BlockSpec((Q, NS), lambda g, c: (rc(c), g)),
                   pl.BlockSpec((Q, NS), lambda g, c: (rc(c), g)),
                   pl.BlockSpec((None, Q, 8), lambda g, c: (g, rc(c), 0)),
                   pl.BlockSpec((None, 1, GW), lambda g, c: (g, 0, 0)),
                   pl.BlockSpec((None, 8, 8), lambda g, c: (g, 0, 0))],
        out_shape=[jax.ShapeDtypeStruct((n, DI), F32), jax.ShapeDtypeStruct((n, NG * NS), F32),
                   jax.ShapeDtypeStruct((n, NG * NS), F32), jax.ShapeDtypeStruct((NG, n, 8), F32),
                   jax.ShapeDtypeStruct((NG, 1, GW), F32), jax.ShapeDtypeStruct((NG, 8, 8), F32)],
        scratch_shapes=[pltpu.VMEM((NS, GW), F32), pltpu.VMEM((Q, GW), F32)],
        compiler_params=_cp("parallel", "arbitrary"),
    )(xbc, xbc, xbc, dtc, dtr, hpc, hpr, dexp, e8, e8t, sprev_all, dy)


GL = 128


def _gnorm_fwd(y, proj, w):
    n = y.shape[0]

    def body(y_ref, z_ref, w_ref, o_ref):
        for g in range(NG):
            sl = slice(g * GW, (g + 1) * GW)
            yz = y_ref[:, sl] * _silu(z_ref[:, sl])
            r = lax.rsqrt(jnp.mean(yz * yz, axis=-1, keepdims=True) + EPS)
            o_ref[:, sl] = (yz * r * w_ref[:, sl]).astype(BF16)

    row = pl.BlockSpec((GL, DI), lambda i: (i, 0))
    return pl.pallas_call(
        body, name="gnorm_fwd", grid=(n // GL,),
        in_specs=[row, row, pl.BlockSpec((1, DI), lambda i: (0, 0))], out_specs=row,
        out_shape=jax.ShapeDtypeStruct((n, DI), BF16), compiler_params=_cp("parallel"),
    )(y, proj, w)


def _gnorm_bwd(dyn, y, proj, w):
    n = y.shape[0]

    def body(d_ref, y_ref, z_ref, w_ref, dy_ref, dz_ref, gw_ref):
        @pl.when(pl.program_id(0) == 0)
        def _():
            gw_ref[...] = jnp.zeros_like(gw_ref)

        for g in range(NG):
            sl = slice(g * GW, (g + 1) * GW)
            yv, zv, dv = y_ref[:, sl], z_ref[:, sl], d_ref[:, sl]
            sz = _silu(zv)
            yz = yv * sz
            r = lax.rsqrt(jnp.mean(yz * yz, axis=-1, keepdims=True) + EPS)
            yh = yz * r
            gg = dv * w_ref[:, sl]
            dyz = r * (gg - yh * jnp.mean(gg * yh, axis=-1, keepdims=True))
            gw_ref[:, sl] += jnp.sum(dv * yh, axis=0, keepdims=True)
            dy_ref[:, sl] = dyz * sz
            dz_ref[:, sl] = (dyz * yv * _dsilu(zv)).astype(BF16)

    row = pl.BlockSpec((GL, DI), lambda i: (i, 0))
    vec = pl.BlockSpec((1, DI), lambda i: (0, 0))
    return pl.pallas_call(
        body, name="gnorm_bwd", grid=(n // GL,),
        in_specs=[row, row, row, vec], out_specs=[row, row, vec],
        out_shape=[jax.ShapeDtypeStruct((n, DI), F32), jax.ShapeDtypeStruct((n, DI), BF16),
                   jax.ShapeDtypeStruct((1, DI), F32)],
        compiler_params=_cp("arbitrary"),
    )(dyn, y, proj, w)


SL = 512
SB = 8
SCB = NCH // SB


def _s5_in(proj, bre, bim):
    n = proj.shape[0]
    uoff = OFF_U // 128

    def body(u_ref, br_ref, bi_ref, or_ref, oi_ref):
        u = u_ref[...].astype(BF16)
        or_ref[...] = _dot(u, br_ref[...])
        oi_ref[...] = _dot(u, bi_ref[...])

    blk = pl.BlockSpec((None, 128, SCB), lambda i, j: (j, 0, 0))
    out = pl.BlockSpec((SL, SCB), lambda i, j: (i, j))
    return pl.pallas_call(
        body, name="s5_in", grid=(n // SL, SB),
        in_specs=[pl.BlockSpec((SL, 128), lambda i, j: (i, uoff + j)), blk, blk], out_specs=[out, out],
        out_shape=[jax.ShapeDtypeStruct((n, NCH), F32)] * 2, compiler_params=_cp("parallel", "parallel"),
    )(proj, bre, bim)


SC = 256


def _s5_scan(vre, vim, tab, reverse, name):
    n = vre.shape[0]
    nl = n // SL
    ng = SL // 8
    ti = (lambda i: nl - 1 - i) if reverse else (lambda i: i)

    def body(re_ref, im_ref, tab_ref, ore_ref, oim_ref, cre, cim):
        @pl.when(pl.program_id(1) == 0)
        def _():
            cre[...] = jnp.zeros_like(cre)
            cim[...] = jnp.zeros_like(cim)

        def step(j, carry):
            cr, ci = carry
            jj = (ng - 1 - j) if reverse else j
            rows = pl.ds(pl.multiple_of(jj * 8, 8), 8)
            vr, vi = re_ref[rows, :], im_ref[rows, :]
            for t, k in enumerate((1, 2, 4)):
                sh = (8 - k) if reverse else k
                rr, ri = pltpu.roll(vr, sh, 0), pltpu.roll(vi, sh, 0)
                pr, pi = tab_ref[2 * t], tab_ref[2 * t + 1]
                vr, vi = vr + pr * rr - pi * ri, vi + pr * ri + pi * rr
            lr, li = tab_ref[6], tab_ref[7]
            vr, vi = vr + lr * cr - li * ci, vi + lr * ci + li * cr
            ore_ref[rows, :] = vr
            oim_ref[rows, :] = vi
            e = 0 if reverse else 7
            return (jnp.broadcast_to(vr[e:e + 1, :], (8, SC)), jnp.broadcast_to(vi[e:e + 1, :], (8, SC)))

        cr, ci = lax.fori_loop(0, ng, step, (cre[...], cim[...]))
        cre[...] = cr
        cim[...] = ci

    blk = pl.BlockSpec((SL, SC), lambda j, i: (ti(i), j))
    return pl.pallas_call(
        body, name=name, grid=(NCH // SC, nl),
        in_specs=[blk, blk, pl.BlockSpec((8, 8, SC), lambda j, i: (0, 0, j))], out_specs=[blk, blk],
        out_shape=[jax.ShapeDtypeStruct((n, NCH), F32)] * 2,
        scratch_shapes=[pltpu.VMEM((8, SC), F32), pltpu.VMEM((8, SC), F32)],
        compiler_params=_cp("parallel", "arbitrary"),
    )(vre, vim, tab)


def _s5_out(xre, xim, cre, cimn, proj, dvec):
    n = xre.shape[0]
    uoff = OFF_U // 128

    def body(xr_ref, xi_ref, cr_ref, ci_ref, u_ref, d_ref, y_ref, g_ref):
        y = (_dot(xr_ref[...].astype(BF16), cr_ref[...]) + _dot(xi_ref[...].astype(BF16), ci_ref[...])
             + d_ref[...] * u_ref[...])
        y_ref[...] = y
        g_ref[...] = _gelu(y).astype(BF16)

    xs = pl.BlockSpec((SL, SCB), lambda i, j: (i, j))
    blk = pl.BlockSpec((None, SCB, 128), lambda i, j: (j, 0, 0))
    out = pl.BlockSpec((SL, 128), lambda i, j: (i, j))
    return pl.pallas_call(
        body, name="s5_out", grid=(n // SL, SB),
        in_specs=[xs, xs, blk, blk, pl.BlockSpec((SL, 128), lambda i, j: (i, uoff + j)),
                  pl.BlockSpec((1, 128), lambda i, j: (0, j))],
        out_specs=[out, out],
        out_shape=[jax.ShapeDtypeStruct((n, DS5), F32), jax.ShapeDtypeStruct((n, DS5), BF16)],
        compiler_params=_cp("parallel", "parallel"),
    )(xre, xim, cre, cimn, proj, dvec)


def _s5_out_bwd(dg, ypre, crt, cimnt, proj, dvec, xre, xim):
    n = dg.shape[0]
    uoff = OFF_U // 128
    nl = n // SL

    def body(dg_ref, y_ref, cr_ref, ci_ref, u_ref, d_ref, xr_ref, xi_ref,
             gr_ref, gi_ref, dus_ref, gcr_ref, gci_ref, gd_ref):
        dy = dg_ref[...] * _dgelu(y_ref[...])
        dyb = dy.astype(BF16)
        gr_ref[...] = _dot(dyb, cr_ref[...])
        gi_ref[...] = _dot(dyb, ci_ref[...])
        dus_ref[...] = dy * d_ref[...]

        @pl.when(pl.program_id(1) == 0)
        def _():
            gcr_ref[...] = jnp.zeros_like(gcr_ref)
            gci_ref[...] = jnp.zeros_like(gci_ref)
            gd_ref[...] = jnp.zeros_like(gd_ref)

        gcr_ref[...] += _dot(xr_ref[...].astype(BF16), dyb, TN)
        gci_ref[...] -= _dot(xi_ref[...].astype(BF16), dyb, TN)
        gd_ref[...] += jnp.sum(dy * u_ref[...], axis=0, keepdims=True)

    u128 = pl.BlockSpec((SL, 128), lambda j, i: (i, j))
    xs = pl.BlockSpec((SL, SCB), lambda j, i: (i, j))
    blk = pl.BlockSpec((None, 128, SCB), lambda j, i: (j, 0, 0))
    gblk = pl.BlockSpec((None, SCB, 128), lambda j, i: (j, 0, 0))
    vec = pl.BlockSpec((1, 128), lambda j, i: (0, j))
    return pl.pallas_call(
        body, name="s5_out_bwd", grid=(SB, nl),
        in_specs=[u128, u128, blk, blk, pl.BlockSpec((SL, 128), lambda j, i: (i, uoff + j)), vec, xs, xs],
        out_specs=[xs, xs, u128, gblk, gblk, vec],
        out_shape=[jax.ShapeDtypeStruct((n, NCH), F32)] * 2 + [jax.ShapeDtypeStruct((n, DS5), F32)]
        + [jax.ShapeDtypeStruct((SB, SCB, 128), F32)] * 2 + [jax.ShapeDtypeStruct((1, DS5), F32)],
        compiler_params=_cp("parallel", "arbitrary"),
    )(dg, ypre, crt, cimnt, proj, dvec, xre, xim)


def _s5_in_bwd(are, aim, brt, bit, proj, dus, xre, xim):
    n = are.shape[0]
    uoff = OFF_U // 128
    per = SL // 8

    def body(ar_ref, ai_ref, br_ref, bi_ref, u_ref, dus_ref, xr_ref, xi_ref, pr_ref, pi_ref,
             du_ref, gbr_ref, gbi_ref, glr_ref, gli_ref):
        i = pl.program_id(1)
        ar, ai = ar_ref[...], ai_ref[...]
        arb, aib = ar.astype(BF16), ai.astype(BF16)
        du_ref[...] = (_dot(arb, br_ref[...]) + _dot(aib, bi_ref[...]) + dus_ref[...]).astype(BF16)

        @pl.when(i == 0)
        def _():
            for r in (gbr_ref, gbi_ref, glr_ref, gli_ref):
                r[...] = jnp.zeros_like(r)

        ub = u_ref[...].astype(BF16)
        gbr_ref[...] += _dot(arb, ub, TN)
        gbi_ref[...] += _dot(aib, ub, TN)
        row0 = lax.broadcasted_iota(jnp.int32, (SL, SCB), 0) == 0
        last_r = jnp.where(i > 0, pr_ref[7:8, :], 0.0)
        last_i = jnp.where(i > 0, pi_ref[7:8, :], 0.0)
        xpr = jnp.where(row0, last_r, pltpu.roll(xr_ref[...], 1, 0))
        xpi = jnp.where(row0, last_i, pltpu.roll(xi_ref[...], 1, 0))
        glr_ref[...] += jnp.sum(ar * xpr + ai * xpi, axis=0, keepdims=True)
        gli_ref[...] += jnp.sum(ai * xpr - ar * xpi, axis=0, keepdims=True)

    xs = pl.BlockSpec((SL, SCB), lambda j, i: (i, j))
    prev = pl.BlockSpec((8, SCB), lambda j, i: (jnp.maximum(i * per - 1, 0), j))
    blk = pl.BlockSpec((None, SCB, 128), lambda j, i: (j, 0, 0))
    u128 = pl.BlockSpec((SL, 128), lambda j, i: (i, j))
    vec = pl.BlockSpec((1, SCB), lambda j, i: (0, j))
    return pl.pallas_call(
        body, name="s5_in_bwd", grid=(SB, n // SL),
        in_specs=[xs, xs, blk, blk, pl.BlockSpec((SL, 128), lambda j, i: (i, uoff + j)), u128, xs, xs, prev, prev],
        out_specs=[u128, blk, blk, vec, vec],
        out_shape=[jax.ShapeDtypeStruct((n, DS5), BF16)] + [jax.ShapeDtypeStruct((SB, SCB, 128), F32)] * 2
        + [jax.ShapeDtypeStruct((1, NCH), F32)] * 2,
        compiler_params=_cp("parallel", "arbitrary"),
    )(are, aim, brt, bit, proj, dus, xre, xim, xre, xim)


MC = 1024


def _merge_specs():
    ga = pl.BlockSpec((TL, MC), lambda i, j: (i, OFF_GA // MC + j))
    gb = pl.BlockSpec((TL, MC), lambda i, j: (i, OFF_GB // MC + j))
    col = pl.BlockSpec((TL, MC), lambda i, j: (i, j))
    gate = pl.BlockSpec((TL, MC), lambda i, j: (i, D // MC + j))
    return ga, gb, col, gate


def _merge_fwd(proj, ya, vg):
    n = ya.shape[0]

    def body(ga_ref, gb_ref, ya_ref, v_ref, g_ref, o_ref):
        yb = v_ref[...] * _sig(g_ref[...])
        o_ref[...] = (_sig(ga_ref[...]) * ya_ref[...] + _sig(gb_ref[...]) * yb).astype(BF16)

    ga, gb, col, gate = _merge_specs()
    return pl.pallas_call(
        body, name="merge_fwd", grid=(n // TL, D // MC), in_specs=[ga, gb, col, col, gate], out_specs=col,
        out_shape=jax.ShapeDtypeStruct((n, D), BF16), compiler_params=_cp("parallel", "parallel"),
    )(proj, proj, ya, vg, vg)


def _merge_bwd(dm, proj, ya, vg):
    n = ya.shape[0]

    def body(dm_ref, ga_ref, gb_ref, ya_ref, v_ref, g_ref, dga_ref, dgb_ref, dya_ref, dv_ref, dg_ref):
        d = dm_ref[...]
        sa, sb, sg = _sig(ga_ref[...]), _sig(gb_ref[...]), _sig(g_ref[...])
        v = v_ref[...]
        yb = v * sg
        dga_ref[...] = (d * ya_ref[...] * sa * (1.0 - sa)).astype(BF16)
        dgb_ref[...] = (d * yb * sb * (1.0 - sb)).astype(BF16)
        dya_ref[...] = (d * sa).astype(BF16)
        dyb = d * sb
        dv_ref[...] = (dyb * sg).astype(BF16)
        dg_ref[...] = (dyb * v * sg * (1.0 - sg)).astype(BF16)

    ga, gb, col, gate = _merge_specs()
    o = jax.ShapeDtypeStruct((n, D), BF16)
    return pl.pallas_call(
        body, name="merge_bwd", grid=(n // TL, D // MC), in_specs=[col, ga, gb, col, col, gate],
        out_specs=[col] * 5, out_shape=[o] * 5, compiler_params=_cp("parallel", "parallel"),
    )(dm, proj, proj, ya, vg, vg)


def _adamw(w, g, m, v, name):
    r, c = w.shape
    tr = _pick(r, 128)

    def body(w_ref, g_ref, m_ref, v_ref, d_ref, nm_ref, nv_ref):
        gv = g_ref[...]
        nm = B1 * m_ref[...] + (1.0 - B1) * gv
        nv = B2 * v_ref[...] + (1.0 - B2) * (gv * gv)
        m_hat = nm / (1.0 - B1 ** STEP)
        v_hat = nv / (1.0 - B2 ** STEP)
        d_ref[...] = -LR * (m_hat / (jnp.sqrt(v_hat) + AEPS) + WD * w_ref[...])
        nm_ref[...] = nm
        nv_ref[...] = nv

    blk = pl.BlockSpec((tr, c), lambda i: (i, 0))
    o = jax.ShapeDtypeStruct((r, c), F32)
    return pl.pallas_call(
        body, name=name, grid=(r // tr,), in_specs=[blk] * 4, out_specs=[blk] * 3, out_shape=[o] * 3,
        compiler_params=_cp("parallel"),
    )(w, g, m, v)


def _sum_slabs(xs, name, out_dtype=F32):
    r, c = xs[0].shape
    tr = _pick(r, 256)

    def body(*refs):
        acc = refs[0][...].astype(F32)
        for ref in refs[1:-1]:
            acc = acc + ref[...].astype(F32)
        refs[-1][...] = acc.astype(out_dtype)

    blk = pl.BlockSpec((tr, c), lambda i: (i, 0))
    return pl.pallas_call(
        body, name=name, grid=(r // tr,), in_specs=[blk] * len(xs), out_specs=blk,
        out_shape=jax.ShapeDtypeStruct((r, c), out_dtype), compiler_params=_cp("parallel"),
    )(*xs)


def _place():
    return lax.axis_index("x"), lax.axis_index("y"), lax.axis_index("c")


def _gather_small(v):
    m_per, n = v.shape

    def body(x_ref, out_ref, send_sems, recv_sems, local_sem):
        x, y, c = _place()
        me, sibling = (x, y, c), (x, y, 1 - c)
        chips = [(1 - x, y), (x, 1 - y), (1 - x, 1 - y)]

        def rows(px, py, pc):
            return out_ref.at[pl.ds((4 * px + 2 * py + pc) * m_per, m_per), :]

        def copy(k, block, to, src=None):
            return pltpu.make_async_remote_copy(
                src_ref=rows(*block) if src is None else src, dst_ref=rows(*block),
                send_sem=send_sems.at[k], recv_sem=recv_sems.at[k], device_id=to, device_id_type=MESH)

        mine = pltpu.make_async_copy(x_ref, rows(*me), local_sem)
        mine.start()
        first = [copy(0, me, sibling, src=x_ref)]
        first += [copy(1 + j, me, (*chip, c), src=x_ref) for j, chip in enumerate(chips)]
        for cp in first:
            cp.start()
        passed = [copy(4 + j, (*chip, c), sibling) for j, chip in enumerate(chips)]
        for j, chip in enumerate(chips):
            copy(1 + j, (*chip, c), me).wait_recv()
            passed[j].start()
        copy(0, sibling, me).wait_recv()
        for j, chip in enumerate(chips):
            copy(4 + j, (*chip, 1 - c), me).wait_recv()
        for cp in first + passed:
            cp.wait_send()
        mine.wait()

    return pl.pallas_call(
        body, name="gather_small_%d" % m_per,
        out_shape=jax.ShapeDtypeStruct((8 * m_per, n), v.dtype),
        in_specs=[pl.BlockSpec(memory_space=pltpu.VMEM)], out_specs=pl.BlockSpec(memory_space=pltpu.VMEM),
        scratch_shapes=[pltpu.SemaphoreType.DMA((7,)), pltpu.SemaphoreType.DMA((7,)), pltpu.SemaphoreType.DMA],
        compiler_params=pltpu.CompilerParams(vmem_limit_bytes=VMEM_LIMIT),
    )(v)


def _allsum_small(v, name):
    r = v.shape[0]
    g = _gather_small(v)
    return _sum_slabs([g[k * r:(k + 1) * r] for k in range(8)], name)


def _gather_big(shards):
    nt = len(shards)

    def body(*refs):
        ins, outs = refs[:nt], refs[nt:2 * nt]
        send_sems, recv_sems, local_sems = refs[2 * nt:]
        x, y, c = _place()
        s = 2 * x + y
        sibling = (x, y, 1 - c)
        chips = [(1 - x, y), (x, 1 - y), (1 - x, 1 - y)]

        def half(t, slot, h):
            hr = ins[t].shape[0] // 2
            return outs[t].at[slot, pl.ds(h * hr, hr), :]

        def ici(t, j, src_slot, to):
            hr = ins[t].shape[0] // 2
            return pltpu.make_async_remote_copy(
                src_ref=ins[t].at[pl.ds(c * hr, hr), :], dst_ref=half(t, src_slot, c),
                send_sem=send_sems.at[6 * t + j], recv_sem=recv_sems.at[6 * t + j], device_id=to, device_id_type=MESH)

        def d2d(t, j, slot, h):
            return pltpu.make_async_remote_copy(
                src_ref=half(t, slot, h), dst_ref=half(t, slot, h),
                send_sem=send_sems.at[6 * t + 3 + j], recv_sem=recv_sems.at[6 * t + 3 + j],
                device_id=sibling, device_id_type=MESH)

        local = [pltpu.make_async_copy(ins[t], outs[t].at[s], local_sems.at[t]) for t in range(nt)]
        for cp in local:
            cp.start()
        sends = [ici(t, j, s, (*chip, c)) for t in range(nt) for j, chip in enumerate(chips)]
        for cp in sends:
            cp.start()
        passed = []
        for t in range(nt):
            for j, (px, py) in enumerate(chips):
                ici(t, j, 2 * px + py, (x, y, c)).wait_recv()
                cp = d2d(t, j, 2 * px + py, c)
                cp.start()
                passed.append(cp)
        for t in range(nt):
            for j, (px, py) in enumerate(chips):
                d2d(t, j, 2 * px + py, 1 - c).wait_recv()
        for cp in sends + passed:
            cp.wait_send()
        for cp in local:
            cp.wait()

    return pl.pallas_call(
        body, name="gather_big",
        out_shape=[jax.ShapeDtypeStruct((4,) + a.shape, a.dtype) for a in shards],
        in_specs=[ANY] * nt, out_specs=[ANY] * nt,
        scratch_shapes=[pltpu.SemaphoreType.DMA((6 * nt,)), pltpu.SemaphoreType.DMA((6 * nt,)),
                        pltpu.SemaphoreType.DMA((nt,))],
    )(*shards)


def _swap_halves(parts):
    nt = len(parts)

    def body(*refs):
        ins, outs = refs[:nt], refs[nt:2 * nt]
        send_sems, recv_sems = refs[2 * nt:]
        x, y, c = _place()
        cps = []
        for t in range(nt):
            hr = ins[t].shape[1] // 2
            cps.append(pltpu.make_async_remote_copy(
                src_ref=ins[t].at[:, pl.ds((1 - c) * hr, hr), :], dst_ref=outs[t],
                send_sem=send_sems.at[t], recv_sem=recv_sems.at[t], device_id=(x, y, 1 - c), device_id_type=MESH))
        for cp in cps:
            cp.start()
        for cp in cps:
            cp.wait()

    return pl.pallas_call(
        body, name="swap_halves",
        out_shape=[jax.ShapeDtypeStruct((4, a.shape[1] // 2, a.shape[2]), a.dtype) for a in parts],
        in_specs=[ANY] * nt, out_specs=[ANY] * nt,
        scratch_shapes=[pltpu.SemaphoreType.DMA((nt,)), pltpu.SemaphoreType.DMA((nt,))],
    )(*parts)


def _scatter_chips(parts):
    nt = len(parts)

    def body(*refs):
        ins, outs = refs[:nt], refs[nt:2 * nt]
        send_sems, recv_sems, local_sems = refs[2 * nt:]
        x, y, c = _place()
        s = 2 * x + y
        chips = [(1 - x, y), (x, 1 - y), (1 - x, 1 - y)]
        local = [pltpu.make_async_copy(ins[t].at[s], outs[t].at[s], local_sems.at[t]) for t in range(nt)]
        for cp in local:
            cp.start()
        cps = []
        for t in range(nt):
            for j, (px, py) in enumerate(chips):
                cps.append(pltpu.make_async_remote_copy(
                    src_ref=ins[t].at[2 * px + py], dst_ref=outs[t].at[s],
                    send_sem=send_sems.at[3 * t + j], recv_sem=recv_sems.at[3 * t + j],
                    device_id=(px, py, c), device_id_type=MESH))
        for cp in cps:
            cp.start()
        for t in range(nt):
            for j, (px, py) in enumerate(chips):
                pltpu.make_async_remote_copy(
                    src_ref=ins[t].at[s], dst_ref=outs[t].at[2 * px + py],
                    send_sem=send_sems.at[3 * t + j], recv_sem=recv_sems.at[3 * t + j],
                    device_id=(px, py, c), device_id_type=MESH).wait_recv()
        for cp in cps:
            cp.wait_send()
        for cp in local:
            cp.wait()

    return pl.pallas_call(
        body, name="scatter_chips",
        out_shape=[jax.ShapeDtypeStruct(a.shape, a.dtype) for a in parts],
        in_specs=[ANY] * nt, out_specs=[ANY] * nt,
        scratch_shapes=[pltpu.SemaphoreType.DMA((3 * nt,)), pltpu.SemaphoreType.DMA((3 * nt,)),
                        pltpu.SemaphoreType.DMA((nt,))],
    )(*parts)


def _join_halves(halves):
    nt = len(halves)

    def body(*refs):
        ins, outs = refs[:nt], refs[nt:2 * nt]
        send_sems, recv_sems, local_sems = refs[2 * nt:]
        x, y, c = _place()
        local, cps = [], []
        for t in range(nt):
            hr = ins[t].shape[0]
            mine = outs[t].at[pl.ds(c * hr, hr), :]
            local.append(pltpu.make_async_copy(ins[t], mine, local_sems.at[t]))
            cps.append(pltpu.make_async_remote_copy(
                src_ref=ins[t], dst_ref=mine, send_sem=send_sems.at[t], recv_sem=recv_sems.at[t],
                device_id=(x, y, 1 - c), device_id_type=MESH))
        for cp in local + cps:
            cp.start()
        for t in range(nt):
            hr = ins[t].shape[0]
            pltpu.make_async_remote_copy(
                src_ref=ins[t], dst_ref=outs[t].at[pl.ds((1 - c) * hr, hr), :],
                send_sem=send_sems.at[t], recv_sem=recv_sems.at[t],
                device_id=(x, y, 1 - c), device_id_type=MESH).wait_recv()
        for cp in cps:
            cp.wait_send()
        for cp in local:
            cp.wait()

    return pl.pallas_call(
        body, name="join_halves",
        out_shape=[jax.ShapeDtypeStruct((2 * a.shape[0], a.shape[1]), a.dtype) for a in halves],
        in_specs=[ANY] * nt, out_specs=[ANY] * nt,
        scratch_shapes=[pltpu.SemaphoreType.DMA((nt,)), pltpu.SemaphoreType.DMA((nt,)),
                        pltpu.SemaphoreType.DMA((nt,))],
    )(*halves)


def _reduce_big(parts):
    x, y, c = _place()
    nt = len(parts)
    sib = _swap_halves(parts)
    chip_sum = []
    for t in range(nt):
        _, r, cc = parts[t].shape
        hr = r // 2
        mine = lax.dynamic_slice_in_dim(parts[t], c * hr, hr, axis=1)
        a, b = mine.reshape(4 * hr, cc), sib[t].reshape(4 * hr, cc)
        chip_sum.append(_sum_slabs([a, b], "chip_sum_%d" % t, BF16).reshape(4, hr, cc))
    got = _scatter_chips(chip_sum)
    halves = [_sum_slabs([got[t][k] for k in range(4)], "shard_sum_%d" % t) for t in range(nt)]
    return _join_halves(halves)


def _s5_params(lam_re, lam_im, log_dt, b_re, b_im):
    lr = jnp.minimum(lam_re, EIG_MAX)
    dt = jnp.exp(log_dt)[:, None]
    mag = jnp.exp(lr * dt)
    lbr, lbi = mag * jnp.cos(lam_im * dt), mag * jnp.sin(lam_im * dt)
    den = lr * lr + lam_im * lam_im
    qr = ((lbr - 1.0) * lr + lbi * lam_im) / den
    qi = (lbi * lr - (lbr - 1.0) * lam_im) / den
    bbr = qr[..., None] * b_re - qi[..., None] * b_im
    bbi = qr[..., None] * b_im + qi[..., None] * b_re
    return lbr, lbi, bbr, bbi


def _cmul(a, b):
    return a[0] * b[0] - a[1] * b[1], a[0] * b[1] + a[1] * b[0]


def _scan_table(lr, li, reverse):
    l1 = (lr.reshape(1, NCH), li.reshape(1, NCH))
    pows = [l1]
    for _ in range(7):
        pows.append(_cmul(pows[-1], l1))
    r = jnp.arange(8)[:, None]
    tabs = []
    for k in (1, 2, 4):
        keep = (r < 8 - k) if reverse else (r >= k)
        tabs += [jnp.where(keep, pows[k - 1][0], 0.0), jnp.where(keep, pows[k - 1][1], 0.0)]
    order = range(7, -1, -1) if reverse else range(8)
    tabs += [jnp.concatenate([pows[e][0] for e in order], axis=0), jnp.concatenate([pows[e][1] for e in order], axis=0)]
    return jnp.stack(tabs).astype(F32)


_EYE8 = lambda: jnp.eye(8, dtype=F32)


def _to_in_blocks(b):
    return jnp.einsum("jgpc,gh->jgchp", b.reshape(8, 8, 64, 16), _EYE8()).reshape(8, 128, 512)


def _to_out_blocks(cm):
    return jnp.einsum("jgcp,gh->jgphc", cm.reshape(8, 8, 16, 64), _EYE8()).reshape(8, 512, 128)


def _from_out_blocks(g):
    return jnp.einsum("jgphc,gh->jgpc", g.reshape(8, 8, 64, 8, 16), _EYE8()).reshape(64, 64, 16)


def _local_step(x, target, p):
    n = x.shape[0]
    g = {}
    hn1 = _rms_fwd(x, p["norm_mix_w"], "rms_mix")
    proj = _matmul(hn1, p["w_main"], "nn", "mm_in")
    dtraw = _matmul(hn1, p["w_dt"], "nn", "mm_dt")
    xbc = _conv_a_fwd(proj, p["conv_a_w"], p["conv_a_b"])
    dth = jnp.pad(dtraw[:, :NG * HPG].reshape(n, NG, HPG), ((0, 0), (0, 0), (0, 8 - HPG)))
    dtc = dth.transpose(1, 0, 2)
    dtr = dth.transpose(1, 2, 0)
    pad_h = lambda v: jnp.pad(v.reshape(NG, HPG), ((0, 0), (0, 8 - HPG)))
    hpc = jnp.zeros((NG, 8, 8), F32).at[:, 0, :].set(pad_h(p["dt_bias"])).at[:, 1, :].set(pad_h(p["a_log"]))
    hpr = jnp.zeros((NG, 8, 128), F32).at[:, :, 0].set(pad_h(p["dt_bias"])).at[:, :, 1].set(pad_h(p["a_log"]))
    dexp = jnp.repeat(p["d_a"].reshape(1, NG * HPG), HD, axis=1)
    e8 = (jnp.arange(8)[:, None] == (jnp.arange(GW)[None, :] // HD)).astype(F32)
    yssd, sprev = _ssd_fwd(xbc, dtc, dtr, hpc, hpr, dexp, e8)
    yn = _gnorm_fwd(yssd, proj, p["norm_a_w"])
    ya = _matmul(yn, p["w_proj_a"], "nn", "mm_proj")

    (lbr, lbi, bbr, bbi), s5_vjp = jax.vjp(_s5_params, p["s5_lam_re"], p["s5_lam_im"], p["s5_log_dt"],
                                           p["s5_b_re"], p["s5_b_im"])
    bin_r, bin_i = _to_in_blocks(bbr), _to_in_blocks(bbi)
    cout_r, cout_in = _to_out_blocks(p["s5_c_re"]), _to_out_blocks(-p["s5_c_im"])
    bur, bui = _s5_in(proj, bin_r.astype(BF16), bin_i.astype(BF16))
    xre, xim = _s5_scan(bur, bui, _scan_table(lbr, lbi, False), False, "s5_scan_fwd")
    ypre, g5 = _s5_out(xre, xim, cout_r.astype(BF16), cout_in.astype(BF16), proj, p["s5_d"])
    vg = _matmul(g5, p["w_s5_glu"], "nn", "mm_glu")
    merged = _merge_fwd(proj, ya, vg)
    h1 = _matmul(merged, p["w_out"], "nn", "mm_out", residual=x)
    hn2 = _rms_fwd(h1, p["norm_ffn_w"], "rms_ffn")
    up = _matmul(hn2, p["w_up"], "nn", "mm_up")
    act = _conv_ffn_fwd(up, p["conv_ffn_w"], p["conv_ffn_b"])
    h2 = _matmul(act, p["w_down"], "nn", "mm_down", residual=h1)
    dh2, dh2b, g["norm_final_w"], loss_blk = _final(h2, p["norm_final_w"], target)
    g["w_down"] = _matmul(act, dh2b, "tn", "mm_gw_down")
    dact = _matmul(dh2b, p["w_down"], "nt", "mm_dact", out_dtype=BF16)
    dup, g["conv_ffn_w"], g["conv_ffn_b"] = _conv_ffn_bwd(up, dact, p["conv_ffn_w"], p["conv_ffn_b"])
    g["w_up"] = _matmul(hn2, dup, "tn", "mm_gw_up")
    dhn2 = _matmul(dup, p["w_up"], "nt", "mm_dhn2")
    dh1, dh1b, g["norm_ffn_w"] = _rms_bwd(dhn2, h1, p["norm_ffn_w"], dh2, "rms_ffn_bwd")
    g["w_out"] = _matmul(merged, dh1b, "tn", "mm_gw_out")
    dmerged = _matmul(dh1b, p["w_out"], "nt", "mm_dmerged")
    dga, dgb, dya, dval, dgate = _merge_bwd(dmerged, proj, ya, vg)
    dvg = jnp.concatenate([dval, dgate], axis=1)
    g["w_s5_glu"] = _matmul(g5, dvg, "tn", "mm_gw_glu")
    dg5 = _matmul(dvg, p["w_s5_glu"], "nt", "mm_dg5")
    tr = lambda b: b.transpose(0, 2, 1)
    gxr, gxi, dus, gcr, gci, g["s5_d"] = _s5_out_bwd(dg5, ypre, tr(cout_r).astype(BF16), tr(cout_in).astype(BF16),
                                                     proj, p["s5_d"], xre, xim)
    are, aim = _s5_scan(gxr, gxi, _scan_table(lbr, -lbi, True), True, "s5_scan_bwd")
    du, gbr, gbi, glr, gli = _s5_in_bwd(are, aim, tr(bin_r).astype(BF16), tr(bin_i).astype(BF16), proj, dus, xre, xim)
    g["s5_c_re"] = _from_out_blocks(gcr).transpose(0, 2, 1)
    g["s5_c_im"] = _from_out_blocks(gci).transpose(0, 2, 1)
    (g["s5_lam_re"], g["s5_lam_im"], g["s5_log_dt"], g["s5_b_re"], g["s5_b_im"]) = s5_vjp(
        (glr.reshape(64, 64), gli.reshape(64, 64), _from_out_blocks(gbr), _from_out_blocks(gbi)))
    g["w_proj_a"] = _matmul(yn, dya, "tn", "mm_gw_proj")
    dyn = _matmul(dya, p["w_proj_a"], "nt", "mm_dyn")
    dyssd, dz, g["norm_a_w"] = _gnorm_bwd(dyn, yssd, proj, p["norm_a_w"])
    e8t = e8.T
    dxs, dbm, dcm, draw, pd, ps = _ssd_bwd(xbc, dtc, dtr, hpc, hpr, dexp, e8, e8t, sprev, dyssd)
    g["dt_bias"] = ps[:, 0, :HPG].reshape(1, NG * HPG)
    g["a_log"] = ps[:, 1, :HPG].reshape(1, NG * HPG)
    g["d_a"] = pd.reshape(NG * HPG, HD).sum(axis=1).reshape(1, NG * HPG)
    ddt = jnp.pad(draw[:, :, :HPG].transpose(1, 0, 2).reshape(n, NG * HPG), ((0, 0), (0, 128 - NG * HPG))).astype(BF16)
    dxbc_parts, gcw, gcb = [], [], []
    for arr, col0, nm in ((dxs, 0, "conv_a_bwd_x"), (dbm, DI, "conv_a_bwd_b"), (dcm, DI + NG * NS, "conv_a_bwd_c")):
        dpart, gw_, gb_ = _conv_a_bwd(proj, arr, p["conv_a_w"], p["conv_a_b"], col0, nm)
        dxbc_parts.append(dpart)
        gcw.append(gw_)
        gcb.append(gb_)
    g["conv_a_w"] = jnp.concatenate(gcw, axis=1)
    g["conv_a_b"] = jnp.concatenate(gcb, axis=1)
    dproj = jnp.concatenate([dz] + dxbc_parts + [du, dga, dgb], axis=1)
    g["w_main"] = _matmul(hn1, dproj, "tn", "mm_gw_in")
    g["w_dt"] = _matmul(hn1, ddt, "tn", "mm_gw_dt")
    dhn1 = _matmul(dproj, p["w_main"], "nt", "mm_dhn1")
    dhn1 = _matmul(ddt, p["w_dt"], "nt", "mm_dhn1_dt", residual=dhn1)
    gx, _, g["norm_mix_w"] = _rms_bwd(dhn1, x, p["norm_mix_w"], dh1, "rms_mix_bwd")
    return loss_blk, gx, g


BIG = ["w_in", "w_proj_a", "w_s5_glu", "w_out", "w_up", "w_down"]
SMALL = ["norm_mix_w", "conv_a_w", "conv_a_b", "dt_bias", "a_log", "d_a", "norm_a_w", "s5_lam_re", "s5_lam_im",
         "s5_log_dt", "s5_b_re", "s5_b_im", "s5_c_re", "s5_c_im", "s5_d", "norm_ffn_w", "conv_ffn_w", "conv_ffn_b",
         "norm_final_w"]
ORDER = ["norm_mix_w", "w_in", "conv_a_w", "conv_a_b", "dt_bias", "a_log", "d_a", "norm_a_w", "w_proj_a", "s5_lam_re",
         "s5_lam_im", "s5_log_dt", "s5_b_re", "s5_b_im", "s5_c_re", "s5_c_im", "s5_d", "w_s5_glu", "w_out",
         "norm_ffn_w", "w_up", "conv_ffn_w", "conv_ffn_b", "w_down", "norm_final_w"]
IN_SPLIT = [DI, DI + CONVD, DI + CONVD + NG * HPG]
CONV_FULL = {"conv_a_w": (KA, CONVD), "conv_ffn_w": (KF, 2 * DFF)}


def _pack(arrs):
    flat = jnp.concatenate([a.reshape(-1).astype(F32) for a in arrs])
    total = flat.shape[0]
    padded = -(-total // 1024) * 1024
    return jnp.pad(flat, (0, padded - total)).reshape(padded // 128, 128)


def _unpack(block, shapes):
    flat = block.reshape(-1)
    out, at = [], 0
    for sh in shapes:
        size = math.prod(sh)
        out.append(flat[at:at + size].reshape(sh))
        at += size
    return out


def _stack_cols(a):
    return a.transpose(1, 0, 2).reshape(a.shape[1], 4 * a.shape[2])


def _unstack_cols(a):
    return a.reshape(a.shape[0], 4, a.shape[1] // 4).transpose(1, 0, 2)


def kernel(x, norm_mix_w, w_in, conv_a_w, conv_a_b, dt_bias, a_log, d_a, norm_a_w, w_proj_a, s5_lam_re, s5_lam_im, s5_log_dt, s5_b_re, s5_b_im, s5_c_re, s5_c_im, s5_d, w_s5_glu, w_out, norm_ffn_w, w_up, conv_ffn_w, conv_ffn_b, w_down, norm_final_w, loss_target, m_norm_mix_w, m_w_in, m_conv_a_w, m_conv_a_b, m_dt_bias, m_a_log, m_d_a, m_norm_a_w, m_w_proj_a, m_s5_lam_re, m_s5_lam_im, m_s5_log_dt, m_s5_b_re, m_s5_b_im, m_s5_c_re, m_s5_c_im, m_s5_d, m_w_s5_glu, m_w_out, m_norm_ffn_w, m_w_up, m_conv_ffn_w, m_conv_ffn_b, m_w_down, m_norm_final_w, v_norm_mix_w, v_w_in, v_conv_a_w, v_conv_a_b, v_dt_bias, v_a_log, v_d_a, v_norm_a_w, v_w_proj_a, v_s5_lam_re, v_s5_lam_im, v_s5_log_dt, v_s5_b_re, v_s5_b_im, v_s5_c_re, v_s5_c_im, v_s5_d, v_w_s5_glu, v_w_out, v_norm_ffn_w, v_w_up, v_conv_ffn_w, v_conv_ffn_b, v_w_down, v_norm_final_w):
    w = dict(norm_mix_w=norm_mix_w, w_in=w_in, conv_a_w=conv_a_w, conv_a_b=conv_a_b, dt_bias=dt_bias, a_log=a_log, d_a=d_a, norm_a_w=norm_a_w, w_proj_a=w_proj_a, s5_lam_re=s5_lam_re, s5_lam_im=s5_lam_im, s5_log_dt=s5_log_dt, s5_b_re=s5_b_re, s5_b_im=s5_b_im, s5_c_re=s5_c_re, s5_c_im=s5_c_im, s5_d=s5_d, w_s5_glu=w_s5_glu, w_out=w_out, norm_ffn_w=norm_ffn_w, w_up=w_up, conv_ffn_w=conv_ffn_w, conv_ffn_b=conv_ffn_b, w_down=w_down, norm_final_w=norm_final_w)
    m = dict(norm_mix_w=m_norm_mix_w, w_in=m_w_in, conv_a_w=m_conv_a_w, conv_a_b=m_conv_a_b, dt_bias=m_dt_bias, a_log=m_a_log, d_a=m_d_a, norm_a_w=m_norm_a_w, w_proj_a=m_w_proj_a, s5_lam_re=m_s5_lam_re, s5_lam_im=m_s5_lam_im, s5_log_dt=m_s5_log_dt, s5_b_re=m_s5_b_re, s5_b_im=m_s5_b_im, s5_c_re=m_s5_c_re, s5_c_im=m_s5_c_im, s5_d=m_s5_d, w_s5_glu=m_w_s5_glu, w_out=m_w_out, norm_ffn_w=m_norm_ffn_w, w_up=m_w_up, conv_ffn_w=m_conv_ffn_w, conv_ffn_b=m_conv_ffn_b, w_down=m_w_down, norm_final_w=m_norm_final_w)
    v = dict(norm_mix_w=v_norm_mix_w, w_in=v_w_in, conv_a_w=v_conv_a_w, conv_a_b=v_conv_a_b, dt_bias=v_dt_bias, a_log=v_a_log, d_a=v_d_a, norm_a_w=v_norm_a_w, w_proj_a=v_w_proj_a, s5_lam_re=v_s5_lam_re, s5_lam_im=v_s5_lam_im, s5_log_dt=v_s5_log_dt, s5_b_re=v_s5_b_re, s5_b_im=v_s5_b_im, s5_c_re=v_s5_c_re, s5_c_im=v_s5_c_im, s5_d=v_s5_d, w_s5_glu=v_w_s5_glu, w_out=v_w_out, norm_ffn_w=v_norm_ffn_w, w_up=v_w_up, conv_ffn_w=v_conv_ffn_w, conv_ffn_b=v_conv_ffn_b, w_down=v_w_down, norm_final_w=v_norm_final_w)
    xi, yi, ci = _place()
    chip = 2 * xi + yi

    shards = [w[k][0].astype(BF16) for k in BIG]
    full = dict(zip(BIG, _gather_big(shards)))
    conv_blocks = []
    for k, (taps, cols) in CONV_FULL.items():
        shard = jnp.where(ci == 0, w[k][0], 0.0)
        conv_blocks.append(lax.dynamic_update_slice_in_dim(jnp.zeros((taps, cols), F32), shard, chip * (cols // 4), 1))
    conv_full = _unpack(_allsum_small(_pack(conv_blocks), "sum_conv_w"), [CONV_FULL[k] for k in CONV_FULL])

    win = _stack_cols(full["w_in"])
    p = {
        "w_main": jnp.concatenate([win[:, :IN_SPLIT[1]], win[:, IN_SPLIT[2]:]], axis=1),
        "w_dt": jnp.pad(win[:, IN_SPLIT[1]:IN_SPLIT[2]], ((0, 0), (0, 128 - NG * HPG))),
        "w_proj_a": full["w_proj_a"].reshape(DI, D),
        "w_s5_glu": _stack_cols(full["w_s5_glu"]),
        "w_out": full["w_out"].reshape(D, D),
        "w_up": _stack_cols(full["w_up"]),
        "w_down": full["w_down"].reshape(DFF, D),
        "conv_a_w": conv_full[0], "conv_ffn_w": conv_full[1],
        "conv_a_b": conv_a_b, "conv_ffn_b": conv_ffn_b,
        "norm_mix_w": norm_mix_w, "norm_a_w": norm_a_w, "norm_ffn_w": norm_ffn_w,
        "norm_final_w": norm_final_w.reshape(1, D),
        "dt_bias": dt_bias, "a_log": a_log, "d_a": d_a, "s5_d": s5_d,
        "s5_lam_re": s5_lam_re[0], "s5_lam_im": s5_lam_im[0], "s5_log_dt": s5_log_dt[0],
        "s5_b_re": s5_b_re[0], "s5_b_im": s5_b_im[0], "s5_c_re": s5_c_re[0], "s5_c_im": s5_c_im[0],
    }
    loss_blk, gx, g = _local_step(x[0], loss_target[0], p)

    gmain, gdt = g["w_main"], g["w_dt"]
    gin = jnp.concatenate([gmain[:, :IN_SPLIT[1]], gdt[:, :NG * HPG], gmain[:, IN_SPLIT[1]:]], axis=1)
    parts = [
        _unstack_cols(gin), g["w_proj_a"].reshape(4, DI // 4, D), _unstack_cols(g["w_s5_glu"]),
        g["w_out"].reshape(4, D // 4, D), _unstack_cols(g["w_up"]), g["w_down"].reshape(4, DFF // 4, D),
    ]
    big_grads = dict(zip(BIG, _reduce_big([a.astype(BF16) for a in parts])))

    small_shapes = [CONV_FULL.get(k, w[k].shape[1:] if k != "norm_final_w" else w[k].shape) for k in SMALL]
    small = _allsum_small(_pack([g[k] for k in SMALL] + [loss_blk[0:1, 0:1]]), "sum_small_grads")
    small_grads = dict(zip(SMALL + ["loss"], _unpack(small, small_shapes + [(1,)])))
    for k, (taps, cols) in CONV_FULL.items():
        small_grads[k] = lax.dynamic_slice_in_dim(small_grads[k], chip * (cols // 4), cols // 4, axis=1)
    loss = small_grads.pop("loss").reshape(())

    grads, delta, new_m, new_v = {}, {}, {}, {}
    for k in BIG:
        grads[k] = big_grads[k][None]
        d_, m_, v_ = _adamw(w[k][0], big_grads[k], m[k][0], v[k][0], "adamw_" + k)
        delta[k], new_m[k], new_v[k] = d_[None], m_[None], v_[None]
    for k in SMALL:
        grads[k] = small_grads[k].reshape(w[k].shape)
    pk = lambda t: _pack([t[k] for k in SMALL])
    d_, m_, v_ = _adamw(pk(w), pk(grads), pk(m), pk(v), "adamw_small")
    shapes = [w[k].shape for k in SMALL]
    for k, dd, mm, vv in zip(SMALL, _unpack(d_, shapes), _unpack(m_, shapes), _unpack(v_, shapes)):
        delta[k], new_m[k], new_v[k] = dd, mm, vv
    return (loss, gx[None], *[grads[k] for k in ORDER], *[delta[k] for k in ORDER],
            *[new_m[k] for k in ORDER], *[new_v[k] for k in ORDER])
```

```python
import functools
import math

import jax
import jax.numpy as jnp
from jax import lax
from jax.experimental import pallas as pl
from jax.experimental.pallas import tpu as pltpu

F32 = jnp.float32
BF16 = jnp.bfloat16
HI = lax.Precision.HIGHEST
MESH = pl.DeviceIdType.MESH
ANY = pl.BlockSpec(memory_space=pl.ANY)

D = 2048
DI = 3072
HD = 64
NG = 8
HPG = 6
GW = HPG * HD
NS = 128
KA = 4
Q = 256
CONVD = DI + 2 * NG * NS
DS5 = 1024
NCH = 4096
DFF = 5632
KF = 3
EPS = 1e-6
EIG_MAX = -1e-4
NMAIN = 13312
OFF_XBC, OFF_U, OFF_GA, OFF_GB = 3072, 8192, 9216, 11264
WSH = 3340
WPAD = 3584
VMEM_LIMIT = 56 * 1024 * 1024

LR, B1, B2, AEPS, WD, STEP = 0.001, 0.9, 0.999, 1e-08, 0.01, 10


def _cp(*sem):
    return pltpu.CompilerParams(dimension_semantics=sem, vmem_limit_bytes=VMEM_LIMIT)


def _sig(x):
    return jax.nn.sigmoid(x)


def _silu(x):
    return x * _sig(x)


def _dsilu(x):
    s = _sig(x)
    return s * (1.0 + x * (1.0 - s))


def _softplus(x):
    return jnp.maximum(x, 0.0) + jnp.log(1.0 + jnp.exp(-jnp.abs(x)))


_GC = math.sqrt(2.0 / math.pi)


def _gelu(x):
    return 0.5 * x * (1.0 + jnp.tanh(_GC * (x + 0.044715 * x * x * x)))


def _dgelu(x):
    t = jnp.tanh(_GC * (x + 0.044715 * x * x * x))
    return 0.5 * (1.0 + t) + 0.5 * x * (1.0 - t * t) * _GC * (1.0 + 3.0 * 0.044715 * x * x)


def _dot(a, b, dims=((1,), (0,)), prec=None):
    return lax.dot_general(a, b, (dims, ((), ())), precision=prec, preferred_element_type=F32)


NT = ((1,), (1,))
TN = ((0,), (0,))


def _pick(n, t):
    for unit in (128, 8):
        for cand in range(min(n, t) // unit * unit, 0, -unit):
            if n % cand == 0:
                return cand
    return n


def _matmul(a, b, mode, name, out_dtype=F32, tm=512, tn=1024, tk=2048, residual=None, b_stacked=False,
            out_stacked=False):
    if b_stacked:
        _, brows, bn = b.shape
        bshape = (brows, 4 * bn)
    else:
        bshape = b.shape
    if mode == "nn":
        (m, k), (k2, n) = a.shape, bshape
    elif mode == "nt":
        (m, k), (n, k2) = a.shape, bshape
    else:
        (k, m), (k2, n) = a.shape, bshape
    assert k == k2
    tm = _pick(m, tm)
    tn = _pick(n // 4 if (out_stacked or (b_stacked and mode != "nt")) else n, tn)
    tk = _pick(k // 4 if (b_stacked and mode == "nt") else k, tk)
    nk = k // tk
    dims = {"nn": ((1,), (0,)), "nt": NT, "tn": TN}[mode]
    has_res = residual is not None

    def body(*refs):
        a_ref, b_ref = refs[0], refs[1]
        r_ref = refs[2] if has_res else None
        o_ref = refs[3] if has_res else refs[2]
        p = _dot(a_ref[...], b_ref[...], dims)

        def finish(r):
            if has_res:
                r = r + r_ref[...]
            o_ref[...] = r.astype(out_dtype)

        if nk == 1:
            finish(p)
        else:
            acc = refs[-1]
            kk = pl.program_id(2)

            @pl.when(kk == 0)
            def _():
                acc[...] = p

            @pl.when(kk > 0)
            def _():
                acc[...] += p

            @pl.when(kk == nk - 1)
            def _():
                finish(acc[...])

    if mode == "tn":
        a_spec = pl.BlockSpec((tk, tm), lambda i, j, kk: (kk, i))
    else:
        a_spec = pl.BlockSpec((tm, tk), lambda i, j, kk: (i, kk))
    if mode == "nt":
        if b_stacked:
            per = bn // tk
            b_spec = pl.BlockSpec((None, tn, tk), lambda i, j, kk: (kk // per, j, kk % per))
        else:
            b_spec = pl.BlockSpec((tn, tk), lambda i, j, kk: (j, kk))
    elif b_stacked:
        per = bn // tn
        b_spec = pl.BlockSpec((None, tk, tn), lambda i, j, kk: (j // per, kk, j % per))
    else:
        b_spec = pl.BlockSpec((tk, tn), lambda i, j, kk: (kk, j))
    o_spec = pl.BlockSpec((tm, tn), lambda i, j, kk: (i, j))
    if out_stacked:
        per_o = n // 4 // tn
        out_spec = pl.BlockSpec((None, tm, tn), lambda i, j, kk: (j // per_o, i, j % per_o))
        out_shape = jax.ShapeDtypeStruct((4, m, n // 4), out_dtype)
    else:
        out_spec, out_shape = o_spec, jax.ShapeDtypeStruct((m, n), out_dtype)
    in_specs, args = [a_spec, b_spec], [a, b]
    if has_res:
        in_specs.append(o_spec)
        args.append(residual)
    return pl.pallas_call(
        body, name=name, grid=(m // tm, n // tn, nk),
        in_specs=in_specs, out_specs=out_spec, out_shape=out_shape,
        scratch_shapes=[pltpu.VMEM((tm, tn), F32)] if nk > 1 else [],
        compiler_params=_cp("parallel", "parallel", "arbitrary"),
    )(*args)


TL = 256


def _rms_fwd(x, w, name):
    n, d = x.shape

    def body(x_ref, w_ref, o_ref):
        xv = x_ref[...]
        r = lax.rsqrt(jnp.mean(xv * xv, axis=-1, keepdims=True) + EPS)
        o_ref[...] = (xv * r * w_ref[...]).astype(BF16)

    return pl.pallas_call(
        body, name=name, grid=(n // TL,),
        in_specs=[pl.BlockSpec((TL, d), lambda i: (i, 0)), pl.BlockSpec((1, d), lambda i: (0, 0))],
        out_specs=pl.BlockSpec((TL, d), lambda i: (i, 0)),
        out_shape=jax.ShapeDtypeStruct((n, d), BF16), compiler_params=_cp("parallel"),
    )(x, w)


def _rms_bwd(dhn, x, w, dres, name):
    n, d = x.shape

    def body(g_ref, x_ref, w_ref, r_ref, dx_ref, dxb_ref, gw_ref):
        xv = x_ref[...]
        r = lax.rsqrt(jnp.mean(xv * xv, axis=-1, keepdims=True) + EPS)
        xh = xv * r
        gv = g_ref[...]
        g = gv * w_ref[...]
        dx = r_ref[...] + r * (g - xh * jnp.mean(g * xh, axis=-1, keepdims=True))
        dx_ref[...] = dx
        dxb_ref[...] = dx.astype(BF16)

        @pl.when(pl.program_id(0) == 0)
        def _():
            gw_ref[...] = jnp.zeros_like(gw_ref)

        gw_ref[...] += jnp.sum(gv * xh, axis=0, keepdims=True)

    row = pl.BlockSpec((TL, d), lambda i: (i, 0))
    vec = pl.BlockSpec((1, d), lambda i: (0, 0))
    return pl.pallas_call(
        body, name=name, grid=(n // TL,),
        in_specs=[row, row, vec, row], out_specs=[row, row, vec],
        out_shape=[jax.ShapeDtypeStruct((n, d), F32), jax.ShapeDtypeStruct((n, d), BF16),
                   jax.ShapeDtypeStruct((1, d), F32)],
        compiler_params=_cp("arbitrary"),
    )(dhn, x, w, dres)


def _final(h2, w, target):
    n, d = h2.shape

    def body(x_ref, w_ref, t_ref, dx_ref, dxb_ref, gw_ref, loss_ref):
        xv = x_ref[...]
        r = lax.rsqrt(jnp.mean(xv * xv, axis=-1, keepdims=True) + EPS)
        xh = xv * r
        diff = xh * w_ref[...] - t_ref[...]
        gv = diff * (1.0 / d)
        g = gv * w_ref[...]
        dx = r * (g - xh * jnp.mean(g * xh, axis=-1, keepdims=True))
        dx_ref[...] = dx
        dxb_ref[...] = dx.astype(BF16)

        @pl.when(pl.program_id(0) == 0)
        def _():
            gw_ref[...] = jnp.zeros_like(gw_ref)
            loss_ref[...] = jnp.zeros_like(loss_ref)

        gw_ref[...] += jnp.sum(gv * xh, axis=0, keepdims=True)
        part = 0.5 * jnp.sum(jnp.mean(diff * diff, axis=-1, keepdims=True), axis=0, keepdims=True)
        loss_ref[...] += jnp.broadcast_to(part, loss_ref.shape)

    row = pl.BlockSpec((TL, d), lambda i: (i, 0))
    vec = pl.BlockSpec((1, d), lambda i: (0, 0))
    return pl.pallas_call(
        body, name="final_loss", grid=(n // TL,),
        in_specs=[row, vec, row], out_specs=[row, row, vec, pl.BlockSpec((8, 128), lambda i: (0, 0))],
        out_shape=[jax.ShapeDtypeStruct((n, d), F32), jax.ShapeDtypeStruct((n, d), BF16),
                   jax.ShapeDtypeStruct((1, d), F32), jax.ShapeDtypeStruct((8, 128), F32)],
        compiler_params=_cp("arbitrary"),
    )(h2, w, target)


CT = 512
CL = 512


def _shift_rows(x, p8, s):
    if s == 0:
        return x
    body = pltpu.roll(x, s, 0)
    head = pltpu.roll(jnp.concatenate([p8, x[0:8]], axis=0), s, 0)[8:16]
    return jnp.concatenate([head, body[8:]], axis=0)


def _shift_up(x, u, n):
    if u == 0:
        return x[0:n]
    return pltpu.roll(x, x.shape[0] - u, 0)[0:n]


def _conv_pre(x, p8, w_ref, b_ref, taps):
    pre = b_ref[...]
    for k in range(taps):
        pre = pre + w_ref[k:k + 1, :] * _shift_rows(x, p8, taps - 1 - k)
    return pre


def _halo_specs(n, col_of):
    per = CL // 8
    cur = pl.BlockSpec((CL, CT), lambda j, i: (i, col_of(j)))
    prev = pl.BlockSpec((8, CT), lambda j, i: (jnp.maximum(i * per - 1, 0), col_of(j)))
    nxt = pl.BlockSpec((8, CT), lambda j, i: (jnp.minimum((i + 1) * per, n // 8 - 1), col_of(j)))
    return prev, cur, nxt


def _conv_a_fwd(proj, w, b):
    n = proj.shape[0]
    off = OFF_XBC // CT

    def body(p_ref, x_ref, w_ref, b_ref, o_ref):
        p8 = jnp.where(pl.program_id(1) > 0, p_ref[...], 0.0)
        o_ref[...] = _silu(_conv_pre(x_ref[...], p8, w_ref, b_ref, KA))

    prev, cur, _ = _halo_specs(n, lambda j: j + off)
    return pl.pallas_call(
        body, name="conv_a_fwd", grid=(CONVD // CT, n // CL),
        in_specs=[prev, cur, pl.BlockSpec((KA, CT), lambda j, i: (0, j)), pl.BlockSpec((1, CT), lambda j, i: (0, j))],
        out_specs=pl.BlockSpec((CL, CT), lambda j, i: (i, j)),
        out_shape=jax.ShapeDtypeStruct((n, CONVD), F32), compiler_params=_cp("parallel", "parallel"),
    )(proj, proj, w, b)


def _conv_a_bwd(proj, dout, w, b, col0, name):
    n, width = dout.shape
    off = (OFF_XBC + col0) // CT
    woff = col0 // CT
    nl = n // CL

    def body(p_ref, x_ref, n_ref, d_ref, dn_ref, w_ref, b_ref, dx_ref, dw_ref, db_ref):
        i = pl.program_id(1)
        p8 = jnp.where(i > 0, p_ref[...], 0.0)
        xe = jnp.concatenate([x_ref[...], n_ref[...]], axis=0)
        de = jnp.concatenate([d_ref[...], jnp.where(i < nl - 1, dn_ref[...], 0.0)], axis=0)
        se = de * _dsilu(_conv_pre(xe, p8, w_ref, b_ref, KA))
        dx = jnp.zeros((CL, CT), F32)
        for k in range(KA):
            dx = dx + w_ref[k:k + 1, :] * _shift_up(se, KA - 1 - k, CL)
        dx_ref[...] = dx.astype(BF16)

        @pl.when(i == 0)
        def _():
            dw_ref[...] = jnp.zeros_like(dw_ref)
            db_ref[...] = jnp.zeros_like(db_ref)

        sc = se[0:CL]
        xc = x_ref[...]
        for k in range(KA):
            dw_ref[k:k + 1, :] += jnp.sum(sc * _shift_rows(xc, p8, KA - 1 - k), axis=0, keepdims=True)
        db_ref[...] += jnp.sum(sc, axis=0, keepdims=True)

    prev, cur, nxt = _halo_specs(n, lambda j: j + off)
    _, dcur, dnxt = _halo_specs(n, lambda j: j)
    wspec = pl.BlockSpec((KA, CT), lambda j, i: (0, j + woff))
    bspec = pl.BlockSpec((1, CT), lambda j, i: (0, j + woff))
    return pl.pallas_call(
        body, name=name, grid=(width // CT, nl),
        in_specs=[prev, cur, nxt, dcur, dnxt, wspec, bspec],
        out_specs=[pl.BlockSpec((CL, CT), lambda j, i: (i, j)), pl.BlockSpec((KA, CT), lambda j, i: (0, j)),
                   pl.BlockSpec((1, CT), lambda j, i: (0, j))],
        out_shape=[jax.ShapeDtypeStruct((n, width), BF16), jax.ShapeDtypeStruct((KA, width), F32),
                   jax.ShapeDtypeStruct((1, width), F32)],
        compiler_params=_cp("parallel", "arbitrary"),
    )(proj, proj, proj, dout, dout, w, b)


def _conv_ffn_fwd(up, w, b):
    n = up.shape[0]
    nb = DFF // CT

    def body(pg_ref, g_ref, pv_ref, v_ref, wg_ref, bg_ref, wv_ref, bv_ref, o_ref):
        first = pl.program_id(1) > 0
        gc = _conv_pre(g_ref[...], jnp.where(first, pg_ref[...], 0.0), wg_ref, bg_ref, KF)
        vc = _conv_pre(v_ref[...], jnp.where(first, pv_ref[...], 0.0), wv_ref, bv_ref, KF)
        o_ref[...] = (_silu(gc) * vc).astype(BF16)

    gp, gcur, _ = _halo_specs(n, lambda j: j)
    vp, vcur, _ = _halo_specs(n, lambda j: j + nb)
    return pl.pallas_call(
        body, name="conv_ffn_fwd", grid=(nb, n // CL),
        in_specs=[gp, gcur, vp, vcur,
                  pl.BlockSpec((KF, CT), lambda j, i: (0, j)), pl.BlockSpec((1, CT), lambda j, i: (0, j)),
                  pl.BlockSpec((KF, CT), lambda j, i: (0, j + nb)), pl.BlockSpec((1, CT), lambda j, i: (0, j + nb))],
        out_specs=pl.BlockSpec((CL, CT), lambda j, i: (i, j)),
        out_shape=jax.ShapeDtypeStruct((n, DFF), BF16), compiler_params=_cp("parallel", "parallel"),
    )(up, up, up, up, w, b, w, b)


def _conv_ffn_bwd(up, dact, w, b):
    n = up.shape[0]
    nb = DFF // CT
    nl = n // CL

    def body(ps_ref, s_ref, ns_ref, po_ref, o_ref, no_ref, d_ref, dn_ref, ws_ref, bs_ref, wo_ref, bo_ref,
             dx_ref, dw_ref, db_ref):
        j, i = pl.program_id(0), pl.program_id(1)
        ps8 = jnp.where(i > 0, ps_ref[...], 0.0)
        po8 = jnp.where(i > 0, po_ref[...], 0.0)
        se = jnp.concatenate([s_ref[...], ns_ref[...]], axis=0)
        oe = jnp.concatenate([o_ref[...], no_ref[...]], axis=0)
        de = jnp.concatenate([d_ref[...], jnp.where(i < nl - 1, dn_ref[...], 0.0)], axis=0).astype(F32)
        sc = _conv_pre(se, ps8, ws_ref, bs_ref, KF)
        oc = _conv_pre(oe, po8, wo_ref, bo_ref, KF)
        ge = jnp.where(j < nb, de * oc * _dsilu(sc), de * _silu(oc))
        dx = jnp.zeros((CL, CT), F32)
        for k in range(KF):
            dx = dx + ws_ref[k:k + 1, :] * _shift_up(ge, KF - 1 - k, CL)
        dx_ref[...] = dx.astype(BF16)

        @pl.when(i == 0)
        def _():
            dw_ref[...] = jnp.zeros_like(dw_ref)
            db_ref[...] = jnp.zeros_like(db_ref)

        gcur = ge[0:CL]
        xc = s_ref[...]
        for k in range(KF):
            dw_ref[k:k + 1, :] += jnp.sum(gcur * _shift_rows(xc, ps8, KF - 1 - k), axis=0, keepdims=True)
        db_ref[...] += jnp.sum(gcur, axis=0, keepdims=True)

    sp, scur, snx = _halo_specs(n, lambda j: j)
    op, ocur, onx = _halo_specs(n, lambda j: (j + nb) % (2 * nb))
    _, dcur, dnx = _halo_specs(n, lambda j: j % nb)
    wcol = lambda f: (pl.BlockSpec((KF, CT), lambda j, i: (0, f(j))), pl.BlockSpec((1, CT), lambda j, i: (0, f(j))))
    ws, bs = wcol(lambda j: j)
    wo, bo = wcol(lambda j: (j + nb) % (2 * nb))
    return pl.pallas_call(
        body, name="conv_ffn_bwd", grid=(2 * nb, nl),
        in_specs=[sp, scur, snx, op, ocur, onx, dcur, dnx, ws, bs, wo, bo],
        out_specs=[pl.BlockSpec((CL, CT), lambda j, i: (i, j)), pl.BlockSpec((KF, CT), lambda j, i: (0, j)),
                   pl.BlockSpec((1, CT), lambda j, i: (0, j))],
        out_shape=[jax.ShapeDtypeStruct((n, 2 * DFF), BF16), jax.ShapeDtypeStruct((KF, 2 * DFF), F32),
                   jax.ShapeDtypeStruct((1, 2 * DFF), F32)],
        compiler_params=_cp("parallel", "arbitrary"),
    )(up, up, up, up, up, up, dact, dact, w, b, w, b)


def _ssd_common(dtc_ref, dtr_ref, hpc_ref, hpr_ref, e8_ref):
    row = lax.broadcasted_iota(jnp.int32, (Q, Q), 0)
    col = lax.broadcasted_iota(jnp.int32, (Q, Q), 1)
    lower = row >= col
    upper = row <= col
    hpc, hpr = hpc_ref[...], hpr_ref[...]
    pre_c = dtc_ref[...] + hpc[0:1, :]
    dt_c = _softplus(pre_c)
    a_c = -jnp.exp(hpc[1:2, :])
    s_c = _dot(lower.astype(F32), dt_c * a_c, prec=HI)
    dt_r = _softplus(dtr_ref[...] + hpr[:, 0:1])
    s_r = _dot(dt_r * (-jnp.exp(hpr[:, 1:2])), upper.astype(F32), prec=HI)
    e8 = e8_ref[...]
    dt_e = _dot(dt_c, e8, prec=HI)
    s_e = _dot(s_c, e8, prec=HI)
    return lower, upper, pre_c, dt_c, a_c, s_c, s_r, dt_e, s_e


def _ssd_specs(nc, rev):
    cc = (lambda c: nc - 1 - c) if rev else (lambda c: c)
    return [
        pl.BlockSpec((Q, GW), lambda g, c: (cc(c), g)),
        pl.BlockSpec((Q, NS), lambda g, c: (cc(c), DI // NS + g)),
        pl.BlockSpec((Q, NS), lambda g, c: (cc(c), (DI + NG * NS) // NS + g)),
        pl.BlockSpec((None, Q, 8), lambda g, c: (g, cc(c), 0)),
        pl.BlockSpec((None, 8, Q), lambda g, c: (g, 0, cc(c))),
        pl.BlockSpec((None, 8, 8), lambda g, c: (g, 0, 0)),
        pl.BlockSpec((None, 8, 128), lambda g, c: (g, 0, 0)),
        pl.BlockSpec((1, GW), lambda g, c: (0, g)),
        pl.BlockSpec((8, GW), lambda g, c: (0, 0)),
    ]


def _ssd_fwd(xbc, dtc, dtr, hpc, hpr, dexp, e8):
    n = xbc.shape[0]
    nc = n // Q

    def body(xs_ref, b_ref, c_ref, dtc_ref, dtr_ref, hpc_ref, hpr_ref, dexp_ref, e8_ref, y_ref, sp_ref, st):
        @pl.when(pl.program_id(1) == 0)
        def _():
            st[...] = jnp.zeros_like(st)

        lower, _, _, _, _, s_c, s_r, dt_e, s_e = _ssd_common(dtc_ref, dtr_ref, hpc_ref, hpr_ref, e8_ref)
        xs = xs_ref[...]
        x = xs * dt_e
        xb = x.astype(BF16)
        bb, cb = b_ref[...].astype(BF16), c_ref[...].astype(BF16)
        cbm = _dot(cb, bb, NT)
        st_e = s_e[Q - 1:Q, :]
        sprev = st[...]
        sp_ref[...] = sprev
        yoff = _dot(cb, sprev.astype(BF16)) * jnp.exp(s_e) + dexp_ref[...] * xs
        for h in range(HPG):
            sl = slice(h * HD, (h + 1) * HD)
            lm = jnp.where(lower, jnp.exp(jnp.minimum(s_c[:, h:h + 1] - s_r[h:h + 1, :], 0.0)), 0.0)
            y_ref[:, sl] = _dot((cbm * lm).astype(BF16), xb[:, sl]) + yoff[:, sl]
        w = (x * jnp.exp(st_e - s_e)).astype(BF16)
        st[...] = jnp.exp(st_e) * sprev + _dot(bb, w, TN)

    return pl.pallas_call(
        body, name="ssd_fwd", grid=(NG, nc), in_specs=_ssd_specs(nc, False),
        out_specs=[pl.BlockSpec((Q, GW), lambda g, c: (c, g)),
                   pl.BlockSpec((None, None, NS, GW), lambda g, c: (c, g, 0, 0))],
        out_shape=[jax.ShapeDtypeStruct((n, DI), F32), jax.ShapeDtypeStruct((nc, NG, NS, GW), F32)],
        scratch_shapes=[pltpu.VMEM((NS, GW), F32)],
        compiler_params=_cp("parallel", "arbitrary"),
    )(xbc, xbc, xbc, dtc, dtr, hpc, hpr, dexp, e8)


def _ssd_bwd(xbc, dtc, dtr, hpc, hpr, dexp, e8, e8t, sprev_all, dy):
    n = xbc.shape[0]
    nc = n // Q
    rc = lambda c: nc - 1 - c

    def body(xs_ref, b_ref, c_ref, dtc_ref, dtr_ref, hpc_ref, hpr_ref, dexp_ref, e8_ref, e8t_ref, sp_ref, dy_ref,
             dxs_ref, db_ref, dc_ref, draw_ref, pd_ref, ps_ref, dst, dxbuf):
        @pl.when(pl.program_id(1) == 0)
        def _():
            dst[...] = jnp.zeros_like(dst)
            pd_ref[...] = jnp.zeros_like(pd_ref)
            ps_ref[...] = jnp.zeros_like(ps_ref)

        lower, upper, pre_c, dt_c, a_c, s_c, s_r, dt_e, s_e = _ssd_common(dtc_ref, dtr_ref, hpc_ref, hpr_ref, e8_ref)
        e8t = e8t_ref[...]
        xs = xs_ref[...]
        x = xs * dt_e
        xb = x.astype(BF16)
        bb, cb = b_ref[...].astype(BF16), c_ref[...].astype(BF16)
        cbm = _dot(cb, bb, NT)
        cbt = _dot(bb, cb, NT)
        st_e = s_e[Q - 1:Q, :]
        dec_out, dec_st, e_t = jnp.exp(s_e), jnp.exp(st_e - s_e), jnp.exp(st_e)
        dyv = dy_ref[...]
        dyb = dyv.astype(BF16)
        sprev = sp_ref[...]
        sb = sprev.astype(BF16)
        ds_in = dst[...]
        dsb = ds_in.astype(BF16)

        cs = _dot(cb, sb)
        dcs = (dyv * dec_out).astype(BF16)
        d_c = _dot(dcs, sb, NT)
        wf = x * dec_st
        d_w = _dot(bb, dsb)
        d_b = _dot(wf.astype(BF16), dsb, NT)
        tw = d_w * wf
        ds_c = _dot(dyv * cs * dec_out - tw, e8t, prec=HI)
        dcb = jnp.zeros((Q, Q), F32)
        dcbt = jnp.zeros((Q, Q), F32)
        lane8 = lax.broadcasted_iota(jnp.int32, (1, 8), 1)
        for h in range(HPG):
            sl = slice(h * HD, (h + 1) * HD)
            sc_h, sr_h = s_c[:, h:h + 1], s_r[h:h + 1, :]
            lm = jnp.where(lower, jnp.exp(jnp.minimum(sc_h - sr_h, 0.0)), 0.0)
            lmt = jnp.where(upper, jnp.exp(jnp.minimum(sr_h - sc_h, 0.0)), 0.0)
            mt = cbt * lmt
            dm = _dot(dyb[:, sl], xb[:, sl], NT)
            dmt = _dot(xb[:, sl], dyb[:, sl], NT)
            dxbuf[:, sl] = _dot(mt.astype(BF16), dyb[:, sl])
            dml = dm * lm
            dmlt = dmt * lmt
            dcb = dcb + dml
            dcbt = dcbt + dmlt
            dsh = jnp.sum(dml * cbm, axis=1, keepdims=True) - jnp.sum(dmlt * cbt, axis=1, keepdims=True)
            ds_c = ds_c + dsh * (lane8 == h).astype(F32)
        d_c = d_c + _dot(dcb.astype(BF16), bb)
        d_b = d_b + _dot(dcbt.astype(BF16), cb)
        dx = d_w * dec_st + dxbuf[...]
        tsum = jnp.sum(tw, axis=0, keepdims=True) + jnp.sum(ds_in * sprev, axis=0, keepdims=True) * e_t
        ds_t = _dot(jnp.broadcast_to(tsum, (8, GW)), e8t, prec=HI)[0:1, :]
        rows = lax.broadcasted_iota(jnp.int32, (Q, 8), 0)
        ds_c = ds_c + jnp.where(rows == Q - 1, ds_t, 0.0)
        d_a = _dot(upper.astype(F32), ds_c, prec=HI)
        ddt = _dot(dx * xs, e8t, prec=HI) + d_a * a_c
        draw = ddt * _sig(pre_c)
        draw_ref[...] = draw
        ps_ref[0:1, :] += jnp.sum(draw, axis=0, keepdims=True)
        ps_ref[1:2, :] += jnp.sum(d_a * dt_c, axis=0, keepdims=True) * a_c
        pd_ref[...] += jnp.sum(dyv * xs, axis=0, keepdims=True)
        dxs_ref[...] = dx * dt_e + dyv * dexp_ref[...]
        db_ref[...] = d_b
        dc_ref[...] = d_c
        dst[...] = e_t * ds_in + _dot(cb, dcs, TN)

    in_specs = _ssd_specs(nc, True) + [
        pl.BlockSpec((GW, 8), lambda g, c: (0, 0)),
        pl.BlockSpec((None, None, NS, GW), lambda g, c: (rc(c), g, 0, 0)),
        pl.BlockSpec((Q, GW), lambda g, c: (rc(c), g)),
    ]
    return pl.pallas_call(
        body, name="ssd_bwd", grid=(NG, nc), in_specs=in_specs,
        out_specs=[pl.BlockSpec((Q, GW), lambda g, c: (rc(c), g)),
                   pl.BlockSpec((Q, NS), lambda g, c: (rc(c), g)),
                   pl.BlockSpec((Q, NS), lambda g, c: (rc(c), g)),
                   pl.BlockSpec((None, Q, 8), lambda g, c: (g, rc(c), 0)),
                   pl.BlockSpec((None, 1, GW), lambda g, c: (g, 0, 0)),
                   pl.BlockSpec((None, 8, 8), lambda g, c: (g, 0, 0))],
        out_shape=[jax.ShapeDtypeStruct((n, DI), F32), jax.ShapeDtypeStruct((n, NG * NS), F32),
                   jax.ShapeDtypeStruct((n, NG * NS), F32), jax.ShapeDtypeStruct((NG, n, 8), F32),
                   jax.ShapeDtypeStruct((NG, 1, GW), F32), jax.ShapeDtypeStruct((NG, 8, 8), F32)],
        scratch_shapes=[pltpu.VMEM((NS, GW), F32), pltpu.VMEM((Q, GW), F32)],
        compiler_params=_cp("parallel", "arbitrary"),
    )(xbc, xbc, xbc, dtc, dtr, hpc, hpr, dexp, e8, e8t, sprev_all, dy)


GL = 128


def _gnorm_fwd(y, proj, w):
    n = y.shape[0]

    def body(y_ref, z_ref, w_ref, o_ref):
        for g in range(NG):
            sl = slice(g * GW, (g + 1) * GW)
            yz = y_ref[:, sl] * _silu(z_ref[:, sl])
            r = lax.rsqrt(jnp.mean(yz * yz, axis=-1, keepdims=True) + EPS)
            o_ref[:, sl] = (yz * r * w_ref[:, sl]).astype(BF16)

    row = pl.BlockSpec((GL, DI), lambda i: (i, 0))
    return pl.pallas_call(
        body, name="gnorm_fwd", grid=(n // GL,),
        in_specs=[row, row, pl.BlockSpec((1, DI), lambda i: (0, 0))], out_specs=row,
        out_shape=jax.ShapeDtypeStruct((n, DI), BF16), compiler_params=_cp("parallel"),
    )(y, proj, w)


def _gnorm_bwd(dyn, y, proj, w):
    n = y.shape[0]

    def body(d_ref, y_ref, z_ref, w_ref, dy_ref, dz_ref, gw_ref):
        @pl.when(pl.program_id(0) == 0)
        def _():
            gw_ref[...] = jnp.zeros_like(gw_ref)

        for g in range(NG):
            sl = slice(g * GW, (g + 1) * GW)
            yv, zv, dv = y_ref[:, sl], z_ref[:, sl], d_ref[:, sl]
            sz = _silu(zv)
            yz = yv * sz
            r = lax.rsqrt(jnp.mean(yz * yz, axis=-1, keepdims=True) + EPS)
            yh = yz * r
            gg = dv * w_ref[:, sl]
            dyz = r * (gg - yh * jnp.mean(gg * yh, axis=-1, keepdims=True))
            gw_ref[:, sl] += jnp.sum(dv * yh, axis=0, keepdims=True)
            dy_ref[:, sl] = dyz * sz
            dz_ref[:, sl] = (dyz * yv * _dsilu(zv)).astype(BF16)

    row = pl.BlockSpec((GL, DI), lambda i: (i, 0))
    vec = pl.BlockSpec((1, DI), lambda i: (0, 0))
    return pl.pallas_call(
        body, name="gnorm_bwd", grid=(n // GL,),
        in_specs=[row, row, row, vec], out_specs=[row, row, vec],
        out_shape=[jax.ShapeDtypeStruct((n, DI), F32), jax.ShapeDtypeStruct((n, DI), BF16),
                   jax.ShapeDtypeStruct((1, DI), F32)],
        compiler_params=_cp("arbitrary"),
    )(dyn, y, proj, w)


SL = 512
SB = 8
SCB = NCH // SB


def _s5_in(proj, bre, bim):
    n = proj.shape[0]
    uoff = OFF_U // 128

    def body(u_ref, br_ref, bi_ref, or_ref, oi_ref):
        u = u_ref[...].astype(BF16)
        or_ref[...] = _dot(u, br_ref[...])
        oi_ref[...] = _dot(u, bi_ref[...])

    blk = pl.BlockSpec((None, 128, SCB), lambda i, j: (j, 0, 0))
    out = pl.BlockSpec((SL, SCB), lambda i, j: (i, j))
    return pl.pallas_call(
        body, name="s5_in", grid=(n // SL, SB),
        in_specs=[pl.BlockSpec((SL, 128), lambda i, j: (i, uoff + j)), blk, blk], out_specs=[out, out],
        out_shape=[jax.ShapeDtypeStruct((n, NCH), F32)] * 2, compiler_params=_cp("parallel", "parallel"),
    )(proj, bre, bim)


SC = 256


def _s5_scan(vre, vim, tab, reverse, name):
    n = vre.shape[0]
    nl = n // SL
    ng = SL // 8
    ti = (lambda i: nl - 1 - i) if reverse else (lambda i: i)

    def body(re_ref, im_ref, tab_ref, ore_ref, oim_ref, cre, cim):
        @pl.when(pl.program_id(1) == 0)
        def _():
            cre[...] = jnp.zeros_like(cre)
            cim[...] = jnp.zeros_like(cim)

        def step(j, carry):
            cr, ci = carry
            jj = (ng - 1 - j) if reverse else j
            rows = pl.ds(pl.multiple_of(jj * 8, 8), 8)
            vr, vi = re_ref[rows, :], im_ref[rows, :]
            for t, k in enumerate((1, 2, 4)):
                sh = (8 - k) if reverse else k
                rr, ri = pltpu.roll(vr, sh, 0), pltpu.roll(vi, sh, 0)
                pr, pi = tab_ref[2 * t], tab_ref[2 * t + 1]
                vr, vi = vr + pr * rr - pi * ri, vi + pr * ri + pi * rr
            lr, li = tab_ref[6], tab_ref[7]
            vr, vi = vr + lr * cr - li * ci, vi + lr * ci + li * cr
            ore_ref[rows, :] = vr
            oim_ref[rows, :] = vi
            e = 0 if reverse else 7
            return (jnp.broadcast_to(vr[e:e + 1, :], (8, SC)), jnp.broadcast_to(vi[e:e + 1, :], (8, SC)))

        cr, ci = lax.fori_loop(0, ng, step, (cre[...], cim[...]))
        cre[...] = cr
        cim[...] = ci

    blk = pl.BlockSpec((SL, SC), lambda j, i: (ti(i), j))
    return pl.pallas_call(
        body, name=name, grid=(NCH // SC, nl),
        in_specs=[blk, blk, pl.BlockSpec((8, 8, SC), lambda j, i: (0, 0, j))], out_specs=[blk, blk],
        out_shape=[jax.ShapeDtypeStruct((n, NCH), F32)] * 2,
        scratch_shapes=[pltpu.VMEM((8, SC), F32), pltpu.VMEM((8, SC), F32)],
        compiler_params=_cp("parallel", "arbitrary"),
    )(vre, vim, tab)


def _s5_out(xre, xim, cre, cimn, proj, dvec):
    n = xre.shape[0]
    uoff = OFF_U // 128

    def body(xr_ref, xi_ref, cr_ref, ci_ref, u_ref, d_ref, y_ref, g_ref):
        y = (_dot(xr_ref[...].astype(BF16), cr_ref[...]) + _dot(xi_ref[...].astype(BF16), ci_ref[...])
             + d_ref[...] * u_ref[...])
        y_ref[...] = y
        g_ref[...] = _gelu(y).astype(BF16)

    xs = pl.BlockSpec((SL, SCB), lambda i, j: (i, j))
    blk = pl.BlockSpec((None, SCB, 128), lambda i, j: (j, 0, 0))
    out = pl.BlockSpec((SL, 128), lambda i, j: (i, j))
    return pl.pallas_call(
        body, name="s5_out", grid=(n // SL, SB),
        in_specs=[xs, xs, blk, blk, pl.BlockSpec((SL, 128), lambda i, j: (i, uoff + j)),
                  pl.BlockSpec((1, 128), lambda i, j: (0, j))],
        out_specs=[out, out],
        out_shape=[jax.ShapeDtypeStruct((n, DS5), F32), jax.ShapeDtypeStruct((n, DS5), BF16)],
        compiler_params=_cp("parallel", "parallel"),
    )(xre, xim, cre, cimn, proj, dvec)


def _s5_out_bwd(dg, ypre, crt, cimnt, proj, dvec, xre, xim):
    n = dg.shape[0]
    uoff = OFF_U // 128
    nl = n // SL

    def body(dg_ref, y_ref, cr_ref, ci_ref, u_ref, d_ref, xr_ref, xi_ref,
             gr_ref, gi_ref, dus_ref, gcr_ref, gci_ref, gd_ref):
        dy = dg_ref[...] * _dgelu(y_ref[...])
        dyb = dy.astype(BF16)
        gr_ref[...] = _dot(dyb, cr_ref[...])
        gi_ref[...] = _dot(dyb, ci_ref[...])
        dus_ref[...] = dy * d_ref[...]

        @pl.when(pl.program_id(1) == 0)
        def _():
            gcr_ref[...] = jnp.zeros_like(gcr_ref)
            gci_ref[...] = jnp.zeros_like(gci_ref)
            gd_ref[...] = jnp.zeros_like(gd_ref)

        gcr_ref[...] += _dot(xr_ref[...].astype(BF16), dyb, TN)
        gci_ref[...] -= _dot(xi_ref[...].astype(BF16), dyb, TN)
        gd_ref[...] += jnp.sum(dy * u_ref[...], axis=0, keepdims=True)

    u128 = pl.BlockSpec((SL, 128), lambda j, i: (i, j))
    xs = pl.BlockSpec((SL, SCB), lambda j, i: (i, j))
    blk = pl.BlockSpec((None, 128, SCB), lambda j, i: (j, 0, 0))
    gblk = pl.BlockSpec((None, SCB, 128), lambda j, i: (j, 0, 0))
    vec = pl.BlockSpec((1, 128), lambda j, i: (0, j))
    return pl.pallas_call(
        body, name="s5_out_bwd", grid=(SB, nl),
        in_specs=[u128, u128, blk, blk, pl.BlockSpec((SL, 128), lambda j, i: (i, uoff + j)), vec, xs, xs],
        out_specs=[xs, xs, u128, gblk, gblk, vec],
        out_shape=[jax.ShapeDtypeStruct((n, NCH), F32)] * 2 + [jax.ShapeDtypeStruct((n, DS5), F32)]
        + [jax.ShapeDtypeStruct((SB, SCB, 128), F32)] * 2 + [jax.ShapeDtypeStruct((1, DS5), F32)],
        compiler_params=_cp("parallel", "arbitrary"),
    )(dg, ypre, crt, cimnt, proj, dvec, xre, xim)


def _s5_in_bwd(are, aim, brt, bit, proj, dus, xre, xim):
    n = are.shape[0]
    uoff = OFF_U // 128
    per = SL // 8

    def body(ar_ref, ai_ref, br_ref, bi_ref, u_ref, dus_ref, xr_ref, xi_ref, pr_ref, pi_ref,
             du_ref, gbr_ref, gbi_ref, glr_ref, gli_ref):
        i = pl.program_id(1)
        ar, ai = ar_ref[...], ai_ref[...]
        arb, aib = ar.astype(BF16), ai.astype(BF16)
        du_ref[...] = (_dot(arb, br_ref[...]) + _dot(aib, bi_ref[...]) + dus_ref[...]).astype(BF16)

        @pl.when(i == 0)
        def _():
            for r in (gbr_ref, gbi_ref, glr_ref, gli_ref):
                r[...] = jnp.zeros_like(r)

        ub = u_ref[...].astype(BF16)
        gbr_ref[...] += _dot(arb, ub, TN)
        gbi_ref[...] += _dot(aib, ub, TN)
        row0 = lax.broadcasted_iota(jnp.int32, (SL, SCB), 0) == 0
        last_r = jnp.where(i > 0, pr_ref[7:8, :], 0.0)
        last_i = jnp.where(i > 0, pi_ref[7:8, :], 0.0)
        xpr = jnp.where(row0, last_r, pltpu.roll(xr_ref[...], 1, 0))
        xpi = jnp.where(row0, last_i, pltpu.roll(xi_ref[...], 1, 0))
        glr_ref[...] += jnp.sum(ar * xpr + ai * xpi, axis=0, keepdims=True)
        gli_ref[...] += jnp.sum(ai * xpr - ar * xpi, axis=0, keepdims=True)

    xs = pl.BlockSpec((SL, SCB), lambda j, i: (i, j))
    prev = pl.BlockSpec((8, SCB), lambda j, i: (jnp.maximum(i * per - 1, 0), j))
    blk = pl.BlockSpec((None, SCB, 128), lambda j, i: (j, 0, 0))
    u128 = pl.BlockSpec((SL, 128), lambda j, i: (i, j))
    vec = pl.BlockSpec((1, SCB), lambda j, i: (0, j))
    return pl.pallas_call(
        body, name="s5_in_bwd", grid=(SB, n // SL),
        in_specs=[xs, xs, blk, blk, pl.BlockSpec((SL, 128), lambda j, i: (i, uoff + j)), u128, xs, xs, prev, prev],
        out_specs=[u128, blk, blk, vec, vec],
        out_shape=[jax.ShapeDtypeStruct((n, DS5), BF16)] + [jax.ShapeDtypeStruct((SB, SCB, 128), F32)] * 2
        + [jax.ShapeDtypeStruct((1, NCH), F32)] * 2,
        compiler_params=_cp("parallel", "arbitrary"),
    )(are, aim, brt, bit, proj, dus, xre, xim, xre, xim)


MC = 1024


def _merge_specs():
    ga = pl.BlockSpec((TL, MC), lambda i, j: (i, OFF_GA // MC + j))
    gb = pl.BlockSpec((TL, MC), lambda i, j: (i, OFF_GB // MC + j))
    col = pl.BlockSpec((TL, MC), lambda i, j: (i, j))
    gate = pl.BlockSpec((TL, MC), lambda i, j: (i, D // MC + j))
    return ga, gb, col, gate


def _merge_fwd(proj, ya, vg):
    n = ya.shape[0]

    def body(ga_ref, gb_ref, ya_ref, v_ref, g_ref, o_ref):
        yb = v_ref[...] * _sig(g_ref[...])
        o_ref[...] = (_sig(ga_ref[...]) * ya_ref[...] + _sig(gb_ref[...]) * yb).astype(BF16)

    ga, gb, col, gate = _merge_specs()
    return pl.pallas_call(
        body, name="merge_fwd", grid=(n // TL, D // MC), in_specs=[ga, gb, col, col, gate], out_specs=col,
        out_shape=jax.ShapeDtypeStruct((n, D), BF16), compiler_params=_cp("parallel", "parallel"),
    )(proj, proj, ya, vg, vg)


def _merge_bwd(dm, proj, ya, vg):
    n = ya.shape[0]

    def body(dm_ref, ga_ref, gb_ref, ya_ref, v_ref, g_ref, dga_ref, dgb_ref, dya_ref, dv_ref, dg_ref):
        d = dm_ref[...]
        sa, sb, sg = _sig(ga_ref[...]), _sig(gb_ref[...]), _sig(g_ref[...])
        v = v_ref[...]
        yb = v * sg
        dga_ref[...] = (d * ya_ref[...] * sa * (1.0 - sa)).astype(BF16)
        dgb_ref[...] = (d * yb * sb * (1.0 - sb)).astype(BF16)
        dya_ref[...] = (d * sa).astype(BF16)
        dyb = d * sb
        dv_ref[...] = (dyb * sg).astype(BF16)
        dg_ref[...] = (dyb * v * sg * (1.0 - sg)).astype(BF16)

    ga, gb, col, gate = _merge_specs()
    o = jax.ShapeDtypeStruct((n, D), BF16)
    return pl.pallas_call(
        body, name="merge_bwd", grid=(n // TL, D // MC), in_specs=[col, ga, gb, col, col, gate],
        out_specs=[col] * 5, out_shape=[o] * 5, compiler_params=_cp("parallel", "parallel"),
    )(dm, proj, proj, ya, vg, vg)


def _adamw_update(wv, gv, mv, vv):
    nm = B1 * mv + (1.0 - B1) * gv
    nv = B2 * vv + (1.0 - B2) * (gv * gv)
    m_hat = nm / (1.0 - B1 ** STEP)
    v_hat = nv / (1.0 - B2 ** STEP)
    return -LR * (m_hat / (jnp.sqrt(v_hat) + AEPS) + WD * wv), nm, nv


def _adamw(w, g, m, v, name):
    r, c = w.shape
    tr = _pick(r, 128)

    def body(w_ref, g_ref, m_ref, v_ref, d_ref, nm_ref, nv_ref):
        d_ref[...], nm_ref[...], nv_ref[...] = _adamw_update(w_ref[...], g_ref[...], m_ref[...], v_ref[...])

    blk = pl.BlockSpec((tr, c), lambda i: (i, 0))
    o = jax.ShapeDtypeStruct((r, c), F32)
    return pl.pallas_call(
        body, name=name, grid=(r // tr,), in_specs=[blk] * 4, out_specs=[blk] * 3, out_shape=[o] * 3,
        compiler_params=_cp("parallel"),
    )(w, g, m, v)


def _adamw_halves(w, g_mine, g_other, m, v, cidx, name):
    _, r, c = w.shape
    hr, gc = g_mine.shape
    tr = _pick(hr, 128)
    nbh = hr // tr

    def body(cs, w_ref, gm_ref, go_ref, m_ref, v_ref, g_ref, d_ref, nm_ref, nv_ref):
        mine = pl.program_id(0) // nbh == cs[0]
        gv = jnp.where(mine, gm_ref[:, 0:c], go_ref[:, 0:c])
        g_ref[...] = gv
        d_ref[...], nm_ref[...], nv_ref[...] = _adamw_update(w_ref[...], gv, m_ref[...], v_ref[...])

    blk = pl.BlockSpec((None, tr, c), lambda i, cs: (0, i, 0))
    gblk = pl.BlockSpec((tr, gc), lambda i, cs: (i % nbh, 0))
    o = jax.ShapeDtypeStruct((1, r, c), F32)
    return pl.pallas_call(
        body, name=name,
        grid_spec=pltpu.PrefetchScalarGridSpec(num_scalar_prefetch=1, grid=(r // tr,),
                                               in_specs=[blk, gblk, gblk, blk, blk], out_specs=[blk] * 4),
        out_shape=[o] * 4, compiler_params=_cp("parallel"),
    )(cidx, w, g_mine, g_other, m, v)


def _chip_sum(part, sib, cidx, name):
    _, r, cc = part.shape
    hr = r // 2
    tr = _pick(hr, 256)

    def body(cs, p_ref, s_ref, o_ref):
        o_ref[...] = (p_ref[...].astype(F32) + s_ref[...].astype(F32)).astype(BF16)

    blk = pl.BlockSpec((None, tr, cc), lambda k, i, cs: (k, i, 0))
    return pl.pallas_call(
        body, name=name,
        grid_spec=pltpu.PrefetchScalarGridSpec(
            num_scalar_prefetch=1, grid=(4, hr // tr),
            in_specs=[pl.BlockSpec((None, None, tr, cc), lambda k, i, cs: (k, cs[0], i, 0)), blk], out_specs=blk),
        out_shape=jax.ShapeDtypeStruct((4, hr, cc), BF16), compiler_params=_cp("parallel", "parallel"),
    )(cidx, part.reshape(4, 2, hr, cc), sib)


def _shard_sum(own, got, sidx, name):
    _, hr, cc = own.shape
    tr = _pick(hr, 256)

    def body(cs, own_ref, g0, g1, g2, g3, o_ref):
        acc = None
        for k, g_ref in enumerate((g0, g1, g2, g3)):
            term = jnp.where(cs[0] == k, own_ref[...], g_ref[...]).astype(F32)
            acc = term if acc is None else acc + term
        o_ref[...] = acc

    def got_spec(k):
        return pl.BlockSpec((None, tr, cc), lambda i, cs: (jnp.where(cs[0] == k, (k + 1) % 4, k), i, 0))

    return pl.pallas_call(
        body, name=name,
        grid_spec=pltpu.PrefetchScalarGridSpec(
            num_scalar_prefetch=1, grid=(hr // tr,),
            in_specs=[pl.BlockSpec((None, tr, cc), lambda i, cs: (cs[0], i, 0))] + [got_spec(k) for k in range(4)],
            out_specs=pl.BlockSpec((tr, cc), lambda i, cs: (i, 0))),
        out_shape=jax.ShapeDtypeStruct((hr, cc), F32), compiler_params=_cp("parallel"),
    )(sidx, own, got, got, got, got)


def _sum_slabs(xs, name, out_dtype=F32):
    r, c = xs[0].shape
    tr = _pick(r, 256)

    def body(*refs):
        acc = refs[0][...].astype(F32)
        for ref in refs[1:-1]:
            acc = acc + ref[...].astype(F32)
        refs[-1][...] = acc.astype(out_dtype)

    blk = pl.BlockSpec((tr, c), lambda i: (i, 0))
    return pl.pallas_call(
        body, name=name, grid=(r // tr,), in_specs=[blk] * len(xs), out_specs=blk,
        out_shape=jax.ShapeDtypeStruct((r, c), out_dtype), compiler_params=_cp("parallel"),
    )(*xs)


def _place():
    return lax.axis_index("x"), lax.axis_index("y"), lax.axis_index("c")


def _gather_small(v):
    m_per, n = v.shape

    def body(x_ref, out_ref, send_sems, recv_sems, local_sem):
        x, y, c = _place()
        me, sibling = (x, y, c), (x, y, 1 - c)
        chips = [(1 - x, y), (x, 1 - y), (1 - x, 1 - y)]

        def rows(px, py, pc):
            return out_ref.at[pl.ds((4 * px + 2 * py + pc) * m_per, m_per), :]

        def copy(k, block, to, src=None):
            return pltpu.make_async_remote_copy(
                src_ref=rows(*block) if src is None else src, dst_ref=rows(*block),
                send_sem=send_sems.at[k], recv_sem=recv_sems.at[k], device_id=to, device_id_type=MESH)

        mine = pltpu.make_async_copy(x_ref, rows(*me), local_sem)
        mine.start()
        first = [copy(0, me, sibling, src=x_ref)]
        first += [copy(1 + j, me, (*chip, c), src=x_ref) for j, chip in enumerate(chips)]
        for cp in first:
            cp.start()
        passed = [copy(4 + j, (*chip, c), sibling) for j, chip in enumerate(chips)]
        for j, chip in enumerate(chips):
            copy(1 + j, (*chip, c), me).wait_recv()
            passed[j].start()
        copy(0, sibling, me).wait_recv()
        for j, chip in enumerate(chips):
            copy(4 + j, (*chip, 1 - c), me).wait_recv()
        for cp in first + passed:
            cp.wait_send()
        mine.wait()

    return pl.pallas_call(
        body, name="gather_small_%d" % m_per,
        out_shape=jax.ShapeDtypeStruct((8 * m_per, n), v.dtype),
        in_specs=[pl.BlockSpec(memory_space=pltpu.VMEM)], out_specs=pl.BlockSpec(memory_space=pltpu.VMEM),
        scratch_shapes=[pltpu.SemaphoreType.DMA((7,)), pltpu.SemaphoreType.DMA((7,)), pltpu.SemaphoreType.DMA],
        compiler_params=pltpu.CompilerParams(vmem_limit_bytes=VMEM_LIMIT),
    )(v)


def _allsum_small(v, name):
    r = v.shape[0]
    g = _gather_small(v)
    return _sum_slabs([g[k * r:(k + 1) * r] for k in range(8)], name)


def _gather_big(shards):
    nt = len(shards)

    def body(*refs):
        ins, outs = refs[:nt], refs[nt:2 * nt]
        send_sems, recv_sems = refs[2 * nt:]
        x, y, c = _place()
        s = 2 * x + y
        sibling = (x, y, 1 - c)
        chips = [(1 - x, y), (x, 1 - y), (1 - x, 1 - y)]

        def half(t, slot, h):
            hr = ins[t].shape[0] // 2
            return outs[t].at[slot, pl.ds(h * hr, hr), :]

        def ici(t, j, src_slot, to):
            hr = ins[t].shape[0] // 2
            return pltpu.make_async_remote_copy(
                src_ref=ins[t].at[pl.ds(c * hr, hr), :], dst_ref=half(t, src_slot, c),
                send_sem=send_sems.at[6 * t + j], recv_sem=recv_sems.at[6 * t + j], device_id=to, device_id_type=MESH)

        def d2d(t, j, slot, h):
            return pltpu.make_async_remote_copy(
                src_ref=half(t, slot, h), dst_ref=half(t, slot, h),
                send_sem=send_sems.at[6 * t + 3 + j], recv_sem=recv_sems.at[6 * t + 3 + j],
                device_id=sibling, device_id_type=MESH)

        sends = [ici(t, j, s, (*chip, c)) for t in range(nt) for j, chip in enumerate(chips)]
        for cp in sends:
            cp.start()
        passed = []
        for t in range(nt):
            for j, (px, py) in enumerate(chips):
                ici(t, j, 2 * px + py, (x, y, c)).wait_recv()
                cp = d2d(t, j, 2 * px + py, c)
                cp.start()
                passed.append(cp)
        for t in range(nt):
            for j, (px, py) in enumerate(chips):
                d2d(t, j, 2 * px + py, 1 - c).wait_recv()
        for cp in sends + passed:
            cp.wait_send()

    return pl.pallas_call(
        body, name="gather_big",
        out_shape=[jax.ShapeDtypeStruct((4,) + a.shape, a.dtype) for a in shards],
        in_specs=[ANY] * nt, out_specs=[ANY] * nt,
        scratch_shapes=[pltpu.SemaphoreType.DMA((6 * nt,)), pltpu.SemaphoreType.DMA((6 * nt,))],
    )(*shards)


def _swap_halves(parts):
    nt = len(parts)

    def body(*refs):
        ins, outs = refs[:nt], refs[nt:2 * nt]
        send_sems, recv_sems = refs[2 * nt:]
        x, y, c = _place()
        cps = []
        for t in range(nt):
            hr = ins[t].shape[1] // 2
            cps.append(pltpu.make_async_remote_copy(
                src_ref=ins[t].at[:, pl.ds((1 - c) * hr, hr), :], dst_ref=outs[t],
                send_sem=send_sems.at[t], recv_sem=recv_sems.at[t], device_id=(x, y, 1 - c), device_id_type=MESH))
        for cp in cps:
            cp.start()
        for cp in cps:
            cp.wait()

    return pl.pallas_call(
        body, name="swap_halves",
        out_shape=[jax.ShapeDtypeStruct((4, a.shape[1] // 2, a.shape[2]), a.dtype) for a in parts],
        in_specs=[ANY] * nt, out_specs=[ANY] * nt,
        scratch_shapes=[pltpu.SemaphoreType.DMA((nt,)), pltpu.SemaphoreType.DMA((nt,))],
    )(*parts)


def _scatter_chips(parts):
    nt = len(parts)

    def body(*refs):
        ins, outs = refs[:nt], refs[nt:2 * nt]
        send_sems, recv_sems = refs[2 * nt:]
        x, y, c = _place()
        s = 2 * x + y
        chips = [(1 - x, y), (x, 1 - y), (1 - x, 1 - y)]
        cps = []
        for t in range(nt):
            for j, (px, py) in enumerate(chips):
                cps.append(pltpu.make_async_remote_copy(
                    src_ref=ins[t].at[2 * px + py], dst_ref=outs[t].at[s],
                    send_sem=send_sems.at[3 * t + j], recv_sem=recv_sems.at[3 * t + j],
                    device_id=(px, py, c), device_id_type=MESH))
        for cp in cps:
            cp.start()
        for t in range(nt):
            for j, (px, py) in enumerate(chips):
                pltpu.make_async_remote_copy(
                    src_ref=ins[t].at[s], dst_ref=outs[t].at[2 * px + py],
                    send_sem=send_sems.at[3 * t + j], recv_sem=recv_sems.at[3 * t + j],
                    device_id=(px, py, c), device_id_type=MESH).wait_recv()
        for cp in cps:
            cp.wait_send()

    return pl.pallas_call(
        body, name="scatter_chips",
        out_shape=[jax.ShapeDtypeStruct(a.shape, a.dtype) for a in parts],
        in_specs=[ANY] * nt, out_specs=[ANY] * nt,
        scratch_shapes=[pltpu.SemaphoreType.DMA((3 * nt,)), pltpu.SemaphoreType.DMA((3 * nt,))],
    )(*parts)


def _swap_whole(halves):
    nt = len(halves)

    def body(*refs):
        ins, outs = refs[:nt], refs[nt:2 * nt]
        send_sems, recv_sems = refs[2 * nt:]
        x, y, c = _place()
        cps = [pltpu.make_async_remote_copy(
            src_ref=ins[t], dst_ref=outs[t], send_sem=send_sems.at[t], recv_sem=recv_sems.at[t],
            device_id=(x, y, 1 - c), device_id_type=MESH) for t in range(nt)]
        for cp in cps:
            cp.start()
        for cp in cps:
            cp.wait()

    return pl.pallas_call(
        body, name="swap_whole",
        out_shape=[jax.ShapeDtypeStruct(a.shape, a.dtype) for a in halves],
        in_specs=[ANY] * nt, out_specs=[ANY] * nt,
        scratch_shapes=[pltpu.SemaphoreType.DMA((nt,)), pltpu.SemaphoreType.DMA((nt,))],
    )(*halves)


def _reduce_big(parts, cidx, sidx):
    nt = len(parts)
    sib = _swap_halves(parts)
    chip_sum = [_chip_sum(parts[t], sib[t], cidx, "chip_sum_%d" % t) for t in range(nt)]
    got = _scatter_chips(chip_sum)
    mine = [_shard_sum(chip_sum[t], got[t], sidx, "shard_sum_%d" % t) for t in range(nt)]
    return mine, _swap_whole(mine)


def _s5_params(lam_re, lam_im, log_dt, b_re, b_im):
    lr = jnp.minimum(lam_re, EIG_MAX)
    dt = jnp.exp(log_dt)[:, None]
    mag = jnp.exp(lr * dt)
    lbr, lbi = mag * jnp.cos(lam_im * dt), mag * jnp.sin(lam_im * dt)
    den = lr * lr + lam_im * lam_im
    qr = ((lbr - 1.0) * lr + lbi * lam_im) / den
    qi = (lbi * lr - (lbr - 1.0) * lam_im) / den
    bbr = qr[..., None] * b_re - qi[..., None] * b_im
    bbi = qr[..., None] * b_im + qi[..., None] * b_re
    return lbr, lbi, bbr, bbi


def _cmul(a, b):
    return a[0] * b[0] - a[1] * b[1], a[0] * b[1] + a[1] * b[0]


def _scan_table(lr, li, reverse):
    l1 = (lr.reshape(1, NCH), li.reshape(1, NCH))
    pows = [l1]
    for _ in range(7):
        pows.append(_cmul(pows[-1], l1))
    r = jnp.arange(8)[:, None]
    tabs = []
    for k in (1, 2, 4):
        keep = (r < 8 - k) if reverse else (r >= k)
        tabs += [jnp.where(keep, pows[k - 1][0], 0.0), jnp.where(keep, pows[k - 1][1], 0.0)]
    order = range(7, -1, -1) if reverse else range(8)
    tabs += [jnp.concatenate([pows[e][0] for e in order], axis=0), jnp.concatenate([pows[e][1] for e in order], axis=0)]
    return jnp.stack(tabs).astype(F32)


_EYE8 = lambda: jnp.eye(8, dtype=F32)


def _to_in_blocks(b):
    return jnp.einsum("jgpc,gh->jgchp", b.reshape(8, 8, 64, 16), _EYE8()).reshape(8, 128, 512)


def _to_out_blocks(cm):
    return jnp.einsum("jgcp,gh->jgphc", cm.reshape(8, 8, 16, 64), _EYE8()).reshape(8, 512, 128)


def _from_out_blocks(g):
    return jnp.einsum("jgphc,gh->jgpc", g.reshape(8, 8, 64, 8, 16), _EYE8()).reshape(64, 64, 16)


def _local_step(x, target, p):
    n = x.shape[0]
    g = {}
    hn1 = _rms_fwd(x, p["norm_mix_w"], "rms_mix")
    proj = _matmul(hn1, p["w_main"], "nn", "mm_in")
    dtraw = _matmul(hn1, p["w_dt"], "nn", "mm_dt")
    xbc = _conv_a_fwd(proj, p["conv_a_w"], p["conv_a_b"])
    dth = jnp.pad(dtraw[:, :NG * HPG].reshape(n, NG, HPG), ((0, 0), (0, 0), (0, 8 - HPG)))
    dtc = dth.transpose(1, 0, 2)
    dtr = dth.transpose(1, 2, 0)
    pad_h = lambda v: jnp.pad(v.reshape(NG, HPG), ((0, 0), (0, 8 - HPG)))
    hpc = jnp.zeros((NG, 8, 8), F32).at[:, 0, :].set(pad_h(p["dt_bias"])).at[:, 1, :].set(pad_h(p["a_log"]))
    hpr = jnp.zeros((NG, 8, 128), F32).at[:, :, 0].set(pad_h(p["dt_bias"])).at[:, :, 1].set(pad_h(p["a_log"]))
    dexp = jnp.repeat(p["d_a"].reshape(1, NG * HPG), HD, axis=1)
    e8 = (jnp.arange(8)[:, None] == (jnp.arange(GW)[None, :] // HD)).astype(F32)
    yssd, sprev = _ssd_fwd(xbc, dtc, dtr, hpc, hpr, dexp, e8)
    yn = _gnorm_fwd(yssd, proj, p["norm_a_w"])
    ya = _matmul(yn, p["w_proj_a"], "nn", "mm_proj")

    (lbr, lbi, bbr, bbi), s5_vjp = jax.vjp(_s5_params, p["s5_lam_re"], p["s5_lam_im"], p["s5_log_dt"],
                                           p["s5_b_re"], p["s5_b_im"])
    bin_r, bin_i = _to_in_blocks(bbr), _to_in_blocks(bbi)
    cout_r, cout_in = _to_out_blocks(p["s5_c_re"]), _to_out_blocks(-p["s5_c_im"])
    bur, bui = _s5_in(proj, bin_r.astype(BF16), bin_i.astype(BF16))
    xre, xim = _s5_scan(bur, bui, _scan_table(lbr, lbi, False), False, "s5_scan_fwd")
    ypre, g5 = _s5_out(xre, xim, cout_r.astype(BF16), cout_in.astype(BF16), proj, p["s5_d"])
    vg = _matmul(g5, p["w_s5_glu"], "nn", "mm_glu", b_stacked=True)
    merged = _merge_fwd(proj, ya, vg)
    h1 = _matmul(merged, p["w_out"], "nn", "mm_out", residual=x)
    hn2 = _rms_fwd(h1, p["norm_ffn_w"], "rms_ffn")
    up = _matmul(hn2, p["w_up"], "nn", "mm_up", tn=1408, b_stacked=True)
    act = _conv_ffn_fwd(up, p["conv_ffn_w"], p["conv_ffn_b"])
    h2 = _matmul(act, p["w_down"], "nn", "mm_down", tk=DFF // 2, residual=h1)
    dh2, dh2b, g["norm_final_w"], loss_blk = _final(h2, p["norm_final_w"], target)
    g["w_down"] = _matmul(act, dh2b, "tn", "mm_gw_down", out_dtype=BF16).reshape(4, DFF // 4, D)
    dact = _matmul(dh2b, p["w_down"], "nt", "mm_dact", out_dtype=BF16)
    dup, g["conv_ffn_w"], g["conv_ffn_b"] = _conv_ffn_bwd(up, dact, p["conv_ffn_w"], p["conv_ffn_b"])
    g["w_up"] = _matmul(hn2, dup, "tn", "mm_gw_up", out_dtype=BF16, tn=1408, out_stacked=True)
    dhn2 = _matmul(dup, p["w_up"], "nt", "mm_dhn2", tk=2816, b_stacked=True)
    dh1, dh1b, g["norm_ffn_w"] = _rms_bwd(dhn2, h1, p["norm_ffn_w"], dh2, "rms_ffn_bwd")
    g["w_out"] = _matmul(merged, dh1b, "tn", "mm_gw_out", out_dtype=BF16).reshape(4, D // 4, D)
    dmerged = _matmul(dh1b, p["w_out"], "nt", "mm_dmerged")
    dga, dgb, dya, dval, dgate = _merge_bwd(dmerged, proj, ya, vg)
    dvg = jnp.concatenate([dval, dgate], axis=1)
    g["w_s5_glu"] = _matmul(g5, dvg, "tn", "mm_gw_glu", out_dtype=BF16, out_stacked=True)
    dg5 = _matmul(dvg, p["w_s5_glu"], "nt", "mm_dg5", b_stacked=True)
    tr = lambda b: b.transpose(0, 2, 1)
    gxr, gxi, dus, gcr, gci, g["s5_d"] = _s5_out_bwd(dg5, ypre, tr(cout_r).astype(BF16), tr(cout_in).astype(BF16),
                                                     proj, p["s5_d"], xre, xim)
    are, aim = _s5_scan(gxr, gxi, _scan_table(lbr, -lbi, True), True, "s5_scan_bwd")
    du, gbr, gbi, glr, gli = _s5_in_bwd(are, aim, tr(bin_r).astype(BF16), tr(bin_i).astype(BF16), proj, dus, xre, xim)
    g["s5_c_re"] = _from_out_blocks(gcr).transpose(0, 2, 1)
    g["s5_c_im"] = _from_out_blocks(gci).transpose(0, 2, 1)
    (g["s5_lam_re"], g["s5_lam_im"], g["s5_log_dt"], g["s5_b_re"], g["s5_b_im"]) = s5_vjp(
        (glr.reshape(64, 64), gli.reshape(64, 64), _from_out_blocks(gbr), _from_out_blocks(gbi)))
    g["w_proj_a"] = _matmul(yn, dya, "tn", "mm_gw_proj", out_dtype=BF16).reshape(4, DI // 4, D)
    dyn = _matmul(dya, p["w_proj_a"], "nt", "mm_dyn")
    dyssd, dz, g["norm_a_w"] = _gnorm_bwd(dyn, yssd, proj, p["norm_a_w"])
    e8t = e8.T
    dxs, dbm, dcm, draw, pd, ps = _ssd_bwd(xbc, dtc, dtr, hpc, hpr, dexp, e8, e8t, sprev, dyssd)
    g["dt_bias"] = ps[:, 0, :HPG].reshape(1, NG * HPG)
    g["a_log"] = ps[:, 1, :HPG].reshape(1, NG * HPG)
    g["d_a"] = pd.reshape(NG * HPG, HD).sum(axis=1).reshape(1, NG * HPG)
    ddt = draw[:, :, :HPG].transpose(1, 0, 2).reshape(n, NG * HPG).astype(BF16)
    dxbc_parts, gcw, gcb = [], [], []
    for arr, col0, nm in ((dxs, 0, "conv_a_bwd_x"), (dbm, DI, "conv_a_bwd_b"), (dcm, DI + NG * NS, "conv_a_bwd_c")):
        dpart, gw_, gb_ = _conv_a_bwd(proj, arr, p["conv_a_w"], p["conv_a_b"], col0, nm)
        dxbc_parts.append(dpart)
        gcw.append(gw_)
        gcb.append(gb_)
    g["conv_a_w"] = jnp.concatenate(gcw, axis=1)
    g["conv_a_b"] = jnp.concatenate(gcb, axis=1)
    dorig = jnp.concatenate([dz] + dxbc_parts + [ddt, du, dga, dgb], axis=1)
    dproj = jnp.concatenate([jnp.pad(dorig[:, WSH * k:WSH * (k + 1)], ((0, 0), (0, WPAD - WSH))) for k in range(4)],
                            axis=1)
    g["w_in"] = _matmul(hn1, dproj, "tn", "mm_gw_in", out_dtype=BF16, tn=1792, out_stacked=True)
    dhn1 = _matmul(dproj, p["w_in"], "nt", "mm_dhn1", tk=1792, b_stacked=True)
    gx, _, g["norm_mix_w"] = _rms_bwd(dhn1, x, p["norm_mix_w"], dh1, "rms_mix_bwd")
    return loss_blk, gx, g


BIG = ["w_in", "w_proj_a", "w_s5_glu", "w_out", "w_up", "w_down"]
SMALL = ["norm_mix_w", "conv_a_w", "conv_a_b", "dt_bias", "a_log", "d_a", "norm_a_w", "s5_lam_re", "s5_lam_im",
         "s5_log_dt", "s5_b_re", "s5_b_im", "s5_c_re", "s5_c_im", "s5_d", "norm_ffn_w", "conv_ffn_w", "conv_ffn_b",
         "norm_final_w"]
ORDER = ["norm_mix_w", "w_in", "conv_a_w", "conv_a_b", "dt_bias", "a_log", "d_a", "norm_a_w", "w_proj_a", "s5_lam_re",
         "s5_lam_im", "s5_log_dt", "s5_b_re", "s5_b_im", "s5_c_re", "s5_c_im", "s5_d", "w_s5_glu", "w_out",
         "norm_ffn_w", "w_up", "conv_ffn_w", "conv_ffn_b", "w_down", "norm_final_w"]
IN_SPLIT = [DI, DI + CONVD, DI + CONVD + NG * HPG]
CONV_FULL = {"conv_a_w": (KA, CONVD), "conv_ffn_w": (KF, 2 * DFF)}


def _pack(arrs):
    flat = jnp.concatenate([a.reshape(-1).astype(F32) for a in arrs])
    total = flat.shape[0]
    padded = -(-total // 1024) * 1024
    return jnp.pad(flat, (0, padded - total)).reshape(padded // 128, 128)


def _unpack(block, shapes):
    flat = block.reshape(-1)
    out, at = [], 0
    for sh in shapes:
        size = math.prod(sh)
        out.append(flat[at:at + size].reshape(sh))
        at += size
    return out


def _stack_cols(a):
    return a.transpose(1, 0, 2).reshape(a.shape[1], 4 * a.shape[2])


def _unstack_cols(a):
    return a.reshape(a.shape[0], 4, a.shape[1] // 4).transpose(1, 0, 2)


def kernel(x, norm_mix_w, w_in, conv_a_w, conv_a_b, dt_bias, a_log, d_a, norm_a_w, w_proj_a, s5_lam_re, s5_lam_im, s5_log_dt, s5_b_re, s5_b_im, s5_c_re, s5_c_im, s5_d, w_s5_glu, w_out, norm_ffn_w, w_up, conv_ffn_w, conv_ffn_b, w_down, norm_final_w, loss_target, m_norm_mix_w, m_w_in, m_conv_a_w, m_conv_a_b, m_dt_bias, m_a_log, m_d_a, m_norm_a_w, m_w_proj_a, m_s5_lam_re, m_s5_lam_im, m_s5_log_dt, m_s5_b_re, m_s5_b_im, m_s5_c_re, m_s5_c_im, m_s5_d, m_w_s5_glu, m_w_out, m_norm_ffn_w, m_w_up, m_conv_ffn_w, m_conv_ffn_b, m_w_down, m_norm_final_w, v_norm_mix_w, v_w_in, v_conv_a_w, v_conv_a_b, v_dt_bias, v_a_log, v_d_a, v_norm_a_w, v_w_proj_a, v_s5_lam_re, v_s5_lam_im, v_s5_log_dt, v_s5_b_re, v_s5_b_im, v_s5_c_re, v_s5_c_im, v_s5_d, v_w_s5_glu, v_w_out, v_norm_ffn_w, v_w_up, v_conv_ffn_w, v_conv_ffn_b, v_w_down, v_norm_final_w):
    w = dict(norm_mix_w=norm_mix_w, w_in=w_in, conv_a_w=conv_a_w, conv_a_b=conv_a_b, dt_bias=dt_bias, a_log=a_log, d_a=d_a, norm_a_w=norm_a_w, w_proj_a=w_proj_a, s5_lam_re=s5_lam_re, s5_lam_im=s5_lam_im, s5_log_dt=s5_log_dt, s5_b_re=s5_b_re, s5_b_im=s5_b_im, s5_c_re=s5_c_re, s5_c_im=s5_c_im, s5_d=s5_d, w_s5_glu=w_s5_glu, w_out=w_out, norm_ffn_w=norm_ffn_w, w_up=w_up, conv_ffn_w=conv_ffn_w, conv_ffn_b=conv_ffn_b, w_down=w_down, norm_final_w=norm_final_w)
    m = dict(norm_mix_w=m_norm_mix_w, w_in=m_w_in, conv_a_w=m_conv_a_w, conv_a_b=m_conv_a_b, dt_bias=m_dt_bias, a_log=m_a_log, d_a=m_d_a, norm_a_w=m_norm_a_w, w_proj_a=m_w_proj_a, s5_lam_re=m_s5_lam_re, s5_lam_im=m_s5_lam_im, s5_log_dt=m_s5_log_dt, s5_b_re=m_s5_b_re, s5_b_im=m_s5_b_im, s5_c_re=m_s5_c_re, s5_c_im=m_s5_c_im, s5_d=m_s5_d, w_s5_glu=m_w_s5_glu, w_out=m_w_out, norm_ffn_w=m_norm_ffn_w, w_up=m_w_up, conv_ffn_w=m_conv_ffn_w, conv_ffn_b=m_conv_ffn_b, w_down=m_w_down, norm_final_w=m_norm_final_w)
    v = dict(norm_mix_w=v_norm_mix_w, w_in=v_w_in, conv_a_w=v_conv_a_w, conv_a_b=v_conv_a_b, dt_bias=v_dt_bias, a_log=v_a_log, d_a=v_d_a, norm_a_w=v_norm_a_w, w_proj_a=v_w_proj_a, s5_lam_re=v_s5_lam_re, s5_lam_im=v_s5_lam_im, s5_log_dt=v_s5_log_dt, s5_b_re=v_s5_b_re, s5_b_im=v_s5_b_im, s5_c_re=v_s5_c_re, s5_c_im=v_s5_c_im, s5_d=v_s5_d, w_s5_glu=v_w_s5_glu, w_out=v_w_out, norm_ffn_w=v_norm_ffn_w, w_up=v_w_up, conv_ffn_w=v_conv_ffn_w, conv_ffn_b=v_conv_ffn_b, w_down=v_w_down, norm_final_w=v_norm_final_w)
    xi, yi, ci = _place()
    chip = 2 * xi + yi

    cidx = jnp.reshape(ci, (1,)).astype(jnp.int32)
    sidx = jnp.reshape(chip, (1,)).astype(jnp.int32)

    shards = [w[k][0].astype(BF16) for k in BIG]
    shards[0] = jnp.pad(shards[0], ((0, 0), (0, WPAD - WSH)))
    gathered = _gather_big(shards)
    full = {k: lax.dynamic_update_slice(gathered[t], shards[t][None], (chip, 0, 0)) for t, k in enumerate(BIG)}
    conv_blocks = []
    for k, (taps, cols) in CONV_FULL.items():
        shard = jnp.where(ci == 0, w[k][0], 0.0)
        conv_blocks.append(lax.dynamic_update_slice_in_dim(jnp.zeros((taps, cols), F32), shard, chip * (cols // 4), 1))
    conv_full = _unpack(_allsum_small(_pack(conv_blocks), "sum_conv_w"), [CONV_FULL[k] for k in CONV_FULL])

    win = jnp.concatenate([full["w_in"][k, :, :WSH] for k in range(4)], axis=1)
    p = {
        "w_main": jnp.concatenate([win[:, :IN_SPLIT[1]], win[:, IN_SPLIT[2]:]], axis=1),
        "w_dt": jnp.pad(win[:, IN_SPLIT[1]:IN_SPLIT[2]], ((0, 0), (0, 128 - NG * HPG))),
        "w_in": full["w_in"],
        "w_proj_a": full["w_proj_a"].reshape(DI, D),
        "w_s5_glu": full["w_s5_glu"],
        "w_out": full["w_out"].reshape(D, D),
        "w_up": full["w_up"],
        "w_down": full["w_down"].reshape(DFF, D),
        "conv_a_w": conv_full[0], "conv_ffn_w": conv_full[1],
        "conv_a_b": conv_a_b, "conv_ffn_b": conv_ffn_b,
        "norm_mix_w": norm_mix_w, "norm_a_w": norm_a_w, "norm_ffn_w": norm_ffn_w,
        "norm_final_w": norm_final_w.reshape(1, D),
        "dt_bias": dt_bias, "a_log": a_log, "d_a": d_a, "s5_d": s5_d,
        "s5_lam_re": s5_lam_re[0], "s5_lam_im": s5_lam_im[0], "s5_log_dt": s5_log_dt[0],
        "s5_b_re": s5_b_re[0], "s5_b_im": s5_b_im[0], "s5_c_re": s5_c_re[0], "s5_c_im": s5_c_im[0],
    }
    loss_blk, gx, g = _local_step(x[0], loss_target[0], p)

    g_mine, g_other = _reduce_big([g[k] for k in BIG], cidx, sidx)

    small_shapes = [CONV_FULL.get(k, w[k].shape[1:] if k != "norm_final_w" else w[k].shape) for k in SMALL]
    small = _allsum_small(_pack([g[k] for k in SMALL] + [loss_blk[0:1, 0:1]]), "sum_small_grads")
    small_grads = dict(zip(SMALL + ["loss"], _unpack(small, small_shapes + [(1,)])))
    for k, (taps, cols) in CONV_FULL.items():
        small_grads[k] = lax.dynamic_slice_in_dim(small_grads[k], chip * (cols // 4), cols // 4, axis=1)
    loss = small_grads.pop("loss").reshape(())

    grads, delta, new_m, new_v = {}, {}, {}, {}
    for t, k in enumerate(BIG):
        grads[k], delta[k], new_m[k], new_v[k] = _adamw_halves(w[k], g_mine[t], g_other[t], m[k], v[k], cidx,
                                                               "adamw_" + k)
    for k in SMALL:
        grads[k] = small_grads[k].reshape(w[k].shape)
    pk = lambda t: _pack([t[k] for k in SMALL])
    d_, m_, v_ = _adamw(pk(w), pk(grads), pk(m), pk(v), "adamw_small")
    shapes = [w[k].shape for k in SMALL]
    for k, dd, mm, vv in zip(SMALL, _unpack(d_, shapes), _unpack(m_, shapes), _unpack(v_, shapes)):
        delta[k], new_m[k], new_v[k] = dd, mm, vv
    return (loss, gx[None], *[grads[k] for k in ORDER], *[delta[k] for k in ORDER],
            *[new_m[k] for k in ORDER], *[new_v[k] for k in ORDER])
```

```python
import functools
import math

import jax
import jax.numpy as jnp
from jax import lax
from jax.experimental import pallas as pl
from jax.experimental.pallas import tpu as pltpu

F32 = jnp.float32
BF16 = jnp.bfloat16
HI = lax.Precision.HIGHEST
MESH = pl.DeviceIdType.MESH
ANY = pl.BlockSpec(memory_space=pl.ANY)

D = 2048
DI = 3072
HD = 64
NG = 8
HPG = 6
GW = HPG * HD
NS = 128
KA = 4
Q = 256
CONVD = DI + 2 * NG * NS
DS5 = 1024
NCH = 4096
DFF = 5632
KF = 3
EPS = 1e-6
EIG_MAX = -1e-4
NMAIN = 13312
OFF_XBC, OFF_U, OFF_GA, OFF_GB = 3072, 8192, 9216, 11264
WSH = 3340
WPAD = 3360
VMEM_LIMIT = 56 * 1024 * 1024

LR, B1, B2, AEPS, WD, STEP = 0.001, 0.9, 0.999, 1e-08, 0.01, 10


def _cp(*sem):
    return pltpu.CompilerParams(dimension_semantics=sem, vmem_limit_bytes=VMEM_LIMIT)


def _sig(x):
    return jax.nn.sigmoid(x)


def _silu(x):
    return x * _sig(x)


def _dsilu(x):
    s = _sig(x)
    return s * (1.0 + x * (1.0 - s))


def _softplus(x):
    return jnp.maximum(x, 0.0) + jnp.log(1.0 + jnp.exp(-jnp.abs(x)))


_GC = math.sqrt(2.0 / math.pi)


def _gelu(x):
    return 0.5 * x * (1.0 + jnp.tanh(_GC * (x + 0.044715 * x * x * x)))


def _dgelu(x):
    t = jnp.tanh(_GC * (x + 0.044715 * x * x * x))
    return 0.5 * (1.0 + t) + 0.5 * x * (1.0 - t * t) * _GC * (1.0 + 3.0 * 0.044715 * x * x)


def _dot(a, b, dims=((1,), (0,)), prec=None):
    return lax.dot_general(a, b, (dims, ((), ())), precision=prec, preferred_element_type=F32)


NT = ((1,), (1,))
TN = ((0,), (0,))


def _pick(n, t):
    for unit in (128, 8):
        for cand in range(min(n, t) // unit * unit, 0, -unit):
            if n % cand == 0:
                return cand
    return n


def _matmul(a, b, mode, name, out_dtype=F32, tm=512, tn=1024, tk=2048, residual=None, b_stacked=False,
            out_stacked=False):
    if b_stacked:
        _, brows, bn = b.shape
        bshape = (brows, 4 * bn)
    else:
        bshape = b.shape
    if mode == "nn":
        (m, k), (k2, n) = a.shape, bshape
    elif mode == "nt":
        (m, k), (n, k2) = a.shape, bshape
    else:
        (k, m), (k2, n) = a.shape, bshape
    assert k == k2
    tm = _pick(m, tm)
    tn = _pick(n // 4 if (out_stacked or (b_stacked and mode != "nt")) else n, tn)
    tk = _pick(k // 4 if (b_stacked and mode == "nt") else k, tk)
    nk = k // tk
    dims = {"nn": ((1,), (0,)), "nt": NT, "tn": TN}[mode]
    has_res = residual is not None

    def body(*refs):
        a_ref, b_ref = refs[0], refs[1]
        r_ref = refs[2] if has_res else None
        o_ref = refs[3] if has_res else refs[2]
        p = _dot(a_ref[...], b_ref[...], dims)

        def finish(r):
            if has_res:
                r = r + r_ref[...]
            o_ref[...] = r.astype(out_dtype)

        if nk == 1:
            finish(p)
        else:
            acc = refs[-1]
            kk = pl.program_id(2)

            @pl.when(kk == 0)
            def _():
                acc[...] = p

            @pl.when(kk > 0)
            def _():
                acc[...] += p

            @pl.when(kk == nk - 1)
            def _():
                finish(acc[...])

    if mode == "tn":
        a_spec = pl.BlockSpec((tk, tm), lambda i, j, kk: (kk, i))
    else:
        a_spec = pl.BlockSpec((tm, tk), lambda i, j, kk: (i, kk))
    if mode == "nt":
        if b_stacked:
            per = bn // tk
            b_spec = pl.BlockSpec((None, tn, tk), lambda i, j, kk: (kk // per, j, kk % per))
        else:
            b_spec = pl.BlockSpec((tn, tk), lambda i, j, kk: (j, kk))
    elif b_stacked:
        per = bn // tn
        b_spec = pl.BlockSpec((None, tk, tn), lambda i, j, kk: (j // per, kk, j % per))
    else:
        b_spec = pl.BlockSpec((tk, tn), lambda i, j, kk: (kk, j))
    o_spec = pl.BlockSpec((tm, tn), lambda i, j, kk: (i, j))
    if out_stacked:
        per_o = n // 4 // tn
        out_spec = pl.BlockSpec((None, tm, tn), lambda i, j, kk: (j // per_o, i, j % per_o))
        out_shape = jax.ShapeDtypeStruct((4, m, n // 4), out_dtype)
    else:
        out_spec, out_shape = o_spec, jax.ShapeDtypeStruct((m, n), out_dtype)
    in_specs, args = [a_spec, b_spec], [a, b]
    if has_res:
        in_specs.append(o_spec)
        args.append(residual)
    return pl.pallas_call(
        body, name=name, grid=(m // tm, n // tn, nk),
        in_specs=in_specs, out_specs=out_spec, out_shape=out_shape,
        scratch_shapes=[pltpu.VMEM((tm, tn), F32)] if nk > 1 else [],
        compiler_params=_cp("parallel", "parallel", "arbitrary"),
    )(*args)


TL = 256


def _rms_fwd(x, w, name):
    n, d = x.shape

    def body(x_ref, w_ref, o_ref):
        xv = x_ref[...]
        r = lax.rsqrt(jnp.mean(xv * xv, axis=-1, keepdims=True) + EPS)
        o_ref[...] = (xv * r * w_ref[...]).astype(BF16)

    return pl.pallas_call(
        body, name=name, grid=(n // TL,),
        in_specs=[pl.BlockSpec((TL, d), lambda i: (i, 0)), pl.BlockSpec((1, d), lambda i: (0, 0))],
        out_specs=pl.BlockSpec((TL, d), lambda i: (i, 0)),
        out_shape=jax.ShapeDtypeStruct((n, d), BF16), compiler_params=_cp("parallel"),
    )(x, w)


def _rms_bwd(dhn, x, w, dres, name):
    n, d = x.shape

    def body(g_ref, x_ref, w_ref, r_ref, dx_ref, dxb_ref, gw_ref):
        xv = x_ref[...]
        r = lax.rsqrt(jnp.mean(xv * xv, axis=-1, keepdims=True) + EPS)
        xh = xv * r
        gv = g_ref[...]
        g = gv * w_ref[...]
        dx = r_ref[...] + r * (g - xh * jnp.mean(g * xh, axis=-1, keepdims=True))
        dx_ref[...] = dx
        dxb_ref[...] = dx.astype(BF16)

        @pl.when(pl.program_id(0) == 0)
        def _():
            gw_ref[...] = jnp.zeros_like(gw_ref)

        gw_ref[...] += jnp.sum(gv * xh, axis=0, keepdims=True)

    row = pl.BlockSpec((TL, d), lambda i: (i, 0))
    vec = pl.BlockSpec((1, d), lambda i: (0, 0))
    return pl.pallas_call(
        body, name=name, grid=(n // TL,),
        in_specs=[row, row, vec, row], out_specs=[row, row, vec],
        out_shape=[jax.ShapeDtypeStruct((n, d), F32), jax.ShapeDtypeStruct((n, d), BF16),
                   jax.ShapeDtypeStruct((1, d), F32)],
        compiler_params=_cp("arbitrary"),
    )(dhn, x, w, dres)


def _final(h2, w, target):
    n, d = h2.shape

    def body(x_ref, w_ref, t_ref, dx_ref, dxb_ref, gw_ref, loss_ref):
        xv = x_ref[...]
        r = lax.rsqrt(jnp.mean(xv * xv, axis=-1, keepdims=True) + EPS)
        xh = xv * r
        diff = xh * w_ref[...] - t_ref[...]
        gv = diff * (1.0 / d)
        g = gv * w_ref[...]
        dx = r * (g - xh * jnp.mean(g * xh, axis=-1, keepdims=True))
        dx_ref[...] = dx
        dxb_ref[...] = dx.astype(BF16)

        @pl.when(pl.program_id(0) == 0)
        def _():
            gw_ref[...] = jnp.zeros_like(gw_ref)
            loss_ref[...] = jnp.zeros_like(loss_ref)

        gw_ref[...] += jnp.sum(gv * xh, axis=0, keepdims=True)
        part = 0.5 * jnp.sum(jnp.mean(diff * diff, axis=-1, keepdims=True), axis=0, keepdims=True)
        loss_ref[...] += jnp.broadcast_to(part, loss_ref.shape)

    row = pl.BlockSpec((TL, d), lambda i: (i, 0))
    vec = pl.BlockSpec((1, d), lambda i: (0, 0))
    return pl.pallas_call(
        body, name="final_loss", grid=(n // TL,),
        in_specs=[row, vec, row], out_specs=[row, row, vec, pl.BlockSpec((8, 128), lambda i: (0, 0))],
        out_shape=[jax.ShapeDtypeStruct((n, d), F32), jax.ShapeDtypeStruct((n, d), BF16),
                   jax.ShapeDtypeStruct((1, d), F32), jax.ShapeDtypeStruct((8, 128), F32)],
        compiler_params=_cp("arbitrary"),
    )(h2, w, target)


CT = 512
CL = 512


def _lagged(xf, taps, rows):
    return [xf[8:8 + rows]] + [pltpu.roll(xf, s, 0)[8:8 + rows] for s in range(1, taps)]


def _shift_up(x, u, n):
    if u == 0:
        return x[0:n]
    return pltpu.roll(x, x.shape[0] - u, 0)[0:n]


def _conv_pre(lagged, w_ref, b_ref, taps):
    pre = b_ref[...]
    for k in range(taps):
        pre = pre + w_ref[k:k + 1, :] * lagged[taps - 1 - k]
    return pre


def _conv_back(e, w_ref, taps):
    dx = w_ref[taps - 1:taps, :] * e[0:CL]
    for k in range(taps - 1):
        dx = dx + w_ref[k:k + 1, :] * _shift_up(e, taps - 1 - k, CL)
    return dx


def _halo_specs(n, col_of):
    per = CL // 8
    cur = pl.BlockSpec((CL, CT), lambda j, i, *_: (i, col_of(j)))
    prev = pl.BlockSpec((8, CT), lambda j, i, *_: (jnp.maximum(i * per - 1, 0), col_of(j)))
    nxt = pl.BlockSpec((8, CT), lambda j, i, *_: (jnp.minimum((i + 1) * per, n // 8 - 1), col_of(j)))
    return prev, cur, nxt


def _conv_a_fwd(proj, w, b):
    n = proj.shape[0]
    off = OFF_XBC // CT

    def body(p_ref, x_ref, w_ref, b_ref, o_ref):
        p8 = jnp.where(pl.program_id(1) > 0, p_ref[...], 0.0)
        xf = jnp.concatenate([p8, x_ref[...]], axis=0)
        o_ref[...] = _silu(_conv_pre(_lagged(xf, KA, CL), w_ref, b_ref, KA))

    prev, cur, _ = _halo_specs(n, lambda j: j + off)
    return pl.pallas_call(
        body, name="conv_a_fwd", grid=(CONVD // CT, n // CL),
        in_specs=[prev, cur, pl.BlockSpec((KA, CT), lambda j, i: (0, j)), pl.BlockSpec((1, CT), lambda j, i: (0, j))],
        out_specs=pl.BlockSpec((CL, CT), lambda j, i: (i, j)),
        out_shape=jax.ShapeDtypeStruct((n, CONVD), F32), compiler_params=_cp("parallel", "parallel"),
    )(proj, proj, w, b)


def _conv_a_bwd(proj, dout, w, b, col0, name):
    n, width = dout.shape
    off = (OFF_XBC + col0) // CT
    woff = col0 // CT
    nl = n // CL

    def body(p_ref, x_ref, n_ref, d_ref, dn_ref, w_ref, b_ref, dx_ref, dw_ref, db_ref):
        i = pl.program_id(1)
        xf = jnp.concatenate([jnp.where(i > 0, p_ref[...], 0.0), x_ref[...], n_ref[...]], axis=0)
        lag = _lagged(xf, KA, CL + 8)
        de = jnp.concatenate([d_ref[...], jnp.where(i < nl - 1, dn_ref[...], 0.0)], axis=0)
        se = de * _dsilu(_conv_pre(lag, w_ref, b_ref, KA))
        dx_ref[...] = _conv_back(se, w_ref, KA).astype(BF16)

        @pl.when(i == 0)
        def _():
            dw_ref[...] = jnp.zeros_like(dw_ref)
            db_ref[...] = jnp.zeros_like(db_ref)

        sc = se[0:CL]
        for k in range(KA):
            dw_ref[k:k + 1, :] += jnp.sum(sc * lag[KA - 1 - k][0:CL], axis=0, keepdims=True)
        db_ref[...] += jnp.sum(sc, axis=0, keepdims=True)

    prev, cur, nxt = _halo_specs(n, lambda j: j + off)
    _, dcur, dnxt = _halo_specs(n, lambda j: j)
    wspec = pl.BlockSpec((KA, CT), lambda j, i: (0, j + woff))
    bspec = pl.BlockSpec((1, CT), lambda j, i: (0, j + woff))
    return pl.pallas_call(
        body, name=name, grid=(width // CT, nl),
        in_specs=[prev, cur, nxt, dcur, dnxt, wspec, bspec],
        out_specs=[pl.BlockSpec((CL, CT), lambda j, i: (i, j)), pl.BlockSpec((KA, CT), lambda j, i: (0, j)),
                   pl.BlockSpec((1, CT), lambda j, i: (0, j))],
        out_shape=[jax.ShapeDtypeStruct((n, width), BF16), jax.ShapeDtypeStruct((KA, width), F32),
                   jax.ShapeDtypeStruct((1, width), F32)],
        compiler_params=_cp("parallel", "arbitrary"),
    )(proj, proj, proj, dout, dout, w, b)


def _conv_ffn_fwd(up, w, b):
    n = up.shape[0]
    nb = DFF // CT

    def body(pg_ref, g_ref, pv_ref, v_ref, wg_ref, bg_ref, wv_ref, bv_ref, o_ref):
        inner = pl.program_id(1) > 0
        gf = jnp.concatenate([jnp.where(inner, pg_ref[...], 0.0), g_ref[...]], axis=0)
        vf = jnp.concatenate([jnp.where(inner, pv_ref[...], 0.0), v_ref[...]], axis=0)
        gc = _conv_pre(_lagged(gf, KF, CL), wg_ref, bg_ref, KF)
        vc = _conv_pre(_lagged(vf, KF, CL), wv_ref, bv_ref, KF)
        o_ref[...] = (_silu(gc) * vc).astype(BF16)

    gp, gcur, _ = _halo_specs(n, lambda j: j)
    vp, vcur, _ = _halo_specs(n, lambda j: j + nb)
    return pl.pallas_call(
        body, name="conv_ffn_fwd", grid=(nb, n // CL),
        in_specs=[gp, gcur, vp, vcur,
                  pl.BlockSpec((KF, CT), lambda j, i: (0, j)), pl.BlockSpec((1, CT), lambda j, i: (0, j)),
                  pl.BlockSpec((KF, CT), lambda j, i: (0, j + nb)), pl.BlockSpec((1, CT), lambda j, i: (0, j + nb))],
        out_specs=pl.BlockSpec((CL, CT), lambda j, i: (i, j)),
        out_shape=jax.ShapeDtypeStruct((n, DFF), BF16), compiler_params=_cp("parallel", "parallel"),
    )(up, up, up, up, w, b, w, b)


def _conv_ffn_bwd(up, dact, w, b):
    n = up.shape[0]
    nb = DFF // CT
    nl = n // CL

    def body(pg_ref, g_ref, ng_ref, pv_ref, v_ref, nv_ref, d_ref, dn_ref, wg_ref, bg_ref, wv_ref, bv_ref,
             dxg_ref, dxv_ref, dwg_ref, dwv_ref, dbg_ref, dbv_ref):
        i = pl.program_id(1)
        gf = jnp.concatenate([jnp.where(i > 0, pg_ref[...], 0.0), g_ref[...], ng_ref[...]], axis=0)
        vf = jnp.concatenate([jnp.where(i > 0, pv_ref[...], 0.0), v_ref[...], nv_ref[...]], axis=0)
        glag, vlag = _lagged(gf, KF, CL + 8), _lagged(vf, KF, CL + 8)
        de = jnp.concatenate([d_ref[...], jnp.where(i < nl - 1, dn_ref[...], 0.0)], axis=0).astype(F32)
        gc = _conv_pre(glag, wg_ref, bg_ref, KF)
        vc = _conv_pre(vlag, wv_ref, bv_ref, KF)
        sg = _sig(gc)
        dgc = de * vc * (sg * (1.0 + gc * (1.0 - sg)))
        dvc = de * (gc * sg)
        dxg_ref[...] = _conv_back(dgc, wg_ref, KF).astype(BF16)
        dxv_ref[...] = _conv_back(dvc, wv_ref, KF).astype(BF16)

        @pl.when(i == 0)
        def _():
            for r in (dwg_ref, dwv_ref, dbg_ref, dbv_ref):
                r[...] = jnp.zeros_like(r)

        for e, lag, dw_ref, db_ref in ((dgc, glag, dwg_ref, dbg_ref), (dvc, vlag, dwv_ref, dbv_ref)):
            ec = e[0:CL]
            for k in range(KF):
                dw_ref[k:k + 1, :] += jnp.sum(ec * lag[KF - 1 - k][0:CL], axis=0, keepdims=True)
            db_ref[...] += jnp.sum(ec, axis=0, keepdims=True)

    gp, gcur, gnx = _halo_specs(n, lambda j: j)
    vp, vcur, vnx = _halo_specs(n, lambda j: j + nb)
    wcol = lambda o: (pl.BlockSpec((KF, CT), lambda j, i: (0, j + o)), pl.BlockSpec((1, CT), lambda j, i: (0, j + o)))
    wg, bg = wcol(0)
    wv, bv = wcol(nb)
    dxs = pl.BlockSpec((CL, CT), lambda j, i: (i, j))
    outs = pl.pallas_call(
        body, name="conv_ffn_bwd", grid=(nb, nl),
        in_specs=[gp, gcur, gnx, vp, vcur, vnx, gcur, gnx, wg, bg, wv, bv],
        out_specs=[dxs, dxs, wg, wg, bg, bg],
        out_shape=[jax.ShapeDtypeStruct((n, DFF), BF16)] * 2 + [jax.ShapeDtypeStruct((KF, DFF), F32)] * 2
        + [jax.ShapeDtypeStruct((1, DFF), F32)] * 2,
        compiler_params=_cp("parallel", "arbitrary"),
    )(up, up, up, up, up, up, dact, dact, w, b, w, b)
    return [jnp.concatenate(outs[k:k + 2], axis=1) for k in (0, 2, 4)]


def _ssd_common(dtc_ref, dtr_ref, hpc_ref, hpr_ref, e8_ref):
    row = lax.broadcasted_iota(jnp.int32, (Q, Q), 0)
    col = lax.broadcasted_iota(jnp.int32, (Q, Q), 1)
    lower = row >= col
    upper = row <= col
    hpc, hpr = hpc_ref[...], hpr_ref[...]
    pre_c = dtc_ref[...] + hpc[0:1, :]
    dt_c = _softplus(pre_c)
    a_c = -jnp.exp(hpc[1:2, :])
    s_c = _dot(lower.astype(F32), dt_c * a_c, prec=HI)
    dt_r = _softplus(dtr_ref[...] + hpr[:, 0:1])
    s_r = _dot(dt_r * (-jnp.exp(hpr[:, 1:2])), upper.astype(F32), prec=HI)
    e8 = e8_ref[...]
    dt_e = _dot(dt_c, e8, prec=HI)
    s_e = _dot(s_c, e8, prec=HI)
    return lower, upper, pre_c, dt_c, a_c, s_c, s_r, dt_e, s_e


def _ssd_specs(nc, rev):
    cc = (lambda c: nc - 1 - c) if rev else (lambda c: c)
    return [
        pl.BlockSpec((Q, GW), lambda g, c: (cc(c), g)),
        pl.BlockSpec((Q, NS), lambda g, c: (cc(c), DI // NS + g)),
        pl.BlockSpec((Q, NS), lambda g, c: (cc(c), (DI + NG * NS) // NS + g)),
        pl.BlockSpec((None, Q, 8), lambda g, c: (g, cc(c), 0)),
        pl.BlockSpec((None, 8, Q), lambda g, c: (g, 0, cc(c))),
        pl.BlockSpec((None, 8, 8), lambda g, c: (g, 0, 0)),
        pl.BlockSpec((None, 8, 128), lambda g, c: (g, 0, 0)),
        pl.BlockSpec((1, GW), lambda g, c: (0, g)),
        pl.BlockSpec((8, GW), lambda g, c: (0, 0)),
    ]


def _ssd_fwd(xbc, dtc, dtr, hpc, hpr, dexp, e8):
    n = xbc.shape[0]
    nc = n // Q

    def body(xs_ref, b_ref, c_ref, dtc_ref, dtr_ref, hpc_ref, hpr_ref, dexp_ref, e8_ref, y_ref, sp_ref, st):
        @pl.when(pl.program_id(1) == 0)
        def _():
            st[...] = jnp.zeros_like(st)

        lower, _, _, _, _, s_c, s_r, dt_e, s_e = _ssd_common(dtc_ref, dtr_ref, hpc_ref, hpr_ref, e8_ref)
        xs = xs_ref[...]
        x = xs * dt_e
        xb = x.astype(BF16)
        bb, cb = b_ref[...].astype(BF16), c_ref[...].astype(BF16)
        cbm = _dot(cb, bb, NT)
        st_e = s_e[Q - 1:Q, :]
        sprev = st[...]
        sp_ref[...] = sprev
        yoff = _dot(cb, sprev.astype(BF16)) * jnp.exp(s_e) + dexp_ref[...] * xs
        for h in range(HPG):
            sl = slice(h * HD, (h + 1) * HD)
            lm = jnp.where(lower, jnp.exp(jnp.minimum(s_c[:, h:h + 1] - s_r[h:h + 1, :], 0.0)), 0.0)
            y_ref[:, sl] = _dot((cbm * lm).astype(BF16), xb[:, sl]) + yoff[:, sl]
        w = (x * jnp.exp(st_e - s_e)).astype(BF16)
        st[...] = jnp.exp(st_e) * sprev + _dot(bb, w, TN)

    return pl.pallas_call(
        body, name="ssd_fwd", grid=(NG, nc), in_specs=_ssd_specs(nc, False),
        out_specs=[pl.BlockSpec((Q, GW), lambda g, c: (c, g)),
                   pl.BlockSpec((None, None, NS, GW), lambda g, c: (c, g, 0, 0))],
        out_shape=[jax.ShapeDtypeStruct((n, DI), F32), jax.ShapeDtypeStruct((nc, NG, NS, GW), F32)],
        scratch_shapes=[pltpu.VMEM((NS, GW), F32)],
        compiler_params=_cp("parallel", "arbitrary"),
    )(xbc, xbc, xbc, dtc, dtr, hpc, hpr, dexp, e8)


def _ssd_bwd(xbc, dtc, dtr, hpc, hpr, dexp, e8, e8t, sprev_all, dy):
    n = xbc.shape[0]
    nc = n // Q
    rc = lambda c: nc - 1 - c

    def body(xs_ref, b_ref, c_ref, dtc_ref, dtr_ref, hpc_ref, hpr_ref, dexp_ref, e8_ref, e8t_ref, sp_ref, dy_ref,
             dxs_ref, db_ref, dc_ref, draw_ref, pd_ref, ps_ref, dst, dxbuf):
        @pl.when(pl.program_id(1) == 0)
        def _():
            dst[...] = jnp.zeros_like(dst)
            pd_ref[...] = jnp.zeros_like(pd_ref)
            ps_ref[...] = jnp.zeros_like(ps_ref)

        lower, upper, pre_c, dt_c, a_c, s_c, s_r, dt_e, s_e = _ssd_common(dtc_ref, dtr_ref, hpc_ref, hpr_ref, e8_ref)
        e8t = e8t_ref[...]
        xs = xs_ref[...]
        x = xs * dt_e
        xb = x.astype(BF16)
        bb, cb = b_ref[...].astype(BF16), c_ref[...].astype(BF16)
        cbm = _dot(cb, bb, NT)
        cbt = _dot(bb, cb, NT)
        st_e = s_e[Q - 1:Q, :]
        dec_out, dec_st, e_t = jnp.exp(s_e), jnp.exp(st_e - s_e), jnp.exp(st_e)
        dyv = dy_ref[...]
        dyb = dyv.astype(BF16)
        sprev = sp_ref[...]
        sb = sprev.astype(BF16)
        ds_in = dst[...]
        dsb = ds_in.astype(BF16)

        cs = _dot(cb, sb)
        dcs = (dyv * dec_out).astype(BF16)
        d_c = _dot(dcs, sb, NT)
        wf = x * dec_st
        d_w = _dot(bb, dsb)
        d_b = _dot(wf.astype(BF16), dsb, NT)
        tw = d_w * wf
        ds_c = _dot(dyv * cs * dec_out - tw, e8t, prec=HI)
        dcb = jnp.zeros((Q, Q), F32)
        dcbt = jnp.zeros((Q, Q), F32)
        lane8 = lax.broadcasted_iota(jnp.int32, (1, 8), 1)
        for h in range(HPG):
            sl = slice(h * HD, (h + 1) * HD)
            sc_h, sr_h = s_c[:, h:h + 1], s_r[h:h + 1, :]
            lm = jnp.where(lower, jnp.exp(jnp.minimum(sc_h - sr_h, 0.0)), 0.0)
            lmt = jnp.where(upper, jnp.exp(jnp.minimum(sr_h - sc_h, 0.0)), 0.0)
            mt = cbt * lmt
            dm = _dot(dyb[:, sl], xb[:, sl], NT)
            dmt = _dot(xb[:, sl], dyb[:, sl], NT)
            dxbuf[:, sl] = _dot(mt.astype(BF16), dyb[:, sl])
            dml = dm * lm
            dmlt = dmt * lmt
            dcb = dcb + dml
            dcbt = dcbt + dmlt
            dsh = jnp.sum(dml * cbm, axis=1, keepdims=True) - jnp.sum(dmlt * cbt, axis=1, keepdims=True)
            ds_c = ds_c + dsh * (lane8 == h).astype(F32)
        d_c = d_c + _dot(dcb.astype(BF16), bb)
        d_b = d_b + _dot(dcbt.astype(BF16), cb)
        dx = d_w * dec_st + dxbuf[...]
        tsum = jnp.sum(tw, axis=0, keepdims=True) + jnp.sum(ds_in * sprev, axis=0, keepdims=True) * e_t
        ds_t = _dot(jnp.broadcast_to(tsum, (8, GW)), e8t, prec=HI)[0:1, :]
        rows = lax.broadcasted_iota(jnp.int32, (Q, 8), 0)
        ds_c = ds_c + jnp.where(rows == Q - 1, ds_t, 0.0)
        d_a = _dot(upper.astype(F32), ds_c, prec=HI)
        ddt = _dot(dx * xs, e8t, prec=HI) + d_a * a_c
        draw = ddt * _sig(pre_c)
        draw_ref[...] = draw
        ps_ref[0:1, :] += jnp.sum(draw, axis=0, keepdims=True)
        ps_ref[1:2, :] += jnp.sum(d_a * dt_c, axis=0, keepdims=True) * a_c
        pd_ref[...] += jnp.sum(dyv * xs, axis=0, keepdims=True)
        dxs_ref[...] = dx * dt_e + dyv * dexp_ref[...]
        db_ref[...] = d_b
        dc_ref[...] = d_c
        dst[...] = e_t * ds_in + _dot(cb, dcs, TN)

    in_specs = _ssd_specs(nc, True) + [
        pl.BlockSpec((GW, 8), lambda g, c: (0, 0)),
        pl.BlockSpec((None, None, NS, GW), lambda g, c: (rc(c), g, 0, 0)),
        pl.BlockSpec((Q, GW), lambda g, c: (rc(c), g)),
    ]
    return pl.pallas_call(
        body, name="ssd_bwd", grid=(NG, nc), in_specs=in_specs,
        out_specs=[pl.BlockSpec((Q, GW), lambda g, c: (rc(c), g)),
                   pl.BlockSpec((Q, NS), lambda g, c: (rc(c), g)),
                   pl.BlockSpec((Q, NS), lambda g, c: (rc(c), g)),
                   pl.BlockSpec((None, Q, 8), lambda g, c: (g, rc(c), 0)),
                   pl.BlockSpec((None, 1, GW), lambda g, c: (g, 0, 0)),
                   pl.BlockSpec((None, 8, 8), lambda g, c: (g, 0, 0))],
        out_shape=[jax.ShapeDtypeStruct((n, DI), F32), jax.ShapeDtypeStruct((n, NG * NS), F32),
                   jax.ShapeDtypeStruct((n, NG * NS), F32), jax.ShapeDtypeStruct((NG, n, 8), F32),
                   jax.ShapeDtypeStruct((NG, 1, GW), F32), jax.ShapeDtypeStruct((NG, 8, 8), F32)],
        scratch_shapes=[pltpu.VMEM((NS, GW), F32), pltpu.VMEM((Q, GW), F32)],
        compiler_params=_cp("parallel", "arbitrary"),
    )(xbc, xbc, xbc, dtc, dtr, hpc, hpr, dexp, e8, e8t, sprev_all, dy)


GL = 128


def _gnorm_fwd(y, proj, w):
    n = y.shape[0]

    def body(y_ref, z_ref, w_ref, o_ref):
        for g in range(NG):
            sl = slice(g * GW, (g + 1) * GW)
            yz = y_ref[:, sl] * _silu(z_ref[:, sl])
            r = lax.rsqrt(jnp.mean(yz * yz, axis=-1, keepdims=True) + EPS)
            o_ref[:, sl] = (yz * r * w_ref[:, sl]).astype(BF16)

    row = pl.BlockSpec((GL, DI), lambda i: (i, 0))
    return pl.pallas_call(
        body, name="gnorm_fwd", grid=(n // GL,),
        in_specs=[row, row, pl.BlockSpec((1, DI), lambda i: (0, 0))], out_specs=row,
        out_shape=jax.ShapeDtypeStruct((n, DI), BF16), compiler_params=_cp("parallel"),
    )(y, proj, w)


def _gnorm_bwd(dyn, y, proj, w):
    n = y.shape[0]

    def body(d_ref, y_ref, z_ref, w_ref, dy_ref, dz_ref, gw_ref):
        @pl.when(pl.program_id(0) == 0)
        def _():
            gw_ref[...] = jnp.zeros_like(gw_ref)

        for g in range(NG):
            sl = slice(g * GW, (g + 1) * GW)
            yv, zv, dv = y_ref[:, sl], z_ref[:, sl], d_ref[:, sl]
            sz = _silu(zv)
            yz = yv * sz
            r = lax.rsqrt(jnp.mean(yz * yz, axis=-1, keepdims=True) + EPS)
            yh = yz * r
            gg = dv * w_ref[:, sl]
            dyz = r * (gg - yh * jnp.mean(gg * yh, axis=-1, keepdims=True))
            gw_ref[:, sl] += jnp.sum(dv * yh, axis=0, keepdims=True)
            dy_ref[:, sl] = dyz * sz
            dz_ref[:, sl] = (dyz * yv * _dsilu(zv)).astype(BF16)

    row = pl.BlockSpec((GL, DI), lambda i: (i, 0))
    vec = pl.BlockSpec((1, DI), lambda i: (0, 0))
    return pl.pallas_call(
        body, name="gnorm_bwd", grid=(n // GL,),
        in_specs=[row, row, row, vec], out_specs=[row, row, vec],
        out_shape=[jax.ShapeDtypeStruct((n, DI), F32), jax.ShapeDtypeStruct((n, DI), BF16),
                   jax.ShapeDtypeStruct((1, DI), F32)],
        compiler_params=_cp("arbitrary"),
    )(dyn, y, proj, w)


SL = 512
SB = 8
SCB = NCH // SB


def _s5_in(proj, bre, bim):
    n = proj.shape[0]
    uoff = OFF_U // 128

    def body(u_ref, br_ref, bi_ref, or_ref, oi_ref):
        u = u_ref[...].astype(BF16)
        or_ref[...] = _dot(u, br_ref[...])
        oi_ref[...] = _dot(u, bi_ref[...])

    blk = pl.BlockSpec((None, 128, SCB), lambda i, j: (j, 0, 0))
    out = pl.BlockSpec((SL, SCB), lambda i, j: (i, j))
    return pl.pallas_call(
        body, name="s5_in", grid=(n // SL, SB),
        in_specs=[pl.BlockSpec((SL, 128), lambda i, j: (i, uoff + j)), blk, blk], out_specs=[out, out],
        out_shape=[jax.ShapeDtypeStruct((n, NCH), F32)] * 2, compiler_params=_cp("parallel", "parallel"),
    )(proj, bre, bim)


SC = 256


def _s5_scan(vre, vim, tab, reverse, name):
    n = vre.shape[0]
    nl = n // SL
    ng = SL // 8
    ti = (lambda i: nl - 1 - i) if reverse else (lambda i: i)

    def body(re_ref, im_ref, tab_ref, ore_ref, oim_ref, cre, cim):
        @pl.when(pl.program_id(1) == 0)
        def _():
            cre[...] = jnp.zeros_like(cre)
            cim[...] = jnp.zeros_like(cim)

        def step(j, carry):
            cr, ci = carry
            jj = (ng - 1 - j) if reverse else j
            rows = pl.ds(pl.multiple_of(jj * 8, 8), 8)
            vr, vi = re_ref[rows, :], im_ref[rows, :]
            for t, k in enumerate((1, 2, 4)):
                sh = (8 - k) if reverse else k
                rr, ri = pltpu.roll(vr, sh, 0), pltpu.roll(vi, sh, 0)
                pr, pi = tab_ref[2 * t], tab_ref[2 * t + 1]
                vr, vi = vr + pr * rr - pi * ri, vi + pr * ri + pi * rr
            lr, li = tab_ref[6], tab_ref[7]
            vr, vi = vr + lr * cr - li * ci, vi + lr * ci + li * cr
            ore_ref[rows, :] = vr
            oim_ref[rows, :] = vi
            e = 0 if reverse else 7
            return (jnp.broadcast_to(vr[e:e + 1, :], (8, SC)), jnp.broadcast_to(vi[e:e + 1, :], (8, SC)))

        cr, ci = lax.fori_loop(0, ng, step, (cre[...], cim[...]))
        cre[...] = cr
        cim[...] = ci

    blk = pl.BlockSpec((SL, SC), lambda j, i: (ti(i), j))
    return pl.pallas_call(
        body, name=name, grid=(NCH // SC, nl),
        in_specs=[blk, blk, pl.BlockSpec((8, 8, SC), lambda j, i: (0, 0, j))], out_specs=[blk, blk],
        out_shape=[jax.ShapeDtypeStruct((n, NCH), F32)] * 2,
        scratch_shapes=[pltpu.VMEM((8, SC), F32), pltpu.VMEM((8, SC), F32)],
        compiler_params=_cp("parallel", "arbitrary"),
    )(vre, vim, tab)


def _s5_out(xre, xim, cre, cimn, proj, dvec):
    n = xre.shape[0]
    uoff = OFF_U // 128

    def body(xr_ref, xi_ref, cr_ref, ci_ref, u_ref, d_ref, y_ref, g_ref):
        y = (_dot(xr_ref[...].astype(BF16), cr_ref[...]) + _dot(xi_ref[...].astype(BF16), ci_ref[...])
             + d_ref[...] * u_ref[...])
        y_ref[...] = y
        g_ref[...] = _gelu(y).astype(BF16)

    xs = pl.BlockSpec((SL, SCB), lambda i, j: (i, j))
    blk = pl.BlockSpec((None, SCB, 128), lambda i, j: (j, 0, 0))
    out = pl.BlockSpec((SL, 128), lambda i, j: (i, j))
    return pl.pallas_call(
        body, name="s5_out", grid=(n // SL, SB),
        in_specs=[xs, xs, blk, blk, pl.BlockSpec((SL, 128), lambda i, j: (i, uoff + j)),
                  pl.BlockSpec((1, 128), lambda i, j: (0, j))],
        out_specs=[out, out],
        out_shape=[jax.ShapeDtypeStruct((n, DS5), F32), jax.ShapeDtypeStruct((n, DS5), BF16)],
        compiler_params=_cp("parallel", "parallel"),
    )(xre, xim, cre, cimn, proj, dvec)


def _s5_out_bwd(dg, ypre, crt, cimnt, proj, dvec, xre, xim):
    n = dg.shape[0]
    uoff = OFF_U // 128
    nl = n // SL

    def body(dg_ref, y_ref, cr_ref, ci_ref, u_ref, d_ref, xr_ref, xi_ref,
             gr_ref, gi_ref, dus_ref, gcr_ref, gci_ref, gd_ref):
        dy = dg_ref[...] * _dgelu(y_ref[...])
        dyb = dy.astype(BF16)
        gr_ref[...] = _dot(dyb, cr_ref[...])
        gi_ref[...] = _dot(dyb, ci_ref[...])
        dus_ref[...] = dy * d_ref[...]

        @pl.when(pl.program_id(1) == 0)
        def _():
            gcr_ref[...] = jnp.zeros_like(gcr_ref)
            gci_ref[...] = jnp.zeros_like(gci_ref)
            gd_ref[...] = jnp.zeros_like(gd_ref)

        gcr_ref[...] += _dot(xr_ref[...].astype(BF16), dyb, TN)
        gci_ref[...] -= _dot(xi_ref[...].astype(BF16), dyb, TN)
        gd_ref[...] += jnp.sum(dy * u_ref[...], axis=0, keepdims=True)

    u128 = pl.BlockSpec((SL, 128), lambda j, i: (i, j))
    xs = pl.BlockSpec((SL, SCB), lambda j, i: (i, j))
    blk = pl.BlockSpec((None, 128, SCB), lambda j, i: (j, 0, 0))
    gblk = pl.BlockSpec((None, SCB, 128), lambda j, i: (j, 0, 0))
    vec = pl.BlockSpec((1, 128), lambda j, i: (0, j))
    return pl.pallas_call(
        body, name="s5_out_bwd", grid=(SB, nl),
        in_specs=[u128, u128, blk, blk, pl.BlockSpec((SL, 128), lambda j, i: (i, uoff + j)), vec, xs, xs],
        out_specs=[xs, xs, u128, gblk, gblk, vec],
        out_shape=[jax.ShapeDtypeStruct((n, NCH), F32)] * 2 + [jax.ShapeDtypeStruct((n, DS5), F32)]
        + [jax.ShapeDtypeStruct((SB, SCB, 128), F32)] * 2 + [jax.ShapeDtypeStruct((1, DS5), F32)],
        compiler_params=_cp("parallel", "arbitrary"),
    )(dg, ypre, crt, cimnt, proj, dvec, xre, xim)


def _s5_in_bwd(are, aim, brt, bit, proj, dus, xre, xim):
    n = are.shape[0]
    uoff = OFF_U // 128
    per = SL // 8

    def body(ar_ref, ai_ref, br_ref, bi_ref, u_ref, dus_ref, xr_ref, xi_ref, pr_ref, pi_ref,
             du_ref, gbr_ref, gbi_ref, glr_ref, gli_ref):
        i = pl.program_id(1)
        ar, ai = ar_ref[...], ai_ref[...]
        arb, aib = ar.astype(BF16), ai.astype(BF16)
        du_ref[...] = (_dot(arb, br_ref[...]) + _dot(aib, bi_ref[...]) + dus_ref[...]).astype(BF16)

        @pl.when(i == 0)
        def _():
            for r in (gbr_ref, gbi_ref, glr_ref, gli_ref):
                r[...] = jnp.zeros_like(r)

        ub = u_ref[...].astype(BF16)
        gbr_ref[...] += _dot(arb, ub, TN)
        gbi_ref[...] += _dot(aib, ub, TN)
        row0 = lax.broadcasted_iota(jnp.int32, (SL, SCB), 0) == 0
        last_r = jnp.where(i > 0, pr_ref[7:8, :], 0.0)
        last_i = jnp.where(i > 0, pi_ref[7:8, :], 0.0)
        xpr = jnp.where(row0, last_r, pltpu.roll(xr_ref[...], 1, 0))
        xpi = jnp.where(row0, last_i, pltpu.roll(xi_ref[...], 1, 0))
        glr_ref[...] += jnp.sum(ar * xpr + ai * xpi, axis=0, keepdims=True)
        gli_ref[...] += jnp.sum(ai * xpr - ar * xpi, axis=0, keepdims=True)

    xs = pl.BlockSpec((SL, SCB), lambda j, i: (i, j))
    prev = pl.BlockSpec((8, SCB), lambda j, i: (jnp.maximum(i * per - 1, 0), j))
    blk = pl.BlockSpec((None, SCB, 128), lambda j, i: (j, 0, 0))
    u128 = pl.BlockSpec((SL, 128), lambda j, i: (i, j))
    vec = pl.BlockSpec((1, SCB), lambda j, i: (0, j))
    return pl.pallas_call(
        body, name="s5_in_bwd", grid=(SB, n // SL),
        in_specs=[xs, xs, blk, blk, pl.BlockSpec((SL, 128), lambda j, i: (i, uoff + j)), u128, xs, xs, prev, prev],
        out_specs=[u128, blk, blk, vec, vec],
        out_shape=[jax.ShapeDtypeStruct((n, DS5), BF16)] + [jax.ShapeDtypeStruct((SB, SCB, 128), F32)] * 2
        + [jax.ShapeDtypeStruct((1, NCH), F32)] * 2,
        compiler_params=_cp("parallel", "arbitrary"),
    )(are, aim, brt, bit, proj, dus, xre, xim, xre, xim)


MC = 1024


def _merge_specs():
    ga = pl.BlockSpec((TL, MC), lambda i, j: (i, OFF_GA // MC + j))
    gb = pl.BlockSpec((TL, MC), lambda i, j: (i, OFF_GB // MC + j))
    col = pl.BlockSpec((TL, MC), lambda i, j: (i, j))
    gate = pl.BlockSpec((TL, MC), lambda i, j: (i, D // MC + j))
    return ga, gb, col, gate


def _merge_fwd(proj, ya, vg):
    n = ya.shape[0]

    def body(ga_ref, gb_ref, ya_ref, v_ref, g_ref, o_ref):
        yb = v_ref[...] * _sig(g_ref[...])
        o_ref[...] = (_sig(ga_ref[...]) * ya_ref[...] + _sig(gb_ref[...]) * yb).astype(BF16)

    ga, gb, col, gate = _merge_specs()
    return pl.pallas_call(
        body, name="merge_fwd", grid=(n // TL, D // MC), in_specs=[ga, gb, col, col, gate], out_specs=col,
        out_shape=jax.ShapeDtypeStruct((n, D), BF16), compiler_params=_cp("parallel", "parallel"),
    )(proj, proj, ya, vg, vg)


def _merge_bwd(dm, proj, ya, vg):
    n = ya.shape[0]

    def body(dm_ref, ga_ref, gb_ref, ya_ref, v_ref, g_ref, dga_ref, dgb_ref, dya_ref, dv_ref, dg_ref):
        d = dm_ref[...]
        sa, sb, sg = _sig(ga_ref[...]), _sig(gb_ref[...]), _sig(g_ref[...])
        v = v_ref[...]
        yb = v * sg
        dga_ref[...] = (d * ya_ref[...] * sa * (1.0 - sa)).astype(BF16)
        dgb_ref[...] = (d * yb * sb * (1.0 - sb)).astype(BF16)
        dya_ref[...] = (d * sa).astype(BF16)
        dyb = d * sb
        dv_ref[...] = (dyb * sg).astype(BF16)
        dg_ref[...] = (dyb * v * sg * (1.0 - sg)).astype(BF16)

    ga, gb, col, gate = _merge_specs()
    o = jax.ShapeDtypeStruct((n, D), BF16)
    return pl.pallas_call(
        body, name="merge_bwd", grid=(n // TL, D // MC), in_specs=[col, ga, gb, col, col, gate],
        out_specs=[col] * 5, out_shape=[o] * 5, compiler_params=_cp("parallel", "parallel"),
    )(dm, proj, proj, ya, vg, vg)


def _adamw_update(wv, gv, mv, vv):
    nm = B1 * mv + (1.0 - B1) * gv
    nv = B2 * vv + (1.0 - B2) * (gv * gv)
    m_hat = nm / (1.0 - B1 ** STEP)
    v_hat = nv / (1.0 - B2 ** STEP)
    return -LR * (m_hat / (jnp.sqrt(v_hat) + AEPS) + WD * wv), nm, nv


def _adamw(w, g, m, v, name):
    r, c = w.shape
    tr = _pick(r, 128)

    def body(w_ref, g_ref, m_ref, v_ref, d_ref, nm_ref, nv_ref):
        d_ref[...], nm_ref[...], nv_ref[...] = _adamw_update(w_ref[...], g_ref[...], m_ref[...], v_ref[...])

    blk = pl.BlockSpec((tr, c), lambda i: (i, 0))
    o = jax.ShapeDtypeStruct((r, c), F32)
    return pl.pallas_call(
        body, name=name, grid=(r // tr,), in_specs=[blk] * 4, out_specs=[blk] * 3, out_shape=[o] * 3,
        compiler_params=_cp("parallel"),
    )(w, g, m, v)


def _adamw_halves(w, g_mine, g_other, m, v, cidx, name):
    _, r, c = w.shape
    hr, gc = g_mine.shape
    tr = _pick(hr, 128)
    nbh = hr // tr
    assert gc == c and 2 * hr - tr < r <= 2 * hr

    def body(cs, w_ref, gm_ref, go_ref, m_ref, v_ref, g_ref, d_ref, nm_ref, nv_ref):
        mine = pl.program_id(0) // nbh == cs[0]
        gv = jnp.where(mine, gm_ref[...], go_ref[...])
        g_ref[...] = gv
        d_ref[...], nm_ref[...], nv_ref[...] = _adamw_update(w_ref[...], gv, m_ref[...], v_ref[...])

    blk = pl.BlockSpec((None, tr, c), lambda i, cs: (0, i, 0))
    gblk = pl.BlockSpec((tr, gc), lambda i, cs: (i % nbh, 0))
    o = jax.ShapeDtypeStruct((1, r, c), F32)
    return pl.pallas_call(
        body, name=name,
        grid_spec=pltpu.PrefetchScalarGridSpec(num_scalar_prefetch=1, grid=(2 * nbh,),
                                               in_specs=[blk, gblk, gblk, blk, blk], out_specs=[blk] * 4),
        out_shape=[o] * 4, compiler_params=_cp("parallel"),
    )(cidx, w, g_mine, g_other, m, v)


def _chip_sum(part, sib, cidx, name):
    _, r, cc = part.shape
    hr = r // 2
    tr = _pick(hr, 256)

    def body(cs, p_ref, s_ref, o_ref):
        o_ref[...] = (p_ref[...].astype(F32) + s_ref[...].astype(F32)).astype(BF16)

    blk = pl.BlockSpec((None, tr, cc), lambda k, i, cs: (k, i, 0))
    return pl.pallas_call(
        body, name=name,
        grid_spec=pltpu.PrefetchScalarGridSpec(
            num_scalar_prefetch=1, grid=(4, hr // tr),
            in_specs=[pl.BlockSpec((None, None, tr, cc), lambda k, i, cs: (k, cs[0], i, 0)), blk], out_specs=blk),
        out_shape=jax.ShapeDtypeStruct((4, hr, cc), BF16), compiler_params=_cp("parallel", "parallel"),
    )(cidx, part.reshape(4, 2, hr, cc), sib)


def _shard_sum(own, got, sidx, name):
    _, hr, cc = own.shape
    tr = _pick(hr, 256)

    def body(cs, own_ref, g0, g1, g2, g3, o_ref):
        acc = None
        for k, g_ref in enumerate((g0, g1, g2, g3)):
            term = jnp.where(cs[0] == k, own_ref[...], g_ref[...]).astype(F32)
            acc = term if acc is None else acc + term
        o_ref[...] = acc

    def got_spec(k):
        return pl.BlockSpec((None, tr, cc), lambda i, cs: (jnp.where(cs[0] == k, (k + 1) % 4, k), i, 0))

    return pl.pallas_call(
        body, name=name,
        grid_spec=pltpu.PrefetchScalarGridSpec(
            num_scalar_prefetch=1, grid=(hr // tr,),
            in_specs=[pl.BlockSpec((None, tr, cc), lambda i, cs: (cs[0], i, 0))] + [got_spec(k) for k in range(4)],
            out_specs=pl.BlockSpec((tr, cc), lambda i, cs: (i, 0))),
        out_shape=jax.ShapeDtypeStruct((hr, cc), F32), compiler_params=_cp("parallel"),
    )(sidx, own, got, got, got, got)


def _sum_slabs(xs, name, out_dtype=F32):
    r, c = xs[0].shape
    tr = _pick(r, 256)

    def body(*refs):
        acc = refs[0][...].astype(F32)
        for ref in refs[1:-1]:
            acc = acc + ref[...].astype(F32)
        refs[-1][...] = acc.astype(out_dtype)

    blk = pl.BlockSpec((tr, c), lambda i: (i, 0))
    return pl.pallas_call(
        body, name=name, grid=(r // tr,), in_specs=[blk] * len(xs), out_specs=blk,
        out_shape=jax.ShapeDtypeStruct((r, c), out_dtype), compiler_params=_cp("parallel"),
    )(*xs)


def _place():
    return lax.axis_index("x"), lax.axis_index("y"), lax.axis_index("c")


def _gather_small(v):
    m_per, n = v.shape

    def body(x_ref, out_ref, send_sems, recv_sems, local_sem):
        x, y, c = _place()
        me, sibling = (x, y, c), (x, y, 1 - c)
        chips = [(1 - x, y), (x, 1 - y), (1 - x, 1 - y)]

        def rows(px, py, pc):
            return out_ref.at[pl.ds((4 * px + 2 * py + pc) * m_per, m_per), :]

        def copy(k, block, to, src=None):
            return pltpu.make_async_remote_copy(
                src_ref=rows(*block) if src is None else src, dst_ref=rows(*block),
                send_sem=send_sems.at[k], recv_sem=recv_sems.at[k], device_id=to, device_id_type=MESH)

        mine = pltpu.make_async_copy(x_ref, rows(*me), local_sem)
        mine.start()
        first = [copy(0, me, sibling, src=x_ref)]
        first += [copy(1 + j, me, (*chip, c), src=x_ref) for j, chip in enumerate(chips)]
        for cp in first:
            cp.start()
        passed = [copy(4 + j, (*chip, c), sibling) for j, chip in enumerate(chips)]
        for j, chip in enumerate(chips):
            copy(1 + j, (*chip, c), me).wait_recv()
            passed[j].start()
        copy(0, sibling, me).wait_recv()
        for j, chip in enumerate(chips):
            copy(4 + j, (*chip, 1 - c), me).wait_recv()
        for cp in first + passed:
            cp.wait_send()
        mine.wait()

    return pl.pallas_call(
        body, name="gather_small_%d" % m_per,
        out_shape=jax.ShapeDtypeStruct((8 * m_per, n), v.dtype),
        in_specs=[pl.BlockSpec(memory_space=pltpu.VMEM)], out_specs=pl.BlockSpec(memory_space=pltpu.VMEM),
        scratch_shapes=[pltpu.SemaphoreType.DMA((7,)), pltpu.SemaphoreType.DMA((7,)), pltpu.SemaphoreType.DMA],
        compiler_params=pltpu.CompilerParams(vmem_limit_bytes=VMEM_LIMIT),
    )(v)


def _allsum_small(v, name):
    r = v.shape[0]
    g = _gather_small(v)
    return _sum_slabs([g[k * r:(k + 1) * r] for k in range(8)], name)


def _gather_big(shards):
    nt = len(shards)

    def body(*refs):
        ins, outs = refs[:nt], refs[nt:2 * nt]
        send_sems, recv_sems = refs[2 * nt:]
        x, y, c = _place()
        s = 2 * x + y
        sibling = (x, y, 1 - c)
        chips = [(1 - x, y), (x, 1 - y), (1 - x, 1 - y)]

        def half(t, slot, h):
            hr = ins[t].shape[0] // 2
            return outs[t].at[slot, pl.ds(h * hr, hr), :]

        def ici(t, j, src_slot, to):
            hr = ins[t].shape[0] // 2
            return pltpu.make_async_remote_copy(
                src_ref=ins[t].at[pl.ds(c * hr, hr), :], dst_ref=half(t, src_slot, c),
                send_sem=send_sems.at[6 * t + j], recv_sem=recv_sems.at[6 * t + j], device_id=to, device_id_type=MESH)

        def d2d(t, j, slot, h):
            return pltpu.make_async_remote_copy(
                src_ref=half(t, slot, h), dst_ref=half(t, slot, h),
                send_sem=send_sems.at[6 * t + 3 + j], recv_sem=recv_sems.at[6 * t + 3 + j],
                device_id=sibling, device_id_type=MESH)

        sends = [ici(t, j, s, (*chip, c)) for t in range(nt) for j, chip in enumerate(chips)]
        for cp in sends:
            cp.start()
        passed = []
        for t in range(nt):
            for j, (px, py) in enumerate(chips):
                ici(t, j, 2 * px + py, (x, y, c)).wait_recv()
                cp = d2d(t, j, 2 * px + py, c)
                cp.start()
                passed.append(cp)
        for t in range(nt):
            for j, (px, py) in enumerate(chips):
                d2d(t, j, 2 * px + py, 1 - c).wait_recv()
        for cp in sends + passed:
            cp.wait_send()

    return pl.pallas_call(
        body, name="gather_big",
        out_shape=[jax.ShapeDtypeStruct((4,) + a.shape, a.dtype) for a in shards],
        in_specs=[ANY] * nt, out_specs=[ANY] * nt,
        scratch_shapes=[pltpu.SemaphoreType.DMA((6 * nt,)), pltpu.SemaphoreType.DMA((6 * nt,))],
    )(*shards)


def _swap_halves(parts):
    nt = len(parts)

    def body(*refs):
        ins, outs = refs[:nt], refs[nt:2 * nt]
        send_sems, recv_sems = refs[2 * nt:]
        x, y, c = _place()
        cps = []
        for t in range(nt):
            hr = ins[t].shape[1] // 2
            cps.append(pltpu.make_async_remote_copy(
                src_ref=ins[t].at[:, pl.ds((1 - c) * hr, hr), :], dst_ref=outs[t],
                send_sem=send_sems.at[t], recv_sem=recv_sems.at[t], device_id=(x, y, 1 - c), device_id_type=MESH))
        for cp in cps:
            cp.start()
        for cp in cps:
            cp.wait()

    return pl.pallas_call(
        body, name="swap_halves",
        out_shape=[jax.ShapeDtypeStruct((4, a.shape[1] // 2, a.shape[2]), a.dtype) for a in parts],
        in_specs=[ANY] * nt, out_specs=[ANY] * nt,
        scratch_shapes=[pltpu.SemaphoreType.DMA((nt,)), pltpu.SemaphoreType.DMA((nt,))],
    )(*parts)


def _scatter_chips(parts):
    nt = len(parts)

    def body(*refs):
        ins, outs = refs[:nt], refs[nt:2 * nt]
        send_sems, recv_sems = refs[2 * nt:]
        x, y, c = _place()
        s = 2 * x + y
        chips = [(1 - x, y), (x, 1 - y), (1 - x, 1 - y)]
        cps = []
        for t in range(nt):
            for j, (px, py) in enumerate(chips):
                cps.append(pltpu.make_async_remote_copy(
                    src_ref=ins[t].at[2 * px + py], dst_ref=outs[t].at[s],
                    send_sem=send_sems.at[3 * t + j], recv_sem=recv_sems.at[3 * t + j],
                    device_id=(px, py, c), device_id_type=MESH))
        for cp in cps:
            cp.start()
        for t in range(nt):
            for j, (px, py) in enumerate(chips):
                pltpu.make_async_remote_copy(
                    src_ref=ins[t].at[s], dst_ref=outs[t].at[2 * px + py],
                    send_sem=send_sems.at[3 * t + j], recv_sem=recv_sems.at[3 * t + j],
                    device_id=(px, py, c), device_id_type=MESH).wait_recv()
        for cp in cps:
            cp.wait_send()

    return pl.pallas_call(
        body, name="scatter_chips",
        out_shape=[jax.ShapeDtypeStruct(a.shape, a.dtype) for a in parts],
        in_specs=[ANY] * nt, out_specs=[ANY] * nt,
        scratch_shapes=[pltpu.SemaphoreType.DMA((3 * nt,)), pltpu.SemaphoreType.DMA((3 * nt,))],
    )(*parts)


def _swap_whole(halves):
    nt = len(halves)

    def body(*refs):
        ins, outs = refs[:nt], refs[nt:2 * nt]
        send_sems, recv_sems = refs[2 * nt:]
        x, y, c = _place()
        cps = [pltpu.make_async_remote_copy(
            src_ref=ins[t], dst_ref=outs[t], send_sem=send_sems.at[t], recv_sem=recv_sems.at[t],
            device_id=(x, y, 1 - c), device_id_type=MESH) for t in range(nt)]
        for cp in cps:
            cp.start()
        for cp in cps:
            cp.wait()

    return pl.pallas_call(
        body, name="swap_whole",
        out_shape=[jax.ShapeDtypeStruct(a.shape, a.dtype) for a in halves],
        in_specs=[ANY] * nt, out_specs=[ANY] * nt,
        scratch_shapes=[pltpu.SemaphoreType.DMA((nt,)), pltpu.SemaphoreType.DMA((nt,))],
    )(*halves)


def _reduce_big(parts, cidx, sidx):
    nt = len(parts)
    sib = _swap_halves(parts)
    chip_sum = [_chip_sum(parts[t], sib[t], cidx, "chip_sum_%d" % t) for t in range(nt)]
    got = _scatter_chips(chip_sum)
    mine = [_shard_sum(chip_sum[t], got[t], sidx, "shard_sum_%d" % t) for t in range(nt)]
    return mine, _swap_whole(mine)


def _s5_params(lam_re, lam_im, log_dt, b_re, b_im):
    lr = jnp.minimum(lam_re, EIG_MAX)
    dt = jnp.exp(log_dt)[:, None]
    mag = jnp.exp(lr * dt)
    lbr, lbi = mag * jnp.cos(lam_im * dt), mag * jnp.sin(lam_im * dt)
    den = lr * lr + lam_im * lam_im
    qr = ((lbr - 1.0) * lr + lbi * lam_im) / den
    qi = (lbi * lr - (lbr - 1.0) * lam_im) / den
    bbr = qr[..., None] * b_re - qi[..., None] * b_im
    bbi = qr[..., None] * b_im + qi[..., None] * b_re
    return lbr, lbi, bbr, bbi


def _cmul(a, b):
    return a[0] * b[0] - a[1] * b[1], a[0] * b[1] + a[1] * b[0]


def _scan_table(lr, li, reverse):
    l1 = (lr.reshape(1, NCH), li.reshape(1, NCH))
    pows = [l1]
    for _ in range(7):
        pows.append(_cmul(pows[-1], l1))
    r = jnp.arange(8)[:, None]
    tabs = []
    for k in (1, 2, 4):
        keep = (r < 8 - k) if reverse else (r >= k)
        tabs += [jnp.where(keep, pows[k - 1][0], 0.0), jnp.where(keep, pows[k - 1][1], 0.0)]
    order = range(7, -1, -1) if reverse else range(8)
    tabs += [jnp.concatenate([pows[e][0] for e in order], axis=0), jnp.concatenate([pows[e][1] for e in order], axis=0)]
    return jnp.stack(tabs).astype(F32)


_EYE8 = lambda: jnp.eye(8, dtype=F32)


def _to_in_blocks(b):
    return jnp.einsum("jgpc,gh->jgchp", b.reshape(8, 8, 64, 16), _EYE8()).reshape(8, 128, 512)


def _to_out_blocks(cm):
    return jnp.einsum("jgcp,gh->jgphc", cm.reshape(8, 8, 16, 64), _EYE8()).reshape(8, 512, 128)


def _from_out_blocks(g):
    return jnp.einsum("jgphc,gh->jgpc", g.reshape(8, 8, 64, 8, 16), _EYE8()).reshape(64, 64, 16)


def _local_step(x, target, p):
    n = x.shape[0]
    g = {}
    hn1 = _rms_fwd(x, p["norm_mix_w"], "rms_mix")
    proj = _matmul(hn1, p["w_main"], "nt", "mm_in")
    dtraw = _matmul(hn1, p["w_dt"], "nt", "mm_dt")
    xbc = _conv_a_fwd(proj, p["conv_a_w"], p["conv_a_b"])
    dth = jnp.pad(dtraw[:, :NG * HPG].reshape(n, NG, HPG), ((0, 0), (0, 0), (0, 8 - HPG)))
    dtc = dth.transpose(1, 0, 2)
    dtr = dth.transpose(1, 2, 0)
    pad_h = lambda v: jnp.pad(v.reshape(NG, HPG), ((0, 0), (0, 8 - HPG)))
    hpc = jnp.zeros((NG, 8, 8), F32).at[:, 0, :].set(pad_h(p["dt_bias"])).at[:, 1, :].set(pad_h(p["a_log"]))
    hpr = jnp.zeros((NG, 8, 128), F32).at[:, :, 0].set(pad_h(p["dt_bias"])).at[:, :, 1].set(pad_h(p["a_log"]))
    dexp = jnp.repeat(p["d_a"].reshape(1, NG * HPG), HD, axis=1)
    e8 = (jnp.arange(8)[:, None] == (jnp.arange(GW)[None, :] // HD)).astype(F32)
    yssd, sprev = _ssd_fwd(xbc, dtc, dtr, hpc, hpr, dexp, e8)
    yn = _gnorm_fwd(yssd, proj, p["norm_a_w"])
    ya = _matmul(yn, p["w_proj_a"], "nn", "mm_proj")

    (lbr, lbi, bbr, bbi), s5_vjp = jax.vjp(_s5_params, p["s5_lam_re"], p["s5_lam_im"], p["s5_log_dt"],
                                           p["s5_b_re"], p["s5_b_im"])
    bin_r, bin_i = _to_in_blocks(bbr), _to_in_blocks(bbi)
    cout_r, cout_in = _to_out_blocks(p["s5_c_re"]), _to_out_blocks(-p["s5_c_im"])
    bur, bui = _s5_in(proj, bin_r.astype(BF16), bin_i.astype(BF16))
    xre, xim = _s5_scan(bur, bui, _scan_table(lbr, lbi, False), False, "s5_scan_fwd")
    ypre, g5 = _s5_out(xre, xim, cout_r.astype(BF16), cout_in.astype(BF16), proj, p["s5_d"])
    vg = _matmul(g5, p["w_s5_glu"], "nn", "mm_glu", b_stacked=True)
    merged = _merge_fwd(proj, ya, vg)
    h1 = _matmul(merged, p["w_out"], "nn", "mm_out", residual=x)
    hn2 = _rms_fwd(h1, p["norm_ffn_w"], "rms_ffn")
    up = _matmul(hn2, p["w_up"], "nn", "mm_up", tn=1408, b_stacked=True)
    act = _conv_ffn_fwd(up, p["conv_ffn_w"], p["conv_ffn_b"])
    h2 = _matmul(act, p["w_down"], "nn", "mm_down", tk=DFF // 2, residual=h1)
    dh2, dh2b, g["norm_final_w"], loss_blk = _final(h2, p["norm_final_w"], target)
    g["w_down"] = _matmul(act, dh2b, "tn", "mm_gw_down", out_dtype=BF16).reshape(4, DFF // 4, D)
    dact = _matmul(dh2b, p["w_down"], "nt", "mm_dact", out_dtype=BF16)
    dup, g["conv_ffn_w"], g["conv_ffn_b"] = _conv_ffn_bwd(up, dact, p["conv_ffn_w"], p["conv_ffn_b"])
    g["w_up"] = _matmul(hn2, dup, "tn", "mm_gw_up", out_dtype=BF16, tn=1408, out_stacked=True)
    dhn2 = _matmul(dup, p["w_up"], "nt", "mm_dhn2", tk=2816, b_stacked=True)
    dh1, dh1b, g["norm_ffn_w"] = _rms_bwd(dhn2, h1, p["norm_ffn_w"], dh2, "rms_ffn_bwd")
    g["w_out"] = _matmul(merged, dh1b, "tn", "mm_gw_out", out_dtype=BF16).reshape(4, D // 4, D)
    dmerged = _matmul(dh1b, p["w_out"], "nt", "mm_dmerged")
    dga, dgb, dya, dval, dgate = _merge_bwd(dmerged, proj, ya, vg)
    dvg = jnp.concatenate([dval, dgate], axis=1)
    g["w_s5_glu"] = _matmul(g5, dvg, "tn", "mm_gw_glu", out_dtype=BF16, out_stacked=True)
    dg5 = _matmul(dvg, p["w_s5_glu"], "nt", "mm_dg5", b_stacked=True)
    tr = lambda b: b.transpose(0, 2, 1)
    gxr, gxi, dus, gcr, gci, g["s5_d"] = _s5_out_bwd(dg5, ypre, tr(cout_r).astype(BF16), tr(cout_in).astype(BF16),
                                                     proj, p["s5_d"], xre, xim)
    are, aim = _s5_scan(gxr, gxi, _scan_table(lbr, -lbi, True), True, "s5_scan_bwd")
    du, gbr, gbi, glr, gli = _s5_in_bwd(are, aim, tr(bin_r).astype(BF16), tr(bin_i).astype(BF16), proj, dus, xre, xim)
    g["s5_c_re"] = _from_out_blocks(gcr).transpose(0, 2, 1)
    g["s5_c_im"] = _from_out_blocks(gci).transpose(0, 2, 1)
    (g["s5_lam_re"], g["s5_lam_im"], g["s5_log_dt"], g["s5_b_re"], g["s5_b_im"]) = s5_vjp(
        (glr.reshape(64, 64), gli.reshape(64, 64), _from_out_blocks(gbr), _from_out_blocks(gbi)))
    g["w_proj_a"] = _matmul(yn, dya, "tn", "mm_gw_proj", out_dtype=BF16).reshape(4, DI // 4, D)
    dyn = _matmul(dya, p["w_proj_a"], "nt", "mm_dyn")
    dyssd, dz, g["norm_a_w"] = _gnorm_bwd(dyn, yssd, proj, p["norm_a_w"])
    e8t = e8.T
    dxs, dbm, dcm, draw, pd, ps = _ssd_bwd(xbc, dtc, dtr, hpc, hpr, dexp, e8, e8t, sprev, dyssd)
    g["dt_bias"] = ps[:, 0, :HPG].reshape(1, NG * HPG)
    g["a_log"] = ps[:, 1, :HPG].reshape(1, NG * HPG)
    g["d_a"] = pd.reshape(NG * HPG, HD).sum(axis=1).reshape(1, NG * HPG)
    ddt = draw[:, :, :HPG].transpose(1, 0, 2).reshape(n, NG * HPG).astype(BF16)
    dxbc_parts, gcw, gcb = [], [], []
    for arr, col0, nm in ((dxs, 0, "conv_a_bwd_x"), (dbm, DI, "conv_a_bwd_b"), (dcm, DI + NG * NS, "conv_a_bwd_c")):
        dpart, gw_, gb_ = _conv_a_bwd(proj, arr, p["conv_a_w"], p["conv_a_b"], col0, nm)
        dxbc_parts.append(dpart)
        gcw.append(gw_)
        gcb.append(gb_)
    g["conv_a_w"] = jnp.concatenate(gcw, axis=1)
    g["conv_a_b"] = jnp.concatenate(gcb, axis=1)
    dorig = jnp.concatenate([dz] + dxbc_parts + [ddt, du, dga, dgb], axis=1)
    dproj = jnp.concatenate([jnp.pad(dorig[:, WSH * k:WSH * (k + 1)], ((0, 0), (0, WPAD - WSH))) for k in range(4)],
                            axis=1)
    g["w_in"] = _matmul(dproj, hn1, "tn", "mm_gw_in", out_dtype=BF16, tm=896, tn=2048).reshape(4, WPAD, D)
    dhn1 = _matmul(dproj, p["w_in"], "nn", "mm_dhn1", tk=2688)
    gx, _, g["norm_mix_w"] = _rms_bwd(dhn1, x, p["norm_mix_w"], dh1, "rms_mix_bwd")
    return loss_blk, gx, g


BIG = ["w_in", "w_proj_a", "w_s5_glu", "w_out", "w_up", "w_down"]
SMALL = ["norm_mix_w", "conv_a_w", "conv_a_b", "dt_bias", "a_log", "d_a", "norm_a_w", "s5_lam_re", "s5_lam_im",
         "s5_log_dt", "s5_b_re", "s5_b_im", "s5_c_re", "s5_c_im", "s5_d", "norm_ffn_w", "conv_ffn_w", "conv_ffn_b",
         "norm_final_w"]
ORDER = ["norm_mix_w", "w_in", "conv_a_w", "conv_a_b", "dt_bias", "a_log", "d_a", "norm_a_w", "w_proj_a", "s5_lam_re",
         "s5_lam_im", "s5_log_dt", "s5_b_re", "s5_b_im", "s5_c_re", "s5_c_im", "s5_d", "w_s5_glu", "w_out",
         "norm_ffn_w", "w_up", "conv_ffn_w", "conv_ffn_b", "w_down", "norm_final_w"]
IN_SPLIT = [DI, DI + CONVD, DI + CONVD + NG * HPG]
CONV_FULL = {"conv_a_w": (KA, CONVD), "conv_ffn_w": (KF, 2 * DFF)}


def _pack(arrs):
    flat = jnp.concatenate([a.reshape(-1).astype(F32) for a in arrs])
    total = flat.shape[0]
    padded = -(-total // 1024) * 1024
    return jnp.pad(flat, (0, padded - total)).reshape(padded // 128, 128)


def _unpack(block, shapes):
    flat = block.reshape(-1)
    out, at = [], 0
    for sh in shapes:
        size = math.prod(sh)
        out.append(flat[at:at + size].reshape(sh))
        at += size
    return out


def _stack_cols(a):
    return a.transpose(1, 0, 2).reshape(a.shape[1], 4 * a.shape[2])


def _unstack_cols(a):
    return a.reshape(a.shape[0], 4, a.shape[1] // 4).transpose(1, 0, 2)


def kernel(x, norm_mix_w, w_in, conv_a_w, conv_a_b, dt_bias, a_log, d_a, norm_a_w, w_proj_a, s5_lam_re, s5_lam_im, s5_log_dt, s5_b_re, s5_b_im, s5_c_re, s5_c_im, s5_d, w_s5_glu, w_out, norm_ffn_w, w_up, conv_ffn_w, conv_ffn_b, w_down, norm_final_w, loss_target, m_norm_mix_w, m_w_in, m_conv_a_w, m_conv_a_b, m_dt_bias, m_a_log, m_d_a, m_norm_a_w, m_w_proj_a, m_s5_lam_re, m_s5_lam_im, m_s5_log_dt, m_s5_b_re, m_s5_b_im, m_s5_c_re, m_s5_c_im, m_s5_d, m_w_s5_glu, m_w_out, m_norm_ffn_w, m_w_up, m_conv_ffn_w, m_conv_ffn_b, m_w_down, m_norm_final_w, v_norm_mix_w, v_w_in, v_conv_a_w, v_conv_a_b, v_dt_bias, v_a_log, v_d_a, v_norm_a_w, v_w_proj_a, v_s5_lam_re, v_s5_lam_im, v_s5_log_dt, v_s5_b_re, v_s5_b_im, v_s5_c_re, v_s5_c_im, v_s5_d, v_w_s5_glu, v_w_out, v_norm_ffn_w, v_w_up, v_conv_ffn_w, v_conv_ffn_b, v_w_down, v_norm_final_w):
    w = dict(norm_mix_w=norm_mix_w, w_in=w_in, conv_a_w=conv_a_w, conv_a_b=conv_a_b, dt_bias=dt_bias, a_log=a_log, d_a=d_a, norm_a_w=norm_a_w, w_proj_a=w_proj_a, s5_lam_re=s5_lam_re, s5_lam_im=s5_lam_im, s5_log_dt=s5_log_dt, s5_b_re=s5_b_re, s5_b_im=s5_b_im, s5_c_re=s5_c_re, s5_c_im=s5_c_im, s5_d=s5_d, w_s5_glu=w_s5_glu, w_out=w_out, norm_ffn_w=norm_ffn_w, w_up=w_up, conv_ffn_w=conv_ffn_w, conv_ffn_b=conv_ffn_b, w_down=w_down, norm_final_w=norm_final_w)
    m = dict(norm_mix_w=m_norm_mix_w, w_in=m_w_in, conv_a_w=m_conv_a_w, conv_a_b=m_conv_a_b, dt_bias=m_dt_bias, a_log=m_a_log, d_a=m_d_a, norm_a_w=m_norm_a_w, w_proj_a=m_w_proj_a, s5_lam_re=m_s5_lam_re, s5_lam_im=m_s5_lam_im, s5_log_dt=m_s5_log_dt, s5_b_re=m_s5_b_re, s5_b_im=m_s5_b_im, s5_c_re=m_s5_c_re, s5_c_im=m_s5_c_im, s5_d=m_s5_d, w_s5_glu=m_w_s5_glu, w_out=m_w_out, norm_ffn_w=m_norm_ffn_w, w_up=m_w_up, conv_ffn_w=m_conv_ffn_w, conv_ffn_b=m_conv_ffn_b, w_down=m_w_down, norm_final_w=m_norm_final_w)
    v = dict(norm_mix_w=v_norm_mix_w, w_in=v_w_in, conv_a_w=v_conv_a_w, conv_a_b=v_conv_a_b, dt_bias=v_dt_bias, a_log=v_a_log, d_a=v_d_a, norm_a_w=v_norm_a_w, w_proj_a=v_w_proj_a, s5_lam_re=v_s5_lam_re, s5_lam_im=v_s5_lam_im, s5_log_dt=v_s5_log_dt, s5_b_re=v_s5_b_re, s5_b_im=v_s5_b_im, s5_c_re=v_s5_c_re, s5_c_im=v_s5_c_im, s5_d=v_s5_d, w_s5_glu=v_w_s5_glu, w_out=v_w_out, norm_ffn_w=v_norm_ffn_w, w_up=v_w_up, conv_ffn_w=v_conv_ffn_w, conv_ffn_b=v_conv_ffn_b, w_down=v_w_down, norm_final_w=v_norm_final_w)
    xi, yi, ci = _place()
    chip = 2 * xi + yi

    cidx = jnp.reshape(ci, (1,)).astype(jnp.int32)
    sidx = jnp.reshape(chip, (1,)).astype(jnp.int32)

    tw = lambda a: jnp.transpose(a[0])[None]
    w["w_in"], m["w_in"], v["w_in"] = tw(w_in), tw(m_w_in), tw(v_w_in)
    shards = [w[k][0].astype(BF16) for k in BIG]
    shards[0] = jnp.pad(shards[0], ((0, WPAD - WSH), (0, 0)))
    gathered = _gather_big(shards)
    full = {k: lax.dynamic_update_slice(gathered[t], shards[t][None], (chip, 0, 0)) for t, k in enumerate(BIG)}
    conv_blocks = []
    for k, (taps, cols) in CONV_FULL.items():
        shard = jnp.where(ci == 0, w[k][0], 0.0)
        conv_blocks.append(lax.dynamic_update_slice_in_dim(jnp.zeros((taps, cols), F32), shard, chip * (cols // 4), 1))
    conv_full = _unpack(_allsum_small(_pack(conv_blocks), "sum_conv_w"), [CONV_FULL[k] for k in CONV_FULL])

    win = jnp.concatenate([full["w_in"][k, :WSH] for k in range(4)], axis=0)
    p = {
        "w_main": jnp.concatenate([win[:IN_SPLIT[1]], win[IN_SPLIT[2]:]], axis=0),
        "w_dt": jnp.pad(win[IN_SPLIT[1]:IN_SPLIT[2]], ((0, 128 - NG * HPG), (0, 0))),
        "w_in": full["w_in"].reshape(4 * WPAD, D),
        "w_proj_a": full["w_proj_a"].reshape(DI, D),
        "w_s5_glu": full["w_s5_glu"],
        "w_out": full["w_out"].reshape(D, D),
        "w_up": full["w_up"],
        "w_down": full["w_down"].reshape(DFF, D),
        "conv_a_w": conv_full[0], "conv_ffn_w": conv_full[1],
        "conv_a_b": conv_a_b, "conv_ffn_b": conv_ffn_b,
        "norm_mix_w": norm_mix_w, "norm_a_w": norm_a_w, "norm_ffn_w": norm_ffn_w,
        "norm_final_w": norm_final_w.reshape(1, D),
        "dt_bias": dt_bias, "a_log": a_log, "d_a": d_a, "s5_d": s5_d,
        "s5_lam_re": s5_lam_re[0], "s5_lam_im": s5_lam_im[0], "s5_log_dt": s5_log_dt[0],
        "s5_b_re": s5_b_re[0], "s5_b_im": s5_b_im[0], "s5_c_re": s5_c_re[0], "s5_c_im": s5_c_im[0],
    }
    loss_blk, gx, g = _local_step(x[0], loss_target[0], p)

    g_mine, g_other = _reduce_big([g[k] for k in BIG], cidx, sidx)

    small_shapes = [CONV_FULL.get(k, w[k].shape[1:] if k != "norm_final_w" else w[k].shape) for k in SMALL]
    small = _allsum_small(_pack([g[k] for k in SMALL] + [loss_blk[0:1, 0:1]]), "sum_small_grads")
    small_grads = dict(zip(SMALL + ["loss"], _unpack(small, small_shapes + [(1,)])))
    for k, (taps, cols) in CONV_FULL.items():
        small_grads[k] = lax.dynamic_slice_in_dim(small_grads[k], chip * (cols // 4), cols // 4, axis=1)
    loss = small_grads.pop("loss").reshape(())

    grads, delta, new_m, new_v = {}, {}, {}, {}
    for t, k in enumerate(BIG):
        outs = _adamw_halves(w[k], g_mine[t], g_other[t], m[k], v[k], cidx, "adamw_" + k)
        grads[k], delta[k], new_m[k], new_v[k] = [tw(o) for o in outs] if k == "w_in" else outs
    for k in SMALL:
        grads[k] = small_grads[k].reshape(w[k].shape)
    pk = lambda t: _pack([t[k] for k in SMALL])
    d_, m_, v_ = _adamw(pk(w), pk(grads), pk(m), pk(v), "adamw_small")
    shapes = [w[k].shape for k in SMALL]
    for k, dd, mm, vv in zip(SMALL, _unpack(d_, shapes), _unpack(m_, shapes), _unpack(v_, shapes)):
        delta[k], new_m[k], new_v[k] = dd, mm, vv
    return (loss, gx[None], *[grads[k] for k in ORDER], *[delta[k] for k in ORDER],
            *[new_m[k] for k in ORDER], *[new_v[k] for k in ORDER])
```

```python
import functools
import math

import jax
import jax.numpy as jnp
from jax import lax
from jax.experimental import pallas as pl
from jax.experimental.pallas import tpu as pltpu

F32 = jnp.float32
BF16 = jnp.bfloat16
HI = lax.Precision.HIGHEST
MESH = pl.DeviceIdType.MESH
ANY = pl.BlockSpec(memory_space=pl.ANY)

D = 2048
DI = 3072
HD = 64
NG = 8
HPG = 6
GW = HPG * HD
NS = 128
KA = 4
Q = 256
CONVD = DI + 2 * NG * NS
DS5 = 1024
NCH = 4096
DFF = 5632
KF = 3
EPS = 1e-6
EIG_MAX = -1e-4
NMAIN = 13312
OFF_XBC, OFF_U, OFF_GA, OFF_GB = 3072, 8192, 9216, 11264
WSH = 3340
WPAD = 3360
VMEM_LIMIT = 56 * 1024 * 1024

LR, B1, B2, AEPS, WD, STEP = 0.001, 0.9, 0.999, 1e-08, 0.01, 10


def _cp(*sem):
    return pltpu.CompilerParams(dimension_semantics=sem, vmem_limit_bytes=VMEM_LIMIT)


def _sig(x):
    return jax.nn.sigmoid(x)


def _silu(x):
    return x * _sig(x)


def _dsilu(x):
    s = _sig(x)
    return s * (1.0 + x * (1.0 - s))


def _softplus(x):
    return jnp.maximum(x, 0.0) + jnp.log(1.0 + jnp.exp(-jnp.abs(x)))


_GC = math.sqrt(2.0 / math.pi)


def _gelu(x):
    return 0.5 * x * (1.0 + jnp.tanh(_GC * (x + 0.044715 * x * x * x)))


def _dgelu(x):
    t = jnp.tanh(_GC * (x + 0.044715 * x * x * x))
    return 0.5 * (1.0 + t) + 0.5 * x * (1.0 - t * t) * _GC * (1.0 + 3.0 * 0.044715 * x * x)


def _dot(a, b, dims=((1,), (0,)), prec=None):
    return lax.dot_general(a, b, (dims, ((), ())), precision=prec, preferred_element_type=F32)


NT = ((1,), (1,))
TN = ((0,), (0,))


def _pick(n, t):
    for unit in (128, 8):
        for cand in range(min(n, t) // unit * unit, 0, -unit):
            if n % cand == 0:
                return cand
    return n


def _matmul(a, b, mode, name, out_dtype=F32, tm=512, tn=1024, tk=2048, residual=None, b_stacked=False,
            out_stacked=False, after=None):
    if b_stacked:
        _, brows, bn = b.shape
        bshape = (brows, 4 * bn)
    else:
        bshape = b.shape
    if mode == "nn":
        (m, k), (k2, n) = a.shape, bshape
    elif mode == "nt":
        (m, k), (n, k2) = a.shape, bshape
    else:
        (k, m), (k2, n) = a.shape, bshape
    assert k == k2
    tm = _pick(m, tm)
    tn = _pick(n // 4 if (out_stacked or (b_stacked and mode != "nt")) else n, tn)
    tk = _pick(k // 4 if (b_stacked and mode == "nt") else k, tk)
    nk = k // tk
    dims = {"nn": ((1,), (0,)), "nt": NT, "tn": TN}[mode]
    has_res = residual is not None
    n_in = 2 + has_res + (after is not None)

    def body(*refs):
        a_ref, b_ref = refs[0], refs[1]
        r_ref = refs[2] if has_res else None
        o_ref = refs[n_in]
        p = _dot(a_ref[...], b_ref[...], dims)

        def finish(r):
            if has_res:
                r = r + r_ref[...]
            o_ref[...] = r.astype(out_dtype)

        if nk == 1:
            finish(p)
        else:
            acc = refs[-1]
            kk = pl.program_id(2)

            @pl.when(kk == 0)
            def _():
                acc[...] = p

            @pl.when(kk > 0)
            def _():
                acc[...] += p

            @pl.when(kk == nk - 1)
            def _():
                finish(acc[...])

    if mode == "tn":
        a_spec = pl.BlockSpec((tk, tm), lambda i, j, kk: (kk, i))
    else:
        a_spec = pl.BlockSpec((tm, tk), lambda i, j, kk: (i, kk))
    if mode == "nt":
        if b_stacked:
            per = bn // tk
            b_spec = pl.BlockSpec((None, tn, tk), lambda i, j, kk: (kk // per, j, kk % per))
        else:
            b_spec = pl.BlockSpec((tn, tk), lambda i, j, kk: (j, kk))
    elif b_stacked:
        per = bn // tn
        b_spec = pl.BlockSpec((None, tk, tn), lambda i, j, kk: (j // per, kk, j % per))
    else:
        b_spec = pl.BlockSpec((tk, tn), lambda i, j, kk: (kk, j))
    o_spec = pl.BlockSpec((tm, tn), lambda i, j, kk: (i, j))
    if out_stacked:
        per_o = n // 4 // tn
        out_spec = pl.BlockSpec((None, tm, tn), lambda i, j, kk: (j // per_o, i, j % per_o))
        out_shape = jax.ShapeDtypeStruct((4, m, n // 4), out_dtype)
    else:
        out_spec, out_shape = o_spec, jax.ShapeDtypeStruct((m, n), out_dtype)
    in_specs, args = [a_spec, b_spec], [a, b]
    if has_res:
        in_specs.append(o_spec)
        args.append(residual)
    if after is not None:
        in_specs.append(ANY)
        args.append(after)
    return pl.pallas_call(
        body, name=name, grid=(m // tm, n // tn, nk),
        in_specs=in_specs, out_specs=out_spec, out_shape=out_shape,
        scratch_shapes=[pltpu.VMEM((tm, tn), F32)] if nk > 1 else [],
        compiler_params=_cp("parallel", "parallel", "arbitrary"),
    )(*args)


TL = 256


def _rms_fwd(x, w, name, after=None):
    n, d = x.shape

    def body(x_ref, w_ref, *rest):
        xv = x_ref[...]
        r = lax.rsqrt(jnp.mean(xv * xv, axis=-1, keepdims=True) + EPS)
        rest[-1][...] = (xv * r * w_ref[...]).astype(BF16)

    extra = [] if after is None else [after]
    return pl.pallas_call(
        body, name=name, grid=(n // TL,),
        in_specs=[pl.BlockSpec((TL, d), lambda i: (i, 0)), pl.BlockSpec((1, d), lambda i: (0, 0))] + [ANY] * len(extra),
        out_specs=pl.BlockSpec((TL, d), lambda i: (i, 0)),
        out_shape=jax.ShapeDtypeStruct((n, d), BF16), compiler_params=_cp("parallel"),
    )(x, w, *extra)


def _rms_bwd(dhn, x, w, dres, name):
    n, d = x.shape

    def body(g_ref, x_ref, w_ref, r_ref, dx_ref, dxb_ref, gw_ref):
        xv = x_ref[...]
        r = lax.rsqrt(jnp.mean(xv * xv, axis=-1, keepdims=True) + EPS)
        xh = xv * r
        gv = g_ref[...]
        g = gv * w_ref[...]
        dx = r_ref[...] + r * (g - xh * jnp.mean(g * xh, axis=-1, keepdims=True))
        dx_ref[...] = dx
        dxb_ref[...] = dx.astype(BF16)

        @pl.when(pl.program_id(0) == 0)
        def _():
            gw_ref[...] = jnp.zeros_like(gw_ref)

        gw_ref[...] += jnp.sum(gv * xh, axis=0, keepdims=True)

    row = pl.BlockSpec((TL, d), lambda i: (i, 0))
    vec = pl.BlockSpec((1, d), lambda i: (0, 0))
    return pl.pallas_call(
        body, name=name, grid=(n // TL,),
        in_specs=[row, row, vec, row], out_specs=[row, row, vec],
        out_shape=[jax.ShapeDtypeStruct((n, d), F32), jax.ShapeDtypeStruct((n, d), BF16),
                   jax.ShapeDtypeStruct((1, d), F32)],
        compiler_params=_cp("arbitrary"),
    )(dhn, x, w, dres)


def _final(h2, w, target):
    n, d = h2.shape

    def body(x_ref, w_ref, t_ref, dx_ref, dxb_ref, gw_ref, loss_ref):
        xv = x_ref[...]
        r = lax.rsqrt(jnp.mean(xv * xv, axis=-1, keepdims=True) + EPS)
        xh = xv * r
        diff = xh * w_ref[...] - t_ref[...]
        gv = diff * (1.0 / d)
        g = gv * w_ref[...]
        dx = r * (g - xh * jnp.mean(g * xh, axis=-1, keepdims=True))
        dx_ref[...] = dx
        dxb_ref[...] = dx.astype(BF16)

        @pl.when(pl.program_id(0) == 0)
        def _():
            gw_ref[...] = jnp.zeros_like(gw_ref)
            loss_ref[...] = jnp.zeros_like(loss_ref)

        gw_ref[...] += jnp.sum(gv * xh, axis=0, keepdims=True)
        part = 0.5 * jnp.sum(jnp.mean(diff * diff, axis=-1, keepdims=True), axis=0, keepdims=True)
        loss_ref[...] += jnp.broadcast_to(part, loss_ref.shape)

    row = pl.BlockSpec((TL, d), lambda i: (i, 0))
    vec = pl.BlockSpec((1, d), lambda i: (0, 0))
    return pl.pallas_call(
        body, name="final_loss", grid=(n // TL,),
        in_specs=[row, vec, row], out_specs=[row, row, vec, pl.BlockSpec((8, 128), lambda i: (0, 0))],
        out_shape=[jax.ShapeDtypeStruct((n, d), F32), jax.ShapeDtypeStruct((n, d), BF16),
                   jax.ShapeDtypeStruct((1, d), F32), jax.ShapeDtypeStruct((8, 128), F32)],
        compiler_params=_cp("arbitrary"),
    )(h2, w, target)


CT = 512
CL = 512


def _lagged(xf, taps, rows):
    return [xf[8:8 + rows]] + [pltpu.roll(xf, s, 0)[8:8 + rows] for s in range(1, taps)]


def _shift_up(x, u, n):
    if u == 0:
        return x[0:n]
    return pltpu.roll(x, x.shape[0] - u, 0)[0:n]


def _conv_pre(lagged, w_ref, b_ref, taps):
    pre = b_ref[...]
    for k in range(taps):
        pre = pre + w_ref[k:k + 1, :] * lagged[taps - 1 - k]
    return pre


def _conv_back(e, w_ref, taps):
    dx = w_ref[taps - 1:taps, :] * e[0:CL]
    for k in range(taps - 1):
        dx = dx + w_ref[k:k + 1, :] * _shift_up(e, taps - 1 - k, CL)
    return dx


def _halo_specs(n, col_of):
    per = CL // 8
    cur = pl.BlockSpec((CL, CT), lambda j, i, *_: (i, col_of(j)))
    prev = pl.BlockSpec((8, CT), lambda j, i, *_: (jnp.maximum(i * per - 1, 0), col_of(j)))
    nxt = pl.BlockSpec((8, CT), lambda j, i, *_: (jnp.minimum((i + 1) * per, n // 8 - 1), col_of(j)))
    return prev, cur, nxt


def _conv_a_fwd(proj, w, b):
    n = proj.shape[0]
    off = OFF_XBC // CT

    def body(p_ref, x_ref, w_ref, b_ref, o_ref):
        p8 = jnp.where(pl.program_id(1) > 0, p_ref[...], 0.0)
        xf = jnp.concatenate([p8, x_ref[...]], axis=0)
        o_ref[...] = _silu(_conv_pre(_lagged(xf, KA, CL), w_ref, b_ref, KA))

    prev, cur, _ = _halo_specs(n, lambda j: j + off)
    return pl.pallas_call(
        body, name="conv_a_fwd", grid=(CONVD // CT, n // CL),
        in_specs=[prev, cur, pl.BlockSpec((KA, CT), lambda j, i: (0, j)), pl.BlockSpec((1, CT), lambda j, i: (0, j))],
        out_specs=pl.BlockSpec((CL, CT), lambda j, i: (i, j)),
        out_shape=jax.ShapeDtypeStruct((n, CONVD), F32), compiler_params=_cp("parallel", "parallel"),
    )(proj, proj, w, b)


def _conv_a_bwd(proj, dout, w, b, col0, name):
    n, width = dout.shape
    off = (OFF_XBC + col0) // CT
    woff = col0 // CT
    nl = n // CL

    def body(p_ref, x_ref, n_ref, d_ref, dn_ref, w_ref, b_ref, dx_ref, dw_ref, db_ref):
        i = pl.program_id(1)
        xf = jnp.concatenate([jnp.where(i > 0, p_ref[...], 0.0), x_ref[...], n_ref[...]], axis=0)
        lag = _lagged(xf, KA, CL + 8)
        de = jnp.concatenate([d_ref[...], jnp.where(i < nl - 1, dn_ref[...], 0.0)], axis=0)
        se = de * _dsilu(_conv_pre(lag, w_ref, b_ref, KA))
        dx_ref[...] = _conv_back(se, w_ref, KA).astype(BF16)

        @pl.when(i == 0)
        def _():
            dw_ref[...] = jnp.zeros_like(dw_ref)
            db_ref[...] = jnp.zeros_like(db_ref)

        sc = se[0:CL]
        for k in range(KA):
            dw_ref[k:k + 1, :] += jnp.sum(sc * lag[KA - 1 - k][0:CL], axis=0, keepdims=True)
        db_ref[...] += jnp.sum(sc, axis=0, keepdims=True)

    prev, cur, nxt = _halo_specs(n, lambda j: j + off)
    _, dcur, dnxt = _halo_specs(n, lambda j: j)
    wspec = pl.BlockSpec((KA, CT), lambda j, i: (0, j + woff))
    bspec = pl.BlockSpec((1, CT), lambda j, i: (0, j + woff))
    return pl.pallas_call(
        body, name=name, grid=(width // CT, nl),
        in_specs=[prev, cur, nxt, dcur, dnxt, wspec, bspec],
        out_specs=[pl.BlockSpec((CL, CT), lambda j, i: (i, j)), pl.BlockSpec((KA, CT), lambda j, i: (0, j)),
                   pl.BlockSpec((1, CT), lambda j, i: (0, j))],
        out_shape=[jax.ShapeDtypeStruct((n, width), BF16), jax.ShapeDtypeStruct((KA, width), F32),
                   jax.ShapeDtypeStruct((1, width), F32)],
        compiler_params=_cp("parallel", "arbitrary"),
    )(proj, proj, proj, dout, dout, w, b)


def _conv_ffn_fwd(up, w, b):
    n = up.shape[0]
    nb = DFF // CT

    def body(pg_ref, g_ref, pv_ref, v_ref, wg_ref, bg_ref, wv_ref, bv_ref, o_ref):
        inner = pl.program_id(1) > 0
        gf = jnp.concatenate([jnp.where(inner, pg_ref[...], 0.0), g_ref[...]], axis=0)
        vf = jnp.concatenate([jnp.where(inner, pv_ref[...], 0.0), v_ref[...]], axis=0)
        gc = _conv_pre(_lagged(gf, KF, CL), wg_ref, bg_ref, KF)
        vc = _conv_pre(_lagged(vf, KF, CL), wv_ref, bv_ref, KF)
        o_ref[...] = (_silu(gc) * vc).astype(BF16)

    gp, gcur, _ = _halo_specs(n, lambda j: j)
    vp, vcur, _ = _halo_specs(n, lambda j: j + nb)
    return pl.pallas_call(
        body, name="conv_ffn_fwd", grid=(nb, n // CL),
        in_specs=[gp, gcur, vp, vcur,
                  pl.BlockSpec((KF, CT), lambda j, i: (0, j)), pl.BlockSpec((1, CT), lambda j, i: (0, j)),
                  pl.BlockSpec((KF, CT), lambda j, i: (0, j + nb)), pl.BlockSpec((1, CT), lambda j, i: (0, j + nb))],
        out_specs=pl.BlockSpec((CL, CT), lambda j, i: (i, j)),
        out_shape=jax.ShapeDtypeStruct((n, DFF), BF16), compiler_params=_cp("parallel", "parallel"),
    )(up, up, up, up, w, b, w, b)


def _conv_ffn_bwd(up, dact, w, b):
    n = up.shape[0]
    nb = DFF // CT
    nl = n // CL

    def body(pg_ref, g_ref, ng_ref, pv_ref, v_ref, nv_ref, d_ref, dn_ref, wg_ref, bg_ref, wv_ref, bv_ref,
             dxg_ref, dxv_ref, dwg_ref, dwv_ref, dbg_ref, dbv_ref):
        i = pl.program_id(1)
        gf = jnp.concatenate([jnp.where(i > 0, pg_ref[...], 0.0), g_ref[...], ng_ref[...]], axis=0)
        vf = jnp.concatenate([jnp.where(i > 0, pv_ref[...], 0.0), v_ref[...], nv_ref[...]], axis=0)
        glag, vlag = _lagged(gf, KF, CL + 8), _lagged(vf, KF, CL + 8)
        de = jnp.concatenate([d_ref[...], jnp.where(i < nl - 1, dn_ref[...], 0.0)], axis=0).astype(F32)
        gc = _conv_pre(glag, wg_ref, bg_ref, KF)
        vc = _conv_pre(vlag, wv_ref, bv_ref, KF)
        sg = _sig(gc)
        dgc = de * vc * (sg * (1.0 + gc * (1.0 - sg)))
        dvc = de * (gc * sg)
        dxg_ref[...] = _conv_back(dgc, wg_ref, KF).astype(BF16)
        dxv_ref[...] = _conv_back(dvc, wv_ref, KF).astype(BF16)

        @pl.when(i == 0)
        def _():
            for r in (dwg_ref, dwv_ref, dbg_ref, dbv_ref):
                r[...] = jnp.zeros_like(r)

        for e, lag, dw_ref, db_ref in ((dgc, glag, dwg_ref, dbg_ref), (dvc, vlag, dwv_ref, dbv_ref)):
            ec = e[0:CL]
            for k in range(KF):
                dw_ref[k:k + 1, :] += jnp.sum(ec * lag[KF - 1 - k][0:CL], axis=0, keepdims=True)
            db_ref[...] += jnp.sum(ec, axis=0, keepdims=True)

    gp, gcur, gnx = _halo_specs(n, lambda j: j)
    vp, vcur, vnx = _halo_specs(n, lambda j: j + nb)
    wcol = lambda o: (pl.BlockSpec((KF, CT), lambda j, i: (0, j + o)), pl.BlockSpec((1, CT), lambda j, i: (0, j + o)))
    wg, bg = wcol(0)
    wv, bv = wcol(nb)
    dxs = pl.BlockSpec((CL, CT), lambda j, i: (i, j))
    outs = pl.pallas_call(
        body, name="conv_ffn_bwd", grid=(nb, nl),
        in_specs=[gp, gcur, gnx, vp, vcur, vnx, gcur, gnx, wg, bg, wv, bv],
        out_specs=[dxs, dxs, wg, wg, bg, bg],
        out_shape=[jax.ShapeDtypeStruct((n, DFF), BF16)] * 2 + [jax.ShapeDtypeStruct((KF, DFF), F32)] * 2
        + [jax.ShapeDtypeStruct((1, DFF), F32)] * 2,
        compiler_params=_cp("parallel", "arbitrary"),
    )(up, up, up, up, up, up, dact, dact, w, b, w, b)
    return [jnp.concatenate(outs[k:k + 2], axis=1) for k in (0, 2, 4)]


def _ssd_common(dtc_ref, dtr_ref, hpc_ref, hpr_ref, e8_ref):
    row = lax.broadcasted_iota(jnp.int32, (Q, Q), 0)
    col = lax.broadcasted_iota(jnp.int32, (Q, Q), 1)
    lower = row >= col
    upper = row <= col
    hpc, hpr = hpc_ref[...], hpr_ref[...]
    pre_c = dtc_ref[...] + hpc[0:1, :]
    dt_c = _softplus(pre_c)
    a_c = -jnp.exp(hpc[1:2, :])
    s_c = _dot(lower.astype(F32), dt_c * a_c, prec=HI)
    dt_r = _softplus(dtr_ref[...] + hpr[:, 0:1])
    s_r = _dot(dt_r * (-jnp.exp(hpr[:, 1:2])), upper.astype(F32), prec=HI)
    e8 = e8_ref[...]
    dt_e = _dot(dt_c, e8, prec=HI)
    s_e = _dot(s_c, e8, prec=HI)
    return lower, upper, pre_c, dt_c, a_c, s_c, s_r, dt_e, s_e


def _ssd_specs(nc, rev):
    cc = (lambda c: nc - 1 - c) if rev else (lambda c: c)
    return [
        pl.BlockSpec((Q, GW), lambda g, c: (cc(c), g)),
        pl.BlockSpec((Q, NS), lambda g, c: (cc(c), DI // NS + g)),
        pl.BlockSpec((Q, NS), lambda g, c: (cc(c), (DI + NG * NS) // NS + g)),
        pl.BlockSpec((None, Q, 8), lambda g, c: (g, cc(c), 0)),
        pl.BlockSpec((None, 8, Q), lambda g, c: (g, 0, cc(c))),
        pl.BlockSpec((None, 8, 8), lambda g, c: (g, 0, 0)),
        pl.BlockSpec((None, 8, 128), lambda g, c: (g, 0, 0)),
        pl.BlockSpec((1, GW), lambda g, c: (0, g)),
        pl.BlockSpec((8, GW), lambda g, c: (0, 0)),
    ]


def _ssd_fwd(xbc, dtc, dtr, hpc, hpr, dexp, e8):
    n = xbc.shape[0]
    nc = n // Q

    def body(xs_ref, b_ref, c_ref, dtc_ref, dtr_ref, hpc_ref, hpr_ref, dexp_ref, e8_ref, y_ref, sp_ref, st):
        @pl.when(pl.program_id(1) == 0)
        def _():
            st[...] = jnp.zeros_like(st)

        lower, _, _, _, _, s_c, s_r, dt_e, s_e = _ssd_common(dtc_ref, dtr_ref, hpc_ref, hpr_ref, e8_ref)
        xs = xs_ref[...]
        x = xs * dt_e
        xb = x.astype(BF16)
        bb, cb = b_ref[...].astype(BF16), c_ref[...].astype(BF16)
        cbm = _dot(cb, bb, NT)
        st_e = s_e[Q - 1:Q, :]
        sprev = st[...]
        sp_ref[...] = sprev
        yoff = _dot(cb, sprev.astype(BF16)) * jnp.exp(s_e) + dexp_ref[...] * xs
        for h in range(HPG):
            sl = slice(h * HD, (h + 1) * HD)
            lm = jnp.where(lower, jnp.exp(jnp.minimum(s_c[:, h:h + 1] - s_r[h:h + 1, :], 0.0)), 0.0)
            y_ref[:, sl] = _dot((cbm * lm).astype(BF16), xb[:, sl]) + yoff[:, sl]
        w = (x * jnp.exp(st_e - s_e)).astype(BF16)
        st[...] = jnp.exp(st_e) * sprev + _dot(bb, w, TN)

    return pl.pallas_call(
        body, name="ssd_fwd", grid=(NG, nc), in_specs=_ssd_specs(nc, False),
        out_specs=[pl.BlockSpec((Q, GW), lambda g, c: (c, g)),
                   pl.BlockSpec((None, None, NS, GW), lambda g, c: (c, g, 0, 0))],
        out_shape=[jax.ShapeDtypeStruct((n, DI), F32), jax.ShapeDtypeStruct((nc, NG, NS, GW), F32)],
        scratch_shapes=[pltpu.VMEM((NS, GW), F32)],
        compiler_params=_cp("parallel", "arbitrary"),
    )(xbc, xbc, xbc, dtc, dtr, hpc, hpr, dexp, e8)


def _ssd_bwd(xbc, dtc, dtr, hpc, hpr, dexp, e8, e8t, sprev_all, dy):
    n = xbc.shape[0]
    nc = n // Q
    rc = lambda c: nc - 1 - c

    def body(xs_ref, b_ref, c_ref, dtc_ref, dtr_ref, hpc_ref, hpr_ref, dexp_ref, e8_ref, e8t_ref, sp_ref, dy_ref,
             dxs_ref, db_ref, dc_ref, draw_ref, pd_ref, ps_ref, dst, dxbuf):
        @pl.when(pl.program_id(1) == 0)
        def _():
            dst[...] = jnp.zeros_like(dst)
            pd_ref[...] = jnp.zeros_like(pd_ref)
            ps_ref[...] = jnp.zeros_like(ps_ref)

        lower, upper, pre_c, dt_c, a_c, s_c, s_r, dt_e, s_e = _ssd_common(dtc_ref, dtr_ref, hpc_ref, hpr_ref, e8_ref)
        e8t = e8t_ref[...]
        xs = xs_ref[...]
        x = xs * dt_e
        xb = x.astype(BF16)
        bb, cb = b_ref[...].astype(BF16), c_ref[...].astype(BF16)
        cbm = _dot(cb, bb, NT)
        cbt = _dot(bb, cb, NT)
        st_e = s_e[Q - 1:Q, :]
        dec_out, dec_st, e_t = jnp.exp(s_e), jnp.exp(st_e - s_e), jnp.exp(st_e)
        dyv = dy_ref[...]
        dyb = dyv.astype(BF16)
        sprev = sp_ref[...]
        sb = sprev.astype(BF16)
        ds_in = dst[...]
        dsb = ds_in.astype(BF16)

        cs = _dot(cb, sb)
        dcs = (dyv * dec_out).astype(BF16)
        d_c = _dot(dcs, sb, NT)
        wf = x * dec_st
        d_w = _dot(bb, dsb)
        d_b = _dot(wf.astype(BF16), dsb, NT)
        tw = d_w * wf
        ds_c = _dot(dyv * cs * dec_out - tw, e8t, prec=HI)
        dcb = jnp.zeros((Q, Q), F32)
        dcbt = jnp.zeros((Q, Q), F32)
        lane8 = lax.broadcasted_iota(jnp.int32, (1, 8), 1)
        for h in range(HPG):
            sl = slice(h * HD, (h + 1) * HD)
            sc_h, sr_h = s_c[:, h:h + 1], s_r[h:h + 1, :]
            lm = jnp.where(lower, jnp.exp(jnp.minimum(sc_h - sr_h, 0.0)), 0.0)
            lmt = jnp.where(upper, jnp.exp(jnp.minimum(sr_h - sc_h, 0.0)), 0.0)
            mt = cbt * lmt
            dm = _dot(dyb[:, sl], xb[:, sl], NT)
            dmt = _dot(xb[:, sl], dyb[:, sl], NT)
            dxbuf[:, sl] = _dot(mt.astype(BF16), dyb[:, sl])
            dml = dm * lm
            dmlt = dmt * lmt
            dcb = dcb + dml
            dcbt = dcbt + dmlt
            dsh = jnp.sum(dml * cbm, axis=1, keepdims=True) - jnp.sum(dmlt * cbt, axis=1, keepdims=True)
            ds_c = ds_c + dsh * (lane8 == h).astype(F32)
        d_c = d_c + _dot(dcb.astype(BF16), bb)
        d_b = d_b + _dot(dcbt.astype(BF16), cb)
        dx = d_w * dec_st + dxbuf[...]
        tsum = jnp.sum(tw, axis=0, keepdims=True) + jnp.sum(ds_in * sprev, axis=0, keepdims=True) * e_t
        ds_t = _dot(jnp.broadcast_to(tsum, (8, GW)), e8t, prec=HI)[0:1, :]
        rows = lax.broadcasted_iota(jnp.int32, (Q, 8), 0)
        ds_c = ds_c + jnp.where(rows == Q - 1, ds_t, 0.0)
        d_a = _dot(upper.astype(F32), ds_c, prec=HI)
        ddt = _dot(dx * xs, e8t, prec=HI) + d_a * a_c
        draw = ddt * _sig(pre_c)
        draw_ref[...] = draw
        ps_ref[0:1, :] += jnp.sum(draw, axis=0, keepdims=True)
        ps_ref[1:2, :] += jnp.sum(d_a * dt_c, axis=0, keepdims=True) * a_c
        pd_ref[...] += jnp.sum(dyv * xs, axis=0, keepdims=True)
        dxs_ref[...] = dx * dt_e + dyv * dexp_ref[...]
        db_ref[...] = d_b
        dc_ref[...] = d_c
        dst[...] = e_t * ds_in + _dot(cb, dcs, TN)

    in_specs = _ssd_specs(nc, True) + [
        pl.BlockSpec((GW, 8), lambda g, c: (0, 0)),
        pl.BlockSpec((None, None, NS, GW), lambda g, c: (rc(c), g, 0, 0)),
        pl.BlockSpec((Q, GW), lambda g, c: (rc(c), g)),
    ]
    return pl.pallas_call(
        body, name="ssd_bwd", grid=(NG, nc), in_specs=in_specs,
        out_specs=[pl.BlockSpec((Q, GW), lambda g, c: (rc(c), g)),
                   pl.BlockSpec((Q, NS), lambda g, c: (rc(c), g)),
                   pl.BlockSpec((Q, NS), lambda g, c: (rc(c), g)),
                   pl.BlockSpec((None, Q, 8), lambda g, c: (g, rc(c), 0)),
                   pl.BlockSpec((None, 1, GW), lambda g, c: (g, 0, 0)),
                   pl.BlockSpec((None, 8, 8), lambda g, c: (g, 0, 0))],
        out_shape=[jax.ShapeDtypeStruct((n, DI), F32), jax.ShapeDtypeStruct((n, NG * NS), F32),
                   jax.ShapeDtypeStruct((n, NG * NS), F32), jax.ShapeDtypeStruct((NG, n, 8), F32),
                   jax.ShapeDtypeStruct((NG, 1, GW), F32), jax.ShapeDtypeStruct((NG, 8, 8), F32)],
        scratch_shapes=[pltpu.VMEM((NS, GW), F32), pltpu.VMEM((Q, GW), F32)],
        compiler_params=_cp("parallel", "arbitrary"),
    )(xbc, xbc, xbc, dtc, dtr, hpc, hpr, dexp, e8, e8t, sprev_all, dy)


GL = 128


def _gnorm_fwd(y, proj, w):
    n = y.shape[0]

    def body(y_ref, z_ref, w_ref, o_ref):
        for g in range(NG):
            sl = slice(g * GW, (g + 1) * GW)
            yz = y_ref[:, sl] * _silu(z_ref[:, sl])
            r = lax.rsqrt(jnp.mean(yz * yz, axis=-1, keepdims=True) + EPS)
            o_ref[:, sl] = (yz * r * w_ref[:, sl]).astype(BF16)

    row = pl.BlockSpec((GL, DI), lambda i: (i, 0))
    return pl.pallas_call(
        body, name="gnorm_fwd", grid=(n // GL,),
        in_specs=[row, row, pl.BlockSpec((1, DI), lambda i: (0, 0))], out_specs=row,
        out_shape=jax.ShapeDtypeStruct((n, DI), BF16), compiler_params=_cp("parallel"),
    )(y, proj, w)


def _gnorm_bwd(dyn, y, proj, w):
    n = y.shape[0]

    def body(d_ref, y_ref, z_ref, w_ref, dy_ref, dz_ref, gw_ref):
        @pl.when(pl.program_id(0) == 0)
        def _():
            gw_ref[...] = jnp.zeros_like(gw_ref)

        for g in range(NG):
            sl = slice(g * GW, (g + 1) * GW)
            yv, zv, dv = y_ref[:, sl], z_ref[:, sl], d_ref[:, sl]
            sz = _silu(zv)
            yz = yv * sz
            r = lax.rsqrt(jnp.mean(yz * yz, axis=-1, keepdims=True) + EPS)
            yh = yz * r
            gg = dv * w_ref[:, sl]
            dyz = r * (gg - yh * jnp.mean(gg * yh, axis=-1, keepdims=True))
            gw_ref[:, sl] += jnp.sum(dv * yh, axis=0, keepdims=True)
            dy_ref[:, sl] = dyz * sz
            dz_ref[:, sl] = (dyz * yv * _dsilu(zv)).astype(BF16)

    row = pl.BlockSpec((GL, DI), lambda i: (i, 0))
    vec = pl.BlockSpec((1, DI), lambda i: (0, 0))
    return pl.pallas_call(
        body, name="gnorm_bwd", grid=(n // GL,),
        in_specs=[row, row, row, vec], out_specs=[row, row, vec],
        out_shape=[jax.ShapeDtypeStruct((n, DI), F32), jax.ShapeDtypeStruct((n, DI), BF16),
                   jax.ShapeDtypeStruct((1, DI), F32)],
        compiler_params=_cp("arbitrary"),
    )(dyn, y, proj, w)


SL = 512
SB = 8
SCB = NCH // SB


def _s5_in(proj, bre, bim):
    n = proj.shape[0]
    uoff = OFF_U // 128

    def body(u_ref, br_ref, bi_ref, or_ref, oi_ref):
        u = u_ref[...].astype(BF16)
        or_ref[...] = _dot(u, br_ref[...])
        oi_ref[...] = _dot(u, bi_ref[...])

    blk = pl.BlockSpec((None, 128, SCB), lambda i, j: (j, 0, 0))
    out = pl.BlockSpec((SL, SCB), lambda i, j: (i, j))
    return pl.pallas_call(
        body, name="s5_in", grid=(n // SL, SB),
        in_specs=[pl.BlockSpec((SL, 128), lambda i, j: (i, uoff + j)), blk, blk], out_specs=[out, out],
        out_shape=[jax.ShapeDtypeStruct((n, NCH), F32)] * 2, compiler_params=_cp("parallel", "parallel"),
    )(proj, bre, bim)


SC = 256


def _s5_scan(vre, vim, tab, reverse, name):
    n = vre.shape[0]
    nl = n // SL
    ng = SL // 8
    ti = (lambda i: nl - 1 - i) if reverse else (lambda i: i)

    def body(re_ref, im_ref, tab_ref, ore_ref, oim_ref, cre, cim):
        @pl.when(pl.program_id(1) == 0)
        def _():
            cre[...] = jnp.zeros_like(cre)
            cim[...] = jnp.zeros_like(cim)

        def step(j, carry):
            cr, ci = carry
            jj = (ng - 1 - j) if reverse else j
            rows = pl.ds(pl.multiple_of(jj * 8, 8), 8)
            vr, vi = re_ref[rows, :], im_ref[rows, :]
            for t, k in enumerate((1, 2, 4)):
                sh = (8 - k) if reverse else k
                rr, ri = pltpu.roll(vr, sh, 0), pltpu.roll(vi, sh, 0)
                pr, pi = tab_ref[2 * t], tab_ref[2 * t + 1]
                vr, vi = vr + pr * rr - pi * ri, vi + pr * ri + pi * rr
            lr, li = tab_ref[6], tab_ref[7]
            vr, vi = vr + lr * cr - li * ci, vi + lr * ci + li * cr
            ore_ref[rows, :] = vr
            oim_ref[rows, :] = vi
            e = 0 if reverse else 7
            return (jnp.broadcast_to(vr[e:e + 1, :], (8, SC)), jnp.broadcast_to(vi[e:e + 1, :], (8, SC)))

        cr, ci = lax.fori_loop(0, ng, step, (cre[...], cim[...]))
        cre[...] = cr
        cim[...] = ci

    blk = pl.BlockSpec((SL, SC), lambda j, i: (ti(i), j))
    return pl.pallas_call(
        body, name=name, grid=(NCH // SC, nl),
        in_specs=[blk, blk, pl.BlockSpec((8, 8, SC), lambda j, i: (0, 0, j))], out_specs=[blk, blk],
        out_shape=[jax.ShapeDtypeStruct((n, NCH), F32)] * 2,
        scratch_shapes=[pltpu.VMEM((8, SC), F32), pltpu.VMEM((8, SC), F32)],
        compiler_params=_cp("parallel", "arbitrary"),
    )(vre, vim, tab)


def _s5_out(xre, xim, cre, cimn, proj, dvec):
    n = xre.shape[0]
    uoff = OFF_U // 128

    def body(xr_ref, xi_ref, cr_ref, ci_ref, u_ref, d_ref, y_ref, g_ref):
        y = (_dot(xr_ref[...].astype(BF16), cr_ref[...]) + _dot(xi_ref[...].astype(BF16), ci_ref[...])
             + d_ref[...] * u_ref[...])
        y_ref[...] = y
        g_ref[...] = _gelu(y).astype(BF16)

    xs = pl.BlockSpec((SL, SCB), lambda i, j: (i, j))
    blk = pl.BlockSpec((None, SCB, 128), lambda i, j: (j, 0, 0))
    out = pl.BlockSpec((SL, 128), lambda i, j: (i, j))
    return pl.pallas_call(
        body, name="s5_out", grid=(n // SL, SB),
        in_specs=[xs, xs, blk, blk, pl.BlockSpec((SL, 128), lambda i, j: (i, uoff + j)),
                  pl.BlockSpec((1, 128), lambda i, j: (0, j))],
        out_specs=[out, out],
        out_shape=[jax.ShapeDtypeStruct((n, DS5), F32), jax.ShapeDtypeStruct((n, DS5), BF16)],
        compiler_params=_cp("parallel", "parallel"),
    )(xre, xim, cre, cimn, proj, dvec)


def _s5_out_bwd(dg, ypre, crt, cimnt, proj, dvec, xre, xim):
    n = dg.shape[0]
    uoff = OFF_U // 128
    nl = n // SL

    def body(dg_ref, y_ref, cr_ref, ci_ref, u_ref, d_ref, xr_ref, xi_ref,
             gr_ref, gi_ref, dus_ref, gcr_ref, gci_ref, gd_ref):
        dy = dg_ref[...] * _dgelu(y_ref[...])
        dyb = dy.astype(BF16)
        gr_ref[...] = _dot(dyb, cr_ref[...])
        gi_ref[...] = _dot(dyb, ci_ref[...])
        dus_ref[...] = dy * d_ref[...]

        @pl.when(pl.program_id(1) == 0)
        def _():
            gcr_ref[...] = jnp.zeros_like(gcr_ref)
            gci_ref[...] = jnp.zeros_like(gci_ref)
            gd_ref[...] = jnp.zeros_like(gd_ref)

        gcr_ref[...] += _dot(xr_ref[...].astype(BF16), dyb, TN)
        gci_ref[...] -= _dot(xi_ref[...].astype(BF16), dyb, TN)
        gd_ref[...] += jnp.sum(dy * u_ref[...], axis=0, keepdims=True)

    u128 = pl.BlockSpec((SL, 128), lambda j, i: (i, j))
    xs = pl.BlockSpec((SL, SCB), lambda j, i: (i, j))
    blk = pl.BlockSpec((None, 128, SCB), lambda j, i: (j, 0, 0))
    gblk = pl.BlockSpec((None, SCB, 128), lambda j, i: (j, 0, 0))
    vec = pl.BlockSpec((1, 128), lambda j, i: (0, j))
    return pl.pallas_call(
        body, name="s5_out_bwd", grid=(SB, nl),
        in_specs=[u128, u128, blk, blk, pl.BlockSpec((SL, 128), lambda j, i: (i, uoff + j)), vec, xs, xs],
        out_specs=[xs, xs, u128, gblk, gblk, vec],
        out_shape=[jax.ShapeDtypeStruct((n, NCH), F32)] * 2 + [jax.ShapeDtypeStruct((n, DS5), F32)]
        + [jax.ShapeDtypeStruct((SB, SCB, 128), F32)] * 2 + [jax.ShapeDtypeStruct((1, DS5), F32)],
        compiler_params=_cp("parallel", "arbitrary"),
    )(dg, ypre, crt, cimnt, proj, dvec, xre, xim)


def _s5_in_bwd(are, aim, brt, bit, proj, dus, xre, xim):
    n = are.shape[0]
    uoff = OFF_U // 128
    per = SL // 8

    def body(ar_ref, ai_ref, br_ref, bi_ref, u_ref, dus_ref, xr_ref, xi_ref, pr_ref, pi_ref,
             du_ref, gbr_ref, gbi_ref, glr_ref, gli_ref):
        i = pl.program_id(1)
        ar, ai = ar_ref[...], ai_ref[...]
        arb, aib = ar.astype(BF16), ai.astype(BF16)
        du_ref[...] = (_dot(arb, br_ref[...]) + _dot(aib, bi_ref[...]) + dus_ref[...]).astype(BF16)

        @pl.when(i == 0)
        def _():
            for r in (gbr_ref, gbi_ref, glr_ref, gli_ref):
                r[...] = jnp.zeros_like(r)

        ub = u_ref[...].astype(BF16)
        gbr_ref[...] += _dot(arb, ub, TN)
        gbi_ref[...] += _dot(aib, ub, TN)
        row0 = lax.broadcasted_iota(jnp.int32, (SL, SCB), 0) == 0
        last_r = jnp.where(i > 0, pr_ref[7:8, :], 0.0)
        last_i = jnp.where(i > 0, pi_ref[7:8, :], 0.0)
        xpr = jnp.where(row0, last_r, pltpu.roll(xr_ref[...], 1, 0))
        xpi = jnp.where(row0, last_i, pltpu.roll(xi_ref[...], 1, 0))
        glr_ref[...] += jnp.sum(ar * xpr + ai * xpi, axis=0, keepdims=True)
        gli_ref[...] += jnp.sum(ai * xpr - ar * xpi, axis=0, keepdims=True)

    xs = pl.BlockSpec((SL, SCB), lambda j, i: (i, j))
    prev = pl.BlockSpec((8, SCB), lambda j, i: (jnp.maximum(i * per - 1, 0), j))
    blk = pl.BlockSpec((None, SCB, 128), lambda j, i: (j, 0, 0))
    u128 = pl.BlockSpec((SL, 128), lambda j, i: (i, j))
    vec = pl.BlockSpec((1, SCB), lambda j, i: (0, j))
    return pl.pallas_call(
        body, name="s5_in_bwd", grid=(SB, n // SL),
        in_specs=[xs, xs, blk, blk, pl.BlockSpec((SL, 128), lambda j, i: (i, uoff + j)), u128, xs, xs, prev, prev],
        out_specs=[u128, blk, blk, vec, vec],
        out_shape=[jax.ShapeDtypeStruct((n, DS5), BF16)] + [jax.ShapeDtypeStruct((SB, SCB, 128), F32)] * 2
        + [jax.ShapeDtypeStruct((1, NCH), F32)] * 2,
        compiler_params=_cp("parallel", "arbitrary"),
    )(are, aim, brt, bit, proj, dus, xre, xim, xre, xim)


MC = 1024


def _merge_specs():
    ga = pl.BlockSpec((TL, MC), lambda i, j: (i, OFF_GA // MC + j))
    gb = pl.BlockSpec((TL, MC), lambda i, j: (i, OFF_GB // MC + j))
    col = pl.BlockSpec((TL, MC), lambda i, j: (i, j))
    gate = pl.BlockSpec((TL, MC), lambda i, j: (i, D // MC + j))
    return ga, gb, col, gate


def _merge_fwd(proj, ya, vg):
    n = ya.shape[0]

    def body(ga_ref, gb_ref, ya_ref, v_ref, g_ref, o_ref):
        yb = v_ref[...] * _sig(g_ref[...])
        o_ref[...] = (_sig(ga_ref[...]) * ya_ref[...] + _sig(gb_ref[...]) * yb).astype(BF16)

    ga, gb, col, gate = _merge_specs()
    return pl.pallas_call(
        body, name="merge_fwd", grid=(n // TL, D // MC), in_specs=[ga, gb, col, col, gate], out_specs=col,
        out_shape=jax.ShapeDtypeStruct((n, D), BF16), compiler_params=_cp("parallel", "parallel"),
    )(proj, proj, ya, vg, vg)


def _merge_bwd(dm, proj, ya, vg):
    n = ya.shape[0]

    def body(dm_ref, ga_ref, gb_ref, ya_ref, v_ref, g_ref, dga_ref, dgb_ref, dya_ref, dv_ref, dg_ref):
        d = dm_ref[...]
        sa, sb, sg = _sig(ga_ref[...]), _sig(gb_ref[...]), _sig(g_ref[...])
        v = v_ref[...]
        yb = v * sg
        dga_ref[...] = (d * ya_ref[...] * sa * (1.0 - sa)).astype(BF16)
        dgb_ref[...] = (d * yb * sb * (1.0 - sb)).astype(BF16)
        dya_ref[...] = (d * sa).astype(BF16)
        dyb = d * sb
        dv_ref[...] = (dyb * sg).astype(BF16)
        dg_ref[...] = (dyb * v * sg * (1.0 - sg)).astype(BF16)

    ga, gb, col, gate = _merge_specs()
    o = jax.ShapeDtypeStruct((n, D), BF16)
    return pl.pallas_call(
        body, name="merge_bwd", grid=(n // TL, D // MC), in_specs=[col, ga, gb, col, col, gate],
        out_specs=[col] * 5, out_shape=[o] * 5, compiler_params=_cp("parallel", "parallel"),
    )(dm, proj, proj, ya, vg, vg)


def _adamw_update(wv, gv, mv, vv):
    nm = B1 * mv + (1.0 - B1) * gv
    nv = B2 * vv + (1.0 - B2) * (gv * gv)
    m_hat = nm / (1.0 - B1 ** STEP)
    v_hat = nv / (1.0 - B2 ** STEP)
    return -LR * (m_hat / (jnp.sqrt(v_hat) + AEPS) + WD * wv), nm, nv


def _adamw(w, g, m, v, name):
    r, c = w.shape
    tr = _pick(r, 128)

    def body(w_ref, g_ref, m_ref, v_ref, d_ref, nm_ref, nv_ref):
        d_ref[...], nm_ref[...], nv_ref[...] = _adamw_update(w_ref[...], g_ref[...], m_ref[...], v_ref[...])

    blk = pl.BlockSpec((tr, c), lambda i: (i, 0))
    o = jax.ShapeDtypeStruct((r, c), F32)
    return pl.pallas_call(
        body, name=name, grid=(r // tr,), in_specs=[blk] * 4, out_specs=[blk] * 3, out_shape=[o] * 3,
        compiler_params=_cp("parallel"),
    )(w, g, m, v)


def _adamw_halves(w, g_mine, g_other, m, v, cidx, name):
    _, r, c = w.shape
    hr, gc = g_mine.shape
    tr = _pick(hr, 128)
    nbh = hr // tr
    assert gc == c and 2 * hr - tr < r <= 2 * hr

    def body(cs, w_ref, gm_ref, go_ref, m_ref, v_ref, g_ref, d_ref, nm_ref, nv_ref):
        mine = pl.program_id(0) // nbh == cs[0]
        gv = jnp.where(mine, gm_ref[...], go_ref[...])
        g_ref[...] = gv
        d_ref[...], nm_ref[...], nv_ref[...] = _adamw_update(w_ref[...], gv, m_ref[...], v_ref[...])

    blk = pl.BlockSpec((None, tr, c), lambda i, cs: (0, i, 0))
    gblk = pl.BlockSpec((tr, gc), lambda i, cs: (i % nbh, 0))
    o = jax.ShapeDtypeStruct((1, r, c), F32)
    return pl.pallas_call(
        body, name=name,
        grid_spec=pltpu.PrefetchScalarGridSpec(num_scalar_prefetch=1, grid=(2 * nbh,),
                                               in_specs=[blk, gblk, gblk, blk, blk], out_specs=[blk] * 4),
        out_shape=[o] * 4, compiler_params=_cp("parallel"),
    )(cidx, w, g_mine, g_other, m, v)


def _chip_sum(part, sib, cidx, name):
    _, r, cc = part.shape
    hr = r // 2
    tr = _pick(hr, 256)

    def body(cs, p_ref, s_ref, o_ref):
        o_ref[...] = (p_ref[...].astype(F32) + s_ref[...].astype(F32)).astype(BF16)

    blk = pl.BlockSpec((None, tr, cc), lambda k, i, cs: (k, i, 0))
    return pl.pallas_call(
        body, name=name,
        grid_spec=pltpu.PrefetchScalarGridSpec(
            num_scalar_prefetch=1, grid=(4, hr // tr),
            in_specs=[pl.BlockSpec((None, None, tr, cc), lambda k, i, cs: (k, cs[0], i, 0)), blk], out_specs=blk),
        out_shape=jax.ShapeDtypeStruct((4, hr, cc), BF16), compiler_params=_cp("parallel", "parallel"),
    )(cidx, part.reshape(4, 2, hr, cc), sib)


def _shard_sum(own, got, sidx, name):
    _, hr, cc = own.shape
    tr = _pick(hr, 256)

    def body(cs, own_ref, g0, g1, g2, g3, o_ref):
        acc = None
        for k, g_ref in enumerate((g0, g1, g2, g3)):
            term = jnp.where(cs[0] == k, own_ref[...], g_ref[...]).astype(F32)
            acc = term if acc is None else acc + term
        o_ref[...] = acc

    def got_spec(k):
        return pl.BlockSpec((None, tr, cc), lambda i, cs: (jnp.where(cs[0] == k, (k + 1) % 4, k), i, 0))

    return pl.pallas_call(
        body, name=name,
        grid_spec=pltpu.PrefetchScalarGridSpec(
            num_scalar_prefetch=1, grid=(hr // tr,),
            in_specs=[pl.BlockSpec((None, tr, cc), lambda i, cs: (cs[0], i, 0))] + [got_spec(k) for k in range(4)],
            out_specs=pl.BlockSpec((tr, cc), lambda i, cs: (i, 0))),
        out_shape=jax.ShapeDtypeStruct((hr, cc), F32), compiler_params=_cp("parallel"),
    )(sidx, own, got, got, got, got)


def _sum_slabs(xs, name, out_dtype=F32):
    r, c = xs[0].shape
    tr = _pick(r, 256)

    def body(*refs):
        acc = refs[0][...].astype(F32)
        for ref in refs[1:-1]:
            acc = acc + ref[...].astype(F32)
        refs[-1][...] = acc.astype(out_dtype)

    blk = pl.BlockSpec((tr, c), lambda i: (i, 0))
    return pl.pallas_call(
        body, name=name, grid=(r // tr,), in_specs=[blk] * len(xs), out_specs=blk,
        out_shape=jax.ShapeDtypeStruct((r, c), out_dtype), compiler_params=_cp("parallel"),
    )(*xs)


def _place():
    return lax.axis_index("x"), lax.axis_index("y"), lax.axis_index("c")


def _gather_small(v):
    m_per, n = v.shape

    def body(x_ref, out_ref, send_sems, recv_sems, local_sem):
        x, y, c = _place()
        me, sibling = (x, y, c), (x, y, 1 - c)
        chips = [(1 - x, y), (x, 1 - y), (1 - x, 1 - y)]

        def rows(px, py, pc):
            return out_ref.at[pl.ds((4 * px + 2 * py + pc) * m_per, m_per), :]

        def copy(k, block, to, src=None):
            return pltpu.make_async_remote_copy(
                src_ref=rows(*block) if src is None else src, dst_ref=rows(*block),
                send_sem=send_sems.at[k], recv_sem=recv_sems.at[k], device_id=to, device_id_type=MESH)

        mine = pltpu.make_async_copy(x_ref, rows(*me), local_sem)
        mine.start()
        first = [copy(0, me, sibling, src=x_ref)]
        first += [copy(1 + j, me, (*chip, c), src=x_ref) for j, chip in enumerate(chips)]
        for cp in first:
            cp.start()
        passed = [copy(4 + j, (*chip, c), sibling) for j, chip in enumerate(chips)]
        for j, chip in enumerate(chips):
            copy(1 + j, (*chip, c), me).wait_recv()
            passed[j].start()
        copy(0, sibling, me).wait_recv()
        for j, chip in enumerate(chips):
            copy(4 + j, (*chip, 1 - c), me).wait_recv()
        for cp in first + passed:
            cp.wait_send()
        mine.wait()

    return pl.pallas_call(
        body, name="gather_small_%d" % m_per,
        out_shape=jax.ShapeDtypeStruct((8 * m_per, n), v.dtype),
        in_specs=[pl.BlockSpec(memory_space=pltpu.VMEM)], out_specs=pl.BlockSpec(memory_space=pltpu.VMEM),
        scratch_shapes=[pltpu.SemaphoreType.DMA((7,)), pltpu.SemaphoreType.DMA((7,)), pltpu.SemaphoreType.DMA],
        compiler_params=pltpu.CompilerParams(vmem_limit_bytes=VMEM_LIMIT),
    )(v)


def _allsum_small(v, name):
    r = v.shape[0]
    g = _gather_small(v)
    return _sum_slabs([g[k * r:(k + 1) * r] for k in range(8)], name)


def _gather_big(shards):
    nt = len(shards)

    def body(*refs):
        ins, outs = refs[:nt], refs[nt:2 * nt]
        send_sems, recv_sems = refs[2 * nt:]
        x, y, c = _place()
        s = 2 * x + y
        sibling = (x, y, 1 - c)
        chips = [(1 - x, y), (x, 1 - y), (1 - x, 1 - y)]

        def half(t, slot, h):
            hr = ins[t].shape[0] // 2
            return outs[t].at[slot, pl.ds(h * hr, hr), :]

        def ici(t, j, src_slot, to):
            hr = ins[t].shape[0] // 2
            return pltpu.make_async_remote_copy(
                src_ref=ins[t].at[pl.ds(c * hr, hr), :], dst_ref=half(t, src_slot, c),
                send_sem=send_sems.at[6 * t + j], recv_sem=recv_sems.at[6 * t + j], device_id=to, device_id_type=MESH)

        def d2d(t, j, slot, h):
            return pltpu.make_async_remote_copy(
                src_ref=half(t, slot, h), dst_ref=half(t, slot, h),
                send_sem=send_sems.at[6 * t + 3 + j], recv_sem=recv_sems.at[6 * t + 3 + j],
                device_id=sibling, device_id_type=MESH)

        sends = [ici(t, j, s, (*chip, c)) for t in range(nt) for j, chip in enumerate(chips)]
        for cp in sends:
            cp.start()
        passed = []
        for t in range(nt):
            for j, (px, py) in enumerate(chips):
                ici(t, j, 2 * px + py, (x, y, c)).wait_recv()
                cp = d2d(t, j, 2 * px + py, c)
                cp.start()
                passed.append(cp)
        for t in range(nt):
            for j, (px, py) in enumerate(chips):
                d2d(t, j, 2 * px + py, 1 - c).wait_recv()
        for cp in sends + passed:
            cp.wait_send()

    return pl.pallas_call(
        body, name="gather_big",
        out_shape=[jax.ShapeDtypeStruct((4,) + a.shape, a.dtype) for a in shards],
        in_specs=[ANY] * nt, out_specs=[ANY] * nt,
        scratch_shapes=[pltpu.SemaphoreType.DMA((6 * nt,)), pltpu.SemaphoreType.DMA((6 * nt,))],
    )(*shards)


def _swap_halves(parts, name):
    nt = len(parts)

    def body(*refs):
        ins, outs = refs[:nt], refs[nt:2 * nt]
        send_sems, recv_sems = refs[2 * nt:]
        x, y, c = _place()
        cps = []
        for t in range(nt):
            hr = ins[t].shape[1] // 2
            cps.append(pltpu.make_async_remote_copy(
                src_ref=ins[t].at[:, pl.ds((1 - c) * hr, hr), :], dst_ref=outs[t],
                send_sem=send_sems.at[t], recv_sem=recv_sems.at[t], device_id=(x, y, 1 - c), device_id_type=MESH))
        for cp in cps:
            cp.start()
        for cp in cps:
            cp.wait()

    return pl.pallas_call(
        body, name=name,
        out_shape=[jax.ShapeDtypeStruct((4, a.shape[1] // 2, a.shape[2]), a.dtype) for a in parts],
        in_specs=[ANY] * nt, out_specs=[ANY] * nt,
        scratch_shapes=[pltpu.SemaphoreType.DMA((nt,)), pltpu.SemaphoreType.DMA((nt,))],
    )(*parts)


def _scatter_chips(parts):
    nt = len(parts)

    def body(*refs):
        ins, outs = refs[:nt], refs[nt:2 * nt]
        send_sems, recv_sems = refs[2 * nt:]
        x, y, c = _place()
        s = 2 * x + y
        chips = [(1 - x, y), (x, 1 - y), (1 - x, 1 - y)]
        cps = []
        for t in range(nt):
            for j, (px, py) in enumerate(chips):
                cps.append(pltpu.make_async_remote_copy(
                    src_ref=ins[t].at[2 * px + py], dst_ref=outs[t].at[s],
                    send_sem=send_sems.at[3 * t + j], recv_sem=recv_sems.at[3 * t + j],
                    device_id=(px, py, c), device_id_type=MESH))
        for cp in cps:
            cp.start()
        for t in range(nt):
            for j, (px, py) in enumerate(chips):
                pltpu.make_async_remote_copy(
                    src_ref=ins[t].at[s], dst_ref=outs[t].at[2 * px + py],
                    send_sem=send_sems.at[3 * t + j], recv_sem=recv_sems.at[3 * t + j],
                    device_id=(px, py, c), device_id_type=MESH).wait_recv()
        for cp in cps:
            cp.wait_send()

    return pl.pallas_call(
        body, name="scatter_chips",
        out_shape=[jax.ShapeDtypeStruct(a.shape, a.dtype) for a in parts],
        in_specs=[ANY] * nt, out_specs=[ANY] * nt,
        scratch_shapes=[pltpu.SemaphoreType.DMA((3 * nt,)), pltpu.SemaphoreType.DMA((3 * nt,))],
    )(*parts)


def _swap_whole(halves):
    nt = len(halves)

    def body(*refs):
        ins, outs = refs[:nt], refs[nt:2 * nt]
        send_sems, recv_sems = refs[2 * nt:]
        x, y, c = _place()
        cps = [pltpu.make_async_remote_copy(
            src_ref=ins[t], dst_ref=outs[t], send_sem=send_sems.at[t], recv_sem=recv_sems.at[t],
            device_id=(x, y, 1 - c), device_id_type=MESH) for t in range(nt)]
        for cp in cps:
            cp.start()
        for cp in cps:
            cp.wait()

    return pl.pallas_call(
        body, name="swap_whole",
        out_shape=[jax.ShapeDtypeStruct(a.shape, a.dtype) for a in halves],
        in_specs=[ANY] * nt, out_specs=[ANY] * nt,
        scratch_shapes=[pltpu.SemaphoreType.DMA((nt,)), pltpu.SemaphoreType.DMA((nt,))],
    )(*halves)


def _pass_halves(got):
    nt = len(got)

    def body(*refs):
        ins, outs = refs[:nt], refs[nt:2 * nt]
        send_sems, recv_sems = refs[2 * nt:]
        x, y, c = _place()
        chips = [(1 - x, y), (x, 1 - y), (1 - x, 1 - y)]

        def half(ref, t, slot, h):
            hr = ins[t].shape[1] // 2
            return ref.at[slot, pl.ds(h * hr, hr), :]

        def copy(t, j, h):
            px, py = chips[j]
            return pltpu.make_async_remote_copy(
                src_ref=half(ins[t], t, 2 * px + py, h), dst_ref=half(outs[t], t, 2 * px + py, h),
                send_sem=send_sems.at[3 * t + j], recv_sem=recv_sems.at[3 * t + j],
                device_id=(x, y, 1 - c), device_id_type=MESH)

        sends = [copy(t, j, c) for t in range(nt) for j in range(3)]
        for cp in sends:
            cp.start()
        for t in range(nt):
            for j in range(3):
                copy(t, j, 1 - c).wait_recv()
        for cp in sends:
            cp.wait_send()

    return pl.pallas_call(
        body, name="pass_halves",
        out_shape=[jax.ShapeDtypeStruct(a.shape, a.dtype) for a in got],
        in_specs=[ANY] * nt, out_specs=[ANY] * nt, input_output_aliases={t: t for t in range(nt)},
        scratch_shapes=[pltpu.SemaphoreType.DMA((3 * nt,)), pltpu.SemaphoreType.DMA((3 * nt,))],
    )(*got)


HBM = pl.BlockSpec(memory_space=pltpu.HBM)
SEM = pl.BlockSpec(memory_space=pltpu.SEMAPHORE)
EFFECT = pltpu.SideEffectType.DATAFLOW_SIDE_EFFECTING


def _ici_copies(kind, srcs, lands, send_sems, recv_sems):
    x, y, c = _place()
    s = 2 * x + y
    cps = []
    for t in range(len(srcs)):
        for j, (px, py) in enumerate([(1 - x, y), (x, 1 - y), (1 - x, 1 - y)]):
            if kind == "gather":
                hr = srcs[t].shape[0] // 2
                src, dst = srcs[t].at[pl.ds(c * hr, hr), :], lands[t].at[s, pl.ds(c * hr, hr), :]
            else:
                src, dst = srcs[t].at[2 * px + py], lands[t].at[s]
            cps.append(pltpu.make_async_remote_copy(
                src_ref=src, dst_ref=dst, send_sem=send_sems[3 * t + j], recv_sem=recv_sems[3 * t + j],
                device_id=(px, py, c), device_id_type=MESH))
    return cps


def _ici_start(kind, srcs, after, name):
    nt = len(srcs)
    nc = 3 * nt
    lands = [(4,) + a.shape if kind == "gather" else a.shape for a in srcs]

    def body(*refs):
        src, land = refs[:nt], refs[nt:2 * nt]
        outs = refs[2 * nt + 1:]
        for cp in _ici_copies(kind, src, land, outs[:nc], outs[nc:2 * nc]):
            cp.start()
        outs[-1][...] = jnp.zeros_like(outs[-1])

    hbm = lambda a: pltpu.with_memory_space_constraint(a, pltpu.HBM)
    outs = pl.pallas_call(
        body, name=name,
        out_shape=tuple([pltpu.SemaphoreType.DMA(())] * (2 * nc) + [pltpu.HBM(a.shape, a.dtype) for a in srcs]
                        + [pltpu.HBM(sh, a.dtype) for sh, a in zip(lands, srcs)] + [jax.ShapeDtypeStruct((8, 128), F32)]),
        in_specs=[HBM] * (2 * nt) + [ANY],
        out_specs=tuple([SEM] * (2 * nc) + [HBM] * (2 * nt) + [pl.BlockSpec(memory_space=pltpu.VMEM)]),
        input_output_aliases={i: 2 * nc + i for i in range(2 * nt)},
        compiler_params=pltpu.CompilerParams(has_side_effects=EFFECT),
    )(*[hbm(a) for a in srcs], *[hbm(lax.empty(sh, a.dtype)) for sh, a in zip(lands, srcs)], after)
    return outs[:2 * nc], outs[2 * nc:2 * nc + nt], outs[2 * nc + nt:2 * nc + 2 * nt], outs[-1]


def _ici_wait(kind, sems, srcs, lands, after, name):
    nt = len(srcs)
    nc = 3 * nt

    def body(*refs):
        src, land = refs[:nt], refs[nt:2 * nt]
        sem = refs[2 * nt:2 * nt + 2 * nc]
        for cp in _ici_copies(kind, src, land, sem[:nc], sem[nc:]):
            cp.wait_send()
            cp.wait_recv()

    outs = pl.pallas_call(
        body, name=name,
        out_shape=tuple(pltpu.HBM(a.shape, a.dtype) for a in list(srcs) + list(lands)),
        in_specs=[HBM] * (2 * nt) + [SEM] * (2 * nc) + [ANY], out_specs=tuple([HBM] * (2 * nt)),
        input_output_aliases={i: i for i in range(2 * nt)},
        compiler_params=pltpu.CompilerParams(has_side_effects=EFFECT),
    )(*srcs, *lands, *sems, after)
    return outs[:nt], outs[nt:]


def _s5_params(lam_re, lam_im, log_dt, b_re, b_im):
    lr = jnp.minimum(lam_re, EIG_MAX)
    dt = jnp.exp(log_dt)[:, None]
    mag = jnp.exp(lr * dt)
    lbr, lbi = mag * jnp.cos(lam_im * dt), mag * jnp.sin(lam_im * dt)
    den = lr * lr + lam_im * lam_im
    qr = ((lbr - 1.0) * lr + lbi * lam_im) / den
    qi = (lbi * lr - (lbr - 1.0) * lam_im) / den
    bbr = qr[..., None] * b_re - qi[..., None] * b_im
    bbi = qr[..., None] * b_im + qi[..., None] * b_re
    return lbr, lbi, bbr, bbi


def _cmul(a, b):
    return a[0] * b[0] - a[1] * b[1], a[0] * b[1] + a[1] * b[0]


def _scan_table(lr, li, reverse):
    l1 = (lr.reshape(1, NCH), li.reshape(1, NCH))
    pows = [l1]
    for _ in range(7):
        pows.append(_cmul(pows[-1], l1))
    r = jnp.arange(8)[:, None]
    tabs = []
    for k in (1, 2, 4):
        keep = (r < 8 - k) if reverse else (r >= k)
        tabs += [jnp.where(keep, pows[k - 1][0], 0.0), jnp.where(keep, pows[k - 1][1], 0.0)]
    order = range(7, -1, -1) if reverse else range(8)
    tabs += [jnp.concatenate([pows[e][0] for e in order], axis=0), jnp.concatenate([pows[e][1] for e in order], axis=0)]
    return jnp.stack(tabs).astype(F32)


_EYE8 = lambda: jnp.eye(8, dtype=F32)


def _to_in_blocks(b):
    return jnp.einsum("jgpc,gh->jgchp", b.reshape(8, 8, 64, 16), _EYE8()).reshape(8, 128, 512)


def _to_out_blocks(cm):
    return jnp.einsum("jgcp,gh->jgphc", cm.reshape(8, 8, 16, 64), _EYE8()).reshape(8, 512, 128)


def _from_out_blocks(g):
    return jnp.einsum("jgphc,gh->jgpc", g.reshape(8, 8, 64, 8, 16), _EYE8()).reshape(64, 64, 16)


def _local_step(x, target, p, token, late_weights, send_grads):
    n = x.shape[0]
    g = {}
    hn1 = _rms_fwd(x, p["norm_mix_w"], "rms_mix", after=token)
    proj = _matmul(hn1, p["w_main"], "nt", "mm_in")
    dtraw = _matmul(hn1, p["w_dt"], "nt", "mm_dt")
    xbc = _conv_a_fwd(proj, p["conv_a_w"], p["conv_a_b"])
    dth = jnp.pad(dtraw[:, :NG * HPG].reshape(n, NG, HPG), ((0, 0), (0, 0), (0, 8 - HPG)))
    dtc = dth.transpose(1, 0, 2)
    dtr = dth.transpose(1, 2, 0)
    pad_h = lambda v: jnp.pad(v.reshape(NG, HPG), ((0, 0), (0, 8 - HPG)))
    hpc = jnp.zeros((NG, 8, 8), F32).at[:, 0, :].set(pad_h(p["dt_bias"])).at[:, 1, :].set(pad_h(p["a_log"]))
    hpr = jnp.zeros((NG, 8, 128), F32).at[:, :, 0].set(pad_h(p["dt_bias"])).at[:, :, 1].set(pad_h(p["a_log"]))
    dexp = jnp.repeat(p["d_a"].reshape(1, NG * HPG), HD, axis=1)
    e8 = (jnp.arange(8)[:, None] == (jnp.arange(GW)[None, :] // HD)).astype(F32)
    yssd, sprev = _ssd_fwd(xbc, dtc, dtr, hpc, hpr, dexp, e8)
    yn = _gnorm_fwd(yssd, proj, p["norm_a_w"])
    p = {**p, **late_weights(yn)}
    ya = _matmul(yn, p["w_proj_a"], "nn", "mm_proj")

    (lbr, lbi, bbr, bbi), s5_vjp = jax.vjp(_s5_params, p["s5_lam_re"], p["s5_lam_im"], p["s5_log_dt"],
                                           p["s5_b_re"], p["s5_b_im"])
    bin_r, bin_i = _to_in_blocks(bbr), _to_in_blocks(bbi)
    cout_r, cout_in = _to_out_blocks(p["s5_c_re"]), _to_out_blocks(-p["s5_c_im"])
    bur, bui = _s5_in(proj, bin_r.astype(BF16), bin_i.astype(BF16))
    xre, xim = _s5_scan(bur, bui, _scan_table(lbr, lbi, False), False, "s5_scan_fwd")
    ypre, g5 = _s5_out(xre, xim, cout_r.astype(BF16), cout_in.astype(BF16), proj, p["s5_d"])
    vg = _matmul(g5, p["w_s5_glu"], "nn", "mm_glu", b_stacked=True)
    merged = _merge_fwd(proj, ya, vg)
    h1 = _matmul(merged, p["w_out"], "nn", "mm_out", residual=x)
    hn2 = _rms_fwd(h1, p["norm_ffn_w"], "rms_ffn")
    up = _matmul(hn2, p["w_up"], "nn", "mm_up", tn=1408, b_stacked=True)
    act = _conv_ffn_fwd(up, p["conv_ffn_w"], p["conv_ffn_b"])
    h2 = _matmul(act, p["w_down"], "nn", "mm_down", tk=DFF // 2, residual=h1)
    dh2, dh2b, g["norm_final_w"], loss_blk = _final(h2, p["norm_final_w"], target)
    g["w_down"] = _matmul(act, dh2b, "tn", "mm_gw_down", out_dtype=BF16).reshape(4, DFF // 4, D)
    dact = _matmul(dh2b, p["w_down"], "nt", "mm_dact", out_dtype=BF16)
    dup, g["conv_ffn_w"], g["conv_ffn_b"] = _conv_ffn_bwd(up, dact, p["conv_ffn_w"], p["conv_ffn_b"])
    g["w_up"] = _matmul(hn2, dup, "tn", "mm_gw_up", out_dtype=BF16, tn=1408, out_stacked=True)
    tok = send_grads(["w_up", "w_down"], g, "s1")
    dhn2 = _matmul(dup, p["w_up"], "nt", "mm_dhn2", tk=2816, b_stacked=True, after=tok)
    dh1, dh1b, g["norm_ffn_w"] = _rms_bwd(dhn2, h1, p["norm_ffn_w"], dh2, "rms_ffn_bwd")
    g["w_out"] = _matmul(merged, dh1b, "tn", "mm_gw_out", out_dtype=BF16).reshape(4, D // 4, D)
    dmerged = _matmul(dh1b, p["w_out"], "nt", "mm_dmerged")
    dga, dgb, dya, dval, dgate = _merge_bwd(dmerged, proj, ya, vg)
    dvg = jnp.concatenate([dval, dgate], axis=1)
    g["w_s5_glu"] = _matmul(g5, dvg, "tn", "mm_gw_glu", out_dtype=BF16, out_stacked=True)
    dg5 = _matmul(dvg, p["w_s5_glu"], "nt", "mm_dg5", b_stacked=True)
    tr = lambda b: b.transpose(0, 2, 1)
    gxr, gxi, dus, gcr, gci, g["s5_d"] = _s5_out_bwd(dg5, ypre, tr(cout_r).astype(BF16), tr(cout_in).astype(BF16),
                                                     proj, p["s5_d"], xre, xim)
    are, aim = _s5_scan(gxr, gxi, _scan_table(lbr, -lbi, True), True, "s5_scan_bwd")
    du, gbr, gbi, glr, gli = _s5_in_bwd(are, aim, tr(bin_r).astype(BF16), tr(bin_i).astype(BF16), proj, dus, xre, xim)
    g["s5_c_re"] = _from_out_blocks(gcr).transpose(0, 2, 1)
    g["s5_c_im"] = _from_out_blocks(gci).transpose(0, 2, 1)
    (g["s5_lam_re"], g["s5_lam_im"], g["s5_log_dt"], g["s5_b_re"], g["s5_b_im"]) = s5_vjp(
        (glr.reshape(64, 64), gli.reshape(64, 64), _from_out_blocks(gbr), _from_out_blocks(gbi)))
    g["w_proj_a"] = _matmul(yn, dya, "tn", "mm_gw_proj", out_dtype=BF16).reshape(4, DI // 4, D)
    tok = send_grads(["w_proj_a", "w_s5_glu", "w_out"], g, "s2")
    dyn = _matmul(dya, p["w_proj_a"], "nt", "mm_dyn", after=tok)
    dyssd, dz, g["norm_a_w"] = _gnorm_bwd(dyn, yssd, proj, p["norm_a_w"])
    e8t = e8.T
    dxs, dbm, dcm, draw, pd, ps = _ssd_bwd(xbc, dtc, dtr, hpc, hpr, dexp, e8, e8t, sprev, dyssd)
    g["dt_bias"] = ps[:, 0, :HPG].reshape(1, NG * HPG)
    g["a_log"] = ps[:, 1, :HPG].reshape(1, NG * HPG)
    g["d_a"] = pd.reshape(NG * HPG, HD).sum(axis=1).reshape(1, NG * HPG)
    ddt = draw[:, :, :HPG].transpose(1, 0, 2).reshape(n, NG * HPG).astype(BF16)
    dxbc_parts, gcw, gcb = [], [], []
    for arr, col0, nm in ((dxs, 0, "conv_a_bwd_x"), (dbm, DI, "conv_a_bwd_b"), (dcm, DI + NG * NS, "conv_a_bwd_c")):
        dpart, gw_, gb_ = _conv_a_bwd(proj, arr, p["conv_a_w"], p["conv_a_b"], col0, nm)
        dxbc_parts.append(dpart)
        gcw.append(gw_)
        gcb.append(gb_)
    g["conv_a_w"] = jnp.concatenate(gcw, axis=1)
    g["conv_a_b"] = jnp.concatenate(gcb, axis=1)
    dorig = jnp.concatenate([dz] + dxbc_parts + [ddt, du, dga, dgb], axis=1)
    dproj = jnp.concatenate([jnp.pad(dorig[:, WSH * k:WSH * (k + 1)], ((0, 0), (0, WPAD - WSH))) for k in range(4)],
                            axis=1)
    g["w_in"] = _matmul(dproj, hn1, "tn", "mm_gw_in", out_dtype=BF16, tm=896, tn=2048).reshape(4, WPAD, D)
    tok = send_grads(["w_in"], g, "s3")
    dhn1 = _matmul(dproj, p["w_in"], "nn", "mm_dhn1", tk=2688, after=tok)
    gx, _, g["norm_mix_w"] = _rms_bwd(dhn1, x, p["norm_mix_w"], dh1, "rms_mix_bwd")
    return loss_blk, gx, g


BIG = ["w_in", "w_proj_a", "w_s5_glu", "w_out", "w_up", "w_down"]
SMALL = ["norm_mix_w", "conv_a_w", "conv_a_b", "dt_bias", "a_log", "d_a", "norm_a_w", "s5_lam_re", "s5_lam_im",
         "s5_log_dt", "s5_b_re", "s5_b_im", "s5_c_re", "s5_c_im", "s5_d", "norm_ffn_w", "conv_ffn_w", "conv_ffn_b",
         "norm_final_w"]
ORDER = ["norm_mix_w", "w_in", "conv_a_w", "conv_a_b", "dt_bias", "a_log", "d_a", "norm_a_w", "w_proj_a", "s5_lam_re",
         "s5_lam_im", "s5_log_dt", "s5_b_re", "s5_b_im", "s5_c_re", "s5_c_im", "s5_d", "w_s5_glu", "w_out",
         "norm_ffn_w", "w_up", "conv_ffn_w", "conv_ffn_b", "w_down", "norm_final_w"]
IN_SPLIT = [DI, DI + CONVD, DI + CONVD + NG * HPG]
CONV_FULL = {"conv_a_w": (KA, CONVD), "conv_ffn_w": (KF, 2 * DFF)}


def _pack(arrs):
    flat = jnp.concatenate([a.reshape(-1).astype(F32) for a in arrs])
    total = flat.shape[0]
    padded = -(-total // 1024) * 1024
    return jnp.pad(flat, (0, padded - total)).reshape(padded // 128, 128)


def _unpack(block, shapes):
    flat = block.reshape(-1)
    out, at = [], 0
    for sh in shapes:
        size = math.prod(sh)
        out.append(flat[at:at + size].reshape(sh))
        at += size
    return out


def _stack_cols(a):
    return a.transpose(1, 0, 2).reshape(a.shape[1], 4 * a.shape[2])


def _unstack_cols(a):
    return a.reshape(a.shape[0], 4, a.shape[1] // 4).transpose(1, 0, 2)


def kernel(x, norm_mix_w, w_in, conv_a_w, conv_a_b, dt_bias, a_log, d_a, norm_a_w, w_proj_a, s5_lam_re, s5_lam_im, s5_log_dt, s5_b_re, s5_b_im, s5_c_re, s5_c_im, s5_d, w_s5_glu, w_out, norm_ffn_w, w_up, conv_ffn_w, conv_ffn_b, w_down, norm_final_w, loss_target, m_norm_mix_w, m_w_in, m_conv_a_w, m_conv_a_b, m_dt_bias, m_a_log, m_d_a, m_norm_a_w, m_w_proj_a, m_s5_lam_re, m_s5_lam_im, m_s5_log_dt, m_s5_b_re, m_s5_b_im, m_s5_c_re, m_s5_c_im, m_s5_d, m_w_s5_glu, m_w_out, m_norm_ffn_w, m_w_up, m_conv_ffn_w, m_conv_ffn_b, m_w_down, m_norm_final_w, v_norm_mix_w, v_w_in, v_conv_a_w, v_conv_a_b, v_dt_bias, v_a_log, v_d_a, v_norm_a_w, v_w_proj_a, v_s5_lam_re, v_s5_lam_im, v_s5_log_dt, v_s5_b_re, v_s5_b_im, v_s5_c_re, v_s5_c_im, v_s5_d, v_w_s5_glu, v_w_out, v_norm_ffn_w, v_w_up, v_conv_ffn_w, v_conv_ffn_b, v_w_down, v_norm_final_w):
    w = dict(norm_mix_w=norm_mix_w, w_in=w_in, conv_a_w=conv_a_w, conv_a_b=conv_a_b, dt_bias=dt_bias, a_log=a_log, d_a=d_a, norm_a_w=norm_a_w, w_proj_a=w_proj_a, s5_lam_re=s5_lam_re, s5_lam_im=s5_lam_im, s5_log_dt=s5_log_dt, s5_b_re=s5_b_re, s5_b_im=s5_b_im, s5_c_re=s5_c_re, s5_c_im=s5_c_im, s5_d=s5_d, w_s5_glu=w_s5_glu, w_out=w_out, norm_ffn_w=norm_ffn_w, w_up=w_up, conv_ffn_w=conv_ffn_w, conv_ffn_b=conv_ffn_b, w_down=w_down, norm_final_w=norm_final_w)
    m = dict(norm_mix_w=m_norm_mix_w, w_in=m_w_in, conv_a_w=m_conv_a_w, conv_a_b=m_conv_a_b, dt_bias=m_dt_bias, a_log=m_a_log, d_a=m_d_a, norm_a_w=m_norm_a_w, w_proj_a=m_w_proj_a, s5_lam_re=m_s5_lam_re, s5_lam_im=m_s5_lam_im, s5_log_dt=m_s5_log_dt, s5_b_re=m_s5_b_re, s5_b_im=m_s5_b_im, s5_c_re=m_s5_c_re, s5_c_im=m_s5_c_im, s5_d=m_s5_d, w_s5_glu=m_w_s5_glu, w_out=m_w_out, norm_ffn_w=m_norm_ffn_w, w_up=m_w_up, conv_ffn_w=m_conv_ffn_w, conv_ffn_b=m_conv_ffn_b, w_down=m_w_down, norm_final_w=m_norm_final_w)
    v = dict(norm_mix_w=v_norm_mix_w, w_in=v_w_in, conv_a_w=v_conv_a_w, conv_a_b=v_conv_a_b, dt_bias=v_dt_bias, a_log=v_a_log, d_a=v_d_a, norm_a_w=v_norm_a_w, w_proj_a=v_w_proj_a, s5_lam_re=v_s5_lam_re, s5_lam_im=v_s5_lam_im, s5_log_dt=v_s5_log_dt, s5_b_re=v_s5_b_re, s5_b_im=v_s5_b_im, s5_c_re=v_s5_c_re, s5_c_im=v_s5_c_im, s5_d=v_s5_d, w_s5_glu=v_w_s5_glu, w_out=v_w_out, norm_ffn_w=v_norm_ffn_w, w_up=v_w_up, conv_ffn_w=v_conv_ffn_w, conv_ffn_b=v_conv_ffn_b, w_down=v_w_down, norm_final_w=v_norm_final_w)
    xi, yi, ci = _place()
    chip = 2 * xi + yi

    cidx = jnp.reshape(ci, (1,)).astype(jnp.int32)
    sidx = jnp.reshape(chip, (1,)).astype(jnp.int32)

    tw = lambda a: jnp.transpose(a[0])[None]
    w["w_in"], m["w_in"], v["w_in"] = tw(w_in), tw(m_w_in), tw(v_w_in)
    shards = [w[k][0].astype(BF16) for k in BIG]
    shards[0] = jnp.pad(shards[0], ((0, WPAD - WSH), (0, 0)))
    own = lambda arr, shard: lax.dynamic_update_slice(arr, shard[None], (chip, 0, 0))
    w_in_full = own(_gather_big(shards[:1])[0], shards[0])
    g_sems, g_srcs, g_lands, token = _ici_start("gather", shards[1:], w_in_full, "gather_rest_start")

    def late_weights(after):
        srcs, got = _ici_wait("gather", g_sems, g_srcs, g_lands, after, "gather_rest_wait")
        full = [own(a, s) for a, s in zip(_pass_halves(list(got)), srcs)]
        return {"w_proj_a": full[0].reshape(DI, D), "w_s5_glu": full[1], "w_out": full[2].reshape(D, D),
                "w_up": full[3], "w_down": full[4].reshape(DFF, D)}

    pending = []

    def send_grads(names, g, tag):
        parts = [g[k] for k in names]
        sib = _swap_halves(parts, "swap_halves_" + tag)
        sums = [_chip_sum(parts[t], sib[t], cidx, "chip_sum_" + k) for t, k in enumerate(names)]
        sems, srcs, lands, tok = _ici_start("scatter", sums, cidx, "scatter_start_" + tag)
        pending.append((names, tag, sems, srcs, lands))
        return tok
    conv_blocks = []
    for k, (taps, cols) in CONV_FULL.items():
        shard = jnp.where(ci == 0, w[k][0], 0.0)
        conv_blocks.append(lax.dynamic_update_slice_in_dim(jnp.zeros((taps, cols), F32), shard, chip * (cols // 4), 1))
    conv_full = _unpack(_allsum_small(_pack(conv_blocks), "sum_conv_w"), [CONV_FULL[k] for k in CONV_FULL])

    win = jnp.concatenate([w_in_full[k, :WSH] for k in range(4)], axis=0)
    p = {
        "w_main": jnp.concatenate([win[:IN_SPLIT[1]], win[IN_SPLIT[2]:]], axis=0),
        "w_dt": jnp.pad(win[IN_SPLIT[1]:IN_SPLIT[2]], ((0, 128 - NG * HPG), (0, 0))),
        "w_in": w_in_full.reshape(4 * WPAD, D),
        "conv_a_w": conv_full[0], "conv_ffn_w": conv_full[1],
        "conv_a_b": conv_a_b, "conv_ffn_b": conv_ffn_b,
        "norm_mix_w": norm_mix_w, "norm_a_w": norm_a_w, "norm_ffn_w": norm_ffn_w,
        "norm_final_w": norm_final_w.reshape(1, D),
        "dt_bias": dt_bias, "a_log": a_log, "d_a": d_a, "s5_d": s5_d,
        "s5_lam_re": s5_lam_re[0], "s5_lam_im": s5_lam_im[0], "s5_log_dt": s5_log_dt[0],
        "s5_b_re": s5_b_re[0], "s5_b_im": s5_b_im[0], "s5_c_re": s5_c_re[0], "s5_c_im": s5_c_im[0],
    }
    loss_blk, gx, g = _local_step(x[0], loss_target[0], p, token, late_weights, send_grads)

    after, halves = gx, {}
    for names, tag, sems, srcs, lands in pending:
        srcs, got = _ici_wait("scatter", sems, srcs, lands, after, "scatter_wait_" + tag)
        for t, k in enumerate(names):
            halves[k] = _shard_sum(srcs[t], got[t], sidx, "shard_sum_" + k)
        after = halves[names[0]]
    g_mine = [halves[k] for k in BIG]
    g_other = _swap_whole(g_mine)

    small_shapes = [CONV_FULL.get(k, w[k].shape[1:] if k != "norm_final_w" else w[k].shape) for k in SMALL]
    small = _allsum_small(_pack([g[k] for k in SMALL] + [loss_blk[0:1, 0:1]]), "sum_small_grads")
    small_grads = dict(zip(SMALL + ["loss"], _unpack(small, small_shapes + [(1,)])))
    for k, (taps, cols) in CONV_FULL.items():
        small_grads[k] = lax.dynamic_slice_in_dim(small_grads[k], chip * (cols // 4), cols // 4, axis=1)
    loss = small_grads.pop("loss").reshape(())

    grads, delta, new_m, new_v = {}, {}, {}, {}
    for t, k in enumerate(BIG):
        outs = _adamw_halves(w[k], g_mine[t], g_other[t], m[k], v[k], cidx, "adamw_" + k)
        grads[k], delta[k], new_m[k], new_v[k] = [tw(o) for o in outs] if k == "w_in" else outs
    for k in SMALL:
        grads[k] = small_grads[k].reshape(w[k].shape)
    pk = lambda t: _pack([t[k] for k in SMALL])
    d_, m_, v_ = _adamw(pk(w), pk(grads), pk(m), pk(v), "adamw_small")
    shapes = [w[k].shape for k in SMALL]
    for k, dd, mm, vv in zip(SMALL, _unpack(d_, shapes), _unpack(m_, shapes), _unpack(v_, shapes)):
        delta[k], new_m[k], new_v[k] = dd, mm, vv
    return (loss, gx[None], *[grads[k] for k in ORDER], *[delta[k] for k in ORDER],
            *[new_m[k] for k in ORDER], *[new_v[k] for k in ORDER])
```

```python
import functools
import math

import jax
import jax.numpy as jnp
from jax import lax
from jax.experimental import pallas as pl
from jax.experimental.pallas import tpu as pltpu

F32 = jnp.float32
BF16 = jnp.bfloat16
HI = lax.Precision.HIGHEST
MESH = pl.DeviceIdType.MESH
ANY = pl.BlockSpec(memory_space=pl.ANY)

D = 2048
DI = 3072
HD = 64
NG = 8
HPG = 6
GW = HPG * HD
NS = 128
KA = 4
Q = 256
CONVD = DI + 2 * NG * NS
DS5 = 1024
NCH = 4096
DFF = 5632
KF = 3
EPS = 1e-6
EIG_MAX = -1e-4
NMAIN = 13312
OFF_XBC, OFF_U, OFF_GA, OFF_GB = 3072, 8192, 9216, 11264
WSH = 3340
WPAD = 3360
VMEM_LIMIT = 56 * 1024 * 1024

LR, B1, B2, AEPS, WD, STEP = 0.001, 0.9, 0.999, 1e-08, 0.01, 10


def _cp(*sem):
    return pltpu.CompilerParams(dimension_semantics=sem, vmem_limit_bytes=VMEM_LIMIT)


def _sig(x):
    return jax.nn.sigmoid(x)


def _silu(x):
    return x * _sig(x)


def _dsilu(x):
    s = _sig(x)
    return s * (1.0 + x * (1.0 - s))


def _softplus(x):
    return jnp.maximum(x, 0.0) + jnp.log(1.0 + jnp.exp(-jnp.abs(x)))


_GC = math.sqrt(2.0 / math.pi)


def _gelu(x):
    return 0.5 * x * (1.0 + jnp.tanh(_GC * (x + 0.044715 * x * x * x)))


def _dgelu(x):
    t = jnp.tanh(_GC * (x + 0.044715 * x * x * x))
    return 0.5 * (1.0 + t) + 0.5 * x * (1.0 - t * t) * _GC * (1.0 + 3.0 * 0.044715 * x * x)


def _dot(a, b, dims=((1,), (0,)), prec=None):
    return lax.dot_general(a, b, (dims, ((), ())), precision=prec, preferred_element_type=F32)


NT = ((1,), (1,))
TN = ((0,), (0,))


def _pick(n, t):
    for unit in (128, 8):
        for cand in range(min(n, t) // unit * unit, 0, -unit):
            if n % cand == 0:
                return cand
    return n


def _matmul(a, b, mode, name, out_dtype=F32, tm=512, tn=1024, tk=2048, residual=None, b_stacked=False,
            out_stacked=False, after=None):
    if b_stacked:
        _, brows, bn = b.shape
        bshape = (brows, 4 * bn)
    else:
        bshape = b.shape
    if mode == "nn":
        (m, k), (k2, n) = a.shape, bshape
    elif mode == "nt":
        (m, k), (n, k2) = a.shape, bshape
    else:
        (k, m), (k2, n) = a.shape, bshape
    assert k == k2
    tm = _pick(m, tm)
    tn = _pick(n // 4 if (out_stacked or (b_stacked and mode != "nt")) else n, tn)
    tk = _pick(k // 4 if (b_stacked and mode == "nt") else k, tk)
    nk = k // tk
    dims = {"nn": ((1,), (0,)), "nt": NT, "tn": TN}[mode]
    has_res = residual is not None
    n_in = 2 + has_res + (after is not None)

    def body(*refs):
        a_ref, b_ref = refs[0], refs[1]
        r_ref = refs[2] if has_res else None
        o_ref = refs[n_in]
        p = _dot(a_ref[...], b_ref[...], dims)

        def finish(r):
            if has_res:
                r = r + r_ref[...]
            o_ref[...] = r.astype(out_dtype)

        if nk == 1:
            finish(p)
        else:
            acc = refs[-1]
            kk = pl.program_id(2)

            @pl.when(kk == 0)
            def _():
                acc[...] = p

            @pl.when(kk > 0)
            def _():
                acc[...] += p

            @pl.when(kk == nk - 1)
            def _():
                finish(acc[...])

    if mode == "tn":
        a_spec = pl.BlockSpec((tk, tm), lambda i, j, kk: (kk, i))
    else:
        a_spec = pl.BlockSpec((tm, tk), lambda i, j, kk: (i, kk))
    if mode == "nt":
        if b_stacked:
            per = bn // tk
            b_spec = pl.BlockSpec((None, tn, tk), lambda i, j, kk: (kk // per, j, kk % per))
        else:
            b_spec = pl.BlockSpec((tn, tk), lambda i, j, kk: (j, kk))
    elif b_stacked:
        per = bn // tn
        b_spec = pl.BlockSpec((None, tk, tn), lambda i, j, kk: (j // per, kk, j % per))
    else:
        b_spec = pl.BlockSpec((tk, tn), lambda i, j, kk: (kk, j))
    o_spec = pl.BlockSpec((tm, tn), lambda i, j, kk: (i, j))
    if out_stacked:
        per_o = n // 4 // tn
        out_spec = pl.BlockSpec((None, tm, tn), lambda i, j, kk: (j // per_o, i, j % per_o))
        out_shape = jax.ShapeDtypeStruct((4, m, n // 4), out_dtype)
    else:
        out_spec, out_shape = o_spec, jax.ShapeDtypeStruct((m, n), out_dtype)
    in_specs, args = [a_spec, b_spec], [a, b]
    if has_res:
        in_specs.append(o_spec)
        args.append(residual)
    if after is not None:
        in_specs.append(ANY)
        args.append(after)
    return pl.pallas_call(
        body, name=name, grid=(m // tm, n // tn, nk),
        in_specs=in_specs, out_specs=out_spec, out_shape=out_shape,
        scratch_shapes=[pltpu.VMEM((tm, tn), F32)] if nk > 1 else [],
        compiler_params=_cp("parallel", "parallel", "arbitrary"),
    )(*args)


TL = 256


def _rms_fwd(x, w, name, after=None):
    n, d = x.shape

    def body(x_ref, w_ref, *rest):
        xv = x_ref[...]
        r = lax.rsqrt(jnp.mean(xv * xv, axis=-1, keepdims=True) + EPS)
        rest[-1][...] = (xv * r * w_ref[...]).astype(BF16)

    extra = [] if after is None else [after]
    return pl.pallas_call(
        body, name=name, grid=(n // TL,),
        in_specs=[pl.BlockSpec((TL, d), lambda i: (i, 0)), pl.BlockSpec((1, d), lambda i: (0, 0))] + [ANY] * len(extra),
        out_specs=pl.BlockSpec((TL, d), lambda i: (i, 0)),
        out_shape=jax.ShapeDtypeStruct((n, d), BF16), compiler_params=_cp("parallel"),
    )(x, w, *extra)


def _rms_bwd(dhn, x, w, dres, name):
    n, d = x.shape

    def body(g_ref, x_ref, w_ref, r_ref, dx_ref, dxb_ref, gw_ref):
        xv = x_ref[...]
        r = lax.rsqrt(jnp.mean(xv * xv, axis=-1, keepdims=True) + EPS)
        xh = xv * r
        gv = g_ref[...]
        g = gv * w_ref[...]
        dx = r_ref[...] + r * (g - xh * jnp.mean(g * xh, axis=-1, keepdims=True))
        dx_ref[...] = dx
        dxb_ref[...] = dx.astype(BF16)

        @pl.when(pl.program_id(0) == 0)
        def _():
            gw_ref[...] = jnp.zeros_like(gw_ref)

        gw_ref[...] += jnp.sum(gv * xh, axis=0, keepdims=True)

    row = pl.BlockSpec((TL, d), lambda i: (i, 0))
    vec = pl.BlockSpec((1, d), lambda i: (0, 0))
    return pl.pallas_call(
        body, name=name, grid=(n // TL,),
        in_specs=[row, row, vec, row], out_specs=[row, row, vec],
        out_shape=[jax.ShapeDtypeStruct((n, d), F32), jax.ShapeDtypeStruct((n, d), BF16),
                   jax.ShapeDtypeStruct((1, d), F32)],
        compiler_params=_cp("arbitrary"),
    )(dhn, x, w, dres)


def _final(h2, w, target):
    n, d = h2.shape

    def body(x_ref, w_ref, t_ref, dx_ref, dxb_ref, gw_ref, loss_ref):
        xv = x_ref[...]
        r = lax.rsqrt(jnp.mean(xv * xv, axis=-1, keepdims=True) + EPS)
        xh = xv * r
        diff = xh * w_ref[...] - t_ref[...]
        gv = diff * (1.0 / d)
        g = gv * w_ref[...]
        dx = r * (g - xh * jnp.mean(g * xh, axis=-1, keepdims=True))
        dx_ref[...] = dx
        dxb_ref[...] = dx.astype(BF16)

        @pl.when(pl.program_id(0) == 0)
        def _():
            gw_ref[...] = jnp.zeros_like(gw_ref)
            loss_ref[...] = jnp.zeros_like(loss_ref)

        gw_ref[...] += jnp.sum(gv * xh, axis=0, keepdims=True)
        part = 0.5 * jnp.sum(jnp.mean(diff * diff, axis=-1, keepdims=True), axis=0, keepdims=True)
        loss_ref[...] += jnp.broadcast_to(part, loss_ref.shape)

    row = pl.BlockSpec((TL, d), lambda i: (i, 0))
    vec = pl.BlockSpec((1, d), lambda i: (0, 0))
    return pl.pallas_call(
        body, name="final_loss", grid=(n // TL,),
        in_specs=[row, vec, row], out_specs=[row, row, vec, pl.BlockSpec((8, 128), lambda i: (0, 0))],
        out_shape=[jax.ShapeDtypeStruct((n, d), F32), jax.ShapeDtypeStruct((n, d), BF16),
                   jax.ShapeDtypeStruct((1, d), F32), jax.ShapeDtypeStruct((8, 128), F32)],
        compiler_params=_cp("arbitrary"),
    )(h2, w, target)


CT = 512
CL = 512


def _lagged(xf, taps, rows):
    return [xf[8:8 + rows]] + [pltpu.roll(xf, s, 0)[8:8 + rows] for s in range(1, taps)]


def _shift_up(x, u, n):
    if u == 0:
        return x[0:n]
    return pltpu.roll(x, x.shape[0] - u, 0)[0:n]


def _conv_pre(lagged, w_ref, b_ref, taps):
    pre = b_ref[...]
    for k in range(taps):
        pre = pre + w_ref[k:k + 1, :] * lagged[taps - 1 - k]
    return pre


def _conv_back(e, w_ref, taps):
    dx = w_ref[taps - 1:taps, :] * e[0:CL]
    for k in range(taps - 1):
        dx = dx + w_ref[k:k + 1, :] * _shift_up(e, taps - 1 - k, CL)
    return dx


def _halo_specs(n, col_of):
    per = CL // 8
    cur = pl.BlockSpec((CL, CT), lambda j, i, *_: (i, col_of(j)))
    prev = pl.BlockSpec((8, CT), lambda j, i, *_: (jnp.maximum(i * per - 1, 0), col_of(j)))
    nxt = pl.BlockSpec((8, CT), lambda j, i, *_: (jnp.minimum((i + 1) * per, n // 8 - 1), col_of(j)))
    return prev, cur, nxt


def _conv_a_fwd(proj, w, b):
    n = proj.shape[0]
    off = OFF_XBC // CT

    def body(p_ref, x_ref, w_ref, b_ref, o_ref):
        p8 = jnp.where(pl.program_id(1) > 0, p_ref[...], 0.0)
        xf = jnp.concatenate([p8, x_ref[...]], axis=0)
        o_ref[...] = _silu(_conv_pre(_lagged(xf, KA, CL), w_ref, b_ref, KA))

    prev, cur, _ = _halo_specs(n, lambda j: j + off)
    return pl.pallas_call(
        body, name="conv_a_fwd", grid=(CONVD // CT, n // CL),
        in_specs=[prev, cur, pl.BlockSpec((KA, CT), lambda j, i: (0, j)), pl.BlockSpec((1, CT), lambda j, i: (0, j))],
        out_specs=pl.BlockSpec((CL, CT), lambda j, i: (i, j)),
        out_shape=jax.ShapeDtypeStruct((n, CONVD), F32), compiler_params=_cp("parallel", "parallel"),
    )(proj, proj, w, b)


def _conv_a_bwd(proj, dout, w, b, col0, name):
    n, width = dout.shape
    off = (OFF_XBC + col0) // CT
    woff = col0 // CT
    nl = n // CL

    def body(p_ref, x_ref, n_ref, d_ref, dn_ref, w_ref, b_ref, dx_ref, dw_ref, db_ref):
        i = pl.program_id(1)
        xf = jnp.concatenate([jnp.where(i > 0, p_ref[...], 0.0), x_ref[...], n_ref[...]], axis=0)
        lag = _lagged(xf, KA, CL + 8)
        de = jnp.concatenate([d_ref[...], jnp.where(i < nl - 1, dn_ref[...], 0.0)], axis=0)
        se = de * _dsilu(_conv_pre(lag, w_ref, b_ref, KA))
        dx_ref[...] = _conv_back(se, w_ref, KA).astype(BF16)

        @pl.when(i == 0)
        def _():
            dw_ref[...] = jnp.zeros_like(dw_ref)
            db_ref[...] = jnp.zeros_like(db_ref)

        sc = se[0:CL]
        for k in range(KA):
            dw_ref[k:k + 1, :] += jnp.sum(sc * lag[KA - 1 - k][0:CL], axis=0, keepdims=True)
        db_ref[...] += jnp.sum(sc, axis=0, keepdims=True)

    prev, cur, nxt = _halo_specs(n, lambda j: j + off)
    _, dcur, dnxt = _halo_specs(n, lambda j: j)
    wspec = pl.BlockSpec((KA, CT), lambda j, i: (0, j + woff))
    bspec = pl.BlockSpec((1, CT), lambda j, i: (0, j + woff))
    return pl.pallas_call(
        body, name=name, grid=(width // CT, nl),
        in_specs=[prev, cur, nxt, dcur, dnxt, wspec, bspec],
        out_specs=[pl.BlockSpec((CL, CT), lambda j, i: (i, j)), pl.BlockSpec((KA, CT), lambda j, i: (0, j)),
                   pl.BlockSpec((1, CT), lambda j, i: (0, j))],
        out_shape=[jax.ShapeDtypeStruct((n, width), BF16), jax.ShapeDtypeStruct((KA, width), F32),
                   jax.ShapeDtypeStruct((1, width), F32)],
        compiler_params=_cp("parallel", "arbitrary"),
    )(proj, proj, proj, dout, dout, w, b)


def _conv_ffn_fwd(up, w, b):
    n = up.shape[0]
    nb = DFF // CT

    def body(pg_ref, g_ref, pv_ref, v_ref, wg_ref, bg_ref, wv_ref, bv_ref, o_ref):
        inner = pl.program_id(1) > 0
        gf = jnp.concatenate([jnp.where(inner, pg_ref[...], 0.0), g_ref[...]], axis=0)
        vf = jnp.concatenate([jnp.where(inner, pv_ref[...], 0.0), v_ref[...]], axis=0)
        gc = _conv_pre(_lagged(gf, KF, CL), wg_ref, bg_ref, KF)
        vc = _conv_pre(_lagged(vf, KF, CL), wv_ref, bv_ref, KF)
        o_ref[...] = (_silu(gc) * vc).astype(BF16)

    gp, gcur, _ = _halo_specs(n, lambda j: j)
    vp, vcur, _ = _halo_specs(n, lambda j: j + nb)
    return pl.pallas_call(
        body, name="conv_ffn_fwd", grid=(nb, n // CL),
        in_specs=[gp, gcur, vp, vcur,
                  pl.BlockSpec((KF, CT), lambda j, i: (0, j)), pl.BlockSpec((1, CT), lambda j, i: (0, j)),
                  pl.BlockSpec((KF, CT), lambda j, i: (0, j + nb)), pl.BlockSpec((1, CT), lambda j, i: (0, j + nb))],
        out_specs=pl.BlockSpec((CL, CT), lambda j, i: (i, j)),
        out_shape=jax.ShapeDtypeStruct((n, DFF), BF16), compiler_params=_cp("parallel", "parallel"),
    )(up, up, up, up, w, b, w, b)


def _conv_ffn_bwd(up, dact, w, b):
    n = up.shape[0]
    nb = DFF // CT
    nl = n // CL

    def body(pg_ref, g_ref, ng_ref, pv_ref, v_ref, nv_ref, d_ref, dn_ref, wg_ref, bg_ref, wv_ref, bv_ref,
             dxg_ref, dxv_ref, dwg_ref, dwv_ref, dbg_ref, dbv_ref):
        i = pl.program_id(1)
        gf = jnp.concatenate([jnp.where(i > 0, pg_ref[...], 0.0), g_ref[...], ng_ref[...]], axis=0)
        vf = jnp.concatenate([jnp.where(i > 0, pv_ref[...], 0.0), v_ref[...], nv_ref[...]], axis=0)
        glag, vlag = _lagged(gf, KF, CL + 8), _lagged(vf, KF, CL + 8)
        de = jnp.concatenate([d_ref[...], jnp.where(i < nl - 1, dn_ref[...], 0.0)], axis=0).astype(F32)
        gc = _conv_pre(glag, wg_ref, bg_ref, KF)
        vc = _conv_pre(vlag, wv_ref, bv_ref, KF)
        sg = _sig(gc)
        dgc = de * vc * (sg * (1.0 + gc * (1.0 - sg)))
        dvc = de * (gc * sg)
        dxg_ref[...] = _conv_back(dgc, wg_ref, KF).astype(BF16)
        dxv_ref[...] = _conv_back(dvc, wv_ref, KF).astype(BF16)

        @pl.when(i == 0)
        def _():
            for r in (dwg_ref, dwv_ref, dbg_ref, dbv_ref):
                r[...] = jnp.zeros_like(r)

        for e, lag, dw_ref, db_ref in ((dgc, glag, dwg_ref, dbg_ref), (dvc, vlag, dwv_ref, dbv_ref)):
            ec = e[0:CL]
            for k in range(KF):
                dw_ref[k:k + 1, :] += jnp.sum(ec * lag[KF - 1 - k][0:CL], axis=0, keepdims=True)
            db_ref[...] += jnp.sum(ec, axis=0, keepdims=True)

    gp, gcur, gnx = _halo_specs(n, lambda j: j)
    vp, vcur, vnx = _halo_specs(n, lambda j: j + nb)
    wcol = lambda o: (pl.BlockSpec((KF, CT), lambda j, i: (0, j + o)), pl.BlockSpec((1, CT), lambda j, i: (0, j + o)))
    wg, bg = wcol(0)
    wv, bv = wcol(nb)
    dxs = pl.BlockSpec((CL, CT), lambda j, i: (i, j))
    outs = pl.pallas_call(
        body, name="conv_ffn_bwd", grid=(nb, nl),
        in_specs=[gp, gcur, gnx, vp, vcur, vnx, gcur, gnx, wg, bg, wv, bv],
        out_specs=[dxs, dxs, wg, wg, bg, bg],
        out_shape=[jax.ShapeDtypeStruct((n, DFF), BF16)] * 2 + [jax.ShapeDtypeStruct((KF, DFF), F32)] * 2
        + [jax.ShapeDtypeStruct((1, DFF), F32)] * 2,
        compiler_params=_cp("parallel", "arbitrary"),
    )(up, up, up, up, up, up, dact, dact, w, b, w, b)
    return [jnp.concatenate(outs[k:k + 2], axis=1) for k in (0, 2, 4)]


HL = 128


def _split3(x):
    hi = x.astype(BF16)
    r1 = x - hi.astype(F32)
    mid = r1.astype(BF16)
    return hi, mid, (r1 - mid.astype(F32)).astype(BF16)


def _dot3(x, m):
    hi, mid, lo = _split3(x)
    return _dot(hi, m) + _dot(mid, m) + _dot(lo, m)


def _tri():
    row = lax.broadcasted_iota(jnp.int32, (Q, Q), 0)
    col = lax.broadcasted_iota(jnp.int32, (Q, Q), 1)
    return row >= col, row <= col


def _ssd_prep(dtraw, hp, emat):
    n = dtraw.shape[0]

    def body(d_ref, hp_ref, e_ref, dt_ref, s_ref, st_ref, dte_ref, se_ref):
        lower, upper = _tri()
        dt = _softplus(d_ref[...] + hp_ref[0:1, :])
        da = dt * (-jnp.exp(hp_ref[1:2, :]))
        s = _dot(lower.astype(F32), da, prec=HI)
        dt_ref[...] = dt
        s_ref[...] = s
        st_ref[...] = _dot(da, upper.astype(F32), TN, prec=HI)
        e = e_ref[...]
        dte_ref[...] = _dot3(dt, e)
        se_ref[...] = _dot3(s, e)

    row = pl.BlockSpec((Q, HL), lambda c: (c, 0))
    wide = pl.BlockSpec((Q, DI), lambda c: (c, 0))
    return pl.pallas_call(
        body, name="ssd_prep", grid=(n // Q,),
        in_specs=[row, pl.BlockSpec((8, HL), lambda c: (0, 0)), pl.BlockSpec((HL, DI), lambda c: (0, 0))],
        out_specs=[row, row, pl.BlockSpec((HL, Q), lambda c: (0, c)), wide, wide],
        out_shape=[jax.ShapeDtypeStruct((n, HL), F32)] * 2 + [jax.ShapeDtypeStruct((HL, n), F32)]
        + [jax.ShapeDtypeStruct((n, DI), F32)] * 2,
        compiler_params=_cp("parallel"),
    )(dtraw, hp, emat)


def _ssd_post(ds_e, ddt_e, tsum, dsh, dtraw, dt, hp, emat_t):
    n = dtraw.shape[0]

    def body(dse_ref, dde_ref, ts_ref, dsh_ref, d_ref, dt_ref, hp_ref, et_ref, draw_ref, ps_ref):
        _, upper = _tri()
        et = et_ref[...]
        a = -jnp.exp(hp_ref[1:2, :])
        rows = lax.broadcasted_iota(jnp.int32, (Q, HL), 0)
        ds_t = _dot3(jnp.broadcast_to(ts_ref[...], (8, DI)), et)[0:1, :]
        ds = _dot3(dse_ref[...], et) + dsh_ref[...] + jnp.where(rows == Q - 1, ds_t, 0.0)
        d_a = _dot(upper.astype(F32), ds, prec=HI)
        draw = (_dot3(dde_ref[...], et) + d_a * a) * _sig(d_ref[...] + hp_ref[0:1, :])
        draw_ref[...] = draw

        @pl.when(pl.program_id(0) == 0)
        def _():
            ps_ref[...] = jnp.zeros_like(ps_ref)

        ps_ref[0:1, :] += jnp.sum(draw, axis=0, keepdims=True)
        ps_ref[1:2, :] += jnp.sum(d_a * dt_ref[...], axis=0, keepdims=True) * a

    row = pl.BlockSpec((Q, HL), lambda c: (c, 0))
    wide = pl.BlockSpec((Q, DI), lambda c: (c, 0))
    small = pl.BlockSpec((8, HL), lambda c: (0, 0))
    return pl.pallas_call(
        body, name="ssd_post", grid=(n // Q,),
        in_specs=[wide, wide, pl.BlockSpec((None, 1, DI), lambda c: (c, 0, 0)), row, row, row, small,
                  pl.BlockSpec((DI, HL), lambda c: (0, 0))],
        out_specs=[row, small],
        out_shape=[jax.ShapeDtypeStruct((n, HL), F32), jax.ShapeDtypeStruct((8, HL), F32)],
        compiler_params=_cp("arbitrary"),
    )(ds_e, ddt_e, tsum, dsh, dtraw, dt, hp, emat_t)


def _ssd_specs(nc, rev):
    cc = (lambda c: nc - 1 - c) if rev else (lambda c: c)
    return [
        pl.BlockSpec((Q, GW), lambda g, c: (cc(c), g)),
        pl.BlockSpec((Q, NS), lambda g, c: (cc(c), DI // NS + g)),
        pl.BlockSpec((Q, NS), lambda g, c: (cc(c), (DI + NG * NS) // NS + g)),
        pl.BlockSpec((None, Q, 8), lambda g, c: (g, cc(c), 0)),
        pl.BlockSpec((8, Q), lambda g, c: (g, cc(c))),
        pl.BlockSpec((Q, GW), lambda g, c: (cc(c), g)),
        pl.BlockSpec((Q, GW), lambda g, c: (cc(c), g)),
        pl.BlockSpec((1, GW), lambda g, c: (0, g)),
    ]


def _ssd_fwd(xbc, s8, s_t, dt_e, s_e, dexp):
    n = xbc.shape[0]
    nc = n // Q

    def body(xs_ref, b_ref, c_ref, sc_ref, sr_ref, dte_ref, se_ref, dexp_ref, y_ref, sp_ref, st):
        @pl.when(pl.program_id(1) == 0)
        def _():
            st[...] = jnp.zeros_like(st)

        lower, _ = _tri()
        s_c, s_r, dt_e, s_e = sc_ref[...], sr_ref[...], dte_ref[...], se_ref[...]
        xs = xs_ref[...]
        x = xs * dt_e
        xb = x.astype(BF16)
        bb, cb = b_ref[...].astype(BF16), c_ref[...].astype(BF16)
        cbm = _dot(cb, bb, NT)
        st_e = s_e[Q - 1:Q, :]
        sprev = st[...]
        sp_ref[...] = sprev
        yoff = _dot(cb, sprev.astype(BF16)) * jnp.exp(s_e) + dexp_ref[...] * xs
        for h in range(HPG):
            sl = slice(h * HD, (h + 1) * HD)
            lm = jnp.where(lower, jnp.exp(jnp.minimum(s_c[:, h:h + 1] - s_r[h:h + 1, :], 0.0)), 0.0)
            y_ref[:, sl] = _dot((cbm * lm).astype(BF16), xb[:, sl]) + yoff[:, sl]
        w = (x * jnp.exp(st_e - s_e)).astype(BF16)
        st[...] = jnp.exp(st_e) * sprev + _dot(bb, w, TN)

    return pl.pallas_call(
        body, name="ssd_fwd", grid=(NG, nc), in_specs=_ssd_specs(nc, False),
        out_specs=[pl.BlockSpec((Q, GW), lambda g, c: (c, g)),
                   pl.BlockSpec((None, None, NS, GW), lambda g, c: (c, g, 0, 0))],
        out_shape=[jax.ShapeDtypeStruct((n, DI), F32), jax.ShapeDtypeStruct((nc, NG, NS, GW), F32)],
        scratch_shapes=[pltpu.VMEM((NS, GW), F32)],
        compiler_params=_cp("parallel", "arbitrary"),
    )(xbc, xbc, xbc, s8, s_t, dt_e, s_e, dexp)


def _ssd_bwd(xbc, s8, s_t, dt_e, s_e, dexp, sprev_all, dy):
    n = xbc.shape[0]
    nc = n // Q
    rc = lambda c: nc - 1 - c

    def body(xs_ref, b_ref, c_ref, sc_ref, sr_ref, dte_ref, se_ref, dexp_ref, sp_ref, dy_ref,
             dxs_ref, db_ref, dc_ref, dse_ref, dde_ref, ts_ref, dsh_ref, pd_ref, dst, dxbuf):
        @pl.when(pl.program_id(1) == 0)
        def _():
            dst[...] = jnp.zeros_like(dst)
            pd_ref[...] = jnp.zeros_like(pd_ref)

        lower, upper = _tri()
        s_c, s_r, dt_e, s_e = sc_ref[...], sr_ref[...], dte_ref[...], se_ref[...]
        xs = xs_ref[...]
        x = xs * dt_e
        xb = x.astype(BF16)
        bb, cb = b_ref[...].astype(BF16), c_ref[...].astype(BF16)
        cbm = _dot(cb, bb, NT)
        cbt = _dot(bb, cb, NT)
        st_e = s_e[Q - 1:Q, :]
        dec_out, dec_st, e_t = jnp.exp(s_e), jnp.exp(st_e - s_e), jnp.exp(st_e)
        dyv = dy_ref[...]
        dyb = dyv.astype(BF16)
        sprev = sp_ref[...]
        sb = sprev.astype(BF16)
        ds_in = dst[...]
        dsb = ds_in.astype(BF16)

        cs = _dot(cb, sb)
        dcs = (dyv * dec_out).astype(BF16)
        d_c = _dot(dcs, sb, NT)
        wf = x * dec_st
        d_w = _dot(bb, dsb)
        d_b = _dot(wf.astype(BF16), dsb, NT)
        tw = d_w * wf
        dse_ref[...] = dyv * cs * dec_out - tw
        ds_c = jnp.zeros((Q, 8), F32)
        dcb = jnp.zeros((Q, Q), F32)
        dcbt = jnp.zeros((Q, Q), F32)
        lane8 = lax.broadcasted_iota(jnp.int32, (1, 8), 1)
        for h in range(HPG):
            sl = slice(h * HD, (h + 1) * HD)
            sc_h, sr_h = s_c[:, h:h + 1], s_r[h:h + 1, :]
            lm = jnp.where(lower, jnp.exp(jnp.minimum(sc_h - sr_h, 0.0)), 0.0)
            lmt = jnp.where(upper, jnp.exp(jnp.minimum(sr_h - sc_h, 0.0)), 0.0)
            mt = cbt * lmt
            dm = _dot(dyb[:, sl], xb[:, sl], NT)
            dmt = _dot(xb[:, sl], dyb[:, sl], NT)
            dxbuf[:, sl] = _dot(mt.astype(BF16), dyb[:, sl])
            dml = dm * lm
            dmlt = dmt * lmt
            dcb = dcb + dml
            dcbt = dcbt + dmlt
            dsh = jnp.sum(dml * cbm, axis=1, keepdims=True) - jnp.sum(dmlt * cbt, axis=1, keepdims=True)
            ds_c = ds_c + dsh * (lane8 == h).astype(F32)
        d_c = d_c + _dot(dcb.astype(BF16), bb)
        d_b = d_b + _dot(dcbt.astype(BF16), cb)
        dx = d_w * dec_st + dxbuf[...]
        ts_ref[...] = jnp.sum(tw, axis=0, keepdims=True) + jnp.sum(ds_in * sprev, axis=0, keepdims=True) * e_t
        dsh_ref[...] = ds_c
        dde_ref[...] = dx * xs
        pd_ref[...] += jnp.sum(dyv * xs, axis=0, keepdims=True)
        dxs_ref[...] = dx * dt_e + dyv * dexp_ref[...]
        db_ref[...] = d_b
        dc_ref[...] = d_c
        dst[...] = e_t * ds_in + _dot(cb, dcs, TN)

    wide = pl.BlockSpec((Q, GW), lambda g, c: (rc(c), g))
    state = pl.BlockSpec((Q, NS), lambda g, c: (rc(c), g))
    in_specs = _ssd_specs(nc, True) + [pl.BlockSpec((None, None, NS, GW), lambda g, c: (rc(c), g, 0, 0)), wide]
    return pl.pallas_call(
        body, name="ssd_bwd", grid=(NG, nc), in_specs=in_specs,
        out_specs=[wide, state, state, wide, wide,
                   pl.BlockSpec((None, 1, GW), lambda g, c: (rc(c), 0, g)),
                   pl.BlockSpec((None, Q, 8), lambda g, c: (g, rc(c), 0)),
                   pl.BlockSpec((None, 1, GW), lambda g, c: (g, 0, 0))],
        out_shape=[jax.ShapeDtypeStruct((n, DI), F32), jax.ShapeDtypeStruct((n, NG * NS), F32),
                   jax.ShapeDtypeStruct((n, NG * NS), F32), jax.ShapeDtypeStruct((n, DI), F32),
                   jax.ShapeDtypeStruct((n, DI), F32), jax.ShapeDtypeStruct((nc, 1, DI), F32),
                   jax.ShapeDtypeStruct((NG, n, 8), F32), jax.ShapeDtypeStruct((NG, 1, GW), F32)],
        scratch_shapes=[pltpu.VMEM((NS, GW), F32), pltpu.VMEM((Q, GW), F32)],
        compiler_params=_cp("parallel", "arbitrary"),
    )(xbc, xbc, xbc, s8, s_t, dt_e, s_e, dexp, sprev_all, dy)


GL = 128


def _gnorm_fwd(y, proj, w):
    n = y.shape[0]

    def body(y_ref, z_ref, w_ref, o_ref):
        for g in range(NG):
            sl = slice(g * GW, (g + 1) * GW)
            yz = y_ref[:, sl] * _silu(z_ref[:, sl])
            r = lax.rsqrt(jnp.mean(yz * yz, axis=-1, keepdims=True) + EPS)
            o_ref[:, sl] = (yz * r * w_ref[:, sl]).astype(BF16)

    row = pl.BlockSpec((GL, DI), lambda i: (i, 0))
    return pl.pallas_call(
        body, name="gnorm_fwd", grid=(n // GL,),
        in_specs=[row, row, pl.BlockSpec((1, DI), lambda i: (0, 0))], out_specs=row,
        out_shape=jax.ShapeDtypeStruct((n, DI), BF16), compiler_params=_cp("parallel"),
    )(y, proj, w)


def _gnorm_bwd(dyn, y, proj, w):
    n = y.shape[0]

    def body(d_ref, y_ref, z_ref, w_ref, dy_ref, dz_ref, gw_ref):
        @pl.when(pl.program_id(0) == 0)
        def _():
            gw_ref[...] = jnp.zeros_like(gw_ref)

        for g in range(NG):
            sl = slice(g * GW, (g + 1) * GW)
            yv, zv, dv = y_ref[:, sl], z_ref[:, sl], d_ref[:, sl]
            sz = _silu(zv)
            yz = yv * sz
            r = lax.rsqrt(jnp.mean(yz * yz, axis=-1, keepdims=True) + EPS)
            yh = yz * r
            gg = dv * w_ref[:, sl]
            dyz = r * (gg - yh * jnp.mean(gg * yh, axis=-1, keepdims=True))
            gw_ref[:, sl] += jnp.sum(dv * yh, axis=0, keepdims=True)
            dy_ref[:, sl] = dyz * sz
            dz_ref[:, sl] = (dyz * yv * _dsilu(zv)).astype(BF16)

    row = pl.BlockSpec((GL, DI), lambda i: (i, 0))
    vec = pl.BlockSpec((1, DI), lambda i: (0, 0))
    return pl.pallas_call(
        body, name="gnorm_bwd", grid=(n // GL,),
        in_specs=[row, row, row, vec], out_specs=[row, row, vec],
        out_shape=[jax.ShapeDtypeStruct((n, DI), F32), jax.ShapeDtypeStruct((n, DI), BF16),
                   jax.ShapeDtypeStruct((1, DI), F32)],
        compiler_params=_cp("arbitrary"),
    )(dyn, y, proj, w)


SL = 512
SB = 8
SCB = NCH // SB


def _s5_in(proj, bre, bim):
    n = proj.shape[0]
    uoff = OFF_U // 128

    def body(u_ref, br_ref, bi_ref, or_ref, oi_ref):
        u = u_ref[...].astype(BF16)
        or_ref[...] = _dot(u, br_ref[...])
        oi_ref[...] = _dot(u, bi_ref[...])

    blk = pl.BlockSpec((None, 128, SCB), lambda i, j: (j, 0, 0))
    out = pl.BlockSpec((SL, SCB), lambda i, j: (i, j))
    return pl.pallas_call(
        body, name="s5_in", grid=(n // SL, SB),
        in_specs=[pl.BlockSpec((SL, 128), lambda i, j: (i, uoff + j)), blk, blk], out_specs=[out, out],
        out_shape=[jax.ShapeDtypeStruct((n, NCH), F32)] * 2, compiler_params=_cp("parallel", "parallel"),
    )(proj, bre, bim)


SC = 256


def _s5_scan(vre, vim, tab, reverse, name):
    n = vre.shape[0]
    nl = n // SL
    ng = SL // 8
    ti = (lambda i: nl - 1 - i) if reverse else (lambda i: i)

    def body(re_ref, im_ref, tab_ref, ore_ref, oim_ref, cre, cim):
        @pl.when(pl.program_id(1) == 0)
        def _():
            cre[...] = jnp.zeros_like(cre)
            cim[...] = jnp.zeros_like(cim)

        def step(j, carry):
            cr, ci = carry
            jj = (ng - 1 - j) if reverse else j
            rows = pl.ds(pl.multiple_of(jj * 8, 8), 8)
            vr, vi = re_ref[rows, :], im_ref[rows, :]
            for t, k in enumerate((1, 2, 4)):
                sh = (8 - k) if reverse else k
                rr, ri = pltpu.roll(vr, sh, 0), pltpu.roll(vi, sh, 0)
                pr, pi = tab_ref[2 * t], tab_ref[2 * t + 1]
                vr, vi = vr + pr * rr - pi * ri, vi + pr * ri + pi * rr
            lr, li = tab_ref[6], tab_ref[7]
            vr, vi = vr + lr * cr - li * ci, vi + lr * ci + li * cr
            ore_ref[rows, :] = vr
            oim_ref[rows, :] = vi
            e = 0 if reverse else 7
            return (jnp.broadcast_to(vr[e:e + 1, :], (8, SC)), jnp.broadcast_to(vi[e:e + 1, :], (8, SC)))

        cr, ci = lax.fori_loop(0, ng, step, (cre[...], cim[...]))
        cre[...] = cr
        cim[...] = ci

    blk = pl.BlockSpec((SL, SC), lambda j, i: (ti(i), j))
    return pl.pallas_call(
        body, name=name, grid=(NCH // SC, nl),
        in_specs=[blk, blk, pl.BlockSpec((8, 8, SC), lambda j, i: (0, 0, j))], out_specs=[blk, blk],
        out_shape=[jax.ShapeDtypeStruct((n, NCH), F32)] * 2,
        scratch_shapes=[pltpu.VMEM((8, SC), F32), pltpu.VMEM((8, SC), F32)],
        compiler_params=_cp("parallel", "arbitrary"),
    )(vre, vim, tab)


def _s5_out(xre, xim, cre, cimn, proj, dvec):
    n = xre.shape[0]
    uoff = OFF_U // 128

    def body(xr_ref, xi_ref, cr_ref, ci_ref, u_ref, d_ref, y_ref, g_ref):
        y = (_dot(xr_ref[...].astype(BF16), cr_ref[...]) + _dot(xi_ref[...].astype(BF16), ci_ref[...])
             + d_ref[...] * u_ref[...])
        y_ref[...] = y
        g_ref[...] = _gelu(y).astype(BF16)

    xs = pl.BlockSpec((SL, SCB), lambda i, j: (i, j))
    blk = pl.BlockSpec((None, SCB, 128), lambda i, j: (j, 0, 0))
    out = pl.BlockSpec((SL, 128), lambda i, j: (i, j))
    return pl.pallas_call(
        body, name="s5_out", grid=(n // SL, SB),
        in_specs=[xs, xs, blk, blk, pl.BlockSpec((SL, 128), lambda i, j: (i, uoff + j)),
                  pl.BlockSpec((1, 128), lambda i, j: (0, j))],
        out_specs=[out, out],
        out_shape=[jax.ShapeDtypeStruct((n, DS5), F32), jax.ShapeDtypeStruct((n, DS5), BF16)],
        compiler_params=_cp("parallel", "parallel"),
    )(xre, xim, cre, cimn, proj, dvec)


def _s5_out_bwd(dg, ypre, crt, cimnt, proj, dvec, xre, xim):
    n = dg.shape[0]
    uoff = OFF_U // 128
    nl = n // SL

    def body(dg_ref, y_ref, cr_ref, ci_ref, u_ref, d_ref, xr_ref, xi_ref,
             gr_ref, gi_ref, dus_ref, gcr_ref, gci_ref, gd_ref):
        dy = dg_ref[...] * _dgelu(y_ref[...])
        dyb = dy.astype(BF16)
        gr_ref[...] = _dot(dyb, cr_ref[...])
        gi_ref[...] = _dot(dyb, ci_ref[...])
        dus_ref[...] = dy * d_ref[...]

        @pl.when(pl.program_id(1) == 0)
        def _():
            gcr_ref[...] = jnp.zeros_like(gcr_ref)
            gci_ref[...] = jnp.zeros_like(gci_ref)
            gd_ref[...] = jnp.zeros_like(gd_ref)

        gcr_ref[...] += _dot(xr_ref[...].astype(BF16), dyb, TN)
        gci_ref[...] -= _dot(xi_ref[...].astype(BF16), dyb, TN)
        gd_ref[...] += jnp.sum(dy * u_ref[...], axis=0, keepdims=True)

    u128 = pl.BlockSpec((SL, 128), lambda j, i: (i, j))
    xs = pl.BlockSpec((SL, SCB), lambda j, i: (i, j))
    blk = pl.BlockSpec((None, 128, SCB), lambda j, i: (j, 0, 0))
    gblk = pl.BlockSpec((None, SCB, 128), lambda j, i: (j, 0, 0))
    vec = pl.BlockSpec((1, 128), lambda j, i: (0, j))
    return pl.pallas_call(
        body, name="s5_out_bwd", grid=(SB, nl),
        in_specs=[u128, u128, blk, blk, pl.BlockSpec((SL, 128), lambda j, i: (i, uoff + j)), vec, xs, xs],
        out_specs=[xs, xs, u128, gblk, gblk, vec],
        out_shape=[jax.ShapeDtypeStruct((n, NCH), F32)] * 2 + [jax.ShapeDtypeStruct((n, DS5), F32)]
        + [jax.ShapeDtypeStruct((SB, SCB, 128), F32)] * 2 + [jax.ShapeDtypeStruct((1, DS5), F32)],
        compiler_params=_cp("parallel", "arbitrary"),
    )(dg, ypre, crt, cimnt, proj, dvec, xre, xim)


def _s5_in_bwd(are, aim, brt, bit, proj, dus, xre, xim):
    n = are.shape[0]
    uoff = OFF_U // 128
    per = SL // 8

    def body(ar_ref, ai_ref, br_ref, bi_ref, u_ref, dus_ref, xr_ref, xi_ref, pr_ref, pi_ref,
             du_ref, gbr_ref, gbi_ref, glr_ref, gli_ref):
        i = pl.program_id(1)
        ar, ai = ar_ref[...], ai_ref[...]
        arb, aib = ar.astype(BF16), ai.astype(BF16)
        du_ref[...] = (_dot(arb, br_ref[...]) + _dot(aib, bi_ref[...]) + dus_ref[...]).astype(BF16)

        @pl.when(i == 0)
        def _():
            for r in (gbr_ref, gbi_ref, glr_ref, gli_ref):
                r[...] = jnp.zeros_like(r)

        ub = u_ref[...].astype(BF16)
        gbr_ref[...] += _dot(arb, ub, TN)
        gbi_ref[...] += _dot(aib, ub, TN)
        row0 = lax.broadcasted_iota(jnp.int32, (SL, SCB), 0) == 0
        last_r = jnp.where(i > 0, pr_ref[7:8, :], 0.0)
        last_i = jnp.where(i > 0, pi_ref[7:8, :], 0.0)
        xpr = jnp.where(row0, last_r, pltpu.roll(xr_ref[...], 1, 0))
        xpi = jnp.where(row0, last_i, pltpu.roll(xi_ref[...], 1, 0))
        glr_ref[...] += jnp.sum(ar * xpr + ai * xpi, axis=0, keepdims=True)
        gli_ref[...] += jnp.sum(ai * xpr - ar * xpi, axis=0, keepdims=True)

    xs = pl.BlockSpec((SL, SCB), lambda j, i: (i, j))
    prev = pl.BlockSpec((8, SCB), lambda j, i: (jnp.maximum(i * per - 1, 0), j))
    blk = pl.BlockSpec((None, SCB, 128), lambda j, i: (j, 0, 0))
    u128 = pl.BlockSpec((SL, 128), lambda j, i: (i, j))
    vec = pl.BlockSpec((1, SCB), lambda j, i: (0, j))
    return pl.pallas_call(
        body, name="s5_in_bwd", grid=(SB, n // SL),
        in_specs=[xs, xs, blk, blk, pl.BlockSpec((SL, 128), lambda j, i: (i, uoff + j)), u128, xs, xs, prev, prev],
        out_specs=[u128, blk, blk, vec, vec],
        out_shape=[jax.ShapeDtypeStruct((n, DS5), BF16)] + [jax.ShapeDtypeStruct((SB, SCB, 128), F32)] * 2
        + [jax.ShapeDtypeStruct((1, NCH), F32)] * 2,
        compiler_params=_cp("parallel", "arbitrary"),
    )(are, aim, brt, bit, proj, dus, xre, xim, xre, xim)


MC = 1024


def _merge_specs():
    ga = pl.BlockSpec((TL, MC), lambda i, j: (i, OFF_GA // MC + j))
    gb = pl.BlockSpec((TL, MC), lambda i, j: (i, OFF_GB // MC + j))
    col = pl.BlockSpec((TL, MC), lambda i, j: (i, j))
    gate = pl.BlockSpec((TL, MC), lambda i, j: (i, D // MC + j))
    return ga, gb, col, gate


def _merge_fwd(proj, ya, vg):
    n = ya.shape[0]

    def body(ga_ref, gb_ref, ya_ref, v_ref, g_ref, o_ref):
        yb = v_ref[...] * _sig(g_ref[...])
        o_ref[...] = (_sig(ga_ref[...]) * ya_ref[...] + _sig(gb_ref[...]) * yb).astype(BF16)

    ga, gb, col, gate = _merge_specs()
    return pl.pallas_call(
        body, name="merge_fwd", grid=(n // TL, D // MC), in_specs=[ga, gb, col, col, gate], out_specs=col,
        out_shape=jax.ShapeDtypeStruct((n, D), BF16), compiler_params=_cp("parallel", "parallel"),
    )(proj, proj, ya, vg, vg)


def _merge_bwd(dm, proj, ya, vg):
    n = ya.shape[0]

    def body(dm_ref, ga_ref, gb_ref, ya_ref, v_ref, g_ref, dga_ref, dgb_ref, dya_ref, dv_ref, dg_ref):
        d = dm_ref[...]
        sa, sb, sg = _sig(ga_ref[...]), _sig(gb_ref[...]), _sig(g_ref[...])
        v = v_ref[...]
        yb = v * sg
        dga_ref[...] = (d * ya_ref[...] * sa * (1.0 - sa)).astype(BF16)
        dgb_ref[...] = (d * yb * sb * (1.0 - sb)).astype(BF16)
        dya_ref[...] = (d * sa).astype(BF16)
        dyb = d * sb
        dv_ref[...] = (dyb * sg).astype(BF16)
        dg_ref[...] = (dyb * v * sg * (1.0 - sg)).astype(BF16)

    ga, gb, col, gate = _merge_specs()
    o = jax.ShapeDtypeStruct((n, D), BF16)
    return pl.pallas_call(
        body, name="merge_bwd", grid=(n // TL, D // MC), in_specs=[col, ga, gb, col, col, gate],
        out_specs=[col] * 5, out_shape=[o] * 5, compiler_params=_cp("parallel", "parallel"),
    )(dm, proj, proj, ya, vg, vg)


def _adamw_update(wv, gv, mv, vv):
    nm = B1 * mv + (1.0 - B1) * gv
    nv = B2 * vv + (1.0 - B2) * (gv * gv)
    m_hat = nm / (1.0 - B1 ** STEP)
    v_hat = nv / (1.0 - B2 ** STEP)
    return -LR * (m_hat / (jnp.sqrt(v_hat) + AEPS) + WD * wv), nm, nv


def _adamw(w, g, m, v, name):
    r, c = w.shape
    tr = _pick(r, 128)

    def body(w_ref, g_ref, m_ref, v_ref, d_ref, nm_ref, nv_ref):
        d_ref[...], nm_ref[...], nv_ref[...] = _adamw_update(w_ref[...], g_ref[...], m_ref[...], v_ref[...])

    blk = pl.BlockSpec((tr, c), lambda i: (i, 0))
    o = jax.ShapeDtypeStruct((r, c), F32)
    return pl.pallas_call(
        body, name=name, grid=(r // tr,), in_specs=[blk] * 4, out_specs=[blk] * 3, out_shape=[o] * 3,
        compiler_params=_cp("parallel"),
    )(w, g, m, v)


def _adamw_halves(w, g_mine, g_other, m, v, cidx, name):
    _, r, c = w.shape
    hr, gc = g_mine.shape
    tr = _pick(hr, 128)
    nbh = hr // tr
    assert gc == c and 2 * hr - tr < r <= 2 * hr

    def body(cs, w_ref, gm_ref, go_ref, m_ref, v_ref, g_ref, d_ref, nm_ref, nv_ref):
        mine = pl.program_id(0) // nbh == cs[0]
        gv = jnp.where(mine, gm_ref[...], go_ref[...])
        g_ref[...] = gv
        d_ref[...], nm_ref[...], nv_ref[...] = _adamw_update(w_ref[...], gv, m_ref[...], v_ref[...])

    blk = pl.BlockSpec((None, tr, c), lambda i, cs: (0, i, 0))
    gblk = pl.BlockSpec((tr, gc), lambda i, cs: (i % nbh, 0))
    o = jax.ShapeDtypeStruct((1, r, c), F32)
    return pl.pallas_call(
        body, name=name,
        grid_spec=pltpu.PrefetchScalarGridSpec(num_scalar_prefetch=1, grid=(2 * nbh,),
                                               in_specs=[blk, gblk, gblk, blk, blk], out_specs=[blk] * 4),
        out_shape=[o] * 4, compiler_params=_cp("parallel"),
    )(cidx, w, g_mine, g_other, m, v)


def _chip_sum(part, sib, cidx, name):
    _, r, cc = part.shape
    hr = r // 2
    tr = _pick(hr, 256)

    def body(cs, p_ref, s_ref, o_ref):
        o_ref[...] = (p_ref[...].astype(F32) + s_ref[...].astype(F32)).astype(BF16)

    blk = pl.BlockSpec((None, tr, cc), lambda k, i, cs: (k, i, 0))
    return pl.pallas_call(
        body, name=name,
        grid_spec=pltpu.PrefetchScalarGridSpec(
            num_scalar_prefetch=1, grid=(4, hr // tr),
            in_specs=[pl.BlockSpec((None, None, tr, cc), lambda k, i, cs: (k, cs[0], i, 0)), blk], out_specs=blk),
        out_shape=jax.ShapeDtypeStruct((4, hr, cc), BF16), compiler_params=_cp("parallel", "parallel"),
    )(cidx, part.reshape(4, 2, hr, cc), sib)


def _shard_sum(own, got, sidx, name):
    _, hr, cc = own.shape
    tr = _pick(hr, 256)

    def body(cs, own_ref, g0, g1, g2, g3, o_ref):
        acc = None
        for k, g_ref in enumerate((g0, g1, g2, g3)):
            term = jnp.where(cs[0] == k, own_ref[...], g_ref[...]).astype(F32)
            acc = term if acc is None else acc + term
        o_ref[...] = acc

    def got_spec(k):
        return pl.BlockSpec((None, tr, cc), lambda i, cs: (jnp.where(cs[0] == k, (k + 1) % 4, k), i, 0))

    return pl.pallas_call(
        body, name=name,
        grid_spec=pltpu.PrefetchScalarGridSpec(
            num_scalar_prefetch=1, grid=(hr // tr,),
            in_specs=[pl.BlockSpec((None, tr, cc), lambda i, cs: (cs[0], i, 0))] + [got_spec(k) for k in range(4)],
            out_specs=pl.BlockSpec((tr, cc), lambda i, cs: (i, 0))),
        out_shape=jax.ShapeDtypeStruct((hr, cc), F32), compiler_params=_cp("parallel"),
    )(sidx, own, got, got, got, got)


def _sum_slabs(xs, name, out_dtype=F32):
    r, c = xs[0].shape
    tr = _pick(r, 256)

    def body(*refs):
        acc = refs[0][...].astype(F32)
        for ref in refs[1:-1]:
            acc = acc + ref[...].astype(F32)
        refs[-1][...] = acc.astype(out_dtype)

    blk = pl.BlockSpec((tr, c), lambda i: (i, 0))
    return pl.pallas_call(
        body, name=name, grid=(r // tr,), in_specs=[blk] * len(xs), out_specs=blk,
        out_shape=jax.ShapeDtypeStruct((r, c), out_dtype), compiler_params=_cp("parallel"),
    )(*xs)


def _place():
    return lax.axis_index("x"), lax.axis_index("y"), lax.axis_index("c")


def _gather_small(v):
    m_per, n = v.shape

    def body(x_ref, out_ref, send_sems, recv_sems, local_sem):
        x, y, c = _place()
        me, sibling = (x, y, c), (x, y, 1 - c)
        chips = [(1 - x, y), (x, 1 - y), (1 - x, 1 - y)]

        def rows(px, py, pc):
            return out_ref.at[pl.ds((4 * px + 2 * py + pc) * m_per, m_per), :]

        def copy(k, block, to, src=None):
            return pltpu.make_async_remote_copy(
                src_ref=rows(*block) if src is None else src, dst_ref=rows(*block),
                send_sem=send_sems.at[k], recv_sem=recv_sems.at[k], device_id=to, device_id_type=MESH)

        mine = pltpu.make_async_copy(x_ref, rows(*me), local_sem)
        mine.start()
        first = [copy(0, me, sibling, src=x_ref)]
        first += [copy(1 + j, me, (*chip, c), src=x_ref) for j, chip in enumerate(chips)]
        for cp in first:
            cp.start()
        passed = [copy(4 + j, (*chip, c), sibling) for j, chip in enumerate(chips)]
        for j, chip in enumerate(chips):
            copy(1 + j, (*chip, c), me).wait_recv()
            passed[j].start()
        copy(0, sibling, me).wait_recv()
        for j, chip in enumerate(chips):
            copy(4 + j, (*chip, 1 - c), me).wait_recv()
        for cp in first + passed:
            cp.wait_send()
        mine.wait()

    return pl.pallas_call(
        body, name="gather_small_%d" % m_per,
        out_shape=jax.ShapeDtypeStruct((8 * m_per, n), v.dtype),
        in_specs=[pl.BlockSpec(memory_space=pltpu.VMEM)], out_specs=pl.BlockSpec(memory_space=pltpu.VMEM),
        scratch_shapes=[pltpu.SemaphoreType.DMA((7,)), pltpu.SemaphoreType.DMA((7,)), pltpu.SemaphoreType.DMA],
        compiler_params=pltpu.CompilerParams(vmem_limit_bytes=VMEM_LIMIT),
    )(v)


def _allsum_small(v, name):
    r = v.shape[0]
    g = _gather_small(v)
    return _sum_slabs([g[k * r:(k + 1) * r] for k in range(8)], name)


def _gather_big(shards):
    nt = len(shards)

    def body(*refs):
        ins, outs = refs[:nt], refs[nt:2 * nt]
        send_sems, recv_sems = refs[2 * nt:]
        x, y, c = _place()
        s = 2 * x + y
        sibling = (x, y, 1 - c)
        chips = [(1 - x, y), (x, 1 - y), (1 - x, 1 - y)]

        def half(t, slot, h):
            hr = ins[t].shape[0] // 2
            return outs[t].at[slot, pl.ds(h * hr, hr), :]

        def ici(t, j, src_slot, to):
            hr = ins[t].shape[0] // 2
            return pltpu.make_async_remote_copy(
                src_ref=ins[t].at[pl.ds(c * hr, hr), :], dst_ref=half(t, src_slot, c),
                send_sem=send_sems.at[7 * t + j], recv_sem=recv_sems.at[7 * t + j], device_id=to, device_id_type=MESH)

        def d2d(t, j, slot, h):
            return pltpu.make_async_remote_copy(
                src_ref=half(t, slot, h), dst_ref=half(t, slot, h),
                send_sem=send_sems.at[7 * t + 3 + j], recv_sem=recv_sems.at[7 * t + 3 + j],
                device_id=sibling, device_id_type=MESH)

        def whole(t):
            return pltpu.make_async_remote_copy(
                src_ref=ins[t], dst_ref=outs[t].at[s], send_sem=send_sems.at[7 * t + 6],
                recv_sem=recv_sems.at[7 * t + 6], device_id=sibling, device_id_type=MESH)

        sends = [ici(t, j, s, (*chip, c)) for t in range(nt) for j, chip in enumerate(chips)]
        sends += [whole(t) for t in range(nt)]
        for cp in sends:
            cp.start()
        passed = []
        for t in range(nt):
            for j, (px, py) in enumerate(chips):
                ici(t, j, 2 * px + py, (x, y, c)).wait_recv()
                cp = d2d(t, j, 2 * px + py, c)
                cp.start()
                passed.append(cp)
        for t in range(nt):
            for j, (px, py) in enumerate(chips):
                d2d(t, j, 2 * px + py, 1 - c).wait_recv()
            whole(t).wait_recv()
        for cp in sends + passed:
            cp.wait_send()

    return pl.pallas_call(
        body, name="gather_big",
        out_shape=[jax.ShapeDtypeStruct((4,) + a.shape, a.dtype) for a in shards],
        in_specs=[ANY] * nt, out_specs=[ANY] * nt,
        scratch_shapes=[pltpu.SemaphoreType.DMA((7 * nt,)), pltpu.SemaphoreType.DMA((7 * nt,))],
    )(*shards)


def _swap_halves(parts, name):
    nt = len(parts)

    def body(*refs):
        ins, outs = refs[:nt], refs[nt:2 * nt]
        send_sems, recv_sems = refs[2 * nt:]
        x, y, c = _place()
        cps = []
        for t in range(nt):
            hr = ins[t].shape[1] // 2
            cps.append(pltpu.make_async_remote_copy(
                src_ref=ins[t].at[:, pl.ds((1 - c) * hr, hr), :], dst_ref=outs[t],
                send_sem=send_sems.at[t], recv_sem=recv_sems.at[t], device_id=(x, y, 1 - c), device_id_type=MESH))
        for cp in cps:
            cp.start()
        for cp in cps:
            cp.wait()

    return pl.pallas_call(
        body, name=name,
        out_shape=[jax.ShapeDtypeStruct((4, a.shape[1] // 2, a.shape[2]), a.dtype) for a in parts],
        in_specs=[ANY] * nt, out_specs=[ANY] * nt,
        scratch_shapes=[pltpu.SemaphoreType.DMA((nt,)), pltpu.SemaphoreType.DMA((nt,))],
    )(*parts)


def _scatter_chips(parts):
    nt = len(parts)

    def body(*refs):
        ins, outs = refs[:nt], refs[nt:2 * nt]
        send_sems, recv_sems = refs[2 * nt:]
        x, y, c = _place()
        s = 2 * x + y
        chips = [(1 - x, y), (x, 1 - y), (1 - x, 1 - y)]
        cps = []
        for t in range(nt):
            for j, (px, py) in enumerate(chips):
                cps.append(pltpu.make_async_remote_copy(
                    src_ref=ins[t].at[2 * px + py], dst_ref=outs[t].at[s],
                    send_sem=send_sems.at[3 * t + j], recv_sem=recv_sems.at[3 * t + j],
                    device_id=(px, py, c), device_id_type=MESH))
        for cp in cps:
            cp.start()
        for t in range(nt):
            for j, (px, py) in enumerate(chips):
                pltpu.make_async_remote_copy(
                    src_ref=ins[t].at[s], dst_ref=outs[t].at[2 * px + py],
                    send_sem=send_sems.at[3 * t + j], recv_sem=recv_sems.at[3 * t + j],
                    device_id=(px, py, c), device_id_type=MESH).wait_recv()
        for cp in cps:
            cp.wait_send()

    return pl.pallas_call(
        body, name="scatter_chips",
        out_shape=[jax.ShapeDtypeStruct(a.shape, a.dtype) for a in parts],
        in_specs=[ANY] * nt, out_specs=[ANY] * nt,
        scratch_shapes=[pltpu.SemaphoreType.DMA((3 * nt,)), pltpu.SemaphoreType.DMA((3 * nt,))],
    )(*parts)


def _swap_whole(halves):
    nt = len(halves)

    def body(*refs):
        ins, outs = refs[:nt], refs[nt:2 * nt]
        send_sems, recv_sems = refs[2 * nt:]
        x, y, c = _place()
        cps = [pltpu.make_async_remote_copy(
            src_ref=ins[t], dst_ref=outs[t], send_sem=send_sems.at[t], recv_sem=recv_sems.at[t],
            device_id=(x, y, 1 - c), device_id_type=MESH) for t in range(nt)]
        for cp in cps:
            cp.start()
        for cp in cps:
            cp.wait()

    return pl.pallas_call(
        body, name="swap_whole",
        out_shape=[jax.ShapeDtypeStruct(a.shape, a.dtype) for a in halves],
        in_specs=[ANY] * nt, out_specs=[ANY] * nt,
        scratch_shapes=[pltpu.SemaphoreType.DMA((nt,)), pltpu.SemaphoreType.DMA((nt,))],
    )(*halves)


def _pass_halves(got, shards):
    nt = len(got)

    def body(*refs):
        ins, own, outs = refs[:nt], refs[nt:2 * nt], refs[2 * nt:3 * nt]
        send_sems, recv_sems = refs[3 * nt:]
        x, y, c = _place()
        s = 2 * x + y
        chips = [(1 - x, y), (x, 1 - y), (1 - x, 1 - y)]

        def half(ref, t, slot, h):
            hr = ins[t].shape[1] // 2
            return ref.at[slot, pl.ds(h * hr, hr), :]

        def copy(t, j, h):
            px, py = chips[j]
            return pltpu.make_async_remote_copy(
                src_ref=half(ins[t], t, 2 * px + py, h), dst_ref=half(outs[t], t, 2 * px + py, h),
                send_sem=send_sems.at[4 * t + j], recv_sem=recv_sems.at[4 * t + j],
                device_id=(x, y, 1 - c), device_id_type=MESH)

        def whole(t):
            return pltpu.make_async_remote_copy(
                src_ref=own[t], dst_ref=outs[t].at[s], send_sem=send_sems.at[4 * t + 3],
                recv_sem=recv_sems.at[4 * t + 3], device_id=(x, y, 1 - c), device_id_type=MESH)

        sends = [copy(t, j, c) for t in range(nt) for j in range(3)] + [whole(t) for t in range(nt)]
        for cp in sends:
            cp.start()
        for t in range(nt):
            for j in range(3):
                copy(t, j, 1 - c).wait_recv()
            whole(t).wait_recv()
        for cp in sends:
            cp.wait_send()

    return pl.pallas_call(
        body, name="pass_halves",
        out_shape=[jax.ShapeDtypeStruct(a.shape, a.dtype) for a in got],
        in_specs=[ANY] * (2 * nt), out_specs=[ANY] * nt, input_output_aliases={t: t for t in range(nt)},
        scratch_shapes=[pltpu.SemaphoreType.DMA((4 * nt,)), pltpu.SemaphoreType.DMA((4 * nt,))],
    )(*got, *shards)


HBM = pl.BlockSpec(memory_space=pltpu.HBM)
SEM = pl.BlockSpec(memory_space=pltpu.SEMAPHORE)
EFFECT = pltpu.SideEffectType.DATAFLOW_SIDE_EFFECTING


def _ici_copies(kind, srcs, lands, send_sems, recv_sems):
    x, y, c = _place()
    s = 2 * x + y
    cps = []
    for t in range(len(srcs)):
        for j, (px, py) in enumerate([(1 - x, y), (x, 1 - y), (1 - x, 1 - y)]):
            if kind == "gather":
                hr = srcs[t].shape[0] // 2
                src, dst = srcs[t].at[pl.ds(c * hr, hr), :], lands[t].at[s, pl.ds(c * hr, hr), :]
            else:
                src, dst = srcs[t].at[2 * px + py], lands[t].at[s]
            cps.append(pltpu.make_async_remote_copy(
                src_ref=src, dst_ref=dst, send_sem=send_sems[3 * t + j], recv_sem=recv_sems[3 * t + j],
                device_id=(px, py, c), device_id_type=MESH))
    return cps


def _ici_start(kind, srcs, after, name):
    nt = len(srcs)
    nc = 3 * nt
    lands = [(4,) + a.shape if kind == "gather" else a.shape for a in srcs]

    def body(*refs):
        src, land = refs[:nt], refs[nt:2 * nt]
        outs = refs[2 * nt + 1:]
        for cp in _ici_copies(kind, src, land, outs[:nc], outs[nc:2 * nc]):
            cp.start()
        outs[-1][...] = jnp.zeros_like(outs[-1])

    hbm = lambda a: pltpu.with_memory_space_constraint(a, pltpu.HBM)
    outs = pl.pallas_call(
        body, name=name,
        out_shape=tuple([pltpu.SemaphoreType.DMA(())] * (2 * nc) + [pltpu.HBM(a.shape, a.dtype) for a in srcs]
                        + [pltpu.HBM(sh, a.dtype) for sh, a in zip(lands, srcs)] + [jax.ShapeDtypeStruct((8, 128), F32)]),
        in_specs=[HBM] * (2 * nt) + [ANY],
        out_specs=tuple([SEM] * (2 * nc) + [HBM] * (2 * nt) + [pl.BlockSpec(memory_space=pltpu.VMEM)]),
        input_output_aliases={i: 2 * nc + i for i in range(2 * nt)},
        compiler_params=pltpu.CompilerParams(has_side_effects=EFFECT),
    )(*[hbm(a) for a in srcs], *[hbm(lax.empty(sh, a.dtype)) for sh, a in zip(lands, srcs)], after)
    return outs[:2 * nc], outs[2 * nc:2 * nc + nt], outs[2 * nc + nt:2 * nc + 2 * nt], outs[-1]


def _ici_wait(kind, sems, srcs, lands, after, name):
    nt = len(srcs)
    nc = 3 * nt

    def body(*refs):
        src, land = refs[:nt], refs[nt:2 * nt]
        sem = refs[2 * nt:2 * nt + 2 * nc]
        for cp in _ici_copies(kind, src, land, sem[:nc], sem[nc:]):
            cp.wait_send()
            cp.wait_recv()

    outs = pl.pallas_call(
        body, name=name,
        out_shape=tuple(pltpu.HBM(a.shape, a.dtype) for a in list(srcs) + list(lands)),
        in_specs=[HBM] * (2 * nt) + [SEM] * (2 * nc) + [ANY], out_specs=tuple([HBM] * (2 * nt)),
        input_output_aliases={i: i for i in range(2 * nt)},
        compiler_params=pltpu.CompilerParams(has_side_effects=EFFECT),
    )(*srcs, *lands, *sems, after)
    return outs[:nt], outs[nt:]


def _s5_params(lam_re, lam_im, log_dt, b_re, b_im):
    lr = jnp.minimum(lam_re, EIG_MAX)
    dt = jnp.exp(log_dt)[:, None]
    mag = jnp.exp(lr * dt)
    lbr, lbi = mag * jnp.cos(lam_im * dt), mag * jnp.sin(lam_im * dt)
    den = lr * lr + lam_im * lam_im
    qr = ((lbr - 1.0) * lr + lbi * lam_im) / den
    qi = (lbi * lr - (lbr - 1.0) * lam_im) / den
    bbr = qr[..., None] * b_re - qi[..., None] * b_im
    bbi = qr[..., None] * b_im + qi[..., None] * b_re
    return lbr, lbi, bbr, bbi


def _cmul(a, b):
    return a[0] * b[0] - a[1] * b[1], a[0] * b[1] + a[1] * b[0]


def _scan_table(lr, li, reverse):
    l1 = (lr.reshape(1, NCH), li.reshape(1, NCH))
    pows = [l1]
    for _ in range(7):
        pows.append(_cmul(pows[-1], l1))
    r = jnp.arange(8)[:, None]
    tabs = []
    for k in (1, 2, 4):
        keep = (r < 8 - k) if reverse else (r >= k)
        tabs += [jnp.where(keep, pows[k - 1][0], 0.0), jnp.where(keep, pows[k - 1][1], 0.0)]
    order = range(7, -1, -1) if reverse else range(8)
    tabs += [jnp.concatenate([pows[e][0] for e in order], axis=0), jnp.concatenate([pows[e][1] for e in order], axis=0)]
    return jnp.stack(tabs).astype(F32)


_EYE8 = lambda: jnp.eye(8, dtype=F32)


def _to_in_blocks(b):
    return jnp.einsum("jgpc,gh->jgchp", b.reshape(8, 8, 64, 16), _EYE8()).reshape(8, 128, 512)


def _to_out_blocks(cm):
    return jnp.einsum("jgcp,gh->jgphc", cm.reshape(8, 8, 16, 64), _EYE8()).reshape(8, 512, 128)


def _from_out_blocks(g):
    return jnp.einsum("jgphc,gh->jgpc", g.reshape(8, 8, 64, 8, 16), _EYE8()).reshape(64, 64, 16)


def _local_step(x, target, p, token, late_weights, send_grads):
    n = x.shape[0]
    g = {}
    hn1 = _rms_fwd(x, p["norm_mix_w"], "rms_mix", after=token)
    proj = _matmul(hn1, p["w_main"], "nt", "mm_in")
    dtraw = _matmul(hn1, p["w_dt"], "nt", "mm_dt")
    xbc = _conv_a_fwd(proj, p["conv_a_w"], p["conv_a_b"])
    to_lanes = lambda v: jnp.pad(jnp.pad(v.reshape(NG, HPG), ((0, 0), (0, 8 - HPG))).reshape(1, 8 * NG),
                                 ((0, 0), (0, HL - 8 * NG)))
    from_lanes = lambda v: v[:, :8 * NG].reshape(-1, NG, 8)[:, :, :HPG].reshape(-1, NG * HPG)
    hp = jnp.concatenate([to_lanes(p["dt_bias"]), to_lanes(p["a_log"]), jnp.zeros((6, HL), F32)], axis=0)
    lane = jnp.arange(HL)[:, None]
    emat = ((lane < 8 * NG) & (lane % 8 < HPG)
            & (jnp.arange(DI)[None, :] // HD == HPG * (lane // 8) + lane % 8)).astype(BF16)
    dexp = jnp.repeat(p["d_a"].reshape(1, NG * HPG), HD, axis=1)
    dt, s_cum, s_t, dt_e, s_e = _ssd_prep(dtraw, hp, emat)
    s8 = s_cum[:, :8 * NG].reshape(n, NG, 8).transpose(1, 0, 2)
    yssd, sprev = _ssd_fwd(xbc, s8, s_t, dt_e, s_e, dexp)
    yn = _gnorm_fwd(yssd, proj, p["norm_a_w"])
    p = {**p, **late_weights(yn)}
    ya = _matmul(yn, p["w_proj_a"], "nn", "mm_proj")

    (lbr, lbi, bbr, bbi), s5_vjp = jax.vjp(_s5_params, p["s5_lam_re"], p["s5_lam_im"], p["s5_log_dt"],
                                           p["s5_b_re"], p["s5_b_im"])
    bin_r, bin_i = _to_in_blocks(bbr), _to_in_blocks(bbi)
    cout_r, cout_in = _to_out_blocks(p["s5_c_re"]), _to_out_blocks(-p["s5_c_im"])
    bur, bui = _s5_in(proj, bin_r.astype(BF16), bin_i.astype(BF16))
    xre, xim = _s5_scan(bur, bui, _scan_table(lbr, lbi, False), False, "s5_scan_fwd")
    ypre, g5 = _s5_out(xre, xim, cout_r.astype(BF16), cout_in.astype(BF16), proj, p["s5_d"])
    vg = _matmul(g5, p["w_s5_glu"], "nn", "mm_glu", b_stacked=True)
    merged = _merge_fwd(proj, ya, vg)
    h1 = _matmul(merged, p["w_out"], "nn", "mm_out", residual=x)
    hn2 = _rms_fwd(h1, p["norm_ffn_w"], "rms_ffn")
    up = _matmul(hn2, p["w_up"], "nn", "mm_up", tn=1408, b_stacked=True)
    act = _conv_ffn_fwd(up, p["conv_ffn_w"], p["conv_ffn_b"])
    h2 = _matmul(act, p["w_down"], "nn", "mm_down", tk=DFF // 2, residual=h1)
    dh2, dh2b, g["norm_final_w"], loss_blk = _final(h2, p["norm_final_w"], target)
    g["w_down"] = _matmul(act, dh2b, "tn", "mm_gw_down", out_dtype=BF16).reshape(4, DFF // 4, D)
    dact = _matmul(dh2b, p["w_down"], "nt", "mm_dact", out_dtype=BF16)
    dup, g["conv_ffn_w"], g["conv_ffn_b"] = _conv_ffn_bwd(up, dact, p["conv_ffn_w"], p["conv_ffn_b"])
    g["w_up"] = _matmul(hn2, dup, "tn", "mm_gw_up", out_dtype=BF16, tn=1408, out_stacked=True)
    tok = send_grads(["w_up", "w_down"], g, "s1")
    dhn2 = _matmul(dup, p["w_up"], "nt", "mm_dhn2", tk=2816, b_stacked=True, after=tok)
    dh1, dh1b, g["norm_ffn_w"] = _rms_bwd(dhn2, h1, p["norm_ffn_w"], dh2, "rms_ffn_bwd")
    g["w_out"] = _matmul(merged, dh1b, "tn", "mm_gw_out", out_dtype=BF16).reshape(4, D // 4, D)
    dmerged = _matmul(dh1b, p["w_out"], "nt", "mm_dmerged")
    dga, dgb, dya, dval, dgate = _merge_bwd(dmerged, proj, ya, vg)
    dvg = jnp.concatenate([dval, dgate], axis=1)
    g["w_s5_glu"] = _matmul(g5, dvg, "tn", "mm_gw_glu", out_dtype=BF16, out_stacked=True)
    dg5 = _matmul(dvg, p["w_s5_glu"], "nt", "mm_dg5", b_stacked=True)
    tr = lambda b: b.transpose(0, 2, 1)
    gxr, gxi, dus, gcr, gci, g["s5_d"] = _s5_out_bwd(dg5, ypre, tr(cout_r).astype(BF16), tr(cout_in).astype(BF16),
                                                     proj, p["s5_d"], xre, xim)
    are, aim = _s5_scan(gxr, gxi, _scan_table(lbr, -lbi, True), True, "s5_scan_bwd")
    du, gbr, gbi, glr, gli = _s5_in_bwd(are, aim, tr(bin_r).astype(BF16), tr(bin_i).astype(BF16), proj, dus, xre, xim)
    g["s5_c_re"] = _from_out_blocks(gcr).transpose(0, 2, 1)
    g["s5_c_im"] = _from_out_blocks(gci).transpose(0, 2, 1)
    (g["s5_lam_re"], g["s5_lam_im"], g["s5_log_dt"], g["s5_b_re"], g["s5_b_im"]) = s5_vjp(
        (glr.reshape(64, 64), gli.reshape(64, 64), _from_out_blocks(gbr), _from_out_blocks(gbi)))
    g["w_proj_a"] = _matmul(yn, dya, "tn", "mm_gw_proj", out_dtype=BF16).reshape(4, DI // 4, D)
    tok = send_grads(["w_proj_a", "w_s5_glu", "w_out"], g, "s2")
    dyn = _matmul(dya, p["w_proj_a"], "nt", "mm_dyn", after=tok)
    dyssd, dz, g["norm_a_w"] = _gnorm_bwd(dyn, yssd, proj, p["norm_a_w"])
    dxs, dbm, dcm, ds_e, ddt_e, tsum, dsh8, pd = _ssd_bwd(xbc, s8, s_t, dt_e, s_e, dexp, sprev, dyssd)
    dsh = jnp.pad(dsh8.transpose(1, 0, 2).reshape(n, 8 * NG), ((0, 0), (0, HL - 8 * NG)))
    draw, ps = _ssd_post(ds_e, ddt_e, tsum, dsh, dtraw, dt, hp, emat.T)
    g["dt_bias"] = from_lanes(ps[0:1])
    g["a_log"] = from_lanes(ps[1:2])
    g["d_a"] = pd.reshape(NG * HPG, HD).sum(axis=1).reshape(1, NG * HPG)
    ddt = from_lanes(draw).astype(BF16)
    dxbc_parts, gcw, gcb = [], [], []
    for arr, col0, nm in ((dxs, 0, "conv_a_bwd_x"), (dbm, DI, "conv_a_bwd_b"), (dcm, DI + NG * NS, "conv_a_bwd_c")):
        dpart, gw_, gb_ = _conv_a_bwd(proj, arr, p["conv_a_w"], p["conv_a_b"], col0, nm)
        dxbc_parts.append(dpart)
        gcw.append(gw_)
        gcb.append(gb_)
    g["conv_a_w"] = jnp.concatenate(gcw, axis=1)
    g["conv_a_b"] = jnp.concatenate(gcb, axis=1)
    dorig = jnp.concatenate([dz] + dxbc_parts + [ddt, du, dga, dgb], axis=1)
    dproj = jnp.concatenate([jnp.pad(dorig[:, WSH * k:WSH * (k + 1)], ((0, 0), (0, WPAD - WSH))) for k in range(4)],
                            axis=1)
    g["w_in"] = _matmul(dproj, hn1, "tn", "mm_gw_in", out_dtype=BF16, tm=896, tn=2048).reshape(4, WPAD, D)
    tok = send_grads(["w_in"], g, "s3")
    dhn1 = _matmul(dproj, p["w_in"], "nn", "mm_dhn1", tk=2688, after=tok)
    gx, _, g["norm_mix_w"] = _rms_bwd(dhn1, x, p["norm_mix_w"], dh1, "rms_mix_bwd")
    return loss_blk, gx, g


BIG = ["w_in", "w_proj_a", "w_s5_glu", "w_out", "w_up", "w_down"]
SMALL = ["norm_mix_w", "conv_a_w", "conv_a_b", "dt_bias", "a_log", "d_a", "norm_a_w", "s5_lam_re", "s5_lam_im",
         "s5_log_dt", "s5_b_re", "s5_b_im", "s5_c_re", "s5_c_im", "s5_d", "norm_ffn_w", "conv_ffn_w", "conv_ffn_b",
         "norm_final_w"]
ORDER = ["norm_mix_w", "w_in", "conv_a_w", "conv_a_b", "dt_bias", "a_log", "d_a", "norm_a_w", "w_proj_a", "s5_lam_re",
         "s5_lam_im", "s5_log_dt", "s5_b_re", "s5_b_im", "s5_c_re", "s5_c_im", "s5_d", "w_s5_glu", "w_out",
         "norm_ffn_w", "w_up", "conv_ffn_w", "conv_ffn_b", "w_down", "norm_final_w"]
IN_SPLIT = [DI, DI + CONVD, DI + CONVD + NG * HPG]
CONV_FULL = {"conv_a_w": (KA, CONVD), "conv_ffn_w": (KF, 2 * DFF)}


def _pack(arrs):
    flat = jnp.concatenate([a.reshape(-1).astype(F32) for a in arrs])
    total = flat.shape[0]
    padded = -(-total // 1024) * 1024
    return jnp.pad(flat, (0, padded - total)).reshape(padded // 128, 128)


def _unpack(block, shapes):
    flat = block.reshape(-1)
    out, at = [], 0
    for sh in shapes:
        size = math.prod(sh)
        out.append(flat[at:at + size].reshape(sh))
        at += size
    return out


def _stack_cols(a):
    return a.transpose(1, 0, 2).reshape(a.shape[1], 4 * a.shape[2])


def _unstack_cols(a):
    return a.reshape(a.shape[0], 4, a.shape[1] // 4).transpose(1, 0, 2)


def kernel(x, norm_mix_w, w_in, conv_a_w, conv_a_b, dt_bias, a_log, d_a, norm_a_w, w_proj_a, s5_lam_re, s5_lam_im, s5_log_dt, s5_b_re, s5_b_im, s5_c_re, s5_c_im, s5_d, w_s5_glu, w_out, norm_ffn_w, w_up, conv_ffn_w, conv_ffn_b, w_down, norm_final_w, loss_target, m_norm_mix_w, m_w_in, m_conv_a_w, m_conv_a_b, m_dt_bias, m_a_log, m_d_a, m_norm_a_w, m_w_proj_a, m_s5_lam_re, m_s5_lam_im, m_s5_log_dt, m_s5_b_re, m_s5_b_im, m_s5_c_re, m_s5_c_im, m_s5_d, m_w_s5_glu, m_w_out, m_norm_ffn_w, m_w_up, m_conv_ffn_w, m_conv_ffn_b, m_w_down, m_norm_final_w, v_norm_mix_w, v_w_in, v_conv_a_w, v_conv_a_b, v_dt_bias, v_a_log, v_d_a, v_norm_a_w, v_w_proj_a, v_s5_lam_re, v_s5_lam_im, v_s5_log_dt, v_s5_b_re, v_s5_b_im, v_s5_c_re, v_s5_c_im, v_s5_d, v_w_s5_glu, v_w_out, v_norm_ffn_w, v_w_up, v_conv_ffn_w, v_conv_ffn_b, v_w_down, v_norm_final_w):
    w = dict(norm_mix_w=norm_mix_w, w_in=w_in, conv_a_w=conv_a_w, conv_a_b=conv_a_b, dt_bias=dt_bias, a_log=a_log, d_a=d_a, norm_a_w=norm_a_w, w_proj_a=w_proj_a, s5_lam_re=s5_lam_re, s5_lam_im=s5_lam_im, s5_log_dt=s5_log_dt, s5_b_re=s5_b_re, s5_b_im=s5_b_im, s5_c_re=s5_c_re, s5_c_im=s5_c_im, s5_d=s5_d, w_s5_glu=w_s5_glu, w_out=w_out, norm_ffn_w=norm_ffn_w, w_up=w_up, conv_ffn_w=conv_ffn_w, conv_ffn_b=conv_ffn_b, w_down=w_down, norm_final_w=norm_final_w)
    m = dict(norm_mix_w=m_norm_mix_w, w_in=m_w_in, conv_a_w=m_conv_a_w, conv_a_b=m_conv_a_b, dt_bias=m_dt_bias, a_log=m_a_log, d_a=m_d_a, norm_a_w=m_norm_a_w, w_proj_a=m_w_proj_a, s5_lam_re=m_s5_lam_re, s5_lam_im=m_s5_lam_im, s5_log_dt=m_s5_log_dt, s5_b_re=m_s5_b_re, s5_b_im=m_s5_b_im, s5_c_re=m_s5_c_re, s5_c_im=m_s5_c_im, s5_d=m_s5_d, w_s5_glu=m_w_s5_glu, w_out=m_w_out, norm_ffn_w=m_norm_ffn_w, w_up=m_w_up, conv_ffn_w=m_conv_ffn_w, conv_ffn_b=m_conv_ffn_b, w_down=m_w_down, norm_final_w=m_norm_final_w)
    v = dict(norm_mix_w=v_norm_mix_w, w_in=v_w_in, conv_a_w=v_conv_a_w, conv_a_b=v_conv_a_b, dt_bias=v_dt_bias, a_log=v_a_log, d_a=v_d_a, norm_a_w=v_norm_a_w, w_proj_a=v_w_proj_a, s5_lam_re=v_s5_lam_re, s5_lam_im=v_s5_lam_im, s5_log_dt=v_s5_log_dt, s5_b_re=v_s5_b_re, s5_b_im=v_s5_b_im, s5_c_re=v_s5_c_re, s5_c_im=v_s5_c_im, s5_d=v_s5_d, w_s5_glu=v_w_s5_glu, w_out=v_w_out, norm_ffn_w=v_norm_ffn_w, w_up=v_w_up, conv_ffn_w=v_conv_ffn_w, conv_ffn_b=v_conv_ffn_b, w_down=v_w_down, norm_final_w=v_norm_final_w)
    xi, yi, ci = _place()
    chip = 2 * xi + yi

    cidx = jnp.reshape(ci, (1,)).astype(jnp.int32)
    sidx = jnp.reshape(chip, (1,)).astype(jnp.int32)

    tw = lambda a: jnp.transpose(a[0])[None]
    w["w_in"], m["w_in"], v["w_in"] = tw(w_in), tw(m_w_in), tw(v_w_in)
    shards = [w[k][0].astype(BF16) for k in BIG]
    shards[0] = jnp.pad(shards[0], ((0, WPAD - WSH), (0, 0)))
    w_in_full = _gather_big(shards[:1])[0]
    g_sems, g_srcs, g_lands, token = _ici_start("gather", shards[1:], w_in_full, "gather_rest_start")

    def late_weights(after):
        srcs, got = _ici_wait("gather", g_sems, g_srcs, g_lands, after, "gather_rest_wait")
        full = _pass_halves(list(got), list(srcs))
        return {"w_proj_a": full[0].reshape(DI, D), "w_s5_glu": full[1], "w_out": full[2].reshape(D, D),
                "w_up": full[3], "w_down": full[4].reshape(DFF, D)}

    pending = []

    def send_grads(names, g, tag):
        parts = [g[k] for k in names]
        sib = _swap_halves(parts, "swap_halves_" + tag)
        sums = [_chip_sum(parts[t], sib[t], cidx, "chip_sum_" + k) for t, k in enumerate(names)]
        sems, srcs, lands, tok = _ici_start("scatter", sums, cidx, "scatter_start_" + tag)
        pending.append((names, tag, sems, srcs, lands))
        return tok
    conv_blocks = []
    for k, (taps, cols) in CONV_FULL.items():
        shard = jnp.where(ci == 0, w[k][0], 0.0)
        conv_blocks.append(lax.dynamic_update_slice_in_dim(jnp.zeros((taps, cols), F32), shard, chip * (cols // 4), 1))
    conv_full = _unpack(_allsum_small(_pack(conv_blocks), "sum_conv_w"), [CONV_FULL[k] for k in CONV_FULL])

    win = jnp.concatenate([w_in_full[k, :WSH] for k in range(4)], axis=0)
    p = {
        "w_main": jnp.concatenate([win[:IN_SPLIT[1]], win[IN_SPLIT[2]:]], axis=0),
        "w_dt": jnp.pad(jnp.pad(win[IN_SPLIT[1]:IN_SPLIT[2]].reshape(NG, HPG, D), ((0, 0), (0, 8 - HPG), (0, 0))
                                ).reshape(8 * NG, D), ((0, HL - 8 * NG), (0, 0))),
        "w_in": w_in_full.reshape(4 * WPAD, D),
        "conv_a_w": conv_full[0], "conv_ffn_w": conv_full[1],
        "conv_a_b": conv_a_b, "conv_ffn_b": conv_ffn_b,
        "norm_mix_w": norm_mix_w, "norm_a_w": norm_a_w, "norm_ffn_w": norm_ffn_w,
        "norm_final_w": norm_final_w.reshape(1, D),
        "dt_bias": dt_bias, "a_log": a_log, "d_a": d_a, "s5_d": s5_d,
        "s5_lam_re": s5_lam_re[0], "s5_lam_im": s5_lam_im[0], "s5_log_dt": s5_log_dt[0],
        "s5_b_re": s5_b_re[0], "s5_b_im": s5_b_im[0], "s5_c_re": s5_c_re[0], "s5_c_im": s5_c_im[0],
    }
    loss_blk, gx, g = _local_step(x[0], loss_target[0], p, token, late_weights, send_grads)

    after, halves = gx, {}
    for names, tag, sems, srcs, lands in pending:
        srcs, got = _ici_wait("scatter", sems, srcs, lands, after, "scatter_wait_" + tag)
        for t, k in enumerate(names):
            halves[k] = _shard_sum(srcs[t], got[t], sidx, "shard_sum_" + k)
        after = halves[names[0]]
    g_mine = [halves[k] for k in BIG]
    g_other = _swap_whole(g_mine)

    small_shapes = [CONV_FULL.get(k, w[k].shape[1:] if k != "norm_final_w" else w[k].shape) for k in SMALL]
    small = _allsum_small(_pack([g[k] for k in SMALL] + [loss_blk[0:1, 0:1]]), "sum_small_grads")
    small_grads = dict(zip(SMALL + ["loss"], _unpack(small, small_shapes + [(1,)])))
    for k, (taps, cols) in CONV_FULL.items():
        small_grads[k] = lax.dynamic_slice_in_dim(small_grads[k], chip * (cols // 4), cols // 4, axis=1)
    loss = small_grads.pop("loss").reshape(())

    grads, delta, new_m, new_v = {}, {}, {}, {}
    for t, k in enumerate(BIG):
        outs = _adamw_halves(w[k], g_mine[t], g_other[t], m[k], v[k], cidx, "adamw_" + k)
        grads[k], delta[k], new_m[k], new_v[k] = [tw(o) for o in outs] if k == "w_in" else outs
    for k in SMALL:
        grads[k] = small_grads[k].reshape(w[k].shape)
    pk = lambda t: _pack([t[k] for k in SMALL])
    d_, m_, v_ = _adamw(pk(w), pk(grads), pk(m), pk(v), "adamw_small")
    shapes = [w[k].shape for k in SMALL]
    for k, dd, mm, vv in zip(SMALL, _unpack(d_, shapes), _unpack(m_, shapes), _unpack(v_, shapes)):
        delta[k], new_m[k], new_v[k] = dd, mm, vv
    return (loss, gx[None], *[grads[k] for k in ORDER], *[delta[k] for k in ORDER],
            *[new_m[k] for k in ORDER], *[new_v[k] for k in ORDER])
```

```python
import functools
import math

import jax
import jax.numpy as jnp
from jax import lax
from jax.experimental import pallas as pl
from jax.experimental.pallas import tpu as pltpu

F32 = jnp.float32
BF16 = jnp.bfloat16
HI = lax.Precision.HIGHEST
MESH = pl.DeviceIdType.MESH
ANY = pl.BlockSpec(memory_space=pl.ANY)

D = 2048
DI = 3072
HD = 64
NG = 8
HPG = 6
GW = HPG * HD
NS = 128
KA = 4
Q = 256
CONVD = DI + 2 * NG * NS
DS5 = 1024
NCH = 4096
DFF = 5632
KF = 3
EPS = 1e-6
EIG_MAX = -1e-4
NMAIN = 13312
OFF_XBC, OFF_U, OFF_GA, OFF_GB = 3072, 8192, 9216, 11264
WSH = 3340
WPAD = 3360
IN_SPLIT = [DI, DI + CONVD, DI + CONVD + NG * HPG]
NFULL = NMAIN + 128
MT = 336


def _w_in_runs():
    runs = []
    for k in range(4):
        for o_lo, o_hi, m_lo in ((0, IN_SPLIT[1], 0), (IN_SPLIT[2], 4 * WSH, IN_SPLIT[1])):
            lo, hi = max(o_lo, WSH * k), min(o_hi, WSH * (k + 1))
            if lo < hi:
                runs.append((m_lo + lo - o_lo, m_lo + hi - o_lo, WPAD * k + lo - WSH * k))
    return runs


RUNS_TO_MAIN = _w_in_runs()
RUNS_TO_SHARDS = [(s_lo, s_lo + m_hi - m_lo, m_lo) for m_lo, m_hi, s_lo in RUNS_TO_MAIN]
DT_SHARD_ROW = WPAD * (IN_SPLIT[1] // WSH) + IN_SPLIT[1] % WSH
assert IN_SPLIT[1] // WSH == (IN_SPLIT[2] - 1) // WSH
VMEM_LIMIT = 56 * 1024 * 1024

LR, B1, B2, AEPS, WD, STEP = 0.001, 0.9, 0.999, 1e-08, 0.01, 10


def _cp(*sem):
    return pltpu.CompilerParams(dimension_semantics=sem, vmem_limit_bytes=VMEM_LIMIT)


def _sig(x):
    return jax.nn.sigmoid(x)


def _silu(x):
    return x * _sig(x)


def _dsilu(x):
    s = _sig(x)
    return s * (1.0 + x * (1.0 - s))


def _softplus(x):
    return jnp.maximum(x, 0.0) + jnp.log(1.0 + jnp.exp(-jnp.abs(x)))


_GC = math.sqrt(2.0 / math.pi)


def _gelu(x):
    return 0.5 * x * (1.0 + jnp.tanh(_GC * (x + 0.044715 * x * x * x)))


def _dgelu(x):
    t = jnp.tanh(_GC * (x + 0.044715 * x * x * x))
    return 0.5 * (1.0 + t) + 0.5 * x * (1.0 - t * t) * _GC * (1.0 + 3.0 * 0.044715 * x * x)


def _dot(a, b, dims=((1,), (0,)), prec=None):
    return lax.dot_general(a, b, (dims, ((), ())), precision=prec, preferred_element_type=F32)


NT = ((1,), (1,))
TN = ((0,), (0,))


def _pick(n, t):
    for unit in (128, 8):
        for cand in range(min(n, t) // unit * unit, 0, -unit):
            if n % cand == 0:
                return cand
    return n


def _matmul(a, b, mode, name, out_dtype=F32, tm=512, tn=1024, tk=2048, residual=None, b_stacked=False,
            out_stacked=False, after=None):
    if b_stacked:
        _, brows, bn = b.shape
        bshape = (brows, 4 * bn)
    else:
        bshape = b.shape
    if mode == "nn":
        (m, k), (k2, n) = a.shape, bshape
    elif mode == "nt":
        (m, k), (n, k2) = a.shape, bshape
    else:
        (k, m), (k2, n) = a.shape, bshape
    assert k == k2
    tm = _pick(m, tm)
    tn = _pick(n // 4 if (out_stacked or (b_stacked and mode != "nt")) else n, tn)
    tk = _pick(k // 4 if (b_stacked and mode == "nt") else k, tk)
    nk = k // tk
    dims = {"nn": ((1,), (0,)), "nt": NT, "tn": TN}[mode]
    has_res = residual is not None
    n_in = 2 + has_res + (after is not None)

    def body(*refs):
        a_ref, b_ref = refs[0], refs[1]
        r_ref = refs[2] if has_res else None
        o_ref = refs[n_in]
        p = _dot(a_ref[...], b_ref[...], dims)

        def finish(r):
            if has_res:
                r = r + r_ref[...]
            o_ref[...] = r.astype(out_dtype)

        if nk == 1:
            finish(p)
        else:
            acc = refs[-1]
            kk = pl.program_id(2)

            @pl.when(kk == 0)
            def _():
                acc[...] = p

            @pl.when(kk > 0)
            def _():
                acc[...] += p

            @pl.when(kk == nk - 1)
            def _():
                finish(acc[...])

    if mode == "tn":
        a_spec = pl.BlockSpec((tk, tm), lambda i, j, kk: (kk, i))
    else:
        a_spec = pl.BlockSpec((tm, tk), lambda i, j, kk: (i, kk))
    if mode == "nt":
        if b_stacked:
            per = bn // tk
            b_spec = pl.BlockSpec((None, tn, tk), lambda i, j, kk: (kk // per, j, kk % per))
        else:
            b_spec = pl.BlockSpec((tn, tk), lambda i, j, kk: (j, kk))
    elif b_stacked:
        per = bn // tn
        b_spec = pl.BlockSpec((None, tk, tn), lambda i, j, kk: (j // per, kk, j % per))
    else:
        b_spec = pl.BlockSpec((tk, tn), lambda i, j, kk: (kk, j))
    o_spec = pl.BlockSpec((tm, tn), lambda i, j, kk: (i, j))
    if out_stacked:
        per_o = n // 4 // tn
        out_spec = pl.BlockSpec((None, tm, tn), lambda i, j, kk: (j // per_o, i, j % per_o))
        out_shape = jax.ShapeDtypeStruct((4, m, n // 4), out_dtype)
    else:
        out_spec, out_shape = o_spec, jax.ShapeDtypeStruct((m, n), out_dtype)
    in_specs, args = [a_spec, b_spec], [a, b]
    if has_res:
        in_specs.append(o_spec)
        args.append(residual)
    if after is not None:
        in_specs.append(ANY)
        args.append(after)
    return pl.pallas_call(
        body, name=name, grid=(m // tm, n // tn, nk),
        in_specs=in_specs, out_specs=out_spec, out_shape=out_shape,
        scratch_shapes=[pltpu.VMEM((tm, tn), F32)] if nk > 1 else [],
        compiler_params=_cp("parallel", "parallel", "arbitrary"),
    )(*args)


def _move_rows(src, runs, rows_out, t_out, t_in, name):
    rows_in, cols = src.shape
    nb_out, nb_in = rows_out // t_out, rows_in // t_in
    assert rows_out % t_out == 0 and rows_in % t_in == 0 and t_in >= t_out
    blk, off, lo, hi = ([[0] * nb_out for _ in range(2)] for _ in range(4))
    for i in range(nb_out):
        hits = [r for r in runs if r[0] < (i + 1) * t_out and r[1] > i * t_out]
        assert len(hits) <= 2
        for s, (o_lo, o_hi, s_lo) in enumerate(hits):
            lo[s][i] = max(o_lo, i * t_out) - i * t_out
            hi[s][i] = min(o_hi, (i + 1) * t_out) - i * t_out
            first = i * t_out + lo[s][i] - o_lo + s_lo
            blk[s][i] = min(first // t_in, nb_in - 1)
            off[s][i] = first - lo[s][i] - blk[s][i] * t_in
    table = jnp.asarray([blk[0], off[0], lo[0], hi[0], blk[1], off[1], lo[1], hi[1]], jnp.int32)

    def body(tab, a0, a1, b0, b1, o_ref):
        i = pl.program_id(0)
        o_ref[...] = jnp.zeros_like(o_ref)
        r = lax.broadcasted_iota(jnp.int32, (t_out, t_in), 0)
        k = lax.broadcasted_iota(jnp.int32, (t_out, t_in), 1)
        for s, (first, second) in enumerate(((a0, a1), (b0, b1))):
            off_s, lo_s, hi_s = tab[4 * s + 1, i], tab[4 * s + 2, i], tab[4 * s + 3, i]
            live = (r >= lo_s) & (r < hi_s)

            @pl.when(hi_s > lo_s)
            def _():
                sel = (live & (k == r + off_s)).astype(BF16)
                o_ref[...] += _dot(sel, first[...]).astype(o_ref.dtype)

            @pl.when((hi_s > lo_s) & (off_s + hi_s > t_in))
            def _():
                sel = (live & (k == r + off_s - t_in)).astype(BF16)
                o_ref[...] += _dot(sel, second[...]).astype(o_ref.dtype)

    def in_spec(s, nxt):
        return pl.BlockSpec((t_in, cols), lambda i, tab: (jnp.minimum(tab[4 * s, i] + nxt, nb_in - 1), 0))

    return pl.pallas_call(
        body, name=name,
        grid_spec=pltpu.PrefetchScalarGridSpec(
            num_scalar_prefetch=1, grid=(nb_out,),
            in_specs=[in_spec(0, 0), in_spec(0, 1), in_spec(1, 0), in_spec(1, 1)],
            out_specs=pl.BlockSpec((t_out, cols), lambda i, tab: (i, 0))),
        out_shape=jax.ShapeDtypeStruct((rows_out, cols), src.dtype), compiler_params=_cp("parallel"),
    )(table, src, src, src, src)


TL = 256


def _rms_fwd(x, w, name, after=None):
    n, d = x.shape

    def body(x_ref, w_ref, *rest):
        xv = x_ref[...]
        r = lax.rsqrt(jnp.mean(xv * xv, axis=-1, keepdims=True) + EPS)
        rest[-1][...] = (xv * r * w_ref[...]).astype(BF16)

    extra = [] if after is None else [after]
    return pl.pallas_call(
        body, name=name, grid=(n // TL,),
        in_specs=[pl.BlockSpec((TL, d), lambda i: (i, 0)), pl.BlockSpec((1, d), lambda i: (0, 0))] + [ANY] * len(extra),
        out_specs=pl.BlockSpec((TL, d), lambda i: (i, 0)),
        out_shape=jax.ShapeDtypeStruct((n, d), BF16), compiler_params=_cp("parallel"),
    )(x, w, *extra)


def _rms_bwd(dhn, x, w, dres, name):
    n, d = x.shape

    def body(g_ref, x_ref, w_ref, r_ref, dx_ref, dxb_ref, gw_ref):
        xv = x_ref[...]
        r = lax.rsqrt(jnp.mean(xv * xv, axis=-1, keepdims=True) + EPS)
        xh = xv * r
        gv = g_ref[...]
        g = gv * w_ref[...]
        dx = r_ref[...] + r * (g - xh * jnp.mean(g * xh, axis=-1, keepdims=True))
        dx_ref[...] = dx
        dxb_ref[...] = dx.astype(BF16)

        @pl.when(pl.program_id(0) == 0)
        def _():
            gw_ref[...] = jnp.zeros_like(gw_ref)

        gw_ref[...] += jnp.sum(gv * xh, axis=0, keepdims=True)

    row = pl.BlockSpec((TL, d), lambda i: (i, 0))
    vec = pl.BlockSpec((1, d), lambda i: (0, 0))
    return pl.pallas_call(
        body, name=name, grid=(n // TL,),
        in_specs=[row, row, vec, row], out_specs=[row, row, vec],
        out_shape=[jax.ShapeDtypeStruct((n, d), F32), jax.ShapeDtypeStruct((n, d), BF16),
                   jax.ShapeDtypeStruct((1, d), F32)],
        compiler_params=_cp("arbitrary"),
    )(dhn, x, w, dres)


def _final(h2, w, target):
    n, d = h2.shape

    def body(x_ref, w_ref, t_ref, dx_ref, dxb_ref, gw_ref, loss_ref):
        xv = x_ref[...]
        r = lax.rsqrt(jnp.mean(xv * xv, axis=-1, keepdims=True) + EPS)
        xh = xv * r
        diff = xh * w_ref[...] - t_ref[...]
        gv = diff * (1.0 / d)
        g = gv * w_ref[...]
        dx = r * (g - xh * jnp.mean(g * xh, axis=-1, keepdims=True))
        dx_ref[...] = dx
        dxb_ref[...] = dx.astype(BF16)

        @pl.when(pl.program_id(0) == 0)
        def _():
            gw_ref[...] = jnp.zeros_like(gw_ref)
            loss_ref[...] = jnp.zeros_like(loss_ref)

        gw_ref[...] += jnp.sum(gv * xh, axis=0, keepdims=True)
        part = 0.5 * jnp.sum(jnp.mean(diff * diff, axis=-1, keepdims=True), axis=0, keepdims=True)
        loss_ref[...] += jnp.broadcast_to(part, loss_ref.shape)

    row = pl.BlockSpec((TL, d), lambda i: (i, 0))
    vec = pl.BlockSpec((1, d), lambda i: (0, 0))
    return pl.pallas_call(
        body, name="final_loss", grid=(n // TL,),
        in_specs=[row, vec, row], out_specs=[row, row, vec, pl.BlockSpec((8, 128), lambda i: (0, 0))],
        out_shape=[jax.ShapeDtypeStruct((n, d), F32), jax.ShapeDtypeStruct((n, d), BF16),
                   jax.ShapeDtypeStruct((1, d), F32), jax.ShapeDtypeStruct((8, 128), F32)],
        compiler_params=_cp("arbitrary"),
    )(h2, w, target)


CT = 512
CL = 512


def _lagged(xf, taps, rows):
    return [xf[8:8 + rows]] + [pltpu.roll(xf, s, 0)[8:8 + rows] for s in range(1, taps)]


def _shift_up(x, u, n):
    if u == 0:
        return x[0:n]
    return pltpu.roll(x, x.shape[0] - u, 0)[0:n]


def _conv_pre(lagged, w_ref, b_ref, taps):
    pre = b_ref[...]
    for k in range(taps):
        pre = pre + w_ref[k:k + 1, :] * lagged[taps - 1 - k]
    return pre


def _conv_back(e, w_ref, taps):
    dx = w_ref[taps - 1:taps, :] * e[0:CL]
    for k in range(taps - 1):
        dx = dx + w_ref[k:k + 1, :] * _shift_up(e, taps - 1 - k, CL)
    return dx


def _halo_specs(n, col_of):
    per = CL // 8
    cur = pl.BlockSpec((CL, CT), lambda j, i, *_: (i, col_of(j)))
    prev = pl.BlockSpec((8, CT), lambda j, i, *_: (jnp.maximum(i * per - 1, 0), col_of(j)))
    nxt = pl.BlockSpec((8, CT), lambda j, i, *_: (jnp.minimum((i + 1) * per, n // 8 - 1), col_of(j)))
    return prev, cur, nxt


def _conv_a_fwd(proj, w, b):
    n = proj.shape[0]
    off = OFF_XBC // CT

    def body(p_ref, x_ref, w_ref, b_ref, o_ref):
        p8 = jnp.where(pl.program_id(1) > 0, p_ref[...], 0.0)
        xf = jnp.concatenate([p8, x_ref[...]], axis=0)
        o_ref[...] = _silu(_conv_pre(_lagged(xf, KA, CL), w_ref, b_ref, KA))

    prev, cur, _ = _halo_specs(n, lambda j: j + off)
    return pl.pallas_call(
        body, name="conv_a_fwd", grid=(CONVD // CT, n // CL),
        in_specs=[prev, cur, pl.BlockSpec((KA, CT), lambda j, i: (0, j)), pl.BlockSpec((1, CT), lambda j, i: (0, j))],
        out_specs=pl.BlockSpec((CL, CT), lambda j, i: (i, j)),
        out_shape=jax.ShapeDtypeStruct((n, CONVD), F32), compiler_params=_cp("parallel", "parallel"),
    )(proj, proj, w, b)


def _conv_a_bwd(proj, dout, w, b, col0, name):
    n, width = dout.shape
    off = (OFF_XBC + col0) // CT
    woff = col0 // CT
    nl = n // CL

    def body(p_ref, x_ref, n_ref, d_ref, dn_ref, w_ref, b_ref, dx_ref, dw_ref, db_ref):
        i = pl.program_id(1)
        xf = jnp.concatenate([jnp.where(i > 0, p_ref[...], 0.0), x_ref[...], n_ref[...]], axis=0)
        lag = _lagged(xf, KA, CL + 8)
        de = jnp.concatenate([d_ref[...], jnp.where(i < nl - 1, dn_ref[...], 0.0)], axis=0)
        se = de * _dsilu(_conv_pre(lag, w_ref, b_ref, KA))
        dx_ref[...] = _conv_back(se, w_ref, KA).astype(BF16)

        @pl.when(i == 0)
        def _():
            dw_ref[...] = jnp.zeros_like(dw_ref)
            db_ref[...] = jnp.zeros_like(db_ref)

        sc = se[0:CL]
        for k in range(KA):
            dw_ref[k:k + 1, :] += jnp.sum(sc * lag[KA - 1 - k][0:CL], axis=0, keepdims=True)
        db_ref[...] += jnp.sum(sc, axis=0, keepdims=True)

    prev, cur, nxt = _halo_specs(n, lambda j: j + off)
    _, dcur, dnxt = _halo_specs(n, lambda j: j)
    wspec = pl.BlockSpec((KA, CT), lambda j, i: (0, j + woff))
    bspec = pl.BlockSpec((1, CT), lambda j, i: (0, j + woff))
    return pl.pallas_call(
        body, name=name, grid=(width // CT, nl),
        in_specs=[prev, cur, nxt, dcur, dnxt, wspec, bspec],
        out_specs=[pl.BlockSpec((CL, CT), lambda j, i: (i, j)), pl.BlockSpec((KA, CT), lambda j, i: (0, j)),
                   pl.BlockSpec((1, CT), lambda j, i: (0, j))],
        out_shape=[jax.ShapeDtypeStruct((n, width), BF16), jax.ShapeDtypeStruct((KA, width), F32),
                   jax.ShapeDtypeStruct((1, width), F32)],
        compiler_params=_cp("parallel", "arbitrary"),
    )(proj, proj, proj, dout, dout, w, b)


def _conv_ffn_fwd(up, w, b):
    n = up.shape[0]
    nb = DFF // CT

    def body(pg_ref, g_ref, pv_ref, v_ref, wg_ref, bg_ref, wv_ref, bv_ref, o_ref):
        inner = pl.program_id(1) > 0
        gf = jnp.concatenate([jnp.where(inner, pg_ref[...], 0.0), g_ref[...]], axis=0)
        vf = jnp.concatenate([jnp.where(inner, pv_ref[...], 0.0), v_ref[...]], axis=0)
        gc = _conv_pre(_lagged(gf, KF, CL), wg_ref, bg_ref, KF)
        vc = _conv_pre(_lagged(vf, KF, CL), wv_ref, bv_ref, KF)
        o_ref[...] = (_silu(gc) * vc).astype(BF16)

    gp, gcur, _ = _halo_specs(n, lambda j: j)
    vp, vcur, _ = _halo_specs(n, lambda j: j + nb)
    return pl.pallas_call(
        body, name="conv_ffn_fwd", grid=(nb, n // CL),
        in_specs=[gp, gcur, vp, vcur,
                  pl.BlockSpec((KF, CT), lambda j, i: (0, j)), pl.BlockSpec((1, CT), lambda j, i: (0, j)),
                  pl.BlockSpec((KF, CT), lambda j, i: (0, j + nb)), pl.BlockSpec((1, CT), lambda j, i: (0, j + nb))],
        out_specs=pl.BlockSpec((CL, CT), lambda j, i: (i, j)),
        out_shape=jax.ShapeDtypeStruct((n, DFF), BF16), compiler_params=_cp("parallel", "parallel"),
    )(up, up, up, up, w, b, w, b)


def _conv_ffn_bwd(up, dact, w, b):
    n = up.shape[0]
    nb = DFF // CT
    nl = n // CL

    def body(pg_ref, g_ref, ng_ref, pv_ref, v_ref, nv_ref, d_ref, dn_ref, wg_ref, bg_ref, wv_ref, bv_ref,
             dxg_ref, dxv_ref, dwg_ref, dwv_ref, dbg_ref, dbv_ref):
        i = pl.program_id(1)
        gf = jnp.concatenate([jnp.where(i > 0, pg_ref[...], 0.0), g_ref[...], ng_ref[...]], axis=0)
        vf = jnp.concatenate([jnp.where(i > 0, pv_ref[...], 0.0), v_ref[...], nv_ref[...]], axis=0)
        glag, vlag = _lagged(gf, KF, CL + 8), _lagged(vf, KF, CL + 8)
        de = jnp.concatenate([d_ref[...], jnp.where(i < nl - 1, dn_ref[...], 0.0)], axis=0).astype(F32)
        gc = _conv_pre(glag, wg_ref, bg_ref, KF)
        vc = _conv_pre(vlag, wv_ref, bv_ref, KF)
        sg = _sig(gc)
        dgc = de * vc * (sg * (1.0 + gc * (1.0 - sg)))
        dvc = de * (gc * sg)
        dxg_ref[...] = _conv_back(dgc, wg_ref, KF).astype(BF16)
        dxv_ref[...] = _conv_back(dvc, wv_ref, KF).astype(BF16)

        @pl.when(i == 0)
        def _():
            for r in (dwg_ref, dwv_ref, dbg_ref, dbv_ref):
                r[...] = jnp.zeros_like(r)

        for e, lag, dw_ref, db_ref in ((dgc, glag, dwg_ref, dbg_ref), (dvc, vlag, dwv_ref, dbv_ref)):
            ec = e[0:CL]
            for k in range(KF):
                dw_ref[k:k + 1, :] += jnp.sum(ec * lag[KF - 1 - k][0:CL], axis=0, keepdims=True)
            db_ref[...] += jnp.sum(ec, axis=0, keepdims=True)

    gp, gcur, gnx = _halo_specs(n, lambda j: j)
    vp, vcur, vnx = _halo_specs(n, lambda j: j + nb)
    wcol = lambda o: (pl.BlockSpec((KF, CT), lambda j, i: (0, j + o)), pl.BlockSpec((1, CT), lambda j, i: (0, j + o)))
    wg, bg = wcol(0)
    wv, bv = wcol(nb)
    dxs = pl.BlockSpec((CL, CT), lambda j, i: (i, j))
    outs = pl.pallas_call(
        body, name="conv_ffn_bwd", grid=(nb, nl),
        in_specs=[gp, gcur, gnx, vp, vcur, vnx, gcur, gnx, wg, bg, wv, bv],
        out_specs=[dxs, dxs, wg, wg, bg, bg],
        out_shape=[jax.ShapeDtypeStruct((n, DFF), BF16)] * 2 + [jax.ShapeDtypeStruct((KF, DFF), F32)] * 2
        + [jax.ShapeDtypeStruct((1, DFF), F32)] * 2,
        compiler_params=_cp("parallel", "arbitrary"),
    )(up, up, up, up, up, up, dact, dact, w, b, w, b)
    return [jnp.concatenate(outs[k:k + 2], axis=1) for k in (0, 2, 4)]


HL = 128


def _split3(x):
    hi = x.astype(BF16)
    r1 = x - hi.astype(F32)
    mid = r1.astype(BF16)
    return hi, mid, (r1 - mid.astype(F32)).astype(BF16)


def _dot3(x, m):
    hi, mid, lo = _split3(x)
    return _dot(hi, m) + _dot(mid, m) + _dot(lo, m)


def _tri():
    row = lax.broadcasted_iota(jnp.int32, (Q, Q), 0)
    col = lax.broadcasted_iota(jnp.int32, (Q, Q), 1)
    return row >= col, row <= col


def _ssd_prep(dtraw, hp, emat):
    n = dtraw.shape[0]

    def body(d_ref, hp_ref, e_ref, dt_ref, s_ref, st_ref, dte_ref, se_ref):
        lower, upper = _tri()
        dt = _softplus(d_ref[...] + hp_ref[0:1, :])
        da = dt * (-jnp.exp(hp_ref[1:2, :]))
        s = _dot(lower.astype(F32), da, prec=HI)
        dt_ref[...] = dt
        s_ref[...] = s
        st_ref[...] = _dot(da, upper.astype(F32), TN, prec=HI)
        e = e_ref[...]
        dte_ref[...] = _dot3(dt, e)
        se_ref[...] = _dot3(s, e)

    row = pl.BlockSpec((Q, HL), lambda c: (c, 0))
    wide = pl.BlockSpec((Q, DI), lambda c: (c, 0))
    return pl.pallas_call(
        body, name="ssd_prep", grid=(n // Q,),
        in_specs=[pl.BlockSpec((Q, HL), lambda c: (c, NMAIN // HL)), pl.BlockSpec((8, HL), lambda c: (0, 0)),
                  pl.BlockSpec((HL, DI), lambda c: (0, 0))],
        out_specs=[row, row, pl.BlockSpec((HL, Q), lambda c: (0, c)), wide, wide],
        out_shape=[jax.ShapeDtypeStruct((n, HL), F32)] * 2 + [jax.ShapeDtypeStruct((HL, n), F32)]
        + [jax.ShapeDtypeStruct((n, DI), F32)] * 2,
        compiler_params=_cp("parallel"),
    )(dtraw, hp, emat)


def _ssd_post(ds_e, ddt_e, tsum, dsh, dtraw, dt, hp, emat_t):
    n = dtraw.shape[0]

    def body(dse_ref, dde_ref, ts_ref, dsh_ref, d_ref, dt_ref, hp_ref, et_ref, draw_ref, ps_ref):
        _, upper = _tri()
        et = et_ref[...]
        a = -jnp.exp(hp_ref[1:2, :])
        rows = lax.broadcasted_iota(jnp.int32, (Q, HL), 0)
        ds_t = _dot3(jnp.broadcast_to(ts_ref[...], (8, DI)), et)[0:1, :]
        ds = _dot3(dse_ref[...], et) + dsh_ref[...] + jnp.where(rows == Q - 1, ds_t, 0.0)
        d_a = _dot(upper.astype(F32), ds, prec=HI)
        draw = (_dot3(dde_ref[...], et) + d_a * a) * _sig(d_ref[...] + hp_ref[0:1, :])
        draw_ref[...] = draw

        @pl.when(pl.program_id(0) == 0)
        def _():
            ps_ref[...] = jnp.zeros_like(ps_ref)

        ps_ref[0:1, :] += jnp.sum(draw, axis=0, keepdims=True)
        ps_ref[1:2, :] += jnp.sum(d_a * dt_ref[...], axis=0, keepdims=True) * a

    row = pl.BlockSpec((Q, HL), lambda c: (c, 0))
    wide = pl.BlockSpec((Q, DI), lambda c: (c, 0))
    small = pl.BlockSpec((8, HL), lambda c: (0, 0))
    return pl.pallas_call(
        body, name="ssd_post", grid=(n // Q,),
        in_specs=[wide, wide, pl.BlockSpec((None, 1, DI), lambda c: (c, 0, 0)), row,
                  pl.BlockSpec((Q, HL), lambda c: (c, NMAIN // HL)), row, small,
                  pl.BlockSpec((DI, HL), lambda c: (0, 0))],
        out_specs=[row, small],
        out_shape=[jax.ShapeDtypeStruct((n, HL), F32), jax.ShapeDtypeStruct((8, HL), F32)],
        compiler_params=_cp("arbitrary"),
    )(ds_e, ddt_e, tsum, dsh, dtraw, dt, hp, emat_t)


def _ssd_specs(nc, rev):
    cc = (lambda c: nc - 1 - c) if rev else (lambda c: c)
    return [
        pl.BlockSpec((Q, GW), lambda g, c: (cc(c), g)),
        pl.BlockSpec((Q, NS), lambda g, c: (cc(c), DI // NS + g)),
        pl.BlockSpec((Q, NS), lambda g, c: (cc(c), (DI + NG * NS) // NS + g)),
        pl.BlockSpec((None, Q, 8), lambda g, c: (g, cc(c), 0)),
        pl.BlockSpec((8, Q), lambda g, c: (g, cc(c))),
        pl.BlockSpec((Q, GW), lambda g, c: (cc(c), g)),
        pl.BlockSpec((Q, GW), lambda g, c: (cc(c), g)),
        pl.BlockSpec((1, GW), lambda g, c: (0, g)),
    ]


def _ssd_fwd(xbc, s8, s_t, dt_e, s_e, dexp):
    n = xbc.shape[0]
    nc = n // Q

    def body(xs_ref, b_ref, c_ref, sc_ref, sr_ref, dte_ref, se_ref, dexp_ref, y_ref, sp_ref, st):
        @pl.when(pl.program_id(1) == 0)
        def _():
            st[...] = jnp.zeros_like(st)

        lower, _ = _tri()
        s_c, s_r, dt_e, s_e = sc_ref[...], sr_ref[...], dte_ref[...], se_ref[...]
        xs = xs_ref[...]
        x = xs * dt_e
        xb = x.astype(BF16)
        bb, cb = b_ref[...].astype(BF16), c_ref[...].astype(BF16)
        cbm = _dot(cb, bb, NT)
        st_e = s_e[Q - 1:Q, :]
        sprev = st[...]
        sp_ref[...] = sprev
        yoff = _dot(cb, sprev.astype(BF16)) * jnp.exp(s_e) + dexp_ref[...] * xs
        for h in range(HPG):
            sl = slice(h * HD, (h + 1) * HD)
            lm = jnp.where(lower, jnp.exp(jnp.minimum(s_c[:, h:h + 1] - s_r[h:h + 1, :], 0.0)), 0.0)
            y_ref[:, sl] = _dot((cbm * lm).astype(BF16), xb[:, sl]) + yoff[:, sl]
        w = (x * jnp.exp(st_e - s_e)).astype(BF16)
        st[...] = jnp.exp(st_e) * sprev + _dot(bb, w, TN)

    return pl.pallas_call(
        body, name="ssd_fwd", grid=(NG, nc), in_specs=_ssd_specs(nc, False),
        out_specs=[pl.BlockSpec((Q, GW), lambda g, c: (c, g)),
                   pl.BlockSpec((None, None, NS, GW), lambda g, c: (c, g, 0, 0))],
        out_shape=[jax.ShapeDtypeStruct((n, DI), F32), jax.ShapeDtypeStruct((nc, NG, NS, GW), F32)],
        scratch_shapes=[pltpu.VMEM((NS, GW), F32)],
        compiler_params=_cp("parallel", "arbitrary"),
    )(xbc, xbc, xbc, s8, s_t, dt_e, s_e, dexp)


def _ssd_bwd(xbc, s8, s_t, dt_e, s_e, dexp, sprev_all, dy):
    n = xbc.shape[0]
    nc = n // Q
    rc = lambda c: nc - 1 - c

    def body(xs_ref, b_ref, c_ref, sc_ref, sr_ref, dte_ref, se_ref, dexp_ref, sp_ref, dy_ref,
             dxs_ref, db_ref, dc_ref, dse_ref, dde_ref, ts_ref, dsh_ref, pd_ref, dst, dxbuf):
        @pl.when(pl.program_id(1) == 0)
        def _():
            dst[...] = jnp.zeros_like(dst)
            pd_ref[...] = jnp.zeros_like(pd_ref)

        lower, upper = _tri()
        s_c, s_r, dt_e, s_e = sc_ref[...], sr_ref[...], dte_ref[...], se_ref[...]
        xs = xs_ref[...]
        x = xs * dt_e
        xb = x.astype(BF16)
        bb, cb = b_ref[...].astype(BF16), c_ref[...].astype(BF16)
        cbm = _dot(cb, bb, NT)
        cbt = _dot(bb, cb, NT)
        st_e = s_e[Q - 1:Q, :]
        dec_out, dec_st, e_t = jnp.exp(s_e), jnp.exp(st_e - s_e), jnp.exp(st_e)
        dyv = dy_ref[...]
        dyb = dyv.astype(BF16)
        sprev = sp_ref[...]
        sb = sprev.astype(BF16)
        ds_in = dst[...]
        dsb = ds_in.astype(BF16)

        cs = _dot(cb, sb)
        dcs = (dyv * dec_out).astype(BF16)
        d_c = _dot(dcs, sb, NT)
        wf = x * dec_st
        d_w = _dot(bb, dsb)
        d_b = _dot(wf.astype(BF16), dsb, NT)
        tw = d_w * wf
        dse_ref[...] = dyv * cs * dec_out - tw
        ds_c = jnp.zeros((Q, 8), F32)
        dcb = jnp.zeros((Q, Q), F32)
        dcbt = jnp.zeros((Q, Q), F32)
        lane8 = lax.broadcasted_iota(jnp.int32, (1, 8), 1)
        for h in range(HPG):
            sl = slice(h * HD, (h + 1) * HD)
            sc_h, sr_h = s_c[:, h:h + 1], s_r[h:h + 1, :]
            lm = jnp.where(lower, jnp.exp(jnp.minimum(sc_h - sr_h, 0.0)), 0.0)
            lmt = jnp.where(upper, jnp.exp(jnp.minimum(sr_h - sc_h, 0.0)), 0.0)
            mt = cbt * lmt
            dm = _dot(dyb[:, sl], xb[:, sl], NT)
            dmt = _dot(xb[:, sl], dyb[:, sl], NT)
            dxbuf[:, sl] = _dot(mt.astype(BF16), dyb[:, sl])
            dml = dm * lm
            dmlt = dmt * lmt
            dcb = dcb + dml
            dcbt = dcbt + dmlt
            dsh = jnp.sum(dml * cbm, axis=1, keepdims=True) - jnp.sum(dmlt * cbt, axis=1, keepdims=True)
            ds_c = ds_c + dsh * (lane8 == h).astype(F32)
        d_c = d_c + _dot(dcb.astype(BF16), bb)
        d_b = d_b + _dot(dcbt.astype(BF16), cb)
        dx = d_w * dec_st + dxbuf[...]
        ts_ref[...] = jnp.sum(tw, axis=0, keepdims=True) + jnp.sum(ds_in * sprev, axis=0, keepdims=True) * e_t
        dsh_ref[...] = ds_c
        dde_ref[...] = dx * xs
        pd_ref[...] += jnp.sum(dyv * xs, axis=0, keepdims=True)
        dxs_ref[...] = dx * dt_e + dyv * dexp_ref[...]
        db_ref[...] = d_b
        dc_ref[...] = d_c
        dst[...] = e_t * ds_in + _dot(cb, dcs, TN)

    wide = pl.BlockSpec((Q, GW), lambda g, c: (rc(c), g))
    state = pl.BlockSpec((Q, NS), lambda g, c: (rc(c), g))
    in_specs = _ssd_specs(nc, True) + [pl.BlockSpec((None, None, NS, GW), lambda g, c: (rc(c), g, 0, 0)), wide]
    return pl.pallas_call(
        body, name="ssd_bwd", grid=(NG, nc), in_specs=in_specs,
        out_specs=[wide, state, state, wide, wide,
                   pl.BlockSpec((None, 1, GW), lambda g, c: (rc(c), 0, g)),
                   pl.BlockSpec((None, Q, 8), lambda g, c: (g, rc(c), 0)),
                   pl.BlockSpec((None, 1, GW), lambda g, c: (g, 0, 0))],
        out_shape=[jax.ShapeDtypeStruct((n, DI), F32), jax.ShapeDtypeStruct((n, NG * NS), F32),
                   jax.ShapeDtypeStruct((n, NG * NS), F32), jax.ShapeDtypeStruct((n, DI), F32),
                   jax.ShapeDtypeStruct((n, DI), F32), jax.ShapeDtypeStruct((nc, 1, DI), F32),
                   jax.ShapeDtypeStruct((NG, n, 8), F32), jax.ShapeDtypeStruct((NG, 1, GW), F32)],
        scratch_shapes=[pltpu.VMEM((NS, GW), F32), pltpu.VMEM((Q, GW), F32)],
        compiler_params=_cp("parallel", "arbitrary"),
    )(xbc, xbc, xbc, s8, s_t, dt_e, s_e, dexp, sprev_all, dy)


GL = 128


def _gnorm_fwd(y, proj, w):
    n = y.shape[0]

    def body(y_ref, z_ref, w_ref, o_ref):
        for g in range(NG):
            sl = slice(g * GW, (g + 1) * GW)
            yz = y_ref[:, sl] * _silu(z_ref[:, sl])
            r = lax.rsqrt(jnp.mean(yz * yz, axis=-1, keepdims=True) + EPS)
            o_ref[:, sl] = (yz * r * w_ref[:, sl]).astype(BF16)

    row = pl.BlockSpec((GL, DI), lambda i: (i, 0))
    return pl.pallas_call(
        body, name="gnorm_fwd", grid=(n // GL,),
        in_specs=[row, row, pl.BlockSpec((1, DI), lambda i: (0, 0))], out_specs=row,
        out_shape=jax.ShapeDtypeStruct((n, DI), BF16), compiler_params=_cp("parallel"),
    )(y, proj, w)


def _gnorm_bwd(dyn, y, proj, w):
    n = y.shape[0]

    def body(d_ref, y_ref, z_ref, w_ref, dy_ref, dz_ref, gw_ref):
        @pl.when(pl.program_id(0) == 0)
        def _():
            gw_ref[...] = jnp.zeros_like(gw_ref)

        for g in range(NG):
            sl = slice(g * GW, (g + 1) * GW)
            yv, zv, dv = y_ref[:, sl], z_ref[:, sl], d_ref[:, sl]
            sz = _silu(zv)
            yz = yv * sz
            r = lax.rsqrt(jnp.mean(yz * yz, axis=-1, keepdims=True) + EPS)
            yh = yz * r
            gg = dv * w_ref[:, sl]
            dyz = r * (gg - yh * jnp.mean(gg * yh, axis=-1, keepdims=True))
            gw_ref[:, sl] += jnp.sum(dv * yh, axis=0, keepdims=True)
            dy_ref[:, sl] = dyz * sz
            dz_ref[:, sl] = (dyz * yv * _dsilu(zv)).astype(BF16)

    row = pl.BlockSpec((GL, DI), lambda i: (i, 0))
    vec = pl.BlockSpec((1, DI), lambda i: (0, 0))
    return pl.pallas_call(
        body, name="gnorm_bwd", grid=(n // GL,),
        in_specs=[row, row, row, vec], out_specs=[row, row, vec],
        out_shape=[jax.ShapeDtypeStruct((n, DI), F32), jax.ShapeDtypeStruct((n, DI), BF16),
                   jax.ShapeDtypeStruct((1, DI), F32)],
        compiler_params=_cp("arbitrary"),
    )(dyn, y, proj, w)


SL = 512
SB = 8
SCB = NCH // SB


def _s5_in(proj, bre, bim):
    n = proj.shape[0]
    uoff = OFF_U // 128

    def body(u_ref, br_ref, bi_ref, or_ref, oi_ref):
        u = u_ref[...].astype(BF16)
        or_ref[...] = _dot(u, br_ref[...])
        oi_ref[...] = _dot(u, bi_ref[...])

    blk = pl.BlockSpec((None, 128, SCB), lambda i, j: (j, 0, 0))
    out = pl.BlockSpec((SL, SCB), lambda i, j: (i, j))
    return pl.pallas_call(
        body, name="s5_in", grid=(n // SL, SB),
        in_specs=[pl.BlockSpec((SL, 128), lambda i, j: (i, uoff + j)), blk, blk], out_specs=[out, out],
        out_shape=[jax.ShapeDtypeStruct((n, NCH), F32)] * 2, compiler_params=_cp("parallel", "parallel"),
    )(proj, bre, bim)


SC = 256


def _s5_scan(vre, vim, tab, reverse, name):
    n = vre.shape[0]
    nl = n // SL
    ng = SL // 8
    ti = (lambda i: nl - 1 - i) if reverse else (lambda i: i)

    def body(re_ref, im_ref, tab_ref, ore_ref, oim_ref, cre, cim):
        @pl.when(pl.program_id(1) == 0)
        def _():
            cre[...] = jnp.zeros_like(cre)
            cim[...] = jnp.zeros_like(cim)

        def step(j, carry):
            cr, ci = carry
            jj = (ng - 1 - j) if reverse else j
            rows = pl.ds(pl.multiple_of(jj * 8, 8), 8)
            vr, vi = re_ref[rows, :], im_ref[rows, :]
            for t, k in enumerate((1, 2, 4)):
                sh = (8 - k) if reverse else k
                rr, ri = pltpu.roll(vr, sh, 0), pltpu.roll(vi, sh, 0)
                pr, pi = tab_ref[2 * t], tab_ref[2 * t + 1]
                vr, vi = vr + pr * rr - pi * ri, vi + pr * ri + pi * rr
            lr, li = tab_ref[6], tab_ref[7]
            vr, vi = vr + lr * cr - li * ci, vi + lr * ci + li * cr
            ore_ref[rows, :] = vr
            oim_ref[rows, :] = vi
            e = 0 if reverse else 7
            return (jnp.broadcast_to(vr[e:e + 1, :], (8, SC)), jnp.broadcast_to(vi[e:e + 1, :], (8, SC)))

        cr, ci = lax.fori_loop(0, ng, step, (cre[...], cim[...]))
        cre[...] = cr
        cim[...] = ci

    blk = pl.BlockSpec((SL, SC), lambda j, i: (ti(i), j))
    return pl.pallas_call(
        body, name=name, grid=(NCH // SC, nl),
        in_specs=[blk, blk, pl.BlockSpec((8, 8, SC), lambda j, i: (0, 0, j))], out_specs=[blk, blk],
        out_shape=[jax.ShapeDtypeStruct((n, NCH), F32)] * 2,
        scratch_shapes=[pltpu.VMEM((8, SC), F32), pltpu.VMEM((8, SC), F32)],
        compiler_params=_cp("parallel", "arbitrary"),
    )(vre, vim, tab)


def _s5_out(xre, xim, cre, cimn, proj, dvec):
    n = xre.shape[0]
    uoff = OFF_U // 128

    def body(xr_ref, xi_ref, cr_ref, ci_ref, u_ref, d_ref, y_ref, g_ref):
        y = (_dot(xr_ref[...].astype(BF16), cr_ref[...]) + _dot(xi_ref[...].astype(BF16), ci_ref[...])
             + d_ref[...] * u_ref[...])
        y_ref[...] = y
        g_ref[...] = _gelu(y).astype(BF16)

    xs = pl.BlockSpec((SL, SCB), lambda i, j: (i, j))
    blk = pl.BlockSpec((None, SCB, 128), lambda i, j: (j, 0, 0))
    out = pl.BlockSpec((SL, 128), lambda i, j: (i, j))
    return pl.pallas_call(
        body, name="s5_out", grid=(n // SL, SB),
        in_specs=[xs, xs, blk, blk, pl.BlockSpec((SL, 128), lambda i, j: (i, uoff + j)),
                  pl.BlockSpec((1, 128), lambda i, j: (0, j))],
        out_specs=[out, out],
        out_shape=[jax.ShapeDtypeStruct((n, DS5), F32), jax.ShapeDtypeStruct((n, DS5), BF16)],
        compiler_params=_cp("parallel", "parallel"),
    )(xre, xim, cre, cimn, proj, dvec)


def _s5_out_bwd(dg, ypre, crt, cimnt, proj, dvec, xre, xim):
    n = dg.shape[0]
    uoff = OFF_U // 128
    nl = n // SL

    def body(dg_ref, y_ref, cr_ref, ci_ref, u_ref, d_ref, xr_ref, xi_ref,
             gr_ref, gi_ref, dus_ref, gcr_ref, gci_ref, gd_ref):
        dy = dg_ref[...] * _dgelu(y_ref[...])
        dyb = dy.astype(BF16)
        gr_ref[...] = _dot(dyb, cr_ref[...])
        gi_ref[...] = _dot(dyb, ci_ref[...])
        dus_ref[...] = dy * d_ref[...]

        @pl.when(pl.program_id(1) == 0)
        def _():
            gcr_ref[...] = jnp.zeros_like(gcr_ref)
            gci_ref[...] = jnp.zeros_like(gci_ref)
            gd_ref[...] = jnp.zeros_like(gd_ref)

        gcr_ref[...] += _dot(xr_ref[...].astype(BF16), dyb, TN)
        gci_ref[...] -= _dot(xi_ref[...].astype(BF16), dyb, TN)
        gd_ref[...] += jnp.sum(dy * u_ref[...], axis=0, keepdims=True)

    u128 = pl.BlockSpec((SL, 128), lambda j, i: (i, j))
    xs = pl.BlockSpec((SL, SCB), lambda j, i: (i, j))
    blk = pl.BlockSpec((None, 128, SCB), lambda j, i: (j, 0, 0))
    gblk = pl.BlockSpec((None, SCB, 128), lambda j, i: (j, 0, 0))
    vec = pl.BlockSpec((1, 128), lambda j, i: (0, j))
    return pl.pallas_call(
        body, name="s5_out_bwd", grid=(SB, nl),
        in_specs=[u128, u128, blk, blk, pl.BlockSpec((SL, 128), lambda j, i: (i, uoff + j)), vec, xs, xs],
        out_specs=[xs, xs, u128, gblk, gblk, vec],
        out_shape=[jax.ShapeDtypeStruct((n, NCH), F32)] * 2 + [jax.ShapeDtypeStruct((n, DS5), F32)]
        + [jax.ShapeDtypeStruct((SB, SCB, 128), F32)] * 2 + [jax.ShapeDtypeStruct((1, DS5), F32)],
        compiler_params=_cp("parallel", "arbitrary"),
    )(dg, ypre, crt, cimnt, proj, dvec, xre, xim)


def _s5_in_bwd(are, aim, brt, bit, proj, dus, xre, xim):
    n = are.shape[0]
    uoff = OFF_U // 128
    per = SL // 8

    def body(ar_ref, ai_ref, br_ref, bi_ref, u_ref, dus_ref, xr_ref, xi_ref, pr_ref, pi_ref,
             du_ref, gbr_ref, gbi_ref, glr_ref, gli_ref):
        i = pl.program_id(1)
        ar, ai = ar_ref[...], ai_ref[...]
        arb, aib = ar.astype(BF16), ai.astype(BF16)
        du_ref[...] = (_dot(arb, br_ref[...]) + _dot(aib, bi_ref[...]) + dus_ref[...]).astype(BF16)

        @pl.when(i == 0)
        def _():
            for r in (gbr_ref, gbi_ref, glr_ref, gli_ref):
                r[...] = jnp.zeros_like(r)

        ub = u_ref[...].astype(BF16)
        gbr_ref[...] += _dot(arb, ub, TN)
        gbi_ref[...] += _dot(aib, ub, TN)
        row0 = lax.broadcasted_iota(jnp.int32, (SL, SCB), 0) == 0
        last_r = jnp.where(i > 0, pr_ref[7:8, :], 0.0)
        last_i = jnp.where(i > 0, pi_ref[7:8, :], 0.0)
        xpr = jnp.where(row0, last_r, pltpu.roll(xr_ref[...], 1, 0))
        xpi = jnp.where(row0, last_i, pltpu.roll(xi_ref[...], 1, 0))
        glr_ref[...] += jnp.sum(ar * xpr + ai * xpi, axis=0, keepdims=True)
        gli_ref[...] += jnp.sum(ai * xpr - ar * xpi, axis=0, keepdims=True)

    xs = pl.BlockSpec((SL, SCB), lambda j, i: (i, j))
    prev = pl.BlockSpec((8, SCB), lambda j, i: (jnp.maximum(i * per - 1, 0), j))
    blk = pl.BlockSpec((None, SCB, 128), lambda j, i: (j, 0, 0))
    u128 = pl.BlockSpec((SL, 128), lambda j, i: (i, j))
    vec = pl.BlockSpec((1, SCB), lambda j, i: (0, j))
    return pl.pallas_call(
        body, name="s5_in_bwd", grid=(SB, n // SL),
        in_specs=[xs, xs, blk, blk, pl.BlockSpec((SL, 128), lambda j, i: (i, uoff + j)), u128, xs, xs, prev, prev],
        out_specs=[u128, blk, blk, vec, vec],
        out_shape=[jax.ShapeDtypeStruct((n, DS5), BF16)] + [jax.ShapeDtypeStruct((SB, SCB, 128), F32)] * 2
        + [jax.ShapeDtypeStruct((1, NCH), F32)] * 2,
        compiler_params=_cp("parallel", "arbitrary"),
    )(are, aim, brt, bit, proj, dus, xre, xim, xre, xim)


MC = 1024


def _merge_specs():
    ga = pl.BlockSpec((TL, MC), lambda i, j: (i, OFF_GA // MC + j))
    gb = pl.BlockSpec((TL, MC), lambda i, j: (i, OFF_GB // MC + j))
    col = pl.BlockSpec((TL, MC), lambda i, j: (i, j))
    gate = pl.BlockSpec((TL, MC), lambda i, j: (i, D // MC + j))
    return ga, gb, col, gate


def _merge_fwd(proj, ya, vg):
    n = ya.shape[0]

    def body(ga_ref, gb_ref, ya_ref, v_ref, g_ref, o_ref):
        yb = v_ref[...] * _sig(g_ref[...])
        o_ref[...] = (_sig(ga_ref[...]) * ya_ref[...] + _sig(gb_ref[...]) * yb).astype(BF16)

    ga, gb, col, gate = _merge_specs()
    return pl.pallas_call(
        body, name="merge_fwd", grid=(n // TL, D // MC), in_specs=[ga, gb, col, col, gate], out_specs=col,
        out_shape=jax.ShapeDtypeStruct((n, D), BF16), compiler_params=_cp("parallel", "parallel"),
    )(proj, proj, ya, vg, vg)


def _merge_bwd(dm, proj, ya, vg):
    n = ya.shape[0]

    def body(dm_ref, ga_ref, gb_ref, ya_ref, v_ref, g_ref, dga_ref, dgb_ref, dya_ref, dv_ref, dg_ref):
        d = dm_ref[...]
        sa, sb, sg = _sig(ga_ref[...]), _sig(gb_ref[...]), _sig(g_ref[...])
        v = v_ref[...]
        yb = v * sg
        dga_ref[...] = (d * ya_ref[...] * sa * (1.0 - sa)).astype(BF16)
        dgb_ref[...] = (d * yb * sb * (1.0 - sb)).astype(BF16)
        dya_ref[...] = (d * sa).astype(BF16)
        dyb = d * sb
        dv_ref[...] = (dyb * sg).astype(BF16)
        dg_ref[...] = (dyb * v * sg * (1.0 - sg)).astype(BF16)

    ga, gb, col, gate = _merge_specs()
    o = jax.ShapeDtypeStruct((n, D), BF16)
    return pl.pallas_call(
        body, name="merge_bwd", grid=(n // TL, D // MC), in_specs=[col, ga, gb, col, col, gate],
        out_specs=[col] * 5, out_shape=[o] * 5, compiler_params=_cp("parallel", "parallel"),
    )(dm, proj, proj, ya, vg, vg)


def _adamw_update(wv, gv, mv, vv):
    nm = B1 * mv + (1.0 - B1) * gv
    nv = B2 * vv + (1.0 - B2) * (gv * gv)
    m_hat = nm / (1.0 - B1 ** STEP)
    v_hat = nv / (1.0 - B2 ** STEP)
    return -LR * (m_hat / (jnp.sqrt(v_hat) + AEPS) + WD * wv), nm, nv


def _adamw(w, g, m, v, name):
    r, c = w.shape
    tr = _pick(r, 128)

    def body(w_ref, g_ref, m_ref, v_ref, d_ref, nm_ref, nv_ref):
        d_ref[...], nm_ref[...], nv_ref[...] = _adamw_update(w_ref[...], g_ref[...], m_ref[...], v_ref[...])

    blk = pl.BlockSpec((tr, c), lambda i: (i, 0))
    o = jax.ShapeDtypeStruct((r, c), F32)
    return pl.pallas_call(
        body, name=name, grid=(r // tr,), in_specs=[blk] * 4, out_specs=[blk] * 3, out_shape=[o] * 3,
        compiler_params=_cp("parallel"),
    )(w, g, m, v)


def _adamw_halves(w, g_mine, g_other, m, v, cidx, name):
    _, r, c = w.shape
    hr, gc = g_mine.shape
    tr = _pick(hr, 128)
    nbh = hr // tr
    assert gc == c and 2 * hr - tr < r <= 2 * hr

    def body(cs, w_ref, gm_ref, go_ref, m_ref, v_ref, g_ref, d_ref, nm_ref, nv_ref):
        mine = pl.program_id(0) // nbh == cs[0]
        gv = jnp.where(mine, gm_ref[...], go_ref[...])
        g_ref[...] = gv
        d_ref[...], nm_ref[...], nv_ref[...] = _adamw_update(w_ref[...], gv, m_ref[...], v_ref[...])

    blk = pl.BlockSpec((None, tr, c), lambda i, cs: (0, i, 0))
    gblk = pl.BlockSpec((tr, gc), lambda i, cs: (i % nbh, 0))
    o = jax.ShapeDtypeStruct((1, r, c), F32)
    return pl.pallas_call(
        body, name=name,
        grid_spec=pltpu.PrefetchScalarGridSpec(num_scalar_prefetch=1, grid=(2 * nbh,),
                                               in_specs=[blk, gblk, gblk, blk, blk], out_specs=[blk] * 4),
        out_shape=[o] * 4, compiler_params=_cp("parallel"),
    )(cidx, w, g_mine, g_other, m, v)


def _chip_sum(part, sib, cidx, name):
    _, r, cc = part.shape
    hr = r // 2
    tr = _pick(hr, 256)

    def body(cs, p_ref, s_ref, o_ref):
        o_ref[...] = (p_ref[...].astype(F32) + s_ref[...].astype(F32)).astype(BF16)

    blk = pl.BlockSpec((None, tr, cc), lambda k, i, cs: (k, i, 0))
    return pl.pallas_call(
        body, name=name,
        grid_spec=pltpu.PrefetchScalarGridSpec(
            num_scalar_prefetch=1, grid=(4, hr // tr),
            in_specs=[pl.BlockSpec((None, None, tr, cc), lambda k, i, cs: (k, cs[0], i, 0)), blk], out_specs=blk),
        out_shape=jax.ShapeDtypeStruct((4, hr, cc), BF16), compiler_params=_cp("parallel", "parallel"),
    )(cidx, part.reshape(4, 2, hr, cc), sib)


def _shard_sum(own, got, sidx, name):
    _, hr, cc = own.shape
    tr = _pick(hr, 256)

    def body(cs, own_ref, g0, g1, g2, g3, o_ref):
        acc = None
        for k, g_ref in enumerate((g0, g1, g2, g3)):
            term = jnp.where(cs[0] == k, own_ref[...], g_ref[...]).astype(F32)
            acc = term if acc is None else acc + term
        o_ref[...] = acc

    def got_spec(k):
        return pl.BlockSpec((None, tr, cc), lambda i, cs: (jnp.where(cs[0] == k, (k + 1) % 4, k), i, 0))

    return pl.pallas_call(
        body, name=name,
        grid_spec=pltpu.PrefetchScalarGridSpec(
            num_scalar_prefetch=1, grid=(hr // tr,),
            in_specs=[pl.BlockSpec((None, tr, cc), lambda i, cs: (cs[0], i, 0))] + [got_spec(k) for k in range(4)],
            out_specs=pl.BlockSpec((tr, cc), lambda i, cs: (i, 0))),
        out_shape=jax.ShapeDtypeStruct((hr, cc), F32), compiler_params=_cp("parallel"),
    )(sidx, own, got, got, got, got)


def _sum_slabs(xs, name, out_dtype=F32):
    r, c = xs[0].shape
    tr = _pick(r, 256)

    def body(*refs):
        acc = refs[0][...].astype(F32)
        for ref in refs[1:-1]:
            acc = acc + ref[...].astype(F32)
        refs[-1][...] = acc.astype(out_dtype)

    blk = pl.BlockSpec((tr, c), lambda i: (i, 0))
    return pl.pallas_call(
        body, name=name, grid=(r // tr,), in_specs=[blk] * len(xs), out_specs=blk,
        out_shape=jax.ShapeDtypeStruct((r, c), out_dtype), compiler_params=_cp("parallel"),
    )(*xs)


def _place():
    return lax.axis_index("x"), lax.axis_index("y"), lax.axis_index("c")


def _gather_small(v):
    m_per, n = v.shape

    def body(x_ref, out_ref, send_sems, recv_sems, local_sem):
        x, y, c = _place()
        me, sibling = (x, y, c), (x, y, 1 - c)
        chips = [(1 - x, y), (x, 1 - y), (1 - x, 1 - y)]

        def rows(px, py, pc):
            return out_ref.at[pl.ds((4 * px + 2 * py + pc) * m_per, m_per), :]

        def copy(k, block, to, src=None):
            return pltpu.make_async_remote_copy(
                src_ref=rows(*block) if src is None else src, dst_ref=rows(*block),
                send_sem=send_sems.at[k], recv_sem=recv_sems.at[k], device_id=to, device_id_type=MESH)

        mine = pltpu.make_async_copy(x_ref, rows(*me), local_sem)
        mine.start()
        first = [copy(0, me, sibling, src=x_ref)]
        first += [copy(1 + j, me, (*chip, c), src=x_ref) for j, chip in enumerate(chips)]
        for cp in first:
            cp.start()
        passed = [copy(4 + j, (*chip, c), sibling) for j, chip in enumerate(chips)]
        for j, chip in enumerate(chips):
            copy(1 + j, (*chip, c), me).wait_recv()
            passed[j].start()
        copy(0, sibling, me).wait_recv()
        for j, chip in enumerate(chips):
            copy(4 + j, (*chip, 1 - c), me).wait_recv()
        for cp in first + passed:
            cp.wait_send()
        mine.wait()

    return pl.pallas_call(
        body, name="gather_small_%d" % m_per,
        out_shape=jax.ShapeDtypeStruct((8 * m_per, n), v.dtype),
        in_specs=[pl.BlockSpec(memory_space=pltpu.VMEM)], out_specs=pl.BlockSpec(memory_space=pltpu.VMEM),
        scratch_shapes=[pltpu.SemaphoreType.DMA((7,)), pltpu.SemaphoreType.DMA((7,)), pltpu.SemaphoreType.DMA],
        compiler_params=pltpu.CompilerParams(vmem_limit_bytes=VMEM_LIMIT),
    )(v)


def _allsum_small(v, name):
    r = v.shape[0]
    g = _gather_small(v)
    return _sum_slabs([g[k * r:(k + 1) * r] for k in range(8)], name)


def _gather_big(shards):
    nt = len(shards)

    def body(*refs):
        ins, outs = refs[:nt], refs[nt:2 * nt]
        send_sems, recv_sems = refs[2 * nt:]
        x, y, c = _place()
        s = 2 * x + y
        sibling = (x, y, 1 - c)
        chips = [(1 - x, y), (x, 1 - y), (1 - x, 1 - y)]

        def half(t, slot, h):
            hr = ins[t].shape[0] // 2
            return outs[t].at[slot, pl.ds(h * hr, hr), :]

        def ici(t, j, src_slot, to):
            hr = ins[t].shape[0] // 2
            return pltpu.make_async_remote_copy(
                src_ref=ins[t].at[pl.ds(c * hr, hr), :], dst_ref=half(t, src_slot, c),
                send_sem=send_sems.at[7 * t + j], recv_sem=recv_sems.at[7 * t + j], device_id=to, device_id_type=MESH)

        def d2d(t, j, slot, h):
            return pltpu.make_async_remote_copy(
                src_ref=half(t, slot, h), dst_ref=half(t, slot, h),
                send_sem=send_sems.at[7 * t + 3 + j], recv_sem=recv_sems.at[7 * t + 3 + j],
                device_id=sibling, device_id_type=MESH)

        def whole(t):
            return pltpu.make_async_remote_copy(
                src_ref=ins[t], dst_ref=outs[t].at[s], send_sem=send_sems.at[7 * t + 6],
                recv_sem=recv_sems.at[7 * t + 6], device_id=sibling, device_id_type=MESH)

        sends = [ici(t, j, s, (*chip, c)) for t in range(nt) for j, chip in enumerate(chips)]
        sends += [whole(t) for t in range(nt)]
        for cp in sends:
            cp.start()
        passed = []
        for t in range(nt):
            for j, (px, py) in enumerate(chips):
                ici(t, j, 2 * px + py, (x, y, c)).wait_recv()
                cp = d2d(t, j, 2 * px + py, c)
                cp.start()
                passed.append(cp)
        for t in range(nt):
            for j, (px, py) in enumerate(chips):
                d2d(t, j, 2 * px + py, 1 - c).wait_recv()
            whole(t).wait_recv()
        for cp in sends + passed:
            cp.wait_send()

    return pl.pallas_call(
        body, name="gather_big",
        out_shape=[jax.ShapeDtypeStruct((4,) + a.shape, a.dtype) for a in shards],
        in_specs=[ANY] * nt, out_specs=[ANY] * nt,
        scratch_shapes=[pltpu.SemaphoreType.DMA((7 * nt,)), pltpu.SemaphoreType.DMA((7 * nt,))],
    )(*shards)


def _swap_halves(parts, name):
    nt = len(parts)

    def body(*refs):
        ins, outs = refs[:nt], refs[nt:2 * nt]
        send_sems, recv_sems = refs[2 * nt:]
        x, y, c = _place()
        cps = []
        for t in range(nt):
            hr = ins[t].shape[1] // 2
            cps.append(pltpu.make_async_remote_copy(
                src_ref=ins[t].at[:, pl.ds((1 - c) * hr, hr), :], dst_ref=outs[t],
                send_sem=send_sems.at[t], recv_sem=recv_sems.at[t], device_id=(x, y, 1 - c), device_id_type=MESH))
        for cp in cps:
            cp.start()
        for cp in cps:
            cp.wait()

    return pl.pallas_call(
        body, name=name,
        out_shape=[jax.ShapeDtypeStruct((4, a.shape[1] // 2, a.shape[2]), a.dtype) for a in parts],
        in_specs=[ANY] * nt, out_specs=[ANY] * nt,
        scratch_shapes=[pltpu.SemaphoreType.DMA((nt,)), pltpu.SemaphoreType.DMA((nt,))],
    )(*parts)


def _scatter_chips(parts):
    nt = len(parts)

    def body(*refs):
        ins, outs = refs[:nt], refs[nt:2 * nt]
        send_sems, recv_sems = refs[2 * nt:]
        x, y, c = _place()
        s = 2 * x + y
        chips = [(1 - x, y), (x, 1 - y), (1 - x, 1 - y)]
        cps = []
        for t in range(nt):
            for j, (px, py) in enumerate(chips):
                cps.append(pltpu.make_async_remote_copy(
                    src_ref=ins[t].at[2 * px + py], dst_ref=outs[t].at[s],
                    send_sem=send_sems.at[3 * t + j], recv_sem=recv_sems.at[3 * t + j],
                    device_id=(px, py, c), device_id_type=MESH))
        for cp in cps:
            cp.start()
        for t in range(nt):
            for j, (px, py) in enumerate(chips):
                pltpu.make_async_remote_copy(
                    src_ref=ins[t].at[s], dst_ref=outs[t].at[2 * px + py],
                    send_sem=send_sems.at[3 * t + j], recv_sem=recv_sems.at[3 * t + j],
                    device_id=(px, py, c), device_id_type=MESH).wait_recv()
        for cp in cps:
            cp.wait_send()

    return pl.pallas_call(
        body, name="scatter_chips",
        out_shape=[jax.ShapeDtypeStruct(a.shape, a.dtype) for a in parts],
        in_specs=[ANY] * nt, out_specs=[ANY] * nt,
        scratch_shapes=[pltpu.SemaphoreType.DMA((3 * nt,)), pltpu.SemaphoreType.DMA((3 * nt,))],
    )(*parts)


def _swap_whole(halves):
    nt = len(halves)

    def body(*refs):
        ins, outs = refs[:nt], refs[nt:2 * nt]
        send_sems, recv_sems = refs[2 * nt:]
        x, y, c = _place()
        cps = [pltpu.make_async_remote_copy(
            src_ref=ins[t], dst_ref=outs[t], send_sem=send_sems.at[t], recv_sem=recv_sems.at[t],
            device_id=(x, y, 1 - c), device_id_type=MESH) for t in range(nt)]
        for cp in cps:
            cp.start()
        for cp in cps:
            cp.wait()

    return pl.pallas_call(
        body, name="swap_whole",
        out_shape=[jax.ShapeDtypeStruct(a.shape, a.dtype) for a in halves],
        in_specs=[ANY] * nt, out_specs=[ANY] * nt,
        scratch_shapes=[pltpu.SemaphoreType.DMA((nt,)), pltpu.SemaphoreType.DMA((nt,))],
    )(*halves)


def _pass_halves(got, shards):
    nt = len(got)

    def body(*refs):
        ins, own, outs = refs[:nt], refs[nt:2 * nt], refs[2 * nt:3 * nt]
        send_sems, recv_sems = refs[3 * nt:]
        x, y, c = _place()
        s = 2 * x + y
        chips = [(1 - x, y), (x, 1 - y), (1 - x, 1 - y)]

        def half(ref, t, slot, h):
            hr = ins[t].shape[1] // 2
            return ref.at[slot, pl.ds(h * hr, hr), :]

        def copy(t, j, h):
            px, py = chips[j]
            return pltpu.make_async_remote_copy(
                src_ref=half(ins[t], t, 2 * px + py, h), dst_ref=half(outs[t], t, 2 * px + py, h),
                send_sem=send_sems.at[4 * t + j], recv_sem=recv_sems.at[4 * t + j],
                device_id=(x, y, 1 - c), device_id_type=MESH)

        def whole(t):
            return pltpu.make_async_remote_copy(
                src_ref=own[t], dst_ref=outs[t].at[s], send_sem=send_sems.at[4 * t + 3],
                recv_sem=recv_sems.at[4 * t + 3], device_id=(x, y, 1 - c), device_id_type=MESH)

        sends = [copy(t, j, c) for t in range(nt) for j in range(3)] + [whole(t) for t in range(nt)]
        for cp in sends:
            cp.start()
        for t in range(nt):
            for j in range(3):
                copy(t, j, 1 - c).wait_recv()
            whole(t).wait_recv()
        for cp in sends:
            cp.wait_send()

    return pl.pallas_call(
        body, name="pass_halves",
        out_shape=[jax.ShapeDtypeStruct(a.shape, a.dtype) for a in got],
        in_specs=[ANY] * (2 * nt), out_specs=[ANY] * nt, input_output_aliases={t: t for t in range(nt)},
        scratch_shapes=[pltpu.SemaphoreType.DMA((4 * nt,)), pltpu.SemaphoreType.DMA((4 * nt,))],
    )(*got, *shards)


HBM = pl.BlockSpec(memory_space=pltpu.HBM)
SEM = pl.BlockSpec(memory_space=pltpu.SEMAPHORE)
EFFECT = pltpu.SideEffectType.DATAFLOW_SIDE_EFFECTING


def _ici_copies(kind, srcs, lands, send_sems, recv_sems):
    x, y, c = _place()
    s = 2 * x + y
    cps = []
    for t in range(len(srcs)):
        for j, (px, py) in enumerate([(1 - x, y), (x, 1 - y), (1 - x, 1 - y)]):
            if kind == "gather":
                hr = srcs[t].shape[0] // 2
                src, dst = srcs[t].at[pl.ds(c * hr, hr), :], lands[t].at[s, pl.ds(c * hr, hr), :]
            else:
                src, dst = srcs[t].at[2 * px + py], lands[t].at[s]
            cps.append(pltpu.make_async_remote_copy(
                src_ref=src, dst_ref=dst, send_sem=send_sems[3 * t + j], recv_sem=recv_sems[3 * t + j],
                device_id=(px, py, c), device_id_type=MESH))
    return cps


def _ici_start(kind, srcs, after, name):
    nt = len(srcs)
    nc = 3 * nt
    lands = [(4,) + a.shape if kind == "gather" else a.shape for a in srcs]

    def body(*refs):
        src, land = refs[:nt], refs[nt:2 * nt]
        outs = refs[2 * nt + 1:]
        for cp in _ici_copies(kind, src, land, outs[:nc], outs[nc:2 * nc]):
            cp.start()
        outs[-1][...] = jnp.zeros_like(outs[-1])

    hbm = lambda a: pltpu.with_memory_space_constraint(a, pltpu.HBM)
    outs = pl.pallas_call(
        body, name=name,
        out_shape=tuple([pltpu.SemaphoreType.DMA(())] * (2 * nc) + [pltpu.HBM(a.shape, a.dtype) for a in srcs]
                        + [pltpu.HBM(sh, a.dtype) for sh, a in zip(lands, srcs)] + [jax.ShapeDtypeStruct((8, 128), F32)]),
        in_specs=[HBM] * (2 * nt) + [ANY],
        out_specs=tuple([SEM] * (2 * nc) + [HBM] * (2 * nt) + [pl.BlockSpec(memory_space=pltpu.VMEM)]),
        input_output_aliases={i: 2 * nc + i for i in range(2 * nt)},
        compiler_params=pltpu.CompilerParams(has_side_effects=EFFECT),
    )(*[hbm(a) for a in srcs], *[hbm(lax.empty(sh, a.dtype)) for sh, a in zip(lands, srcs)], after)
    return outs[:2 * nc], outs[2 * nc:2 * nc + nt], outs[2 * nc + nt:2 * nc + 2 * nt], outs[-1]


def _ici_wait(kind, sems, srcs, lands, after, name):
    nt = len(srcs)
    nc = 3 * nt

    def body(*refs):
        src, land = refs[:nt], refs[nt:2 * nt]
        sem = refs[2 * nt:2 * nt + 2 * nc]
        for cp in _ici_copies(kind, src, land, sem[:nc], sem[nc:]):
            cp.wait_send()
            cp.wait_recv()

    outs = pl.pallas_call(
        body, name=name,
        out_shape=tuple(pltpu.HBM(a.shape, a.dtype) for a in list(srcs) + list(lands)),
        in_specs=[HBM] * (2 * nt) + [SEM] * (2 * nc) + [ANY], out_specs=tuple([HBM] * (2 * nt)),
        input_output_aliases={i: i for i in range(2 * nt)},
        compiler_params=pltpu.CompilerParams(has_side_effects=EFFECT),
    )(*srcs, *lands, *sems, after)
    return outs[:nt], outs[nt:]


def _s5_params(lam_re, lam_im, log_dt, b_re, b_im):
    lr = jnp.minimum(lam_re, EIG_MAX)
    dt = jnp.exp(log_dt)[:, None]
    mag = jnp.exp(lr * dt)
    lbr, lbi = mag * jnp.cos(lam_im * dt), mag * jnp.sin(lam_im * dt)
    den = lr * lr + lam_im * lam_im
    qr = ((lbr - 1.0) * lr + lbi * lam_im) / den
    qi = (lbi * lr - (lbr - 1.0) * lam_im) / den
    bbr = qr[..., None] * b_re - qi[..., None] * b_im
    bbi = qr[..., None] * b_im + qi[..., None] * b_re
    return lbr, lbi, bbr, bbi


def _cmul(a, b):
    return a[0] * b[0] - a[1] * b[1], a[0] * b[1] + a[1] * b[0]


def _scan_table(lr, li, reverse):
    l1 = (lr.reshape(1, NCH), li.reshape(1, NCH))
    pows = [l1]
    for _ in range(7):
        pows.append(_cmul(pows[-1], l1))
    r = jnp.arange(8)[:, None]
    tabs = []
    for k in (1, 2, 4):
        keep = (r < 8 - k) if reverse else (r >= k)
        tabs += [jnp.where(keep, pows[k - 1][0], 0.0), jnp.where(keep, pows[k - 1][1], 0.0)]
    order = range(7, -1, -1) if reverse else range(8)
    tabs += [jnp.concatenate([pows[e][0] for e in order], axis=0), jnp.concatenate([pows[e][1] for e in order], axis=0)]
    return jnp.stack(tabs).astype(F32)


_EYE8 = lambda: jnp.eye(8, dtype=F32)


def _to_in_blocks(b):
    return jnp.einsum("jgpc,gh->jgchp", b.reshape(8, 8, 64, 16), _EYE8()).reshape(8, 128, 512)


def _to_out_blocks(cm):
    return jnp.einsum("jgcp,gh->jgphc", cm.reshape(8, 8, 16, 64), _EYE8()).reshape(8, 512, 128)


def _from_out_blocks(g):
    return jnp.einsum("jgphc,gh->jgpc", g.reshape(8, 8, 64, 8, 16), _EYE8()).reshape(64, 64, 16)


def _local_step(x, target, p, token, late_weights, send_grads):
    n = x.shape[0]
    g = {}
    hn1 = _rms_fwd(x, p["norm_mix_w"], "rms_mix", after=token)
    proj = _matmul(hn1, p["w_full"], "nt", "mm_in", tn=896)
    dtraw = proj
    xbc = _conv_a_fwd(proj, p["conv_a_w"], p["conv_a_b"])
    to_lanes = lambda v: jnp.pad(jnp.pad(v.reshape(NG, HPG), ((0, 0), (0, 8 - HPG))).reshape(1, 8 * NG),
                                 ((0, 0), (0, HL - 8 * NG)))
    from_lanes = lambda v: v[:, :8 * NG].reshape(-1, NG, 8)[:, :, :HPG].reshape(-1, NG * HPG)
    hp = jnp.concatenate([to_lanes(p["dt_bias"]), to_lanes(p["a_log"]), jnp.zeros((6, HL), F32)], axis=0)
    lane = jnp.arange(HL)[:, None]
    emat = ((lane < 8 * NG) & (lane % 8 < HPG)
            & (jnp.arange(DI)[None, :] // HD == HPG * (lane // 8) + lane % 8)).astype(BF16)
    dexp = jnp.repeat(p["d_a"].reshape(1, NG * HPG), HD, axis=1)
    dt, s_cum, s_t, dt_e, s_e = _ssd_prep(dtraw, hp, emat)
    s8 = s_cum[:, :8 * NG].reshape(n, NG, 8).transpose(1, 0, 2)
    yssd, sprev = _ssd_fwd(xbc, s8, s_t, dt_e, s_e, dexp)
    yn = _gnorm_fwd(yssd, proj, p["norm_a_w"])
    p = {**p, **late_weights(yn)}
    ya = _matmul(yn, p["w_proj_a"], "nn", "mm_proj")

    (lbr, lbi, bbr, bbi), s5_vjp = jax.vjp(_s5_params, p["s5_lam_re"], p["s5_lam_im"], p["s5_log_dt"],
                                           p["s5_b_re"], p["s5_b_im"])
    bin_r, bin_i = _to_in_blocks(bbr), _to_in_blocks(bbi)
    cout_r, cout_in = _to_out_blocks(p["s5_c_re"]), _to_out_blocks(-p["s5_c_im"])
    bur, bui = _s5_in(proj, bin_r.astype(BF16), bin_i.astype(BF16))
    xre, xim = _s5_scan(bur, bui, _scan_table(lbr, lbi, False), False, "s5_scan_fwd")
    ypre, g5 = _s5_out(xre, xim, cout_r.astype(BF16), cout_in.astype(BF16), proj, p["s5_d"])
    vg = _matmul(g5, p["w_s5_glu"], "nn", "mm_glu", b_stacked=True)
    merged = _merge_fwd(proj, ya, vg)
    h1 = _matmul(merged, p["w_out"], "nn", "mm_out", residual=x)
    hn2 = _rms_fwd(h1, p["norm_ffn_w"], "rms_ffn")
    up = _matmul(hn2, p["w_up"], "nn", "mm_up", tn=1408, b_stacked=True)
    act = _conv_ffn_fwd(up, p["conv_ffn_w"], p["conv_ffn_b"])
    h2 = _matmul(act, p["w_down"], "nn", "mm_down", tk=DFF // 2, residual=h1)
    dh2, dh2b, g["norm_final_w"], loss_blk = _final(h2, p["norm_final_w"], target)
    g["w_down"] = _matmul(act, dh2b, "tn", "mm_gw_down", out_dtype=BF16).reshape(4, DFF // 4, D)
    dact = _matmul(dh2b, p["w_down"], "nt", "mm_dact", out_dtype=BF16)
    dup, g["conv_ffn_w"], g["conv_ffn_b"] = _conv_ffn_bwd(up, dact, p["conv_ffn_w"], p["conv_ffn_b"])
    g["w_up"] = _matmul(hn2, dup, "tn", "mm_gw_up", out_dtype=BF16, tn=1408, out_stacked=True)
    tok = send_grads(["w_up", "w_down"], g, "s1")
    dhn2 = _matmul(dup, p["w_up"], "nt", "mm_dhn2", tk=2816, b_stacked=True, after=tok)
    dh1, dh1b, g["norm_ffn_w"] = _rms_bwd(dhn2, h1, p["norm_ffn_w"], dh2, "rms_ffn_bwd")
    g["w_out"] = _matmul(merged, dh1b, "tn", "mm_gw_out", out_dtype=BF16).reshape(4, D // 4, D)
    dmerged = _matmul(dh1b, p["w_out"], "nt", "mm_dmerged")
    dga, dgb, dya, dval, dgate = _merge_bwd(dmerged, proj, ya, vg)
    dvg = jnp.concatenate([dval, dgate], axis=1)
    g["w_s5_glu"] = _matmul(g5, dvg, "tn", "mm_gw_glu", out_dtype=BF16, out_stacked=True)
    dg5 = _matmul(dvg, p["w_s5_glu"], "nt", "mm_dg5", b_stacked=True)
    tr = lambda b: b.transpose(0, 2, 1)
    gxr, gxi, dus, gcr, gci, g["s5_d"] = _s5_out_bwd(dg5, ypre, tr(cout_r).astype(BF16), tr(cout_in).astype(BF16),
                                                     proj, p["s5_d"], xre, xim)
    are, aim = _s5_scan(gxr, gxi, _scan_table(lbr, -lbi, True), True, "s5_scan_bwd")
    du, gbr, gbi, glr, gli = _s5_in_bwd(are, aim, tr(bin_r).astype(BF16), tr(bin_i).astype(BF16), proj, dus, xre, xim)
    g["s5_c_re"] = _from_out_blocks(gcr).transpose(0, 2, 1)
    g["s5_c_im"] = _from_out_blocks(gci).transpose(0, 2, 1)
    (g["s5_lam_re"], g["s5_lam_im"], g["s5_log_dt"], g["s5_b_re"], g["s5_b_im"]) = s5_vjp(
        (glr.reshape(64, 64), gli.reshape(64, 64), _from_out_blocks(gbr), _from_out_blocks(gbi)))
    g["w_proj_a"] = _matmul(yn, dya, "tn", "mm_gw_proj", out_dtype=BF16).reshape(4, DI // 4, D)
    tok = send_grads(["w_proj_a", "w_s5_glu", "w_out"], g, "s2")
    dyn = _matmul(dya, p["w_proj_a"], "nt", "mm_dyn", after=tok)
    dyssd, dz, g["norm_a_w"] = _gnorm_bwd(dyn, yssd, proj, p["norm_a_w"])
    dxs, dbm, dcm, ds_e, ddt_e, tsum, dsh8, pd = _ssd_bwd(xbc, s8, s_t, dt_e, s_e, dexp, sprev, dyssd)
    dsh = jnp.pad(dsh8.transpose(1, 0, 2).reshape(n, 8 * NG), ((0, 0), (0, HL - 8 * NG)))
    draw, ps = _ssd_post(ds_e, ddt_e, tsum, dsh, dtraw, dt, hp, emat.T)
    g["dt_bias"] = from_lanes(ps[0:1])
    g["a_log"] = from_lanes(ps[1:2])
    g["d_a"] = pd.reshape(NG * HPG, HD).sum(axis=1).reshape(1, NG * HPG)
    ddt = draw.astype(BF16)
    dxbc_parts, gcw, gcb = [], [], []
    for arr, col0, nm in ((dxs, 0, "conv_a_bwd_x"), (dbm, DI, "conv_a_bwd_b"), (dcm, DI + NG * NS, "conv_a_bwd_c")):
        dpart, gw_, gb_ = _conv_a_bwd(proj, arr, p["conv_a_w"], p["conv_a_b"], col0, nm)
        dxbc_parts.append(dpart)
        gcw.append(gw_)
        gcb.append(gb_)
    g["conv_a_w"] = jnp.concatenate(gcw, axis=1)
    g["conv_a_b"] = jnp.concatenate(gcb, axis=1)
    dproj = jnp.concatenate([dz] + dxbc_parts + [du, dga, dgb, ddt], axis=1)
    g_main = _matmul(dproj, hn1, "tn", "mm_gw_in", out_dtype=BF16, tm=896, tn=2048)
    g_dt = g_main[NMAIN:NMAIN + 8 * NG].reshape(NG, 8, D)[:, :HPG].reshape(NG * HPG, D)
    g_sh = _move_rows(g_main, RUNS_TO_SHARDS, 4 * WPAD, 240, MT, "rows_to_shards")
    g["w_in"] = lax.dynamic_update_slice(g_sh, g_dt, (DT_SHARD_ROW, 0)).reshape(4, WPAD, D)
    tok = send_grads(["w_in"], g, "s3")
    dhn1 = _matmul(dproj, p["w_full"], "nn", "mm_dhn1", tk=2688, after=tok)
    gx, _, g["norm_mix_w"] = _rms_bwd(dhn1, x, p["norm_mix_w"], dh1, "rms_mix_bwd")
    return loss_blk, gx, g


BIG = ["w_in", "w_proj_a", "w_s5_glu", "w_out", "w_up", "w_down"]
SMALL = ["norm_mix_w", "conv_a_w", "conv_a_b", "dt_bias", "a_log", "d_a", "norm_a_w", "s5_lam_re", "s5_lam_im",
         "s5_log_dt", "s5_b_re", "s5_b_im", "s5_c_re", "s5_c_im", "s5_d", "norm_ffn_w", "conv_ffn_w", "conv_ffn_b",
         "norm_final_w"]
ORDER = ["norm_mix_w", "w_in", "conv_a_w", "conv_a_b", "dt_bias", "a_log", "d_a", "norm_a_w", "w_proj_a", "s5_lam_re",
         "s5_lam_im", "s5_log_dt", "s5_b_re", "s5_b_im", "s5_c_re", "s5_c_im", "s5_d", "w_s5_glu", "w_out",
         "norm_ffn_w", "w_up", "conv_ffn_w", "conv_ffn_b", "w_down", "norm_final_w"]
CONV_FULL = {"conv_a_w": (KA, CONVD), "conv_ffn_w": (KF, 2 * DFF)}


def _pack(arrs):
    flat = jnp.concatenate([a.reshape(-1).astype(F32) for a in arrs])
    total = flat.shape[0]
    padded = -(-total // 1024) * 1024
    return jnp.pad(flat, (0, padded - total)).reshape(padded // 128, 128)


def _unpack(block, shapes):
    flat = block.reshape(-1)
    out, at = [], 0
    for sh in shapes:
        size = math.prod(sh)
        out.append(flat[at:at + size].reshape(sh))
        at += size
    return out


def _stack_cols(a):
    return a.transpose(1, 0, 2).reshape(a.shape[1], 4 * a.shape[2])


def _unstack_cols(a):
    return a.reshape(a.shape[0], 4, a.shape[1] // 4).transpose(1, 0, 2)


def kernel(x, norm_mix_w, w_in, conv_a_w, conv_a_b, dt_bias, a_log, d_a, norm_a_w, w_proj_a, s5_lam_re, s5_lam_im, s5_log_dt, s5_b_re, s5_b_im, s5_c_re, s5_c_im, s5_d, w_s5_glu, w_out, norm_ffn_w, w_up, conv_ffn_w, conv_ffn_b, w_down, norm_final_w, loss_target, m_norm_mix_w, m_w_in, m_conv_a_w, m_conv_a_b, m_dt_bias, m_a_log, m_d_a, m_norm_a_w, m_w_proj_a, m_s5_lam_re, m_s5_lam_im, m_s5_log_dt, m_s5_b_re, m_s5_b_im, m_s5_c_re, m_s5_c_im, m_s5_d, m_w_s5_glu, m_w_out, m_norm_ffn_w, m_w_up, m_conv_ffn_w, m_conv_ffn_b, m_w_down, m_norm_final_w, v_norm_mix_w, v_w_in, v_conv_a_w, v_conv_a_b, v_dt_bias, v_a_log, v_d_a, v_norm_a_w, v_w_proj_a, v_s5_lam_re, v_s5_lam_im, v_s5_log_dt, v_s5_b_re, v_s5_b_im, v_s5_c_re, v_s5_c_im, v_s5_d, v_w_s5_glu, v_w_out, v_norm_ffn_w, v_w_up, v_conv_ffn_w, v_conv_ffn_b, v_w_down, v_norm_final_w):
    w = dict(norm_mix_w=norm_mix_w, w_in=w_in, conv_a_w=conv_a_w, conv_a_b=conv_a_b, dt_bias=dt_bias, a_log=a_log, d_a=d_a, norm_a_w=norm_a_w, w_proj_a=w_proj_a, s5_lam_re=s5_lam_re, s5_lam_im=s5_lam_im, s5_log_dt=s5_log_dt, s5_b_re=s5_b_re, s5_b_im=s5_b_im, s5_c_re=s5_c_re, s5_c_im=s5_c_im, s5_d=s5_d, w_s5_glu=w_s5_glu, w_out=w_out, norm_ffn_w=norm_ffn_w, w_up=w_up, conv_ffn_w=conv_ffn_w, conv_ffn_b=conv_ffn_b, w_down=w_down, norm_final_w=norm_final_w)
    m = dict(norm_mix_w=m_norm_mix_w, w_in=m_w_in, conv_a_w=m_conv_a_w, conv_a_b=m_conv_a_b, dt_bias=m_dt_bias, a_log=m_a_log, d_a=m_d_a, norm_a_w=m_norm_a_w, w_proj_a=m_w_proj_a, s5_lam_re=m_s5_lam_re, s5_lam_im=m_s5_lam_im, s5_log_dt=m_s5_log_dt, s5_b_re=m_s5_b_re, s5_b_im=m_s5_b_im, s5_c_re=m_s5_c_re, s5_c_im=m_s5_c_im, s5_d=m_s5_d, w_s5_glu=m_w_s5_glu, w_out=m_w_out, norm_ffn_w=m_norm_ffn_w, w_up=m_w_up, conv_ffn_w=m_conv_ffn_w, conv_ffn_b=m_conv_ffn_b, w_down=m_w_down, norm_final_w=m_norm_final_w)
    v = dict(norm_mix_w=v_norm_mix_w, w_in=v_w_in, conv_a_w=v_conv_a_w, conv_a_b=v_conv_a_b, dt_bias=v_dt_bias, a_log=v_a_log, d_a=v_d_a, norm_a_w=v_norm_a_w, w_proj_a=v_w_proj_a, s5_lam_re=v_s5_lam_re, s5_lam_im=v_s5_lam_im, s5_log_dt=v_s5_log_dt, s5_b_re=v_s5_b_re, s5_b_im=v_s5_b_im, s5_c_re=v_s5_c_re, s5_c_im=v_s5_c_im, s5_d=v_s5_d, w_s5_glu=v_w_s5_glu, w_out=v_w_out, norm_ffn_w=v_norm_ffn_w, w_up=v_w_up, conv_ffn_w=v_conv_ffn_w, conv_ffn_b=v_conv_ffn_b, w_down=v_w_down, norm_final_w=v_norm_final_w)
    xi, yi, ci = _place()
    chip = 2 * xi + yi

    cidx = jnp.reshape(ci, (1,)).astype(jnp.int32)
    sidx = jnp.reshape(chip, (1,)).astype(jnp.int32)

    tw = lambda a: jnp.transpose(a[0])[None]
    w["w_in"], m["w_in"], v["w_in"] = tw(w_in), tw(m_w_in), tw(v_w_in)
    shards = [w[k][0].astype(BF16) for k in BIG]
    shards[0] = jnp.pad(shards[0], ((0, WPAD - WSH), (0, 0)))
    w_in_full = _gather_big(shards[:1])[0]
    g_sems, g_srcs, g_lands, token = _ici_start("gather", shards[1:], w_in_full, "gather_rest_start")

    def late_weights(after):
        srcs, got = _ici_wait("gather", g_sems, g_srcs, g_lands, after, "gather_rest_wait")
        full = _pass_halves(list(got), list(srcs))
        return {"w_proj_a": full[0].reshape(DI, D), "w_s5_glu": full[1], "w_out": full[2].reshape(D, D),
                "w_up": full[3], "w_down": full[4].reshape(DFF, D)}

    pending = []

    def send_grads(names, g, tag):
        parts = [g[k] for k in names]
        sib = _swap_halves(parts, "swap_halves_" + tag)
        sums = [_chip_sum(parts[t], sib[t], cidx, "chip_sum_" + k) for t, k in enumerate(names)]
        sems, srcs, lands, tok = _ici_start("scatter", sums, cidx, "scatter_start_" + tag)
        pending.append((names, tag, sems, srcs, lands))
        return tok
    conv_blocks = []
    for k, (taps, cols) in CONV_FULL.items():
        shard = jnp.where(ci == 0, w[k][0], 0.0)
        conv_blocks.append(lax.dynamic_update_slice_in_dim(jnp.zeros((taps, cols), F32), shard, chip * (cols // 4), 1))
    conv_full = _unpack(_allsum_small(_pack(conv_blocks), "sum_conv_w"), [CONV_FULL[k] for k in CONV_FULL])

    w_sh = w_in_full.reshape(4 * WPAD, D)
    w_dt = jnp.pad(w_sh[DT_SHARD_ROW:DT_SHARD_ROW + NG * HPG].reshape(NG, HPG, D), ((0, 0), (0, 8 - HPG), (0, 0)))
    w_full = lax.dynamic_update_slice(_move_rows(w_sh, RUNS_TO_MAIN, NFULL, 240, MT, "rows_to_main"),
                                      w_dt.reshape(8 * NG, D), (NMAIN, 0))
    p = {
        "w_full": w_full,
        "conv_a_w": conv_full[0], "conv_ffn_w": conv_full[1],
        "conv_a_b": conv_a_b, "conv_ffn_b": conv_ffn_b,
        "norm_mix_w": norm_mix_w, "norm_a_w": norm_a_w, "norm_ffn_w": norm_ffn_w,
        "norm_final_w": norm_final_w.reshape(1, D),
        "dt_bias": dt_bias, "a_log": a_log, "d_a": d_a, "s5_d": s5_d,
        "s5_lam_re": s5_lam_re[0], "s5_lam_im": s5_lam_im[0], "s5_log_dt": s5_log_dt[0],
        "s5_b_re": s5_b_re[0], "s5_b_im": s5_b_im[0], "s5_c_re": s5_c_re[0], "s5_c_im": s5_c_im[0],
    }
    loss_blk, gx, g = _local_step(x[0], loss_target[0], p, token, late_weights, send_grads)

    after, halves = gx, {}
    for names, tag, sems, srcs, lands in pending:
        srcs, got = _ici_wait("scatter", sems, srcs, lands, after, "scatter_wait_" + tag)
        for t, k in enumerate(names):
            halves[k] = _shard_sum(srcs[t], got[t], sidx, "shard_sum_" + k)
        after = halves[names[0]]
    g_mine = [halves[k] for k in BIG]
    g_other = _swap_whole(g_mine)

    small_shapes = [CONV_FULL.get(k, w[k].shape[1:] if k != "norm_final_w" else w[k].shape) for k in SMALL]
    small = _allsum_small(_pack([g[k] for k in SMALL] + [loss_blk[0:1, 0:1]]), "sum_small_grads")
    small_grads = dict(zip(SMALL + ["loss"], _unpack(small, small_shapes + [(1,)])))
    for k, (taps, cols) in CONV_FULL.items():
        small_grads[k] = lax.dynamic_slice_in_dim(small_grads[k], chip * (cols // 4), cols // 4, axis=1)
    loss = small_grads.pop("loss").reshape(())

    grads, delta, new_m, new_v = {}, {}, {}, {}
    for t, k in enumerate(BIG):
        outs = _adamw_halves(w[k], g_mine[t], g_other[t], m[k], v[k], cidx, "adamw_" + k)
        grads[k], delta[k], new_m[k], new_v[k] = [tw(o) for o in outs] if k == "w_in" else outs
    for k in SMALL:
        grads[k] = small_grads[k].reshape(w[k].shape)
    pk = lambda t: _pack([t[k] for k in SMALL])
    d_, m_, v_ = _adamw(pk(w), pk(grads), pk(m), pk(v), "adamw_small")
    shapes = [w[k].shape for k in SMALL]
    for k, dd, mm, vv in zip(SMALL, _unpack(d_, shapes), _unpack(m_, shapes), _unpack(v_, shapes)):
        delta[k], new_m[k], new_v[k] = dd, mm, vv
    return (loss, gx[None], *[grads[k] for k in ORDER], *[delta[k] for k in ORDER],
            *[new_m[k] for k in ORDER], *[new_v[k] for k in ORDER])
```

```python
import functools
import math

import jax
import jax.numpy as jnp
from jax import lax
from jax.experimental import pallas as pl
from jax.experimental.pallas import tpu as pltpu

F32 = jnp.float32
BF16 = jnp.bfloat16
HI = lax.Precision.HIGHEST
MESH = pl.DeviceIdType.MESH
ANY = pl.BlockSpec(memory_space=pl.ANY)

D = 2048
DI = 3072
HD = 64
NG = 8
HPG = 6
GW = HPG * HD
NS = 128
KA = 4
Q = 256
CONVD = DI + 2 * NG * NS
DS5 = 1024
NCH = 4096
DFF = 5632
KF = 3
EPS = 1e-6
EIG_MAX = -1e-4
NMAIN = 13312
OFF_XBC, OFF_U, OFF_GA, OFF_GB = 3072, 8192, 9216, 11264
WSH = 3340
WPAD = 3360
IN_SPLIT = [DI, DI + CONVD, DI + CONVD + NG * HPG]
NFULL = NMAIN + 128
MT = 336


def _w_in_runs():
    runs = []
    for k in range(4):
        for o_lo, o_hi, m_lo in ((0, IN_SPLIT[1], 0), (IN_SPLIT[2], 4 * WSH, IN_SPLIT[1])):
            lo, hi = max(o_lo, WSH * k), min(o_hi, WSH * (k + 1))
            if lo < hi:
                runs.append((m_lo + lo - o_lo, m_lo + hi - o_lo, WPAD * k + lo - WSH * k))
    return runs


RUNS_TO_MAIN = _w_in_runs()
RUNS_TO_SHARDS = [(s_lo, s_lo + m_hi - m_lo, m_lo) for m_lo, m_hi, s_lo in RUNS_TO_MAIN]
DT_SHARD_ROW = WPAD * (IN_SPLIT[1] // WSH) + IN_SPLIT[1] % WSH
assert IN_SPLIT[1] // WSH == (IN_SPLIT[2] - 1) // WSH
VMEM_LIMIT = 56 * 1024 * 1024

LR, B1, B2, AEPS, WD, STEP = 0.001, 0.9, 0.999, 1e-08, 0.01, 10


def _cp(*sem):
    return pltpu.CompilerParams(dimension_semantics=sem, vmem_limit_bytes=VMEM_LIMIT)


def _sig(x):
    return jax.nn.sigmoid(x)


def _silu(x):
    return x * _sig(x)


def _dsilu(x):
    s = _sig(x)
    return s * (1.0 + x * (1.0 - s))


def _softplus(x):
    return jnp.maximum(x, 0.0) + jnp.log(1.0 + jnp.exp(-jnp.abs(x)))


_GC = math.sqrt(2.0 / math.pi)


def _gelu(x):
    return 0.5 * x * (1.0 + jnp.tanh(_GC * (x + 0.044715 * x * x * x)))


def _dgelu(x):
    t = jnp.tanh(_GC * (x + 0.044715 * x * x * x))
    return 0.5 * (1.0 + t) + 0.5 * x * (1.0 - t * t) * _GC * (1.0 + 3.0 * 0.044715 * x * x)


def _dot(a, b, dims=((1,), (0,)), prec=None):
    return lax.dot_general(a, b, (dims, ((), ())), precision=prec, preferred_element_type=F32)


NT = ((1,), (1,))
TN = ((0,), (0,))


def _pick(n, t):
    for unit in (128, 8):
        for cand in range(min(n, t) // unit * unit, 0, -unit):
            if n % cand == 0:
                return cand
    return n


def _matmul(a, b, mode, name, out_dtype=F32, tm=1024, tn=1024, tk=2048, residual=None, b_stacked=False,
            out_stacked=False, after=None):
    if b_stacked:
        _, brows, bn = b.shape
        bshape = (brows, 4 * bn)
    else:
        bshape = b.shape
    if mode == "nn":
        (m, k), (k2, n) = a.shape, bshape
    elif mode == "nt":
        (m, k), (n, k2) = a.shape, bshape
    else:
        (k, m), (k2, n) = a.shape, bshape
    assert k == k2
    tm = _pick(m, tm)
    tn = _pick(n // 4 if (out_stacked or (b_stacked and mode != "nt")) else n, tn)
    tk = _pick(k // 4 if (b_stacked and mode == "nt") else k, tk)
    nk = k // tk
    dims = {"nn": ((1,), (0,)), "nt": NT, "tn": TN}[mode]
    has_res = residual is not None
    n_in = 2 + has_res + (after is not None)

    def body(*refs):
        a_ref, b_ref = refs[0], refs[1]
        r_ref = refs[2] if has_res else None
        o_ref = refs[n_in]
        p = _dot(a_ref[...], b_ref[...], dims)

        def finish(r):
            if has_res:
                r = r + r_ref[...]
            o_ref[...] = r.astype(out_dtype)

        if nk == 1:
            finish(p)
        else:
            acc = refs[-1]
            kk = pl.program_id(2)

            @pl.when(kk == 0)
            def _():
                acc[...] = p

            @pl.when(kk > 0)
            def _():
                acc[...] += p

            @pl.when(kk == nk - 1)
            def _():
                finish(acc[...])

    if mode == "tn":
        a_spec = pl.BlockSpec((tk, tm), lambda i, j, kk: (kk, i))
    else:
        a_spec = pl.BlockSpec((tm, tk), lambda i, j, kk: (i, kk))
    if mode == "nt":
        if b_stacked:
            per = bn // tk
            b_spec = pl.BlockSpec((None, tn, tk), lambda i, j, kk: (kk // per, j, kk % per))
        else:
            b_spec = pl.BlockSpec((tn, tk), lambda i, j, kk: (j, kk))
    elif b_stacked:
        per = bn // tn
        b_spec = pl.BlockSpec((None, tk, tn), lambda i, j, kk: (j // per, kk, j % per))
    else:
        b_spec = pl.BlockSpec((tk, tn), lambda i, j, kk: (kk, j))
    o_spec = pl.BlockSpec((tm, tn), lambda i, j, kk: (i, j))
    if out_stacked:
        per_o = n // 4 // tn
        out_spec = pl.BlockSpec((None, tm, tn), lambda i, j, kk: (j // per_o, i, j % per_o))
        out_shape = jax.ShapeDtypeStruct((4, m, n // 4), out_dtype)
    else:
        out_spec, out_shape = o_spec, jax.ShapeDtypeStruct((m, n), out_dtype)
    in_specs, args = [a_spec, b_spec], [a, b]
    if has_res:
        in_specs.append(o_spec)
        args.append(residual)
    if after is not None:
        in_specs.append(ANY)
        args.append(after)
    return pl.pallas_call(
        body, name=name, grid=(m // tm, n // tn, nk),
        in_specs=in_specs, out_specs=out_spec, out_shape=out_shape,
        scratch_shapes=[pltpu.VMEM((tm, tn), F32)] if nk > 1 else [],
        compiler_params=_cp("parallel", "parallel", "arbitrary"),
    )(*args)


def _move_rows(src, runs, rows_out, t_out, t_in, name):
    rows_in, cols = src.shape
    nb_out, nb_in = rows_out // t_out, rows_in // t_in
    assert rows_out % t_out == 0 and rows_in % t_in == 0 and t_in >= t_out
    blk, off, lo, hi = ([[0] * nb_out for _ in range(2)] for _ in range(4))
    for i in range(nb_out):
        hits = [r for r in runs if r[0] < (i + 1) * t_out and r[1] > i * t_out]
        assert len(hits) <= 2
        for s, (o_lo, o_hi, s_lo) in enumerate(hits):
            lo[s][i] = max(o_lo, i * t_out) - i * t_out
            hi[s][i] = min(o_hi, (i + 1) * t_out) - i * t_out
            first = i * t_out + lo[s][i] - o_lo + s_lo
            blk[s][i] = min(first // t_in, nb_in - 1)
            off[s][i] = first - lo[s][i] - blk[s][i] * t_in
    table = jnp.asarray([blk[0], off[0], lo[0], hi[0], blk[1], off[1], lo[1], hi[1]], jnp.int32)

    def body(tab, a0, a1, b0, b1, o_ref):
        i = pl.program_id(0)
        o_ref[...] = jnp.zeros_like(o_ref)
        r = lax.broadcasted_iota(jnp.int32, (t_out, t_in), 0)
        k = lax.broadcasted_iota(jnp.int32, (t_out, t_in), 1)
        for s, (first, second) in enumerate(((a0, a1), (b0, b1))):
            off_s, lo_s, hi_s = tab[4 * s + 1, i], tab[4 * s + 2, i], tab[4 * s + 3, i]
            live = (r >= lo_s) & (r < hi_s)

            @pl.when(hi_s > lo_s)
            def _():
                sel = (live & (k == r + off_s)).astype(BF16)
                o_ref[...] += _dot(sel, first[...]).astype(o_ref.dtype)

            @pl.when((hi_s > lo_s) & (off_s + hi_s > t_in))
            def _():
                sel = (live & (k == r + off_s - t_in)).astype(BF16)
                o_ref[...] += _dot(sel, second[...]).astype(o_ref.dtype)

    def in_spec(s, nxt):
        return pl.BlockSpec((t_in, cols), lambda i, tab: (jnp.minimum(tab[4 * s, i] + nxt, nb_in - 1), 0))

    return pl.pallas_call(
        body, name=name,
        grid_spec=pltpu.PrefetchScalarGridSpec(
            num_scalar_prefetch=1, grid=(nb_out,),
            in_specs=[in_spec(0, 0), in_spec(0, 1), in_spec(1, 0), in_spec(1, 1)],
            out_specs=pl.BlockSpec((t_out, cols), lambda i, tab: (i, 0))),
        out_shape=jax.ShapeDtypeStruct((rows_out, cols), src.dtype), compiler_params=_cp("parallel"),
    )(table, src, src, src, src)


TL = 256


def _rms_fwd(x, w, name, after=None):
    n, d = x.shape

    def body(x_ref, w_ref, *rest):
        xv = x_ref[...]
        r = lax.rsqrt(jnp.mean(xv * xv, axis=-1, keepdims=True) + EPS)
        rest[-1][...] = (xv * r * w_ref[...]).astype(BF16)

    extra = [] if after is None else [after]
    return pl.pallas_call(
        body, name=name, grid=(n // TL,),
        in_specs=[pl.BlockSpec((TL, d), lambda i: (i, 0)), pl.BlockSpec((1, d), lambda i: (0, 0))] + [ANY] * len(extra),
        out_specs=pl.BlockSpec((TL, d), lambda i: (i, 0)),
        out_shape=jax.ShapeDtypeStruct((n, d), BF16), compiler_params=_cp("parallel"),
    )(x, w, *extra)


def _rms_bwd(dhn, x, w, dres, name):
    n, d = x.shape

    def body(g_ref, x_ref, w_ref, r_ref, dx_ref, dxb_ref, gw_ref):
        xv = x_ref[...]
        r = lax.rsqrt(jnp.mean(xv * xv, axis=-1, keepdims=True) + EPS)
        xh = xv * r
        gv = g_ref[...]
        g = gv * w_ref[...]
        dx = r_ref[...] + r * (g - xh * jnp.mean(g * xh, axis=-1, keepdims=True))
        dx_ref[...] = dx
        dxb_ref[...] = dx.astype(BF16)

        @pl.when(pl.program_id(0) == 0)
        def _():
            gw_ref[...] = jnp.zeros_like(gw_ref)

        gw_ref[...] += jnp.sum(gv * xh, axis=0, keepdims=True)

    row = pl.BlockSpec((TL, d), lambda i: (i, 0))
    vec = pl.BlockSpec((1, d), lambda i: (0, 0))
    return pl.pallas_call(
        body, name=name, grid=(n // TL,),
        in_specs=[row, row, vec, row], out_specs=[row, row, vec],
        out_shape=[jax.ShapeDtypeStruct((n, d), F32), jax.ShapeDtypeStruct((n, d), BF16),
                   jax.ShapeDtypeStruct((1, d), F32)],
        compiler_params=_cp("arbitrary"),
    )(dhn, x, w, dres)


def _final(h2, w, target):
    n, d = h2.shape

    def body(x_ref, w_ref, t_ref, dx_ref, dxb_ref, gw_ref, loss_ref):
        xv = x_ref[...]
        r = lax.rsqrt(jnp.mean(xv * xv, axis=-1, keepdims=True) + EPS)
        xh = xv * r
        diff = xh * w_ref[...] - t_ref[...]
        gv = diff * (1.0 / d)
        g = gv * w_ref[...]
        dx = r * (g - xh * jnp.mean(g * xh, axis=-1, keepdims=True))
        dx_ref[...] = dx
        dxb_ref[...] = dx.astype(BF16)

        @pl.when(pl.program_id(0) == 0)
        def _():
            gw_ref[...] = jnp.zeros_like(gw_ref)
            loss_ref[...] = jnp.zeros_like(loss_ref)

        gw_ref[...] += jnp.sum(gv * xh, axis=0, keepdims=True)
        part = 0.5 * jnp.sum(jnp.mean(diff * diff, axis=-1, keepdims=True), axis=0, keepdims=True)
        loss_ref[...] += jnp.broadcast_to(part, loss_ref.shape)

    row = pl.BlockSpec((TL, d), lambda i: (i, 0))
    vec = pl.BlockSpec((1, d), lambda i: (0, 0))
    return pl.pallas_call(
        body, name="final_loss", grid=(n // TL,),
        in_specs=[row, vec, row], out_specs=[row, row, vec, pl.BlockSpec((8, 128), lambda i: (0, 0))],
        out_shape=[jax.ShapeDtypeStruct((n, d), F32), jax.ShapeDtypeStruct((n, d), BF16),
                   jax.ShapeDtypeStruct((1, d), F32), jax.ShapeDtypeStruct((8, 128), F32)],
        compiler_params=_cp("arbitrary"),
    )(h2, w, target)


CT = 512
CL = 512


def _lagged(xf, taps, rows):
    return [xf[8:8 + rows]] + [pltpu.roll(xf, s, 0)[8:8 + rows] for s in range(1, taps)]


def _shift_up(x, u, n):
    if u == 0:
        return x[0:n]
    return pltpu.roll(x, x.shape[0] - u, 0)[0:n]


def _conv_pre(lagged, w_ref, b_ref, taps):
    pre = b_ref[...]
    for k in range(taps):
        pre = pre + w_ref[k:k + 1, :] * lagged[taps - 1 - k]
    return pre


def _conv_back(e, w_ref, taps):
    dx = w_ref[taps - 1:taps, :] * e[0:CL]
    for k in range(taps - 1):
        dx = dx + w_ref[k:k + 1, :] * _shift_up(e, taps - 1 - k, CL)
    return dx


def _halo_specs(n, col_of):
    per = CL // 8
    cur = pl.BlockSpec((CL, CT), lambda j, i, *_: (i, col_of(j)))
    prev = pl.BlockSpec((8, CT), lambda j, i, *_: (jnp.maximum(i * per - 1, 0), col_of(j)))
    nxt = pl.BlockSpec((8, CT), lambda j, i, *_: (jnp.minimum((i + 1) * per, n // 8 - 1), col_of(j)))
    return prev, cur, nxt


def _conv_a_fwd(proj, w, b):
    n = proj.shape[0]
    off = OFF_XBC // CT

    def body(p_ref, x_ref, w_ref, b_ref, o_ref):
        p8 = jnp.where(pl.program_id(1) > 0, p_ref[...], 0.0)
        xf = jnp.concatenate([p8, x_ref[...]], axis=0)
        o_ref[...] = _silu(_conv_pre(_lagged(xf, KA, CL), w_ref, b_ref, KA))

    prev, cur, _ = _halo_specs(n, lambda j: j + off)
    return pl.pallas_call(
        body, name="conv_a_fwd", grid=(CONVD // CT, n // CL),
        in_specs=[prev, cur, pl.BlockSpec((KA, CT), lambda j, i: (0, j)), pl.BlockSpec((1, CT), lambda j, i: (0, j))],
        out_specs=pl.BlockSpec((CL, CT), lambda j, i: (i, j)),
        out_shape=jax.ShapeDtypeStruct((n, CONVD), F32), compiler_params=_cp("parallel", "parallel"),
    )(proj, proj, w, b)


def _conv_a_bwd(proj, dout, w, b, col0, name):
    n, width = dout.shape
    off = (OFF_XBC + col0) // CT
    woff = col0 // CT
    nl = n // CL

    def body(p_ref, x_ref, n_ref, d_ref, dn_ref, w_ref, b_ref, dx_ref, dw_ref, db_ref):
        i = pl.program_id(1)
        xf = jnp.concatenate([jnp.where(i > 0, p_ref[...], 0.0), x_ref[...], n_ref[...]], axis=0)
        lag = _lagged(xf, KA, CL + 8)
        de = jnp.concatenate([d_ref[...], jnp.where(i < nl - 1, dn_ref[...], 0.0)], axis=0)
        se = de * _dsilu(_conv_pre(lag, w_ref, b_ref, KA))
        dx_ref[...] = _conv_back(se, w_ref, KA).astype(BF16)

        @pl.when(i == 0)
        def _():
            dw_ref[...] = jnp.zeros_like(dw_ref)
            db_ref[...] = jnp.zeros_like(db_ref)

        sc = se[0:CL]
        for k in range(KA):
            dw_ref[k:k + 1, :] += jnp.sum(sc * lag[KA - 1 - k][0:CL], axis=0, keepdims=True)
        db_ref[...] += jnp.sum(sc, axis=0, keepdims=True)

    prev, cur, nxt = _halo_specs(n, lambda j: j + off)
    _, dcur, dnxt = _halo_specs(n, lambda j: j)
    wspec = pl.BlockSpec((KA, CT), lambda j, i: (0, j + woff))
    bspec = pl.BlockSpec((1, CT), lambda j, i: (0, j + woff))
    return pl.pallas_call(
        body, name=name, grid=(width // CT, nl),
        in_specs=[prev, cur, nxt, dcur, dnxt, wspec, bspec],
        out_specs=[pl.BlockSpec((CL, CT), lambda j, i: (i, j)), pl.BlockSpec((KA, CT), lambda j, i: (0, j)),
                   pl.BlockSpec((1, CT), lambda j, i: (0, j))],
        out_shape=[jax.ShapeDtypeStruct((n, width), BF16), jax.ShapeDtypeStruct((KA, width), F32),
                   jax.ShapeDtypeStruct((1, width), F32)],
        compiler_params=_cp("parallel", "arbitrary"),
    )(proj, proj, proj, dout, dout, w, b)


def _conv_ffn_fwd(up, w, b):
    n = up.shape[0]
    nb = DFF // CT

    def body(pg_ref, g_ref, pv_ref, v_ref, wg_ref, bg_ref, wv_ref, bv_ref, o_ref):
        inner = pl.program_id(1) > 0
        gf = jnp.concatenate([jnp.where(inner, pg_ref[...], 0.0), g_ref[...]], axis=0)
        vf = jnp.concatenate([jnp.where(inner, pv_ref[...], 0.0), v_ref[...]], axis=0)
        gc = _conv_pre(_lagged(gf, KF, CL), wg_ref, bg_ref, KF)
        vc = _conv_pre(_lagged(vf, KF, CL), wv_ref, bv_ref, KF)
        o_ref[...] = (_silu(gc) * vc).astype(BF16)

    gp, gcur, _ = _halo_specs(n, lambda j: j)
    vp, vcur, _ = _halo_specs(n, lambda j: j + nb)
    return pl.pallas_call(
        body, name="conv_ffn_fwd", grid=(nb, n // CL),
        in_specs=[gp, gcur, vp, vcur,
                  pl.BlockSpec((KF, CT), lambda j, i: (0, j)), pl.BlockSpec((1, CT), lambda j, i: (0, j)),
                  pl.BlockSpec((KF, CT), lambda j, i: (0, j + nb)), pl.BlockSpec((1, CT), lambda j, i: (0, j + nb))],
        out_specs=pl.BlockSpec((CL, CT), lambda j, i: (i, j)),
        out_shape=jax.ShapeDtypeStruct((n, DFF), BF16), compiler_params=_cp("parallel", "parallel"),
    )(up, up, up, up, w, b, w, b)


def _conv_ffn_bwd(up, dact, w, b):
    n = up.shape[0]
    nb = DFF // CT
    nl = n // CL

    def body(pg_ref, g_ref, ng_ref, pv_ref, v_ref, nv_ref, d_ref, dn_ref, wg_ref, bg_ref, wv_ref, bv_ref,
             dxg_ref, dxv_ref, dwg_ref, dwv_ref, dbg_ref, dbv_ref):
        i = pl.program_id(1)
        gf = jnp.concatenate([jnp.where(i > 0, pg_ref[...], 0.0), g_ref[...], ng_ref[...]], axis=0)
        vf = jnp.concatenate([jnp.where(i > 0, pv_ref[...], 0.0), v_ref[...], nv_ref[...]], axis=0)
        glag, vlag = _lagged(gf, KF, CL + 8), _lagged(vf, KF, CL + 8)
        de = jnp.concatenate([d_ref[...], jnp.where(i < nl - 1, dn_ref[...], 0.0)], axis=0).astype(F32)
        gc = _conv_pre(glag, wg_ref, bg_ref, KF)
        vc = _conv_pre(vlag, wv_ref, bv_ref, KF)
        sg = _sig(gc)
        dgc = de * vc * (sg * (1.0 + gc * (1.0 - sg)))
        dvc = de * (gc * sg)
        dxg_ref[...] = _conv_back(dgc, wg_ref, KF).astype(BF16)
        dxv_ref[...] = _conv_back(dvc, wv_ref, KF).astype(BF16)

        @pl.when(i == 0)
        def _():
            for r in (dwg_ref, dwv_ref, dbg_ref, dbv_ref):
                r[...] = jnp.zeros_like(r)

        for e, lag, dw_ref, db_ref in ((dgc, glag, dwg_ref, dbg_ref), (dvc, vlag, dwv_ref, dbv_ref)):
            ec = e[0:CL]
            for k in range(KF):
                dw_ref[k:k + 1, :] += jnp.sum(ec * lag[KF - 1 - k][0:CL], axis=0, keepdims=True)
            db_ref[...] += jnp.sum(ec, axis=0, keepdims=True)

    gp, gcur, gnx = _halo_specs(n, lambda j: j)
    vp, vcur, vnx = _halo_specs(n, lambda j: j + nb)
    wcol = lambda o: (pl.BlockSpec((KF, CT), lambda j, i: (0, j + o)), pl.BlockSpec((1, CT), lambda j, i: (0, j + o)))
    wg, bg = wcol(0)
    wv, bv = wcol(nb)
    dxs = pl.BlockSpec((CL, CT), lambda j, i: (i, j))
    outs = pl.pallas_call(
        body, name="conv_ffn_bwd", grid=(nb, nl),
        in_specs=[gp, gcur, gnx, vp, vcur, vnx, gcur, gnx, wg, bg, wv, bv],
        out_specs=[dxs, dxs, wg, wg, bg, bg],
        out_shape=[jax.ShapeDtypeStruct((n, DFF), BF16)] * 2 + [jax.ShapeDtypeStruct((KF, DFF), F32)] * 2
        + [jax.ShapeDtypeStruct((1, DFF), F32)] * 2,
        compiler_params=_cp("parallel", "arbitrary"),
    )(up, up, up, up, up, up, dact, dact, w, b, w, b)
    return [jnp.concatenate(outs[k:k + 2], axis=1) for k in (0, 2, 4)]


HL = 128


def _split3(x):
    hi = x.astype(BF16)
    r1 = x - hi.astype(F32)
    mid = r1.astype(BF16)
    return hi, mid, (r1 - mid.astype(F32)).astype(BF16)


def _dot3(x, m):
    hi, mid, lo = _split3(x)
    return _dot(hi, m) + _dot(mid, m) + _dot(lo, m)


def _tri():
    row = lax.broadcasted_iota(jnp.int32, (Q, Q), 0)
    col = lax.broadcasted_iota(jnp.int32, (Q, Q), 1)
    return row >= col, row <= col


def _ssd_prep(dtraw, hp, emat):
    n = dtraw.shape[0]

    def body(d_ref, hp_ref, e_ref, dt_ref, s_ref, st_ref, dte_ref, se_ref):
        lower, upper = _tri()
        dt = _softplus(d_ref[...] + hp_ref[0:1, :])
        da = dt * (-jnp.exp(hp_ref[1:2, :]))
        s = _dot(lower.astype(F32), da, prec=HI)
        dt_ref[...] = dt
        s_ref[...] = s
        st_ref[...] = _dot(da, upper.astype(F32), TN, prec=HI)
        e = e_ref[...]
        dte_ref[...] = _dot3(dt, e)
        se_ref[...] = _dot3(s, e)

    row = pl.BlockSpec((Q, HL), lambda c: (c, 0))
    wide = pl.BlockSpec((Q, DI), lambda c: (c, 0))
    return pl.pallas_call(
        body, name="ssd_prep", grid=(n // Q,),
        in_specs=[pl.BlockSpec((Q, HL), lambda c: (c, NMAIN // HL)), pl.BlockSpec((8, HL), lambda c: (0, 0)),
                  pl.BlockSpec((HL, DI), lambda c: (0, 0))],
        out_specs=[row, row, pl.BlockSpec((HL, Q), lambda c: (0, c)), wide, wide],
        out_shape=[jax.ShapeDtypeStruct((n, HL), F32)] * 2 + [jax.ShapeDtypeStruct((HL, n), F32)]
        + [jax.ShapeDtypeStruct((n, DI), F32)] * 2,
        compiler_params=_cp("parallel"),
    )(dtraw, hp, emat)


def _ssd_post(ds_e, ddt_e, tsum, dsh, dtraw, dt, hp, emat_t):
    n = dtraw.shape[0]

    def body(dse_ref, dde_ref, ts_ref, dsh_ref, d_ref, dt_ref, hp_ref, et_ref, draw_ref, ps_ref):
        _, upper = _tri()
        et = et_ref[...]
        a = -jnp.exp(hp_ref[1:2, :])
        rows = lax.broadcasted_iota(jnp.int32, (Q, HL), 0)
        ds_t = _dot3(jnp.broadcast_to(ts_ref[...], (8, DI)), et)[0:1, :]
        ds = _dot3(dse_ref[...], et) + dsh_ref[...] + jnp.where(rows == Q - 1, ds_t, 0.0)
        d_a = _dot(upper.astype(F32), ds, prec=HI)
        draw = (_dot3(dde_ref[...], et) + d_a * a) * _sig(d_ref[...] + hp_ref[0:1, :])
        draw_ref[...] = draw

        @pl.when(pl.program_id(0) == 0)
        def _():
            ps_ref[...] = jnp.zeros_like(ps_ref)

        ps_ref[0:1, :] += jnp.sum(draw, axis=0, keepdims=True)
        ps_ref[1:2, :] += jnp.sum(d_a * dt_ref[...], axis=0, keepdims=True) * a

    row = pl.BlockSpec((Q, HL), lambda c: (c, 0))
    wide = pl.BlockSpec((Q, DI), lambda c: (c, 0))
    small = pl.BlockSpec((8, HL), lambda c: (0, 0))
    return pl.pallas_call(
        body, name="ssd_post", grid=(n // Q,),
        in_specs=[wide, wide, pl.BlockSpec((None, 1, DI), lambda c: (c, 0, 0)), row,
                  pl.BlockSpec((Q, HL), lambda c: (c, NMAIN // HL)), row, small,
                  pl.BlockSpec((DI, HL), lambda c: (0, 0))],
        out_specs=[row, small],
        out_shape=[jax.ShapeDtypeStruct((n, HL), F32), jax.ShapeDtypeStruct((8, HL), F32)],
        compiler_params=_cp("arbitrary"),
    )(ds_e, ddt_e, tsum, dsh, dtraw, dt, hp, emat_t)


def _ssd_specs(nc, rev):
    cc = (lambda c: nc - 1 - c) if rev else (lambda c: c)
    return [
        pl.BlockSpec((Q, GW), lambda g, c: (cc(c), g)),
        pl.BlockSpec((Q, NS), lambda g, c: (cc(c), DI // NS + g)),
        pl.BlockSpec((Q, NS), lambda g, c: (cc(c), (DI + NG * NS) // NS + g)),
        pl.BlockSpec((None, Q, 8), lambda g, c: (g, cc(c), 0)),
        pl.BlockSpec((8, Q), lambda g, c: (g, cc(c))),
        pl.BlockSpec((Q, GW), lambda g, c: (cc(c), g)),
        pl.BlockSpec((Q, GW), lambda g, c: (cc(c), g)),
        pl.BlockSpec((1, GW), lambda g, c: (0, g)),
    ]


def _ssd_fwd(xbc, s8, s_t, dt_e, s_e, dexp):
    n = xbc.shape[0]
    nc = n // Q

    def body(xs_ref, b_ref, c_ref, sc_ref, sr_ref, dte_ref, se_ref, dexp_ref, y_ref, sp_ref, st):
        @pl.when(pl.program_id(1) == 0)
        def _():
            st[...] = jnp.zeros_like(st)

        lower, _ = _tri()
        s_c, s_r, dt_e, s_e = sc_ref[...], sr_ref[...], dte_ref[...], se_ref[...]
        xs = xs_ref[...]
        x = xs * dt_e
        xb = x.astype(BF16)
        bb, cb = b_ref[...].astype(BF16), c_ref[...].astype(BF16)
        cbm = _dot(cb, bb, NT)
        st_e = s_e[Q - 1:Q, :]
        sprev = st[...]
        sp_ref[...] = sprev
        yoff = _dot(cb, sprev.astype(BF16)) * jnp.exp(s_e) + dexp_ref[...] * xs
        for h in range(HPG):
            sl = slice(h * HD, (h + 1) * HD)
            lm = jnp.where(lower, jnp.exp(jnp.minimum(s_c[:, h:h + 1] - s_r[h:h + 1, :], 0.0)), 0.0)
            y_ref[:, sl] = _dot((cbm * lm).astype(BF16), xb[:, sl]) + yoff[:, sl]
        w = (x * jnp.exp(st_e - s_e)).astype(BF16)
        st[...] = jnp.exp(st_e) * sprev + _dot(bb, w, TN)

    return pl.pallas_call(
        body, name="ssd_fwd", grid=(NG, nc), in_specs=_ssd_specs(nc, False),
        out_specs=[pl.BlockSpec((Q, GW), lambda g, c: (c, g)),
                   pl.BlockSpec((None, None, NS, GW), lambda g, c: (c, g, 0, 0))],
        out_shape=[jax.ShapeDtypeStruct((n, DI), F32), jax.ShapeDtypeStruct((nc, NG, NS, GW), F32)],
        scratch_shapes=[pltpu.VMEM((NS, GW), F32)],
        compiler_params=_cp("parallel", "arbitrary"),
    )(xbc, xbc, xbc, s8, s_t, dt_e, s_e, dexp)


def _ssd_bwd(xbc, s8, s_t, dt_e, s_e, dexp, sprev_all, dy):
    n = xbc.shape[0]
    nc = n // Q
    rc = lambda c: nc - 1 - c

    def body(xs_ref, b_ref, c_ref, sc_ref, sr_ref, dte_ref, se_ref, dexp_ref, sp_ref, dy_ref,
             dxs_ref, db_ref, dc_ref, dse_ref, dde_ref, ts_ref, dsh_ref, pd_ref, dst, dxbuf):
        @pl.when(pl.program_id(1) == 0)
        def _():
            dst[...] = jnp.zeros_like(dst)
            pd_ref[...] = jnp.zeros_like(pd_ref)

        lower, upper = _tri()
        s_c, s_r, dt_e, s_e = sc_ref[...], sr_ref[...], dte_ref[...], se_ref[...]
        xs = xs_ref[...]
        x = xs * dt_e
        xb = x.astype(BF16)
        bb, cb = b_ref[...].astype(BF16), c_ref[...].astype(BF16)
        cbm = _dot(cb, bb, NT)
        cbt = _dot(bb, cb, NT)
        st_e = s_e[Q - 1:Q, :]
        dec_out, dec_st, e_t = jnp.exp(s_e), jnp.exp(st_e - s_e), jnp.exp(st_e)
        dyv = dy_ref[...]
        dyb = dyv.astype(BF16)
        sprev = sp_ref[...]
        sb = sprev.astype(BF16)
        ds_in = dst[...]
        dsb = ds_in.astype(BF16)

        cs = _dot(cb, sb)
        dcs = (dyv * dec_out).astype(BF16)
        d_c = _dot(dcs, sb, NT)
        wf = x * dec_st
        d_w = _dot(bb, dsb)
        d_b = _dot(wf.astype(BF16), dsb, NT)
        tw = d_w * wf
        dse_ref[...] = dyv * cs * dec_out - tw
        ds_c = jnp.zeros((Q, 8), F32)
        dcb = jnp.zeros((Q, Q), F32)
        dcbt = jnp.zeros((Q, Q), F32)
        lane8 = lax.broadcasted_iota(jnp.int32, (1, 8), 1)
        for h in range(HPG):
            sl = slice(h * HD, (h + 1) * HD)
            sc_h, sr_h = s_c[:, h:h + 1], s_r[h:h + 1, :]
            lm = jnp.where(lower, jnp.exp(jnp.minimum(sc_h - sr_h, 0.0)), 0.0)
            lmt = jnp.where(upper, jnp.exp(jnp.minimum(sr_h - sc_h, 0.0)), 0.0)
            mt = cbt * lmt
            dm = _dot(dyb[:, sl], xb[:, sl], NT)
            dmt = _dot(xb[:, sl], dyb[:, sl], NT)
            dxbuf[:, sl] = _dot(mt.astype(BF16), dyb[:, sl])
            dml = dm * lm
            dmlt = dmt * lmt
            dcb = dcb + dml
            dcbt = dcbt + dmlt
            dsh = jnp.sum(dml * cbm, axis=1, keepdims=True) - jnp.sum(dmlt * cbt, axis=1, keepdims=True)
            ds_c = ds_c + dsh * (lane8 == h).astype(F32)
        d_c = d_c + _dot(dcb.astype(BF16), bb)
        d_b = d_b + _dot(dcbt.astype(BF16), cb)
        dx = d_w * dec_st + dxbuf[...]
        ts_ref[...] = jnp.sum(tw, axis=0, keepdims=True) + jnp.sum(ds_in * sprev, axis=0, keepdims=True) * e_t
        dsh_ref[...] = ds_c
        dde_ref[...] = dx * xs
        pd_ref[...] += jnp.sum(dyv * xs, axis=0, keepdims=True)
        dxs_ref[...] = dx * dt_e + dyv * dexp_ref[...]
        db_ref[...] = d_b
        dc_ref[...] = d_c
        dst[...] = e_t * ds_in + _dot(cb, dcs, TN)

    wide = pl.BlockSpec((Q, GW), lambda g, c: (rc(c), g))
    state = pl.BlockSpec((Q, NS), lambda g, c: (rc(c), g))
    in_specs = _ssd_specs(nc, True) + [pl.BlockSpec((None, None, NS, GW), lambda g, c: (rc(c), g, 0, 0)), wide]
    return pl.pallas_call(
        body, name="ssd_bwd", grid=(NG, nc), in_specs=in_specs,
        out_specs=[wide, state, state, wide, wide,
                   pl.BlockSpec((None, 1, GW), lambda g, c: (rc(c), 0, g)),
                   pl.BlockSpec((None, Q, 8), lambda g, c: (g, rc(c), 0)),
                   pl.BlockSpec((None, 1, GW), lambda g, c: (g, 0, 0))],
        out_shape=[jax.ShapeDtypeStruct((n, DI), F32), jax.ShapeDtypeStruct((n, NG * NS), F32),
                   jax.ShapeDtypeStruct((n, NG * NS), F32), jax.ShapeDtypeStruct((n, DI), F32),
                   jax.ShapeDtypeStruct((n, DI), F32), jax.ShapeDtypeStruct((nc, 1, DI), F32),
                   jax.ShapeDtypeStruct((NG, n, 8), F32), jax.ShapeDtypeStruct((NG, 1, GW), F32)],
        scratch_shapes=[pltpu.VMEM((NS, GW), F32), pltpu.VMEM((Q, GW), F32)],
        compiler_params=_cp("parallel", "arbitrary"),
    )(xbc, xbc, xbc, s8, s_t, dt_e, s_e, dexp, sprev_all, dy)


GL = 128


def _gnorm_fwd(y, proj, w):
    n = y.shape[0]

    def body(y_ref, z_ref, w_ref, o_ref):
        for g in range(NG):
            sl = slice(g * GW, (g + 1) * GW)
            yz = y_ref[:, sl] * _silu(z_ref[:, sl])
            r = lax.rsqrt(jnp.mean(yz * yz, axis=-1, keepdims=True) + EPS)
            o_ref[:, sl] = (yz * r * w_ref[:, sl]).astype(BF16)

    row = pl.BlockSpec((GL, DI), lambda i: (i, 0))
    return pl.pallas_call(
        body, name="gnorm_fwd", grid=(n // GL,),
        in_specs=[row, row, pl.BlockSpec((1, DI), lambda i: (0, 0))], out_specs=row,
        out_shape=jax.ShapeDtypeStruct((n, DI), BF16), compiler_params=_cp("parallel"),
    )(y, proj, w)


def _gnorm_bwd(dyn, y, proj, w):
    n = y.shape[0]

    def body(d_ref, y_ref, z_ref, w_ref, dy_ref, dz_ref, gw_ref):
        @pl.when(pl.program_id(0) == 0)
        def _():
            gw_ref[...] = jnp.zeros_like(gw_ref)

        for g in range(NG):
            sl = slice(g * GW, (g + 1) * GW)
            yv, zv, dv = y_ref[:, sl], z_ref[:, sl], d_ref[:, sl]
            sz = _silu(zv)
            yz = yv * sz
            r = lax.rsqrt(jnp.mean(yz * yz, axis=-1, keepdims=True) + EPS)
            yh = yz * r
            gg = dv * w_ref[:, sl]
            dyz = r * (gg - yh * jnp.mean(gg * yh, axis=-1, keepdims=True))
            gw_ref[:, sl] += jnp.sum(dv * yh, axis=0, keepdims=True)
            dy_ref[:, sl] = dyz * sz
            dz_ref[:, sl] = (dyz * yv * _dsilu(zv)).astype(BF16)

    row = pl.BlockSpec((GL, DI), lambda i: (i, 0))
    vec = pl.BlockSpec((1, DI), lambda i: (0, 0))
    return pl.pallas_call(
        body, name="gnorm_bwd", grid=(n // GL,),
        in_specs=[row, row, row, vec], out_specs=[row, row, vec],
        out_shape=[jax.ShapeDtypeStruct((n, DI), F32), jax.ShapeDtypeStruct((n, DI), BF16),
                   jax.ShapeDtypeStruct((1, DI), F32)],
        compiler_params=_cp("arbitrary"),
    )(dyn, y, proj, w)


SL = 512
SB = 8
SCB = NCH // SB


def _s5_in(proj, bre, bim):
    n = proj.shape[0]
    uoff = OFF_U // 128

    def body(u_ref, br_ref, bi_ref, or_ref, oi_ref):
        u = u_ref[...].astype(BF16)
        or_ref[...] = _dot(u, br_ref[...])
        oi_ref[...] = _dot(u, bi_ref[...])

    blk = pl.BlockSpec((None, 128, SCB), lambda i, j: (j, 0, 0))
    out = pl.BlockSpec((SL, SCB), lambda i, j: (i, j))
    return pl.pallas_call(
        body, name="s5_in", grid=(n // SL, SB),
        in_specs=[pl.BlockSpec((SL, 128), lambda i, j: (i, uoff + j)), blk, blk], out_specs=[out, out],
        out_shape=[jax.ShapeDtypeStruct((n, NCH), F32)] * 2, compiler_params=_cp("parallel", "parallel"),
    )(proj, bre, bim)


SC = 256


def _s5_scan(vre, vim, tab, reverse, name):
    n = vre.shape[0]
    nl = n // SL
    ng = SL // 8
    ti = (lambda i: nl - 1 - i) if reverse else (lambda i: i)

    def body(re_ref, im_ref, tab_ref, ore_ref, oim_ref, cre, cim):
        @pl.when(pl.program_id(1) == 0)
        def _():
            cre[...] = jnp.zeros_like(cre)
            cim[...] = jnp.zeros_like(cim)

        def step(j, carry):
            cr, ci = carry
            jj = (ng - 1 - j) if reverse else j
            rows = pl.ds(pl.multiple_of(jj * 8, 8), 8)
            vr, vi = re_ref[rows, :], im_ref[rows, :]
            for t, k in enumerate((1, 2, 4)):
                sh = (8 - k) if reverse else k
                rr, ri = pltpu.roll(vr, sh, 0), pltpu.roll(vi, sh, 0)
                pr, pi = tab_ref[2 * t], tab_ref[2 * t + 1]
                vr, vi = vr + pr * rr - pi * ri, vi + pr * ri + pi * rr
            lr, li = tab_ref[6], tab_ref[7]
            vr, vi = vr + lr * cr - li * ci, vi + lr * ci + li * cr
            ore_ref[rows, :] = vr
            oim_ref[rows, :] = vi
            e = 0 if reverse else 7
            return (jnp.broadcast_to(vr[e:e + 1, :], (8, SC)), jnp.broadcast_to(vi[e:e + 1, :], (8, SC)))

        cr, ci = lax.fori_loop(0, ng, step, (cre[...], cim[...]))
        cre[...] = cr
        cim[...] = ci

    blk = pl.BlockSpec((SL, SC), lambda j, i: (ti(i), j))
    return pl.pallas_call(
        body, name=name, grid=(NCH // SC, nl),
        in_specs=[blk, blk, pl.BlockSpec((8, 8, SC), lambda j, i: (0, 0, j))], out_specs=[blk, blk],
        out_shape=[jax.ShapeDtypeStruct((n, NCH), F32)] * 2,
        scratch_shapes=[pltpu.VMEM((8, SC), F32), pltpu.VMEM((8, SC), F32)],
        compiler_params=_cp("parallel", "arbitrary"),
    )(vre, vim, tab)


def _s5_out(xre, xim, cre, cimn, proj, dvec):
    n = xre.shape[0]
    uoff = OFF_U // 128

    def body(xr_ref, xi_ref, cr_ref, ci_ref, u_ref, d_ref, y_ref, g_ref):
        y = (_dot(xr_ref[...].astype(BF16), cr_ref[...]) + _dot(xi_ref[...].astype(BF16), ci_ref[...])
             + d_ref[...] * u_ref[...])
        y_ref[...] = y
        g_ref[...] = _gelu(y).astype(BF16)

    xs = pl.BlockSpec((SL, SCB), lambda i, j: (i, j))
    blk = pl.BlockSpec((None, SCB, 128), lambda i, j: (j, 0, 0))
    out = pl.BlockSpec((SL, 128), lambda i, j: (i, j))
    return pl.pallas_call(
        body, name="s5_out", grid=(n // SL, SB),
        in_specs=[xs, xs, blk, blk, pl.BlockSpec((SL, 128), lambda i, j: (i, uoff + j)),
                  pl.BlockSpec((1, 128), lambda i, j: (0, j))],
        out_specs=[out, out],
        out_shape=[jax.ShapeDtypeStruct((n, DS5), F32), jax.ShapeDtypeStruct((n, DS5), BF16)],
        compiler_params=_cp("parallel", "parallel"),
    )(xre, xim, cre, cimn, proj, dvec)


def _s5_out_bwd(dg, ypre, crt, cimnt, proj, dvec, xre, xim):
    n = dg.shape[0]
    uoff = OFF_U // 128
    nl = n // SL

    def body(dg_ref, y_ref, cr_ref, ci_ref, u_ref, d_ref, xr_ref, xi_ref,
             gr_ref, gi_ref, dus_ref, gcr_ref, gci_ref, gd_ref):
        dy = dg_ref[...] * _dgelu(y_ref[...])
        dyb = dy.astype(BF16)
        gr_ref[...] = _dot(dyb, cr_ref[...])
        gi_ref[...] = _dot(dyb, ci_ref[...])
        dus_ref[...] = dy * d_ref[...]

        @pl.when(pl.program_id(1) == 0)
        def _():
            gcr_ref[...] = jnp.zeros_like(gcr_ref)
            gci_ref[...] = jnp.zeros_like(gci_ref)
            gd_ref[...] = jnp.zeros_like(gd_ref)

        gcr_ref[...] += _dot(xr_ref[...].astype(BF16), dyb, TN)
        gci_ref[...] -= _dot(xi_ref[...].astype(BF16), dyb, TN)
        gd_ref[...] += jnp.sum(dy * u_ref[...], axis=0, keepdims=True)

    u128 = pl.BlockSpec((SL, 128), lambda j, i: (i, j))
    xs = pl.BlockSpec((SL, SCB), lambda j, i: (i, j))
    blk = pl.BlockSpec((None, 128, SCB), lambda j, i: (j, 0, 0))
    gblk = pl.BlockSpec((None, SCB, 128), lambda j, i: (j, 0, 0))
    vec = pl.BlockSpec((1, 128), lambda j, i: (0, j))
    return pl.pallas_call(
        body, name="s5_out_bwd", grid=(SB, nl),
        in_specs=[u128, u128, blk, blk, pl.BlockSpec((SL, 128), lambda j, i: (i, uoff + j)), vec, xs, xs],
        out_specs=[xs, xs, u128, gblk, gblk, vec],
        out_shape=[jax.ShapeDtypeStruct((n, NCH), F32)] * 2 + [jax.ShapeDtypeStruct((n, DS5), F32)]
        + [jax.ShapeDtypeStruct((SB, SCB, 128), F32)] * 2 + [jax.ShapeDtypeStruct((1, DS5), F32)],
        compiler_params=_cp("parallel", "arbitrary"),
    )(dg, ypre, crt, cimnt, proj, dvec, xre, xim)


def _s5_in_bwd(are, aim, brt, bit, proj, dus, xre, xim):
    n = are.shape[0]
    uoff = OFF_U // 128
    per = SL // 8

    def body(ar_ref, ai_ref, br_ref, bi_ref, u_ref, dus_ref, xr_ref, xi_ref, pr_ref, pi_ref,
             du_ref, gbr_ref, gbi_ref, glr_ref, gli_ref):
        i = pl.program_id(1)
        ar, ai = ar_ref[...], ai_ref[...]
        arb, aib = ar.astype(BF16), ai.astype(BF16)
        du_ref[...] = (_dot(arb, br_ref[...]) + _dot(aib, bi_ref[...]) + dus_ref[...]).astype(BF16)

        @pl.when(i == 0)
        def _():
            for r in (gbr_ref, gbi_ref, glr_ref, gli_ref):
                r[...] = jnp.zeros_like(r)

        ub = u_ref[...].astype(BF16)
        gbr_ref[...] += _dot(arb, ub, TN)
        gbi_ref[...] += _dot(aib, ub, TN)
        row0 = lax.broadcasted_iota(jnp.int32, (SL, SCB), 0) == 0
        last_r = jnp.where(i > 0, pr_ref[7:8, :], 0.0)
        last_i = jnp.where(i > 0, pi_ref[7:8, :], 0.0)
        xpr = jnp.where(row0, last_r, pltpu.roll(xr_ref[...], 1, 0))
        xpi = jnp.where(row0, last_i, pltpu.roll(xi_ref[...], 1, 0))
        glr_ref[...] += jnp.sum(ar * xpr + ai * xpi, axis=0, keepdims=True)
        gli_ref[...] += jnp.sum(ai * xpr - ar * xpi, axis=0, keepdims=True)

    xs = pl.BlockSpec((SL, SCB), lambda j, i: (i, j))
    prev = pl.BlockSpec((8, SCB), lambda j, i: (jnp.maximum(i * per - 1, 0), j))
    blk = pl.BlockSpec((None, SCB, 128), lambda j, i: (j, 0, 0))
    u128 = pl.BlockSpec((SL, 128), lambda j, i: (i, j))
    vec = pl.BlockSpec((1, SCB), lambda j, i: (0, j))
    return pl.pallas_call(
        body, name="s5_in_bwd", grid=(SB, n // SL),
        in_specs=[xs, xs, blk, blk, pl.BlockSpec((SL, 128), lambda j, i: (i, uoff + j)), u128, xs, xs, prev, prev],
        out_specs=[u128, blk, blk, vec, vec],
        out_shape=[jax.ShapeDtypeStruct((n, DS5), BF16)] + [jax.ShapeDtypeStruct((SB, SCB, 128), F32)] * 2
        + [jax.ShapeDtypeStruct((1, NCH), F32)] * 2,
        compiler_params=_cp("parallel", "arbitrary"),
    )(are, aim, brt, bit, proj, dus, xre, xim, xre, xim)


MC = 1024


def _merge_specs():
    ga = pl.BlockSpec((TL, MC), lambda i, j: (i, OFF_GA // MC + j))
    gb = pl.BlockSpec((TL, MC), lambda i, j: (i, OFF_GB // MC + j))
    col = pl.BlockSpec((TL, MC), lambda i, j: (i, j))
    gate = pl.BlockSpec((TL, MC), lambda i, j: (i, D // MC + j))
    return ga, gb, col, gate


def _merge_fwd(proj, ya, vg):
    n = ya.shape[0]

    def body(ga_ref, gb_ref, ya_ref, v_ref, g_ref, o_ref):
        yb = v_ref[...] * _sig(g_ref[...])
        o_ref[...] = (_sig(ga_ref[...]) * ya_ref[...] + _sig(gb_ref[...]) * yb).astype(BF16)

    ga, gb, col, gate = _merge_specs()
    return pl.pallas_call(
        body, name="merge_fwd", grid=(n // TL, D // MC), in_specs=[ga, gb, col, col, gate], out_specs=col,
        out_shape=jax.ShapeDtypeStruct((n, D), BF16), compiler_params=_cp("parallel", "parallel"),
    )(proj, proj, ya, vg, vg)


def _merge_bwd(dm, proj, ya, vg):
    n = ya.shape[0]

    def body(dm_ref, ga_ref, gb_ref, ya_ref, v_ref, g_ref, dga_ref, dgb_ref, dya_ref, dv_ref, dg_ref):
        d = dm_ref[...]
        sa, sb, sg = _sig(ga_ref[...]), _sig(gb_ref[...]), _sig(g_ref[...])
        v = v_ref[...]
        yb = v * sg
        dga_ref[...] = (d * ya_ref[...] * sa * (1.0 - sa)).astype(BF16)
        dgb_ref[...] = (d * yb * sb * (1.0 - sb)).astype(BF16)
        dya_ref[...] = (d * sa).astype(BF16)
        dyb = d * sb
        dv_ref[...] = (dyb * sg).astype(BF16)
        dg_ref[...] = (dyb * v * sg * (1.0 - sg)).astype(BF16)

    ga, gb, col, gate = _merge_specs()
    o = jax.ShapeDtypeStruct((n, D), BF16)
    return pl.pallas_call(
        body, name="merge_bwd", grid=(n // TL, D // MC), in_specs=[col, ga, gb, col, col, gate],
        out_specs=[col] * 5, out_shape=[o] * 5, compiler_params=_cp("parallel", "parallel"),
    )(dm, proj, proj, ya, vg, vg)


def _adamw_update(wv, gv, mv, vv):
    nm = B1 * mv + (1.0 - B1) * gv
    nv = B2 * vv + (1.0 - B2) * (gv * gv)
    m_hat = nm / (1.0 - B1 ** STEP)
    v_hat = nv / (1.0 - B2 ** STEP)
    return -LR * (m_hat / (jnp.sqrt(v_hat) + AEPS) + WD * wv), nm, nv


def _adamw(w, g, m, v, name):
    r, c = w.shape
    tr = _pick(r, 128)

    def body(w_ref, g_ref, m_ref, v_ref, d_ref, nm_ref, nv_ref):
        d_ref[...], nm_ref[...], nv_ref[...] = _adamw_update(w_ref[...], g_ref[...], m_ref[...], v_ref[...])

    blk = pl.BlockSpec((tr, c), lambda i: (i, 0))
    o = jax.ShapeDtypeStruct((r, c), F32)
    return pl.pallas_call(
        body, name=name, grid=(r // tr,), in_specs=[blk] * 4, out_specs=[blk] * 3, out_shape=[o] * 3,
        compiler_params=_cp("parallel"),
    )(w, g, m, v)


def _adamw_halves(w, g_mine, g_other, m, v, cidx, name):
    _, r, c = w.shape
    hr, gc = g_mine.shape
    tr = _pick(hr, 128)
    nbh = hr // tr
    assert gc == c and 2 * hr - tr < r <= 2 * hr

    def body(cs, w_ref, gm_ref, go_ref, m_ref, v_ref, g_ref, d_ref, nm_ref, nv_ref):
        mine = pl.program_id(0) // nbh == cs[0]
        gv = jnp.where(mine, gm_ref[...], go_ref[...])
        g_ref[...] = gv
        d_ref[...], nm_ref[...], nv_ref[...] = _adamw_update(w_ref[...], gv, m_ref[...], v_ref[...])

    blk = pl.BlockSpec((None, tr, c), lambda i, cs: (0, i, 0))
    gblk = pl.BlockSpec((tr, gc), lambda i, cs: (i % nbh, 0))
    o = jax.ShapeDtypeStruct((1, r, c), F32)
    return pl.pallas_call(
        body, name=name,
        grid_spec=pltpu.PrefetchScalarGridSpec(num_scalar_prefetch=1, grid=(2 * nbh,),
                                               in_specs=[blk, gblk, gblk, blk, blk], out_specs=[blk] * 4),
        out_shape=[o] * 4, compiler_params=_cp("parallel"),
    )(cidx, w, g_mine, g_other, m, v)


def _chip_sum(part, sib, cidx, name):
    _, r, cc = part.shape
    hr = r // 2
    tr = _pick(hr, 256)

    def body(cs, p_ref, s_ref, o_ref):
        o_ref[...] = (p_ref[...].astype(F32) + s_ref[...].astype(F32)).astype(BF16)

    blk = pl.BlockSpec((None, tr, cc), lambda k, i, cs: (k, i, 0))
    return pl.pallas_call(
        body, name=name,
        grid_spec=pltpu.PrefetchScalarGridSpec(
            num_scalar_prefetch=1, grid=(4, hr // tr),
            in_specs=[pl.BlockSpec((None, None, tr, cc), lambda k, i, cs: (k, cs[0], i, 0)), blk], out_specs=blk),
        out_shape=jax.ShapeDtypeStruct((4, hr, cc), BF16), compiler_params=_cp("parallel", "parallel"),
    )(cidx, part.reshape(4, 2, hr, cc), sib)


def _shard_sum(own, got, sidx, name):
    _, hr, cc = own.shape
    tr = _pick(hr, 256)

    def body(cs, own_ref, g0, g1, g2, g3, o_ref):
        acc = None
        for k, g_ref in enumerate((g0, g1, g2, g3)):
            term = jnp.where(cs[0] == k, own_ref[...], g_ref[...]).astype(F32)
            acc = term if acc is None else acc + term
        o_ref[...] = acc

    def got_spec(k):
        return pl.BlockSpec((None, tr, cc), lambda i, cs: (jnp.where(cs[0] == k, (k + 1) % 4, k), i, 0))

    return pl.pallas_call(
        body, name=name,
        grid_spec=pltpu.PrefetchScalarGridSpec(
            num_scalar_prefetch=1, grid=(hr // tr,),
            in_specs=[pl.BlockSpec((None, tr, cc), lambda i, cs: (cs[0], i, 0))] + [got_spec(k) for k in range(4)],
            out_specs=pl.BlockSpec((tr, cc), lambda i, cs: (i, 0))),
        out_shape=jax.ShapeDtypeStruct((hr, cc), F32), compiler_params=_cp("parallel"),
    )(sidx, own, got, got, got, got)


def _sum_slabs(xs, name, out_dtype=F32):
    r, c = xs[0].shape
    tr = _pick(r, 256)

    def body(*refs):
        acc = refs[0][...].astype(F32)
        for ref in refs[1:-1]:
            acc = acc + ref[...].astype(F32)
        refs[-1][...] = acc.astype(out_dtype)

    blk = pl.BlockSpec((tr, c), lambda i: (i, 0))
    return pl.pallas_call(
        body, name=name, grid=(r // tr,), in_specs=[blk] * len(xs), out_specs=blk,
        out_shape=jax.ShapeDtypeStruct((r, c), out_dtype), compiler_params=_cp("parallel"),
    )(*xs)


def _place():
    return lax.axis_index("x"), lax.axis_index("y"), lax.axis_index("c")


def _gather_small(v):
    m_per, n = v.shape

    def body(x_ref, out_ref, send_sems, recv_sems, local_sem):
        x, y, c = _place()
        me, sibling = (x, y, c), (x, y, 1 - c)
        chips = [(1 - x, y), (x, 1 - y), (1 - x, 1 - y)]

        def rows(px, py, pc):
            return out_ref.at[pl.ds((4 * px + 2 * py + pc) * m_per, m_per), :]

        def copy(k, block, to, src=None):
            return pltpu.make_async_remote_copy(
                src_ref=rows(*block) if src is None else src, dst_ref=rows(*block),
                send_sem=send_sems.at[k], recv_sem=recv_sems.at[k], device_id=to, device_id_type=MESH)

        mine = pltpu.make_async_copy(x_ref, rows(*me), local_sem)
        mine.start()
        first = [copy(0, me, sibling, src=x_ref)]
        first += [copy(1 + j, me, (*chip, c), src=x_ref) for j, chip in enumerate(chips)]
        for cp in first:
            cp.start()
        passed = [copy(4 + j, (*chip, c), sibling) for j, chip in enumerate(chips)]
        for j, chip in enumerate(chips):
            copy(1 + j, (*chip, c), me).wait_recv()
            passed[j].start()
        copy(0, sibling, me).wait_recv()
        for j, chip in enumerate(chips):
            copy(4 + j, (*chip, 1 - c), me).wait_recv()
        for cp in first + passed:
            cp.wait_send()
        mine.wait()

    return pl.pallas_call(
        body, name="gather_small_%d" % m_per,
        out_shape=jax.ShapeDtypeStruct((8 * m_per, n), v.dtype),
        in_specs=[pl.BlockSpec(memory_space=pltpu.VMEM)], out_specs=pl.BlockSpec(memory_space=pltpu.VMEM),
        scratch_shapes=[pltpu.SemaphoreType.DMA((7,)), pltpu.SemaphoreType.DMA((7,)), pltpu.SemaphoreType.DMA],
        compiler_params=pltpu.CompilerParams(vmem_limit_bytes=VMEM_LIMIT),
    )(v)


def _allsum_small(v, name):
    r = v.shape[0]
    g = _gather_small(v)
    return _sum_slabs([g[k * r:(k + 1) * r] for k in range(8)], name)


def _gather_big(shards):
    nt = len(shards)

    def body(*refs):
        ins, outs = refs[:nt], refs[nt:2 * nt]
        send_sems, recv_sems = refs[2 * nt:]
        x, y, c = _place()
        s = 2 * x + y
        sibling = (x, y, 1 - c)
        chips = [(1 - x, y), (x, 1 - y), (1 - x, 1 - y)]

        def half(t, slot, h):
            hr = ins[t].shape[0] // 2
            return outs[t].at[slot, pl.ds(h * hr, hr), :]

        def ici(t, j, src_slot, to):
            hr = ins[t].shape[0] // 2
            return pltpu.make_async_remote_copy(
                src_ref=ins[t].at[pl.ds(c * hr, hr), :], dst_ref=half(t, src_slot, c),
                send_sem=send_sems.at[7 * t + j], recv_sem=recv_sems.at[7 * t + j], device_id=to, device_id_type=MESH)

        def d2d(t, j, slot, h):
            return pltpu.make_async_remote_copy(
                src_ref=half(t, slot, h), dst_ref=half(t, slot, h),
                send_sem=send_sems.at[7 * t + 3 + j], recv_sem=recv_sems.at[7 * t + 3 + j],
                device_id=sibling, device_id_type=MESH)

        def whole(t):
            return pltpu.make_async_remote_copy(
                src_ref=ins[t], dst_ref=outs[t].at[s], send_sem=send_sems.at[7 * t + 6],
                recv_sem=recv_sems.at[7 * t + 6], device_id=sibling, device_id_type=MESH)

        sends = [ici(t, j, s, (*chip, c)) for t in range(nt) for j, chip in enumerate(chips)]
        sends += [whole(t) for t in range(nt)]
        for cp in sends:
            cp.start()
        passed = []
        for t in range(nt):
            for j, (px, py) in enumerate(chips):
                ici(t, j, 2 * px + py, (x, y, c)).wait_recv()
                cp = d2d(t, j, 2 * px + py, c)
                cp.start()
                passed.append(cp)
        for t in range(nt):
            for j, (px, py) in enumerate(chips):
                d2d(t, j, 2 * px + py, 1 - c).wait_recv()
            whole(t).wait_recv()
        for cp in sends + passed:
            cp.wait_send()

    return pl.pallas_call(
        body, name="gather_big",
        out_shape=[jax.ShapeDtypeStruct((4,) + a.shape, a.dtype) for a in shards],
        in_specs=[ANY] * nt, out_specs=[ANY] * nt,
        scratch_shapes=[pltpu.SemaphoreType.DMA((7 * nt,)), pltpu.SemaphoreType.DMA((7 * nt,))],
    )(*shards)


def _swap_halves(parts, name):
    nt = len(parts)

    def body(*refs):
        ins, outs = refs[:nt], refs[nt:2 * nt]
        send_sems, recv_sems = refs[2 * nt:]
        x, y, c = _place()
        cps = []
        for t in range(nt):
            hr = ins[t].shape[1] // 2
            cps.append(pltpu.make_async_remote_copy(
                src_ref=ins[t].at[:, pl.ds((1 - c) * hr, hr), :], dst_ref=outs[t],
                send_sem=send_sems.at[t], recv_sem=recv_sems.at[t], device_id=(x, y, 1 - c), device_id_type=MESH))
        for cp in cps:
            cp.start()
        for cp in cps:
            cp.wait()

    return pl.pallas_call(
        body, name=name,
        out_shape=[jax.ShapeDtypeStruct((4, a.shape[1] // 2, a.shape[2]), a.dtype) for a in parts],
        in_specs=[ANY] * nt, out_specs=[ANY] * nt,
        scratch_shapes=[pltpu.SemaphoreType.DMA((nt,)), pltpu.SemaphoreType.DMA((nt,))],
    )(*parts)


def _scatter_chips(parts):
    nt = len(parts)

    def body(*refs):
        ins, outs = refs[:nt], refs[nt:2 * nt]
        send_sems, recv_sems = refs[2 * nt:]
        x, y, c = _place()
        s = 2 * x + y
        chips = [(1 - x, y), (x, 1 - y), (1 - x, 1 - y)]
        cps = []
        for t in range(nt):
            for j, (px, py) in enumerate(chips):
                cps.append(pltpu.make_async_remote_copy(
                    src_ref=ins[t].at[2 * px + py], dst_ref=outs[t].at[s],
                    send_sem=send_sems.at[3 * t + j], recv_sem=recv_sems.at[3 * t + j],
                    device_id=(px, py, c), device_id_type=MESH))
        for cp in cps:
            cp.start()
        for t in range(nt):
            for j, (px, py) in enumerate(chips):
                pltpu.make_async_remote_copy(
                    src_ref=ins[t].at[s], dst_ref=outs[t].at[2 * px + py],
                    send_sem=send_sems.at[3 * t + j], recv_sem=recv_sems.at[3 * t + j],
                    device_id=(px, py, c), device_id_type=MESH).wait_recv()
        for cp in cps:
            cp.wait_send()

    return pl.pallas_call(
        body, name="scatter_chips",
        out_shape=[jax.ShapeDtypeStruct(a.shape, a.dtype) for a in parts],
        in_specs=[ANY] * nt, out_specs=[ANY] * nt,
        scratch_shapes=[pltpu.SemaphoreType.DMA((3 * nt,)), pltpu.SemaphoreType.DMA((3 * nt,))],
    )(*parts)


def _swap_whole(halves):
    nt = len(halves)

    def body(*refs):
        ins, outs = refs[:nt], refs[nt:2 * nt]
        send_sems, recv_sems = refs[2 * nt:]
        x, y, c = _place()
        cps = [pltpu.make_async_remote_copy(
            src_ref=ins[t], dst_ref=outs[t], send_sem=send_sems.at[t], recv_sem=recv_sems.at[t],
            device_id=(x, y, 1 - c), device_id_type=MESH) for t in range(nt)]
        for cp in cps:
            cp.start()
        for cp in cps:
            cp.wait()

    return pl.pallas_call(
        body, name="swap_whole",
        out_shape=[jax.ShapeDtypeStruct(a.shape, a.dtype) for a in halves],
        in_specs=[ANY] * nt, out_specs=[ANY] * nt,
        scratch_shapes=[pltpu.SemaphoreType.DMA((nt,)), pltpu.SemaphoreType.DMA((nt,))],
    )(*halves)


def _pass_halves(got, shards, name):
    nt = len(got)

    def body(*refs):
        ins, own, outs = refs[:nt], refs[nt:2 * nt], refs[2 * nt:3 * nt]
        send_sems, recv_sems = refs[3 * nt:]
        x, y, c = _place()
        s = 2 * x + y
        chips = [(1 - x, y), (x, 1 - y), (1 - x, 1 - y)]

        def half(ref, t, slot, h):
            hr = ins[t].shape[1] // 2
            return ref.at[slot, pl.ds(h * hr, hr), :]

        def copy(t, j, h):
            px, py = chips[j]
            return pltpu.make_async_remote_copy(
                src_ref=half(ins[t], t, 2 * px + py, h), dst_ref=half(outs[t], t, 2 * px + py, h),
                send_sem=send_sems.at[4 * t + j], recv_sem=recv_sems.at[4 * t + j],
                device_id=(x, y, 1 - c), device_id_type=MESH)

        def whole(t):
            return pltpu.make_async_remote_copy(
                src_ref=own[t], dst_ref=outs[t].at[s], send_sem=send_sems.at[4 * t + 3],
                recv_sem=recv_sems.at[4 * t + 3], device_id=(x, y, 1 - c), device_id_type=MESH)

        sends = [copy(t, j, c) for t in range(nt) for j in range(3)] + [whole(t) for t in range(nt)]
        for cp in sends:
            cp.start()
        for t in range(nt):
            for j in range(3):
                copy(t, j, 1 - c).wait_recv()
            whole(t).wait_recv()
        for cp in sends:
            cp.wait_send()

    return pl.pallas_call(
        body, name=name,
        out_shape=[jax.ShapeDtypeStruct(a.shape, a.dtype) for a in got],
        in_specs=[ANY] * (2 * nt), out_specs=[ANY] * nt, input_output_aliases={t: t for t in range(nt)},
        scratch_shapes=[pltpu.SemaphoreType.DMA((4 * nt,)), pltpu.SemaphoreType.DMA((4 * nt,))],
    )(*got, *shards)


HBM = pl.BlockSpec(memory_space=pltpu.HBM)
SEM = pl.BlockSpec(memory_space=pltpu.SEMAPHORE)
EFFECT = pltpu.SideEffectType.DATAFLOW_SIDE_EFFECTING


def _ici_copies(kind, srcs, lands, send_sems, recv_sems):
    x, y, c = _place()
    s = 2 * x + y
    cps = []
    for t in range(len(srcs)):
        for j, (px, py) in enumerate([(1 - x, y), (x, 1 - y), (1 - x, 1 - y)]):
            if kind == "gather":
                hr = srcs[t].shape[0] // 2
                src, dst = srcs[t].at[pl.ds(c * hr, hr), :], lands[t].at[s, pl.ds(c * hr, hr), :]
            else:
                src, dst = srcs[t].at[2 * px + py], lands[t].at[s]
            cps.append(pltpu.make_async_remote_copy(
                src_ref=src, dst_ref=dst, send_sem=send_sems[3 * t + j], recv_sem=recv_sems[3 * t + j],
                device_id=(px, py, c), device_id_type=MESH))
    return cps


def _ici_start(kind, srcs, after, name):
    nt = len(srcs)
    nc = 3 * nt
    lands = [(4,) + a.shape if kind == "gather" else a.shape for a in srcs]

    def body(*refs):
        src, land = refs[:nt], refs[nt:2 * nt]
        outs = refs[2 * nt + 1:]
        for cp in _ici_copies(kind, src, land, outs[:nc], outs[nc:2 * nc]):
            cp.start()
        outs[-1][...] = jnp.zeros_like(outs[-1])

    hbm = lambda a: pltpu.with_memory_space_constraint(a, pltpu.HBM)
    outs = pl.pallas_call(
        body, name=name,
        out_shape=tuple([pltpu.SemaphoreType.DMA(())] * (2 * nc) + [pltpu.HBM(a.shape, a.dtype) for a in srcs]
                        + [pltpu.HBM(sh, a.dtype) for sh, a in zip(lands, srcs)] + [jax.ShapeDtypeStruct((8, 128), F32)]),
        in_specs=[HBM] * (2 * nt) + [ANY],
        out_specs=tuple([SEM] * (2 * nc) + [HBM] * (2 * nt) + [pl.BlockSpec(memory_space=pltpu.VMEM)]),
        input_output_aliases={i: 2 * nc + i for i in range(2 * nt)},
        compiler_params=pltpu.CompilerParams(has_side_effects=EFFECT),
    )(*[hbm(a) for a in srcs], *[hbm(lax.empty(sh, a.dtype)) for sh, a in zip(lands, srcs)], after)
    return outs[:2 * nc], outs[2 * nc:2 * nc + nt], outs[2 * nc + nt:2 * nc + 2 * nt], outs[-1]


def _ici_wait(kind, sems, srcs, lands, after, name):
    nt = len(srcs)
    nc = 3 * nt

    def body(*refs):
        src, land = refs[:nt], refs[nt:2 * nt]
        sem = refs[2 * nt:2 * nt + 2 * nc]
        for cp in _ici_copies(kind, src, land, sem[:nc], sem[nc:]):
            cp.wait_send()
            cp.wait_recv()

    outs = pl.pallas_call(
        body, name=name,
        out_shape=tuple(pltpu.HBM(a.shape, a.dtype) for a in list(srcs) + list(lands)),
        in_specs=[HBM] * (2 * nt) + [SEM] * (2 * nc) + [ANY], out_specs=tuple([HBM] * (2 * nt)),
        input_output_aliases={i: i for i in range(2 * nt)},
        compiler_params=pltpu.CompilerParams(has_side_effects=EFFECT),
    )(*srcs, *lands, *sems, after)
    return outs[:nt], outs[nt:]


def _s5_params(lam_re, lam_im, log_dt, b_re, b_im):
    lr = jnp.minimum(lam_re, EIG_MAX)
    dt = jnp.exp(log_dt)[:, None]
    mag = jnp.exp(lr * dt)
    lbr, lbi = mag * jnp.cos(lam_im * dt), mag * jnp.sin(lam_im * dt)
    den = lr * lr + lam_im * lam_im
    qr = ((lbr - 1.0) * lr + lbi * lam_im) / den
    qi = (lbi * lr - (lbr - 1.0) * lam_im) / den
    bbr = qr[..., None] * b_re - qi[..., None] * b_im
    bbi = qr[..., None] * b_im + qi[..., None] * b_re
    return lbr, lbi, bbr, bbi


def _cmul(a, b):
    return a[0] * b[0] - a[1] * b[1], a[0] * b[1] + a[1] * b[0]


def _scan_table(lr, li, reverse):
    l1 = (lr.reshape(1, NCH), li.reshape(1, NCH))
    pows = [l1]
    for _ in range(7):
        pows.append(_cmul(pows[-1], l1))
    r = jnp.arange(8)[:, None]
    tabs = []
    for k in (1, 2, 4):
        keep = (r < 8 - k) if reverse else (r >= k)
        tabs += [jnp.where(keep, pows[k - 1][0], 0.0), jnp.where(keep, pows[k - 1][1], 0.0)]
    order = range(7, -1, -1) if reverse else range(8)
    tabs += [jnp.concatenate([pows[e][0] for e in order], axis=0), jnp.concatenate([pows[e][1] for e in order], axis=0)]
    return jnp.stack(tabs).astype(F32)


_EYE8 = lambda: jnp.eye(8, dtype=F32)


def _to_in_blocks(b):
    return jnp.einsum("jgpc,gh->jgchp", b.reshape(8, 8, 64, 16), _EYE8()).reshape(8, 128, 512)


def _to_out_blocks(cm):
    return jnp.einsum("jgcp,gh->jgphc", cm.reshape(8, 8, 16, 64), _EYE8()).reshape(8, 512, 128)


def _from_out_blocks(g):
    return jnp.einsum("jgphc,gh->jgpc", g.reshape(8, 8, 64, 8, 16), _EYE8()).reshape(64, 64, 16)


def _local_step(x, target, hn1, p, token, late_weights, send_grads):
    n = x.shape[0]
    g = {}
    proj = _matmul(hn1, p["w_full"], "nt", "mm_in", tn=896, after=token)
    dtraw = proj
    xbc = _conv_a_fwd(proj, p["conv_a_w"], p["conv_a_b"])
    to_lanes = lambda v: jnp.pad(jnp.pad(v.reshape(NG, HPG), ((0, 0), (0, 8 - HPG))).reshape(1, 8 * NG),
                                 ((0, 0), (0, HL - 8 * NG)))
    from_lanes = lambda v: v[:, :8 * NG].reshape(-1, NG, 8)[:, :, :HPG].reshape(-1, NG * HPG)
    hp = jnp.concatenate([to_lanes(p["dt_bias"]), to_lanes(p["a_log"]), jnp.zeros((6, HL), F32)], axis=0)
    lane = jnp.arange(HL)[:, None]
    emat = ((lane < 8 * NG) & (lane % 8 < HPG)
            & (jnp.arange(DI)[None, :] // HD == HPG * (lane // 8) + lane % 8)).astype(BF16)
    dexp = jnp.repeat(p["d_a"].reshape(1, NG * HPG), HD, axis=1)
    dt, s_cum, s_t, dt_e, s_e = _ssd_prep(dtraw, hp, emat)
    s8 = s_cum[:, :8 * NG].reshape(n, NG, 8).transpose(1, 0, 2)
    yssd, sprev = _ssd_fwd(xbc, s8, s_t, dt_e, s_e, dexp)
    yn = _gnorm_fwd(yssd, proj, p["norm_a_w"])
    p = {**p, **late_weights(yn)}
    ya = _matmul(yn, p["w_proj_a"], "nn", "mm_proj")

    (lbr, lbi, bbr, bbi), s5_vjp = jax.vjp(_s5_params, p["s5_lam_re"], p["s5_lam_im"], p["s5_log_dt"],
                                           p["s5_b_re"], p["s5_b_im"])
    bin_r, bin_i = _to_in_blocks(bbr), _to_in_blocks(bbi)
    cout_r, cout_in = _to_out_blocks(p["s5_c_re"]), _to_out_blocks(-p["s5_c_im"])
    bur, bui = _s5_in(proj, bin_r.astype(BF16), bin_i.astype(BF16))
    xre, xim = _s5_scan(bur, bui, _scan_table(lbr, lbi, False), False, "s5_scan_fwd")
    ypre, g5 = _s5_out(xre, xim, cout_r.astype(BF16), cout_in.astype(BF16), proj, p["s5_d"])
    vg = _matmul(g5, p["w_s5_glu"], "nn", "mm_glu", b_stacked=True)
    merged = _merge_fwd(proj, ya, vg)
    h1 = _matmul(merged, p["w_out"], "nn", "mm_out", residual=x)
    hn2 = _rms_fwd(h1, p["norm_ffn_w"], "rms_ffn")
    up = _matmul(hn2, p["w_up"], "nn", "mm_up", tn=1408, b_stacked=True)
    act = _conv_ffn_fwd(up, p["conv_ffn_w"], p["conv_ffn_b"])
    h2 = _matmul(act, p["w_down"], "nn", "mm_down", tk=DFF // 2, residual=h1)
    dh2, dh2b, g["norm_final_w"], loss_blk = _final(h2, p["norm_final_w"], target)
    g["w_down"] = _matmul(act, dh2b, "tn", "mm_gw_down", out_dtype=BF16).reshape(4, DFF // 4, D)
    dact = _matmul(dh2b, p["w_down"], "nt", "mm_dact", out_dtype=BF16)
    dup, g["conv_ffn_w"], g["conv_ffn_b"] = _conv_ffn_bwd(up, dact, p["conv_ffn_w"], p["conv_ffn_b"])
    g["w_up"] = _matmul(hn2, dup, "tn", "mm_gw_up", out_dtype=BF16, tn=1408, out_stacked=True)
    tok = send_grads(["w_up", "w_down"], g, "s1")
    dhn2 = _matmul(dup, p["w_up"], "nt", "mm_dhn2", tk=2816, b_stacked=True, after=tok)
    dh1, dh1b, g["norm_ffn_w"] = _rms_bwd(dhn2, h1, p["norm_ffn_w"], dh2, "rms_ffn_bwd")
    g["w_out"] = _matmul(merged, dh1b, "tn", "mm_gw_out", out_dtype=BF16).reshape(4, D // 4, D)
    dmerged = _matmul(dh1b, p["w_out"], "nt", "mm_dmerged")
    dga, dgb, dya, dval, dgate = _merge_bwd(dmerged, proj, ya, vg)
    dvg = jnp.concatenate([dval, dgate], axis=1)
    g["w_s5_glu"] = _matmul(g5, dvg, "tn", "mm_gw_glu", out_dtype=BF16, out_stacked=True)
    dg5 = _matmul(dvg, p["w_s5_glu"], "nt", "mm_dg5", b_stacked=True)
    tr = lambda b: b.transpose(0, 2, 1)
    gxr, gxi, dus, gcr, gci, g["s5_d"] = _s5_out_bwd(dg5, ypre, tr(cout_r).astype(BF16), tr(cout_in).astype(BF16),
                                                     proj, p["s5_d"], xre, xim)
    are, aim = _s5_scan(gxr, gxi, _scan_table(lbr, -lbi, True), True, "s5_scan_bwd")
    du, gbr, gbi, glr, gli = _s5_in_bwd(are, aim, tr(bin_r).astype(BF16), tr(bin_i).astype(BF16), proj, dus, xre, xim)
    g["s5_c_re"] = _from_out_blocks(gcr).transpose(0, 2, 1)
    g["s5_c_im"] = _from_out_blocks(gci).transpose(0, 2, 1)
    (g["s5_lam_re"], g["s5_lam_im"], g["s5_log_dt"], g["s5_b_re"], g["s5_b_im"]) = s5_vjp(
        (glr.reshape(64, 64), gli.reshape(64, 64), _from_out_blocks(gbr), _from_out_blocks(gbi)))
    g["w_proj_a"] = _matmul(yn, dya, "tn", "mm_gw_proj", out_dtype=BF16).reshape(4, DI // 4, D)
    tok = send_grads(["w_proj_a", "w_s5_glu", "w_out"], g, "s2")
    dyn = _matmul(dya, p["w_proj_a"], "nt", "mm_dyn", after=tok)
    dyssd, dz, g["norm_a_w"] = _gnorm_bwd(dyn, yssd, proj, p["norm_a_w"])
    dxs, dbm, dcm, ds_e, ddt_e, tsum, dsh8, pd = _ssd_bwd(xbc, s8, s_t, dt_e, s_e, dexp, sprev, dyssd)
    dsh = jnp.pad(dsh8.transpose(1, 0, 2).reshape(n, 8 * NG), ((0, 0), (0, HL - 8 * NG)))
    draw, ps = _ssd_post(ds_e, ddt_e, tsum, dsh, dtraw, dt, hp, emat.T)
    g["dt_bias"] = from_lanes(ps[0:1])
    g["a_log"] = from_lanes(ps[1:2])
    g["d_a"] = pd.reshape(NG * HPG, HD).sum(axis=1).reshape(1, NG * HPG)
    ddt = draw.astype(BF16)
    dxbc_parts, gcw, gcb = [], [], []
    for arr, col0, nm in ((dxs, 0, "conv_a_bwd_x"), (dbm, DI, "conv_a_bwd_b"), (dcm, DI + NG * NS, "conv_a_bwd_c")):
        dpart, gw_, gb_ = _conv_a_bwd(proj, arr, p["conv_a_w"], p["conv_a_b"], col0, nm)
        dxbc_parts.append(dpart)
        gcw.append(gw_)
        gcb.append(gb_)
    g["conv_a_w"] = jnp.concatenate(gcw, axis=1)
    g["conv_a_b"] = jnp.concatenate(gcb, axis=1)
    dproj = jnp.concatenate([dz] + dxbc_parts + [du, dga, dgb, ddt], axis=1)
    g_main = _matmul(dproj, hn1, "tn", "mm_gw_in", out_dtype=BF16, tm=896, tn=2048)
    g_dt = g_main[NMAIN:NMAIN + 8 * NG].reshape(NG, 8, D)[:, :HPG].reshape(NG * HPG, D)
    g_sh = _move_rows(g_main, RUNS_TO_SHARDS, 4 * WPAD, MT, MT, "rows_to_shards")
    g["w_in"] = lax.dynamic_update_slice(g_sh, g_dt, (DT_SHARD_ROW, 0)).reshape(4, WPAD, D)
    tok = send_grads(["w_in"], g, "s3")
    dhn1 = _matmul(dproj, p["w_full"], "nn", "mm_dhn1", tk=2688, after=tok)
    gx, _, g["norm_mix_w"] = _rms_bwd(dhn1, x, p["norm_mix_w"], dh1, "rms_mix_bwd")
    return loss_blk, gx, g


BIG = ["w_in", "w_proj_a", "w_s5_glu", "w_out", "w_up", "w_down"]
SMALL = ["norm_mix_w", "conv_a_w", "conv_a_b", "dt_bias", "a_log", "d_a", "norm_a_w", "s5_lam_re", "s5_lam_im",
         "s5_log_dt", "s5_b_re", "s5_b_im", "s5_c_re", "s5_c_im", "s5_d", "norm_ffn_w", "conv_ffn_w", "conv_ffn_b",
         "norm_final_w"]
ORDER = ["norm_mix_w", "w_in", "conv_a_w", "conv_a_b", "dt_bias", "a_log", "d_a", "norm_a_w", "w_proj_a", "s5_lam_re",
         "s5_lam_im", "s5_log_dt", "s5_b_re", "s5_b_im", "s5_c_re", "s5_c_im", "s5_d", "w_s5_glu", "w_out",
         "norm_ffn_w", "w_up", "conv_ffn_w", "conv_ffn_b", "w_down", "norm_final_w"]
CONV_FULL = {"conv_a_w": (KA, CONVD), "conv_ffn_w": (KF, 2 * DFF)}


def _pack(arrs):
    flat = jnp.concatenate([a.reshape(-1).astype(F32) for a in arrs])
    total = flat.shape[0]
    padded = -(-total // 1024) * 1024
    return jnp.pad(flat, (0, padded - total)).reshape(padded // 128, 128)


def _unpack(block, shapes):
    flat = block.reshape(-1)
    out, at = [], 0
    for sh in shapes:
        size = math.prod(sh)
        out.append(flat[at:at + size].reshape(sh))
        at += size
    return out


def _stack_cols(a):
    return a.transpose(1, 0, 2).reshape(a.shape[1], 4 * a.shape[2])


def _unstack_cols(a):
    return a.reshape(a.shape[0], 4, a.shape[1] // 4).transpose(1, 0, 2)


def kernel(x, norm_mix_w, w_in, conv_a_w, conv_a_b, dt_bias, a_log, d_a, norm_a_w, w_proj_a, s5_lam_re, s5_lam_im, s5_log_dt, s5_b_re, s5_b_im, s5_c_re, s5_c_im, s5_d, w_s5_glu, w_out, norm_ffn_w, w_up, conv_ffn_w, conv_ffn_b, w_down, norm_final_w, loss_target, m_norm_mix_w, m_w_in, m_conv_a_w, m_conv_a_b, m_dt_bias, m_a_log, m_d_a, m_norm_a_w, m_w_proj_a, m_s5_lam_re, m_s5_lam_im, m_s5_log_dt, m_s5_b_re, m_s5_b_im, m_s5_c_re, m_s5_c_im, m_s5_d, m_w_s5_glu, m_w_out, m_norm_ffn_w, m_w_up, m_conv_ffn_w, m_conv_ffn_b, m_w_down, m_norm_final_w, v_norm_mix_w, v_w_in, v_conv_a_w, v_conv_a_b, v_dt_bias, v_a_log, v_d_a, v_norm_a_w, v_w_proj_a, v_s5_lam_re, v_s5_lam_im, v_s5_log_dt, v_s5_b_re, v_s5_b_im, v_s5_c_re, v_s5_c_im, v_s5_d, v_w_s5_glu, v_w_out, v_norm_ffn_w, v_w_up, v_conv_ffn_w, v_conv_ffn_b, v_w_down, v_norm_final_w):
    w = dict(norm_mix_w=norm_mix_w, w_in=w_in, conv_a_w=conv_a_w, conv_a_b=conv_a_b, dt_bias=dt_bias, a_log=a_log, d_a=d_a, norm_a_w=norm_a_w, w_proj_a=w_proj_a, s5_lam_re=s5_lam_re, s5_lam_im=s5_lam_im, s5_log_dt=s5_log_dt, s5_b_re=s5_b_re, s5_b_im=s5_b_im, s5_c_re=s5_c_re, s5_c_im=s5_c_im, s5_d=s5_d, w_s5_glu=w_s5_glu, w_out=w_out, norm_ffn_w=norm_ffn_w, w_up=w_up, conv_ffn_w=conv_ffn_w, conv_ffn_b=conv_ffn_b, w_down=w_down, norm_final_w=norm_final_w)
    m = dict(norm_mix_w=m_norm_mix_w, w_in=m_w_in, conv_a_w=m_conv_a_w, conv_a_b=m_conv_a_b, dt_bias=m_dt_bias, a_log=m_a_log, d_a=m_d_a, norm_a_w=m_norm_a_w, w_proj_a=m_w_proj_a, s5_lam_re=m_s5_lam_re, s5_lam_im=m_s5_lam_im, s5_log_dt=m_s5_log_dt, s5_b_re=m_s5_b_re, s5_b_im=m_s5_b_im, s5_c_re=m_s5_c_re, s5_c_im=m_s5_c_im, s5_d=m_s5_d, w_s5_glu=m_w_s5_glu, w_out=m_w_out, norm_ffn_w=m_norm_ffn_w, w_up=m_w_up, conv_ffn_w=m_conv_ffn_w, conv_ffn_b=m_conv_ffn_b, w_down=m_w_down, norm_final_w=m_norm_final_w)
    v = dict(norm_mix_w=v_norm_mix_w, w_in=v_w_in, conv_a_w=v_conv_a_w, conv_a_b=v_conv_a_b, dt_bias=v_dt_bias, a_log=v_a_log, d_a=v_d_a, norm_a_w=v_norm_a_w, w_proj_a=v_w_proj_a, s5_lam_re=v_s5_lam_re, s5_lam_im=v_s5_lam_im, s5_log_dt=v_s5_log_dt, s5_b_re=v_s5_b_re, s5_b_im=v_s5_b_im, s5_c_re=v_s5_c_re, s5_c_im=v_s5_c_im, s5_d=v_s5_d, w_s5_glu=v_w_s5_glu, w_out=v_w_out, norm_ffn_w=v_norm_ffn_w, w_up=v_w_up, conv_ffn_w=v_conv_ffn_w, conv_ffn_b=v_conv_ffn_b, w_down=v_w_down, norm_final_w=v_norm_final_w)
    xi, yi, ci = _place()
    chip = 2 * xi + yi

    cidx = jnp.reshape(ci, (1,)).astype(jnp.int32)
    sidx = jnp.reshape(chip, (1,)).astype(jnp.int32)

    tw = lambda a: jnp.transpose(a[0])[None]
    w["w_in"], m["w_in"], v["w_in"] = tw(w_in), tw(m_w_in), tw(v_w_in)
    shards = [w[k][0].astype(BF16) for k in BIG]
    shards[0] = jnp.pad(shards[0], ((0, WPAD - WSH), (0, 0)))
    i_sems, i_srcs, i_lands, i_tok = _ici_start("gather", shards[:1], cidx, "gather_in_start")
    hn1 = _rms_fwd(x[0], norm_mix_w, "rms_mix", after=i_tok)

    def late_weights(after):
        srcs, got = _ici_wait("gather", g_sems, g_srcs, g_lands, after, "gather_rest_wait")
        full = _pass_halves(list(got), list(srcs), "pass_halves_rest")
        return {"w_proj_a": full[0].reshape(DI, D), "w_s5_glu": full[1], "w_out": full[2].reshape(D, D),
                "w_up": full[3], "w_down": full[4].reshape(DFF, D)}

    pending = []

    def send_grads(names, g, tag):
        parts = [g[k] for k in names]
        sib = _swap_halves(parts, "swap_halves_" + tag)
        sums = [_chip_sum(parts[t], sib[t], cidx, "chip_sum_" + k) for t, k in enumerate(names)]
        sems, srcs, lands, tok = _ici_start("scatter", sums, cidx, "scatter_start_" + tag)
        pending.append((names, tag, sems, srcs, lands))
        return tok
    conv_blocks = []
    for k, (taps, cols) in CONV_FULL.items():
        shard = jnp.where(ci == 0, w[k][0], 0.0)
        conv_blocks.append(lax.dynamic_update_slice_in_dim(jnp.zeros((taps, cols), F32), shard, chip * (cols // 4), 1))
    conv_full = _unpack(_allsum_small(_pack(conv_blocks), "sum_conv_w"), [CONV_FULL[k] for k in CONV_FULL])

    i_srcs, i_got = _ici_wait("gather", i_sems, i_srcs, i_lands, conv_full[0], "gather_in_wait")
    w_sh = _pass_halves(list(i_got), list(i_srcs), "pass_halves_in")[0].reshape(4 * WPAD, D)
    w_dt = jnp.pad(w_sh[DT_SHARD_ROW:DT_SHARD_ROW + NG * HPG].reshape(NG, HPG, D), ((0, 0), (0, 8 - HPG), (0, 0)))
    w_full = lax.dynamic_update_slice(_move_rows(w_sh, RUNS_TO_MAIN, NFULL, MT, MT, "rows_to_main"),
                                      w_dt.reshape(8 * NG, D), (NMAIN, 0))
    g_sems, g_srcs, g_lands, token = _ici_start("gather", shards[1:], w_full, "gather_rest_start")
    p = {
        "w_full": w_full,
        "conv_a_w": conv_full[0], "conv_ffn_w": conv_full[1],
        "conv_a_b": conv_a_b, "conv_ffn_b": conv_ffn_b,
        "norm_mix_w": norm_mix_w, "norm_a_w": norm_a_w, "norm_ffn_w": norm_ffn_w,
        "norm_final_w": norm_final_w.reshape(1, D),
        "dt_bias": dt_bias, "a_log": a_log, "d_a": d_a, "s5_d": s5_d,
        "s5_lam_re": s5_lam_re[0], "s5_lam_im": s5_lam_im[0], "s5_log_dt": s5_log_dt[0],
        "s5_b_re": s5_b_re[0], "s5_b_im": s5_b_im[0], "s5_c_re": s5_c_re[0], "s5_c_im": s5_c_im[0],
    }
    loss_blk, gx, g = _local_step(x[0], loss_target[0], hn1, p, token, late_weights, send_grads)

    after, halves = gx, {}
    for names, tag, sems, srcs, lands in pending:
        srcs, got = _ici_wait("scatter", sems, srcs, lands, after, "scatter_wait_" + tag)
        for t, k in enumerate(names):
            halves[k] = _shard_sum(srcs[t], got[t], sidx, "shard_sum_" + k)
        after = halves[names[0]]
    g_mine = [halves[k] for k in BIG]
    g_other = _swap_whole(g_mine)

    small_shapes = [CONV_FULL.get(k, w[k].shape[1:] if k != "norm_final_w" else w[k].shape) for k in SMALL]
    small = _allsum_small(_pack([g[k] for k in SMALL] + [loss_blk[0:1, 0:1]]), "sum_small_grads")
    small_grads = dict(zip(SMALL + ["loss"], _unpack(small, small_shapes + [(1,)])))
    for k, (taps, cols) in CONV_FULL.items():
        small_grads[k] = lax.dynamic_slice_in_dim(small_grads[k], chip * (cols // 4), cols // 4, axis=1)
    loss = small_grads.pop("loss").reshape(())

    grads, delta, new_m, new_v = {}, {}, {}, {}
    for t, k in enumerate(BIG):
        outs = _adamw_halves(w[k], g_mine[t], g_other[t], m[k], v[k], cidx, "adamw_" + k)
        grads[k], delta[k], new_m[k], new_v[k] = [tw(o) for o in outs] if k == "w_in" else outs
    for k in SMALL:
        grads[k] = small_grads[k].reshape(w[k].shape)
    pk = lambda t: _pack([t[k] for k in SMALL])
    d_, m_, v_ = _adamw(pk(w), pk(grads), pk(m), pk(v), "adamw_small")
    shapes = [w[k].shape for k in SMALL]
    for k, dd, mm, vv in zip(SMALL, _unpack(d_, shapes), _unpack(m_, shapes), _unpack(v_, shapes)):
        delta[k], new_m[k], new_v[k] = dd, mm, vv
    return (loss, gx[None], *[grads[k] for k in ORDER], *[delta[k] for k in ORDER],
            *[new_m[k] for k in ORDER], *[new_v[k] for k in ORDER])
```

```python
import functools
import math

import jax
import jax.numpy as jnp
from jax import lax
from jax.experimental import pallas as pl
from jax.experimental.pallas import tpu as pltpu

F32 = jnp.float32
BF16 = jnp.bfloat16
HI = lax.Precision.HIGHEST
MESH = pl.DeviceIdType.MESH
ANY = pl.BlockSpec(memory_space=pl.ANY)

D = 2048
DI = 3072
HD = 64
NG = 8
HPG = 6
GW = HPG * HD
NS = 128
KA = 4
Q = 256
CONVD = DI + 2 * NG * NS
DS5 = 1024
NCH = 4096
DFF = 5632
KF = 3
EPS = 1e-6
EIG_MAX = -1e-4
NMAIN = 13312
OFF_XBC, OFF_U, OFF_GA, OFF_GB = 3072, 8192, 9216, 11264
WSH = 3340
WPAD = 3360
IN_SPLIT = [DI, DI + CONVD, DI + CONVD + NG * HPG]
NFULL = NMAIN + 128
MT = 336


def _w_in_runs():
    runs = []
    for k in range(4):
        for o_lo, o_hi, m_lo in ((0, IN_SPLIT[1], 0), (IN_SPLIT[2], 4 * WSH, IN_SPLIT[1])):
            lo, hi = max(o_lo, WSH * k), min(o_hi, WSH * (k + 1))
            if lo < hi:
                runs.append((m_lo + lo - o_lo, m_lo + hi - o_lo, WPAD * k + lo - WSH * k))
    return runs


RUNS_TO_MAIN = _w_in_runs()
RUNS_TO_SHARDS = [(s_lo, s_lo + m_hi - m_lo, m_lo) for m_lo, m_hi, s_lo in RUNS_TO_MAIN]
DT_SHARD_ROW = WPAD * (IN_SPLIT[1] // WSH) + IN_SPLIT[1] % WSH
assert IN_SPLIT[1] // WSH == (IN_SPLIT[2] - 1) // WSH
VMEM_LIMIT = 56 * 1024 * 1024

LR, B1, B2, AEPS, WD, STEP = 0.001, 0.9, 0.999, 1e-08, 0.01, 10


def _cp(*sem):
    return pltpu.CompilerParams(dimension_semantics=sem, vmem_limit_bytes=VMEM_LIMIT)


def _sig(x):
    return jax.nn.sigmoid(x)


def _silu(x):
    return x * _sig(x)


def _dsilu(x):
    s = _sig(x)
    return s * (1.0 + x * (1.0 - s))


def _softplus(x):
    return jnp.maximum(x, 0.0) + jnp.log(1.0 + jnp.exp(-jnp.abs(x)))


_GC = math.sqrt(2.0 / math.pi)


def _gelu(x):
    return 0.5 * x * (1.0 + jnp.tanh(_GC * (x + 0.044715 * x * x * x)))


def _dgelu(x):
    t = jnp.tanh(_GC * (x + 0.044715 * x * x * x))
    return 0.5 * (1.0 + t) + 0.5 * x * (1.0 - t * t) * _GC * (1.0 + 3.0 * 0.044715 * x * x)


def _dot(a, b, dims=((1,), (0,)), prec=None):
    return lax.dot_general(a, b, (dims, ((), ())), precision=prec, preferred_element_type=F32)


NT = ((1,), (1,))
TN = ((0,), (0,))


def _pick(n, t):
    for unit in (128, 8):
        for cand in range(min(n, t) // unit * unit, 0, -unit):
            if n % cand == 0:
                return cand
    return n


def _matmul(a, b, mode, name, out_dtype=F32, tm=1024, tn=1024, tk=2048, residual=None, b_stacked=False,
            out_stacked=False, after=None):
    if b_stacked:
        _, brows, bn = b.shape
        bshape = (brows, 4 * bn)
    else:
        bshape = b.shape
    if mode == "nn":
        (m, k), (k2, n) = a.shape, bshape
    elif mode == "nt":
        (m, k), (n, k2) = a.shape, bshape
    else:
        (k, m), (k2, n) = a.shape, bshape
    assert k == k2
    tm = _pick(m, tm)
    tn = _pick(n // 4 if (out_stacked or (b_stacked and mode != "nt")) else n, tn)
    tk = _pick(k // 4 if (b_stacked and mode == "nt") else k, tk)
    nk = k // tk
    dims = {"nn": ((1,), (0,)), "nt": NT, "tn": TN}[mode]
    has_res = residual is not None
    n_in = 2 + has_res + (after is not None)

    def body(*refs):
        a_ref, b_ref = refs[0], refs[1]
        r_ref = refs[2] if has_res else None
        o_ref = refs[n_in]
        p = _dot(a_ref[...], b_ref[...], dims)

        def finish(r):
            if has_res:
                r = r + r_ref[...]
            o_ref[...] = r.astype(out_dtype)

        if nk == 1:
            finish(p)
        else:
            acc = refs[-1]
            kk = pl.program_id(2)

            @pl.when(kk == 0)
            def _():
                acc[...] = p

            @pl.when(kk > 0)
            def _():
                acc[...] += p

            @pl.when(kk == nk - 1)
            def _():
                finish(acc[...])

    if mode == "tn":
        a_spec = pl.BlockSpec((tk, tm), lambda i, j, kk: (kk, i))
    else:
        a_spec = pl.BlockSpec((tm, tk), lambda i, j, kk: (i, kk))
    if mode == "nt":
        if b_stacked:
            per = bn // tk
            b_spec = pl.BlockSpec((None, tn, tk), lambda i, j, kk: (kk // per, j, kk % per))
        else:
            b_spec = pl.BlockSpec((tn, tk), lambda i, j, kk: (j, kk))
    elif b_stacked:
        per = bn // tn
        b_spec = pl.BlockSpec((None, tk, tn), lambda i, j, kk: (j // per, kk, j % per))
    else:
        b_spec = pl.BlockSpec((tk, tn), lambda i, j, kk: (kk, j))
    o_spec = pl.BlockSpec((tm, tn), lambda i, j, kk: (i, j))
    if out_stacked:
        per_o = n // 4 // tn
        out_spec = pl.BlockSpec((None, tm, tn), lambda i, j, kk: (j // per_o, i, j % per_o))
        out_shape = jax.ShapeDtypeStruct((4, m, n // 4), out_dtype)
    else:
        out_spec, out_shape = o_spec, jax.ShapeDtypeStruct((m, n), out_dtype)
    in_specs, args = [a_spec, b_spec], [a, b]
    if has_res:
        in_specs.append(o_spec)
        args.append(residual)
    if after is not None:
        in_specs.append(ANY)
        args.append(after)
    return pl.pallas_call(
        body, name=name, grid=(m // tm, n // tn, nk),
        in_specs=in_specs, out_specs=out_spec, out_shape=out_shape,
        scratch_shapes=[pltpu.VMEM((tm, tn), F32)] if nk > 1 else [],
        compiler_params=_cp("parallel", "parallel", "arbitrary"),
    )(*args)


def _move_rows(src, runs, rows_out, t_out, t_in, name):
    rows_in, cols = src.shape
    nb_out, nb_in = rows_out // t_out, rows_in // t_in
    assert rows_out % t_out == 0 and rows_in % t_in == 0 and t_in >= t_out
    blk, off, lo, hi = ([[0] * nb_out for _ in range(2)] for _ in range(4))
    for i in range(nb_out):
        hits = [r for r in runs if r[0] < (i + 1) * t_out and r[1] > i * t_out]
        assert len(hits) <= 2
        for s, (o_lo, o_hi, s_lo) in enumerate(hits):
            lo[s][i] = max(o_lo, i * t_out) - i * t_out
            hi[s][i] = min(o_hi, (i + 1) * t_out) - i * t_out
            first = i * t_out + lo[s][i] - o_lo + s_lo
            blk[s][i] = min(first // t_in, nb_in - 1)
            off[s][i] = first - lo[s][i] - blk[s][i] * t_in
    table = jnp.asarray([blk[0], off[0], lo[0], hi[0], blk[1], off[1], lo[1], hi[1]], jnp.int32)

    def body(tab, a0, a1, b0, b1, o_ref):
        i = pl.program_id(0)
        o_ref[...] = jnp.zeros_like(o_ref)
        r = lax.broadcasted_iota(jnp.int32, (t_out, t_in), 0)
        k = lax.broadcasted_iota(jnp.int32, (t_out, t_in), 1)
        for s, (first, second) in enumerate(((a0, a1), (b0, b1))):
            off_s, lo_s, hi_s = tab[4 * s + 1, i], tab[4 * s + 2, i], tab[4 * s + 3, i]
            live = (r >= lo_s) & (r < hi_s)

            @pl.when(hi_s > lo_s)
            def _():
                sel = (live & (k == r + off_s)).astype(BF16)
                o_ref[...] += _dot(sel, first[...]).astype(o_ref.dtype)

            @pl.when((hi_s > lo_s) & (off_s + hi_s > t_in))
            def _():
                sel = (live & (k == r + off_s - t_in)).astype(BF16)
                o_ref[...] += _dot(sel, second[...]).astype(o_ref.dtype)

    def in_spec(s, nxt):
        return pl.BlockSpec((t_in, cols), lambda i, tab: (jnp.minimum(tab[4 * s, i] + nxt, nb_in - 1), 0))

    return pl.pallas_call(
        body, name=name,
        grid_spec=pltpu.PrefetchScalarGridSpec(
            num_scalar_prefetch=1, grid=(nb_out,),
            in_specs=[in_spec(0, 0), in_spec(0, 1), in_spec(1, 0), in_spec(1, 1)],
            out_specs=pl.BlockSpec((t_out, cols), lambda i, tab: (i, 0))),
        out_shape=jax.ShapeDtypeStruct((rows_out, cols), src.dtype), compiler_params=_cp("parallel"),
    )(table, src, src, src, src)


TL = 256


def _rms_fwd(x, w, name, after=None):
    n, d = x.shape

    def body(x_ref, w_ref, *rest):
        xv = x_ref[...]
        r = lax.rsqrt(jnp.mean(xv * xv, axis=-1, keepdims=True) + EPS)
        rest[-1][...] = (xv * r * w_ref[...]).astype(BF16)

    extra = [] if after is None else [after]
    return pl.pallas_call(
        body, name=name, grid=(n // TL,),
        in_specs=[pl.BlockSpec((TL, d), lambda i: (i, 0)), pl.BlockSpec((1, d), lambda i: (0, 0))] + [ANY] * len(extra),
        out_specs=pl.BlockSpec((TL, d), lambda i: (i, 0)),
        out_shape=jax.ShapeDtypeStruct((n, d), BF16), compiler_params=_cp("parallel"),
    )(x, w, *extra)


def _rms_bwd(dhn, x, w, dres, name, after=None):
    n, d = x.shape

    def body(g_ref, x_ref, w_ref, r_ref, *rest):
        dx_ref, dxb_ref, gw_ref = rest[-3:]
        xv = x_ref[...]
        r = lax.rsqrt(jnp.mean(xv * xv, axis=-1, keepdims=True) + EPS)
        xh = xv * r
        gv = g_ref[...]
        g = gv * w_ref[...]
        dx = r_ref[...] + r * (g - xh * jnp.mean(g * xh, axis=-1, keepdims=True))
        dx_ref[...] = dx
        dxb_ref[...] = dx.astype(BF16)

        @pl.when(pl.program_id(0) == 0)
        def _():
            gw_ref[...] = jnp.zeros_like(gw_ref)

        gw_ref[...] += jnp.sum(gv * xh, axis=0, keepdims=True)

    extra = [] if after is None else [after]
    row = pl.BlockSpec((TL, d), lambda i: (i, 0))
    vec = pl.BlockSpec((1, d), lambda i: (0, 0))
    return pl.pallas_call(
        body, name=name, grid=(n // TL,),
        in_specs=[row, row, vec, row] + [ANY] * len(extra), out_specs=[row, row, vec],
        out_shape=[jax.ShapeDtypeStruct((n, d), F32), jax.ShapeDtypeStruct((n, d), BF16),
                   jax.ShapeDtypeStruct((1, d), F32)],
        compiler_params=_cp("arbitrary"),
    )(dhn, x, w, dres, *extra)


def _final(h2, w, target):
    n, d = h2.shape

    def body(x_ref, w_ref, t_ref, dx_ref, dxb_ref, gw_ref, loss_ref):
        xv = x_ref[...]
        r = lax.rsqrt(jnp.mean(xv * xv, axis=-1, keepdims=True) + EPS)
        xh = xv * r
        diff = xh * w_ref[...] - t_ref[...]
        gv = diff * (1.0 / d)
        g = gv * w_ref[...]
        dx = r * (g - xh * jnp.mean(g * xh, axis=-1, keepdims=True))
        dx_ref[...] = dx
        dxb_ref[...] = dx.astype(BF16)

        @pl.when(pl.program_id(0) == 0)
        def _():
            gw_ref[...] = jnp.zeros_like(gw_ref)
            loss_ref[...] = jnp.zeros_like(loss_ref)

        gw_ref[...] += jnp.sum(gv * xh, axis=0, keepdims=True)
        part = 0.5 * jnp.sum(jnp.mean(diff * diff, axis=-1, keepdims=True), axis=0, keepdims=True)
        loss_ref[...] += jnp.broadcast_to(part, loss_ref.shape)

    row = pl.BlockSpec((TL, d), lambda i: (i, 0))
    vec = pl.BlockSpec((1, d), lambda i: (0, 0))
    return pl.pallas_call(
        body, name="final_loss", grid=(n // TL,),
        in_specs=[row, vec, row], out_specs=[row, row, vec, pl.BlockSpec((8, 128), lambda i: (0, 0))],
        out_shape=[jax.ShapeDtypeStruct((n, d), F32), jax.ShapeDtypeStruct((n, d), BF16),
                   jax.ShapeDtypeStruct((1, d), F32), jax.ShapeDtypeStruct((8, 128), F32)],
        compiler_params=_cp("arbitrary"),
    )(h2, w, target)


CT = 512
CL = 512


def _lagged(xf, taps, rows):
    return [xf[8:8 + rows]] + [pltpu.roll(xf, s, 0)[8:8 + rows] for s in range(1, taps)]


def _shift_up(x, u, n):
    if u == 0:
        return x[0:n]
    return pltpu.roll(x, x.shape[0] - u, 0)[0:n]


def _conv_pre(lagged, w_ref, b_ref, taps):
    pre = b_ref[...]
    for k in range(taps):
        pre = pre + w_ref[k:k + 1, :] * lagged[taps - 1 - k]
    return pre


def _conv_back(e, w_ref, taps):
    dx = w_ref[taps - 1:taps, :] * e[0:CL]
    for k in range(taps - 1):
        dx = dx + w_ref[k:k + 1, :] * _shift_up(e, taps - 1 - k, CL)
    return dx


def _halo_specs(n, col_of):
    per = CL // 8
    cur = pl.BlockSpec((CL, CT), lambda j, i, *_: (i, col_of(j)))
    prev = pl.BlockSpec((8, CT), lambda j, i, *_: (jnp.maximum(i * per - 1, 0), col_of(j)))
    nxt = pl.BlockSpec((8, CT), lambda j, i, *_: (jnp.minimum((i + 1) * per, n // 8 - 1), col_of(j)))
    return prev, cur, nxt


def _conv_a_fwd(proj, w, b):
    n = proj.shape[0]
    off = OFF_XBC // CT

    def body(p_ref, x_ref, w_ref, b_ref, o_ref):
        p8 = jnp.where(pl.program_id(1) > 0, p_ref[...], 0.0)
        xf = jnp.concatenate([p8, x_ref[...]], axis=0)
        o_ref[...] = _silu(_conv_pre(_lagged(xf, KA, CL), w_ref, b_ref, KA))

    prev, cur, _ = _halo_specs(n, lambda j: j + off)
    return pl.pallas_call(
        body, name="conv_a_fwd", grid=(CONVD // CT, n // CL),
        in_specs=[prev, cur, pl.BlockSpec((KA, CT), lambda j, i: (0, j)), pl.BlockSpec((1, CT), lambda j, i: (0, j))],
        out_specs=pl.BlockSpec((CL, CT), lambda j, i: (i, j)),
        out_shape=jax.ShapeDtypeStruct((n, CONVD), F32), compiler_params=_cp("parallel", "parallel"),
    )(proj, proj, w, b)


def _conv_a_bwd(proj, dout, w, b, col0, name):
    n, width = dout.shape
    off = (OFF_XBC + col0) // CT
    woff = col0 // CT
    nl = n // CL

    def body(p_ref, x_ref, n_ref, d_ref, dn_ref, w_ref, b_ref, dx_ref, dw_ref, db_ref):
        i = pl.program_id(1)
        xf = jnp.concatenate([jnp.where(i > 0, p_ref[...], 0.0), x_ref[...], n_ref[...]], axis=0)
        lag = _lagged(xf, KA, CL + 8)
        de = jnp.concatenate([d_ref[...], jnp.where(i < nl - 1, dn_ref[...], 0.0)], axis=0)
        se = de * _dsilu(_conv_pre(lag, w_ref, b_ref, KA))
        dx_ref[...] = _conv_back(se, w_ref, KA).astype(BF16)

        @pl.when(i == 0)
        def _():
            dw_ref[...] = jnp.zeros_like(dw_ref)
            db_ref[...] = jnp.zeros_like(db_ref)

        sc = se[0:CL]
        for k in range(KA):
            dw_ref[k:k + 1, :] += jnp.sum(sc * lag[KA - 1 - k][0:CL], axis=0, keepdims=True)
        db_ref[...] += jnp.sum(sc, axis=0, keepdims=True)

    prev, cur, nxt = _halo_specs(n, lambda j: j + off)
    _, dcur, dnxt = _halo_specs(n, lambda j: j)
    wspec = pl.BlockSpec((KA, CT), lambda j, i: (0, j + woff))
    bspec = pl.BlockSpec((1, CT), lambda j, i: (0, j + woff))
    return pl.pallas_call(
        body, name=name, grid=(width // CT, nl),
        in_specs=[prev, cur, nxt, dcur, dnxt, wspec, bspec],
        out_specs=[pl.BlockSpec((CL, CT), lambda j, i: (i, j)), pl.BlockSpec((KA, CT), lambda j, i: (0, j)),
                   pl.BlockSpec((1, CT), lambda j, i: (0, j))],
        out_shape=[jax.ShapeDtypeStruct((n, width), BF16), jax.ShapeDtypeStruct((KA, width), F32),
                   jax.ShapeDtypeStruct((1, width), F32)],
        compiler_params=_cp("parallel", "arbitrary"),
    )(proj, proj, proj, dout, dout, w, b)


def _conv_ffn_fwd(up, w, b):
    n = up.shape[0]
    nb = DFF // CT

    def body(pg_ref, g_ref, pv_ref, v_ref, wg_ref, bg_ref, wv_ref, bv_ref, o_ref):
        inner = pl.program_id(1) > 0
        gf = jnp.concatenate([jnp.where(inner, pg_ref[...], 0.0), g_ref[...]], axis=0)
        vf = jnp.concatenate([jnp.where(inner, pv_ref[...], 0.0), v_ref[...]], axis=0)
        gc = _conv_pre(_lagged(gf, KF, CL), wg_ref, bg_ref, KF)
        vc = _conv_pre(_lagged(vf, KF, CL), wv_ref, bv_ref, KF)
        o_ref[...] = (_silu(gc) * vc).astype(BF16)

    gp, gcur, _ = _halo_specs(n, lambda j: j)
    vp, vcur, _ = _halo_specs(n, lambda j: j + nb)
    return pl.pallas_call(
        body, name="conv_ffn_fwd", grid=(nb, n // CL),
        in_specs=[gp, gcur, vp, vcur,
                  pl.BlockSpec((KF, CT), lambda j, i: (0, j)), pl.BlockSpec((1, CT), lambda j, i: (0, j)),
                  pl.BlockSpec((KF, CT), lambda j, i: (0, j + nb)), pl.BlockSpec((1, CT), lambda j, i: (0, j + nb))],
        out_specs=pl.BlockSpec((CL, CT), lambda j, i: (i, j)),
        out_shape=jax.ShapeDtypeStruct((n, DFF), BF16), compiler_params=_cp("parallel", "parallel"),
    )(up, up, up, up, w, b, w, b)


def _conv_ffn_bwd(up, dact, w, b):
    n = up.shape[0]
    nb = DFF // CT
    nl = n // CL

    def body(pg_ref, g_ref, ng_ref, pv_ref, v_ref, nv_ref, d_ref, dn_ref, wg_ref, bg_ref, wv_ref, bv_ref,
             dxg_ref, dxv_ref, dwg_ref, dwv_ref, dbg_ref, dbv_ref):
        i = pl.program_id(1)
        gf = jnp.concatenate([jnp.where(i > 0, pg_ref[...], 0.0), g_ref[...], ng_ref[...]], axis=0)
        vf = jnp.concatenate([jnp.where(i > 0, pv_ref[...], 0.0), v_ref[...], nv_ref[...]], axis=0)
        glag, vlag = _lagged(gf, KF, CL + 8), _lagged(vf, KF, CL + 8)
        de = jnp.concatenate([d_ref[...], jnp.where(i < nl - 1, dn_ref[...], 0.0)], axis=0).astype(F32)
        gc = _conv_pre(glag, wg_ref, bg_ref, KF)
        vc = _conv_pre(vlag, wv_ref, bv_ref, KF)
        sg = _sig(gc)
        dgc = de * vc * (sg * (1.0 + gc * (1.0 - sg)))
        dvc = de * (gc * sg)
        dxg_ref[...] = _conv_back(dgc, wg_ref, KF).astype(BF16)
        dxv_ref[...] = _conv_back(dvc, wv_ref, KF).astype(BF16)

        @pl.when(i == 0)
        def _():
            for r in (dwg_ref, dwv_ref, dbg_ref, dbv_ref):
                r[...] = jnp.zeros_like(r)

        for e, lag, dw_ref, db_ref in ((dgc, glag, dwg_ref, dbg_ref), (dvc, vlag, dwv_ref, dbv_ref)):
            ec = e[0:CL]
            for k in range(KF):
                dw_ref[k:k + 1, :] += jnp.sum(ec * lag[KF - 1 - k][0:CL], axis=0, keepdims=True)
            db_ref[...] += jnp.sum(ec, axis=0, keepdims=True)

    gp, gcur, gnx = _halo_specs(n, lambda j: j)
    vp, vcur, vnx = _halo_specs(n, lambda j: j + nb)
    wcol = lambda o: (pl.BlockSpec((KF, CT), lambda j, i: (0, j + o)), pl.BlockSpec((1, CT), lambda j, i: (0, j + o)))
    wg, bg = wcol(0)
    wv, bv = wcol(nb)
    dxs = pl.BlockSpec((CL, CT), lambda j, i: (i, j))
    outs = pl.pallas_call(
        body, name="conv_ffn_bwd", grid=(nb, nl),
        in_specs=[gp, gcur, gnx, vp, vcur, vnx, gcur, gnx, wg, bg, wv, bv],
        out_specs=[dxs, dxs, wg, wg, bg, bg],
        out_shape=[jax.ShapeDtypeStruct((n, DFF), BF16)] * 2 + [jax.ShapeDtypeStruct((KF, DFF), F32)] * 2
        + [jax.ShapeDtypeStruct((1, DFF), F32)] * 2,
        compiler_params=_cp("parallel", "arbitrary"),
    )(up, up, up, up, up, up, dact, dact, w, b, w, b)
    return [jnp.concatenate(outs[k:k + 2], axis=1) for k in (0, 2, 4)]


HL = 128


def _split3(x):
    hi = x.astype(BF16)
    r1 = x - hi.astype(F32)
    mid = r1.astype(BF16)
    return hi, mid, (r1 - mid.astype(F32)).astype(BF16)


def _dot3(x, m):
    hi, mid, lo = _split3(x)
    return _dot(hi, m) + _dot(mid, m) + _dot(lo, m)


def _tri():
    row = lax.broadcasted_iota(jnp.int32, (Q, Q), 0)
    col = lax.broadcasted_iota(jnp.int32, (Q, Q), 1)
    return row >= col, row <= col


def _ssd_prep(dtraw, hp, emat):
    n = dtraw.shape[0]

    def body(d_ref, hp_ref, e_ref, dt_ref, s_ref, st_ref, dte_ref, se_ref):
        lower, upper = _tri()
        dt = _softplus(d_ref[...] + hp_ref[0:1, :])
        da = dt * (-jnp.exp(hp_ref[1:2, :]))
        s = _dot(lower.astype(F32), da, prec=HI)
        dt_ref[...] = dt
        s_ref[...] = s
        st_ref[...] = _dot(da, upper.astype(F32), TN, prec=HI)
        e = e_ref[...]
        dte_ref[...] = _dot3(dt, e)
        se_ref[...] = _dot3(s, e)

    row = pl.BlockSpec((Q, HL), lambda c: (c, 0))
    wide = pl.BlockSpec((Q, DI), lambda c: (c, 0))
    return pl.pallas_call(
        body, name="ssd_prep", grid=(n // Q,),
        in_specs=[pl.BlockSpec((Q, HL), lambda c: (c, NMAIN // HL)), pl.BlockSpec((8, HL), lambda c: (0, 0)),
                  pl.BlockSpec((HL, DI), lambda c: (0, 0))],
        out_specs=[row, row, pl.BlockSpec((HL, Q), lambda c: (0, c)), wide, wide],
        out_shape=[jax.ShapeDtypeStruct((n, HL), F32)] * 2 + [jax.ShapeDtypeStruct((HL, n), F32)]
        + [jax.ShapeDtypeStruct((n, DI), F32)] * 2,
        compiler_params=_cp("parallel"),
    )(dtraw, hp, emat)


def _ssd_post(ds_e, ddt_e, tsum, dsh, dtraw, dt, hp, emat_t):
    n = dtraw.shape[0]

    def body(dse_ref, dde_ref, ts_ref, dsh_ref, d_ref, dt_ref, hp_ref, et_ref, draw_ref, ps_ref):
        _, upper = _tri()
        et = et_ref[...]
        a = -jnp.exp(hp_ref[1:2, :])
        rows = lax.broadcasted_iota(jnp.int32, (Q, HL), 0)
        ds_t = _dot3(jnp.broadcast_to(ts_ref[...], (8, DI)), et)[0:1, :]
        ds = _dot3(dse_ref[...], et) + dsh_ref[...] + jnp.where(rows == Q - 1, ds_t, 0.0)
        d_a = _dot(upper.astype(F32), ds, prec=HI)
        draw = (_dot3(dde_ref[...], et) + d_a * a) * _sig(d_ref[...] + hp_ref[0:1, :])
        draw_ref[...] = draw

        @pl.when(pl.program_id(0) == 0)
        def _():
            ps_ref[...] = jnp.zeros_like(ps_ref)

        ps_ref[0:1, :] += jnp.sum(draw, axis=0, keepdims=True)
        ps_ref[1:2, :] += jnp.sum(d_a * dt_ref[...], axis=0, keepdims=True) * a

    row = pl.BlockSpec((Q, HL), lambda c: (c, 0))
    wide = pl.BlockSpec((Q, DI), lambda c: (c, 0))
    small = pl.BlockSpec((8, HL), lambda c: (0, 0))
    return pl.pallas_call(
        body, name="ssd_post", grid=(n // Q,),
        in_specs=[wide, wide, pl.BlockSpec((None, 1, DI), lambda c: (c, 0, 0)), row,
                  pl.BlockSpec((Q, HL), lambda c: (c, NMAIN // HL)), row, small,
                  pl.BlockSpec((DI, HL), lambda c: (0, 0))],
        out_specs=[row, small],
        out_shape=[jax.ShapeDtypeStruct((n, HL), F32), jax.ShapeDtypeStruct((8, HL), F32)],
        compiler_params=_cp("arbitrary"),
    )(ds_e, ddt_e, tsum, dsh, dtraw, dt, hp, emat_t)


def _ssd_specs(nc, rev):
    cc = (lambda c: nc - 1 - c) if rev else (lambda c: c)
    return [
        pl.BlockSpec((Q, GW), lambda g, c: (cc(c), g)),
        pl.BlockSpec((Q, NS), lambda g, c: (cc(c), DI // NS + g)),
        pl.BlockSpec((Q, NS), lambda g, c: (cc(c), (DI + NG * NS) // NS + g)),
        pl.BlockSpec((None, Q, 8), lambda g, c: (g, cc(c), 0)),
        pl.BlockSpec((8, Q), lambda g, c: (g, cc(c))),
        pl.BlockSpec((Q, GW), lambda g, c: (cc(c), g)),
        pl.BlockSpec((Q, GW), lambda g, c: (cc(c), g)),
        pl.BlockSpec((1, GW), lambda g, c: (0, g)),
    ]


def _ssd_fwd(xbc, s8, s_t, dt_e, s_e, dexp):
    n = xbc.shape[0]
    nc = n // Q

    def body(xs_ref, b_ref, c_ref, sc_ref, sr_ref, dte_ref, se_ref, dexp_ref, y_ref, sp_ref, st):
        @pl.when(pl.program_id(1) == 0)
        def _():
            st[...] = jnp.zeros_like(st)

        lower, _ = _tri()
        s_c, s_r, dt_e, s_e = sc_ref[...], sr_ref[...], dte_ref[...], se_ref[...]
        xs = xs_ref[...]
        x = xs * dt_e
        xb = x.astype(BF16)
        bb, cb = b_ref[...].astype(BF16), c_ref[...].astype(BF16)
        cbm = _dot(cb, bb, NT)
        st_e = s_e[Q - 1:Q, :]
        sprev = st[...]
        sp_ref[...] = sprev
        yoff = _dot(cb, sprev.astype(BF16)) * jnp.exp(s_e) + dexp_ref[...] * xs
        for h in range(HPG):
            sl = slice(h * HD, (h + 1) * HD)
            lm = jnp.where(lower, jnp.exp(jnp.minimum(s_c[:, h:h + 1] - s_r[h:h + 1, :], 0.0)), 0.0)
            y_ref[:, sl] = _dot((cbm * lm).astype(BF16), xb[:, sl]) + yoff[:, sl]
        w = (x * jnp.exp(st_e - s_e)).astype(BF16)
        st[...] = jnp.exp(st_e) * sprev + _dot(bb, w, TN)

    return pl.pallas_call(
        body, name="ssd_fwd", grid=(NG, nc), in_specs=_ssd_specs(nc, False),
        out_specs=[pl.BlockSpec((Q, GW), lambda g, c: (c, g)),
                   pl.BlockSpec((None, None, NS, GW), lambda g, c: (c, g, 0, 0))],
        out_shape=[jax.ShapeDtypeStruct((n, DI), F32), jax.ShapeDtypeStruct((nc, NG, NS, GW), F32)],
        scratch_shapes=[pltpu.VMEM((NS, GW), F32)],
        compiler_params=_cp("parallel", "arbitrary"),
    )(xbc, xbc, xbc, s8, s_t, dt_e, s_e, dexp)


def _ssd_bwd(xbc, s8, s_t, dt_e, s_e, dexp, sprev_all, dy):
    n = xbc.shape[0]
    nc = n // Q
    rc = lambda c: nc - 1 - c

    def body(xs_ref, b_ref, c_ref, sc_ref, sr_ref, dte_ref, se_ref, dexp_ref, sp_ref, dy_ref,
             dxs_ref, db_ref, dc_ref, dse_ref, dde_ref, ts_ref, dsh_ref, pd_ref, dst, dxbuf):
        @pl.when(pl.program_id(1) == 0)
        def _():
            dst[...] = jnp.zeros_like(dst)
            pd_ref[...] = jnp.zeros_like(pd_ref)

        lower, upper = _tri()
        s_c, s_r, dt_e, s_e = sc_ref[...], sr_ref[...], dte_ref[...], se_ref[...]
        xs = xs_ref[...]
        x = xs * dt_e
        xb = x.astype(BF16)
        bb, cb = b_ref[...].astype(BF16), c_ref[...].astype(BF16)
        cbm = _dot(cb, bb, NT)
        cbt = _dot(bb, cb, NT)
        st_e = s_e[Q - 1:Q, :]
        dec_out, dec_st, e_t = jnp.exp(s_e), jnp.exp(st_e - s_e), jnp.exp(st_e)
        dyv = dy_ref[...]
        dyb = dyv.astype(BF16)
        sprev = sp_ref[...]
        sb = sprev.astype(BF16)
        ds_in = dst[...]
        dsb = ds_in.astype(BF16)

        cs = _dot(cb, sb)
        dcs = (dyv * dec_out).astype(BF16)
        d_c = _dot(dcs, sb, NT)
        wf = x * dec_st
        d_w = _dot(bb, dsb)
        d_b = _dot(wf.astype(BF16), dsb, NT)
        tw = d_w * wf
        dse_ref[...] = dyv * cs * dec_out - tw
        ds_c = jnp.zeros((Q, 8), F32)
        dcb = jnp.zeros((Q, Q), F32)
        dcbt = jnp.zeros((Q, Q), F32)
        lane8 = lax.broadcasted_iota(jnp.int32, (1, 8), 1)
        for h in range(HPG):
            sl = slice(h * HD, (h + 1) * HD)
            sc_h, sr_h = s_c[:, h:h + 1], s_r[h:h + 1, :]
            lm = jnp.where(lower, jnp.exp(jnp.minimum(sc_h - sr_h, 0.0)), 0.0)
            lmt = jnp.where(upper, jnp.exp(jnp.minimum(sr_h - sc_h, 0.0)), 0.0)
            mt = cbt * lmt
            dm = _dot(dyb[:, sl], xb[:, sl], NT)
            dmt = _dot(xb[:, sl], dyb[:, sl], NT)
            dxbuf[:, sl] = _dot(mt.astype(BF16), dyb[:, sl])
            dml = dm * lm
            dmlt = dmt * lmt
            dcb = dcb + dml
            dcbt = dcbt + dmlt
            dsh = jnp.sum(dml * cbm, axis=1, keepdims=True) - jnp.sum(dmlt * cbt, axis=1, keepdims=True)
            ds_c = ds_c + dsh * (lane8 == h).astype(F32)
        d_c = d_c + _dot(dcb.astype(BF16), bb)
        d_b = d_b + _dot(dcbt.astype(BF16), cb)
        dx = d_w * dec_st + dxbuf[...]
        ts_ref[...] = jnp.sum(tw, axis=0, keepdims=True) + jnp.sum(ds_in * sprev, axis=0, keepdims=True) * e_t
        dsh_ref[...] = ds_c
        dde_ref[...] = dx * xs
        pd_ref[...] += jnp.sum(dyv * xs, axis=0, keepdims=True)
        dxs_ref[...] = dx * dt_e + dyv * dexp_ref[...]
        db_ref[...] = d_b
        dc_ref[...] = d_c
        dst[...] = e_t * ds_in + _dot(cb, dcs, TN)

    wide = pl.BlockSpec((Q, GW), lambda g, c: (rc(c), g))
    state = pl.BlockSpec((Q, NS), lambda g, c: (rc(c), g))
    in_specs = _ssd_specs(nc, True) + [pl.BlockSpec((None, None, NS, GW), lambda g, c: (rc(c), g, 0, 0)), wide]
    return pl.pallas_call(
        body, name="ssd_bwd", grid=(NG, nc), in_specs=in_specs,
        out_specs=[wide, state, state, wide, wide,
                   pl.BlockSpec((None, 1, GW), lambda g, c: (rc(c), 0, g)),
                   pl.BlockSpec((None, Q, 8), lambda g, c: (g, rc(c), 0)),
                   pl.BlockSpec((None, 1, GW), lambda g, c: (g, 0, 0))],
        out_shape=[jax.ShapeDtypeStruct((n, DI), F32), jax.ShapeDtypeStruct((n, NG * NS), F32),
                   jax.ShapeDtypeStruct((n, NG * NS), F32), jax.ShapeDtypeStruct((n, DI), F32),
                   jax.ShapeDtypeStruct((n, DI), F32), jax.ShapeDtypeStruct((nc, 1, DI), F32),
                   jax.ShapeDtypeStruct((NG, n, 8), F32), jax.ShapeDtypeStruct((NG, 1, GW), F32)],
        scratch_shapes=[pltpu.VMEM((NS, GW), F32), pltpu.VMEM((Q, GW), F32)],
        compiler_params=_cp("parallel", "arbitrary"),
    )(xbc, xbc, xbc, s8, s_t, dt_e, s_e, dexp, sprev_all, dy)


GL = 128


def _gnorm_fwd(y, proj, w):
    n = y.shape[0]

    def body(y_ref, z_ref, w_ref, o_ref):
        for g in range(NG):
            sl = slice(g * GW, (g + 1) * GW)
            yz = y_ref[:, sl] * _silu(z_ref[:, sl])
            r = lax.rsqrt(jnp.mean(yz * yz, axis=-1, keepdims=True) + EPS)
            o_ref[:, sl] = (yz * r * w_ref[:, sl]).astype(BF16)

    row = pl.BlockSpec((GL, DI), lambda i: (i, 0))
    return pl.pallas_call(
        body, name="gnorm_fwd", grid=(n // GL,),
        in_specs=[row, row, pl.BlockSpec((1, DI), lambda i: (0, 0))], out_specs=row,
        out_shape=jax.ShapeDtypeStruct((n, DI), BF16), compiler_params=_cp("parallel"),
    )(y, proj, w)


def _gnorm_bwd(dyn, y, proj, w, after):
    n = y.shape[0]

    def body(d_ref, y_ref, z_ref, w_ref, _, dy_ref, dz_ref, gw_ref):
        @pl.when(pl.program_id(0) == 0)
        def _():
            gw_ref[...] = jnp.zeros_like(gw_ref)

        for g in range(NG):
            sl = slice(g * GW, (g + 1) * GW)
            yv, zv, dv = y_ref[:, sl], z_ref[:, sl], d_ref[:, sl]
            sz = _silu(zv)
            yz = yv * sz
            r = lax.rsqrt(jnp.mean(yz * yz, axis=-1, keepdims=True) + EPS)
            yh = yz * r
            gg = dv * w_ref[:, sl]
            dyz = r * (gg - yh * jnp.mean(gg * yh, axis=-1, keepdims=True))
            gw_ref[:, sl] += jnp.sum(dv * yh, axis=0, keepdims=True)
            dy_ref[:, sl] = dyz * sz
            dz_ref[:, sl] = (dyz * yv * _dsilu(zv)).astype(BF16)

    row = pl.BlockSpec((GL, DI), lambda i: (i, 0))
    vec = pl.BlockSpec((1, DI), lambda i: (0, 0))
    return pl.pallas_call(
        body, name="gnorm_bwd", grid=(n // GL,),
        in_specs=[row, row, row, vec, ANY], out_specs=[row, row, vec],
        out_shape=[jax.ShapeDtypeStruct((n, DI), F32), jax.ShapeDtypeStruct((n, DI), BF16),
                   jax.ShapeDtypeStruct((1, DI), F32)],
        compiler_params=_cp("arbitrary"),
    )(dyn, y, proj, w, after)


SL = 512
SB = 8
SCB = NCH // SB


def _s5_in(proj, bre, bim, after=None):
    n = proj.shape[0]
    uoff = OFF_U // 128

    def body(u_ref, br_ref, bi_ref, *rest):
        or_ref, oi_ref = rest[-2:]
        u = u_ref[...].astype(BF16)
        or_ref[...] = _dot(u, br_ref[...])
        oi_ref[...] = _dot(u, bi_ref[...])

    extra = [] if after is None else [after]
    blk = pl.BlockSpec((None, 128, SCB), lambda i, j: (j, 0, 0))
    out = pl.BlockSpec((SL, SCB), lambda i, j: (i, j))
    return pl.pallas_call(
        body, name="s5_in", grid=(n // SL, SB),
        in_specs=[pl.BlockSpec((SL, 128), lambda i, j: (i, uoff + j)), blk, blk] + [ANY] * len(extra),
        out_specs=[out, out],
        out_shape=[jax.ShapeDtypeStruct((n, NCH), F32)] * 2, compiler_params=_cp("parallel", "parallel"),
    )(proj, bre, bim, *extra)


SC = 256


def _s5_scan(vre, vim, tab, reverse, name):
    n = vre.shape[0]
    nl = n // SL
    ng = SL // 8
    ti = (lambda i: nl - 1 - i) if reverse else (lambda i: i)

    def body(re_ref, im_ref, tab_ref, ore_ref, oim_ref, cre, cim):
        @pl.when(pl.program_id(1) == 0)
        def _():
            cre[...] = jnp.zeros_like(cre)
            cim[...] = jnp.zeros_like(cim)

        def step(j, carry):
            cr, ci = carry
            jj = (ng - 1 - j) if reverse else j
            rows = pl.ds(pl.multiple_of(jj * 8, 8), 8)
            vr, vi = re_ref[rows, :], im_ref[rows, :]
            for t, k in enumerate((1, 2, 4)):
                sh = (8 - k) if reverse else k
                rr, ri = pltpu.roll(vr, sh, 0), pltpu.roll(vi, sh, 0)
                pr, pi = tab_ref[2 * t], tab_ref[2 * t + 1]
                vr, vi = vr + pr * rr - pi * ri, vi + pr * ri + pi * rr
            lr, li = tab_ref[6], tab_ref[7]
            vr, vi = vr + lr * cr - li * ci, vi + lr * ci + li * cr
            ore_ref[rows, :] = vr
            oim_ref[rows, :] = vi
            e = 0 if reverse else 7
            return (jnp.broadcast_to(vr[e:e + 1, :], (8, SC)), jnp.broadcast_to(vi[e:e + 1, :], (8, SC)))

        cr, ci = lax.fori_loop(0, ng, step, (cre[...], cim[...]))
        cre[...] = cr
        cim[...] = ci

    blk = pl.BlockSpec((SL, SC), lambda j, i: (ti(i), j))
    return pl.pallas_call(
        body, name=name, grid=(NCH // SC, nl),
        in_specs=[blk, blk, pl.BlockSpec((8, 8, SC), lambda j, i: (0, 0, j))], out_specs=[blk, blk],
        out_shape=[jax.ShapeDtypeStruct((n, NCH), F32)] * 2,
        scratch_shapes=[pltpu.VMEM((8, SC), F32), pltpu.VMEM((8, SC), F32)],
        compiler_params=_cp("parallel", "arbitrary"),
    )(vre, vim, tab)


def _s5_out(xre, xim, cre, cimn, proj, dvec):
    n = xre.shape[0]
    uoff = OFF_U // 128

    def body(xr_ref, xi_ref, cr_ref, ci_ref, u_ref, d_ref, y_ref, g_ref):
        y = (_dot(xr_ref[...].astype(BF16), cr_ref[...]) + _dot(xi_ref[...].astype(BF16), ci_ref[...])
             + d_ref[...] * u_ref[...])
        y_ref[...] = y
        g_ref[...] = _gelu(y).astype(BF16)

    xs = pl.BlockSpec((SL, SCB), lambda i, j: (i, j))
    blk = pl.BlockSpec((None, SCB, 128), lambda i, j: (j, 0, 0))
    out = pl.BlockSpec((SL, 128), lambda i, j: (i, j))
    return pl.pallas_call(
        body, name="s5_out", grid=(n // SL, SB),
        in_specs=[xs, xs, blk, blk, pl.BlockSpec((SL, 128), lambda i, j: (i, uoff + j)),
                  pl.BlockSpec((1, 128), lambda i, j: (0, j))],
        out_specs=[out, out],
        out_shape=[jax.ShapeDtypeStruct((n, DS5), F32), jax.ShapeDtypeStruct((n, DS5), BF16)],
        compiler_params=_cp("parallel", "parallel"),
    )(xre, xim, cre, cimn, proj, dvec)


def _s5_out_bwd(dg, ypre, crt, cimnt, proj, dvec, xre, xim):
    n = dg.shape[0]
    uoff = OFF_U // 128
    nl = n // SL

    def body(dg_ref, y_ref, cr_ref, ci_ref, u_ref, d_ref, xr_ref, xi_ref,
             gr_ref, gi_ref, dus_ref, gcr_ref, gci_ref, gd_ref):
        dy = dg_ref[...] * _dgelu(y_ref[...])
        dyb = dy.astype(BF16)
        gr_ref[...] = _dot(dyb, cr_ref[...])
        gi_ref[...] = _dot(dyb, ci_ref[...])
        dus_ref[...] = dy * d_ref[...]

        @pl.when(pl.program_id(1) == 0)
        def _():
            gcr_ref[...] = jnp.zeros_like(gcr_ref)
            gci_ref[...] = jnp.zeros_like(gci_ref)
            gd_ref[...] = jnp.zeros_like(gd_ref)

        gcr_ref[...] += _dot(xr_ref[...].astype(BF16), dyb, TN)
        gci_ref[...] -= _dot(xi_ref[...].astype(BF16), dyb, TN)
        gd_ref[...] += jnp.sum(dy * u_ref[...], axis=0, keepdims=True)

    u128 = pl.BlockSpec((SL, 128), lambda j, i: (i, j))
    xs = pl.BlockSpec((SL, SCB), lambda j, i: (i, j))
    blk = pl.BlockSpec((None, 128, SCB), lambda j, i: (j, 0, 0))
    gblk = pl.BlockSpec((None, SCB, 128), lambda j, i: (j, 0, 0))
    vec = pl.BlockSpec((1, 128), lambda j, i: (0, j))
    return pl.pallas_call(
        body, name="s5_out_bwd", grid=(SB, nl),
        in_specs=[u128, u128, blk, blk, pl.BlockSpec((SL, 128), lambda j, i: (i, uoff + j)), vec, xs, xs],
        out_specs=[xs, xs, u128, gblk, gblk, vec],
        out_shape=[jax.ShapeDtypeStruct((n, NCH), F32)] * 2 + [jax.ShapeDtypeStruct((n, DS5), F32)]
        + [jax.ShapeDtypeStruct((SB, SCB, 128), F32)] * 2 + [jax.ShapeDtypeStruct((1, DS5), F32)],
        compiler_params=_cp("parallel", "arbitrary"),
    )(dg, ypre, crt, cimnt, proj, dvec, xre, xim)


def _s5_in_bwd(are, aim, brt, bit, proj, dus, xre, xim):
    n = are.shape[0]
    uoff = OFF_U // 128
    per = SL // 8

    def body(ar_ref, ai_ref, br_ref, bi_ref, u_ref, dus_ref, xr_ref, xi_ref, pr_ref, pi_ref,
             du_ref, gbr_ref, gbi_ref, glr_ref, gli_ref):
        i = pl.program_id(1)
        ar, ai = ar_ref[...], ai_ref[...]
        arb, aib = ar.astype(BF16), ai.astype(BF16)
        du_ref[...] = (_dot(arb, br_ref[...]) + _dot(aib, bi_ref[...]) + dus_ref[...]).astype(BF16)

        @pl.when(i == 0)
        def _():
            for r in (gbr_ref, gbi_ref, glr_ref, gli_ref):
                r[...] = jnp.zeros_like(r)

        ub = u_ref[...].astype(BF16)
        gbr_ref[...] += _dot(arb, ub, TN)
        gbi_ref[...] += _dot(aib, ub, TN)
        row0 = lax.broadcasted_iota(jnp.int32, (SL, SCB), 0) == 0
        last_r = jnp.where(i > 0, pr_ref[7:8, :], 0.0)
        last_i = jnp.where(i > 0, pi_ref[7:8, :], 0.0)
        xpr = jnp.where(row0, last_r, pltpu.roll(xr_ref[...], 1, 0))
        xpi = jnp.where(row0, last_i, pltpu.roll(xi_ref[...], 1, 0))
        glr_ref[...] += jnp.sum(ar * xpr + ai * xpi, axis=0, keepdims=True)
        gli_ref[...] += jnp.sum(ai * xpr - ar * xpi, axis=0, keepdims=True)

    xs = pl.BlockSpec((SL, SCB), lambda j, i: (i, j))
    prev = pl.BlockSpec((8, SCB), lambda j, i: (jnp.maximum(i * per - 1, 0), j))
    blk = pl.BlockSpec((None, SCB, 128), lambda j, i: (j, 0, 0))
    u128 = pl.BlockSpec((SL, 128), lambda j, i: (i, j))
    vec = pl.BlockSpec((1, SCB), lambda j, i: (0, j))
    return pl.pallas_call(
        body, name="s5_in_bwd", grid=(SB, n // SL),
        in_specs=[xs, xs, blk, blk, pl.BlockSpec((SL, 128), lambda j, i: (i, uoff + j)), u128, xs, xs, prev, prev],
        out_specs=[u128, blk, blk, vec, vec],
        out_shape=[jax.ShapeDtypeStruct((n, DS5), BF16)] + [jax.ShapeDtypeStruct((SB, SCB, 128), F32)] * 2
        + [jax.ShapeDtypeStruct((1, NCH), F32)] * 2,
        compiler_params=_cp("parallel", "arbitrary"),
    )(are, aim, brt, bit, proj, dus, xre, xim, xre, xim)


MC = 1024


def _merge_specs():
    ga = pl.BlockSpec((TL, MC), lambda i, j: (i, OFF_GA // MC + j))
    gb = pl.BlockSpec((TL, MC), lambda i, j: (i, OFF_GB // MC + j))
    col = pl.BlockSpec((TL, MC), lambda i, j: (i, j))
    gate = pl.BlockSpec((TL, MC), lambda i, j: (i, D // MC + j))
    return ga, gb, col, gate


def _merge_fwd(proj, ya, vg):
    n = ya.shape[0]

    def body(ga_ref, gb_ref, ya_ref, v_ref, g_ref, o_ref):
        yb = v_ref[...] * _sig(g_ref[...])
        o_ref[...] = (_sig(ga_ref[...]) * ya_ref[...] + _sig(gb_ref[...]) * yb).astype(BF16)

    ga, gb, col, gate = _merge_specs()
    return pl.pallas_call(
        body, name="merge_fwd", grid=(n // TL, D // MC), in_specs=[ga, gb, col, col, gate], out_specs=col,
        out_shape=jax.ShapeDtypeStruct((n, D), BF16), compiler_params=_cp("parallel", "parallel"),
    )(proj, proj, ya, vg, vg)


def _merge_bwd(dm, proj, ya, vg):
    n = ya.shape[0]

    def body(dm_ref, ga_ref, gb_ref, ya_ref, v_ref, g_ref, dga_ref, dgb_ref, dya_ref, dv_ref, dg_ref):
        d = dm_ref[...]
        sa, sb, sg = _sig(ga_ref[...]), _sig(gb_ref[...]), _sig(g_ref[...])
        v = v_ref[...]
        yb = v * sg
        dga_ref[...] = (d * ya_ref[...] * sa * (1.0 - sa)).astype(BF16)
        dgb_ref[...] = (d * yb * sb * (1.0 - sb)).astype(BF16)
        dya_ref[...] = (d * sa).astype(BF16)
        dyb = d * sb
        dv_ref[...] = (dyb * sg).astype(BF16)
        dg_ref[...] = (dyb * v * sg * (1.0 - sg)).astype(BF16)

    ga, gb, col, gate = _merge_specs()
    o = jax.ShapeDtypeStruct((n, D), BF16)
    return pl.pallas_call(
        body, name="merge_bwd", grid=(n // TL, D // MC), in_specs=[col, ga, gb, col, col, gate],
        out_specs=[col] * 5, out_shape=[o] * 5, compiler_params=_cp("parallel", "parallel"),
    )(dm, proj, proj, ya, vg, vg)


def _adamw_update(wv, gv, mv, vv):
    nm = B1 * mv + (1.0 - B1) * gv
    nv = B2 * vv + (1.0 - B2) * (gv * gv)
    m_hat = nm / (1.0 - B1 ** STEP)
    v_hat = nv / (1.0 - B2 ** STEP)
    return -LR * (m_hat / (jnp.sqrt(v_hat) + AEPS) + WD * wv), nm, nv


def _adamw(w, g, m, v, name):
    r, c = w.shape
    tr = _pick(r, 128)

    def body(w_ref, g_ref, m_ref, v_ref, d_ref, nm_ref, nv_ref):
        d_ref[...], nm_ref[...], nv_ref[...] = _adamw_update(w_ref[...], g_ref[...], m_ref[...], v_ref[...])

    blk = pl.BlockSpec((tr, c), lambda i: (i, 0))
    o = jax.ShapeDtypeStruct((r, c), F32)
    return pl.pallas_call(
        body, name=name, grid=(r // tr,), in_specs=[blk] * 4, out_specs=[blk] * 3, out_shape=[o] * 3,
        compiler_params=_cp("parallel"),
    )(w, g, m, v)


def _adamw_halves(w, g_mine, g_other, m, v, cidx, name):
    _, r, c = w.shape
    hr, gc = g_mine.shape
    tr = _pick(hr, 128)
    nbh = hr // tr
    assert gc == c and 2 * hr - tr < r <= 2 * hr

    def body(cs, w_ref, gm_ref, go_ref, m_ref, v_ref, g_ref, d_ref, nm_ref, nv_ref):
        mine = pl.program_id(0) // nbh == cs[0]
        gv = jnp.where(mine, gm_ref[...], go_ref[...])
        g_ref[...] = gv
        d_ref[...], nm_ref[...], nv_ref[...] = _adamw_update(w_ref[...], gv, m_ref[...], v_ref[...])

    blk = pl.BlockSpec((None, tr, c), lambda i, cs: (0, i, 0))
    gmine = pl.BlockSpec((tr, gc), lambda i, cs: (jnp.where(i // nbh == cs[0], i % nbh, 0), 0))
    gother = pl.BlockSpec((tr, gc), lambda i, cs: (jnp.where(i // nbh == cs[0], 0, i % nbh), 0))
    o = jax.ShapeDtypeStruct((1, r, c), F32)
    return pl.pallas_call(
        body, name=name,
        grid_spec=pltpu.PrefetchScalarGridSpec(num_scalar_prefetch=1, grid=(2 * nbh,),
                                               in_specs=[blk, gmine, gother, blk, blk], out_specs=[blk] * 4),
        out_shape=[o] * 4, compiler_params=_cp("parallel"),
    )(cidx, w, g_mine, g_other, m, v)


def _chip_sum(part, sib, cidx, name):
    _, r, cc = part.shape
    hr = r // 2
    tr = _pick(hr, 256)

    def body(cs, p_ref, s_ref, o_ref):
        o_ref[...] = (p_ref[...].astype(F32) + s_ref[...].astype(F32)).astype(BF16)

    blk = pl.BlockSpec((None, tr, cc), lambda k, i, cs: (k, i, 0))
    return pl.pallas_call(
        body, name=name,
        grid_spec=pltpu.PrefetchScalarGridSpec(
            num_scalar_prefetch=1, grid=(4, hr // tr),
            in_specs=[pl.BlockSpec((None, None, tr, cc), lambda k, i, cs: (k, cs[0], i, 0)), blk], out_specs=blk),
        out_shape=jax.ShapeDtypeStruct((4, hr, cc), BF16), compiler_params=_cp("parallel", "parallel"),
    )(cidx, part.reshape(4, 2, hr, cc), sib)


def _shard_sum(own, got, sidx, name):
    _, hr, cc = own.shape
    tr = _pick(hr, 256)

    def body(cs, own_ref, g0, g1, g2, g3, o_ref):
        acc = None
        for k, g_ref in enumerate((g0, g1, g2, g3)):
            term = jnp.where(cs[0] == k, own_ref[...], g_ref[...]).astype(F32)
            acc = term if acc is None else acc + term
        o_ref[...] = acc

    def got_spec(k):
        return pl.BlockSpec((None, tr, cc), lambda i, cs: (jnp.where(cs[0] == k, (k + 1) % 4, k), i, 0))

    return pl.pallas_call(
        body, name=name,
        grid_spec=pltpu.PrefetchScalarGridSpec(
            num_scalar_prefetch=1, grid=(hr // tr,),
            in_specs=[pl.BlockSpec((None, tr, cc), lambda i, cs: (cs[0], i, 0))] + [got_spec(k) for k in range(4)],
            out_specs=pl.BlockSpec((tr, cc), lambda i, cs: (i, 0))),
        out_shape=jax.ShapeDtypeStruct((hr, cc), F32), compiler_params=_cp("parallel"),
    )(sidx, own, got, got, got, got)


def _sum_slabs(xs, name, out_dtype=F32):
    r, c = xs[0].shape
    tr = _pick(r, 256)

    def body(*refs):
        acc = refs[0][...].astype(F32)
        for ref in refs[1:-1]:
            acc = acc + ref[...].astype(F32)
        refs[-1][...] = acc.astype(out_dtype)

    blk = pl.BlockSpec((tr, c), lambda i: (i, 0))
    return pl.pallas_call(
        body, name=name, grid=(r // tr,), in_specs=[blk] * len(xs), out_specs=blk,
        out_shape=jax.ShapeDtypeStruct((r, c), out_dtype), compiler_params=_cp("parallel"),
    )(*xs)


def _place():
    return lax.axis_index("x"), lax.axis_index("y"), lax.axis_index("c")


def _gather_small(v):
    m_per, n = v.shape

    def body(x_ref, out_ref, send_sems, recv_sems, local_sem):
        x, y, c = _place()
        me, sibling = (x, y, c), (x, y, 1 - c)
        chips = [(1 - x, y), (x, 1 - y), (1 - x, 1 - y)]

        def rows(px, py, pc):
            return out_ref.at[pl.ds((4 * px + 2 * py + pc) * m_per, m_per), :]

        def copy(k, block, to, src=None):
            return pltpu.make_async_remote_copy(
                src_ref=rows(*block) if src is None else src, dst_ref=rows(*block),
                send_sem=send_sems.at[k], recv_sem=recv_sems.at[k], device_id=to, device_id_type=MESH)

        mine = pltpu.make_async_copy(x_ref, rows(*me), local_sem)
        mine.start()
        first = [copy(0, me, sibling, src=x_ref)]
        first += [copy(1 + j, me, (*chip, c), src=x_ref) for j, chip in enumerate(chips)]
        for cp in first:
            cp.start()
        passed = [copy(4 + j, (*chip, c), sibling) for j, chip in enumerate(chips)]
        for j, chip in enumerate(chips):
            copy(1 + j, (*chip, c), me).wait_recv()
            passed[j].start()
        copy(0, sibling, me).wait_recv()
        for j, chip in enumerate(chips):
            copy(4 + j, (*chip, 1 - c), me).wait_recv()
        for cp in first + passed:
            cp.wait_send()
        mine.wait()

    return pl.pallas_call(
        body, name="gather_small_%d" % m_per,
        out_shape=jax.ShapeDtypeStruct((8 * m_per, n), v.dtype),
        in_specs=[pl.BlockSpec(memory_space=pltpu.VMEM)], out_specs=pl.BlockSpec(memory_space=pltpu.VMEM),
        scratch_shapes=[pltpu.SemaphoreType.DMA((7,)), pltpu.SemaphoreType.DMA((7,)), pltpu.SemaphoreType.DMA],
        compiler_params=pltpu.CompilerParams(vmem_limit_bytes=VMEM_LIMIT),
    )(v)


def _allsum_small(v, name):
    r = v.shape[0]
    g = _gather_small(v)
    return _sum_slabs([g[k * r:(k + 1) * r] for k in range(8)], name)


def _gather_big(shards):
    nt = len(shards)

    def body(*refs):
        ins, outs = refs[:nt], refs[nt:2 * nt]
        send_sems, recv_sems = refs[2 * nt:]
        x, y, c = _place()
        s = 2 * x + y
        sibling = (x, y, 1 - c)
        chips = [(1 - x, y), (x, 1 - y), (1 - x, 1 - y)]

        def half(t, slot, h):
            hr = ins[t].shape[0] // 2
            return outs[t].at[slot, pl.ds(h * hr, hr), :]

        def ici(t, j, src_slot, to):
            hr = ins[t].shape[0] // 2
            return pltpu.make_async_remote_copy(
                src_ref=ins[t].at[pl.ds(c * hr, hr), :], dst_ref=half(t, src_slot, c),
                send_sem=send_sems.at[7 * t + j], recv_sem=recv_sems.at[7 * t + j], device_id=to, device_id_type=MESH)

        def d2d(t, j, slot, h):
            return pltpu.make_async_remote_copy(
                src_ref=half(t, slot, h), dst_ref=half(t, slot, h),
                send_sem=send_sems.at[7 * t + 3 + j], recv_sem=recv_sems.at[7 * t + 3 + j],
                device_id=sibling, device_id_type=MESH)

        def whole(t):
            return pltpu.make_async_remote_copy(
                src_ref=ins[t], dst_ref=outs[t].at[s], send_sem=send_sems.at[7 * t + 6],
                recv_sem=recv_sems.at[7 * t + 6], device_id=sibling, device_id_type=MESH)

        sends = [ici(t, j, s, (*chip, c)) for t in range(nt) for j, chip in enumerate(chips)]
        sends += [whole(t) for t in range(nt)]
        for cp in sends:
            cp.start()
        passed = []
        for t in range(nt):
            for j, (px, py) in enumerate(chips):
                ici(t, j, 2 * px + py, (x, y, c)).wait_recv()
                cp = d2d(t, j, 2 * px + py, c)
                cp.start()
                passed.append(cp)
        for t in range(nt):
            for j, (px, py) in enumerate(chips):
                d2d(t, j, 2 * px + py, 1 - c).wait_recv()
            whole(t).wait_recv()
        for cp in sends + passed:
            cp.wait_send()

    return pl.pallas_call(
        body, name="gather_big",
        out_shape=[jax.ShapeDtypeStruct((4,) + a.shape, a.dtype) for a in shards],
        in_specs=[ANY] * nt, out_specs=[ANY] * nt,
        scratch_shapes=[pltpu.SemaphoreType.DMA((7 * nt,)), pltpu.SemaphoreType.DMA((7 * nt,))],
    )(*shards)


def _swap_halves(parts, name):
    nt = len(parts)

    def body(*refs):
        ins, outs = refs[:nt], refs[nt:2 * nt]
        send_sems, recv_sems = refs[2 * nt:]
        x, y, c = _place()
        cps = []
        for t in range(nt):
            hr = ins[t].shape[1] // 2
            cps.append(pltpu.make_async_remote_copy(
                src_ref=ins[t].at[:, pl.ds((1 - c) * hr, hr), :], dst_ref=outs[t],
                send_sem=send_sems.at[t], recv_sem=recv_sems.at[t], device_id=(x, y, 1 - c), device_id_type=MESH))
        for cp in cps:
            cp.start()
        for cp in cps:
            cp.wait()

    return pl.pallas_call(
        body, name=name,
        out_shape=[jax.ShapeDtypeStruct((4, a.shape[1] // 2, a.shape[2]), a.dtype) for a in parts],
        in_specs=[ANY] * nt, out_specs=[ANY] * nt,
        scratch_shapes=[pltpu.SemaphoreType.DMA((nt,)), pltpu.SemaphoreType.DMA((nt,))],
    )(*parts)


def _scatter_chips(parts):
    nt = len(parts)

    def body(*refs):
        ins, outs = refs[:nt], refs[nt:2 * nt]
        send_sems, recv_sems = refs[2 * nt:]
        x, y, c = _place()
        s = 2 * x + y
        chips = [(1 - x, y), (x, 1 - y), (1 - x, 1 - y)]
        cps = []
        for t in range(nt):
            for j, (px, py) in enumerate(chips):
                cps.append(pltpu.make_async_remote_copy(
                    src_ref=ins[t].at[2 * px + py], dst_ref=outs[t].at[s],
                    send_sem=send_sems.at[3 * t + j], recv_sem=recv_sems.at[3 * t + j],
                    device_id=(px, py, c), device_id_type=MESH))
        for cp in cps:
            cp.start()
        for t in range(nt):
            for j, (px, py) in enumerate(chips):
                pltpu.make_async_remote_copy(
                    src_ref=ins[t].at[s], dst_ref=outs[t].at[2 * px + py],
                    send_sem=send_sems.at[3 * t + j], recv_sem=recv_sems.at[3 * t + j],
                    device_id=(px, py, c), device_id_type=MESH).wait_recv()
        for cp in cps:
            cp.wait_send()

    return pl.pallas_call(
        body, name="scatter_chips",
        out_shape=[jax.ShapeDtypeStruct(a.shape, a.dtype) for a in parts],
        in_specs=[ANY] * nt, out_specs=[ANY] * nt,
        scratch_shapes=[pltpu.SemaphoreType.DMA((3 * nt,)), pltpu.SemaphoreType.DMA((3 * nt,))],
    )(*parts)


def _swap_whole(halves):
    nt = len(halves)

    def body(*refs):
        ins, outs = refs[:nt], refs[nt:2 * nt]
        send_sems, recv_sems = refs[2 * nt:]
        x, y, c = _place()
        cps = [pltpu.make_async_remote_copy(
            src_ref=ins[t], dst_ref=outs[t], send_sem=send_sems.at[t], recv_sem=recv_sems.at[t],
            device_id=(x, y, 1 - c), device_id_type=MESH) for t in range(nt)]
        for cp in cps:
            cp.start()
        for cp in cps:
            cp.wait()

    return pl.pallas_call(
        body, name="swap_whole",
        out_shape=[jax.ShapeDtypeStruct(a.shape, a.dtype) for a in halves],
        in_specs=[ANY] * nt, out_specs=[ANY] * nt,
        scratch_shapes=[pltpu.SemaphoreType.DMA((nt,)), pltpu.SemaphoreType.DMA((nt,))],
    )(*halves)


def _pass_halves(got, shards, name):
    nt = len(got)

    def body(*refs):
        ins, own, outs = refs[:nt], refs[nt:2 * nt], refs[2 * nt:3 * nt]
        send_sems, recv_sems = refs[3 * nt:]
        x, y, c = _place()
        s = 2 * x + y
        chips = [(1 - x, y), (x, 1 - y), (1 - x, 1 - y)]

        def half(ref, t, slot, h):
            hr = ins[t].shape[1] // 2
            return ref.at[slot, pl.ds(h * hr, hr), :]

        def copy(t, j, h):
            px, py = chips[j]
            return pltpu.make_async_remote_copy(
                src_ref=half(ins[t], t, 2 * px + py, h), dst_ref=half(outs[t], t, 2 * px + py, h),
                send_sem=send_sems.at[4 * t + j], recv_sem=recv_sems.at[4 * t + j],
                device_id=(x, y, 1 - c), device_id_type=MESH)

        def whole(t):
            return pltpu.make_async_remote_copy(
                src_ref=own[t], dst_ref=outs[t].at[s], send_sem=send_sems.at[4 * t + 3],
                recv_sem=recv_sems.at[4 * t + 3], device_id=(x, y, 1 - c), device_id_type=MESH)

        sends = [copy(t, j, c) for t in range(nt) for j in range(3)] + [whole(t) for t in range(nt)]
        for cp in sends:
            cp.start()
        for t in range(nt):
            for j in range(3):
                copy(t, j, 1 - c).wait_recv()
            whole(t).wait_recv()
        for cp in sends:
            cp.wait_send()

    return pl.pallas_call(
        body, name=name,
        out_shape=[jax.ShapeDtypeStruct(a.shape, a.dtype) for a in got],
        in_specs=[ANY] * (2 * nt), out_specs=[ANY] * nt, input_output_aliases={t: t for t in range(nt)},
        scratch_shapes=[pltpu.SemaphoreType.DMA((4 * nt,)), pltpu.SemaphoreType.DMA((4 * nt,))],
    )(*got, *shards)


HBM = pl.BlockSpec(memory_space=pltpu.HBM)
SEM = pl.BlockSpec(memory_space=pltpu.SEMAPHORE)
EFFECT = pltpu.SideEffectType.DATAFLOW_SIDE_EFFECTING


PER_TENSOR = {"gather": 3, "scatter": 3, "swap": 1, "pass": 4, "whole": 1}


def _ici_copies(kind, srcs, lands, send_sems, recv_sems):
    x, y, c = _place()
    s = 2 * x + y
    sib = (x, y, 1 - c)
    chips = [(1 - x, y), (x, 1 - y), (1 - x, 1 - y)]
    cps = []

    def add(src, dst, dev):
        k = len(cps)
        cps.append(pltpu.make_async_remote_copy(src_ref=src, dst_ref=dst, send_sem=send_sems[k], recv_sem=recv_sems[k],
                                                device_id=dev, device_id_type=MESH))

    for t in range(len(srcs)):
        if kind == "gather":
            hr = srcs[t].shape[0] // 2
            for px, py in chips:
                add(srcs[t].at[pl.ds(c * hr, hr), :], lands[t].at[s, pl.ds(c * hr, hr), :], (px, py, c))
        elif kind == "scatter":
            for px, py in chips:
                add(srcs[t].at[2 * px + py], lands[t].at[s], (px, py, c))
        elif kind == "swap":
            hr = srcs[t].shape[1] // 2
            add(srcs[t].at[:, pl.ds((1 - c) * hr, hr), :], lands[t], sib)
        elif kind == "pass":
            hr = srcs[t].shape[1] // 2
            for px, py in chips:
                half = srcs[t].at[2 * px + py, pl.ds(c * hr, hr), :]
                add(half, half, sib)
            add(lands[t], srcs[t].at[s], sib)
        else:
            add(srcs[t], lands[t], sib)
    return cps


def _ici_start(kind, srcs, after, name, lands=None):
    nt = len(srcs)
    nc = PER_TENSOR[kind] * nt
    hbm = lambda a: pltpu.with_memory_space_constraint(a, pltpu.HBM)
    if lands is None:
        shape = {"gather": lambda a: (4,) + a.shape, "scatter": lambda a: a.shape,
                 "swap": lambda a: (4, a.shape[1] // 2, a.shape[2]), "whole": lambda a: a.shape}[kind]
        lands = [lax.empty(shape(a), a.dtype) for a in srcs]

    def body(*refs):
        src, land = refs[:nt], refs[nt:2 * nt]
        outs = refs[2 * nt + 1:]
        for cp in _ici_copies(kind, src, land, outs[:nc], outs[nc:2 * nc]):
            cp.start()
        outs[-1][...] = jnp.zeros_like(outs[-1])

    outs = pl.pallas_call(
        body, name=name,
        out_shape=tuple([pltpu.SemaphoreType.DMA(())] * (2 * nc) + [pltpu.HBM(a.shape, a.dtype) for a in srcs]
                        + [pltpu.HBM(a.shape, a.dtype) for a in lands] + [jax.ShapeDtypeStruct((8, 128), F32)]),
        in_specs=[HBM] * (2 * nt) + [ANY],
        out_specs=tuple([SEM] * (2 * nc) + [HBM] * (2 * nt) + [pl.BlockSpec(memory_space=pltpu.VMEM)]),
        input_output_aliases={i: 2 * nc + i for i in range(2 * nt)},
        compiler_params=pltpu.CompilerParams(has_side_effects=EFFECT),
    )(*[hbm(a) for a in srcs], *[hbm(a) for a in lands], after)
    return outs[:2 * nc], outs[2 * nc:2 * nc + nt], outs[2 * nc + nt:2 * nc + 2 * nt], outs[-1]


def _ici_wait(kind, sems, srcs, lands, after, name):
    nt = len(srcs)
    nc = PER_TENSOR[kind] * nt

    def body(*refs):
        src, land = refs[:nt], refs[nt:2 * nt]
        sem = refs[2 * nt:2 * nt + 2 * nc]
        for cp in _ici_copies(kind, src, land, sem[:nc], sem[nc:]):
            cp.wait_send()
            cp.wait_recv()

    outs = pl.pallas_call(
        body, name=name,
        out_shape=tuple(pltpu.HBM(a.shape, a.dtype) for a in list(srcs) + list(lands)),
        in_specs=[HBM] * (2 * nt) + [SEM] * (2 * nc) + [ANY], out_specs=tuple([HBM] * (2 * nt)),
        input_output_aliases={i: i for i in range(2 * nt)},
        compiler_params=pltpu.CompilerParams(has_side_effects=EFFECT),
    )(*srcs, *lands, *sems, after)
    return outs[:nt], outs[nt:]


def _s5_params(lam_re, lam_im, log_dt, b_re, b_im):
    lr = jnp.minimum(lam_re, EIG_MAX)
    dt = jnp.exp(log_dt)[:, None]
    mag = jnp.exp(lr * dt)
    lbr, lbi = mag * jnp.cos(lam_im * dt), mag * jnp.sin(lam_im * dt)
    den = lr * lr + lam_im * lam_im
    qr = ((lbr - 1.0) * lr + lbi * lam_im) / den
    qi = (lbi * lr - (lbr - 1.0) * lam_im) / den
    bbr = qr[..., None] * b_re - qi[..., None] * b_im
    bbi = qr[..., None] * b_im + qi[..., None] * b_re
    return lbr, lbi, bbr, bbi


def _cmul(a, b):
    return a[0] * b[0] - a[1] * b[1], a[0] * b[1] + a[1] * b[0]


def _scan_table(lr, li, reverse):
    l1 = (lr.reshape(1, NCH), li.reshape(1, NCH))
    pows = [l1]
    for _ in range(7):
        pows.append(_cmul(pows[-1], l1))
    r = jnp.arange(8)[:, None]
    tabs = []
    for k in (1, 2, 4):
        keep = (r < 8 - k) if reverse else (r >= k)
        tabs += [jnp.where(keep, pows[k - 1][0], 0.0), jnp.where(keep, pows[k - 1][1], 0.0)]
    order = range(7, -1, -1) if reverse else range(8)
    tabs += [jnp.concatenate([pows[e][0] for e in order], axis=0), jnp.concatenate([pows[e][1] for e in order], axis=0)]
    return jnp.stack(tabs).astype(F32)


_EYE8 = lambda: jnp.eye(8, dtype=F32)


def _to_in_blocks(b):
    return jnp.einsum("jgpc,gh->jgchp", b.reshape(8, 8, 64, 16), _EYE8()).reshape(8, 128, 512)


def _to_out_blocks(cm):
    return jnp.einsum("jgcp,gh->jgphc", cm.reshape(8, 8, 16, 64), _EYE8()).reshape(8, 512, 128)


def _from_out_blocks(g):
    return jnp.einsum("jgphc,gh->jgpc", g.reshape(8, 8, 64, 8, 16), _EYE8()).reshape(64, 64, 16)


def _local_step(x, target, hn1, p, token, hooks):
    n = x.shape[0]
    g = {}
    proj = _matmul(hn1, p["w_full"], "nt", "mm_in", tn=896, after=token)
    dtraw = proj
    xbc = _conv_a_fwd(proj, p["conv_a_w"], p["conv_a_b"])
    to_lanes = lambda v: jnp.pad(jnp.pad(v.reshape(NG, HPG), ((0, 0), (0, 8 - HPG))).reshape(1, 8 * NG),
                                 ((0, 0), (0, HL - 8 * NG)))
    from_lanes = lambda v: v[:, :8 * NG].reshape(-1, NG, 8)[:, :, :HPG].reshape(-1, NG * HPG)
    hp = jnp.concatenate([to_lanes(p["dt_bias"]), to_lanes(p["a_log"]), jnp.zeros((6, HL), F32)], axis=0)
    lane = jnp.arange(HL)[:, None]
    emat = ((lane < 8 * NG) & (lane % 8 < HPG)
            & (jnp.arange(DI)[None, :] // HD == HPG * (lane // 8) + lane % 8)).astype(BF16)
    dexp = jnp.repeat(p["d_a"].reshape(1, NG * HPG), HD, axis=1)
    dt, s_cum, s_t, dt_e, s_e = _ssd_prep(dtraw, hp, emat)
    s8 = s_cum[:, :8 * NG].reshape(n, NG, 8).transpose(1, 0, 2)
    yssd, sprev = _ssd_fwd(xbc, s8, s_t, dt_e, s_e, dexp)
    yn = _gnorm_fwd(yssd, proj, p["norm_a_w"])
    tok = hooks["late_start"](yn)
    (lbr, lbi, bbr, bbi), s5_vjp = jax.vjp(_s5_params, p["s5_lam_re"], p["s5_lam_im"], p["s5_log_dt"],
                                           p["s5_b_re"], p["s5_b_im"])
    bin_r, bin_i = _to_in_blocks(bbr), _to_in_blocks(bbi)
    cout_r, cout_in = _to_out_blocks(p["s5_c_re"]), _to_out_blocks(-p["s5_c_im"])
    bur, bui = _s5_in(proj, bin_r.astype(BF16), bin_i.astype(BF16), after=tok)
    xre, xim = _s5_scan(bur, bui, _scan_table(lbr, lbi, False), False, "s5_scan_fwd")
    ypre, g5 = _s5_out(xre, xim, cout_r.astype(BF16), cout_in.astype(BF16), proj, p["s5_d"])
    p = {**p, **hooks["late_weights"](ypre)}
    ya = _matmul(yn, p["w_proj_a"], "nn", "mm_proj")
    vg = _matmul(g5, p["w_s5_glu"], "nn", "mm_glu", b_stacked=True)
    merged = _merge_fwd(proj, ya, vg)
    h1 = _matmul(merged, p["w_out"], "nn", "mm_out", residual=x)
    hn2 = _rms_fwd(h1, p["norm_ffn_w"], "rms_ffn")
    up = _matmul(hn2, p["w_up"], "nn", "mm_up", tn=1408, b_stacked=True)
    act = _conv_ffn_fwd(up, p["conv_ffn_w"], p["conv_ffn_b"])
    h2 = _matmul(act, p["w_down"], "nn", "mm_down", tk=DFF // 2, residual=h1)
    dh2, dh2b, g["norm_final_w"], loss_blk = _final(h2, p["norm_final_w"], target)
    g["w_down"] = _matmul(act, dh2b, "tn", "mm_gw_down", out_dtype=BF16).reshape(4, DFF // 4, D)
    dact = _matmul(dh2b, p["w_down"], "nt", "mm_dact", out_dtype=BF16)
    dup, g["conv_ffn_w"], g["conv_ffn_b"] = _conv_ffn_bwd(up, dact, p["conv_ffn_w"], p["conv_ffn_b"])
    g["w_up"] = _matmul(hn2, dup, "tn", "mm_gw_up", out_dtype=BF16, tn=1408, out_stacked=True)
    tok = hooks["swap_start"](["w_up", "w_down"], g, "s1")
    dhn2 = _matmul(dup, p["w_up"], "nt", "mm_dhn2", tk=2816, b_stacked=True, after=tok)
    tok = hooks["scatter_go"]("s1", dhn2)
    dh1, dh1b, g["norm_ffn_w"] = _rms_bwd(dhn2, h1, p["norm_ffn_w"], dh2, "rms_ffn_bwd", after=tok)
    g["w_out"] = _matmul(merged, dh1b, "tn", "mm_gw_out", out_dtype=BF16).reshape(4, D // 4, D)
    dmerged = _matmul(dh1b, p["w_out"], "nt", "mm_dmerged")
    dga, dgb, dya, dval, dgate = _merge_bwd(dmerged, proj, ya, vg)
    dvg = jnp.concatenate([dval, dgate], axis=1)
    g["w_s5_glu"] = _matmul(g5, dvg, "tn", "mm_gw_glu", out_dtype=BF16, out_stacked=True)
    dg5 = _matmul(dvg, p["w_s5_glu"], "nt", "mm_dg5", b_stacked=True)
    tr = lambda b: b.transpose(0, 2, 1)
    gxr, gxi, dus, gcr, gci, g["s5_d"] = _s5_out_bwd(dg5, ypre, tr(cout_r).astype(BF16), tr(cout_in).astype(BF16),
                                                     proj, p["s5_d"], xre, xim)
    are, aim = _s5_scan(gxr, gxi, _scan_table(lbr, -lbi, True), True, "s5_scan_bwd")
    du, gbr, gbi, glr, gli = _s5_in_bwd(are, aim, tr(bin_r).astype(BF16), tr(bin_i).astype(BF16), proj, dus, xre, xim)
    g["s5_c_re"] = _from_out_blocks(gcr).transpose(0, 2, 1)
    g["s5_c_im"] = _from_out_blocks(gci).transpose(0, 2, 1)
    (g["s5_lam_re"], g["s5_lam_im"], g["s5_log_dt"], g["s5_b_re"], g["s5_b_im"]) = s5_vjp(
        (glr.reshape(64, 64), gli.reshape(64, 64), _from_out_blocks(gbr), _from_out_blocks(gbi)))
    g["w_proj_a"] = _matmul(yn, dya, "tn", "mm_gw_proj", out_dtype=BF16).reshape(4, DI // 4, D)
    tok = hooks["swap_start"](["w_proj_a", "w_s5_glu", "w_out"], g, "s2")
    dyn = _matmul(dya, p["w_proj_a"], "nt", "mm_dyn", after=tok)
    tok = hooks["scatter_go"]("s2", dyn)
    dyssd, dz, g["norm_a_w"] = _gnorm_bwd(dyn, yssd, proj, p["norm_a_w"], tok)
    dxs, dbm, dcm, ds_e, ddt_e, tsum, dsh8, pd = _ssd_bwd(xbc, s8, s_t, dt_e, s_e, dexp, sprev, dyssd)
    dsh = jnp.pad(dsh8.transpose(1, 0, 2).reshape(n, 8 * NG), ((0, 0), (0, HL - 8 * NG)))
    draw, ps = _ssd_post(ds_e, ddt_e, tsum, dsh, dtraw, dt, hp, emat.T)
    g["dt_bias"] = from_lanes(ps[0:1])
    g["a_log"] = from_lanes(ps[1:2])
    g["d_a"] = pd.reshape(NG * HPG, HD).sum(axis=1).reshape(1, NG * HPG)
    ddt = draw.astype(BF16)
    dxbc_parts, gcw, gcb = [], [], []
    for arr, col0, nm in ((dxs, 0, "conv_a_bwd_x"), (dbm, DI, "conv_a_bwd_b"), (dcm, DI + NG * NS, "conv_a_bwd_c")):
        dpart, gw_, gb_ = _conv_a_bwd(proj, arr, p["conv_a_w"], p["conv_a_b"], col0, nm)
        dxbc_parts.append(dpart)
        gcw.append(gw_)
        gcb.append(gb_)
    g["conv_a_w"] = jnp.concatenate(gcw, axis=1)
    g["conv_a_b"] = jnp.concatenate(gcb, axis=1)
    dproj = jnp.concatenate([dz] + dxbc_parts + [du, dga, dgb, ddt], axis=1)
    g_main = _matmul(dproj, hn1, "tn", "mm_gw_in", out_dtype=BF16, tm=896, tn=2048)
    g_dt = g_main[NMAIN:NMAIN + 8 * NG].reshape(NG, 8, D)[:, :HPG].reshape(NG * HPG, D)
    g_sh = _move_rows(g_main, RUNS_TO_SHARDS, 4 * WPAD, MT, MT, "rows_to_shards")
    g["w_in"] = lax.dynamic_update_slice(g_sh, g_dt, (DT_SHARD_ROW, 0)).reshape(4, WPAD, D)
    tok = hooks["swap_start"](["w_in"], g, "s3")
    dhn1 = _matmul(dproj, p["w_full"], "nn", "mm_dhn1", tk=2688, after=tok)
    tok = hooks["scatter_go"]("s3", dhn1)
    gx, _, g["norm_mix_w"] = _rms_bwd(dhn1, x, p["norm_mix_w"], dh1, "rms_mix_bwd", after=tok)
    return loss_blk, gx, g


BIG = ["w_in", "w_proj_a", "w_s5_glu", "w_out", "w_up", "w_down"]
SMALL = ["norm_mix_w", "conv_a_w", "conv_a_b", "dt_bias", "a_log", "d_a", "norm_a_w", "s5_lam_re", "s5_lam_im",
         "s5_log_dt", "s5_b_re", "s5_b_im", "s5_c_re", "s5_c_im", "s5_d", "norm_ffn_w", "conv_ffn_w", "conv_ffn_b",
         "norm_final_w"]
ORDER = ["norm_mix_w", "w_in", "conv_a_w", "conv_a_b", "dt_bias", "a_log", "d_a", "norm_a_w", "w_proj_a", "s5_lam_re",
         "s5_lam_im", "s5_log_dt", "s5_b_re", "s5_b_im", "s5_c_re", "s5_c_im", "s5_d", "w_s5_glu", "w_out",
         "norm_ffn_w", "w_up", "conv_ffn_w", "conv_ffn_b", "w_down", "norm_final_w"]
CONV_FULL = {"conv_a_w": (KA, CONVD), "conv_ffn_w": (KF, 2 * DFF)}


def _pack(arrs):
    flat = jnp.concatenate([a.reshape(-1).astype(F32) for a in arrs])
    total = flat.shape[0]
    padded = -(-total // 1024) * 1024
    return jnp.pad(flat, (0, padded - total)).reshape(padded // 128, 128)


def _unpack(block, shapes):
    flat = block.reshape(-1)
    out, at = [], 0
    for sh in shapes:
        size = math.prod(sh)
        out.append(flat[at:at + size].reshape(sh))
        at += size
    return out


def _stack_cols(a):
    return a.transpose(1, 0, 2).reshape(a.shape[1], 4 * a.shape[2])


def _unstack_cols(a):
    return a.reshape(a.shape[0], 4, a.shape[1] // 4).transpose(1, 0, 2)


def kernel(x, norm_mix_w, w_in, conv_a_w, conv_a_b, dt_bias, a_log, d_a, norm_a_w, w_proj_a, s5_lam_re, s5_lam_im, s5_log_dt, s5_b_re, s5_b_im, s5_c_re, s5_c_im, s5_d, w_s5_glu, w_out, norm_ffn_w, w_up, conv_ffn_w, conv_ffn_b, w_down, norm_final_w, loss_target, m_norm_mix_w, m_w_in, m_conv_a_w, m_conv_a_b, m_dt_bias, m_a_log, m_d_a, m_norm_a_w, m_w_proj_a, m_s5_lam_re, m_s5_lam_im, m_s5_log_dt, m_s5_b_re, m_s5_b_im, m_s5_c_re, m_s5_c_im, m_s5_d, m_w_s5_glu, m_w_out, m_norm_ffn_w, m_w_up, m_conv_ffn_w, m_conv_ffn_b, m_w_down, m_norm_final_w, v_norm_mix_w, v_w_in, v_conv_a_w, v_conv_a_b, v_dt_bias, v_a_log, v_d_a, v_norm_a_w, v_w_proj_a, v_s5_lam_re, v_s5_lam_im, v_s5_log_dt, v_s5_b_re, v_s5_b_im, v_s5_c_re, v_s5_c_im, v_s5_d, v_w_s5_glu, v_w_out, v_norm_ffn_w, v_w_up, v_conv_ffn_w, v_conv_ffn_b, v_w_down, v_norm_final_w):
    w = dict(norm_mix_w=norm_mix_w, w_in=w_in, conv_a_w=conv_a_w, conv_a_b=conv_a_b, dt_bias=dt_bias, a_log=a_log, d_a=d_a, norm_a_w=norm_a_w, w_proj_a=w_proj_a, s5_lam_re=s5_lam_re, s5_lam_im=s5_lam_im, s5_log_dt=s5_log_dt, s5_b_re=s5_b_re, s5_b_im=s5_b_im, s5_c_re=s5_c_re, s5_c_im=s5_c_im, s5_d=s5_d, w_s5_glu=w_s5_glu, w_out=w_out, norm_ffn_w=norm_ffn_w, w_up=w_up, conv_ffn_w=conv_ffn_w, conv_ffn_b=conv_ffn_b, w_down=w_down, norm_final_w=norm_final_w)
    m = dict(norm_mix_w=m_norm_mix_w, w_in=m_w_in, conv_a_w=m_conv_a_w, conv_a_b=m_conv_a_b, dt_bias=m_dt_bias, a_log=m_a_log, d_a=m_d_a, norm_a_w=m_norm_a_w, w_proj_a=m_w_proj_a, s5_lam_re=m_s5_lam_re, s5_lam_im=m_s5_lam_im, s5_log_dt=m_s5_log_dt, s5_b_re=m_s5_b_re, s5_b_im=m_s5_b_im, s5_c_re=m_s5_c_re, s5_c_im=m_s5_c_im, s5_d=m_s5_d, w_s5_glu=m_w_s5_glu, w_out=m_w_out, norm_ffn_w=m_norm_ffn_w, w_up=m_w_up, conv_ffn_w=m_conv_ffn_w, conv_ffn_b=m_conv_ffn_b, w_down=m_w_down, norm_final_w=m_norm_final_w)
    v = dict(norm_mix_w=v_norm_mix_w, w_in=v_w_in, conv_a_w=v_conv_a_w, conv_a_b=v_conv_a_b, dt_bias=v_dt_bias, a_log=v_a_log, d_a=v_d_a, norm_a_w=v_norm_a_w, w_proj_a=v_w_proj_a, s5_lam_re=v_s5_lam_re, s5_lam_im=v_s5_lam_im, s5_log_dt=v_s5_log_dt, s5_b_re=v_s5_b_re, s5_b_im=v_s5_b_im, s5_c_re=v_s5_c_re, s5_c_im=v_s5_c_im, s5_d=v_s5_d, w_s5_glu=v_w_s5_glu, w_out=v_w_out, norm_ffn_w=v_norm_ffn_w, w_up=v_w_up, conv_ffn_w=v_conv_ffn_w, conv_ffn_b=v_conv_ffn_b, w_down=v_w_down, norm_final_w=v_norm_final_w)
    xi, yi, ci = _place()
    chip = 2 * xi + yi

    cidx = jnp.reshape(ci, (1,)).astype(jnp.int32)
    sidx = jnp.reshape(chip, (1,)).astype(jnp.int32)

    tw = lambda a: jnp.transpose(a[0])[None]
    w["w_in"], m["w_in"], v["w_in"] = tw(w_in), tw(m_w_in), tw(v_w_in)
    shards = [w[k][0].astype(BF16) for k in BIG]
    shards[0] = jnp.pad(shards[0], ((0, WPAD - WSH), (0, 0)))
    i_sems, i_srcs, i_lands, i_tok = _ici_start("gather", shards[:1], cidx, "gather_in_start")
    hn1 = _rms_fwd(x[0], norm_mix_w, "rms_mix", after=i_tok)

    late = {}

    def late_start(after):
        srcs, got = _ici_wait("gather", g_sems, g_srcs, g_lands, after, "gather_rest_wait")
        late["sems"], late["got"], late["srcs"], tok = _ici_start("pass", list(got), cidx, "pass_rest_start",
                                                                  lands=list(srcs))
        return tok

    def late_weights(after):
        full, _ = _ici_wait("pass", late["sems"], late["got"], late["srcs"], after, "pass_rest_wait")
        return {"w_proj_a": full[0].reshape(DI, D), "w_s5_glu": full[1], "w_out": full[2].reshape(D, D),
                "w_up": full[3], "w_down": full[4].reshape(DFF, D)}

    swaps, pending = {}, []

    def swap_start(names, g, tag):
        sems, parts, lands, tok = _ici_start("swap", [g[k] for k in names], cidx, "swap_start_" + tag)
        swaps[tag] = (names, sems, parts, lands)
        return tok

    def scatter_go(tag, after):
        names, sems, parts, lands = swaps[tag]
        parts, sib = _ici_wait("swap", sems, parts, lands, after, "swap_wait_" + tag)
        sums = [_chip_sum(parts[t], sib[t], cidx, "chip_sum_" + k) for t, k in enumerate(names)]
        sems, srcs, lands, tok = _ici_start("scatter", sums, cidx, "scatter_start_" + tag)
        pending.append((names, tag, sems, srcs, lands))
        return tok

    hooks = {"late_start": late_start, "late_weights": late_weights, "swap_start": swap_start,
             "scatter_go": scatter_go}
    conv_blocks = []
    for k, (taps, cols) in CONV_FULL.items():
        shard = jnp.where(ci == 0, w[k][0], 0.0)
        conv_blocks.append(lax.dynamic_update_slice_in_dim(jnp.zeros((taps, cols), F32), shard, chip * (cols // 4), 1))
    conv_full = _unpack(_allsum_small(_pack(conv_blocks), "sum_conv_w"), [CONV_FULL[k] for k in CONV_FULL])

    i_srcs, i_got = _ici_wait("gather", i_sems, i_srcs, i_lands, conv_full[0], "gather_in_wait")
    w_sh = _pass_halves(list(i_got), list(i_srcs), "pass_halves_in")[0].reshape(4 * WPAD, D)
    w_dt = jnp.pad(w_sh[DT_SHARD_ROW:DT_SHARD_ROW + NG * HPG].reshape(NG, HPG, D), ((0, 0), (0, 8 - HPG), (0, 0)))
    w_full = lax.dynamic_update_slice(_move_rows(w_sh, RUNS_TO_MAIN, NFULL, MT, MT, "rows_to_main"),
                                      w_dt.reshape(8 * NG, D), (NMAIN, 0))
    g_sems, g_srcs, g_lands, token = _ici_start("gather", shards[1:], w_full, "gather_rest_start")
    p = {
        "w_full": w_full,
        "conv_a_w": conv_full[0], "conv_ffn_w": conv_full[1],
        "conv_a_b": conv_a_b, "conv_ffn_b": conv_ffn_b,
        "norm_mix_w": norm_mix_w, "norm_a_w": norm_a_w, "norm_ffn_w": norm_ffn_w,
        "norm_final_w": norm_final_w.reshape(1, D),
        "dt_bias": dt_bias, "a_log": a_log, "d_a": d_a, "s5_d": s5_d,
        "s5_lam_re": s5_lam_re[0], "s5_lam_im": s5_lam_im[0], "s5_log_dt": s5_log_dt[0],
        "s5_b_re": s5_b_re[0], "s5_b_im": s5_b_im[0], "s5_c_re": s5_c_re[0], "s5_c_im": s5_c_im[0],
    }
    loss_blk, gx, g = _local_step(x[0], loss_target[0], hn1, p, token, hooks)

    after, halves = gx, {}
    for names, tag, sems, srcs, lands in pending:
        srcs, got = _ici_wait("scatter", sems, srcs, lands, after, "scatter_wait_" + tag)
        for t, k in enumerate(names):
            halves[k] = _shard_sum(srcs[t], got[t], sidx, "shard_sum_" + k)
        after = halves[names[0]]
    w_sems, g_mine, w_lands, w_tok = _ici_start("whole", [halves[k] for k in BIG], cidx, "whole_start")

    small_shapes = [CONV_FULL.get(k, w[k].shape[1:] if k != "norm_final_w" else w[k].shape) for k in SMALL]
    small = _allsum_small(_pack([g[k] for k in SMALL] + [loss_blk[0:1, 0:1]]), "sum_small_grads")
    small_grads = dict(zip(SMALL + ["loss"], _unpack(small, small_shapes + [(1,)])))
    for k, (taps, cols) in CONV_FULL.items():
        small_grads[k] = lax.dynamic_slice_in_dim(small_grads[k], chip * (cols // 4), cols // 4, axis=1)
    loss = small_grads.pop("loss").reshape(())

    grads, delta, new_m, new_v = {}, {}, {}, {}
    for k in SMALL:
        grads[k] = small_grads[k].reshape(w[k].shape)
    pk = lambda t: _pack([t[k] for k in SMALL])
    d_, m_, v_ = _adamw(pk(w), pk(grads), pk(m), pk(v), "adamw_small")
    shapes = [w[k].shape for k in SMALL]
    for k, dd, mm, vv in zip(SMALL, _unpack(d_, shapes), _unpack(m_, shapes), _unpack(v_, shapes)):
        delta[k], new_m[k], new_v[k] = dd, mm, vv
    g_mine, g_other = _ici_wait("whole", w_sems, g_mine, w_lands, d_, "whole_wait")
    for t, k in enumerate(BIG):
        outs = _adamw_halves(w[k], g_mine[t], g_other[t], m[k], v[k], cidx, "adamw_" + k)
        grads[k], delta[k], new_m[k], new_v[k] = [tw(o) for o in outs] if k == "w_in" else outs
    return (loss, gx[None], *[grads[k] for k in ORDER], *[delta[k] for k in ORDER],
            *[new_m[k] for k in ORDER], *[new_v[k] for k in ORDER])
```

```python
import functools
import math

import jax
import jax.numpy as jnp
from jax import lax
from jax.experimental import pallas as pl
from jax.experimental.pallas import tpu as pltpu

F32 = jnp.float32
BF16 = jnp.bfloat16
HI = lax.Precision.HIGHEST
MESH = pl.DeviceIdType.MESH
ANY = pl.BlockSpec(memory_space=pl.ANY)

D = 2048
DI = 3072
HD = 64
NG = 8
HPG = 6
GW = HPG * HD
NS = 128
KA = 4
Q = 256
CONVD = DI + 2 * NG * NS
DS5 = 1024
NCH = 4096
DFF = 5632
KF = 3
EPS = 1e-6
EIG_MAX = -1e-4
NMAIN = 13312
OFF_XBC, OFF_U, OFF_GA, OFF_GB = 3072, 8192, 9216, 11264
WSH = 3340
WPAD = 3360
IN_SPLIT = [DI, DI + CONVD, DI + CONVD + NG * HPG]
NFULL = NMAIN + 128
MT = 336


def _w_in_runs():
    runs = []
    for k in range(4):
        for o_lo, o_hi, m_lo in ((0, IN_SPLIT[1], 0), (IN_SPLIT[2], 4 * WSH, IN_SPLIT[1])):
            lo, hi = max(o_lo, WSH * k), min(o_hi, WSH * (k + 1))
            if lo < hi:
                runs.append((m_lo + lo - o_lo, m_lo + hi - o_lo, WPAD * k + lo - WSH * k))
    return runs


RUNS_TO_MAIN = _w_in_runs()
RUNS_TO_SHARDS = [(s_lo, s_lo + m_hi - m_lo, m_lo) for m_lo, m_hi, s_lo in RUNS_TO_MAIN]
DT_SHARD_ROW = WPAD * (IN_SPLIT[1] // WSH) + IN_SPLIT[1] % WSH
assert IN_SPLIT[1] // WSH == (IN_SPLIT[2] - 1) // WSH
VMEM_LIMIT = 56 * 1024 * 1024

LR, B1, B2, AEPS, WD, STEP = 0.001, 0.9, 0.999, 1e-08, 0.01, 10


def _cp(*sem):
    return pltpu.CompilerParams(dimension_semantics=sem, vmem_limit_bytes=VMEM_LIMIT)


def _sig(x):
    return jax.nn.sigmoid(x)


def _silu(x):
    return x * _sig(x)


def _dsilu(x):
    s = _sig(x)
    return s * (1.0 + x * (1.0 - s))


def _softplus(x):
    return jnp.maximum(x, 0.0) + jnp.log(1.0 + jnp.exp(-jnp.abs(x)))


_GC = math.sqrt(2.0 / math.pi)


def _gelu(x):
    return 0.5 * x * (1.0 + jnp.tanh(_GC * (x + 0.044715 * x * x * x)))


def _dgelu(x):
    t = jnp.tanh(_GC * (x + 0.044715 * x * x * x))
    return 0.5 * (1.0 + t) + 0.5 * x * (1.0 - t * t) * _GC * (1.0 + 3.0 * 0.044715 * x * x)


def _dot(a, b, dims=((1,), (0,)), prec=None):
    return lax.dot_general(a, b, (dims, ((), ())), precision=prec, preferred_element_type=F32)


NT = ((1,), (1,))
TN = ((0,), (0,))


def _pick(n, t):
    for unit in (128, 8):
        for cand in range(min(n, t) // unit * unit, 0, -unit):
            if n % cand == 0:
                return cand
    return n


def _matmul(a, b, mode, name, out_dtype=F32, tm=1024, tn=1024, tk=2048, residual=None, b_stacked=False,
            out_stacked=False, after=None):
    if b_stacked:
        _, brows, bn = b.shape
        bshape = (brows, 4 * bn)
    else:
        bshape = b.shape
    if mode == "nn":
        (m, k), (k2, n) = a.shape, bshape
    elif mode == "nt":
        (m, k), (n, k2) = a.shape, bshape
    else:
        (k, m), (k2, n) = a.shape, bshape
    assert k == k2
    tm = _pick(m, tm)
    tn = _pick(n // 4 if (out_stacked or (b_stacked and mode != "nt")) else n, tn)
    tk = _pick(k // 4 if (b_stacked and mode == "nt") else k, tk)
    nk = k // tk
    dims = {"nn": ((1,), (0,)), "nt": NT, "tn": TN}[mode]
    has_res = residual is not None
    n_in = 2 + has_res + (after is not None)

    def body(*refs):
        a_ref, b_ref = refs[0], refs[1]
        r_ref = refs[2] if has_res else None
        o_ref = refs[n_in]
        p = _dot(a_ref[...], b_ref[...], dims)

        def finish(r):
            if has_res:
                r = r + r_ref[...]
            o_ref[...] = r.astype(out_dtype)

        if nk == 1:
            finish(p)
        else:
            acc = refs[-1]
            kk = pl.program_id(2)

            @pl.when(kk == 0)
            def _():
                acc[...] = p

            @pl.when(kk > 0)
            def _():
                acc[...] += p

            @pl.when(kk == nk - 1)
            def _():
                finish(acc[...])

    if mode == "tn":
        a_spec = pl.BlockSpec((tk, tm), lambda i, j, kk: (kk, i))
    else:
        a_spec = pl.BlockSpec((tm, tk), lambda i, j, kk: (i, kk))
    if mode == "nt":
        if b_stacked:
            per = bn // tk
            b_spec = pl.BlockSpec((None, tn, tk), lambda i, j, kk: (kk // per, j, kk % per))
        else:
            b_spec = pl.BlockSpec((tn, tk), lambda i, j, kk: (j, kk))
    elif b_stacked:
        per = bn // tn
        b_spec = pl.BlockSpec((None, tk, tn), lambda i, j, kk: (j // per, kk, j % per))
    else:
        b_spec = pl.BlockSpec((tk, tn), lambda i, j, kk: (kk, j))
    o_spec = pl.BlockSpec((tm, tn), lambda i, j, kk: (i, j))
    if out_stacked:
        per_o = n // 4 // tn
        out_spec = pl.BlockSpec((None, tm, tn), lambda i, j, kk: (j // per_o, i, j % per_o))
        out_shape = jax.ShapeDtypeStruct((4, m, n // 4), out_dtype)
    else:
        out_spec, out_shape = o_spec, jax.ShapeDtypeStruct((m, n), out_dtype)
    in_specs, args = [a_spec, b_spec], [a, b]
    if has_res:
        in_specs.append(o_spec)
        args.append(residual)
    if after is not None:
        in_specs.append(ANY)
        args.append(after)
    return pl.pallas_call(
        body, name=name, grid=(m // tm, n // tn, nk),
        in_specs=in_specs, out_specs=out_spec, out_shape=out_shape,
        scratch_shapes=[pltpu.VMEM((tm, tn), F32)] if nk > 1 else [],
        compiler_params=_cp("parallel", "parallel", "arbitrary"),
    )(*args)


def _move_rows(src, runs, rows_out, t_out, t_in, name):
    rows_in, cols = src.shape
    nb_out, nb_in = rows_out // t_out, rows_in // t_in
    assert rows_out % t_out == 0 and rows_in % t_in == 0 and t_in >= t_out
    blk, off, lo, hi = ([[0] * nb_out for _ in range(2)] for _ in range(4))
    for i in range(nb_out):
        hits = [r for r in runs if r[0] < (i + 1) * t_out and r[1] > i * t_out]
        assert len(hits) <= 2
        for s, (o_lo, o_hi, s_lo) in enumerate(hits):
            lo[s][i] = max(o_lo, i * t_out) - i * t_out
            hi[s][i] = min(o_hi, (i + 1) * t_out) - i * t_out
            first = i * t_out + lo[s][i] - o_lo + s_lo
            blk[s][i] = min(first // t_in, nb_in - 1)
            off[s][i] = first - lo[s][i] - blk[s][i] * t_in
    table = jnp.asarray([blk[0], off[0], lo[0], hi[0], blk[1], off[1], lo[1], hi[1]], jnp.int32)

    def body(tab, a0, a1, b0, b1, o_ref):
        i = pl.program_id(0)
        o_ref[...] = jnp.zeros_like(o_ref)
        r = lax.broadcasted_iota(jnp.int32, (t_out, t_in), 0)
        k = lax.broadcasted_iota(jnp.int32, (t_out, t_in), 1)
        for s, (first, second) in enumerate(((a0, a1), (b0, b1))):
            off_s, lo_s, hi_s = tab[4 * s + 1, i], tab[4 * s + 2, i], tab[4 * s + 3, i]
            live = (r >= lo_s) & (r < hi_s)

            @pl.when(hi_s > lo_s)
            def _():
                sel = (live & (k == r + off_s)).astype(BF16)
                o_ref[...] += _dot(sel, first[...]).astype(o_ref.dtype)

            @pl.when((hi_s > lo_s) & (off_s + hi_s > t_in))
            def _():
                sel = (live & (k == r + off_s - t_in)).astype(BF16)
                o_ref[...] += _dot(sel, second[...]).astype(o_ref.dtype)

    def in_spec(s, nxt):
        return pl.BlockSpec((t_in, cols), lambda i, tab: (jnp.minimum(tab[4 * s, i] + nxt, nb_in - 1), 0))

    return pl.pallas_call(
        body, name=name,
        grid_spec=pltpu.PrefetchScalarGridSpec(
            num_scalar_prefetch=1, grid=(nb_out,),
            in_specs=[in_spec(0, 0), in_spec(0, 1), in_spec(1, 0), in_spec(1, 1)],
            out_specs=pl.BlockSpec((t_out, cols), lambda i, tab: (i, 0))),
        out_shape=jax.ShapeDtypeStruct((rows_out, cols), src.dtype), compiler_params=_cp("parallel"),
    )(table, src, src, src, src)


TL = 256


def _rms_fwd(x, w, name, after=None):
    n, d = x.shape

    def body(x_ref, w_ref, *rest):
        xv = x_ref[...]
        r = lax.rsqrt(jnp.mean(xv * xv, axis=-1, keepdims=True) + EPS)
        rest[-1][...] = (xv * r * w_ref[...]).astype(BF16)

    extra = [] if after is None else [after]
    return pl.pallas_call(
        body, name=name, grid=(n // TL,),
        in_specs=[pl.BlockSpec((TL, d), lambda i: (i, 0)), pl.BlockSpec((1, d), lambda i: (0, 0))] + [ANY] * len(extra),
        out_specs=pl.BlockSpec((TL, d), lambda i: (i, 0)),
        out_shape=jax.ShapeDtypeStruct((n, d), BF16), compiler_params=_cp("parallel"),
    )(x, w, *extra)


def _rms_bwd(dhn, x, w, dres, name, after=None):
    n, d = x.shape

    def body(g_ref, x_ref, w_ref, r_ref, *rest):
        dx_ref, dxb_ref, gw_ref = rest[-3:]
        xv = x_ref[...]
        r = lax.rsqrt(jnp.mean(xv * xv, axis=-1, keepdims=True) + EPS)
        xh = xv * r
        gv = g_ref[...]
        g = gv * w_ref[...]
        dx = r_ref[...] + r * (g - xh * jnp.mean(g * xh, axis=-1, keepdims=True))
        dx_ref[...] = dx
        dxb_ref[...] = dx.astype(BF16)

        @pl.when(pl.program_id(0) == 0)
        def _():
            gw_ref[...] = jnp.zeros_like(gw_ref)

        gw_ref[...] += jnp.sum(gv * xh, axis=0, keepdims=True)

    extra = [] if after is None else [after]
    row = pl.BlockSpec((TL, d), lambda i: (i, 0))
    vec = pl.BlockSpec((1, d), lambda i: (0, 0))
    return pl.pallas_call(
        body, name=name, grid=(n // TL,),
        in_specs=[row, row, vec, row] + [ANY] * len(extra), out_specs=[row, row, vec],
        out_shape=[jax.ShapeDtypeStruct((n, d), F32), jax.ShapeDtypeStruct((n, d), BF16),
                   jax.ShapeDtypeStruct((1, d), F32)],
        compiler_params=_cp("arbitrary"),
    )(dhn, x, w, dres, *extra)


def _final(h2, w, target):
    n, d = h2.shape

    def body(x_ref, w_ref, t_ref, dx_ref, dxb_ref, gw_ref, loss_ref):
        xv = x_ref[...]
        r = lax.rsqrt(jnp.mean(xv * xv, axis=-1, keepdims=True) + EPS)
        xh = xv * r
        diff = xh * w_ref[...] - t_ref[...]
        gv = diff * (1.0 / d)
        g = gv * w_ref[...]
        dx = r * (g - xh * jnp.mean(g * xh, axis=-1, keepdims=True))
        dx_ref[...] = dx
        dxb_ref[...] = dx.astype(BF16)

        @pl.when(pl.program_id(0) == 0)
        def _():
            gw_ref[...] = jnp.zeros_like(gw_ref)
            loss_ref[...] = jnp.zeros_like(loss_ref)

        gw_ref[...] += jnp.sum(gv * xh, axis=0, keepdims=True)
        part = 0.5 * jnp.sum(jnp.mean(diff * diff, axis=-1, keepdims=True), axis=0, keepdims=True)
        loss_ref[...] += jnp.broadcast_to(part, loss_ref.shape)

    row = pl.BlockSpec((TL, d), lambda i: (i, 0))
    vec = pl.BlockSpec((1, d), lambda i: (0, 0))
    return pl.pallas_call(
        body, name="final_loss", grid=(n // TL,),
        in_specs=[row, vec, row], out_specs=[row, row, vec, pl.BlockSpec((8, 128), lambda i: (0, 0))],
        out_shape=[jax.ShapeDtypeStruct((n, d), F32), jax.ShapeDtypeStruct((n, d), BF16),
                   jax.ShapeDtypeStruct((1, d), F32), jax.ShapeDtypeStruct((8, 128), F32)],
        compiler_params=_cp("arbitrary"),
    )(h2, w, target)


CT = 512
CL = 512


def _lagged(xf, taps, rows):
    return [xf[8:8 + rows]] + [pltpu.roll(xf, s, 0)[8:8 + rows] for s in range(1, taps)]


def _shift_up(x, u, n):
    if u == 0:
        return x[0:n]
    return pltpu.roll(x, x.shape[0] - u, 0)[0:n]


def _conv_pre(lagged, w_ref, b_ref, taps):
    pre = b_ref[...]
    for k in range(taps):
        pre = pre + w_ref[k:k + 1, :] * lagged[taps - 1 - k]
    return pre


def _conv_back(e, w_ref, taps):
    dx = w_ref[taps - 1:taps, :] * e[0:CL]
    for k in range(taps - 1):
        dx = dx + w_ref[k:k + 1, :] * _shift_up(e, taps - 1 - k, CL)
    return dx


def _halo_specs(n, col_of):
    per = CL // 8
    cur = pl.BlockSpec((CL, CT), lambda j, i, *_: (i, col_of(j)))
    prev = pl.BlockSpec((8, CT), lambda j, i, *_: (jnp.maximum(i * per - 1, 0), col_of(j)))
    nxt = pl.BlockSpec((8, CT), lambda j, i, *_: (jnp.minimum((i + 1) * per, n // 8 - 1), col_of(j)))
    return prev, cur, nxt


def _conv_a_fwd(proj, w, b):
    n = proj.shape[0]
    off = OFF_XBC // CT

    def body(p_ref, x_ref, w_ref, b_ref, o_ref):
        p8 = jnp.where(pl.program_id(1) > 0, p_ref[...], 0.0)
        xf = jnp.concatenate([p8, x_ref[...]], axis=0)
        o_ref[...] = _silu(_conv_pre(_lagged(xf, KA, CL), w_ref, b_ref, KA))

    prev, cur, _ = _halo_specs(n, lambda j: j + off)
    return pl.pallas_call(
        body, name="conv_a_fwd", grid=(CONVD // CT, n // CL),
        in_specs=[prev, cur, pl.BlockSpec((KA, CT), lambda j, i: (0, j)), pl.BlockSpec((1, CT), lambda j, i: (0, j))],
        out_specs=pl.BlockSpec((CL, CT), lambda j, i: (i, j)),
        out_shape=jax.ShapeDtypeStruct((n, CONVD), F32), compiler_params=_cp("parallel", "parallel"),
    )(proj, proj, w, b)


def _conv_a_bwd(proj, dout, w, b, col0, name):
    n, width = dout.shape
    off = (OFF_XBC + col0) // CT
    woff = col0 // CT
    nl = n // CL

    def body(p_ref, x_ref, n_ref, d_ref, dn_ref, w_ref, b_ref, dx_ref, dw_ref, db_ref):
        i = pl.program_id(1)
        xf = jnp.concatenate([jnp.where(i > 0, p_ref[...], 0.0), x_ref[...], n_ref[...]], axis=0)
        lag = _lagged(xf, KA, CL + 8)
        de = jnp.concatenate([d_ref[...], jnp.where(i < nl - 1, dn_ref[...], 0.0)], axis=0)
        se = de * _dsilu(_conv_pre(lag, w_ref, b_ref, KA))
        dx_ref[...] = _conv_back(se, w_ref, KA).astype(BF16)

        @pl.when(i == 0)
        def _():
            dw_ref[...] = jnp.zeros_like(dw_ref)
            db_ref[...] = jnp.zeros_like(db_ref)

        sc = se[0:CL]
        for k in range(KA):
            dw_ref[k:k + 1, :] += jnp.sum(sc * lag[KA - 1 - k][0:CL], axis=0, keepdims=True)
        db_ref[...] += jnp.sum(sc, axis=0, keepdims=True)

    prev, cur, nxt = _halo_specs(n, lambda j: j + off)
    _, dcur, dnxt = _halo_specs(n, lambda j: j)
    wspec = pl.BlockSpec((KA, CT), lambda j, i: (0, j + woff))
    bspec = pl.BlockSpec((1, CT), lambda j, i: (0, j + woff))
    return pl.pallas_call(
        body, name=name, grid=(width // CT, nl),
        in_specs=[prev, cur, nxt, dcur, dnxt, wspec, bspec],
        out_specs=[pl.BlockSpec((CL, CT), lambda j, i: (i, j)), pl.BlockSpec((KA, CT), lambda j, i: (0, j)),
                   pl.BlockSpec((1, CT), lambda j, i: (0, j))],
        out_shape=[jax.ShapeDtypeStruct((n, width), BF16), jax.ShapeDtypeStruct((KA, width), F32),
                   jax.ShapeDtypeStruct((1, width), F32)],
        compiler_params=_cp("parallel", "arbitrary"),
    )(proj, proj, proj, dout, dout, w, b)


def _conv_ffn_fwd(up, w, b):
    n = up.shape[0]
    nb = DFF // CT

    def body(pg_ref, g_ref, pv_ref, v_ref, wg_ref, bg_ref, wv_ref, bv_ref, o_ref):
        inner = pl.program_id(1) > 0
        gf = jnp.concatenate([jnp.where(inner, pg_ref[...], 0.0), g_ref[...]], axis=0)
        vf = jnp.concatenate([jnp.where(inner, pv_ref[...], 0.0), v_ref[...]], axis=0)
        gc = _conv_pre(_lagged(gf, KF, CL), wg_ref, bg_ref, KF)
        vc = _conv_pre(_lagged(vf, KF, CL), wv_ref, bv_ref, KF)
        o_ref[...] = (_silu(gc) * vc).astype(BF16)

    gp, gcur, _ = _halo_specs(n, lambda j: j)
    vp, vcur, _ = _halo_specs(n, lambda j: j + nb)
    return pl.pallas_call(
        body, name="conv_ffn_fwd", grid=(nb, n // CL),
        in_specs=[gp, gcur, vp, vcur,
                  pl.BlockSpec((KF, CT), lambda j, i: (0, j)), pl.BlockSpec((1, CT), lambda j, i: (0, j)),
                  pl.BlockSpec((KF, CT), lambda j, i: (0, j + nb)), pl.BlockSpec((1, CT), lambda j, i: (0, j + nb))],
        out_specs=pl.BlockSpec((CL, CT), lambda j, i: (i, j)),
        out_shape=jax.ShapeDtypeStruct((n, DFF), BF16), compiler_params=_cp("parallel", "parallel"),
    )(up, up, up, up, w, b, w, b)


def _conv_ffn_bwd(up, dact, w, b):
    n = up.shape[0]
    nb = DFF // CT
    nl = n // CL

    def body(pg_ref, g_ref, ng_ref, pv_ref, v_ref, nv_ref, d_ref, dn_ref, wg_ref, bg_ref, wv_ref, bv_ref,
             dxg_ref, dxv_ref, dwg_ref, dwv_ref, dbg_ref, dbv_ref):
        i = pl.program_id(1)
        gf = jnp.concatenate([jnp.where(i > 0, pg_ref[...], 0.0), g_ref[...], ng_ref[...]], axis=0)
        vf = jnp.concatenate([jnp.where(i > 0, pv_ref[...], 0.0), v_ref[...], nv_ref[...]], axis=0)
        glag, vlag = _lagged(gf, KF, CL + 8), _lagged(vf, KF, CL + 8)
        de = jnp.concatenate([d_ref[...], jnp.where(i < nl - 1, dn_ref[...], 0.0)], axis=0).astype(F32)
        gc = _conv_pre(glag, wg_ref, bg_ref, KF)
        vc = _conv_pre(vlag, wv_ref, bv_ref, KF)
        sg = _sig(gc)
        dgc = de * vc * (sg * (1.0 + gc * (1.0 - sg)))
        dvc = de * (gc * sg)
        dxg_ref[...] = _conv_back(dgc, wg_ref, KF).astype(BF16)
        dxv_ref[...] = _conv_back(dvc, wv_ref, KF).astype(BF16)

        @pl.when(i == 0)
        def _():
            for r in (dwg_ref, dwv_ref, dbg_ref, dbv_ref):
                r[...] = jnp.zeros_like(r)

        for e, lag, dw_ref, db_ref in ((dgc, glag, dwg_ref, dbg_ref), (dvc, vlag, dwv_ref, dbv_ref)):
            ec = e[0:CL]
            for k in range(KF):
                dw_ref[k:k + 1, :] += jnp.sum(ec * lag[KF - 1 - k][0:CL], axis=0, keepdims=True)
            db_ref[...] += jnp.sum(ec, axis=0, keepdims=True)

    gp, gcur, gnx = _halo_specs(n, lambda j: j)
    vp, vcur, vnx = _halo_specs(n, lambda j: j + nb)
    wcol = lambda o: (pl.BlockSpec((KF, CT), lambda j, i: (0, j + o)), pl.BlockSpec((1, CT), lambda j, i: (0, j + o)))
    wg, bg = wcol(0)
    wv, bv = wcol(nb)
    dxs = pl.BlockSpec((CL, CT), lambda j, i: (i, j))
    outs = pl.pallas_call(
        body, name="conv_ffn_bwd", grid=(nb, nl),
        in_specs=[gp, gcur, gnx, vp, vcur, vnx, gcur, gnx, wg, bg, wv, bv],
        out_specs=[dxs, dxs, wg, wg, bg, bg],
        out_shape=[jax.ShapeDtypeStruct((n, DFF), BF16)] * 2 + [jax.ShapeDtypeStruct((KF, DFF), F32)] * 2
        + [jax.ShapeDtypeStruct((1, DFF), F32)] * 2,
        compiler_params=_cp("parallel", "arbitrary"),
    )(up, up, up, up, up, up, dact, dact, w, b, w, b)
    return [jnp.concatenate(outs[k:k + 2], axis=1) for k in (0, 2, 4)]


HL = 128


def _split3(x):
    hi = x.astype(BF16)
    r1 = x - hi.astype(F32)
    mid = r1.astype(BF16)
    return hi, mid, (r1 - mid.astype(F32)).astype(BF16)


def _dot3(x, m):
    hi, mid, lo = _split3(x)
    return _dot(hi, m) + _dot(mid, m) + _dot(lo, m)


def _tri():
    row = lax.broadcasted_iota(jnp.int32, (Q, Q), 0)
    col = lax.broadcasted_iota(jnp.int32, (Q, Q), 1)
    return row >= col, row <= col


def _ssd_prep(dtraw, hp, emat):
    n = dtraw.shape[0]

    def body(d_ref, hp_ref, e_ref, dt_ref, s_ref, st_ref, dte_ref, se_ref):
        lower, upper = _tri()
        dt = _softplus(d_ref[...] + hp_ref[0:1, :])
        da = dt * (-jnp.exp(hp_ref[1:2, :]))
        s = _dot(lower.astype(F32), da, prec=HI)
        dt_ref[...] = dt
        s_ref[...] = s
        st_ref[...] = _dot(da, upper.astype(F32), TN, prec=HI)
        e = e_ref[...]
        dte_ref[...] = _dot3(dt, e)
        se_ref[...] = _dot3(s, e)

    row = pl.BlockSpec((Q, HL), lambda c: (c, 0))
    wide = pl.BlockSpec((Q, DI), lambda c: (c, 0))
    return pl.pallas_call(
        body, name="ssd_prep", grid=(n // Q,),
        in_specs=[pl.BlockSpec((Q, HL), lambda c: (c, NMAIN // HL)), pl.BlockSpec((8, HL), lambda c: (0, 0)),
                  pl.BlockSpec((HL, DI), lambda c: (0, 0))],
        out_specs=[row, row, pl.BlockSpec((HL, Q), lambda c: (0, c)), wide, wide],
        out_shape=[jax.ShapeDtypeStruct((n, HL), F32)] * 2 + [jax.ShapeDtypeStruct((HL, n), F32)]
        + [jax.ShapeDtypeStruct((n, DI), F32)] * 2,
        compiler_params=_cp("parallel"),
    )(dtraw, hp, emat)


def _ssd_post(ds_e, ddt_e, tsum, dsh, dtraw, dt, hp, emat_t):
    n = dtraw.shape[0]

    def body(dse_ref, dde_ref, ts_ref, dsh_ref, d_ref, dt_ref, hp_ref, et_ref, draw_ref, ps_ref):
        _, upper = _tri()
        et = et_ref[...]
        a = -jnp.exp(hp_ref[1:2, :])
        rows = lax.broadcasted_iota(jnp.int32, (Q, HL), 0)
        ds_t = _dot3(jnp.broadcast_to(ts_ref[...], (8, DI)), et)[0:1, :]
        ds = _dot3(dse_ref[...], et) + dsh_ref[...] + jnp.where(rows == Q - 1, ds_t, 0.0)
        d_a = _dot(upper.astype(F32), ds, prec=HI)
        draw = (_dot3(dde_ref[...], et) + d_a * a) * _sig(d_ref[...] + hp_ref[0:1, :])
        draw_ref[...] = draw

        @pl.when(pl.program_id(0) == 0)
        def _():
            ps_ref[...] = jnp.zeros_like(ps_ref)

        ps_ref[0:1, :] += jnp.sum(draw, axis=0, keepdims=True)
        ps_ref[1:2, :] += jnp.sum(d_a * dt_ref[...], axis=0, keepdims=True) * a

    row = pl.BlockSpec((Q, HL), lambda c: (c, 0))
    wide = pl.BlockSpec((Q, DI), lambda c: (c, 0))
    small = pl.BlockSpec((8, HL), lambda c: (0, 0))
    return pl.pallas_call(
        body, name="ssd_post", grid=(n // Q,),
        in_specs=[wide, wide, pl.BlockSpec((None, 1, DI), lambda c: (c, 0, 0)), row,
                  pl.BlockSpec((Q, HL), lambda c: (c, NMAIN // HL)), row, small,
                  pl.BlockSpec((DI, HL), lambda c: (0, 0))],
        out_specs=[row, small],
        out_shape=[jax.ShapeDtypeStruct((n, HL), F32), jax.ShapeDtypeStruct((8, HL), F32)],
        compiler_params=_cp("arbitrary"),
    )(ds_e, ddt_e, tsum, dsh, dtraw, dt, hp, emat_t)


def _ssd_specs(nc, rev):
    cc = (lambda c: nc - 1 - c) if rev else (lambda c: c)
    return [
        pl.BlockSpec((Q, GW), lambda g, c: (cc(c), g)),
        pl.BlockSpec((Q, NS), lambda g, c: (cc(c), DI // NS + g)),
        pl.BlockSpec((Q, NS), lambda g, c: (cc(c), (DI + NG * NS) // NS + g)),
        pl.BlockSpec((None, Q, 8), lambda g, c: (g, cc(c), 0)),
        pl.BlockSpec((8, Q), lambda g, c: (g, cc(c))),
        pl.BlockSpec((Q, GW), lambda g, c: (cc(c), g)),
        pl.BlockSpec((Q, GW), lambda g, c: (cc(c), g)),
        pl.BlockSpec((1, GW), lambda g, c: (0, g)),
    ]


def _ssd_fwd(xbc, s8, s_t, dt_e, s_e, dexp):
    n = xbc.shape[0]
    nc = n // Q

    def body(xs_ref, b_ref, c_ref, sc_ref, sr_ref, dte_ref, se_ref, dexp_ref, y_ref, sp_ref, st):
        @pl.when(pl.program_id(1) == 0)
        def _():
            st[...] = jnp.zeros_like(st)

        lower, _ = _tri()
        s_c, s_r, dt_e, s_e = sc_ref[...], sr_ref[...], dte_ref[...], se_ref[...]
        xs = xs_ref[...]
        x = xs * dt_e
        xb = x.astype(BF16)
        bb, cb = b_ref[...].astype(BF16), c_ref[...].astype(BF16)
        cbm = _dot(cb, bb, NT)
        st_e = s_e[Q - 1:Q, :]
        sprev = st[...]
        sp_ref[...] = sprev
        yoff = _dot(cb, sprev.astype(BF16)) * jnp.exp(s_e) + dexp_ref[...] * xs
        for h in range(HPG):
            sl = slice(h * HD, (h + 1) * HD)
            lm = jnp.where(lower, jnp.exp(jnp.minimum(s_c[:, h:h + 1] - s_r[h:h + 1, :], 0.0)), 0.0)
            y_ref[:, sl] = _dot((cbm * lm).astype(BF16), xb[:, sl]) + yoff[:, sl]
        w = (x * jnp.exp(st_e - s_e)).astype(BF16)
        st[...] = jnp.exp(st_e) * sprev + _dot(bb, w, TN)

    return pl.pallas_call(
        body, name="ssd_fwd", grid=(NG, nc), in_specs=_ssd_specs(nc, False),
        out_specs=[pl.BlockSpec((Q, GW), lambda g, c: (c, g)),
                   pl.BlockSpec((None, None, NS, GW), lambda g, c: (c, g, 0, 0))],
        out_shape=[jax.ShapeDtypeStruct((n, DI), F32), jax.ShapeDtypeStruct((nc, NG, NS, GW), F32)],
        scratch_shapes=[pltpu.VMEM((NS, GW), F32)],
        compiler_params=_cp("parallel", "arbitrary"),
    )(xbc, xbc, xbc, s8, s_t, dt_e, s_e, dexp)


def _ssd_bwd(xbc, s8, s_t, dt_e, s_e, dexp, sprev_all, dy):
    n = xbc.shape[0]
    nc = n // Q
    rc = lambda c: nc - 1 - c

    def body(xs_ref, b_ref, c_ref, sc_ref, sr_ref, dte_ref, se_ref, dexp_ref, sp_ref, dy_ref,
             dxs_ref, db_ref, dc_ref, dse_ref, dde_ref, ts_ref, dsh_ref, pd_ref, dst, dxbuf):
        @pl.when(pl.program_id(1) == 0)
        def _():
            dst[...] = jnp.zeros_like(dst)
            pd_ref[...] = jnp.zeros_like(pd_ref)

        lower, upper = _tri()
        s_c, s_r, dt_e, s_e = sc_ref[...], sr_ref[...], dte_ref[...], se_ref[...]
        xs = xs_ref[...]
        x = xs * dt_e
        xb = x.astype(BF16)
        bb, cb = b_ref[...].astype(BF16), c_ref[...].astype(BF16)
        cbm = _dot(cb, bb, NT)
        cbt = _dot(bb, cb, NT)
        st_e = s_e[Q - 1:Q, :]
        dec_out, dec_st, e_t = jnp.exp(s_e), jnp.exp(st_e - s_e), jnp.exp(st_e)
        dyv = dy_ref[...]
        dyb = dyv.astype(BF16)
        sprev = sp_ref[...]
        sb = sprev.astype(BF16)
        ds_in = dst[...]
        dsb = ds_in.astype(BF16)

        cs = _dot(cb, sb)
        dcs = (dyv * dec_out).astype(BF16)
        d_c = _dot(dcs, sb, NT)
        wf = x * dec_st
        d_w = _dot(bb, dsb)
        d_b = _dot(wf.astype(BF16), dsb, NT)
        tw = d_w * wf
        dse_ref[...] = dyv * cs * dec_out - tw
        ds_c = jnp.zeros((Q, 8), F32)
        dcb = jnp.zeros((Q, Q), F32)
        dcbt = jnp.zeros((Q, Q), F32)
        lane8 = lax.broadcasted_iota(jnp.int32, (1, 8), 1)
        for h in range(HPG):
            sl = slice(h * HD, (h + 1) * HD)
            sc_h, sr_h = s_c[:, h:h + 1], s_r[h:h + 1, :]
            lm = jnp.where(lower, jnp.exp(jnp.minimum(sc_h - sr_h, 0.0)), 0.0)
            lmt = jnp.where(upper, jnp.exp(jnp.minimum(sr_h - sc_h, 0.0)), 0.0)
            mt = cbt * lmt
            dm = _dot(dyb[:, sl], xb[:, sl], NT)
            dmt = _dot(xb[:, sl], dyb[:, sl], NT)
            dxbuf[:, sl] = _dot(mt.astype(BF16), dyb[:, sl])
            dml = dm * lm
            dmlt = dmt * lmt
            dcb = dcb + dml
            dcbt = dcbt + dmlt
            dsh = jnp.sum(dml * cbm, axis=1, keepdims=True) - jnp.sum(dmlt * cbt, axis=1, keepdims=True)
            ds_c = ds_c + dsh * (lane8 == h).astype(F32)
        d_c = d_c + _dot(dcb.astype(BF16), bb)
        d_b = d_b + _dot(dcbt.astype(BF16), cb)
        dx = d_w * dec_st + dxbuf[...]
        ts_ref[...] = jnp.sum(tw, axis=0, keepdims=True) + jnp.sum(ds_in * sprev, axis=0, keepdims=True) * e_t
        dsh_ref[...] = ds_c
        dde_ref[...] = dx * xs
        pd_ref[...] += jnp.sum(dyv * xs, axis=0, keepdims=True)
        dxs_ref[...] = dx * dt_e + dyv * dexp_ref[...]
        db_ref[...] = d_b
        dc_ref[...] = d_c
        dst[...] = e_t * ds_in + _dot(cb, dcs, TN)

    wide = pl.BlockSpec((Q, GW), lambda g, c: (rc(c), g))
    state = pl.BlockSpec((Q, NS), lambda g, c: (rc(c), g))
    in_specs = _ssd_specs(nc, True) + [pl.BlockSpec((None, None, NS, GW), lambda g, c: (rc(c), g, 0, 0)), wide]
    return pl.pallas_call(
        body, name="ssd_bwd", grid=(NG, nc), in_specs=in_specs,
        out_specs=[wide, state, state, wide, wide,
                   pl.BlockSpec((None, 1, GW), lambda g, c: (rc(c), 0, g)),
                   pl.BlockSpec((None, Q, 8), lambda g, c: (g, rc(c), 0)),
                   pl.BlockSpec((None, 1, GW), lambda g, c: (g, 0, 0))],
        out_shape=[jax.ShapeDtypeStruct((n, DI), F32), jax.ShapeDtypeStruct((n, NG * NS), F32),
                   jax.ShapeDtypeStruct((n, NG * NS), F32), jax.ShapeDtypeStruct((n, DI), F32),
                   jax.ShapeDtypeStruct((n, DI), F32), jax.ShapeDtypeStruct((nc, 1, DI), F32),
                   jax.ShapeDtypeStruct((NG, n, 8), F32), jax.ShapeDtypeStruct((NG, 1, GW), F32)],
        scratch_shapes=[pltpu.VMEM((NS, GW), F32), pltpu.VMEM((Q, GW), F32)],
        compiler_params=_cp("parallel", "arbitrary"),
    )(xbc, xbc, xbc, s8, s_t, dt_e, s_e, dexp, sprev_all, dy)


GL = 128


def _gnorm_fwd(y, proj, w):
    n = y.shape[0]

    def body(y_ref, z_ref, w_ref, o_ref):
        for g in range(NG):
            sl = slice(g * GW, (g + 1) * GW)
            yz = y_ref[:, sl] * _silu(z_ref[:, sl])
            r = lax.rsqrt(jnp.mean(yz * yz, axis=-1, keepdims=True) + EPS)
            o_ref[:, sl] = (yz * r * w_ref[:, sl]).astype(BF16)

    row = pl.BlockSpec((GL, DI), lambda i: (i, 0))
    return pl.pallas_call(
        body, name="gnorm_fwd", grid=(n // GL,),
        in_specs=[row, row, pl.BlockSpec((1, DI), lambda i: (0, 0))], out_specs=row,
        out_shape=jax.ShapeDtypeStruct((n, DI), BF16), compiler_params=_cp("parallel"),
    )(y, proj, w)


def _gnorm_bwd(dyn, y, proj, w, after):
    n = y.shape[0]

    def body(d_ref, y_ref, z_ref, w_ref, _, dy_ref, dz_ref, gw_ref):
        @pl.when(pl.program_id(0) == 0)
        def _():
            gw_ref[...] = jnp.zeros_like(gw_ref)

        for g in range(NG):
            sl = slice(g * GW, (g + 1) * GW)
            yv, zv, dv = y_ref[:, sl], z_ref[:, sl], d_ref[:, sl]
            sz = _silu(zv)
            yz = yv * sz
            r = lax.rsqrt(jnp.mean(yz * yz, axis=-1, keepdims=True) + EPS)
            yh = yz * r
            gg = dv * w_ref[:, sl]
            dyz = r * (gg - yh * jnp.mean(gg * yh, axis=-1, keepdims=True))
            gw_ref[:, sl] += jnp.sum(dv * yh, axis=0, keepdims=True)
            dy_ref[:, sl] = dyz * sz
            dz_ref[:, sl] = (dyz * yv * _dsilu(zv)).astype(BF16)

    row = pl.BlockSpec((GL, DI), lambda i: (i, 0))
    vec = pl.BlockSpec((1, DI), lambda i: (0, 0))
    return pl.pallas_call(
        body, name="gnorm_bwd", grid=(n // GL,),
        in_specs=[row, row, row, vec, ANY], out_specs=[row, row, vec],
        out_shape=[jax.ShapeDtypeStruct((n, DI), F32), jax.ShapeDtypeStruct((n, DI), BF16),
                   jax.ShapeDtypeStruct((1, DI), F32)],
        compiler_params=_cp("arbitrary"),
    )(dyn, y, proj, w, after)


SL = 512
SB = 8
SCB = NCH // SB


def _s5_in(proj, bre, bim, after=None):
    n = proj.shape[0]
    uoff = OFF_U // 128

    def body(u_ref, br_ref, bi_ref, *rest):
        or_ref, oi_ref = rest[-2:]
        u = u_ref[...].astype(BF16)
        or_ref[...] = _dot(u, br_ref[...])
        oi_ref[...] = _dot(u, bi_ref[...])

    extra = [] if after is None else [after]
    blk = pl.BlockSpec((None, 128, SCB), lambda i, j: (j, 0, 0))
    out = pl.BlockSpec((SL, SCB), lambda i, j: (i, j))
    return pl.pallas_call(
        body, name="s5_in", grid=(n // SL, SB),
        in_specs=[pl.BlockSpec((SL, 128), lambda i, j: (i, uoff + j)), blk, blk] + [ANY] * len(extra),
        out_specs=[out, out],
        out_shape=[jax.ShapeDtypeStruct((n, NCH), F32)] * 2, compiler_params=_cp("parallel", "parallel"),
    )(proj, bre, bim, *extra)


SC = 256


def _s5_scan(vre, vim, tab, reverse, name):
    n = vre.shape[0]
    nl = n // SL
    ng = SL // 8
    ti = (lambda i: nl - 1 - i) if reverse else (lambda i: i)

    def body(re_ref, im_ref, tab_ref, ore_ref, oim_ref, cre, cim):
        @pl.when(pl.program_id(1) == 0)
        def _():
            cre[...] = jnp.zeros_like(cre)
            cim[...] = jnp.zeros_like(cim)

        def step(j, carry):
            cr, ci = carry
            jj = (ng - 1 - j) if reverse else j
            rows = pl.ds(pl.multiple_of(jj * 8, 8), 8)
            vr, vi = re_ref[rows, :], im_ref[rows, :]
            for t, k in enumerate((1, 2, 4)):
                sh = (8 - k) if reverse else k
                rr, ri = pltpu.roll(vr, sh, 0), pltpu.roll(vi, sh, 0)
                pr, pi = tab_ref[2 * t], tab_ref[2 * t + 1]
                vr, vi = vr + pr * rr - pi * ri, vi + pr * ri + pi * rr
            lr, li = tab_ref[6], tab_ref[7]
            vr, vi = vr + lr * cr - li * ci, vi + lr * ci + li * cr
            ore_ref[rows, :] = vr
            oim_ref[rows, :] = vi
            e = 0 if reverse else 7
            return (jnp.broadcast_to(vr[e:e + 1, :], (8, SC)), jnp.broadcast_to(vi[e:e + 1, :], (8, SC)))

        cr, ci = lax.fori_loop(0, ng, step, (cre[...], cim[...]))
        cre[...] = cr
        cim[...] = ci

    blk = pl.BlockSpec((SL, SC), lambda j, i: (ti(i), j))
    return pl.pallas_call(
        body, name=name, grid=(NCH // SC, nl),
        in_specs=[blk, blk, pl.BlockSpec((8, 8, SC), lambda j, i: (0, 0, j))], out_specs=[blk, blk],
        out_shape=[jax.ShapeDtypeStruct((n, NCH), F32)] * 2,
        scratch_shapes=[pltpu.VMEM((8, SC), F32), pltpu.VMEM((8, SC), F32)],
        compiler_params=_cp("parallel", "arbitrary"),
    )(vre, vim, tab)


def _s5_out(xre, xim, cre, cimn, proj, dvec):
    n = xre.shape[0]
    uoff = OFF_U // 128

    def body(xr_ref, xi_ref, cr_ref, ci_ref, u_ref, d_ref, y_ref, g_ref):
        y = (_dot(xr_ref[...].astype(BF16), cr_ref[...]) + _dot(xi_ref[...].astype(BF16), ci_ref[...])
             + d_ref[...] * u_ref[...])
        y_ref[...] = y
        g_ref[...] = _gelu(y).astype(BF16)

    xs = pl.BlockSpec((SL, SCB), lambda i, j: (i, j))
    blk = pl.BlockSpec((None, SCB, 128), lambda i, j: (j, 0, 0))
    out = pl.BlockSpec((SL, 128), lambda i, j: (i, j))
    return pl.pallas_call(
        body, name="s5_out", grid=(n // SL, SB),
        in_specs=[xs, xs, blk, blk, pl.BlockSpec((SL, 128), lambda i, j: (i, uoff + j)),
                  pl.BlockSpec((1, 128), lambda i, j: (0, j))],
        out_specs=[out, out],
        out_shape=[jax.ShapeDtypeStruct((n, DS5), F32), jax.ShapeDtypeStruct((n, DS5), BF16)],
        compiler_params=_cp("parallel", "parallel"),
    )(xre, xim, cre, cimn, proj, dvec)


def _s5_out_bwd(dg, ypre, crt, cimnt, proj, dvec, xre, xim):
    n = dg.shape[0]
    uoff = OFF_U // 128
    nl = n // SL

    def body(dg_ref, y_ref, cr_ref, ci_ref, u_ref, d_ref, xr_ref, xi_ref,
             gr_ref, gi_ref, dus_ref, gcr_ref, gci_ref, gd_ref):
        dy = dg_ref[...] * _dgelu(y_ref[...])
        dyb = dy.astype(BF16)
        gr_ref[...] = _dot(dyb, cr_ref[...])
        gi_ref[...] = _dot(dyb, ci_ref[...])
        dus_ref[...] = dy * d_ref[...]

        @pl.when(pl.program_id(1) == 0)
        def _():
            gcr_ref[...] = jnp.zeros_like(gcr_ref)
            gci_ref[...] = jnp.zeros_like(gci_ref)
            gd_ref[...] = jnp.zeros_like(gd_ref)

        gcr_ref[...] += _dot(xr_ref[...].astype(BF16), dyb, TN)
        gci_ref[...] -= _dot(xi_ref[...].astype(BF16), dyb, TN)
        gd_ref[...] += jnp.sum(dy * u_ref[...], axis=0, keepdims=True)

    u128 = pl.BlockSpec((SL, 128), lambda j, i: (i, j))
    xs = pl.BlockSpec((SL, SCB), lambda j, i: (i, j))
    blk = pl.BlockSpec((None, 128, SCB), lambda j, i: (j, 0, 0))
    gblk = pl.BlockSpec((None, SCB, 128), lambda j, i: (j, 0, 0))
    vec = pl.BlockSpec((1, 128), lambda j, i: (0, j))
    return pl.pallas_call(
        body, name="s5_out_bwd", grid=(SB, nl),
        in_specs=[u128, u128, blk, blk, pl.BlockSpec((SL, 128), lambda j, i: (i, uoff + j)), vec, xs, xs],
        out_specs=[xs, xs, u128, gblk, gblk, vec],
        out_shape=[jax.ShapeDtypeStruct((n, NCH), F32)] * 2 + [jax.ShapeDtypeStruct((n, DS5), F32)]
        + [jax.ShapeDtypeStruct((SB, SCB, 128), F32)] * 2 + [jax.ShapeDtypeStruct((1, DS5), F32)],
        compiler_params=_cp("parallel", "arbitrary"),
    )(dg, ypre, crt, cimnt, proj, dvec, xre, xim)


def _s5_in_bwd(are, aim, brt, bit, proj, dus, xre, xim):
    n = are.shape[0]
    uoff = OFF_U // 128
    per = SL // 8

    def body(ar_ref, ai_ref, br_ref, bi_ref, u_ref, dus_ref, xr_ref, xi_ref, pr_ref, pi_ref,
             du_ref, gbr_ref, gbi_ref, glr_ref, gli_ref):
        i = pl.program_id(1)
        ar, ai = ar_ref[...], ai_ref[...]
        arb, aib = ar.astype(BF16), ai.astype(BF16)
        du_ref[...] = (_dot(arb, br_ref[...]) + _dot(aib, bi_ref[...]) + dus_ref[...]).astype(BF16)

        @pl.when(i == 0)
        def _():
            for r in (gbr_ref, gbi_ref, glr_ref, gli_ref):
                r[...] = jnp.zeros_like(r)

        ub = u_ref[...].astype(BF16)
        gbr_ref[...] += _dot(arb, ub, TN)
        gbi_ref[...] += _dot(aib, ub, TN)
        row0 = lax.broadcasted_iota(jnp.int32, (SL, SCB), 0) == 0
        last_r = jnp.where(i > 0, pr_ref[7:8, :], 0.0)
        last_i = jnp.where(i > 0, pi_ref[7:8, :], 0.0)
        xpr = jnp.where(row0, last_r, pltpu.roll(xr_ref[...], 1, 0))
        xpi = jnp.where(row0, last_i, pltpu.roll(xi_ref[...], 1, 0))
        glr_ref[...] += jnp.sum(ar * xpr + ai * xpi, axis=0, keepdims=True)
        gli_ref[...] += jnp.sum(ai * xpr - ar * xpi, axis=0, keepdims=True)

    xs = pl.BlockSpec((SL, SCB), lambda j, i: (i, j))
    prev = pl.BlockSpec((8, SCB), lambda j, i: (jnp.maximum(i * per - 1, 0), j))
    blk = pl.BlockSpec((None, SCB, 128), lambda j, i: (j, 0, 0))
    u128 = pl.BlockSpec((SL, 128), lambda j, i: (i, j))
    vec = pl.BlockSpec((1, SCB), lambda j, i: (0, j))
    return pl.pallas_call(
        body, name="s5_in_bwd", grid=(SB, n // SL),
        in_specs=[xs, xs, blk, blk, pl.BlockSpec((SL, 128), lambda j, i: (i, uoff + j)), u128, xs, xs, prev, prev],
        out_specs=[u128, blk, blk, vec, vec],
        out_shape=[jax.ShapeDtypeStruct((n, DS5), BF16)] + [jax.ShapeDtypeStruct((SB, SCB, 128), F32)] * 2
        + [jax.ShapeDtypeStruct((1, NCH), F32)] * 2,
        compiler_params=_cp("parallel", "arbitrary"),
    )(are, aim, brt, bit, proj, dus, xre, xim, xre, xim)


MC = 1024


def _merge_specs():
    ga = pl.BlockSpec((TL, MC), lambda i, j: (i, OFF_GA // MC + j))
    gb = pl.BlockSpec((TL, MC), lambda i, j: (i, OFF_GB // MC + j))
    col = pl.BlockSpec((TL, MC), lambda i, j: (i, j))
    gate = pl.BlockSpec((TL, MC), lambda i, j: (i, D // MC + j))
    return ga, gb, col, gate


def _merge_fwd(proj, ya, vg):
    n = ya.shape[0]

    def body(ga_ref, gb_ref, ya_ref, v_ref, g_ref, o_ref):
        yb = v_ref[...] * _sig(g_ref[...])
        o_ref[...] = (_sig(ga_ref[...]) * ya_ref[...] + _sig(gb_ref[...]) * yb).astype(BF16)

    ga, gb, col, gate = _merge_specs()
    return pl.pallas_call(
        body, name="merge_fwd", grid=(n // TL, D // MC), in_specs=[ga, gb, col, col, gate], out_specs=col,
        out_shape=jax.ShapeDtypeStruct((n, D), BF16), compiler_params=_cp("parallel", "parallel"),
    )(proj, proj, ya, vg, vg)


def _merge_bwd(dm, proj, ya, vg):
    n = ya.shape[0]

    def body(dm_ref, ga_ref, gb_ref, ya_ref, v_ref, g_ref, dga_ref, dgb_ref, dya_ref, dv_ref, dg_ref):
        d = dm_ref[...]
        sa, sb, sg = _sig(ga_ref[...]), _sig(gb_ref[...]), _sig(g_ref[...])
        v = v_ref[...]
        yb = v * sg
        dga_ref[...] = (d * ya_ref[...] * sa * (1.0 - sa)).astype(BF16)
        dgb_ref[...] = (d * yb * sb * (1.0 - sb)).astype(BF16)
        dya_ref[...] = (d * sa).astype(BF16)
        dyb = d * sb
        dv_ref[...] = (dyb * sg).astype(BF16)
        dg_ref[...] = (dyb * v * sg * (1.0 - sg)).astype(BF16)

    ga, gb, col, gate = _merge_specs()
    o = jax.ShapeDtypeStruct((n, D), BF16)
    return pl.pallas_call(
        body, name="merge_bwd", grid=(n // TL, D // MC), in_specs=[col, ga, gb, col, col, gate],
        out_specs=[col] * 5, out_shape=[o] * 5, compiler_params=_cp("parallel", "parallel"),
    )(dm, proj, proj, ya, vg, vg)


def _adamw_update(wv, gv, mv, vv):
    nm = B1 * mv + (1.0 - B1) * gv
    nv = B2 * vv + (1.0 - B2) * (gv * gv)
    m_hat = nm / (1.0 - B1 ** STEP)
    v_hat = nv / (1.0 - B2 ** STEP)
    return -LR * (m_hat / (jnp.sqrt(v_hat) + AEPS) + WD * wv), nm, nv


def _adamw(w, g, m, v, name):
    r, c = w.shape
    tr = _pick(r, 128)

    def body(w_ref, g_ref, m_ref, v_ref, d_ref, nm_ref, nv_ref):
        d_ref[...], nm_ref[...], nv_ref[...] = _adamw_update(w_ref[...], g_ref[...], m_ref[...], v_ref[...])

    blk = pl.BlockSpec((tr, c), lambda i: (i, 0))
    o = jax.ShapeDtypeStruct((r, c), F32)
    return pl.pallas_call(
        body, name=name, grid=(r // tr,), in_specs=[blk] * 4, out_specs=[blk] * 3, out_shape=[o] * 3,
        compiler_params=_cp("parallel"),
    )(w, g, m, v)


def _adamw_halves(w, g_mine, g_other, m, v, cidx, name):
    _, r, c = w.shape
    hr, gc = g_mine.shape
    tr = _pick(hr, 128)
    nbh = hr // tr
    assert gc == c and 2 * hr - tr < r <= 2 * hr

    def body(cs, w_ref, gm_ref, go_ref, m_ref, v_ref, g_ref, d_ref, nm_ref, nv_ref):
        mine = pl.program_id(0) // nbh == cs[0]
        gv = jnp.where(mine, gm_ref[...], go_ref[...])
        g_ref[...] = gv
        d_ref[...], nm_ref[...], nv_ref[...] = _adamw_update(w_ref[...], gv, m_ref[...], v_ref[...])

    blk = pl.BlockSpec((None, tr, c), lambda i, cs: (0, i, 0))
    gmine = pl.BlockSpec((tr, gc), lambda i, cs: (jnp.where(i // nbh == cs[0], i % nbh, 0), 0))
    gother = pl.BlockSpec((tr, gc), lambda i, cs: (jnp.where(i // nbh == cs[0], 0, i % nbh), 0))
    o = jax.ShapeDtypeStruct((1, r, c), F32)
    return pl.pallas_call(
        body, name=name,
        grid_spec=pltpu.PrefetchScalarGridSpec(num_scalar_prefetch=1, grid=(2 * nbh,),
                                               in_specs=[blk, gmine, gother, blk, blk], out_specs=[blk] * 4),
        out_shape=[o] * 4, compiler_params=_cp("parallel"),
    )(cidx, w, g_mine, g_other, m, v)


def _chip_sum(part, sib, cidx, name):
    _, r, cc = part.shape
    hr = r // 2
    tr = _pick(hr, 256)

    def body(cs, p_ref, s_ref, o_ref):
        o_ref[...] = (p_ref[...].astype(F32) + s_ref[...].astype(F32)).astype(BF16)

    blk = pl.BlockSpec((None, tr, cc), lambda k, i, cs: (k, i, 0))
    return pl.pallas_call(
        body, name=name,
        grid_spec=pltpu.PrefetchScalarGridSpec(
            num_scalar_prefetch=1, grid=(4, hr // tr),
            in_specs=[pl.BlockSpec((None, None, tr, cc), lambda k, i, cs: (k, cs[0], i, 0)), blk], out_specs=blk),
        out_shape=jax.ShapeDtypeStruct((4, hr, cc), BF16), compiler_params=_cp("parallel", "parallel"),
    )(cidx, part.reshape(4, 2, hr, cc), sib)


def _shard_sum(own, got, sidx, name):
    _, hr, cc = own.shape
    tr = _pick(hr, 256)

    def body(cs, own_ref, g0, g1, g2, g3, o_ref):
        acc = None
        for k, g_ref in enumerate((g0, g1, g2, g3)):
            term = jnp.where(cs[0] == k, own_ref[...], g_ref[...]).astype(F32)
            acc = term if acc is None else acc + term
        o_ref[...] = acc

    def got_spec(k):
        return pl.BlockSpec((None, tr, cc), lambda i, cs: (jnp.where(cs[0] == k, (k + 1) % 4, k), i, 0))

    return pl.pallas_call(
        body, name=name,
        grid_spec=pltpu.PrefetchScalarGridSpec(
            num_scalar_prefetch=1, grid=(hr // tr,),
            in_specs=[pl.BlockSpec((None, tr, cc), lambda i, cs: (cs[0], i, 0))] + [got_spec(k) for k in range(4)],
            out_specs=pl.BlockSpec((tr, cc), lambda i, cs: (i, 0))),
        out_shape=jax.ShapeDtypeStruct((hr, cc), F32), compiler_params=_cp("parallel"),
    )(sidx, own, got, got, got, got)


def _sum_slabs(xs, name, out_dtype=F32):
    r, c = xs[0].shape
    tr = _pick(r, 256)

    def body(*refs):
        acc = refs[0][...].astype(F32)
        for ref in refs[1:-1]:
            acc = acc + ref[...].astype(F32)
        refs[-1][...] = acc.astype(out_dtype)

    blk = pl.BlockSpec((tr, c), lambda i: (i, 0))
    return pl.pallas_call(
        body, name=name, grid=(r // tr,), in_specs=[blk] * len(xs), out_specs=blk,
        out_shape=jax.ShapeDtypeStruct((r, c), out_dtype), compiler_params=_cp("parallel"),
    )(*xs)


def _place():
    return lax.axis_index("x"), lax.axis_index("y"), lax.axis_index("c")


def _gather_small(v, after):
    m_per, n = v.shape

    def body(x_ref, _, out_ref, send_sems, recv_sems, local_sem):
        x, y, c = _place()
        me, sibling = (x, y, c), (x, y, 1 - c)
        chips = [(1 - x, y), (x, 1 - y), (1 - x, 1 - y)]

        def rows(px, py, pc):
            return out_ref.at[pl.ds((4 * px + 2 * py + pc) * m_per, m_per), :]

        def copy(k, block, to, src=None):
            return pltpu.make_async_remote_copy(
                src_ref=rows(*block) if src is None else src, dst_ref=rows(*block),
                send_sem=send_sems.at[k], recv_sem=recv_sems.at[k], device_id=to, device_id_type=MESH)

        mine = pltpu.make_async_copy(x_ref, rows(*me), local_sem)
        mine.start()
        first = [copy(0, me, sibling, src=x_ref)]
        first += [copy(1 + j, me, (*chip, c), src=x_ref) for j, chip in enumerate(chips)]
        for cp in first:
            cp.start()
        passed = [copy(4 + j, (*chip, c), sibling) for j, chip in enumerate(chips)]
        for j, chip in enumerate(chips):
            copy(1 + j, (*chip, c), me).wait_recv()
            passed[j].start()
        copy(0, sibling, me).wait_recv()
        for j, chip in enumerate(chips):
            copy(4 + j, (*chip, 1 - c), me).wait_recv()
        for cp in first + passed:
            cp.wait_send()
        mine.wait()

    return pl.pallas_call(
        body, name="gather_small_%d" % m_per,
        out_shape=jax.ShapeDtypeStruct((8 * m_per, n), v.dtype),
        in_specs=[pl.BlockSpec(memory_space=pltpu.VMEM), ANY], out_specs=pl.BlockSpec(memory_space=pltpu.VMEM),
        scratch_shapes=[pltpu.SemaphoreType.DMA((7,)), pltpu.SemaphoreType.DMA((7,)), pltpu.SemaphoreType.DMA],
        compiler_params=pltpu.CompilerParams(vmem_limit_bytes=VMEM_LIMIT),
    )(v, after)


def _allsum_small(v, name, after):
    r = v.shape[0]
    g = _gather_small(v, after)
    return _sum_slabs([g[k * r:(k + 1) * r] for k in range(8)], name)


def _gather_big(shards):
    nt = len(shards)

    def body(*refs):
        ins, outs = refs[:nt], refs[nt:2 * nt]
        send_sems, recv_sems = refs[2 * nt:]
        x, y, c = _place()
        s = 2 * x + y
        sibling = (x, y, 1 - c)
        chips = [(1 - x, y), (x, 1 - y), (1 - x, 1 - y)]

        def half(t, slot, h):
            hr = ins[t].shape[0] // 2
            return outs[t].at[slot, pl.ds(h * hr, hr), :]

        def ici(t, j, src_slot, to):
            hr = ins[t].shape[0] // 2
            return pltpu.make_async_remote_copy(
                src_ref=ins[t].at[pl.ds(c * hr, hr), :], dst_ref=half(t, src_slot, c),
                send_sem=send_sems.at[7 * t + j], recv_sem=recv_sems.at[7 * t + j], device_id=to, device_id_type=MESH)

        def d2d(t, j, slot, h):
            return pltpu.make_async_remote_copy(
                src_ref=half(t, slot, h), dst_ref=half(t, slot, h),
                send_sem=send_sems.at[7 * t + 3 + j], recv_sem=recv_sems.at[7 * t + 3 + j],
                device_id=sibling, device_id_type=MESH)

        def whole(t):
            return pltpu.make_async_remote_copy(
                src_ref=ins[t], dst_ref=outs[t].at[s], send_sem=send_sems.at[7 * t + 6],
                recv_sem=recv_sems.at[7 * t + 6], device_id=sibling, device_id_type=MESH)

        sends = [ici(t, j, s, (*chip, c)) for t in range(nt) for j, chip in enumerate(chips)]
        sends += [whole(t) for t in range(nt)]
        for cp in sends:
            cp.start()
        passed = []
        for t in range(nt):
            for j, (px, py) in enumerate(chips):
                ici(t, j, 2 * px + py, (x, y, c)).wait_recv()
                cp = d2d(t, j, 2 * px + py, c)
                cp.start()
                passed.append(cp)
        for t in range(nt):
            for j, (px, py) in enumerate(chips):
                d2d(t, j, 2 * px + py, 1 - c).wait_recv()
            whole(t).wait_recv()
        for cp in sends + passed:
            cp.wait_send()

    return pl.pallas_call(
        body, name="gather_big",
        out_shape=[jax.ShapeDtypeStruct((4,) + a.shape, a.dtype) for a in shards],
        in_specs=[ANY] * nt, out_specs=[ANY] * nt,
        scratch_shapes=[pltpu.SemaphoreType.DMA((7 * nt,)), pltpu.SemaphoreType.DMA((7 * nt,))],
    )(*shards)


def _swap_halves(parts, name):
    nt = len(parts)

    def body(*refs):
        ins, outs = refs[:nt], refs[nt:2 * nt]
        send_sems, recv_sems = refs[2 * nt:]
        x, y, c = _place()
        cps = []
        for t in range(nt):
            hr = ins[t].shape[1] // 2
            cps.append(pltpu.make_async_remote_copy(
                src_ref=ins[t].at[:, pl.ds((1 - c) * hr, hr), :], dst_ref=outs[t],
                send_sem=send_sems.at[t], recv_sem=recv_sems.at[t], device_id=(x, y, 1 - c), device_id_type=MESH))
        for cp in cps:
            cp.start()
        for cp in cps:
            cp.wait()

    return pl.pallas_call(
        body, name=name,
        out_shape=[jax.ShapeDtypeStruct((4, a.shape[1] // 2, a.shape[2]), a.dtype) for a in parts],
        in_specs=[ANY] * nt, out_specs=[ANY] * nt,
        scratch_shapes=[pltpu.SemaphoreType.DMA((nt,)), pltpu.SemaphoreType.DMA((nt,))],
    )(*parts)


def _scatter_chips(parts):
    nt = len(parts)

    def body(*refs):
        ins, outs = refs[:nt], refs[nt:2 * nt]
        send_sems, recv_sems = refs[2 * nt:]
        x, y, c = _place()
        s = 2 * x + y
        chips = [(1 - x, y), (x, 1 - y), (1 - x, 1 - y)]
        cps = []
        for t in range(nt):
            for j, (px, py) in enumerate(chips):
                cps.append(pltpu.make_async_remote_copy(
                    src_ref=ins[t].at[2 * px + py], dst_ref=outs[t].at[s],
                    send_sem=send_sems.at[3 * t + j], recv_sem=recv_sems.at[3 * t + j],
                    device_id=(px, py, c), device_id_type=MESH))
        for cp in cps:
            cp.start()
        for t in range(nt):
            for j, (px, py) in enumerate(chips):
                pltpu.make_async_remote_copy(
                    src_ref=ins[t].at[s], dst_ref=outs[t].at[2 * px + py],
                    send_sem=send_sems.at[3 * t + j], recv_sem=recv_sems.at[3 * t + j],
                    device_id=(px, py, c), device_id_type=MESH).wait_recv()
        for cp in cps:
            cp.wait_send()

    return pl.pallas_call(
        body, name="scatter_chips",
        out_shape=[jax.ShapeDtypeStruct(a.shape, a.dtype) for a in parts],
        in_specs=[ANY] * nt, out_specs=[ANY] * nt,
        scratch_shapes=[pltpu.SemaphoreType.DMA((3 * nt,)), pltpu.SemaphoreType.DMA((3 * nt,))],
    )(*parts)


def _swap_whole(halves):
    nt = len(halves)

    def body(*refs):
        ins, outs = refs[:nt], refs[nt:2 * nt]
        send_sems, recv_sems = refs[2 * nt:]
        x, y, c = _place()
        cps = [pltpu.make_async_remote_copy(
            src_ref=ins[t], dst_ref=outs[t], send_sem=send_sems.at[t], recv_sem=recv_sems.at[t],
            device_id=(x, y, 1 - c), device_id_type=MESH) for t in range(nt)]
        for cp in cps:
            cp.start()
        for cp in cps:
            cp.wait()

    return pl.pallas_call(
        body, name="swap_whole",
        out_shape=[jax.ShapeDtypeStruct(a.shape, a.dtype) for a in halves],
        in_specs=[ANY] * nt, out_specs=[ANY] * nt,
        scratch_shapes=[pltpu.SemaphoreType.DMA((nt,)), pltpu.SemaphoreType.DMA((nt,))],
    )(*halves)


def _pass_halves(got, shards, name):
    nt = len(got)

    def body(*refs):
        ins, own, outs = refs[:nt], refs[nt:2 * nt], refs[2 * nt:3 * nt]
        send_sems, recv_sems = refs[3 * nt:]
        x, y, c = _place()
        s = 2 * x + y
        chips = [(1 - x, y), (x, 1 - y), (1 - x, 1 - y)]

        def half(ref, t, slot, h):
            hr = ins[t].shape[1] // 2
            return ref.at[slot, pl.ds(h * hr, hr), :]

        def copy(t, j, h):
            px, py = chips[j]
            return pltpu.make_async_remote_copy(
                src_ref=half(ins[t], t, 2 * px + py, h), dst_ref=half(outs[t], t, 2 * px + py, h),
                send_sem=send_sems.at[4 * t + j], recv_sem=recv_sems.at[4 * t + j],
                device_id=(x, y, 1 - c), device_id_type=MESH)

        def whole(t):
            return pltpu.make_async_remote_copy(
                src_ref=own[t], dst_ref=outs[t].at[s], send_sem=send_sems.at[4 * t + 3],
                recv_sem=recv_sems.at[4 * t + 3], device_id=(x, y, 1 - c), device_id_type=MESH)

        sends = [copy(t, j, c) for t in range(nt) for j in range(3)] + [whole(t) for t in range(nt)]
        for cp in sends:
            cp.start()
        for t in range(nt):
            for j in range(3):
                copy(t, j, 1 - c).wait_recv()
            whole(t).wait_recv()
        for cp in sends:
            cp.wait_send()

    return pl.pallas_call(
        body, name=name,
        out_shape=[jax.ShapeDtypeStruct(a.shape, a.dtype) for a in got],
        in_specs=[ANY] * (2 * nt), out_specs=[ANY] * nt, input_output_aliases={t: t for t in range(nt)},
        scratch_shapes=[pltpu.SemaphoreType.DMA((4 * nt,)), pltpu.SemaphoreType.DMA((4 * nt,))],
    )(*got, *shards)


HBM = pl.BlockSpec(memory_space=pltpu.HBM)
SEM = pl.BlockSpec(memory_space=pltpu.SEMAPHORE)
EFFECT = pltpu.SideEffectType.DATAFLOW_SIDE_EFFECTING


PER_TENSOR = {"gather": 3, "scatter": 3, "swap": 1, "pass": 4, "whole": 1}


def _ici_copies(kind, srcs, lands, send_sems, recv_sems):
    x, y, c = _place()
    s = 2 * x + y
    sib = (x, y, 1 - c)
    chips = [(1 - x, y), (x, 1 - y), (1 - x, 1 - y)]
    cps = []

    def add(src, dst, dev):
        k = len(cps)
        cps.append(pltpu.make_async_remote_copy(src_ref=src, dst_ref=dst, send_sem=send_sems[k], recv_sem=recv_sems[k],
                                                device_id=dev, device_id_type=MESH))

    for t in range(len(srcs)):
        if kind == "gather":
            hr = srcs[t].shape[0] // 2
            for px, py in chips:
                add(srcs[t].at[pl.ds(c * hr, hr), :], lands[t].at[s, pl.ds(c * hr, hr), :], (px, py, c))
        elif kind == "scatter":
            for px, py in chips:
                add(srcs[t].at[2 * px + py], lands[t].at[s], (px, py, c))
        elif kind == "swap":
            hr = srcs[t].shape[1] // 2
            add(srcs[t].at[:, pl.ds((1 - c) * hr, hr), :], lands[t], sib)
        elif kind == "pass":
            hr = srcs[t].shape[1] // 2
            for px, py in chips:
                half = srcs[t].at[2 * px + py, pl.ds(c * hr, hr), :]
                add(half, half, sib)
            add(lands[t], srcs[t].at[s], sib)
        else:
            add(srcs[t], lands[t], sib)
    return cps


def _ici_start(kind, srcs, after, name, lands=None):
    nt = len(srcs)
    nc = PER_TENSOR[kind] * nt
    hbm = lambda a: pltpu.with_memory_space_constraint(a, pltpu.HBM)
    if lands is None:
        shape = {"gather": lambda a: (4,) + a.shape, "scatter": lambda a: a.shape,
                 "swap": lambda a: (4, a.shape[1] // 2, a.shape[2]), "whole": lambda a: a.shape}[kind]
        lands = [lax.empty(shape(a), a.dtype) for a in srcs]

    def body(*refs):
        src, land = refs[:nt], refs[nt:2 * nt]
        outs = refs[2 * nt + 1:]
        for cp in _ici_copies(kind, src, land, outs[:nc], outs[nc:2 * nc]):
            cp.start()
        outs[-1][...] = jnp.zeros_like(outs[-1])

    outs = pl.pallas_call(
        body, name=name,
        out_shape=tuple([pltpu.SemaphoreType.DMA(())] * (2 * nc) + [pltpu.HBM(a.shape, a.dtype) for a in srcs]
                        + [pltpu.HBM(a.shape, a.dtype) for a in lands] + [jax.ShapeDtypeStruct((8, 128), F32)]),
        in_specs=[HBM] * (2 * nt) + [ANY],
        out_specs=tuple([SEM] * (2 * nc) + [HBM] * (2 * nt) + [pl.BlockSpec(memory_space=pltpu.VMEM)]),
        input_output_aliases={i: 2 * nc + i for i in range(2 * nt)},
        compiler_params=pltpu.CompilerParams(has_side_effects=EFFECT),
    )(*[hbm(a) for a in srcs], *[hbm(a) for a in lands], after)
    return outs[:2 * nc], outs[2 * nc:2 * nc + nt], outs[2 * nc + nt:2 * nc + 2 * nt], outs[-1]


def _ici_wait(kind, sems, srcs, lands, after, name):
    nt = len(srcs)
    nc = PER_TENSOR[kind] * nt

    def body(*refs):
        src, land = refs[:nt], refs[nt:2 * nt]
        sem = refs[2 * nt:2 * nt + 2 * nc]
        for cp in _ici_copies(kind, src, land, sem[:nc], sem[nc:]):
            cp.wait_send()
            cp.wait_recv()

    outs = pl.pallas_call(
        body, name=name,
        out_shape=tuple(pltpu.HBM(a.shape, a.dtype) for a in list(srcs) + list(lands)),
        in_specs=[HBM] * (2 * nt) + [SEM] * (2 * nc) + [ANY], out_specs=tuple([HBM] * (2 * nt)),
        input_output_aliases={i: i for i in range(2 * nt)},
        compiler_params=pltpu.CompilerParams(has_side_effects=EFFECT),
    )(*srcs, *lands, *sems, after)
    return outs[:nt], outs[nt:]


def _s5_params(lam_re, lam_im, log_dt, b_re, b_im):
    lr = jnp.minimum(lam_re, EIG_MAX)
    dt = jnp.exp(log_dt)[:, None]
    mag = jnp.exp(lr * dt)
    lbr, lbi = mag * jnp.cos(lam_im * dt), mag * jnp.sin(lam_im * dt)
    den = lr * lr + lam_im * lam_im
    qr = ((lbr - 1.0) * lr + lbi * lam_im) / den
    qi = (lbi * lr - (lbr - 1.0) * lam_im) / den
    bbr = qr[..., None] * b_re - qi[..., None] * b_im
    bbi = qr[..., None] * b_im + qi[..., None] * b_re
    return lbr, lbi, bbr, bbi


def _cmul(a, b):
    return a[0] * b[0] - a[1] * b[1], a[0] * b[1] + a[1] * b[0]


def _scan_table(lr, li, reverse):
    l1 = (lr.reshape(1, NCH), li.reshape(1, NCH))
    pows = [l1]
    for _ in range(7):
        pows.append(_cmul(pows[-1], l1))
    r = jnp.arange(8)[:, None]
    tabs = []
    for k in (1, 2, 4):
        keep = (r < 8 - k) if reverse else (r >= k)
        tabs += [jnp.where(keep, pows[k - 1][0], 0.0), jnp.where(keep, pows[k - 1][1], 0.0)]
    order = range(7, -1, -1) if reverse else range(8)
    tabs += [jnp.concatenate([pows[e][0] for e in order], axis=0), jnp.concatenate([pows[e][1] for e in order], axis=0)]
    return jnp.stack(tabs).astype(F32)


_EYE8 = lambda: jnp.eye(8, dtype=F32)


def _to_in_blocks(b):
    return jnp.einsum("jgpc,gh->jgchp", b.reshape(8, 8, 64, 16), _EYE8()).reshape(8, 128, 512)


def _to_out_blocks(cm):
    return jnp.einsum("jgcp,gh->jgphc", cm.reshape(8, 8, 16, 64), _EYE8()).reshape(8, 512, 128)


def _from_out_blocks(g):
    return jnp.einsum("jgphc,gh->jgpc", g.reshape(8, 8, 64, 8, 16), _EYE8()).reshape(64, 64, 16)


def _local_step(x, target, hn1, p, token, hooks):
    n = x.shape[0]
    g = {}
    proj = _matmul(hn1, p["w_full"], "nt", "mm_in", tn=896, after=token)
    dtraw = proj
    xbc = _conv_a_fwd(proj, p["conv_a_w"], p["conv_a_b"])
    to_lanes = lambda v: jnp.pad(jnp.pad(v.reshape(NG, HPG), ((0, 0), (0, 8 - HPG))).reshape(1, 8 * NG),
                                 ((0, 0), (0, HL - 8 * NG)))
    from_lanes = lambda v: v[:, :8 * NG].reshape(-1, NG, 8)[:, :, :HPG].reshape(-1, NG * HPG)
    hp = jnp.concatenate([to_lanes(p["dt_bias"]), to_lanes(p["a_log"]), jnp.zeros((6, HL), F32)], axis=0)
    lane = jnp.arange(HL)[:, None]
    emat = ((lane < 8 * NG) & (lane % 8 < HPG)
            & (jnp.arange(DI)[None, :] // HD == HPG * (lane // 8) + lane % 8)).astype(BF16)
    dexp = jnp.repeat(p["d_a"].reshape(1, NG * HPG), HD, axis=1)
    dt, s_cum, s_t, dt_e, s_e = _ssd_prep(dtraw, hp, emat)
    s8 = s_cum[:, :8 * NG].reshape(n, NG, 8).transpose(1, 0, 2)
    yssd, sprev = _ssd_fwd(xbc, s8, s_t, dt_e, s_e, dexp)
    yn = _gnorm_fwd(yssd, proj, p["norm_a_w"])
    tok = hooks["late_start"](yn)
    (lbr, lbi, bbr, bbi), s5_vjp = jax.vjp(_s5_params, p["s5_lam_re"], p["s5_lam_im"], p["s5_log_dt"],
                                           p["s5_b_re"], p["s5_b_im"])
    bin_r, bin_i = _to_in_blocks(bbr), _to_in_blocks(bbi)
    cout_r, cout_in = _to_out_blocks(p["s5_c_re"]), _to_out_blocks(-p["s5_c_im"])
    bur, bui = _s5_in(proj, bin_r.astype(BF16), bin_i.astype(BF16), after=tok)
    xre, xim = _s5_scan(bur, bui, _scan_table(lbr, lbi, False), False, "s5_scan_fwd")
    ypre, g5 = _s5_out(xre, xim, cout_r.astype(BF16), cout_in.astype(BF16), proj, p["s5_d"])
    p = {**p, **hooks["late_weights"](ypre)}
    ya = _matmul(yn, p["w_proj_a"], "nn", "mm_proj")
    vg = _matmul(g5, p["w_s5_glu"], "nn", "mm_glu", b_stacked=True)
    merged = _merge_fwd(proj, ya, vg)
    h1 = _matmul(merged, p["w_out"], "nn", "mm_out", residual=x)
    hn2 = _rms_fwd(h1, p["norm_ffn_w"], "rms_ffn")
    up = _matmul(hn2, p["w_up"], "nn", "mm_up", tn=1408, b_stacked=True)
    act = _conv_ffn_fwd(up, p["conv_ffn_w"], p["conv_ffn_b"])
    h2 = _matmul(act, p["w_down"], "nn", "mm_down", tk=DFF // 2, residual=h1)
    dh2, dh2b, g["norm_final_w"], loss_blk = _final(h2, p["norm_final_w"], target)
    g["w_down"] = _matmul(act, dh2b, "tn", "mm_gw_down", out_dtype=BF16).reshape(4, DFF // 4, D)
    dact = _matmul(dh2b, p["w_down"], "nt", "mm_dact", out_dtype=BF16)
    dup, g["conv_ffn_w"], g["conv_ffn_b"] = _conv_ffn_bwd(up, dact, p["conv_ffn_w"], p["conv_ffn_b"])
    g["w_up"] = _matmul(hn2, dup, "tn", "mm_gw_up", out_dtype=BF16, tn=1408, out_stacked=True)
    tok = hooks["swap_start"](["w_up", "w_down"], g, "s1")
    dhn2 = _matmul(dup, p["w_up"], "nt", "mm_dhn2", tk=2816, b_stacked=True, after=tok)
    tok = hooks["scatter_go"]("s1", dhn2)
    dh1, dh1b, g["norm_ffn_w"] = _rms_bwd(dhn2, h1, p["norm_ffn_w"], dh2, "rms_ffn_bwd", after=tok)
    g["w_out"] = _matmul(merged, dh1b, "tn", "mm_gw_out", out_dtype=BF16).reshape(4, D // 4, D)
    dmerged = _matmul(dh1b, p["w_out"], "nt", "mm_dmerged")
    dga, dgb, dya, dval, dgate = _merge_bwd(dmerged, proj, ya, vg)
    dvg = jnp.concatenate([dval, dgate], axis=1)
    g["w_s5_glu"] = _matmul(g5, dvg, "tn", "mm_gw_glu", out_dtype=BF16, out_stacked=True)
    dg5 = _matmul(dvg, p["w_s5_glu"], "nt", "mm_dg5", b_stacked=True)
    tr = lambda b: b.transpose(0, 2, 1)
    gxr, gxi, dus, gcr, gci, g["s5_d"] = _s5_out_bwd(dg5, ypre, tr(cout_r).astype(BF16), tr(cout_in).astype(BF16),
                                                     proj, p["s5_d"], xre, xim)
    are, aim = _s5_scan(gxr, gxi, _scan_table(lbr, -lbi, True), True, "s5_scan_bwd")
    du, gbr, gbi, glr, gli = _s5_in_bwd(are, aim, tr(bin_r).astype(BF16), tr(bin_i).astype(BF16), proj, dus, xre, xim)
    g["s5_c_re"] = _from_out_blocks(gcr).transpose(0, 2, 1)
    g["s5_c_im"] = _from_out_blocks(gci).transpose(0, 2, 1)
    (g["s5_lam_re"], g["s5_lam_im"], g["s5_log_dt"], g["s5_b_re"], g["s5_b_im"]) = s5_vjp(
        (glr.reshape(64, 64), gli.reshape(64, 64), _from_out_blocks(gbr), _from_out_blocks(gbi)))
    g["w_proj_a"] = _matmul(yn, dya, "tn", "mm_gw_proj", out_dtype=BF16).reshape(4, DI // 4, D)
    tok = hooks["swap_start"](["w_proj_a", "w_s5_glu", "w_out"], g, "s2")
    dyn = _matmul(dya, p["w_proj_a"], "nt", "mm_dyn", after=tok)
    tok = hooks["scatter_go"]("s2", dyn)
    dyssd, dz, g["norm_a_w"] = _gnorm_bwd(dyn, yssd, proj, p["norm_a_w"], tok)
    dxs, dbm, dcm, ds_e, ddt_e, tsum, dsh8, pd = _ssd_bwd(xbc, s8, s_t, dt_e, s_e, dexp, sprev, dyssd)
    dsh = jnp.pad(dsh8.transpose(1, 0, 2).reshape(n, 8 * NG), ((0, 0), (0, HL - 8 * NG)))
    draw, ps = _ssd_post(ds_e, ddt_e, tsum, dsh, dtraw, dt, hp, emat.T)
    g["dt_bias"] = from_lanes(ps[0:1])
    g["a_log"] = from_lanes(ps[1:2])
    g["d_a"] = pd.reshape(NG * HPG, HD).sum(axis=1).reshape(1, NG * HPG)
    ddt = draw.astype(BF16)
    dxbc_parts, gcw, gcb = [], [], []
    for arr, col0, nm in ((dxs, 0, "conv_a_bwd_x"), (dbm, DI, "conv_a_bwd_b"), (dcm, DI + NG * NS, "conv_a_bwd_c")):
        dpart, gw_, gb_ = _conv_a_bwd(proj, arr, p["conv_a_w"], p["conv_a_b"], col0, nm)
        dxbc_parts.append(dpart)
        gcw.append(gw_)
        gcb.append(gb_)
    g["conv_a_w"] = jnp.concatenate(gcw, axis=1)
    g["conv_a_b"] = jnp.concatenate(gcb, axis=1)
    dproj = jnp.concatenate([dz] + dxbc_parts + [du, dga, dgb, ddt], axis=1)
    g_main = _matmul(dproj, hn1, "tn", "mm_gw_in", out_dtype=BF16, tm=896, tn=2048)
    g_dt = g_main[NMAIN:NMAIN + 8 * NG].reshape(NG, 8, D)[:, :HPG].reshape(NG * HPG, D)
    g_sh = _move_rows(g_main, RUNS_TO_SHARDS, 4 * WPAD, MT, MT, "rows_to_shards")
    g["w_in"] = lax.dynamic_update_slice(g_sh, g_dt, (DT_SHARD_ROW, 0)).reshape(4, WPAD, D)
    hooks["swap_start"](["w_in"], g, "s3")
    tok = hooks["scatter_go"]("s3", g_dt)
    dhn1 = _matmul(dproj, p["w_full"], "nn", "mm_dhn1", tk=2688, after=tok)
    gx, _, g["norm_mix_w"] = _rms_bwd(dhn1, x, p["norm_mix_w"], dh1, "rms_mix_bwd")
    return loss_blk, gx, g


BIG = ["w_in", "w_proj_a", "w_s5_glu", "w_out", "w_up", "w_down"]
SMALL = ["norm_mix_w", "conv_a_w", "conv_a_b", "dt_bias", "a_log", "d_a", "norm_a_w", "s5_lam_re", "s5_lam_im",
         "s5_log_dt", "s5_b_re", "s5_b_im", "s5_c_re", "s5_c_im", "s5_d", "norm_ffn_w", "conv_ffn_w", "conv_ffn_b",
         "norm_final_w"]
ORDER = ["norm_mix_w", "w_in", "conv_a_w", "conv_a_b", "dt_bias", "a_log", "d_a", "norm_a_w", "w_proj_a", "s5_lam_re",
         "s5_lam_im", "s5_log_dt", "s5_b_re", "s5_b_im", "s5_c_re", "s5_c_im", "s5_d", "w_s5_glu", "w_out",
         "norm_ffn_w", "w_up", "conv_ffn_w", "conv_ffn_b", "w_down", "norm_final_w"]
CONV_FULL = {"conv_a_w": (KA, CONVD), "conv_ffn_w": (KF, 2 * DFF)}


def _pack(arrs):
    flat = jnp.concatenate([a.reshape(-1).astype(F32) for a in arrs])
    total = flat.shape[0]
    padded = -(-total // 1024) * 1024
    return jnp.pad(flat, (0, padded - total)).reshape(padded // 128, 128)


def _unpack(block, shapes):
    flat = block.reshape(-1)
    out, at = [], 0
    for sh in shapes:
        size = math.prod(sh)
        out.append(flat[at:at + size].reshape(sh))
        at += size
    return out


def _stack_cols(a):
    return a.transpose(1, 0, 2).reshape(a.shape[1], 4 * a.shape[2])


def _unstack_cols(a):
    return a.reshape(a.shape[0], 4, a.shape[1] // 4).transpose(1, 0, 2)


def kernel(x, norm_mix_w, w_in, conv_a_w, conv_a_b, dt_bias, a_log, d_a, norm_a_w, w_proj_a, s5_lam_re, s5_lam_im, s5_log_dt, s5_b_re, s5_b_im, s5_c_re, s5_c_im, s5_d, w_s5_glu, w_out, norm_ffn_w, w_up, conv_ffn_w, conv_ffn_b, w_down, norm_final_w, loss_target, m_norm_mix_w, m_w_in, m_conv_a_w, m_conv_a_b, m_dt_bias, m_a_log, m_d_a, m_norm_a_w, m_w_proj_a, m_s5_lam_re, m_s5_lam_im, m_s5_log_dt, m_s5_b_re, m_s5_b_im, m_s5_c_re, m_s5_c_im, m_s5_d, m_w_s5_glu, m_w_out, m_norm_ffn_w, m_w_up, m_conv_ffn_w, m_conv_ffn_b, m_w_down, m_norm_final_w, v_norm_mix_w, v_w_in, v_conv_a_w, v_conv_a_b, v_dt_bias, v_a_log, v_d_a, v_norm_a_w, v_w_proj_a, v_s5_lam_re, v_s5_lam_im, v_s5_log_dt, v_s5_b_re, v_s5_b_im, v_s5_c_re, v_s5_c_im, v_s5_d, v_w_s5_glu, v_w_out, v_norm_ffn_w, v_w_up, v_conv_ffn_w, v_conv_ffn_b, v_w_down, v_norm_final_w):
    w = dict(norm_mix_w=norm_mix_w, w_in=w_in, conv_a_w=conv_a_w, conv_a_b=conv_a_b, dt_bias=dt_bias, a_log=a_log, d_a=d_a, norm_a_w=norm_a_w, w_proj_a=w_proj_a, s5_lam_re=s5_lam_re, s5_lam_im=s5_lam_im, s5_log_dt=s5_log_dt, s5_b_re=s5_b_re, s5_b_im=s5_b_im, s5_c_re=s5_c_re, s5_c_im=s5_c_im, s5_d=s5_d, w_s5_glu=w_s5_glu, w_out=w_out, norm_ffn_w=norm_ffn_w, w_up=w_up, conv_ffn_w=conv_ffn_w, conv_ffn_b=conv_ffn_b, w_down=w_down, norm_final_w=norm_final_w)
    m = dict(norm_mix_w=m_norm_mix_w, w_in=m_w_in, conv_a_w=m_conv_a_w, conv_a_b=m_conv_a_b, dt_bias=m_dt_bias, a_log=m_a_log, d_a=m_d_a, norm_a_w=m_norm_a_w, w_proj_a=m_w_proj_a, s5_lam_re=m_s5_lam_re, s5_lam_im=m_s5_lam_im, s5_log_dt=m_s5_log_dt, s5_b_re=m_s5_b_re, s5_b_im=m_s5_b_im, s5_c_re=m_s5_c_re, s5_c_im=m_s5_c_im, s5_d=m_s5_d, w_s5_glu=m_w_s5_glu, w_out=m_w_out, norm_ffn_w=m_norm_ffn_w, w_up=m_w_up, conv_ffn_w=m_conv_ffn_w, conv_ffn_b=m_conv_ffn_b, w_down=m_w_down, norm_final_w=m_norm_final_w)
    v = dict(norm_mix_w=v_norm_mix_w, w_in=v_w_in, conv_a_w=v_conv_a_w, conv_a_b=v_conv_a_b, dt_bias=v_dt_bias, a_log=v_a_log, d_a=v_d_a, norm_a_w=v_norm_a_w, w_proj_a=v_w_proj_a, s5_lam_re=v_s5_lam_re, s5_lam_im=v_s5_lam_im, s5_log_dt=v_s5_log_dt, s5_b_re=v_s5_b_re, s5_b_im=v_s5_b_im, s5_c_re=v_s5_c_re, s5_c_im=v_s5_c_im, s5_d=v_s5_d, w_s5_glu=v_w_s5_glu, w_out=v_w_out, norm_ffn_w=v_norm_ffn_w, w_up=v_w_up, conv_ffn_w=v_conv_ffn_w, conv_ffn_b=v_conv_ffn_b, w_down=v_w_down, norm_final_w=v_norm_final_w)
    xi, yi, ci = _place()
    chip = 2 * xi + yi

    cidx = jnp.reshape(ci, (1,)).astype(jnp.int32)
    sidx = jnp.reshape(chip, (1,)).astype(jnp.int32)

    tw = lambda a: jnp.transpose(a[0])[None]
    w["w_in"], m["w_in"], v["w_in"] = tw(w_in), tw(m_w_in), tw(v_w_in)
    shards = [w[k][0].astype(BF16) for k in BIG]
    shards[0] = jnp.pad(shards[0], ((0, WPAD - WSH), (0, 0)))
    i_sems, i_srcs, i_lands, i_tok = _ici_start("gather", shards[:1], cidx, "gather_in_start")
    hn1 = _rms_fwd(x[0], norm_mix_w, "rms_mix", after=i_tok)

    late = {}

    def late_start(after):
        srcs, got = _ici_wait("gather", g_sems, g_srcs, g_lands, after, "gather_rest_wait")
        late["sems"], late["got"], late["srcs"], tok = _ici_start("pass", list(got), cidx, "pass_rest_start",
                                                                  lands=list(srcs))
        return tok

    def late_weights(after):
        full, _ = _ici_wait("pass", late["sems"], late["got"], late["srcs"], after, "pass_rest_wait")
        return {"w_proj_a": full[0].reshape(DI, D), "w_s5_glu": full[1], "w_out": full[2].reshape(D, D),
                "w_up": full[3], "w_down": full[4].reshape(DFF, D)}

    swaps, pending = {}, []

    def swap_start(names, g, tag):
        sems, parts, lands, tok = _ici_start("swap", [g[k] for k in names], cidx, "swap_start_" + tag)
        swaps[tag] = (names, sems, parts, lands)
        return tok

    def scatter_go(tag, after):
        names, sems, parts, lands = swaps[tag]
        parts, sib = _ici_wait("swap", sems, parts, lands, after, "swap_wait_" + tag)
        sums = [_chip_sum(parts[t], sib[t], cidx, "chip_sum_" + k) for t, k in enumerate(names)]
        sems, srcs, lands, tok = _ici_start("scatter", sums, cidx, "scatter_start_" + tag)
        pending.append((names, tag, sems, srcs, lands))
        return tok

    hooks = {"late_start": late_start, "late_weights": late_weights, "swap_start": swap_start,
             "scatter_go": scatter_go}
    conv_blocks = []
    for k, (taps, cols) in CONV_FULL.items():
        shard = jnp.where(ci == 0, w[k][0], 0.0)
        conv_blocks.append(lax.dynamic_update_slice_in_dim(jnp.zeros((taps, cols), F32), shard, chip * (cols // 4), 1))
    conv_full = _unpack(_allsum_small(_pack(conv_blocks), "sum_conv_w", i_tok), [CONV_FULL[k] for k in CONV_FULL])

    i_srcs, i_got = _ici_wait("gather", i_sems, i_srcs, i_lands, conv_full[0], "gather_in_wait")
    w_sh = _pass_halves(list(i_got), list(i_srcs), "pass_halves_in")[0].reshape(4 * WPAD, D)
    w_dt = jnp.pad(w_sh[DT_SHARD_ROW:DT_SHARD_ROW + NG * HPG].reshape(NG, HPG, D), ((0, 0), (0, 8 - HPG), (0, 0)))
    w_full = lax.dynamic_update_slice(_move_rows(w_sh, RUNS_TO_MAIN, NFULL, MT, MT, "rows_to_main"),
                                      w_dt.reshape(8 * NG, D), (NMAIN, 0))
    g_sems, g_srcs, g_lands, token = _ici_start("gather", shards[1:], w_full, "gather_rest_start")
    p = {
        "w_full": w_full,
        "conv_a_w": conv_full[0], "conv_ffn_w": conv_full[1],
        "conv_a_b": conv_a_b, "conv_ffn_b": conv_ffn_b,
        "norm_mix_w": norm_mix_w, "norm_a_w": norm_a_w, "norm_ffn_w": norm_ffn_w,
        "norm_final_w": norm_final_w.reshape(1, D),
        "dt_bias": dt_bias, "a_log": a_log, "d_a": d_a, "s5_d": s5_d,
        "s5_lam_re": s5_lam_re[0], "s5_lam_im": s5_lam_im[0], "s5_log_dt": s5_log_dt[0],
        "s5_b_re": s5_b_re[0], "s5_b_im": s5_b_im[0], "s5_c_re": s5_c_re[0], "s5_c_im": s5_c_im[0],
    }
    loss_blk, gx, g = _local_step(x[0], loss_target[0], hn1, p, token, hooks)

    after, halves = gx, {}
    for names, tag, sems, srcs, lands in pending:
        srcs, got = _ici_wait("scatter", sems, srcs, lands, after, "scatter_wait_" + tag)
        for t, k in enumerate(names):
            halves[k] = _shard_sum(srcs[t], got[t], sidx, "shard_sum_" + k)
        after = halves[names[0]]
    w_sems, g_mine, w_lands, w_tok = _ici_start("whole", [halves[k] for k in BIG], cidx, "whole_start")

    small_shapes = [CONV_FULL.get(k, w[k].shape[1:] if k != "norm_final_w" else w[k].shape) for k in SMALL]
    small = _allsum_small(_pack([g[k] for k in SMALL] + [loss_blk[0:1, 0:1]]), "sum_small_grads", w_tok)
    small_grads = dict(zip(SMALL + ["loss"], _unpack(small, small_shapes + [(1,)])))
    for k, (taps, cols) in CONV_FULL.items():
        small_grads[k] = lax.dynamic_slice_in_dim(small_grads[k], chip * (cols // 4), cols // 4, axis=1)
    loss = small_grads.pop("loss").reshape(())

    grads, delta, new_m, new_v = {}, {}, {}, {}
    for k in SMALL:
        grads[k] = small_grads[k].reshape(w[k].shape)
    pk = lambda t: _pack([t[k] for k in SMALL])
    d_, m_, v_ = _adamw(pk(w), pk(grads), pk(m), pk(v), "adamw_small")
    shapes = [w[k].shape for k in SMALL]
    for k, dd, mm, vv in zip(SMALL, _unpack(d_, shapes), _unpack(m_, shapes), _unpack(v_, shapes)):
        delta[k], new_m[k], new_v[k] = dd, mm, vv
    g_mine, g_other = _ici_wait("whole", w_sems, g_mine, w_lands, d_, "whole_wait")
    for t, k in enumerate(BIG):
        outs = _adamw_halves(w[k], g_mine[t], g_other[t], m[k], v[k], cidx, "adamw_" + k)
        grads[k], delta[k], new_m[k], new_v[k] = [tw(o) for o in outs] if k == "w_in" else outs
    return (loss, gx[None], *[grads[k] for k in ORDER], *[delta[k] for k in ORDER],
            *[new_m[k] for k in ORDER], *[new_v[k] for k in ORDER])
```

```python
import functools
import math

import jax
import jax.numpy as jnp
from jax import lax
from jax.experimental import pallas as pl
from jax.experimental.pallas import tpu as pltpu

F32 = jnp.float32
BF16 = jnp.bfloat16
HI = lax.Precision.HIGHEST
MESH = pl.DeviceIdType.MESH
ANY = pl.BlockSpec(memory_space=pl.ANY)

D = 2048
DI = 3072
HD = 64
NG = 8
HPG = 6
GW = HPG * HD
NS = 128
KA = 4
Q = 256
CONVD = DI + 2 * NG * NS
DS5 = 1024
NCH = 4096
DFF = 5632
KF = 3
EPS = 1e-6
EIG_MAX = -1e-4
NMAIN = 13312
OFF_XBC, OFF_U, OFF_GA, OFF_GB = 3072, 8192, 9216, 11264
WSH = 3340
WPAD = 3360
IN_SPLIT = [DI, DI + CONVD, DI + CONVD + NG * HPG]
NFULL = NMAIN + 128
MT = 336


def _w_in_runs():
    runs = []
    for k in range(4):
        for o_lo, o_hi, m_lo in ((0, IN_SPLIT[1], 0), (IN_SPLIT[2], 4 * WSH, IN_SPLIT[1])):
            lo, hi = max(o_lo, WSH * k), min(o_hi, WSH * (k + 1))
            if lo < hi:
                runs.append((m_lo + lo - o_lo, m_lo + hi - o_lo, WPAD * k + lo - WSH * k))
    return runs


RUNS_TO_MAIN = _w_in_runs()
RUNS_TO_SHARDS = [(s_lo, s_lo + m_hi - m_lo, m_lo) for m_lo, m_hi, s_lo in RUNS_TO_MAIN]
DT_SHARD_ROW = WPAD * (IN_SPLIT[1] // WSH) + IN_SPLIT[1] % WSH
assert IN_SPLIT[1] // WSH == (IN_SPLIT[2] - 1) // WSH
VMEM_LIMIT = 56 * 1024 * 1024

LR, B1, B2, AEPS, WD, STEP = 0.001, 0.9, 0.999, 1e-08, 0.01, 10


def _cp(*sem):
    return pltpu.CompilerParams(dimension_semantics=sem, vmem_limit_bytes=VMEM_LIMIT)


def _sig(x):
    return jax.nn.sigmoid(x)


def _silu(x):
    return x * _sig(x)


def _dsilu(x):
    s = _sig(x)
    return s * (1.0 + x * (1.0 - s))


def _softplus(x):
    return jnp.maximum(x, 0.0) + jnp.log(1.0 + jnp.exp(-jnp.abs(x)))


_GC = math.sqrt(2.0 / math.pi)


def _gelu(x):
    return 0.5 * x * (1.0 + jnp.tanh(_GC * (x + 0.044715 * x * x * x)))


def _dgelu(x):
    t = jnp.tanh(_GC * (x + 0.044715 * x * x * x))
    return 0.5 * (1.0 + t) + 0.5 * x * (1.0 - t * t) * _GC * (1.0 + 3.0 * 0.044715 * x * x)


def _dot(a, b, dims=((1,), (0,)), prec=None):
    return lax.dot_general(a, b, (dims, ((), ())), precision=prec, preferred_element_type=F32)


NT = ((1,), (1,))
TN = ((0,), (0,))


def _pick(n, t):
    for unit in (128, 8):
        for cand in range(min(n, t) // unit * unit, 0, -unit):
            if n % cand == 0:
                return cand
    return n


def _matmul(a, b, mode, name, out_dtype=F32, tm=1024, tn=1024, tk=2048, residual=None, b_stacked=False,
            out_stacked=False, after=None):
    if b_stacked:
        _, brows, bn = b.shape
        bshape = (brows, 4 * bn)
    else:
        bshape = b.shape
    if mode == "nn":
        (m, k), (k2, n) = a.shape, bshape
    elif mode == "nt":
        (m, k), (n, k2) = a.shape, bshape
    else:
        (k, m), (k2, n) = a.shape, bshape
    assert k == k2
    tm = _pick(m, tm)
    tn = _pick(n // 4 if (out_stacked or (b_stacked and mode != "nt")) else n, tn)
    tk = _pick(k // 4 if (b_stacked and mode == "nt") else k, tk)
    nk = k // tk
    dims = {"nn": ((1,), (0,)), "nt": NT, "tn": TN}[mode]
    has_res = residual is not None
    n_in = 2 + has_res + (after is not None)

    def body(*refs):
        a_ref, b_ref = refs[0], refs[1]
        r_ref = refs[2] if has_res else None
        o_ref = refs[n_in]
        p = _dot(a_ref[...], b_ref[...], dims)

        def finish(r):
            if has_res:
                r = r + r_ref[...]
            o_ref[...] = r.astype(out_dtype)

        if nk == 1:
            finish(p)
        else:
            acc = refs[-1]
            kk = pl.program_id(2)

            @pl.when(kk == 0)
            def _():
                acc[...] = p

            @pl.when(kk > 0)
            def _():
                acc[...] += p

            @pl.when(kk == nk - 1)
            def _():
                finish(acc[...])

    if mode == "tn":
        a_spec = pl.BlockSpec((tk, tm), lambda i, j, kk: (kk, i))
    else:
        a_spec = pl.BlockSpec((tm, tk), lambda i, j, kk: (i, kk))
    if mode == "nt":
        if b_stacked:
            per = bn // tk
            b_spec = pl.BlockSpec((None, tn, tk), lambda i, j, kk: (kk // per, j, kk % per))
        else:
            b_spec = pl.BlockSpec((tn, tk), lambda i, j, kk: (j, kk))
    elif b_stacked:
        per = bn // tn
        b_spec = pl.BlockSpec((None, tk, tn), lambda i, j, kk: (j // per, kk, j % per))
    else:
        b_spec = pl.BlockSpec((tk, tn), lambda i, j, kk: (kk, j))
    o_spec = pl.BlockSpec((tm, tn), lambda i, j, kk: (i, j))
    if out_stacked:
        per_o = n // 4 // tn
        out_spec = pl.BlockSpec((None, tm, tn), lambda i, j, kk: (j // per_o, i, j % per_o))
        out_shape = jax.ShapeDtypeStruct((4, m, n // 4), out_dtype)
    else:
        out_spec, out_shape = o_spec, jax.ShapeDtypeStruct((m, n), out_dtype)
    in_specs, args = [a_spec, b_spec], [a, b]
    if has_res:
        in_specs.append(o_spec)
        args.append(residual)
    if after is not None:
        in_specs.append(ANY)
        args.append(after)
    return pl.pallas_call(
        body, name=name, grid=(m // tm, n // tn, nk),
        in_specs=in_specs, out_specs=out_spec, out_shape=out_shape,
        scratch_shapes=[pltpu.VMEM((tm, tn), F32)] if nk > 1 else [],
        compiler_params=_cp("parallel", "parallel", "arbitrary"),
    )(*args)


def _move_rows(src, runs, rows_out, t_out, t_in, name):
    rows_in, cols = src.shape
    nb_out, nb_in = rows_out // t_out, rows_in // t_in
    assert rows_out % t_out == 0 and rows_in % t_in == 0 and t_in >= t_out
    blk, off, lo, hi = ([[0] * nb_out for _ in range(2)] for _ in range(4))
    for i in range(nb_out):
        hits = [r for r in runs if r[0] < (i + 1) * t_out and r[1] > i * t_out]
        assert len(hits) <= 2
        for s, (o_lo, o_hi, s_lo) in enumerate(hits):
            lo[s][i] = max(o_lo, i * t_out) - i * t_out
            hi[s][i] = min(o_hi, (i + 1) * t_out) - i * t_out
            first = i * t_out + lo[s][i] - o_lo + s_lo
            blk[s][i] = min(first // t_in, nb_in - 1)
            off[s][i] = first - lo[s][i] - blk[s][i] * t_in
    table = jnp.asarray([blk[0], off[0], lo[0], hi[0], blk[1], off[1], lo[1], hi[1]], jnp.int32)

    def body(tab, a0, a1, b0, b1, o_ref):
        i = pl.program_id(0)
        o_ref[...] = jnp.zeros_like(o_ref)
        r = lax.broadcasted_iota(jnp.int32, (t_out, t_in), 0)
        k = lax.broadcasted_iota(jnp.int32, (t_out, t_in), 1)
        for s, (first, second) in enumerate(((a0, a1), (b0, b1))):
            off_s, lo_s, hi_s = tab[4 * s + 1, i], tab[4 * s + 2, i], tab[4 * s + 3, i]
            live = (r >= lo_s) & (r < hi_s)

            @pl.when(hi_s > lo_s)
            def _():
                sel = (live & (k == r + off_s)).astype(BF16)
                o_ref[...] += _dot(sel, first[...]).astype(o_ref.dtype)

            @pl.when((hi_s > lo_s) & (off_s + hi_s > t_in))
            def _():
                sel = (live & (k == r + off_s - t_in)).astype(BF16)
                o_ref[...] += _dot(sel, second[...]).astype(o_ref.dtype)

    def in_spec(s, nxt):
        return pl.BlockSpec((t_in, cols), lambda i, tab: (jnp.minimum(tab[4 * s, i] + nxt, nb_in - 1), 0))

    return pl.pallas_call(
        body, name=name,
        grid_spec=pltpu.PrefetchScalarGridSpec(
            num_scalar_prefetch=1, grid=(nb_out,),
            in_specs=[in_spec(0, 0), in_spec(0, 1), in_spec(1, 0), in_spec(1, 1)],
            out_specs=pl.BlockSpec((t_out, cols), lambda i, tab: (i, 0))),
        out_shape=jax.ShapeDtypeStruct((rows_out, cols), src.dtype), compiler_params=_cp("parallel"),
    )(table, src, src, src, src)


TL = 256


def _rms_fwd(x, w, name, after=None):
    n, d = x.shape

    def body(x_ref, w_ref, *rest):
        xv = x_ref[...]
        r = lax.rsqrt(jnp.mean(xv * xv, axis=-1, keepdims=True) + EPS)
        rest[-1][...] = (xv * r * w_ref[...]).astype(BF16)

    extra = [] if after is None else [after]
    return pl.pallas_call(
        body, name=name, grid=(n // TL,),
        in_specs=[pl.BlockSpec((TL, d), lambda i: (i, 0)), pl.BlockSpec((1, d), lambda i: (0, 0))] + [ANY] * len(extra),
        out_specs=pl.BlockSpec((TL, d), lambda i: (i, 0)),
        out_shape=jax.ShapeDtypeStruct((n, d), BF16), compiler_params=_cp("parallel"),
    )(x, w, *extra)


def _rms_bwd(dhn, x, w, dres, name, after=None):
    n, d = x.shape

    def body(g_ref, x_ref, w_ref, r_ref, *rest):
        dx_ref, dxb_ref, gw_ref = rest[-3:]
        xv = x_ref[...]
        r = lax.rsqrt(jnp.mean(xv * xv, axis=-1, keepdims=True) + EPS)
        xh = xv * r
        gv = g_ref[...]
        g = gv * w_ref[...]
        dx = r_ref[...] + r * (g - xh * jnp.mean(g * xh, axis=-1, keepdims=True))
        dx_ref[...] = dx
        dxb_ref[...] = dx.astype(BF16)

        @pl.when(pl.program_id(0) == 0)
        def _():
            gw_ref[...] = jnp.zeros_like(gw_ref)

        gw_ref[...] += jnp.sum(gv * xh, axis=0, keepdims=True)

    extra = [] if after is None else [after]
    row = pl.BlockSpec((TL, d), lambda i: (i, 0))
    vec = pl.BlockSpec((1, d), lambda i: (0, 0))
    return pl.pallas_call(
        body, name=name, grid=(n // TL,),
        in_specs=[row, row, vec, row] + [ANY] * len(extra), out_specs=[row, row, vec],
        out_shape=[jax.ShapeDtypeStruct((n, d), F32), jax.ShapeDtypeStruct((n, d), BF16),
                   jax.ShapeDtypeStruct((1, d), F32)],
        compiler_params=_cp("arbitrary"),
    )(dhn, x, w, dres, *extra)


def _final(h2, w, target):
    n, d = h2.shape

    def body(x_ref, w_ref, t_ref, dx_ref, dxb_ref, gw_ref, loss_ref):
        xv = x_ref[...]
        r = lax.rsqrt(jnp.mean(xv * xv, axis=-1, keepdims=True) + EPS)
        xh = xv * r
        diff = xh * w_ref[...] - t_ref[...]
        gv = diff * (1.0 / d)
        g = gv * w_ref[...]
        dx = r * (g - xh * jnp.mean(g * xh, axis=-1, keepdims=True))
        dx_ref[...] = dx
        dxb_ref[...] = dx.astype(BF16)

        @pl.when(pl.program_id(0) == 0)
        def _():
            gw_ref[...] = jnp.zeros_like(gw_ref)
            loss_ref[...] = jnp.zeros_like(loss_ref)

        gw_ref[...] += jnp.sum(gv * xh, axis=0, keepdims=True)
        part = 0.5 * jnp.sum(jnp.mean(diff * diff, axis=-1, keepdims=True), axis=0, keepdims=True)
        loss_ref[...] += jnp.broadcast_to(part, loss_ref.shape)

    row = pl.BlockSpec((TL, d), lambda i: (i, 0))
    vec = pl.BlockSpec((1, d), lambda i: (0, 0))
    return pl.pallas_call(
        body, name="final_loss", grid=(n // TL,),
        in_specs=[row, vec, row], out_specs=[row, row, vec, pl.BlockSpec((8, 128), lambda i: (0, 0))],
        out_shape=[jax.ShapeDtypeStruct((n, d), F32), jax.ShapeDtypeStruct((n, d), BF16),
                   jax.ShapeDtypeStruct((1, d), F32), jax.ShapeDtypeStruct((8, 128), F32)],
        compiler_params=_cp("arbitrary"),
    )(h2, w, target)


CT = 512
CL = 512


def _lagged(xf, taps, rows):
    return [xf[8:8 + rows]] + [pltpu.roll(xf, s, 0)[8:8 + rows] for s in range(1, taps)]


def _shift_up(x, u, n):
    if u == 0:
        return x[0:n]
    return pltpu.roll(x, x.shape[0] - u, 0)[0:n]


def _conv_pre(lagged, w_ref, b_ref, taps):
    pre = b_ref[...]
    for k in range(taps):
        pre = pre + w_ref[k:k + 1, :] * lagged[taps - 1 - k]
    return pre


def _conv_back(e, w_ref, taps):
    dx = w_ref[taps - 1:taps, :] * e[0:CL]
    for k in range(taps - 1):
        dx = dx + w_ref[k:k + 1, :] * _shift_up(e, taps - 1 - k, CL)
    return dx


def _halo_specs(n, col_of):
    per = CL // 8
    cur = pl.BlockSpec((CL, CT), lambda j, i, *_: (i, col_of(j)))
    prev = pl.BlockSpec((8, CT), lambda j, i, *_: (jnp.maximum(i * per - 1, 0), col_of(j)))
    nxt = pl.BlockSpec((8, CT), lambda j, i, *_: (jnp.minimum((i + 1) * per, n // 8 - 1), col_of(j)))
    return prev, cur, nxt


def _conv_a_fwd(proj, w, b):
    n = proj.shape[0]
    off = OFF_XBC // CT

    def body(p_ref, x_ref, w_ref, b_ref, o_ref):
        p8 = jnp.where(pl.program_id(1) > 0, p_ref[...], 0.0)
        xf = jnp.concatenate([p8, x_ref[...]], axis=0)
        o_ref[...] = _silu(_conv_pre(_lagged(xf, KA, CL), w_ref, b_ref, KA))

    prev, cur, _ = _halo_specs(n, lambda j: j + off)
    return pl.pallas_call(
        body, name="conv_a_fwd", grid=(CONVD // CT, n // CL),
        in_specs=[prev, cur, pl.BlockSpec((KA, CT), lambda j, i: (0, j)), pl.BlockSpec((1, CT), lambda j, i: (0, j))],
        out_specs=pl.BlockSpec((CL, CT), lambda j, i: (i, j)),
        out_shape=jax.ShapeDtypeStruct((n, CONVD), F32), compiler_params=_cp("parallel", "parallel"),
    )(proj, proj, w, b)


def _conv_a_bwd(proj, dout, w, b, col0, name):
    n, width = dout.shape
    off = (OFF_XBC + col0) // CT
    woff = col0 // CT
    nl = n // CL

    def body(p_ref, x_ref, n_ref, d_ref, dn_ref, w_ref, b_ref, dx_ref, dw_ref, db_ref):
        i = pl.program_id(1)
        xf = jnp.concatenate([jnp.where(i > 0, p_ref[...], 0.0), x_ref[...], n_ref[...]], axis=0)
        lag = _lagged(xf, KA, CL + 8)
        de = jnp.concatenate([d_ref[...], jnp.where(i < nl - 1, dn_ref[...], 0.0)], axis=0)
        se = de * _dsilu(_conv_pre(lag, w_ref, b_ref, KA))
        dx_ref[...] = _conv_back(se, w_ref, KA).astype(BF16)

        @pl.when(i == 0)
        def _():
            dw_ref[...] = jnp.zeros_like(dw_ref)
            db_ref[...] = jnp.zeros_like(db_ref)

        sc = se[0:CL]
        for k in range(KA):
            dw_ref[k:k + 1, :] += jnp.sum(sc * lag[KA - 1 - k][0:CL], axis=0, keepdims=True)
        db_ref[...] += jnp.sum(sc, axis=0, keepdims=True)

    prev, cur, nxt = _halo_specs(n, lambda j: j + off)
    _, dcur, dnxt = _halo_specs(n, lambda j: j)
    wspec = pl.BlockSpec((KA, CT), lambda j, i: (0, j + woff))
    bspec = pl.BlockSpec((1, CT), lambda j, i: (0, j + woff))
    return pl.pallas_call(
        body, name=name, grid=(width // CT, nl),
        in_specs=[prev, cur, nxt, dcur, dnxt, wspec, bspec],
        out_specs=[pl.BlockSpec((CL, CT), lambda j, i: (i, j)), pl.BlockSpec((KA, CT), lambda j, i: (0, j)),
                   pl.BlockSpec((1, CT), lambda j, i: (0, j))],
        out_shape=[jax.ShapeDtypeStruct((n, width), BF16), jax.ShapeDtypeStruct((KA, width), F32),
                   jax.ShapeDtypeStruct((1, width), F32)],
        compiler_params=_cp("parallel", "arbitrary"),
    )(proj, proj, proj, dout, dout, w, b)


def _conv_ffn_fwd(up, w, b):
    n = up.shape[0]
    nb = DFF // CT

    def body(pg_ref, g_ref, pv_ref, v_ref, wg_ref, bg_ref, wv_ref, bv_ref, o_ref):
        inner = pl.program_id(1) > 0
        gf = jnp.concatenate([jnp.where(inner, pg_ref[...], 0.0), g_ref[...]], axis=0)
        vf = jnp.concatenate([jnp.where(inner, pv_ref[...], 0.0), v_ref[...]], axis=0)
        gc = _conv_pre(_lagged(gf, KF, CL), wg_ref, bg_ref, KF)
        vc = _conv_pre(_lagged(vf, KF, CL), wv_ref, bv_ref, KF)
        o_ref[...] = (_silu(gc) * vc).astype(BF16)

    gp, gcur, _ = _halo_specs(n, lambda j: j)
    vp, vcur, _ = _halo_specs(n, lambda j: j + nb)
    return pl.pallas_call(
        body, name="conv_ffn_fwd", grid=(nb, n // CL),
        in_specs=[gp, gcur, vp, vcur,
                  pl.BlockSpec((KF, CT), lambda j, i: (0, j)), pl.BlockSpec((1, CT), lambda j, i: (0, j)),
                  pl.BlockSpec((KF, CT), lambda j, i: (0, j + nb)), pl.BlockSpec((1, CT), lambda j, i: (0, j + nb))],
        out_specs=pl.BlockSpec((CL, CT), lambda j, i: (i, j)),
        out_shape=jax.ShapeDtypeStruct((n, DFF), BF16), compiler_params=_cp("parallel", "parallel"),
    )(up, up, up, up, w, b, w, b)


def _conv_ffn_bwd(up, dact, w, b):
    n = up.shape[0]
    nb = DFF // CT
    nl = n // CL

    def body(pg_ref, g_ref, ng_ref, pv_ref, v_ref, nv_ref, d_ref, dn_ref, wg_ref, bg_ref, wv_ref, bv_ref,
             dxg_ref, dxv_ref, dwg_ref, dwv_ref, dbg_ref, dbv_ref):
        i = pl.program_id(1)
        gf = jnp.concatenate([jnp.where(i > 0, pg_ref[...], 0.0), g_ref[...], ng_ref[...]], axis=0)
        vf = jnp.concatenate([jnp.where(i > 0, pv_ref[...], 0.0), v_ref[...], nv_ref[...]], axis=0)
        glag, vlag = _lagged(gf, KF, CL + 8), _lagged(vf, KF, CL + 8)
        de = jnp.concatenate([d_ref[...], jnp.where(i < nl - 1, dn_ref[...], 0.0)], axis=0).astype(F32)
        gc = _conv_pre(glag, wg_ref, bg_ref, KF)
        vc = _conv_pre(vlag, wv_ref, bv_ref, KF)
        sg = _sig(gc)
        dgc = de * vc * (sg * (1.0 + gc * (1.0 - sg)))
        dvc = de * (gc * sg)
        dxg_ref[...] = _conv_back(dgc, wg_ref, KF).astype(BF16)
        dxv_ref[...] = _conv_back(dvc, wv_ref, KF).astype(BF16)

        @pl.when(i == 0)
        def _():
            for r in (dwg_ref, dwv_ref, dbg_ref, dbv_ref):
                r[...] = jnp.zeros_like(r)

        for e, lag, dw_ref, db_ref in ((dgc, glag, dwg_ref, dbg_ref), (dvc, vlag, dwv_ref, dbv_ref)):
            ec = e[0:CL]
            for k in range(KF):
                dw_ref[k:k + 1, :] += jnp.sum(ec * lag[KF - 1 - k][0:CL], axis=0, keepdims=True)
            db_ref[...] += jnp.sum(ec, axis=0, keepdims=True)

    gp, gcur, gnx = _halo_specs(n, lambda j: j)
    vp, vcur, vnx = _halo_specs(n, lambda j: j + nb)
    wcol = lambda o: (pl.BlockSpec((KF, CT), lambda j, i: (0, j + o)), pl.BlockSpec((1, CT), lambda j, i: (0, j + o)))
    wg, bg = wcol(0)
    wv, bv = wcol(nb)
    dxs = pl.BlockSpec((CL, CT), lambda j, i: (i, j))
    outs = pl.pallas_call(
        body, name="conv_ffn_bwd", grid=(nb, nl),
        in_specs=[gp, gcur, gnx, vp, vcur, vnx, gcur, gnx, wg, bg, wv, bv],
        out_specs=[dxs, dxs, wg, wg, bg, bg],
        out_shape=[jax.ShapeDtypeStruct((n, DFF), BF16)] * 2 + [jax.ShapeDtypeStruct((KF, DFF), F32)] * 2
        + [jax.ShapeDtypeStruct((1, DFF), F32)] * 2,
        compiler_params=_cp("parallel", "arbitrary"),
    )(up, up, up, up, up, up, dact, dact, w, b, w, b)
    return [jnp.concatenate(outs[k:k + 2], axis=1) for k in (0, 2, 4)]


HL = 128


def _split3(x):
    hi = x.astype(BF16)
    r1 = x - hi.astype(F32)
    mid = r1.astype(BF16)
    return hi, mid, (r1 - mid.astype(F32)).astype(BF16)


def _dot3(x, m):
    hi, mid, lo = _split3(x)
    return _dot(hi, m) + _dot(mid, m) + _dot(lo, m)


def _tri():
    row = lax.broadcasted_iota(jnp.int32, (Q, Q), 0)
    col = lax.broadcasted_iota(jnp.int32, (Q, Q), 1)
    return row >= col, row <= col


def _ssd_prep(dtraw, hp, emat):
    n = dtraw.shape[0]

    def body(d_ref, hp_ref, e_ref, dt_ref, s_ref, st_ref, dte_ref, se_ref):
        lower, upper = _tri()
        dt = _softplus(d_ref[...] + hp_ref[0:1, :])
        da = dt * (-jnp.exp(hp_ref[1:2, :]))
        s = _dot(lower.astype(F32), da, prec=HI)
        dt_ref[...] = dt
        s_ref[...] = s
        st_ref[...] = _dot(da, upper.astype(F32), TN, prec=HI)
        e = e_ref[...]
        dte_ref[...] = _dot3(dt, e)
        se_ref[...] = _dot3(s, e)

    row = pl.BlockSpec((Q, HL), lambda c: (c, 0))
    wide = pl.BlockSpec((Q, DI), lambda c: (c, 0))
    return pl.pallas_call(
        body, name="ssd_prep", grid=(n // Q,),
        in_specs=[pl.BlockSpec((Q, HL), lambda c: (c, NMAIN // HL)), pl.BlockSpec((8, HL), lambda c: (0, 0)),
                  pl.BlockSpec((HL, DI), lambda c: (0, 0))],
        out_specs=[row, row, pl.BlockSpec((HL, Q), lambda c: (0, c)), wide, wide],
        out_shape=[jax.ShapeDtypeStruct((n, HL), F32)] * 2 + [jax.ShapeDtypeStruct((HL, n), F32)]
        + [jax.ShapeDtypeStruct((n, DI), F32)] * 2,
        compiler_params=_cp("parallel"),
    )(dtraw, hp, emat)


def _ssd_post(ds_e, ddt_e, tsum, dsh, dtraw, dt, hp, emat_t):
    n = dtraw.shape[0]

    def body(dse_ref, dde_ref, ts_ref, dsh_ref, d_ref, dt_ref, hp_ref, et_ref, draw_ref, ps_ref):
        _, upper = _tri()
        et = et_ref[...]
        a = -jnp.exp(hp_ref[1:2, :])
        rows = lax.broadcasted_iota(jnp.int32, (Q, HL), 0)
        ds_t = _dot3(jnp.broadcast_to(ts_ref[...], (8, DI)), et)[0:1, :]
        ds = _dot3(dse_ref[...], et) + dsh_ref[...] + jnp.where(rows == Q - 1, ds_t, 0.0)
        d_a = _dot(upper.astype(F32), ds, prec=HI)
        draw = (_dot3(dde_ref[...], et) + d_a * a) * _sig(d_ref[...] + hp_ref[0:1, :])
        draw_ref[...] = draw

        @pl.when(pl.program_id(0) == 0)
        def _():
            ps_ref[...] = jnp.zeros_like(ps_ref)

        ps_ref[0:1, :] += jnp.sum(draw, axis=0, keepdims=True)
        ps_ref[1:2, :] += jnp.sum(d_a * dt_ref[...], axis=0, keepdims=True) * a

    row = pl.BlockSpec((Q, HL), lambda c: (c, 0))
    wide = pl.BlockSpec((Q, DI), lambda c: (c, 0))
    small = pl.BlockSpec((8, HL), lambda c: (0, 0))
    return pl.pallas_call(
        body, name="ssd_post", grid=(n // Q,),
        in_specs=[wide, wide, pl.BlockSpec((None, 1, DI), lambda c: (c, 0, 0)), row,
                  pl.BlockSpec((Q, HL), lambda c: (c, NMAIN // HL)), row, small,
                  pl.BlockSpec((DI, HL), lambda c: (0, 0))],
        out_specs=[row, small],
        out_shape=[jax.ShapeDtypeStruct((n, HL), F32), jax.ShapeDtypeStruct((8, HL), F32)],
        compiler_params=_cp("arbitrary"),
    )(ds_e, ddt_e, tsum, dsh, dtraw, dt, hp, emat_t)


def _ssd_specs(nc, rev):
    cc = (lambda c: nc - 1 - c) if rev else (lambda c: c)
    return [
        pl.BlockSpec((Q, GW), lambda g, c: (cc(c), g)),
        pl.BlockSpec((Q, NS), lambda g, c: (cc(c), DI // NS + g)),
        pl.BlockSpec((Q, NS), lambda g, c: (cc(c), (DI + NG * NS) // NS + g)),
        pl.BlockSpec((None, Q, 8), lambda g, c: (g, cc(c), 0)),
        pl.BlockSpec((8, Q), lambda g, c: (g, cc(c))),
        pl.BlockSpec((Q, GW), lambda g, c: (cc(c), g)),
        pl.BlockSpec((Q, GW), lambda g, c: (cc(c), g)),
        pl.BlockSpec((1, GW), lambda g, c: (0, g)),
    ]


def _ssd_fwd(xbc, s8, s_t, dt_e, s_e, dexp):
    n = xbc.shape[0]
    nc = n // Q

    def body(xs_ref, b_ref, c_ref, sc_ref, sr_ref, dte_ref, se_ref, dexp_ref, y_ref, sp_ref, st):
        @pl.when(pl.program_id(1) == 0)
        def _():
            st[...] = jnp.zeros_like(st)

        lower, _ = _tri()
        s_c, s_r, dt_e, s_e = sc_ref[...], sr_ref[...], dte_ref[...], se_ref[...]
        xs = xs_ref[...]
        x = xs * dt_e
        xb = x.astype(BF16)
        bb, cb = b_ref[...].astype(BF16), c_ref[...].astype(BF16)
        cbm = _dot(cb, bb, NT)
        st_e = s_e[Q - 1:Q, :]
        sprev = st[...]
        sp_ref[...] = sprev
        yoff = _dot(cb, sprev.astype(BF16)) * jnp.exp(s_e) + dexp_ref[...] * xs
        for h in range(HPG):
            sl = slice(h * HD, (h + 1) * HD)
            lm = jnp.where(lower, jnp.exp(jnp.minimum(s_c[:, h:h + 1] - s_r[h:h + 1, :], 0.0)), 0.0)
            y_ref[:, sl] = _dot((cbm * lm).astype(BF16), xb[:, sl]) + yoff[:, sl]
        w = (x * jnp.exp(st_e - s_e)).astype(BF16)
        st[...] = jnp.exp(st_e) * sprev + _dot(bb, w, TN)

    return pl.pallas_call(
        body, name="ssd_fwd", grid=(NG, nc), in_specs=_ssd_specs(nc, False),
        out_specs=[pl.BlockSpec((Q, GW), lambda g, c: (c, g)),
                   pl.BlockSpec((None, None, NS, GW), lambda g, c: (c, g, 0, 0))],
        out_shape=[jax.ShapeDtypeStruct((n, DI), F32), jax.ShapeDtypeStruct((nc, NG, NS, GW), F32)],
        scratch_shapes=[pltpu.VMEM((NS, GW), F32)],
        compiler_params=_cp("parallel", "arbitrary"),
    )(xbc, xbc, xbc, s8, s_t, dt_e, s_e, dexp)


def _ssd_bwd(xbc, s8, s_t, dt_e, s_e, dexp, sprev_all, dy):
    n = xbc.shape[0]
    nc = n // Q
    rc = lambda c: nc - 1 - c

    def body(xs_ref, b_ref, c_ref, sc_ref, sr_ref, dte_ref, se_ref, dexp_ref, sp_ref, dy_ref,
             dxs_ref, db_ref, dc_ref, dse_ref, dde_ref, ts_ref, dsh_ref, pd_ref, dst, dxbuf):
        @pl.when(pl.program_id(1) == 0)
        def _():
            dst[...] = jnp.zeros_like(dst)
            pd_ref[...] = jnp.zeros_like(pd_ref)

        lower, upper = _tri()
        s_c, s_r, dt_e, s_e = sc_ref[...], sr_ref[...], dte_ref[...], se_ref[...]
        xs = xs_ref[...]
        x = xs * dt_e
        xb = x.astype(BF16)
        bb, cb = b_ref[...].astype(BF16), c_ref[...].astype(BF16)
        cbm = _dot(cb, bb, NT)
        cbt = _dot(bb, cb, NT)
        st_e = s_e[Q - 1:Q, :]
        dec_out, dec_st, e_t = jnp.exp(s_e), jnp.exp(st_e - s_e), jnp.exp(st_e)
        dyv = dy_ref[...]
        dyb = dyv.astype(BF16)
        sprev = sp_ref[...]
        sb = sprev.astype(BF16)
        ds_in = dst[...]
        dsb = ds_in.astype(BF16)

        cs = _dot(cb, sb)
        dcs = (dyv * dec_out).astype(BF16)
        d_c = _dot(dcs, sb, NT)
        wf = x * dec_st
        d_w = _dot(bb, dsb)
        d_b = _dot(wf.astype(BF16), dsb, NT)
        tw = d_w * wf
        dse_ref[...] = dyv * cs * dec_out - tw
        ds_c = jnp.zeros((Q, 8), F32)
        dcb = jnp.zeros((Q, Q), F32)
        dcbt = jnp.zeros((Q, Q), F32)
        lane8 = lax.broadcasted_iota(jnp.int32, (1, 8), 1)
        for h in range(HPG):
            sl = slice(h * HD, (h + 1) * HD)
            sc_h, sr_h = s_c[:, h:h + 1], s_r[h:h + 1, :]
            lm = jnp.where(lower, jnp.exp(jnp.minimum(sc_h - sr_h, 0.0)), 0.0)
            lmt = jnp.where(upper, jnp.exp(jnp.minimum(sr_h - sc_h, 0.0)), 0.0)
            mt = cbt * lmt
            dm = _dot(dyb[:, sl], xb[:, sl], NT)
            dmt = _dot(xb[:, sl], dyb[:, sl], NT)
            dxbuf[:, sl] = _dot(mt.astype(BF16), dyb[:, sl])
            dml = dm * lm
            dmlt = dmt * lmt
            dcb = dcb + dml
            dcbt = dcbt + dmlt
            dsh = jnp.sum(dml * cbm, axis=1, keepdims=True) - jnp.sum(dmlt * cbt, axis=1, keepdims=True)
            ds_c = ds_c + dsh * (lane8 == h).astype(F32)
        d_c = d_c + _dot(dcb.astype(BF16), bb)
        d_b = d_b + _dot(dcbt.astype(BF16), cb)
        dx = d_w * dec_st + dxbuf[...]
        ts_ref[...] = jnp.sum(tw, axis=0, keepdims=True) + jnp.sum(ds_in * sprev, axis=0, keepdims=True) * e_t
        dsh_ref[...] = ds_c
        dde_ref[...] = dx * xs
        pd_ref[...] += jnp.sum(dyv * xs, axis=0, keepdims=True)
        dxs_ref[...] = dx * dt_e + dyv * dexp_ref[...]
        db_ref[...] = d_b
        dc_ref[...] = d_c
        dst[...] = e_t * ds_in + _dot(cb, dcs, TN)

    wide = pl.BlockSpec((Q, GW), lambda g, c: (rc(c), g))
    state = pl.BlockSpec((Q, NS), lambda g, c: (rc(c), g))
    in_specs = _ssd_specs(nc, True) + [pl.BlockSpec((None, None, NS, GW), lambda g, c: (rc(c), g, 0, 0)), wide]
    return pl.pallas_call(
        body, name="ssd_bwd", grid=(NG, nc), in_specs=in_specs,
        out_specs=[wide, state, state, wide, wide,
                   pl.BlockSpec((None, 1, GW), lambda g, c: (rc(c), 0, g)),
                   pl.BlockSpec((None, Q, 8), lambda g, c: (g, rc(c), 0)),
                   pl.BlockSpec((None, 1, GW), lambda g, c: (g, 0, 0))],
        out_shape=[jax.ShapeDtypeStruct((n, DI), F32), jax.ShapeDtypeStruct((n, NG * NS), F32),
                   jax.ShapeDtypeStruct((n, NG * NS), F32), jax.ShapeDtypeStruct((n, DI), F32),
                   jax.ShapeDtypeStruct((n, DI), F32), jax.ShapeDtypeStruct((nc, 1, DI), F32),
                   jax.ShapeDtypeStruct((NG, n, 8), F32), jax.ShapeDtypeStruct((NG, 1, GW), F32)],
        scratch_shapes=[pltpu.VMEM((NS, GW), F32), pltpu.VMEM((Q, GW), F32)],
        compiler_params=_cp("parallel", "arbitrary"),
    )(xbc, xbc, xbc, s8, s_t, dt_e, s_e, dexp, sprev_all, dy)


GL = 128


def _gnorm_fwd(y, proj, w):
    n = y.shape[0]

    def body(y_ref, z_ref, w_ref, o_ref):
        for g in range(NG):
            sl = slice(g * GW, (g + 1) * GW)
            yz = y_ref[:, sl] * _silu(z_ref[:, sl])
            r = lax.rsqrt(jnp.mean(yz * yz, axis=-1, keepdims=True) + EPS)
            o_ref[:, sl] = (yz * r * w_ref[:, sl]).astype(BF16)

    row = pl.BlockSpec((GL, DI), lambda i: (i, 0))
    return pl.pallas_call(
        body, name="gnorm_fwd", grid=(n // GL,),
        in_specs=[row, row, pl.BlockSpec((1, DI), lambda i: (0, 0))], out_specs=row,
        out_shape=jax.ShapeDtypeStruct((n, DI), BF16), compiler_params=_cp("parallel"),
    )(y, proj, w)


def _gnorm_bwd(dyn, y, proj, w, after):
    n = y.shape[0]

    def body(d_ref, y_ref, z_ref, w_ref, _, dy_ref, dz_ref, gw_ref):
        @pl.when(pl.program_id(0) == 0)
        def _():
            gw_ref[...] = jnp.zeros_like(gw_ref)

        for g in range(NG):
            sl = slice(g * GW, (g + 1) * GW)
            yv, zv, dv = y_ref[:, sl], z_ref[:, sl], d_ref[:, sl]
            sz = _silu(zv)
            yz = yv * sz
            r = lax.rsqrt(jnp.mean(yz * yz, axis=-1, keepdims=True) + EPS)
            yh = yz * r
            gg = dv * w_ref[:, sl]
            dyz = r * (gg - yh * jnp.mean(gg * yh, axis=-1, keepdims=True))
            gw_ref[:, sl] += jnp.sum(dv * yh, axis=0, keepdims=True)
            dy_ref[:, sl] = dyz * sz
            dz_ref[:, sl] = (dyz * yv * _dsilu(zv)).astype(BF16)

    row = pl.BlockSpec((GL, DI), lambda i: (i, 0))
    vec = pl.BlockSpec((1, DI), lambda i: (0, 0))
    return pl.pallas_call(
        body, name="gnorm_bwd", grid=(n // GL,),
        in_specs=[row, row, row, vec, ANY], out_specs=[row, row, vec],
        out_shape=[jax.ShapeDtypeStruct((n, DI), F32), jax.ShapeDtypeStruct((n, DI), BF16),
                   jax.ShapeDtypeStruct((1, DI), F32)],
        compiler_params=_cp("arbitrary"),
    )(dyn, y, proj, w, after)


SL = 512
SB = 8
SCB = NCH // SB


def _s5_in(proj, bre, bim, after=None):
    n = proj.shape[0]
    uoff = OFF_U // 128

    def body(u_ref, br_ref, bi_ref, *rest):
        or_ref, oi_ref = rest[-2:]
        u = u_ref[...].astype(BF16)
        or_ref[...] = _dot(u, br_ref[...])
        oi_ref[...] = _dot(u, bi_ref[...])

    extra = [] if after is None else [after]
    blk = pl.BlockSpec((None, 128, SCB), lambda i, j: (j, 0, 0))
    out = pl.BlockSpec((SL, SCB), lambda i, j: (i, j))
    return pl.pallas_call(
        body, name="s5_in", grid=(n // SL, SB),
        in_specs=[pl.BlockSpec((SL, 128), lambda i, j: (i, uoff + j)), blk, blk] + [ANY] * len(extra),
        out_specs=[out, out],
        out_shape=[jax.ShapeDtypeStruct((n, NCH), F32)] * 2, compiler_params=_cp("parallel", "parallel"),
    )(proj, bre, bim, *extra)


SC = 256


def _s5_scan(vre, vim, tab, reverse, name):
    n = vre.shape[0]
    nl = n // SL
    ng = SL // 8
    ti = (lambda i: nl - 1 - i) if reverse else (lambda i: i)

    def body(re_ref, im_ref, tab_ref, ore_ref, oim_ref, cre, cim):
        @pl.when(pl.program_id(1) == 0)
        def _():
            cre[...] = jnp.zeros_like(cre)
            cim[...] = jnp.zeros_like(cim)

        def step(j, carry):
            cr, ci = carry
            jj = (ng - 1 - j) if reverse else j
            rows = pl.ds(pl.multiple_of(jj * 8, 8), 8)
            vr, vi = re_ref[rows, :], im_ref[rows, :]
            for t, k in enumerate((1, 2, 4)):
                sh = (8 - k) if reverse else k
                rr, ri = pltpu.roll(vr, sh, 0), pltpu.roll(vi, sh, 0)
                pr, pi = tab_ref[2 * t], tab_ref[2 * t + 1]
                vr, vi = vr + pr * rr - pi * ri, vi + pr * ri + pi * rr
            lr, li = tab_ref[6], tab_ref[7]
            vr, vi = vr + lr * cr - li * ci, vi + lr * ci + li * cr
            ore_ref[rows, :] = vr
            oim_ref[rows, :] = vi
            e = 0 if reverse else 7
            return (jnp.broadcast_to(vr[e:e + 1, :], (8, SC)), jnp.broadcast_to(vi[e:e + 1, :], (8, SC)))

        cr, ci = lax.fori_loop(0, ng, step, (cre[...], cim[...]), unroll=8)
        cre[...] = cr
        cim[...] = ci

    blk = pl.BlockSpec((SL, SC), lambda j, i: (ti(i), j))
    return pl.pallas_call(
        body, name=name, grid=(NCH // SC, nl),
        in_specs=[blk, blk, pl.BlockSpec((8, 8, SC), lambda j, i: (0, 0, j))], out_specs=[blk, blk],
        out_shape=[jax.ShapeDtypeStruct((n, NCH), F32)] * 2,
        scratch_shapes=[pltpu.VMEM((8, SC), F32), pltpu.VMEM((8, SC), F32)],
        compiler_params=_cp("parallel", "arbitrary"),
    )(vre, vim, tab)


def _s5_out(xre, xim, cre, cimn, proj, dvec):
    n = xre.shape[0]
    uoff = OFF_U // 128

    def body(xr_ref, xi_ref, cr_ref, ci_ref, u_ref, d_ref, y_ref, g_ref):
        y = (_dot(xr_ref[...].astype(BF16), cr_ref[...]) + _dot(xi_ref[...].astype(BF16), ci_ref[...])
             + d_ref[...] * u_ref[...])
        y_ref[...] = y
        g_ref[...] = _gelu(y).astype(BF16)

    xs = pl.BlockSpec((SL, SCB), lambda i, j: (i, j))
    blk = pl.BlockSpec((None, SCB, 128), lambda i, j: (j, 0, 0))
    out = pl.BlockSpec((SL, 128), lambda i, j: (i, j))
    return pl.pallas_call(
        body, name="s5_out", grid=(n // SL, SB),
        in_specs=[xs, xs, blk, blk, pl.BlockSpec((SL, 128), lambda i, j: (i, uoff + j)),
                  pl.BlockSpec((1, 128), lambda i, j: (0, j))],
        out_specs=[out, out],
        out_shape=[jax.ShapeDtypeStruct((n, DS5), F32), jax.ShapeDtypeStruct((n, DS5), BF16)],
        compiler_params=_cp("parallel", "parallel"),
    )(xre, xim, cre, cimn, proj, dvec)


def _s5_out_bwd(dg, ypre, crt, cimnt, proj, dvec, xre, xim):
    n = dg.shape[0]
    uoff = OFF_U // 128
    nl = n // SL

    def body(dg_ref, y_ref, cr_ref, ci_ref, u_ref, d_ref, xr_ref, xi_ref,
             gr_ref, gi_ref, dus_ref, gcr_ref, gci_ref, gd_ref):
        dy = dg_ref[...] * _dgelu(y_ref[...])
        dyb = dy.astype(BF16)
        gr_ref[...] = _dot(dyb, cr_ref[...])
        gi_ref[...] = _dot(dyb, ci_ref[...])
        dus_ref[...] = dy * d_ref[...]

        @pl.when(pl.program_id(1) == 0)
        def _():
            gcr_ref[...] = jnp.zeros_like(gcr_ref)
            gci_ref[...] = jnp.zeros_like(gci_ref)
            gd_ref[...] = jnp.zeros_like(gd_ref)

        gcr_ref[...] += _dot(xr_ref[...].astype(BF16), dyb, TN)
        gci_ref[...] -= _dot(xi_ref[...].astype(BF16), dyb, TN)
        gd_ref[...] += jnp.sum(dy * u_ref[...], axis=0, keepdims=True)

    u128 = pl.BlockSpec((SL, 128), lambda j, i: (i, j))
    xs = pl.BlockSpec((SL, SCB), lambda j, i: (i, j))
    blk = pl.BlockSpec((None, 128, SCB), lambda j, i: (j, 0, 0))
    gblk = pl.BlockSpec((None, SCB, 128), lambda j, i: (j, 0, 0))
    vec = pl.BlockSpec((1, 128), lambda j, i: (0, j))
    return pl.pallas_call(
        body, name="s5_out_bwd", grid=(SB, nl),
        in_specs=[u128, u128, blk, blk, pl.BlockSpec((SL, 128), lambda j, i: (i, uoff + j)), vec, xs, xs],
        out_specs=[xs, xs, u128, gblk, gblk, vec],
        out_shape=[jax.ShapeDtypeStruct((n, NCH), F32)] * 2 + [jax.ShapeDtypeStruct((n, DS5), F32)]
        + [jax.ShapeDtypeStruct((SB, SCB, 128), F32)] * 2 + [jax.ShapeDtypeStruct((1, DS5), F32)],
        compiler_params=_cp("parallel", "arbitrary"),
    )(dg, ypre, crt, cimnt, proj, dvec, xre, xim)


def _s5_in_bwd(are, aim, brt, bit, proj, dus, xre, xim):
    n = are.shape[0]
    uoff = OFF_U // 128
    per = SL // 8

    def body(ar_ref, ai_ref, br_ref, bi_ref, u_ref, dus_ref, xr_ref, xi_ref, pr_ref, pi_ref,
             du_ref, gbr_ref, gbi_ref, glr_ref, gli_ref):
        i = pl.program_id(1)
        ar, ai = ar_ref[...], ai_ref[...]
        arb, aib = ar.astype(BF16), ai.astype(BF16)
        du_ref[...] = (_dot(arb, br_ref[...]) + _dot(aib, bi_ref[...]) + dus_ref[...]).astype(BF16)

        @pl.when(i == 0)
        def _():
            for r in (gbr_ref, gbi_ref, glr_ref, gli_ref):
                r[...] = jnp.zeros_like(r)

        ub = u_ref[...].astype(BF16)
        gbr_ref[...] += _dot(arb, ub, TN)
        gbi_ref[...] += _dot(aib, ub, TN)
        row0 = lax.broadcasted_iota(jnp.int32, (SL, SCB), 0) == 0
        last_r = jnp.where(i > 0, pr_ref[7:8, :], 0.0)
        last_i = jnp.where(i > 0, pi_ref[7:8, :], 0.0)
        xpr = jnp.where(row0, last_r, pltpu.roll(xr_ref[...], 1, 0))
        xpi = jnp.where(row0, last_i, pltpu.roll(xi_ref[...], 1, 0))
        glr_ref[...] += jnp.sum(ar * xpr + ai * xpi, axis=0, keepdims=True)
        gli_ref[...] += jnp.sum(ai * xpr - ar * xpi, axis=0, keepdims=True)

    xs = pl.BlockSpec((SL, SCB), lambda j, i: (i, j))
    prev = pl.BlockSpec((8, SCB), lambda j, i: (jnp.maximum(i * per - 1, 0), j))
    blk = pl.BlockSpec((None, SCB, 128), lambda j, i: (j, 0, 0))
    u128 = pl.BlockSpec((SL, 128), lambda j, i: (i, j))
    vec = pl.BlockSpec((1, SCB), lambda j, i: (0, j))
    return pl.pallas_call(
        body, name="s5_in_bwd", grid=(SB, n // SL),
        in_specs=[xs, xs, blk, blk, pl.BlockSpec((SL, 128), lambda j, i: (i, uoff + j)), u128, xs, xs, prev, prev],
        out_specs=[u128, blk, blk, vec, vec],
        out_shape=[jax.ShapeDtypeStruct((n, DS5), BF16)] + [jax.ShapeDtypeStruct((SB, SCB, 128), F32)] * 2
        + [jax.ShapeDtypeStruct((1, NCH), F32)] * 2,
        compiler_params=_cp("parallel", "arbitrary"),
    )(are, aim, brt, bit, proj, dus, xre, xim, xre, xim)


MC = 1024


def _merge_specs():
    ga = pl.BlockSpec((TL, MC), lambda i, j: (i, OFF_GA // MC + j))
    gb = pl.BlockSpec((TL, MC), lambda i, j: (i, OFF_GB // MC + j))
    col = pl.BlockSpec((TL, MC), lambda i, j: (i, j))
    gate = pl.BlockSpec((TL, MC), lambda i, j: (i, D // MC + j))
    return ga, gb, col, gate


def _merge_fwd(proj, ya, vg):
    n = ya.shape[0]

    def body(ga_ref, gb_ref, ya_ref, v_ref, g_ref, o_ref):
        yb = v_ref[...] * _sig(g_ref[...])
        o_ref[...] = (_sig(ga_ref[...]) * ya_ref[...] + _sig(gb_ref[...]) * yb).astype(BF16)

    ga, gb, col, gate = _merge_specs()
    return pl.pallas_call(
        body, name="merge_fwd", grid=(n // TL, D // MC), in_specs=[ga, gb, col, col, gate], out_specs=col,
        out_shape=jax.ShapeDtypeStruct((n, D), BF16), compiler_params=_cp("parallel", "parallel"),
    )(proj, proj, ya, vg, vg)


def _merge_bwd(dm, proj, ya, vg):
    n = ya.shape[0]

    def body(dm_ref, ga_ref, gb_ref, ya_ref, v_ref, g_ref, dga_ref, dgb_ref, dya_ref, dv_ref, dg_ref):
        d = dm_ref[...]
        sa, sb, sg = _sig(ga_ref[...]), _sig(gb_ref[...]), _sig(g_ref[...])
        v = v_ref[...]
        yb = v * sg
        dga_ref[...] = (d * ya_ref[...] * sa * (1.0 - sa)).astype(BF16)
        dgb_ref[...] = (d * yb * sb * (1.0 - sb)).astype(BF16)
        dya_ref[...] = (d * sa).astype(BF16)
        dyb = d * sb
        dv_ref[...] = (dyb * sg).astype(BF16)
        dg_ref[...] = (dyb * v * sg * (1.0 - sg)).astype(BF16)

    ga, gb, col, gate = _merge_specs()
    o = jax.ShapeDtypeStruct((n, D), BF16)
    return pl.pallas_call(
        body, name="merge_bwd", grid=(n // TL, D // MC), in_specs=[col, ga, gb, col, col, gate],
        out_specs=[col] * 5, out_shape=[o] * 5, compiler_params=_cp("parallel", "parallel"),
    )(dm, proj, proj, ya, vg, vg)


def _adamw_update(wv, gv, mv, vv):
    nm = B1 * mv + (1.0 - B1) * gv
    nv = B2 * vv + (1.0 - B2) * (gv * gv)
    m_hat = nm / (1.0 - B1 ** STEP)
    v_hat = nv / (1.0 - B2 ** STEP)
    return -LR * (m_hat / (jnp.sqrt(v_hat) + AEPS) + WD * wv), nm, nv


def _adamw(w, g, m, v, name):
    r, c = w.shape
    tr = _pick(r, 128)

    def body(w_ref, g_ref, m_ref, v_ref, d_ref, nm_ref, nv_ref):
        d_ref[...], nm_ref[...], nv_ref[...] = _adamw_update(w_ref[...], g_ref[...], m_ref[...], v_ref[...])

    blk = pl.BlockSpec((tr, c), lambda i: (i, 0))
    o = jax.ShapeDtypeStruct((r, c), F32)
    return pl.pallas_call(
        body, name=name, grid=(r // tr,), in_specs=[blk] * 4, out_specs=[blk] * 3, out_shape=[o] * 3,
        compiler_params=_cp("parallel"),
    )(w, g, m, v)


def _adamw_halves(w, g_mine, g_other, m, v, cidx, name):
    _, r, c = w.shape
    hr, gc = g_mine.shape
    tr = _pick(hr, 128)
    nbh = hr // tr
    assert gc == c and 2 * hr - tr < r <= 2 * hr

    def body(cs, w_ref, gm_ref, go_ref, m_ref, v_ref, g_ref, d_ref, nm_ref, nv_ref):
        mine = pl.program_id(0) // nbh == cs[0]
        gv = jnp.where(mine, gm_ref[...], go_ref[...])
        g_ref[...] = gv
        d_ref[...], nm_ref[...], nv_ref[...] = _adamw_update(w_ref[...], gv, m_ref[...], v_ref[...])

    blk = pl.BlockSpec((None, tr, c), lambda i, cs: (0, i, 0))
    gmine = pl.BlockSpec((tr, gc), lambda i, cs: (jnp.where(i // nbh == cs[0], i % nbh, 0), 0))
    gother = pl.BlockSpec((tr, gc), lambda i, cs: (jnp.where(i // nbh == cs[0], 0, i % nbh), 0))
    o = jax.ShapeDtypeStruct((1, r, c), F32)
    return pl.pallas_call(
        body, name=name,
        grid_spec=pltpu.PrefetchScalarGridSpec(num_scalar_prefetch=1, grid=(2 * nbh,),
                                               in_specs=[blk, gmine, gother, blk, blk], out_specs=[blk] * 4),
        out_shape=[o] * 4, compiler_params=_cp("parallel"),
    )(cidx, w, g_mine, g_other, m, v)


def _chip_sum(part, sib, cidx, name):
    _, r, cc = part.shape
    hr = r // 2
    tr = _pick(hr, 256)

    def body(cs, p_ref, s_ref, o_ref):
        o_ref[...] = (p_ref[...].astype(F32) + s_ref[...].astype(F32)).astype(BF16)

    blk = pl.BlockSpec((None, tr, cc), lambda k, i, cs: (k, i, 0))
    return pl.pallas_call(
        body, name=name,
        grid_spec=pltpu.PrefetchScalarGridSpec(
            num_scalar_prefetch=1, grid=(4, hr // tr),
            in_specs=[pl.BlockSpec((None, None, tr, cc), lambda k, i, cs: (k, cs[0], i, 0)), blk], out_specs=blk),
        out_shape=jax.ShapeDtypeStruct((4, hr, cc), BF16), compiler_params=_cp("parallel", "parallel"),
    )(cidx, part.reshape(4, 2, hr, cc), sib)


def _shard_sum(own, got, sidx, name):
    _, hr, cc = own.shape
    tr = _pick(hr, 256)

    def body(cs, own_ref, g0, g1, g2, g3, o_ref):
        acc = None
        for k, g_ref in enumerate((g0, g1, g2, g3)):
            term = jnp.where(cs[0] == k, own_ref[...], g_ref[...]).astype(F32)
            acc = term if acc is None else acc + term
        o_ref[...] = acc

    def got_spec(k):
        return pl.BlockSpec((None, tr, cc), lambda i, cs: (jnp.where(cs[0] == k, (k + 1) % 4, k), i, 0))

    return pl.pallas_call(
        body, name=name,
        grid_spec=pltpu.PrefetchScalarGridSpec(
            num_scalar_prefetch=1, grid=(hr // tr,),
            in_specs=[pl.BlockSpec((None, tr, cc), lambda i, cs: (cs[0], i, 0))] + [got_spec(k) for k in range(4)],
            out_specs=pl.BlockSpec((tr, cc), lambda i, cs: (i, 0))),
        out_shape=jax.ShapeDtypeStruct((hr, cc), F32), compiler_params=_cp("parallel"),
    )(sidx, own, got, got, got, got)


def _sum_slabs(xs, name, out_dtype=F32):
    r, c = xs[0].shape
    tr = _pick(r, 256)

    def body(*refs):
        acc = refs[0][...].astype(F32)
        for ref in refs[1:-1]:
            acc = acc + ref[...].astype(F32)
        refs[-1][...] = acc.astype(out_dtype)

    blk = pl.BlockSpec((tr, c), lambda i: (i, 0))
    return pl.pallas_call(
        body, name=name, grid=(r // tr,), in_specs=[blk] * len(xs), out_specs=blk,
        out_shape=jax.ShapeDtypeStruct((r, c), out_dtype), compiler_params=_cp("parallel"),
    )(*xs)


def _place():
    return lax.axis_index("x"), lax.axis_index("y"), lax.axis_index("c")


def _gather_small(v, after):
    m_per, n = v.shape

    def body(x_ref, _, out_ref, send_sems, recv_sems, local_sem):
        x, y, c = _place()
        me, sibling = (x, y, c), (x, y, 1 - c)
        chips = [(1 - x, y), (x, 1 - y), (1 - x, 1 - y)]

        def rows(px, py, pc):
            return out_ref.at[pl.ds((4 * px + 2 * py + pc) * m_per, m_per), :]

        def copy(k, block, to, src=None):
            return pltpu.make_async_remote_copy(
                src_ref=rows(*block) if src is None else src, dst_ref=rows(*block),
                send_sem=send_sems.at[k], recv_sem=recv_sems.at[k], device_id=to, device_id_type=MESH)

        mine = pltpu.make_async_copy(x_ref, rows(*me), local_sem)
        mine.start()
        first = [copy(0, me, sibling, src=x_ref)]
        first += [copy(1 + j, me, (*chip, c), src=x_ref) for j, chip in enumerate(chips)]
        for cp in first:
            cp.start()
        passed = [copy(4 + j, (*chip, c), sibling) for j, chip in enumerate(chips)]
        for j, chip in enumerate(chips):
            copy(1 + j, (*chip, c), me).wait_recv()
            passed[j].start()
        copy(0, sibling, me).wait_recv()
        for j, chip in enumerate(chips):
            copy(4 + j, (*chip, 1 - c), me).wait_recv()
        for cp in first + passed:
            cp.wait_send()
        mine.wait()

    return pl.pallas_call(
        body, name="gather_small_%d" % m_per,
        out_shape=jax.ShapeDtypeStruct((8 * m_per, n), v.dtype),
        in_specs=[pl.BlockSpec(memory_space=pltpu.VMEM), ANY], out_specs=pl.BlockSpec(memory_space=pltpu.VMEM),
        scratch_shapes=[pltpu.SemaphoreType.DMA((7,)), pltpu.SemaphoreType.DMA((7,)), pltpu.SemaphoreType.DMA],
        compiler_params=pltpu.CompilerParams(vmem_limit_bytes=VMEM_LIMIT),
    )(v, after)


def _allsum_small(v, name, after):
    r = v.shape[0]
    g = _gather_small(v, after)
    return _sum_slabs([g[k * r:(k + 1) * r] for k in range(8)], name)


def _gather_big(shards):
    nt = len(shards)

    def body(*refs):
        ins, outs = refs[:nt], refs[nt:2 * nt]
        send_sems, recv_sems = refs[2 * nt:]
        x, y, c = _place()
        s = 2 * x + y
        sibling = (x, y, 1 - c)
        chips = [(1 - x, y), (x, 1 - y), (1 - x, 1 - y)]

        def half(t, slot, h):
            hr = ins[t].shape[0] // 2
            return outs[t].at[slot, pl.ds(h * hr, hr), :]

        def ici(t, j, src_slot, to):
            hr = ins[t].shape[0] // 2
            return pltpu.make_async_remote_copy(
                src_ref=ins[t].at[pl.ds(c * hr, hr), :], dst_ref=half(t, src_slot, c),
                send_sem=send_sems.at[7 * t + j], recv_sem=recv_sems.at[7 * t + j], device_id=to, device_id_type=MESH)

        def d2d(t, j, slot, h):
            return pltpu.make_async_remote_copy(
                src_ref=half(t, slot, h), dst_ref=half(t, slot, h),
                send_sem=send_sems.at[7 * t + 3 + j], recv_sem=recv_sems.at[7 * t + 3 + j],
                device_id=sibling, device_id_type=MESH)

        def whole(t):
            return pltpu.make_async_remote_copy(
                src_ref=ins[t], dst_ref=outs[t].at[s], send_sem=send_sems.at[7 * t + 6],
                recv_sem=recv_sems.at[7 * t + 6], device_id=sibling, device_id_type=MESH)

        sends = [ici(t, j, s, (*chip, c)) for t in range(nt) for j, chip in enumerate(chips)]
        sends += [whole(t) for t in range(nt)]
        for cp in sends:
            cp.start()
        passed = []
        for t in range(nt):
            for j, (px, py) in enumerate(chips):
                ici(t, j, 2 * px + py, (x, y, c)).wait_recv()
                cp = d2d(t, j, 2 * px + py, c)
                cp.start()
                passed.append(cp)
        for t in range(nt):
            for j, (px, py) in enumerate(chips):
                d2d(t, j, 2 * px + py, 1 - c).wait_recv()
            whole(t).wait_recv()
        for cp in sends + passed:
            cp.wait_send()

    return pl.pallas_call(
        body, name="gather_big",
        out_shape=[jax.ShapeDtypeStruct((4,) + a.shape, a.dtype) for a in shards],
        in_specs=[ANY] * nt, out_specs=[ANY] * nt,
        scratch_shapes=[pltpu.SemaphoreType.DMA((7 * nt,)), pltpu.SemaphoreType.DMA((7 * nt,))],
    )(*shards)


def _swap_halves(parts, name):
    nt = len(parts)

    def body(*refs):
        ins, outs = refs[:nt], refs[nt:2 * nt]
        send_sems, recv_sems = refs[2 * nt:]
        x, y, c = _place()
        cps = []
        for t in range(nt):
            hr = ins[t].shape[1] // 2
            cps.append(pltpu.make_async_remote_copy(
                src_ref=ins[t].at[:, pl.ds((1 - c) * hr, hr), :], dst_ref=outs[t],
                send_sem=send_sems.at[t], recv_sem=recv_sems.at[t], device_id=(x, y, 1 - c), device_id_type=MESH))
        for cp in cps:
            cp.start()
        for cp in cps:
            cp.wait()

    return pl.pallas_call(
        body, name=name,
        out_shape=[jax.ShapeDtypeStruct((4, a.shape[1] // 2, a.shape[2]), a.dtype) for a in parts],
        in_specs=[ANY] * nt, out_specs=[ANY] * nt,
        scratch_shapes=[pltpu.SemaphoreType.DMA((nt,)), pltpu.SemaphoreType.DMA((nt,))],
    )(*parts)


def _scatter_chips(parts):
    nt = len(parts)

    def body(*refs):
        ins, outs = refs[:nt], refs[nt:2 * nt]
        send_sems, recv_sems = refs[2 * nt:]
        x, y, c = _place()
        s = 2 * x + y
        chips = [(1 - x, y), (x, 1 - y), (1 - x, 1 - y)]
        cps = []
        for t in range(nt):
            for j, (px, py) in enumerate(chips):
                cps.append(pltpu.make_async_remote_copy(
                    src_ref=ins[t].at[2 * px + py], dst_ref=outs[t].at[s],
                    send_sem=send_sems.at[3 * t + j], recv_sem=recv_sems.at[3 * t + j],
                    device_id=(px, py, c), device_id_type=MESH))
        for cp in cps:
            cp.start()
        for t in range(nt):
            for j, (px, py) in enumerate(chips):
                pltpu.make_async_remote_copy(
                    src_ref=ins[t].at[s], dst_ref=outs[t].at[2 * px + py],
                    send_sem=send_sems.at[3 * t + j], recv_sem=recv_sems.at[3 * t + j],
                    device_id=(px, py, c), device_id_type=MESH).wait_recv()
        for cp in cps:
            cp.wait_send()

    return pl.pallas_call(
        body, name="scatter_chips",
        out_shape=[jax.ShapeDtypeStruct(a.shape, a.dtype) for a in parts],
        in_specs=[ANY] * nt, out_specs=[ANY] * nt,
        scratch_shapes=[pltpu.SemaphoreType.DMA((3 * nt,)), pltpu.SemaphoreType.DMA((3 * nt,))],
    )(*parts)


def _swap_whole(halves):
    nt = len(halves)

    def body(*refs):
        ins, outs = refs[:nt], refs[nt:2 * nt]
        send_sems, recv_sems = refs[2 * nt:]
        x, y, c = _place()
        cps = [pltpu.make_async_remote_copy(
            src_ref=ins[t], dst_ref=outs[t], send_sem=send_sems.at[t], recv_sem=recv_sems.at[t],
            device_id=(x, y, 1 - c), device_id_type=MESH) for t in range(nt)]
        for cp in cps:
            cp.start()
        for cp in cps:
            cp.wait()

    return pl.pallas_call(
        body, name="swap_whole",
        out_shape=[jax.ShapeDtypeStruct(a.shape, a.dtype) for a in halves],
        in_specs=[ANY] * nt, out_specs=[ANY] * nt,
        scratch_shapes=[pltpu.SemaphoreType.DMA((nt,)), pltpu.SemaphoreType.DMA((nt,))],
    )(*halves)


def _pass_halves(got, shards, name):
    nt = len(got)

    def body(*refs):
        ins, own, outs = refs[:nt], refs[nt:2 * nt], refs[2 * nt:3 * nt]
        send_sems, recv_sems = refs[3 * nt:]
        x, y, c = _place()
        s = 2 * x + y
        chips = [(1 - x, y), (x, 1 - y), (1 - x, 1 - y)]

        def half(ref, t, slot, h):
            hr = ins[t].shape[1] // 2
            return ref.at[slot, pl.ds(h * hr, hr), :]

        def copy(t, j, h):
            px, py = chips[j]
            return pltpu.make_async_remote_copy(
                src_ref=half(ins[t], t, 2 * px + py, h), dst_ref=half(outs[t], t, 2 * px + py, h),
                send_sem=send_sems.at[4 * t + j], recv_sem=recv_sems.at[4 * t + j],
                device_id=(x, y, 1 - c), device_id_type=MESH)

        def whole(t):
            return pltpu.make_async_remote_copy(
                src_ref=own[t], dst_ref=outs[t].at[s], send_sem=send_sems.at[4 * t + 3],
                recv_sem=recv_sems.at[4 * t + 3], device_id=(x, y, 1 - c), device_id_type=MESH)

        sends = [copy(t, j, c) for t in range(nt) for j in range(3)] + [whole(t) for t in range(nt)]
        for cp in sends:
            cp.start()
        for t in range(nt):
            for j in range(3):
                copy(t, j, 1 - c).wait_recv()
            whole(t).wait_recv()
        for cp in sends:
            cp.wait_send()

    return pl.pallas_call(
        body, name=name,
        out_shape=[jax.ShapeDtypeStruct(a.shape, a.dtype) for a in got],
        in_specs=[ANY] * (2 * nt), out_specs=[ANY] * nt, input_output_aliases={t: t for t in range(nt)},
        scratch_shapes=[pltpu.SemaphoreType.DMA((4 * nt,)), pltpu.SemaphoreType.DMA((4 * nt,))],
    )(*got, *shards)


HBM = pl.BlockSpec(memory_space=pltpu.HBM)
SEM = pl.BlockSpec(memory_space=pltpu.SEMAPHORE)
EFFECT = pltpu.SideEffectType.DATAFLOW_SIDE_EFFECTING


PER_TENSOR = {"gather": 3, "scatter": 3, "swap": 1, "pass": 4, "whole": 1}


def _ici_copies(kind, srcs, lands, send_sems, recv_sems):
    x, y, c = _place()
    s = 2 * x + y
    sib = (x, y, 1 - c)
    chips = [(1 - x, y), (x, 1 - y), (1 - x, 1 - y)]
    cps = []

    def add(src, dst, dev):
        k = len(cps)
        cps.append(pltpu.make_async_remote_copy(src_ref=src, dst_ref=dst, send_sem=send_sems[k], recv_sem=recv_sems[k],
                                                device_id=dev, device_id_type=MESH))

    for t in range(len(srcs)):
        if kind == "gather":
            hr = srcs[t].shape[0] // 2
            for px, py in chips:
                add(srcs[t].at[pl.ds(c * hr, hr), :], lands[t].at[s, pl.ds(c * hr, hr), :], (px, py, c))
        elif kind == "scatter":
            for px, py in chips:
                add(srcs[t].at[2 * px + py], lands[t].at[s], (px, py, c))
        elif kind == "swap":
            hr = srcs[t].shape[1] // 2
            add(srcs[t].at[:, pl.ds((1 - c) * hr, hr), :], lands[t], sib)
        elif kind == "pass":
            hr = srcs[t].shape[1] // 2
            for px, py in chips:
                half = srcs[t].at[2 * px + py, pl.ds(c * hr, hr), :]
                add(half, half, sib)
            add(lands[t], srcs[t].at[s], sib)
        else:
            add(srcs[t], lands[t], sib)
    return cps


def _ici_start(kind, srcs, after, name, lands=None):
    nt = len(srcs)
    nc = PER_TENSOR[kind] * nt
    hbm = lambda a: pltpu.with_memory_space_constraint(a, pltpu.HBM)
    if lands is None:
        shape = {"gather": lambda a: (4,) + a.shape, "scatter": lambda a: a.shape,
                 "swap": lambda a: (4, a.shape[1] // 2, a.shape[2]), "whole": lambda a: a.shape}[kind]
        lands = [lax.empty(shape(a), a.dtype) for a in srcs]

    def body(*refs):
        src, land = refs[:nt], refs[nt:2 * nt]
        outs = refs[2 * nt + 1:]
        for cp in _ici_copies(kind, src, land, outs[:nc], outs[nc:2 * nc]):
            cp.start()
        outs[-1][...] = jnp.zeros_like(outs[-1])

    outs = pl.pallas_call(
        body, name=name,
        out_shape=tuple([pltpu.SemaphoreType.DMA(())] * (2 * nc) + [pltpu.HBM(a.shape, a.dtype) for a in srcs]
                        + [pltpu.HBM(a.shape, a.dtype) for a in lands] + [jax.ShapeDtypeStruct((8, 128), F32)]),
        in_specs=[HBM] * (2 * nt) + [ANY],
        out_specs=tuple([SEM] * (2 * nc) + [HBM] * (2 * nt) + [pl.BlockSpec(memory_space=pltpu.VMEM)]),
        input_output_aliases={i: 2 * nc + i for i in range(2 * nt)},
        compiler_params=pltpu.CompilerParams(has_side_effects=EFFECT),
    )(*[hbm(a) for a in srcs], *[hbm(a) for a in lands], after)
    return outs[:2 * nc], outs[2 * nc:2 * nc + nt], outs[2 * nc + nt:2 * nc + 2 * nt], outs[-1]


def _ici_wait(kind, sems, srcs, lands, after, name):
    nt = len(srcs)
    nc = PER_TENSOR[kind] * nt

    def body(*refs):
        src, land = refs[:nt], refs[nt:2 * nt]
        sem = refs[2 * nt:2 * nt + 2 * nc]
        for cp in _ici_copies(kind, src, land, sem[:nc], sem[nc:]):
            cp.wait_send()
            cp.wait_recv()

    outs = pl.pallas_call(
        body, name=name,
        out_shape=tuple(pltpu.HBM(a.shape, a.dtype) for a in list(srcs) + list(lands)),
        in_specs=[HBM] * (2 * nt) + [SEM] * (2 * nc) + [ANY], out_specs=tuple([HBM] * (2 * nt)),
        input_output_aliases={i: i for i in range(2 * nt)},
        compiler_params=pltpu.CompilerParams(has_side_effects=EFFECT),
    )(*srcs, *lands, *sems, after)
    return outs[:nt], outs[nt:]


def _s5_params(lam_re, lam_im, log_dt, b_re, b_im):
    lr = jnp.minimum(lam_re, EIG_MAX)
    dt = jnp.exp(log_dt)[:, None]
    mag = jnp.exp(lr * dt)
    lbr, lbi = mag * jnp.cos(lam_im * dt), mag * jnp.sin(lam_im * dt)
    den = lr * lr + lam_im * lam_im
    qr = ((lbr - 1.0) * lr + lbi * lam_im) / den
    qi = (lbi * lr - (lbr - 1.0) * lam_im) / den
    bbr = qr[..., None] * b_re - qi[..., None] * b_im
    bbi = qr[..., None] * b_im + qi[..., None] * b_re
    return lbr, lbi, bbr, bbi


def _cmul(a, b):
    return a[0] * b[0] - a[1] * b[1], a[0] * b[1] + a[1] * b[0]


def _scan_table(lr, li, reverse):
    l1 = (lr.reshape(1, NCH), li.reshape(1, NCH))
    pows = [l1]
    for _ in range(7):
        pows.append(_cmul(pows[-1], l1))
    r = jnp.arange(8)[:, None]
    tabs = []
    for k in (1, 2, 4):
        keep = (r < 8 - k) if reverse else (r >= k)
        tabs += [jnp.where(keep, pows[k - 1][0], 0.0), jnp.where(keep, pows[k - 1][1], 0.0)]
    order = range(7, -1, -1) if reverse else range(8)
    tabs += [jnp.concatenate([pows[e][0] for e in order], axis=0), jnp.concatenate([pows[e][1] for e in order], axis=0)]
    return jnp.stack(tabs).astype(F32)


_EYE8 = lambda: jnp.eye(8, dtype=F32)


def _to_in_blocks(b):
    return jnp.einsum("jgpc,gh->jgchp", b.reshape(8, 8, 64, 16), _EYE8()).reshape(8, 128, 512)


def _to_out_blocks(cm):
    return jnp.einsum("jgcp,gh->jgphc", cm.reshape(8, 8, 16, 64), _EYE8()).reshape(8, 512, 128)


def _from_out_blocks(g):
    return jnp.einsum("jgphc,gh->jgpc", g.reshape(8, 8, 64, 8, 16), _EYE8()).reshape(64, 64, 16)


def _local_step(x, target, hn1, proj, p, hooks):
    n = x.shape[0]
    g = {}
    dtraw = proj
    xbc = _conv_a_fwd(proj, p["conv_a_w"], p["conv_a_b"])
    to_lanes = lambda v: jnp.pad(jnp.pad(v.reshape(NG, HPG), ((0, 0), (0, 8 - HPG))).reshape(1, 8 * NG),
                                 ((0, 0), (0, HL - 8 * NG)))
    from_lanes = lambda v: v[:, :8 * NG].reshape(-1, NG, 8)[:, :, :HPG].reshape(-1, NG * HPG)
    hp = jnp.concatenate([to_lanes(p["dt_bias"]), to_lanes(p["a_log"]), jnp.zeros((6, HL), F32)], axis=0)
    lane = jnp.arange(HL)[:, None]
    emat = ((lane < 8 * NG) & (lane % 8 < HPG)
            & (jnp.arange(DI)[None, :] // HD == HPG * (lane // 8) + lane % 8)).astype(BF16)
    dexp = jnp.repeat(p["d_a"].reshape(1, NG * HPG), HD, axis=1)
    dt, s_cum, s_t, dt_e, s_e = _ssd_prep(dtraw, hp, emat)
    s8 = s_cum[:, :8 * NG].reshape(n, NG, 8).transpose(1, 0, 2)
    yssd, sprev = _ssd_fwd(xbc, s8, s_t, dt_e, s_e, dexp)
    yn = _gnorm_fwd(yssd, proj, p["norm_a_w"])
    tok = hooks["late_start"](yn)
    (lbr, lbi, bbr, bbi), s5_vjp = jax.vjp(_s5_params, p["s5_lam_re"], p["s5_lam_im"], p["s5_log_dt"],
                                           p["s5_b_re"], p["s5_b_im"])
    bin_r, bin_i = _to_in_blocks(bbr), _to_in_blocks(bbi)
    cout_r, cout_in = _to_out_blocks(p["s5_c_re"]), _to_out_blocks(-p["s5_c_im"])
    bur, bui = _s5_in(proj, bin_r.astype(BF16), bin_i.astype(BF16), after=tok)
    xre, xim = _s5_scan(bur, bui, _scan_table(lbr, lbi, False), False, "s5_scan_fwd")
    ypre, g5 = _s5_out(xre, xim, cout_r.astype(BF16), cout_in.astype(BF16), proj, p["s5_d"])
    p = {**p, **hooks["late_weights"](ypre)}
    ya = _matmul(yn, p["w_proj_a"], "nn", "mm_proj")
    vg = _matmul(g5, p["w_s5_glu"], "nn", "mm_glu", b_stacked=True)
    merged = _merge_fwd(proj, ya, vg)
    h1 = _matmul(merged, p["w_out"], "nn", "mm_out", residual=x)
    hn2 = _rms_fwd(h1, p["norm_ffn_w"], "rms_ffn")
    up = _matmul(hn2, p["w_up"], "nn", "mm_up", tn=1408, b_stacked=True)
    act = _conv_ffn_fwd(up, p["conv_ffn_w"], p["conv_ffn_b"])
    h2 = _matmul(act, p["w_down"], "nn", "mm_down", tk=DFF // 2, residual=h1)
    dh2, dh2b, g["norm_final_w"], loss_blk = _final(h2, p["norm_final_w"], target)
    g["w_down"] = _matmul(act, dh2b, "tn", "mm_gw_down", out_dtype=BF16).reshape(4, DFF // 4, D)
    dact = _matmul(dh2b, p["w_down"], "nt", "mm_dact", out_dtype=BF16)
    dup, g["conv_ffn_w"], g["conv_ffn_b"] = _conv_ffn_bwd(up, dact, p["conv_ffn_w"], p["conv_ffn_b"])
    g["w_up"] = _matmul(hn2, dup, "tn", "mm_gw_up", out_dtype=BF16, tn=1408, out_stacked=True)
    tok = hooks["swap_start"](["w_up", "w_down"], g, "s1")
    dhn2 = _matmul(dup, p["w_up"], "nt", "mm_dhn2", tk=2816, b_stacked=True, after=tok)
    tok = hooks["scatter_go"]("s1", dhn2)
    dh1, dh1b, g["norm_ffn_w"] = _rms_bwd(dhn2, h1, p["norm_ffn_w"], dh2, "rms_ffn_bwd", after=tok)
    g["w_out"] = _matmul(merged, dh1b, "tn", "mm_gw_out", out_dtype=BF16).reshape(4, D // 4, D)
    dmerged = _matmul(dh1b, p["w_out"], "nt", "mm_dmerged")
    dga, dgb, dya, dval, dgate = _merge_bwd(dmerged, proj, ya, vg)
    dvg = jnp.concatenate([dval, dgate], axis=1)
    g["w_s5_glu"] = _matmul(g5, dvg, "tn", "mm_gw_glu", out_dtype=BF16, out_stacked=True)
    dg5 = _matmul(dvg, p["w_s5_glu"], "nt", "mm_dg5", b_stacked=True)
    tr = lambda b: b.transpose(0, 2, 1)
    gxr, gxi, dus, gcr, gci, g["s5_d"] = _s5_out_bwd(dg5, ypre, tr(cout_r).astype(BF16), tr(cout_in).astype(BF16),
                                                     proj, p["s5_d"], xre, xim)
    are, aim = _s5_scan(gxr, gxi, _scan_table(lbr, -lbi, True), True, "s5_scan_bwd")
    du, gbr, gbi, glr, gli = _s5_in_bwd(are, aim, tr(bin_r).astype(BF16), tr(bin_i).astype(BF16), proj, dus, xre, xim)
    g["s5_c_re"] = _from_out_blocks(gcr).transpose(0, 2, 1)
    g["s5_c_im"] = _from_out_blocks(gci).transpose(0, 2, 1)
    (g["s5_lam_re"], g["s5_lam_im"], g["s5_log_dt"], g["s5_b_re"], g["s5_b_im"]) = s5_vjp(
        (glr.reshape(64, 64), gli.reshape(64, 64), _from_out_blocks(gbr), _from_out_blocks(gbi)))
    g["w_proj_a"] = _matmul(yn, dya, "tn", "mm_gw_proj", out_dtype=BF16).reshape(4, DI // 4, D)
    tok = hooks["swap_start"](["w_proj_a", "w_s5_glu", "w_out"], g, "s2")
    dyn = _matmul(dya, p["w_proj_a"], "nt", "mm_dyn", after=tok)
    tok = hooks["scatter_go"]("s2", dyn)
    dyssd, dz, g["norm_a_w"] = _gnorm_bwd(dyn, yssd, proj, p["norm_a_w"], tok)
    dxs, dbm, dcm, ds_e, ddt_e, tsum, dsh8, pd = _ssd_bwd(xbc, s8, s_t, dt_e, s_e, dexp, sprev, dyssd)
    dsh = jnp.pad(dsh8.transpose(1, 0, 2).reshape(n, 8 * NG), ((0, 0), (0, HL - 8 * NG)))
    draw, ps = _ssd_post(ds_e, ddt_e, tsum, dsh, dtraw, dt, hp, emat.T)
    g["dt_bias"] = from_lanes(ps[0:1])
    g["a_log"] = from_lanes(ps[1:2])
    g["d_a"] = pd.reshape(NG * HPG, HD).sum(axis=1).reshape(1, NG * HPG)
    ddt = draw.astype(BF16)
    dxbc_parts, gcw, gcb = [], [], []
    for arr, col0, nm in ((dxs, 0, "conv_a_bwd_x"), (dbm, DI, "conv_a_bwd_b"), (dcm, DI + NG * NS, "conv_a_bwd_c")):
        dpart, gw_, gb_ = _conv_a_bwd(proj, arr, p["conv_a_w"], p["conv_a_b"], col0, nm)
        dxbc_parts.append(dpart)
        gcw.append(gw_)
        gcb.append(gb_)
    g["conv_a_w"] = jnp.concatenate(gcw, axis=1)
    g["conv_a_b"] = jnp.concatenate(gcb, axis=1)
    dproj = jnp.concatenate([dz] + dxbc_parts + [du, dga, dgb, ddt], axis=1)
    g_main = _matmul(dproj, hn1, "tn", "mm_gw_in", out_dtype=BF16, tm=896, tn=2048)
    g_dt = g_main[NMAIN:NMAIN + 8 * NG].reshape(NG, 8, D)[:, :HPG].reshape(NG * HPG, D)
    g_sh = _move_rows(g_main, RUNS_TO_SHARDS, 4 * WPAD, MT, MT, "rows_to_shards")
    g["w_in"] = lax.dynamic_update_slice(g_sh, g_dt, (DT_SHARD_ROW, 0)).reshape(4, WPAD, D)
    hooks["swap_start"](["w_in"], g, "s3")
    tok = hooks["scatter_go"]("s3", g_dt)
    dhn1 = _matmul(dproj, p["w_full"], "nn", "mm_dhn1", tk=2688, after=tok)
    gx, _, g["norm_mix_w"] = _rms_bwd(dhn1, x, p["norm_mix_w"], dh1, "rms_mix_bwd")
    return loss_blk, gx, g


BIG = ["w_in", "w_proj_a", "w_s5_glu", "w_out", "w_up", "w_down"]
SMALL = ["norm_mix_w", "conv_a_w", "conv_a_b", "dt_bias", "a_log", "d_a", "norm_a_w", "s5_lam_re", "s5_lam_im",
         "s5_log_dt", "s5_b_re", "s5_b_im", "s5_c_re", "s5_c_im", "s5_d", "norm_ffn_w", "conv_ffn_w", "conv_ffn_b",
         "norm_final_w"]
ORDER = ["norm_mix_w", "w_in", "conv_a_w", "conv_a_b", "dt_bias", "a_log", "d_a", "norm_a_w", "w_proj_a", "s5_lam_re",
         "s5_lam_im", "s5_log_dt", "s5_b_re", "s5_b_im", "s5_c_re", "s5_c_im", "s5_d", "w_s5_glu", "w_out",
         "norm_ffn_w", "w_up", "conv_ffn_w", "conv_ffn_b", "w_down", "norm_final_w"]
CONV_FULL = {"conv_a_w": (KA, CONVD), "conv_ffn_w": (KF, 2 * DFF)}


def _pack(arrs):
    flat = jnp.concatenate([a.reshape(-1).astype(F32) for a in arrs])
    total = flat.shape[0]
    padded = -(-total // 1024) * 1024
    return jnp.pad(flat, (0, padded - total)).reshape(padded // 128, 128)


def _unpack(block, shapes):
    flat = block.reshape(-1)
    out, at = [], 0
    for sh in shapes:
        size = math.prod(sh)
        out.append(flat[at:at + size].reshape(sh))
        at += size
    return out


def _stack_cols(a):
    return a.transpose(1, 0, 2).reshape(a.shape[1], 4 * a.shape[2])


def _unstack_cols(a):
    return a.reshape(a.shape[0], 4, a.shape[1] // 4).transpose(1, 0, 2)


def kernel(x, norm_mix_w, w_in, conv_a_w, conv_a_b, dt_bias, a_log, d_a, norm_a_w, w_proj_a, s5_lam_re, s5_lam_im, s5_log_dt, s5_b_re, s5_b_im, s5_c_re, s5_c_im, s5_d, w_s5_glu, w_out, norm_ffn_w, w_up, conv_ffn_w, conv_ffn_b, w_down, norm_final_w, loss_target, m_norm_mix_w, m_w_in, m_conv_a_w, m_conv_a_b, m_dt_bias, m_a_log, m_d_a, m_norm_a_w, m_w_proj_a, m_s5_lam_re, m_s5_lam_im, m_s5_log_dt, m_s5_b_re, m_s5_b_im, m_s5_c_re, m_s5_c_im, m_s5_d, m_w_s5_glu, m_w_out, m_norm_ffn_w, m_w_up, m_conv_ffn_w, m_conv_ffn_b, m_w_down, m_norm_final_w, v_norm_mix_w, v_w_in, v_conv_a_w, v_conv_a_b, v_dt_bias, v_a_log, v_d_a, v_norm_a_w, v_w_proj_a, v_s5_lam_re, v_s5_lam_im, v_s5_log_dt, v_s5_b_re, v_s5_b_im, v_s5_c_re, v_s5_c_im, v_s5_d, v_w_s5_glu, v_w_out, v_norm_ffn_w, v_w_up, v_conv_ffn_w, v_conv_ffn_b, v_w_down, v_norm_final_w):
    w = dict(norm_mix_w=norm_mix_w, w_in=w_in, conv_a_w=conv_a_w, conv_a_b=conv_a_b, dt_bias=dt_bias, a_log=a_log, d_a=d_a, norm_a_w=norm_a_w, w_proj_a=w_proj_a, s5_lam_re=s5_lam_re, s5_lam_im=s5_lam_im, s5_log_dt=s5_log_dt, s5_b_re=s5_b_re, s5_b_im=s5_b_im, s5_c_re=s5_c_re, s5_c_im=s5_c_im, s5_d=s5_d, w_s5_glu=w_s5_glu, w_out=w_out, norm_ffn_w=norm_ffn_w, w_up=w_up, conv_ffn_w=conv_ffn_w, conv_ffn_b=conv_ffn_b, w_down=w_down, norm_final_w=norm_final_w)
    m = dict(norm_mix_w=m_norm_mix_w, w_in=m_w_in, conv_a_w=m_conv_a_w, conv_a_b=m_conv_a_b, dt_bias=m_dt_bias, a_log=m_a_log, d_a=m_d_a, norm_a_w=m_norm_a_w, w_proj_a=m_w_proj_a, s5_lam_re=m_s5_lam_re, s5_lam_im=m_s5_lam_im, s5_log_dt=m_s5_log_dt, s5_b_re=m_s5_b_re, s5_b_im=m_s5_b_im, s5_c_re=m_s5_c_re, s5_c_im=m_s5_c_im, s5_d=m_s5_d, w_s5_glu=m_w_s5_glu, w_out=m_w_out, norm_ffn_w=m_norm_ffn_w, w_up=m_w_up, conv_ffn_w=m_conv_ffn_w, conv_ffn_b=m_conv_ffn_b, w_down=m_w_down, norm_final_w=m_norm_final_w)
    v = dict(norm_mix_w=v_norm_mix_w, w_in=v_w_in, conv_a_w=v_conv_a_w, conv_a_b=v_conv_a_b, dt_bias=v_dt_bias, a_log=v_a_log, d_a=v_d_a, norm_a_w=v_norm_a_w, w_proj_a=v_w_proj_a, s5_lam_re=v_s5_lam_re, s5_lam_im=v_s5_lam_im, s5_log_dt=v_s5_log_dt, s5_b_re=v_s5_b_re, s5_b_im=v_s5_b_im, s5_c_re=v_s5_c_re, s5_c_im=v_s5_c_im, s5_d=v_s5_d, w_s5_glu=v_w_s5_glu, w_out=v_w_out, norm_ffn_w=v_norm_ffn_w, w_up=v_w_up, conv_ffn_w=v_conv_ffn_w, conv_ffn_b=v_conv_ffn_b, w_down=v_w_down, norm_final_w=v_norm_final_w)
    xi, yi, ci = _place()
    chip = 2 * xi + yi

    cidx = jnp.reshape(ci, (1,)).astype(jnp.int32)
    sidx = jnp.reshape(chip, (1,)).astype(jnp.int32)

    tw = lambda a: jnp.transpose(a[0])[None]
    w["w_in"], m["w_in"], v["w_in"] = tw(w_in), tw(m_w_in), tw(v_w_in)
    shards = [w[k][0].astype(BF16) for k in BIG]
    shards[0] = jnp.pad(shards[0], ((0, WPAD - WSH), (0, 0)))

    late = {}

    def late_start(after):
        srcs, got = _ici_wait("gather", g_sems, g_srcs, g_lands, after, "gather_rest_wait")
        late["sems"], late["got"], late["srcs"], tok = _ici_start("pass", list(got), cidx, "pass_rest_start",
                                                                  lands=list(srcs))
        return tok

    def late_weights(after):
        full, _ = _ici_wait("pass", late["sems"], late["got"], late["srcs"], after, "pass_rest_wait")
        return {"w_proj_a": full[0].reshape(DI, D), "w_s5_glu": full[1], "w_out": full[2].reshape(D, D),
                "w_up": full[3], "w_down": full[4].reshape(DFF, D)}

    swaps, pending = {}, []

    def swap_start(names, g, tag):
        sems, parts, lands, tok = _ici_start("swap", [g[k] for k in names], cidx, "swap_start_" + tag)
        swaps[tag] = (names, sems, parts, lands)
        return tok

    def scatter_go(tag, after):
        names, sems, parts, lands = swaps[tag]
        parts, sib = _ici_wait("swap", sems, parts, lands, after, "swap_wait_" + tag)
        sums = [_chip_sum(parts[t], sib[t], cidx, "chip_sum_" + k) for t, k in enumerate(names)]
        sems, srcs, lands, tok = _ici_start("scatter", sums, cidx, "scatter_start_" + tag)
        pending.append((names, tag, sems, srcs, lands))
        return tok

    hooks = {"late_start": late_start, "late_weights": late_weights, "swap_start": swap_start,
             "scatter_go": scatter_go}
    conv_blocks = []
    for k, (taps, cols) in CONV_FULL.items():
        shard = jnp.where(ci == 0, w[k][0], 0.0)
        conv_blocks.append(lax.dynamic_update_slice_in_dim(jnp.zeros((taps, cols), F32), shard, chip * (cols // 4), 1))
    conv_full = _unpack(_allsum_small(_pack(conv_blocks), "sum_conv_w", cidx), [CONV_FULL[k] for k in CONV_FULL])

    half = D // 2
    sh_a, sh_b = shards[0][:, :half], shards[0][:, half:]
    a_sems, a_srcs, a_lands, a_tok = _ici_start("gather", [sh_a], conv_full[0], "gather_in_a_start")
    b_sems, b_srcs, b_lands, b_tok = _ici_start("gather", [sh_b], a_tok, "gather_in_b_start")
    hn1 = _rms_fwd(x[0], norm_mix_w, "rms_mix", after=b_tok)

    def w_in_part(sems, srcs, lands, after, tag):
        srcs, got = _ici_wait("gather", sems, srcs, lands, after, "gather_in_%s_wait" % tag)
        w_sh = _pass_halves(list(got), list(srcs), "pass_halves_in_" + tag)[0].reshape(4 * WPAD, half)
        w_dt = jnp.pad(w_sh[DT_SHARD_ROW:DT_SHARD_ROW + NG * HPG].reshape(NG, HPG, half),
                       ((0, 0), (0, 8 - HPG), (0, 0)))
        return lax.dynamic_update_slice(_move_rows(w_sh, RUNS_TO_MAIN, NFULL, MT, MT, "rows_to_main_" + tag),
                                        w_dt.reshape(8 * NG, half), (NMAIN, 0))

    w_a = w_in_part(a_sems, a_srcs, a_lands, hn1, "a")
    proj_a = _matmul(hn1[:, :half], w_a, "nt", "mm_in_a", tn=896)
    w_b = w_in_part(b_sems, b_srcs, b_lands, proj_a, "b")
    g_sems, g_srcs, g_lands, token = _ici_start("gather", shards[1:], w_b, "gather_rest_start")
    proj = _matmul(hn1[:, half:], w_b, "nt", "mm_in_b", tn=896, residual=proj_a, after=token)
    w_full = jnp.concatenate([w_a, w_b], axis=1)
    p = {
        "w_full": w_full,
        "conv_a_w": conv_full[0], "conv_ffn_w": conv_full[1],
        "conv_a_b": conv_a_b, "conv_ffn_b": conv_ffn_b,
        "norm_mix_w": norm_mix_w, "norm_a_w": norm_a_w, "norm_ffn_w": norm_ffn_w,
        "norm_final_w": norm_final_w.reshape(1, D),
        "dt_bias": dt_bias, "a_log": a_log, "d_a": d_a, "s5_d": s5_d,
        "s5_lam_re": s5_lam_re[0], "s5_lam_im": s5_lam_im[0], "s5_log_dt": s5_log_dt[0],
        "s5_b_re": s5_b_re[0], "s5_b_im": s5_b_im[0], "s5_c_re": s5_c_re[0], "s5_c_im": s5_c_im[0],
    }
    loss_blk, gx, g = _local_step(x[0], loss_target[0], hn1, proj, p, hooks)

    after, halves = gx, {}
    for names, tag, sems, srcs, lands in pending:
        srcs, got = _ici_wait("scatter", sems, srcs, lands, after, "scatter_wait_" + tag)
        for t, k in enumerate(names):
            halves[k] = _shard_sum(srcs[t], got[t], sidx, "shard_sum_" + k)
        after = halves[names[0]]
    w_sems, g_mine, w_lands, w_tok = _ici_start("whole", [halves[k] for k in BIG], cidx, "whole_start")

    small_shapes = [CONV_FULL.get(k, w[k].shape[1:] if k != "norm_final_w" else w[k].shape) for k in SMALL]
    small = _allsum_small(_pack([g[k] for k in SMALL] + [loss_blk[0:1, 0:1]]), "sum_small_grads", w_tok)
    small_grads = dict(zip(SMALL + ["loss"], _unpack(small, small_shapes + [(1,)])))
    for k, (taps, cols) in CONV_FULL.items():
        small_grads[k] = lax.dynamic_slice_in_dim(small_grads[k], chip * (cols // 4), cols // 4, axis=1)
    loss = small_grads.pop("loss").reshape(())

    grads, delta, new_m, new_v = {}, {}, {}, {}
    for k in SMALL:
        grads[k] = small_grads[k].reshape(w[k].shape)
    pk = lambda t: _pack([t[k] for k in SMALL])
    d_, m_, v_ = _adamw(pk(w), pk(grads), pk(m), pk(v), "adamw_small")
    shapes = [w[k].shape for k in SMALL]
    for k, dd, mm, vv in zip(SMALL, _unpack(d_, shapes), _unpack(m_, shapes), _unpack(v_, shapes)):
        delta[k], new_m[k], new_v[k] = dd, mm, vv
    g_mine, g_other = _ici_wait("whole", w_sems, g_mine, w_lands, d_, "whole_wait")
    for t, k in enumerate(BIG):
        outs = _adamw_halves(w[k], g_mine[t], g_other[t], m[k], v[k], cidx, "adamw_" + k)
        grads[k], delta[k], new_m[k], new_v[k] = [tw(o) for o in outs] if k == "w_in" else outs
    return (loss, gx[None], *[grads[k] for k in ORDER], *[delta[k] for k in ORDER],
            *[new_m[k] for k in ORDER], *[new_v[k] for k in ORDER])
```

```python
import functools
import math

import jax
import jax.numpy as jnp
from jax import lax
from jax.experimental import pallas as pl
from jax.experimental.pallas import tpu as pltpu

F32 = jnp.float32
BF16 = jnp.bfloat16
HI = lax.Precision.HIGHEST
MESH = pl.DeviceIdType.MESH
ANY = pl.BlockSpec(memory_space=pl.ANY)

D = 2048
DI = 3072
HD = 64
NG = 8
HPG = 6
GW = HPG * HD
NS = 128
KA = 4
Q = 256
CONVD = DI + 2 * NG * NS
DS5 = 1024
NCH = 4096
DFF = 5632
KF = 3
EPS = 1e-6
EIG_MAX = -1e-4
NMAIN = 13312
OFF_XBC, OFF_U, OFF_GA, OFF_GB = 3072, 8192, 9216, 11264
WSH = 3340
WPAD = 3360
IN_SPLIT = [DI, DI + CONVD, DI + CONVD + NG * HPG]
NFULL = NMAIN + 128
MT = 336


def _w_in_runs():
    runs = []
    for k in range(4):
        for o_lo, o_hi, m_lo in ((0, IN_SPLIT[1], 0), (IN_SPLIT[2], 4 * WSH, IN_SPLIT[1])):
            lo, hi = max(o_lo, WSH * k), min(o_hi, WSH * (k + 1))
            if lo < hi:
                runs.append((m_lo + lo - o_lo, m_lo + hi - o_lo, WPAD * k + lo - WSH * k))
    return runs


RUNS_TO_MAIN = _w_in_runs()
RUNS_TO_SHARDS = [(s_lo, s_lo + m_hi - m_lo, m_lo) for m_lo, m_hi, s_lo in RUNS_TO_MAIN]
DT_SHARD_ROW = WPAD * (IN_SPLIT[1] // WSH) + IN_SPLIT[1] % WSH
assert IN_SPLIT[1] // WSH == (IN_SPLIT[2] - 1) // WSH
VMEM_LIMIT = 56 * 1024 * 1024

LR, B1, B2, AEPS, WD, STEP = 0.001, 0.9, 0.999, 1e-08, 0.01, 10


def _cp(*sem):
    return pltpu.CompilerParams(dimension_semantics=sem, vmem_limit_bytes=VMEM_LIMIT)


def _sig(x):
    return jax.nn.sigmoid(x)


def _silu(x):
    return x * _sig(x)


def _dsilu(x):
    s = _sig(x)
    return s * (1.0 + x * (1.0 - s))


def _softplus(x):
    return jnp.maximum(x, 0.0) + jnp.log(1.0 + jnp.exp(-jnp.abs(x)))


_GC = math.sqrt(2.0 / math.pi)


def _gelu(x):
    return 0.5 * x * (1.0 + jnp.tanh(_GC * (x + 0.044715 * x * x * x)))


def _dgelu(x):
    t = jnp.tanh(_GC * (x + 0.044715 * x * x * x))
    return 0.5 * (1.0 + t) + 0.5 * x * (1.0 - t * t) * _GC * (1.0 + 3.0 * 0.044715 * x * x)


def _dot(a, b, dims=((1,), (0,)), prec=None):
    return lax.dot_general(a, b, (dims, ((), ())), precision=prec, preferred_element_type=F32)


NT = ((1,), (1,))
TN = ((0,), (0,))


def _pick(n, t):
    for unit in (128, 8):
        for cand in range(min(n, t) // unit * unit, 0, -unit):
            if n % cand == 0:
                return cand
    return n


def _matmul(a, b, mode, name, out_dtype=F32, tm=1024, tn=1024, tk=2048, residual=None, b_stacked=False,
            out_stacked=False, after=None):
    if b_stacked:
        _, brows, bn = b.shape
        bshape = (brows, 4 * bn)
    else:
        bshape = b.shape
    if mode == "nn":
        (m, k), (k2, n) = a.shape, bshape
    elif mode == "nt":
        (m, k), (n, k2) = a.shape, bshape
    else:
        (k, m), (k2, n) = a.shape, bshape
    assert k == k2
    tm = _pick(m, tm)
    tn = _pick(n // 4 if (out_stacked or (b_stacked and mode != "nt")) else n, tn)
    tk = _pick(k // 4 if (b_stacked and mode == "nt") else k, tk)
    nk = k // tk
    dims = {"nn": ((1,), (0,)), "nt": NT, "tn": TN}[mode]
    has_res = residual is not None
    n_in = 2 + has_res + (after is not None)

    def body(*refs):
        a_ref, b_ref = refs[0], refs[1]
        r_ref = refs[2] if has_res else None
        o_ref = refs[n_in]
        p = _dot(a_ref[...], b_ref[...], dims)

        def finish(r):
            if has_res:
                r = r + r_ref[...]
            o_ref[...] = r.astype(out_dtype)

        if nk == 1:
            finish(p)
        else:
            acc = refs[-1]
            kk = pl.program_id(2)

            @pl.when(kk == 0)
            def _():
                acc[...] = p

            @pl.when(kk > 0)
            def _():
                acc[...] += p

            @pl.when(kk == nk - 1)
            def _():
                finish(acc[...])

    if mode == "tn":
        a_spec = pl.BlockSpec((tk, tm), lambda i, j, kk: (kk, i))
    else:
        a_spec = pl.BlockSpec((tm, tk), lambda i, j, kk: (i, kk))
    if mode == "nt":
        if b_stacked:
            per = bn // tk
            b_spec = pl.BlockSpec((None, tn, tk), lambda i, j, kk: (kk // per, j, kk % per))
        else:
            b_spec = pl.BlockSpec((tn, tk), lambda i, j, kk: (j, kk))
    elif b_stacked:
        per = bn // tn
        b_spec = pl.BlockSpec((None, tk, tn), lambda i, j, kk: (j // per, kk, j % per))
    else:
        b_spec = pl.BlockSpec((tk, tn), lambda i, j, kk: (kk, j))
    o_spec = pl.BlockSpec((tm, tn), lambda i, j, kk: (i, j))
    if out_stacked:
        per_o = n // 4 // tn
        out_spec = pl.BlockSpec((None, tm, tn), lambda i, j, kk: (j // per_o, i, j % per_o))
        out_shape = jax.ShapeDtypeStruct((4, m, n // 4), out_dtype)
    else:
        out_spec, out_shape = o_spec, jax.ShapeDtypeStruct((m, n), out_dtype)
    in_specs, args = [a_spec, b_spec], [a, b]
    if has_res:
        in_specs.append(o_spec)
        args.append(residual)
    if after is not None:
        in_specs.append(ANY)
        args.append(after)
    return pl.pallas_call(
        body, name=name, grid=(m // tm, n // tn, nk),
        in_specs=in_specs, out_specs=out_spec, out_shape=out_shape,
        scratch_shapes=[pltpu.VMEM((tm, tn), F32)] if nk > 1 else [],
        compiler_params=_cp("parallel", "parallel", "arbitrary"),
    )(*args)


def _move_rows(src, runs, rows_out, t_out, t_in, name):
    rows_in, cols = src.shape
    nb_out, nb_in = rows_out // t_out, rows_in // t_in
    assert rows_out % t_out == 0 and rows_in % t_in == 0 and t_in >= t_out
    blk, off, lo, hi = ([[0] * nb_out for _ in range(2)] for _ in range(4))
    for i in range(nb_out):
        hits = [r for r in runs if r[0] < (i + 1) * t_out and r[1] > i * t_out]
        assert len(hits) <= 2
        for s, (o_lo, o_hi, s_lo) in enumerate(hits):
            lo[s][i] = max(o_lo, i * t_out) - i * t_out
            hi[s][i] = min(o_hi, (i + 1) * t_out) - i * t_out
            first = i * t_out + lo[s][i] - o_lo + s_lo
            blk[s][i] = min(first // t_in, nb_in - 1)
            off[s][i] = first - lo[s][i] - blk[s][i] * t_in
    table = jnp.asarray([blk[0], off[0], lo[0], hi[0], blk[1], off[1], lo[1], hi[1]], jnp.int32)

    def body(tab, a0, a1, b0, b1, o_ref):
        i = pl.program_id(0)
        o_ref[...] = jnp.zeros_like(o_ref)
        r = lax.broadcasted_iota(jnp.int32, (t_out, t_in), 0)
        k = lax.broadcasted_iota(jnp.int32, (t_out, t_in), 1)
        for s, (first, second) in enumerate(((a0, a1), (b0, b1))):
            off_s, lo_s, hi_s = tab[4 * s + 1, i], tab[4 * s + 2, i], tab[4 * s + 3, i]
            live = (r >= lo_s) & (r < hi_s)

            @pl.when(hi_s > lo_s)
            def _():
                sel = (live & (k == r + off_s)).astype(BF16)
                o_ref[...] += _dot(sel, first[...]).astype(o_ref.dtype)

            @pl.when((hi_s > lo_s) & (off_s + hi_s > t_in))
            def _():
                sel = (live & (k == r + off_s - t_in)).astype(BF16)
                o_ref[...] += _dot(sel, second[...]).astype(o_ref.dtype)

    def in_spec(s, nxt):
        return pl.BlockSpec((t_in, cols), lambda i, tab: (jnp.minimum(tab[4 * s, i] + nxt, nb_in - 1), 0))

    return pl.pallas_call(
        body, name=name,
        grid_spec=pltpu.PrefetchScalarGridSpec(
            num_scalar_prefetch=1, grid=(nb_out,),
            in_specs=[in_spec(0, 0), in_spec(0, 1), in_spec(1, 0), in_spec(1, 1)],
            out_specs=pl.BlockSpec((t_out, cols), lambda i, tab: (i, 0))),
        out_shape=jax.ShapeDtypeStruct((rows_out, cols), src.dtype), compiler_params=_cp("parallel"),
    )(table, src, src, src, src)


TL = 256


def _rms_fwd(x, w, name, after=None):
    n, d = x.shape

    def body(x_ref, w_ref, *rest):
        xv = x_ref[...]
        r = lax.rsqrt(jnp.mean(xv * xv, axis=-1, keepdims=True) + EPS)
        rest[-1][...] = (xv * r * w_ref[...]).astype(BF16)

    extra = [] if after is None else [after]
    return pl.pallas_call(
        body, name=name, grid=(n // TL,),
        in_specs=[pl.BlockSpec((TL, d), lambda i: (i, 0)), pl.BlockSpec((1, d), lambda i: (0, 0))] + [ANY] * len(extra),
        out_specs=pl.BlockSpec((TL, d), lambda i: (i, 0)),
        out_shape=jax.ShapeDtypeStruct((n, d), BF16), compiler_params=_cp("parallel"),
    )(x, w, *extra)


def _rms_bwd(dhn, x, w, dres, name, after=None):
    n, d = x.shape

    def body(g_ref, x_ref, w_ref, r_ref, *rest):
        dx_ref, dxb_ref, gw_ref = rest[-3:]
        xv = x_ref[...]
        r = lax.rsqrt(jnp.mean(xv * xv, axis=-1, keepdims=True) + EPS)
        xh = xv * r
        gv = g_ref[...]
        g = gv * w_ref[...]
        dx = r_ref[...] + r * (g - xh * jnp.mean(g * xh, axis=-1, keepdims=True))
        dx_ref[...] = dx
        dxb_ref[...] = dx.astype(BF16)

        @pl.when(pl.program_id(0) == 0)
        def _():
            gw_ref[...] = jnp.zeros_like(gw_ref)

        gw_ref[...] += jnp.sum(gv * xh, axis=0, keepdims=True)

    extra = [] if after is None else [after]
    row = pl.BlockSpec((TL, d), lambda i: (i, 0))
    vec = pl.BlockSpec((1, d), lambda i: (0, 0))
    return pl.pallas_call(
        body, name=name, grid=(n // TL,),
        in_specs=[row, row, vec, row] + [ANY] * len(extra), out_specs=[row, row, vec],
        out_shape=[jax.ShapeDtypeStruct((n, d), F32), jax.ShapeDtypeStruct((n, d), BF16),
                   jax.ShapeDtypeStruct((1, d), F32)],
        compiler_params=_cp("arbitrary"),
    )(dhn, x, w, dres, *extra)


def _final(h2, w, target):
    n, d = h2.shape

    def body(x_ref, w_ref, t_ref, dx_ref, dxb_ref, gw_ref, loss_ref):
        xv = x_ref[...]
        r = lax.rsqrt(jnp.mean(xv * xv, axis=-1, keepdims=True) + EPS)
        xh = xv * r
        diff = xh * w_ref[...] - t_ref[...]
        gv = diff * (1.0 / d)
        g = gv * w_ref[...]
        dx = r * (g - xh * jnp.mean(g * xh, axis=-1, keepdims=True))
        dx_ref[...] = dx
        dxb_ref[...] = dx.astype(BF16)

        @pl.when(pl.program_id(0) == 0)
        def _():
            gw_ref[...] = jnp.zeros_like(gw_ref)
            loss_ref[...] = jnp.zeros_like(loss_ref)

        gw_ref[...] += jnp.sum(gv * xh, axis=0, keepdims=True)
        part = 0.5 * jnp.sum(jnp.mean(diff * diff, axis=-1, keepdims=True), axis=0, keepdims=True)
        loss_ref[...] += jnp.broadcast_to(part, loss_ref.shape)

    row = pl.BlockSpec((TL, d), lambda i: (i, 0))
    vec = pl.BlockSpec((1, d), lambda i: (0, 0))
    return pl.pallas_call(
        body, name="final_loss", grid=(n // TL,),
        in_specs=[row, vec, row], out_specs=[row, row, vec, pl.BlockSpec((8, 128), lambda i: (0, 0))],
        out_shape=[jax.ShapeDtypeStruct((n, d), F32), jax.ShapeDtypeStruct((n, d), BF16),
                   jax.ShapeDtypeStruct((1, d), F32), jax.ShapeDtypeStruct((8, 128), F32)],
        compiler_params=_cp("arbitrary"),
    )(h2, w, target)


CT = 512
CL = 512


def _lagged(xf, taps, rows):
    return [xf[8:8 + rows]] + [pltpu.roll(xf, s, 0)[8:8 + rows] for s in range(1, taps)]


def _shift_up(x, u, n):
    if u == 0:
        return x[0:n]
    return pltpu.roll(x, x.shape[0] - u, 0)[0:n]


def _conv_pre(lagged, w_ref, b_ref, taps):
    pre = b_ref[...]
    for k in range(taps):
        pre = pre + w_ref[k:k + 1, :] * lagged[taps - 1 - k]
    return pre


def _conv_back(e, w_ref, taps):
    dx = w_ref[taps - 1:taps, :] * e[0:CL]
    for k in range(taps - 1):
        dx = dx + w_ref[k:k + 1, :] * _shift_up(e, taps - 1 - k, CL)
    return dx


def _halo_specs(n, col_of):
    per = CL // 8
    cur = pl.BlockSpec((CL, CT), lambda j, i, *_: (i, col_of(j)))
    prev = pl.BlockSpec((8, CT), lambda j, i, *_: (jnp.maximum(i * per - 1, 0), col_of(j)))
    nxt = pl.BlockSpec((8, CT), lambda j, i, *_: (jnp.minimum((i + 1) * per, n // 8 - 1), col_of(j)))
    return prev, cur, nxt


def _conv_a_fwd(proj, w, b):
    n = proj.shape[0]
    off = OFF_XBC // CT

    def body(p_ref, x_ref, w_ref, b_ref, o_ref):
        p8 = jnp.where(pl.program_id(1) > 0, p_ref[...], 0.0)
        xf = jnp.concatenate([p8, x_ref[...]], axis=0)
        o_ref[...] = _silu(_conv_pre(_lagged(xf, KA, CL), w_ref, b_ref, KA))

    prev, cur, _ = _halo_specs(n, lambda j: j + off)
    return pl.pallas_call(
        body, name="conv_a_fwd", grid=(CONVD // CT, n // CL),
        in_specs=[prev, cur, pl.BlockSpec((KA, CT), lambda j, i: (0, j)), pl.BlockSpec((1, CT), lambda j, i: (0, j))],
        out_specs=pl.BlockSpec((CL, CT), lambda j, i: (i, j)),
        out_shape=jax.ShapeDtypeStruct((n, CONVD), F32), compiler_params=_cp("parallel", "parallel"),
    )(proj, proj, w, b)


def _conv_a_bwd(proj, dout, w, b, col0, name):
    n, width = dout.shape
    off = (OFF_XBC + col0) // CT
    woff = col0 // CT
    nl = n // CL

    def body(p_ref, x_ref, n_ref, d_ref, dn_ref, w_ref, b_ref, dx_ref, dw_ref, db_ref):
        i = pl.program_id(1)
        xf = jnp.concatenate([jnp.where(i > 0, p_ref[...], 0.0), x_ref[...], n_ref[...]], axis=0)
        lag = _lagged(xf, KA, CL + 8)
        de = jnp.concatenate([d_ref[...], jnp.where(i < nl - 1, dn_ref[...], 0.0)], axis=0)
        se = de * _dsilu(_conv_pre(lag, w_ref, b_ref, KA))
        dx_ref[...] = _conv_back(se, w_ref, KA).astype(BF16)

        @pl.when(i == 0)
        def _():
            dw_ref[...] = jnp.zeros_like(dw_ref)
            db_ref[...] = jnp.zeros_like(db_ref)

        sc = se[0:CL]
        for k in range(KA):
            dw_ref[k:k + 1, :] += jnp.sum(sc * lag[KA - 1 - k][0:CL], axis=0, keepdims=True)
        db_ref[...] += jnp.sum(sc, axis=0, keepdims=True)

    prev, cur, nxt = _halo_specs(n, lambda j: j + off)
    _, dcur, dnxt = _halo_specs(n, lambda j: j)
    wspec = pl.BlockSpec((KA, CT), lambda j, i: (0, j + woff))
    bspec = pl.BlockSpec((1, CT), lambda j, i: (0, j + woff))
    return pl.pallas_call(
        body, name=name, grid=(width // CT, nl),
        in_specs=[prev, cur, nxt, dcur, dnxt, wspec, bspec],
        out_specs=[pl.BlockSpec((CL, CT), lambda j, i: (i, j)), pl.BlockSpec((KA, CT), lambda j, i: (0, j)),
                   pl.BlockSpec((1, CT), lambda j, i: (0, j))],
        out_shape=[jax.ShapeDtypeStruct((n, width), BF16), jax.ShapeDtypeStruct((KA, width), F32),
                   jax.ShapeDtypeStruct((1, width), F32)],
        compiler_params=_cp("parallel", "arbitrary"),
    )(proj, proj, proj, dout, dout, w, b)


def _conv_ffn_fwd(up, w, b):
    n = up.shape[0]
    nb = DFF // CT

    def body(pg_ref, g_ref, pv_ref, v_ref, wg_ref, bg_ref, wv_ref, bv_ref, o_ref):
        inner = pl.program_id(1) > 0
        gf = jnp.concatenate([jnp.where(inner, pg_ref[...], 0.0), g_ref[...]], axis=0)
        vf = jnp.concatenate([jnp.where(inner, pv_ref[...], 0.0), v_ref[...]], axis=0)
        gc = _conv_pre(_lagged(gf, KF, CL), wg_ref, bg_ref, KF)
        vc = _conv_pre(_lagged(vf, KF, CL), wv_ref, bv_ref, KF)
        o_ref[...] = (_silu(gc) * vc).astype(BF16)

    gp, gcur, _ = _halo_specs(n, lambda j: j)
    vp, vcur, _ = _halo_specs(n, lambda j: j + nb)
    return pl.pallas_call(
        body, name="conv_ffn_fwd", grid=(nb, n // CL),
        in_specs=[gp, gcur, vp, vcur,
                  pl.BlockSpec((KF, CT), lambda j, i: (0, j)), pl.BlockSpec((1, CT), lambda j, i: (0, j)),
                  pl.BlockSpec((KF, CT), lambda j, i: (0, j + nb)), pl.BlockSpec((1, CT), lambda j, i: (0, j + nb))],
        out_specs=pl.BlockSpec((CL, CT), lambda j, i: (i, j)),
        out_shape=jax.ShapeDtypeStruct((n, DFF), BF16), compiler_params=_cp("parallel", "parallel"),
    )(up, up, up, up, w, b, w, b)


def _conv_ffn_bwd(up, dact, w, b):
    n = up.shape[0]
    nb = DFF // CT
    nl = n // CL

    def body(pg_ref, g_ref, ng_ref, pv_ref, v_ref, nv_ref, d_ref, dn_ref, wg_ref, bg_ref, wv_ref, bv_ref,
             dxg_ref, dxv_ref, dwg_ref, dwv_ref, dbg_ref, dbv_ref):
        i = pl.program_id(1)
        gf = jnp.concatenate([jnp.where(i > 0, pg_ref[...], 0.0), g_ref[...], ng_ref[...]], axis=0)
        vf = jnp.concatenate([jnp.where(i > 0, pv_ref[...], 0.0), v_ref[...], nv_ref[...]], axis=0)
        glag, vlag = _lagged(gf, KF, CL + 8), _lagged(vf, KF, CL + 8)
        de = jnp.concatenate([d_ref[...], jnp.where(i < nl - 1, dn_ref[...], 0.0)], axis=0).astype(F32)
        gc = _conv_pre(glag, wg_ref, bg_ref, KF)
        vc = _conv_pre(vlag, wv_ref, bv_ref, KF)
        sg = _sig(gc)
        dgc = de * vc * (sg * (1.0 + gc * (1.0 - sg)))
        dvc = de * (gc * sg)
        dxg_ref[...] = _conv_back(dgc, wg_ref, KF).astype(BF16)
        dxv_ref[...] = _conv_back(dvc, wv_ref, KF).astype(BF16)

        @pl.when(i == 0)
        def _():
            for r in (dwg_ref, dwv_ref, dbg_ref, dbv_ref):
                r[...] = jnp.zeros_like(r)

        for e, lag, dw_ref, db_ref in ((dgc, glag, dwg_ref, dbg_ref), (dvc, vlag, dwv_ref, dbv_ref)):
            ec = e[0:CL]
            for k in range(KF):
                dw_ref[k:k + 1, :] += jnp.sum(ec * lag[KF - 1 - k][0:CL], axis=0, keepdims=True)
            db_ref[...] += jnp.sum(ec, axis=0, keepdims=True)

    gp, gcur, gnx = _halo_specs(n, lambda j: j)
    vp, vcur, vnx = _halo_specs(n, lambda j: j + nb)
    wcol = lambda o: (pl.BlockSpec((KF, CT), lambda j, i: (0, j + o)), pl.BlockSpec((1, CT), lambda j, i: (0, j + o)))
    wg, bg = wcol(0)
    wv, bv = wcol(nb)
    dxs = pl.BlockSpec((CL, CT), lambda j, i: (i, j))
    outs = pl.pallas_call(
        body, name="conv_ffn_bwd", grid=(nb, nl),
        in_specs=[gp, gcur, gnx, vp, vcur, vnx, gcur, gnx, wg, bg, wv, bv],
        out_specs=[dxs, dxs, wg, wg, bg, bg],
        out_shape=[jax.ShapeDtypeStruct((n, DFF), BF16)] * 2 + [jax.ShapeDtypeStruct((KF, DFF), F32)] * 2
        + [jax.ShapeDtypeStruct((1, DFF), F32)] * 2,
        compiler_params=_cp("parallel", "arbitrary"),
    )(up, up, up, up, up, up, dact, dact, w, b, w, b)
    return [jnp.concatenate(outs[k:k + 2], axis=1) for k in (0, 2, 4)]


HL = 128


def _split3(x):
    hi = x.astype(BF16)
    r1 = x - hi.astype(F32)
    mid = r1.astype(BF16)
    return hi, mid, (r1 - mid.astype(F32)).astype(BF16)


def _dot3(x, m):
    hi, mid, lo = _split3(x)
    return _dot(hi, m) + _dot(mid, m) + _dot(lo, m)


def _tri():
    row = lax.broadcasted_iota(jnp.int32, (Q, Q), 0)
    col = lax.broadcasted_iota(jnp.int32, (Q, Q), 1)
    return row >= col, row <= col


def _ssd_prep(dtraw, hp, emat):
    n = dtraw.shape[0]

    def body(d_ref, hp_ref, e_ref, dt_ref, s_ref, st_ref, dte_ref, se_ref):
        lower, upper = _tri()
        dt = _softplus(d_ref[...] + hp_ref[0:1, :])
        da = dt * (-jnp.exp(hp_ref[1:2, :]))
        s = _dot(lower.astype(F32), da, prec=HI)
        dt_ref[...] = dt
        s_ref[...] = s
        st_ref[...] = _dot(da, upper.astype(F32), TN, prec=HI)
        e = e_ref[...]
        dte_ref[...] = _dot3(dt, e)
        se_ref[...] = _dot3(s, e)

    row = pl.BlockSpec((Q, HL), lambda c: (c, 0))
    wide = pl.BlockSpec((Q, DI), lambda c: (c, 0))
    return pl.pallas_call(
        body, name="ssd_prep", grid=(n // Q,),
        in_specs=[pl.BlockSpec((Q, HL), lambda c: (c, NMAIN // HL)), pl.BlockSpec((8, HL), lambda c: (0, 0)),
                  pl.BlockSpec((HL, DI), lambda c: (0, 0))],
        out_specs=[row, row, pl.BlockSpec((HL, Q), lambda c: (0, c)), wide, wide],
        out_shape=[jax.ShapeDtypeStruct((n, HL), F32)] * 2 + [jax.ShapeDtypeStruct((HL, n), F32)]
        + [jax.ShapeDtypeStruct((n, DI), F32)] * 2,
        compiler_params=_cp("parallel"),
    )(dtraw, hp, emat)


def _ssd_post(ds_e, ddt_e, tsum, dsh, dtraw, dt, hp, emat_t):
    n = dtraw.shape[0]

    def body(dse_ref, dde_ref, ts_ref, dsh_ref, d_ref, dt_ref, hp_ref, et_ref, draw_ref, ps_ref):
        _, upper = _tri()
        et = et_ref[...]
        a = -jnp.exp(hp_ref[1:2, :])
        rows = lax.broadcasted_iota(jnp.int32, (Q, HL), 0)
        ds_t = _dot3(jnp.broadcast_to(ts_ref[...], (8, DI)), et)[0:1, :]
        ds = _dot3(dse_ref[...], et) + dsh_ref[...] + jnp.where(rows == Q - 1, ds_t, 0.0)
        d_a = _dot(upper.astype(F32), ds, prec=HI)
        draw = (_dot3(dde_ref[...], et) + d_a * a) * _sig(d_ref[...] + hp_ref[0:1, :])
        draw_ref[...] = draw

        @pl.when(pl.program_id(0) == 0)
        def _():
            ps_ref[...] = jnp.zeros_like(ps_ref)

        ps_ref[0:1, :] += jnp.sum(draw, axis=0, keepdims=True)
        ps_ref[1:2, :] += jnp.sum(d_a * dt_ref[...], axis=0, keepdims=True) * a

    row = pl.BlockSpec((Q, HL), lambda c: (c, 0))
    wide = pl.BlockSpec((Q, DI), lambda c: (c, 0))
    small = pl.BlockSpec((8, HL), lambda c: (0, 0))
    return pl.pallas_call(
        body, name="ssd_post", grid=(n // Q,),
        in_specs=[wide, wide, pl.BlockSpec((None, 1, DI), lambda c: (c, 0, 0)), row,
                  pl.BlockSpec((Q, HL), lambda c: (c, NMAIN // HL)), row, small,
                  pl.BlockSpec((DI, HL), lambda c: (0, 0))],
        out_specs=[row, small],
        out_shape=[jax.ShapeDtypeStruct((n, HL), F32), jax.ShapeDtypeStruct((8, HL), F32)],
        compiler_params=_cp("arbitrary"),
    )(ds_e, ddt_e, tsum, dsh, dtraw, dt, hp, emat_t)


def _ssd_specs(nc, rev):
    cc = (lambda c: nc - 1 - c) if rev else (lambda c: c)
    return [
        pl.BlockSpec((Q, GW), lambda g, c: (cc(c), g)),
        pl.BlockSpec((Q, NS), lambda g, c: (cc(c), DI // NS + g)),
        pl.BlockSpec((Q, NS), lambda g, c: (cc(c), (DI + NG * NS) // NS + g)),
        pl.BlockSpec((None, Q, 8), lambda g, c: (g, cc(c), 0)),
        pl.BlockSpec((8, Q), lambda g, c: (g, cc(c))),
        pl.BlockSpec((Q, GW), lambda g, c: (cc(c), g)),
        pl.BlockSpec((Q, GW), lambda g, c: (cc(c), g)),
        pl.BlockSpec((1, GW), lambda g, c: (0, g)),
    ]


def _ssd_fwd(xbc, s8, s_t, dt_e, s_e, dexp):
    n = xbc.shape[0]
    nc = n // Q

    def body(xs_ref, b_ref, c_ref, sc_ref, sr_ref, dte_ref, se_ref, dexp_ref, y_ref, sp_ref, st):
        @pl.when(pl.program_id(1) == 0)
        def _():
            st[...] = jnp.zeros_like(st)

        lower, _ = _tri()
        s_c, s_r, dt_e, s_e = sc_ref[...], sr_ref[...], dte_ref[...], se_ref[...]
        xs = xs_ref[...]
        x = xs * dt_e
        xb = x.astype(BF16)
        bb, cb = b_ref[...].astype(BF16), c_ref[...].astype(BF16)
        cbm = _dot(cb, bb, NT)
        st_e = s_e[Q - 1:Q, :]
        sprev = st[...]
        sp_ref[...] = sprev
        yoff = _dot(cb, sprev.astype(BF16)) * jnp.exp(s_e) + dexp_ref[...] * xs
        for h in range(HPG):
            sl = slice(h * HD, (h + 1) * HD)
            lm = jnp.where(lower, jnp.exp(jnp.minimum(s_c[:, h:h + 1] - s_r[h:h + 1, :], 0.0)), 0.0)
            y_ref[:, sl] = _dot((cbm * lm).astype(BF16), xb[:, sl]) + yoff[:, sl]
        w = (x * jnp.exp(st_e - s_e)).astype(BF16)
        st[...] = jnp.exp(st_e) * sprev + _dot(bb, w, TN)

    return pl.pallas_call(
        body, name="ssd_fwd", grid=(NG, nc), in_specs=_ssd_specs(nc, False),
        out_specs=[pl.BlockSpec((Q, GW), lambda g, c: (c, g)),
                   pl.BlockSpec((None, None, NS, GW), lambda g, c: (c, g, 0, 0))],
        out_shape=[jax.ShapeDtypeStruct((n, DI), F32), jax.ShapeDtypeStruct((nc, NG, NS, GW), F32)],
        scratch_shapes=[pltpu.VMEM((NS, GW), F32)],
        compiler_params=_cp("parallel", "arbitrary"),
    )(xbc, xbc, xbc, s8, s_t, dt_e, s_e, dexp)


def _ssd_bwd(xbc, s8, s_t, dt_e, s_e, dexp, sprev_all, dy):
    n = xbc.shape[0]
    nc = n // Q
    rc = lambda c: nc - 1 - c

    def body(xs_ref, b_ref, c_ref, sc_ref, sr_ref, dte_ref, se_ref, dexp_ref, sp_ref, dy_ref,
             dxs_ref, db_ref, dc_ref, dse_ref, dde_ref, ts_ref, dsh_ref, pd_ref, dst, dxbuf):
        @pl.when(pl.program_id(1) == 0)
        def _():
            dst[...] = jnp.zeros_like(dst)
            pd_ref[...] = jnp.zeros_like(pd_ref)

        lower, upper = _tri()
        s_c, s_r, dt_e, s_e = sc_ref[...], sr_ref[...], dte_ref[...], se_ref[...]
        xs = xs_ref[...]
        x = xs * dt_e
        xb = x.astype(BF16)
        bb, cb = b_ref[...].astype(BF16), c_ref[...].astype(BF16)
        cbm = _dot(cb, bb, NT)
        cbt = _dot(bb, cb, NT)
        st_e = s_e[Q - 1:Q, :]
        dec_out, dec_st, e_t = jnp.exp(s_e), jnp.exp(st_e - s_e), jnp.exp(st_e)
        dyv = dy_ref[...]
        dyb = dyv.astype(BF16)
        sprev = sp_ref[...]
        sb = sprev.astype(BF16)
        ds_in = dst[...]
        dsb = ds_in.astype(BF16)

        cs = _dot(cb, sb)
        dcs = (dyv * dec_out).astype(BF16)
        d_c = _dot(dcs, sb, NT)
        wf = x * dec_st
        d_w = _dot(bb, dsb)
        d_b = _dot(wf.astype(BF16), dsb, NT)
        tw = d_w * wf
        dse_ref[...] = dyv * cs * dec_out - tw
        ds_c = jnp.zeros((Q, 8), F32)
        dcb = jnp.zeros((Q, Q), F32)
        dcbt = jnp.zeros((Q, Q), F32)
        lane8 = lax.broadcasted_iota(jnp.int32, (1, 8), 1)
        for h in range(HPG):
            sl = slice(h * HD, (h + 1) * HD)
            sc_h, sr_h = s_c[:, h:h + 1], s_r[h:h + 1, :]
            lm = jnp.where(lower, jnp.exp(jnp.minimum(sc_h - sr_h, 0.0)), 0.0)
            lmt = jnp.where(upper, jnp.exp(jnp.minimum(sr_h - sc_h, 0.0)), 0.0)
            mt = cbt * lmt
            dm = _dot(dyb[:, sl], xb[:, sl], NT)
            dmt = _dot(xb[:, sl], dyb[:, sl], NT)
            dxbuf[:, sl] = _dot(mt.astype(BF16), dyb[:, sl])
            dml = dm * lm
            dmlt = dmt * lmt
            dcb = dcb + dml
            dcbt = dcbt + dmlt
            dsh = jnp.sum(dml * cbm, axis=1, keepdims=True) - jnp.sum(dmlt * cbt, axis=1, keepdims=True)
            ds_c = ds_c + dsh * (lane8 == h).astype(F32)
        d_c = d_c + _dot(dcb.astype(BF16), bb)
        d_b = d_b + _dot(dcbt.astype(BF16), cb)
        dx = d_w * dec_st + dxbuf[...]
        ts_ref[...] = jnp.sum(tw, axis=0, keepdims=True) + jnp.sum(ds_in * sprev, axis=0, keepdims=True) * e_t
        dsh_ref[...] = ds_c
        dde_ref[...] = dx * xs
        pd_ref[...] += jnp.sum(dyv * xs, axis=0, keepdims=True)
        dxs_ref[...] = dx * dt_e + dyv * dexp_ref[...]
        db_ref[...] = d_b
        dc_ref[...] = d_c
        dst[...] = e_t * ds_in + _dot(cb, dcs, TN)

    wide = pl.BlockSpec((Q, GW), lambda g, c: (rc(c), g))
    state = pl.BlockSpec((Q, NS), lambda g, c: (rc(c), g))
    in_specs = _ssd_specs(nc, True) + [pl.BlockSpec((None, None, NS, GW), lambda g, c: (rc(c), g, 0, 0)), wide]
    return pl.pallas_call(
        body, name="ssd_bwd", grid=(NG, nc), in_specs=in_specs,
        out_specs=[wide, state, state, wide, wide,
                   pl.BlockSpec((None, 1, GW), lambda g, c: (rc(c), 0, g)),
                   pl.BlockSpec((None, Q, 8), lambda g, c: (g, rc(c), 0)),
                   pl.BlockSpec((None, 1, GW), lambda g, c: (g, 0, 0))],
        out_shape=[jax.ShapeDtypeStruct((n, DI), F32), jax.ShapeDtypeStruct((n, NG * NS), F32),
                   jax.ShapeDtypeStruct((n, NG * NS), F32), jax.ShapeDtypeStruct((n, DI), F32),
                   jax.ShapeDtypeStruct((n, DI), F32), jax.ShapeDtypeStruct((nc, 1, DI), F32),
                   jax.ShapeDtypeStruct((NG, n, 8), F32), jax.ShapeDtypeStruct((NG, 1, GW), F32)],
        scratch_shapes=[pltpu.VMEM((NS, GW), F32), pltpu.VMEM((Q, GW), F32)],
        compiler_params=_cp("parallel", "arbitrary"),
    )(xbc, xbc, xbc, s8, s_t, dt_e, s_e, dexp, sprev_all, dy)


GL = 128


def _gnorm_fwd(y, proj, w):
    n = y.shape[0]

    def body(y_ref, z_ref, w_ref, o_ref):
        for g in range(NG):
            sl = slice(g * GW, (g + 1) * GW)
            yz = y_ref[:, sl] * _silu(z_ref[:, sl])
            r = lax.rsqrt(jnp.mean(yz * yz, axis=-1, keepdims=True) + EPS)
            o_ref[:, sl] = (yz * r * w_ref[:, sl]).astype(BF16)

    row = pl.BlockSpec((GL, DI), lambda i: (i, 0))
    return pl.pallas_call(
        body, name="gnorm_fwd", grid=(n // GL,),
        in_specs=[row, row, pl.BlockSpec((1, DI), lambda i: (0, 0))], out_specs=row,
        out_shape=jax.ShapeDtypeStruct((n, DI), BF16), compiler_params=_cp("parallel"),
    )(y, proj, w)


def _gnorm_bwd(dyn, y, proj, w, after):
    n = y.shape[0]

    def body(d_ref, y_ref, z_ref, w_ref, _, dy_ref, dz_ref, gw_ref):
        @pl.when(pl.program_id(0) == 0)
        def _():
            gw_ref[...] = jnp.zeros_like(gw_ref)

        for g in range(NG):
            sl = slice(g * GW, (g + 1) * GW)
            yv, zv, dv = y_ref[:, sl], z_ref[:, sl], d_ref[:, sl]
            sz = _silu(zv)
            yz = yv * sz
            r = lax.rsqrt(jnp.mean(yz * yz, axis=-1, keepdims=True) + EPS)
            yh = yz * r
            gg = dv * w_ref[:, sl]
            dyz = r * (gg - yh * jnp.mean(gg * yh, axis=-1, keepdims=True))
            gw_ref[:, sl] += jnp.sum(dv * yh, axis=0, keepdims=True)
            dy_ref[:, sl] = dyz * sz
            dz_ref[:, sl] = (dyz * yv * _dsilu(zv)).astype(BF16)

    row = pl.BlockSpec((GL, DI), lambda i: (i, 0))
    vec = pl.BlockSpec((1, DI), lambda i: (0, 0))
    return pl.pallas_call(
        body, name="gnorm_bwd", grid=(n // GL,),
        in_specs=[row, row, row, vec, ANY], out_specs=[row, row, vec],
        out_shape=[jax.ShapeDtypeStruct((n, DI), F32), jax.ShapeDtypeStruct((n, DI), BF16),
                   jax.ShapeDtypeStruct((1, DI), F32)],
        compiler_params=_cp("arbitrary"),
    )(dyn, y, proj, w, after)


SL = 512
SB = 8
SCB = NCH // SB


def _s5_in(proj, bre, bim, after=None):
    n = proj.shape[0]
    uoff = OFF_U // 128

    def body(u_ref, br_ref, bi_ref, *rest):
        or_ref, oi_ref = rest[-2:]
        u = u_ref[...].astype(BF16)
        or_ref[...] = _dot(u, br_ref[...])
        oi_ref[...] = _dot(u, bi_ref[...])

    extra = [] if after is None else [after]
    blk = pl.BlockSpec((None, 128, SCB), lambda i, j: (j, 0, 0))
    out = pl.BlockSpec((SL, SCB), lambda i, j: (i, j))
    return pl.pallas_call(
        body, name="s5_in", grid=(n // SL, SB),
        in_specs=[pl.BlockSpec((SL, 128), lambda i, j: (i, uoff + j)), blk, blk] + [ANY] * len(extra),
        out_specs=[out, out],
        out_shape=[jax.ShapeDtypeStruct((n, NCH), F32)] * 2, compiler_params=_cp("parallel", "parallel"),
    )(proj, bre, bim, *extra)


SC = 1024


def _s5_scan(vre, vim, tab, reverse, name):
    n = vre.shape[0]
    nl = n // SL
    ng = SL // 8
    ti = (lambda i: nl - 1 - i) if reverse else (lambda i: i)

    def body(re_ref, im_ref, tab_ref, ore_ref, oim_ref, cre, cim):
        @pl.when(pl.program_id(1) == 0)
        def _():
            cre[...] = jnp.zeros_like(cre)
            cim[...] = jnp.zeros_like(cim)

        def step(j, carry):
            cr, ci = carry
            jj = (ng - 1 - j) if reverse else j
            rows = pl.ds(pl.multiple_of(jj * 8, 8), 8)
            vr, vi = re_ref[rows, :], im_ref[rows, :]
            for t, k in enumerate((1, 2, 4)):
                sh = (8 - k) if reverse else k
                rr, ri = pltpu.roll(vr, sh, 0), pltpu.roll(vi, sh, 0)
                pr, pi = tab_ref[2 * t], tab_ref[2 * t + 1]
                vr, vi = vr + pr * rr - pi * ri, vi + pr * ri + pi * rr
            lr, li = tab_ref[6], tab_ref[7]
            vr, vi = vr + lr * cr - li * ci, vi + lr * ci + li * cr
            ore_ref[rows, :] = vr
            oim_ref[rows, :] = vi
            e = 0 if reverse else 7
            return (jnp.broadcast_to(vr[e:e + 1, :], (8, SC)), jnp.broadcast_to(vi[e:e + 1, :], (8, SC)))

        cr, ci = lax.fori_loop(0, ng, step, (cre[...], cim[...]))
        cre[...] = cr
        cim[...] = ci

    blk = pl.BlockSpec((SL, SC), lambda j, i: (ti(i), j))
    return pl.pallas_call(
        body, name=name, grid=(NCH // SC, nl),
        in_specs=[blk, blk, pl.BlockSpec((8, 8, SC), lambda j, i: (0, 0, j))], out_specs=[blk, blk],
        out_shape=[jax.ShapeDtypeStruct((n, NCH), F32)] * 2,
        scratch_shapes=[pltpu.VMEM((8, SC), F32), pltpu.VMEM((8, SC), F32)],
        compiler_params=_cp("parallel", "arbitrary"),
    )(vre, vim, tab)


def _s5_out(xre, xim, cre, cimn, proj, dvec):
    n = xre.shape[0]
    uoff = OFF_U // 128

    def body(xr_ref, xi_ref, cr_ref, ci_ref, u_ref, d_ref, y_ref, g_ref):
        y = (_dot(xr_ref[...].astype(BF16), cr_ref[...]) + _dot(xi_ref[...].astype(BF16), ci_ref[...])
             + d_ref[...] * u_ref[...])
        y_ref[...] = y
        g_ref[...] = _gelu(y).astype(BF16)

    xs = pl.BlockSpec((SL, SCB), lambda i, j: (i, j))
    blk = pl.BlockSpec((None, SCB, 128), lambda i, j: (j, 0, 0))
    out = pl.BlockSpec((SL, 128), lambda i, j: (i, j))
    return pl.pallas_call(
        body, name="s5_out", grid=(n // SL, SB),
        in_specs=[xs, xs, blk, blk, pl.BlockSpec((SL, 128), lambda i, j: (i, uoff + j)),
                  pl.BlockSpec((1, 128), lambda i, j: (0, j))],
        out_specs=[out, out],
        out_shape=[jax.ShapeDtypeStruct((n, DS5), F32), jax.ShapeDtypeStruct((n, DS5), BF16)],
        compiler_params=_cp("parallel", "parallel"),
    )(xre, xim, cre, cimn, proj, dvec)


def _s5_out_bwd(dg, ypre, crt, cimnt, proj, dvec, xre, xim):
    n = dg.shape[0]
    uoff = OFF_U // 128
    nl = n // SL

    def body(dg_ref, y_ref, cr_ref, ci_ref, u_ref, d_ref, xr_ref, xi_ref,
             gr_ref, gi_ref, dus_ref, gcr_ref, gci_ref, gd_ref):
        dy = dg_ref[...] * _dgelu(y_ref[...])
        dyb = dy.astype(BF16)
        gr_ref[...] = _dot(dyb, cr_ref[...])
        gi_ref[...] = _dot(dyb, ci_ref[...])
        dus_ref[...] = dy * d_ref[...]

        @pl.when(pl.program_id(1) == 0)
        def _():
            gcr_ref[...] = jnp.zeros_like(gcr_ref)
            gci_ref[...] = jnp.zeros_like(gci_ref)
            gd_ref[...] = jnp.zeros_like(gd_ref)

        gcr_ref[...] += _dot(xr_ref[...].astype(BF16), dyb, TN)
        gci_ref[...] -= _dot(xi_ref[...].astype(BF16), dyb, TN)
        gd_ref[...] += jnp.sum(dy * u_ref[...], axis=0, keepdims=True)

    u128 = pl.BlockSpec((SL, 128), lambda j, i: (i, j))
    xs = pl.BlockSpec((SL, SCB), lambda j, i: (i, j))
    blk = pl.BlockSpec((None, 128, SCB), lambda j, i: (j, 0, 0))
    gblk = pl.BlockSpec((None, SCB, 128), lambda j, i: (j, 0, 0))
    vec = pl.BlockSpec((1, 128), lambda j, i: (0, j))
    return pl.pallas_call(
        body, name="s5_out_bwd", grid=(SB, nl),
        in_specs=[u128, u128, blk, blk, pl.BlockSpec((SL, 128), lambda j, i: (i, uoff + j)), vec, xs, xs],
        out_specs=[xs, xs, u128, gblk, gblk, vec],
        out_shape=[jax.ShapeDtypeStruct((n, NCH), F32)] * 2 + [jax.ShapeDtypeStruct((n, DS5), F32)]
        + [jax.ShapeDtypeStruct((SB, SCB, 128), F32)] * 2 + [jax.ShapeDtypeStruct((1, DS5), F32)],
        compiler_params=_cp("parallel", "arbitrary"),
    )(dg, ypre, crt, cimnt, proj, dvec, xre, xim)


def _s5_in_bwd(are, aim, brt, bit, proj, dus, xre, xim):
    n = are.shape[0]
    uoff = OFF_U // 128
    per = SL // 8

    def body(ar_ref, ai_ref, br_ref, bi_ref, u_ref, dus_ref, xr_ref, xi_ref, pr_ref, pi_ref,
             du_ref, gbr_ref, gbi_ref, glr_ref, gli_ref):
        i = pl.program_id(1)
        ar, ai = ar_ref[...], ai_ref[...]
        arb, aib = ar.astype(BF16), ai.astype(BF16)
        du_ref[...] = (_dot(arb, br_ref[...]) + _dot(aib, bi_ref[...]) + dus_ref[...]).astype(BF16)

        @pl.when(i == 0)
        def _():
            for r in (gbr_ref, gbi_ref, glr_ref, gli_ref):
                r[...] = jnp.zeros_like(r)

        ub = u_ref[...].astype(BF16)
        gbr_ref[...] += _dot(arb, ub, TN)
        gbi_ref[...] += _dot(aib, ub, TN)
        row0 = lax.broadcasted_iota(jnp.int32, (SL, SCB), 0) == 0
        last_r = jnp.where(i > 0, pr_ref[7:8, :], 0.0)
        last_i = jnp.where(i > 0, pi_ref[7:8, :], 0.0)
        xpr = jnp.where(row0, last_r, pltpu.roll(xr_ref[...], 1, 0))
        xpi = jnp.where(row0, last_i, pltpu.roll(xi_ref[...], 1, 0))
        glr_ref[...] += jnp.sum(ar * xpr + ai * xpi, axis=0, keepdims=True)
        gli_ref[...] += jnp.sum(ai * xpr - ar * xpi, axis=0, keepdims=True)

    xs = pl.BlockSpec((SL, SCB), lambda j, i: (i, j))
    prev = pl.BlockSpec((8, SCB), lambda j, i: (jnp.maximum(i * per - 1, 0), j))
    blk = pl.BlockSpec((None, SCB, 128), lambda j, i: (j, 0, 0))
    u128 = pl.BlockSpec((SL, 128), lambda j, i: (i, j))
    vec = pl.BlockSpec((1, SCB), lambda j, i: (0, j))
    return pl.pallas_call(
        body, name="s5_in_bwd", grid=(SB, n // SL),
        in_specs=[xs, xs, blk, blk, pl.BlockSpec((SL, 128), lambda j, i: (i, uoff + j)), u128, xs, xs, prev, prev],
        out_specs=[u128, blk, blk, vec, vec],
        out_shape=[jax.ShapeDtypeStruct((n, DS5), BF16)] + [jax.ShapeDtypeStruct((SB, SCB, 128), F32)] * 2
        + [jax.ShapeDtypeStruct((1, NCH), F32)] * 2,
        compiler_params=_cp("parallel", "arbitrary"),
    )(are, aim, brt, bit, proj, dus, xre, xim, xre, xim)


MC = 1024


def _merge_specs():
    ga = pl.BlockSpec((TL, MC), lambda i, j: (i, OFF_GA // MC + j))
    gb = pl.BlockSpec((TL, MC), lambda i, j: (i, OFF_GB // MC + j))
    col = pl.BlockSpec((TL, MC), lambda i, j: (i, j))
    gate = pl.BlockSpec((TL, MC), lambda i, j: (i, D // MC + j))
    return ga, gb, col, gate


def _merge_fwd(proj, ya, vg):
    n = ya.shape[0]

    def body(ga_ref, gb_ref, ya_ref, v_ref, g_ref, o_ref):
        yb = v_ref[...] * _sig(g_ref[...])
        o_ref[...] = (_sig(ga_ref[...]) * ya_ref[...] + _sig(gb_ref[...]) * yb).astype(BF16)

    ga, gb, col, gate = _merge_specs()
    return pl.pallas_call(
        body, name="merge_fwd", grid=(n // TL, D // MC), in_specs=[ga, gb, col, col, gate], out_specs=col,
        out_shape=jax.ShapeDtypeStruct((n, D), BF16), compiler_params=_cp("parallel", "parallel"),
    )(proj, proj, ya, vg, vg)


def _merge_bwd(dm, proj, ya, vg):
    n = ya.shape[0]

    def body(dm_ref, ga_ref, gb_ref, ya_ref, v_ref, g_ref, dga_ref, dgb_ref, dya_ref, dv_ref, dg_ref):
        d = dm_ref[...]
        sa, sb, sg = _sig(ga_ref[...]), _sig(gb_ref[...]), _sig(g_ref[...])
        v = v_ref[...]
        yb = v * sg
        dga_ref[...] = (d * ya_ref[...] * sa * (1.0 - sa)).astype(BF16)
        dgb_ref[...] = (d * yb * sb * (1.0 - sb)).astype(BF16)
        dya_ref[...] = (d * sa).astype(BF16)
        dyb = d * sb
        dv_ref[...] = (dyb * sg).astype(BF16)
        dg_ref[...] = (dyb * v * sg * (1.0 - sg)).astype(BF16)

    ga, gb, col, gate = _merge_specs()
    o = jax.ShapeDtypeStruct((n, D), BF16)
    return pl.pallas_call(
        body, name="merge_bwd", grid=(n // TL, D // MC), in_specs=[col, ga, gb, col, col, gate],
        out_specs=[col] * 5, out_shape=[o] * 5, compiler_params=_cp("parallel", "parallel"),
    )(dm, proj, proj, ya, vg, vg)


def _adamw_update(wv, gv, mv, vv):
    nm = B1 * mv + (1.0 - B1) * gv
    nv = B2 * vv + (1.0 - B2) * (gv * gv)
    m_hat = nm / (1.0 - B1 ** STEP)
    v_hat = nv / (1.0 - B2 ** STEP)
    return -LR * (m_hat / (jnp.sqrt(v_hat) + AEPS) + WD * wv), nm, nv


def _adamw(w, g, m, v, name):
    r, c = w.shape
    tr = _pick(r, 128)

    def body(w_ref, g_ref, m_ref, v_ref, d_ref, nm_ref, nv_ref):
        d_ref[...], nm_ref[...], nv_ref[...] = _adamw_update(w_ref[...], g_ref[...], m_ref[...], v_ref[...])

    blk = pl.BlockSpec((tr, c), lambda i: (i, 0))
    o = jax.ShapeDtypeStruct((r, c), F32)
    return pl.pallas_call(
        body, name=name, grid=(r // tr,), in_specs=[blk] * 4, out_specs=[blk] * 3, out_shape=[o] * 3,
        compiler_params=_cp("parallel"),
    )(w, g, m, v)


def _adamw_halves(w, g_mine, g_other, m, v, cidx, name):
    _, r, c = w.shape
    hr, gc = g_mine.shape
    tr = _pick(hr, 128)
    nbh = hr // tr
    assert gc == c and 2 * hr - tr < r <= 2 * hr

    def body(cs, w_ref, gm_ref, go_ref, m_ref, v_ref, g_ref, d_ref, nm_ref, nv_ref):
        mine = pl.program_id(0) // nbh == cs[0]
        gv = jnp.where(mine, gm_ref[...], go_ref[...])
        g_ref[...] = gv
        d_ref[...], nm_ref[...], nv_ref[...] = _adamw_update(w_ref[...], gv, m_ref[...], v_ref[...])

    blk = pl.BlockSpec((None, tr, c), lambda i, cs: (0, i, 0))
    gmine = pl.BlockSpec((tr, gc), lambda i, cs: (jnp.where(i // nbh == cs[0], i % nbh, 0), 0))
    gother = pl.BlockSpec((tr, gc), lambda i, cs: (jnp.where(i // nbh == cs[0], 0, i % nbh), 0))
    o = jax.ShapeDtypeStruct((1, r, c), F32)
    return pl.pallas_call(
        body, name=name,
        grid_spec=pltpu.PrefetchScalarGridSpec(num_scalar_prefetch=1, grid=(2 * nbh,),
                                               in_specs=[blk, gmine, gother, blk, blk], out_specs=[blk] * 4),
        out_shape=[o] * 4, compiler_params=_cp("parallel"),
    )(cidx, w, g_mine, g_other, m, v)


def _chip_sum(part, sib, cidx, name):
    _, r, cc = part.shape
    hr = r // 2
    tr = _pick(hr, 256)

    def body(cs, p_ref, s_ref, o_ref):
        o_ref[...] = (p_ref[...].astype(F32) + s_ref[...].astype(F32)).astype(BF16)

    blk = pl.BlockSpec((None, tr, cc), lambda k, i, cs: (k, i, 0))
    return pl.pallas_call(
        body, name=name,
        grid_spec=pltpu.PrefetchScalarGridSpec(
            num_scalar_prefetch=1, grid=(4, hr // tr),
            in_specs=[pl.BlockSpec((None, None, tr, cc), lambda k, i, cs: (k, cs[0], i, 0)), blk], out_specs=blk),
        out_shape=jax.ShapeDtypeStruct((4, hr, cc), BF16), compiler_params=_cp("parallel", "parallel"),
    )(cidx, part.reshape(4, 2, hr, cc), sib)


def _shard_sum(own, got, sidx, name):
    _, hr, cc = own.shape
    tr = _pick(hr, 256)

    def body(cs, own_ref, g0, g1, g2, g3, o_ref):
        acc = None
        for k, g_ref in enumerate((g0, g1, g2, g3)):
            term = jnp.where(cs[0] == k, own_ref[...], g_ref[...]).astype(F32)
            acc = term if acc is None else acc + term
        o_ref[...] = acc

    def got_spec(k):
        return pl.BlockSpec((None, tr, cc), lambda i, cs: (jnp.where(cs[0] == k, (k + 1) % 4, k), i, 0))

    return pl.pallas_call(
        body, name=name,
        grid_spec=pltpu.PrefetchScalarGridSpec(
            num_scalar_prefetch=1, grid=(hr // tr,),
            in_specs=[pl.BlockSpec((None, tr, cc), lambda i, cs: (cs[0], i, 0))] + [got_spec(k) for k in range(4)],
            out_specs=pl.BlockSpec((tr, cc), lambda i, cs: (i, 0))),
        out_shape=jax.ShapeDtypeStruct((hr, cc), F32), compiler_params=_cp("parallel"),
    )(sidx, own, got, got, got, got)


def _sum_slabs(xs, name, out_dtype=F32):
    r, c = xs[0].shape
    tr = _pick(r, 256)

    def body(*refs):
        acc = refs[0][...].astype(F32)
        for ref in refs[1:-1]:
            acc = acc + ref[...].astype(F32)
        refs[-1][...] = acc.astype(out_dtype)

    blk = pl.BlockSpec((tr, c), lambda i: (i, 0))
    return pl.pallas_call(
        body, name=name, grid=(r // tr,), in_specs=[blk] * len(xs), out_specs=blk,
        out_shape=jax.ShapeDtypeStruct((r, c), out_dtype), compiler_params=_cp("parallel"),
    )(*xs)


def _place():
    return lax.axis_index("x"), lax.axis_index("y"), lax.axis_index("c")


def _gather_small(v, after):
    m_per, n = v.shape

    def body(x_ref, _, out_ref, send_sems, recv_sems, local_sem):
        x, y, c = _place()
        me, sibling = (x, y, c), (x, y, 1 - c)
        chips = [(1 - x, y), (x, 1 - y), (1 - x, 1 - y)]

        def rows(px, py, pc):
            return out_ref.at[pl.ds((4 * px + 2 * py + pc) * m_per, m_per), :]

        def copy(k, block, to, src=None):
            return pltpu.make_async_remote_copy(
                src_ref=rows(*block) if src is None else src, dst_ref=rows(*block),
                send_sem=send_sems.at[k], recv_sem=recv_sems.at[k], device_id=to, device_id_type=MESH)

        mine = pltpu.make_async_copy(x_ref, rows(*me), local_sem)
        mine.start()
        first = [copy(0, me, sibling, src=x_ref)]
        first += [copy(1 + j, me, (*chip, c), src=x_ref) for j, chip in enumerate(chips)]
        for cp in first:
            cp.start()
        passed = [copy(4 + j, (*chip, c), sibling) for j, chip in enumerate(chips)]
        for j, chip in enumerate(chips):
            copy(1 + j, (*chip, c), me).wait_recv()
            passed[j].start()
        copy(0, sibling, me).wait_recv()
        for j, chip in enumerate(chips):
            copy(4 + j, (*chip, 1 - c), me).wait_recv()
        for cp in first + passed:
            cp.wait_send()
        mine.wait()

    return pl.pallas_call(
        body, name="gather_small_%d" % m_per,
        out_shape=jax.ShapeDtypeStruct((8 * m_per, n), v.dtype),
        in_specs=[pl.BlockSpec(memory_space=pltpu.VMEM), ANY], out_specs=pl.BlockSpec(memory_space=pltpu.VMEM),
        scratch_shapes=[pltpu.SemaphoreType.DMA((7,)), pltpu.SemaphoreType.DMA((7,)), pltpu.SemaphoreType.DMA],
        compiler_params=pltpu.CompilerParams(vmem_limit_bytes=VMEM_LIMIT),
    )(v, after)


def _allsum_small(v, name, after):
    r = v.shape[0]
    g = _gather_small(v, after)
    return _sum_slabs([g[k * r:(k + 1) * r] for k in range(8)], name)


def _gather_big(shards):
    nt = len(shards)

    def body(*refs):
        ins, outs = refs[:nt], refs[nt:2 * nt]
        send_sems, recv_sems = refs[2 * nt:]
        x, y, c = _place()
        s = 2 * x + y
        sibling = (x, y, 1 - c)
        chips = [(1 - x, y), (x, 1 - y), (1 - x, 1 - y)]

        def half(t, slot, h):
            hr = ins[t].shape[0] // 2
            return outs[t].at[slot, pl.ds(h * hr, hr), :]

        def ici(t, j, src_slot, to):
            hr = ins[t].shape[0] // 2
            return pltpu.make_async_remote_copy(
                src_ref=ins[t].at[pl.ds(c * hr, hr), :], dst_ref=half(t, src_slot, c),
                send_sem=send_sems.at[7 * t + j], recv_sem=recv_sems.at[7 * t + j], device_id=to, device_id_type=MESH)

        def d2d(t, j, slot, h):
            return pltpu.make_async_remote_copy(
                src_ref=half(t, slot, h), dst_ref=half(t, slot, h),
                send_sem=send_sems.at[7 * t + 3 + j], recv_sem=recv_sems.at[7 * t + 3 + j],
                device_id=sibling, device_id_type=MESH)

        def whole(t):
            return pltpu.make_async_remote_copy(
                src_ref=ins[t], dst_ref=outs[t].at[s], send_sem=send_sems.at[7 * t + 6],
                recv_sem=recv_sems.at[7 * t + 6], device_id=sibling, device_id_type=MESH)

        sends = [ici(t, j, s, (*chip, c)) for t in range(nt) for j, chip in enumerate(chips)]
        sends += [whole(t) for t in range(nt)]
        for cp in sends:
            cp.start()
        passed = []
        for t in range(nt):
            for j, (px, py) in enumerate(chips):
                ici(t, j, 2 * px + py, (x, y, c)).wait_recv()
                cp = d2d(t, j, 2 * px + py, c)
                cp.start()
                passed.append(cp)
        for t in range(nt):
            for j, (px, py) in enumerate(chips):
                d2d(t, j, 2 * px + py, 1 - c).wait_recv()
            whole(t).wait_recv()
        for cp in sends + passed:
            cp.wait_send()

    return pl.pallas_call(
        body, name="gather_big",
        out_shape=[jax.ShapeDtypeStruct((4,) + a.shape, a.dtype) for a in shards],
        in_specs=[ANY] * nt, out_specs=[ANY] * nt,
        scratch_shapes=[pltpu.SemaphoreType.DMA((7 * nt,)), pltpu.SemaphoreType.DMA((7 * nt,))],
    )(*shards)


def _swap_halves(parts, name):
    nt = len(parts)

    def body(*refs):
        ins, outs = refs[:nt], refs[nt:2 * nt]
        send_sems, recv_sems = refs[2 * nt:]
        x, y, c = _place()
        cps = []
        for t in range(nt):
            hr = ins[t].shape[1] // 2
            cps.append(pltpu.make_async_remote_copy(
                src_ref=ins[t].at[:, pl.ds((1 - c) * hr, hr), :], dst_ref=outs[t],
                send_sem=send_sems.at[t], recv_sem=recv_sems.at[t], device_id=(x, y, 1 - c), device_id_type=MESH))
        for cp in cps:
            cp.start()
        for cp in cps:
            cp.wait()

    return pl.pallas_call(
        body, name=name,
        out_shape=[jax.ShapeDtypeStruct((4, a.shape[1] // 2, a.shape[2]), a.dtype) for a in parts],
        in_specs=[ANY] * nt, out_specs=[ANY] * nt,
        scratch_shapes=[pltpu.SemaphoreType.DMA((nt,)), pltpu.SemaphoreType.DMA((nt,))],
    )(*parts)


def _scatter_chips(parts):
    nt = len(parts)

    def body(*refs):
        ins, outs = refs[:nt], refs[nt:2 * nt]
        send_sems, recv_sems = refs[2 * nt:]
        x, y, c = _place()
        s = 2 * x + y
        chips = [(1 - x, y), (x, 1 - y), (1 - x, 1 - y)]
        cps = []
        for t in range(nt):
            for j, (px, py) in enumerate(chips):
                cps.append(pltpu.make_async_remote_copy(
                    src_ref=ins[t].at[2 * px + py], dst_ref=outs[t].at[s],
                    send_sem=send_sems.at[3 * t + j], recv_sem=recv_sems.at[3 * t + j],
                    device_id=(px, py, c), device_id_type=MESH))
        for cp in cps:
            cp.start()
        for t in range(nt):
            for j, (px, py) in enumerate(chips):
                pltpu.make_async_remote_copy(
                    src_ref=ins[t].at[s], dst_ref=outs[t].at[2 * px + py],
                    send_sem=send_sems.at[3 * t + j], recv_sem=recv_sems.at[3 * t + j],
                    device_id=(px, py, c), device_id_type=MESH).wait_recv()
        for cp in cps:
            cp.wait_send()

    return pl.pallas_call(
        body, name="scatter_chips",
        out_shape=[jax.ShapeDtypeStruct(a.shape, a.dtype) for a in parts],
        in_specs=[ANY] * nt, out_specs=[ANY] * nt,
        scratch_shapes=[pltpu.SemaphoreType.DMA((3 * nt,)), pltpu.SemaphoreType.DMA((3 * nt,))],
    )(*parts)


def _swap_whole(halves):
    nt = len(halves)

    def body(*refs):
        ins, outs = refs[:nt], refs[nt:2 * nt]
        send_sems, recv_sems = refs[2 * nt:]
        x, y, c = _place()
        cps = [pltpu.make_async_remote_copy(
            src_ref=ins[t], dst_ref=outs[t], send_sem=send_sems.at[t], recv_sem=recv_sems.at[t],
            device_id=(x, y, 1 - c), device_id_type=MESH) for t in range(nt)]
        for cp in cps:
            cp.start()
        for cp in cps:
            cp.wait()

    return pl.pallas_call(
        body, name="swap_whole",
        out_shape=[jax.ShapeDtypeStruct(a.shape, a.dtype) for a in halves],
        in_specs=[ANY] * nt, out_specs=[ANY] * nt,
        scratch_shapes=[pltpu.SemaphoreType.DMA((nt,)), pltpu.SemaphoreType.DMA((nt,))],
    )(*halves)


def _pass_halves(got, shards, name):
    nt = len(got)

    def body(*refs):
        ins, own, outs = refs[:nt], refs[nt:2 * nt], refs[2 * nt:3 * nt]
        send_sems, recv_sems = refs[3 * nt:]
        x, y, c = _place()
        s = 2 * x + y
        chips = [(1 - x, y), (x, 1 - y), (1 - x, 1 - y)]

        def half(ref, t, slot, h):
            hr = ins[t].shape[1] // 2
            return ref.at[slot, pl.ds(h * hr, hr), :]

        def copy(t, j, h):
            px, py = chips[j]
            return pltpu.make_async_remote_copy(
                src_ref=half(ins[t], t, 2 * px + py, h), dst_ref=half(outs[t], t, 2 * px + py, h),
                send_sem=send_sems.at[4 * t + j], recv_sem=recv_sems.at[4 * t + j],
                device_id=(x, y, 1 - c), device_id_type=MESH)

        def whole(t):
            return pltpu.make_async_remote_copy(
                src_ref=own[t], dst_ref=outs[t].at[s], send_sem=send_sems.at[4 * t + 3],
                recv_sem=recv_sems.at[4 * t + 3], device_id=(x, y, 1 - c), device_id_type=MESH)

        sends = [copy(t, j, c) for t in range(nt) for j in range(3)] + [whole(t) for t in range(nt)]
        for cp in sends:
            cp.start()
        for t in range(nt):
            for j in range(3):
                copy(t, j, 1 - c).wait_recv()
            whole(t).wait_recv()
        for cp in sends:
            cp.wait_send()

    return pl.pallas_call(
        body, name=name,
        out_shape=[jax.ShapeDtypeStruct(a.shape, a.dtype) for a in got],
        in_specs=[ANY] * (2 * nt), out_specs=[ANY] * nt, input_output_aliases={t: t for t in range(nt)},
        scratch_shapes=[pltpu.SemaphoreType.DMA((4 * nt,)), pltpu.SemaphoreType.DMA((4 * nt,))],
    )(*got, *shards)


HBM = pl.BlockSpec(memory_space=pltpu.HBM)
SEM = pl.BlockSpec(memory_space=pltpu.SEMAPHORE)
EFFECT = pltpu.SideEffectType.DATAFLOW_SIDE_EFFECTING


PER_TENSOR = {"gather": 3, "scatter": 3, "swap": 1, "pass": 4, "whole": 1}


def _ici_copies(kind, srcs, lands, send_sems, recv_sems):
    x, y, c = _place()
    s = 2 * x + y
    sib = (x, y, 1 - c)
    chips = [(1 - x, y), (x, 1 - y), (1 - x, 1 - y)]
    cps = []

    def add(src, dst, dev):
        k = len(cps)
        cps.append(pltpu.make_async_remote_copy(src_ref=src, dst_ref=dst, send_sem=send_sems[k], recv_sem=recv_sems[k],
                                                device_id=dev, device_id_type=MESH))

    for t in range(len(srcs)):
        if kind == "gather":
            hr = srcs[t].shape[0] // 2
            for px, py in chips:
                add(srcs[t].at[pl.ds(c * hr, hr), :], lands[t].at[s, pl.ds(c * hr, hr), :], (px, py, c))
        elif kind == "scatter":
            for px, py in chips:
                add(srcs[t].at[2 * px + py], lands[t].at[s], (px, py, c))
        elif kind == "swap":
            hr = srcs[t].shape[1] // 2
            add(srcs[t].at[:, pl.ds((1 - c) * hr, hr), :], lands[t], sib)
        elif kind == "pass":
            hr = srcs[t].shape[1] // 2
            for px, py in chips:
                half = srcs[t].at[2 * px + py, pl.ds(c * hr, hr), :]
                add(half, half, sib)
            add(lands[t], srcs[t].at[s], sib)
        else:
            add(srcs[t], lands[t], sib)
    return cps


def _ici_start(kind, srcs, after, name, lands=None):
    nt = len(srcs)
    nc = PER_TENSOR[kind] * nt
    hbm = lambda a: pltpu.with_memory_space_constraint(a, pltpu.HBM)
    if lands is None:
        shape = {"gather": lambda a: (4,) + a.shape, "scatter": lambda a: a.shape,
                 "swap": lambda a: (4, a.shape[1] // 2, a.shape[2]), "whole": lambda a: a.shape}[kind]
        lands = [lax.empty(shape(a), a.dtype) for a in srcs]

    def body(*refs):
        src, land = refs[:nt], refs[nt:2 * nt]
        outs = refs[2 * nt + 1:]
        for cp in _ici_copies(kind, src, land, outs[:nc], outs[nc:2 * nc]):
            cp.start()
        outs[-1][...] = jnp.zeros_like(outs[-1])

    outs = pl.pallas_call(
        body, name=name,
        out_shape=tuple([pltpu.SemaphoreType.DMA(())] * (2 * nc) + [pltpu.HBM(a.shape, a.dtype) for a in srcs]
                        + [pltpu.HBM(a.shape, a.dtype) for a in lands] + [jax.ShapeDtypeStruct((8, 128), F32)]),
        in_specs=[HBM] * (2 * nt) + [ANY],
        out_specs=tuple([SEM] * (2 * nc) + [HBM] * (2 * nt) + [pl.BlockSpec(memory_space=pltpu.VMEM)]),
        input_output_aliases={i: 2 * nc + i for i in range(2 * nt)},
        compiler_params=pltpu.CompilerParams(has_side_effects=EFFECT),
    )(*[hbm(a) for a in srcs], *[hbm(a) for a in lands], after)
    return outs[:2 * nc], outs[2 * nc:2 * nc + nt], outs[2 * nc + nt:2 * nc + 2 * nt], outs[-1]


def _ici_wait(kind, sems, srcs, lands, after, name):
    nt = len(srcs)
    nc = PER_TENSOR[kind] * nt

    def body(*refs):
        src, land = refs[:nt], refs[nt:2 * nt]
        sem = refs[2 * nt:2 * nt + 2 * nc]
        for cp in _ici_copies(kind, src, land, sem[:nc], sem[nc:]):
            cp.wait_send()
            cp.wait_recv()

    outs = pl.pallas_call(
        body, name=name,
        out_shape=tuple(pltpu.HBM(a.shape, a.dtype) for a in list(srcs) + list(lands)),
        in_specs=[HBM] * (2 * nt) + [SEM] * (2 * nc) + [ANY], out_specs=tuple([HBM] * (2 * nt)),
        input_output_aliases={i: i for i in range(2 * nt)},
        compiler_params=pltpu.CompilerParams(has_side_effects=EFFECT),
    )(*srcs, *lands, *sems, after)
    return outs[:nt], outs[nt:]


def _s5_params(lam_re, lam_im, log_dt, b_re, b_im):
    lr = jnp.minimum(lam_re, EIG_MAX)
    dt = jnp.exp(log_dt)[:, None]
    mag = jnp.exp(lr * dt)
    lbr, lbi = mag * jnp.cos(lam_im * dt), mag * jnp.sin(lam_im * dt)
    den = lr * lr + lam_im * lam_im
    qr = ((lbr - 1.0) * lr + lbi * lam_im) / den
    qi = (lbi * lr - (lbr - 1.0) * lam_im) / den
    bbr = qr[..., None] * b_re - qi[..., None] * b_im
    bbi = qr[..., None] * b_im + qi[..., None] * b_re
    return lbr, lbi, bbr, bbi


def _cmul(a, b):
    return a[0] * b[0] - a[1] * b[1], a[0] * b[1] + a[1] * b[0]


def _scan_table(lr, li, reverse):
    l1 = (lr.reshape(1, NCH), li.reshape(1, NCH))
    pows = [l1]
    for _ in range(7):
        pows.append(_cmul(pows[-1], l1))
    r = jnp.arange(8)[:, None]
    tabs = []
    for k in (1, 2, 4):
        keep = (r < 8 - k) if reverse else (r >= k)
        tabs += [jnp.where(keep, pows[k - 1][0], 0.0), jnp.where(keep, pows[k - 1][1], 0.0)]
    order = range(7, -1, -1) if reverse else range(8)
    tabs += [jnp.concatenate([pows[e][0] for e in order], axis=0), jnp.concatenate([pows[e][1] for e in order], axis=0)]
    return jnp.stack(tabs).astype(F32)


_EYE8 = lambda: jnp.eye(8, dtype=F32)


def _to_in_blocks(b):
    return jnp.einsum("jgpc,gh->jgchp", b.reshape(8, 8, 64, 16), _EYE8()).reshape(8, 128, 512)


def _to_out_blocks(cm):
    return jnp.einsum("jgcp,gh->jgphc", cm.reshape(8, 8, 16, 64), _EYE8()).reshape(8, 512, 128)


def _from_out_blocks(g):
    return jnp.einsum("jgphc,gh->jgpc", g.reshape(8, 8, 64, 8, 16), _EYE8()).reshape(64, 64, 16)


def _local_step(x, target, hn1, proj, p, hooks):
    n = x.shape[0]
    g = {}
    dtraw = proj
    xbc = _conv_a_fwd(proj, p["conv_a_w"], p["conv_a_b"])
    to_lanes = lambda v: jnp.pad(jnp.pad(v.reshape(NG, HPG), ((0, 0), (0, 8 - HPG))).reshape(1, 8 * NG),
                                 ((0, 0), (0, HL - 8 * NG)))
    from_lanes = lambda v: v[:, :8 * NG].reshape(-1, NG, 8)[:, :, :HPG].reshape(-1, NG * HPG)
    hp = jnp.concatenate([to_lanes(p["dt_bias"]), to_lanes(p["a_log"]), jnp.zeros((6, HL), F32)], axis=0)
    lane = jnp.arange(HL)[:, None]
    emat = ((lane < 8 * NG) & (lane % 8 < HPG)
            & (jnp.arange(DI)[None, :] // HD == HPG * (lane // 8) + lane % 8)).astype(BF16)
    dexp = jnp.repeat(p["d_a"].reshape(1, NG * HPG), HD, axis=1)
    dt, s_cum, s_t, dt_e, s_e = _ssd_prep(dtraw, hp, emat)
    s8 = s_cum[:, :8 * NG].reshape(n, NG, 8).transpose(1, 0, 2)
    yssd, sprev = _ssd_fwd(xbc, s8, s_t, dt_e, s_e, dexp)
    yn = _gnorm_fwd(yssd, proj, p["norm_a_w"])
    tok = hooks["late_start"](yn)
    (lbr, lbi, bbr, bbi), s5_vjp = jax.vjp(_s5_params, p["s5_lam_re"], p["s5_lam_im"], p["s5_log_dt"],
                                           p["s5_b_re"], p["s5_b_im"])
    bin_r, bin_i = _to_in_blocks(bbr), _to_in_blocks(bbi)
    cout_r, cout_in = _to_out_blocks(p["s5_c_re"]), _to_out_blocks(-p["s5_c_im"])
    bur, bui = _s5_in(proj, bin_r.astype(BF16), bin_i.astype(BF16), after=tok)
    xre, xim = _s5_scan(bur, bui, _scan_table(lbr, lbi, False), False, "s5_scan_fwd")
    ypre, g5 = _s5_out(xre, xim, cout_r.astype(BF16), cout_in.astype(BF16), proj, p["s5_d"])
    p = {**p, **hooks["late_weights"](ypre)}
    ya = _matmul(yn, p["w_proj_a"], "nn", "mm_proj")
    vg = _matmul(g5, p["w_s5_glu"], "nn", "mm_glu", b_stacked=True)
    merged = _merge_fwd(proj, ya, vg)
    h1 = _matmul(merged, p["w_out"], "nn", "mm_out", residual=x)
    hn2 = _rms_fwd(h1, p["norm_ffn_w"], "rms_ffn")
    up = _matmul(hn2, p["w_up"], "nn", "mm_up", tn=1408, b_stacked=True)
    act = _conv_ffn_fwd(up, p["conv_ffn_w"], p["conv_ffn_b"])
    h2 = _matmul(act, p["w_down"], "nn", "mm_down", tk=DFF // 2, residual=h1)
    dh2, dh2b, g["norm_final_w"], loss_blk = _final(h2, p["norm_final_w"], target)
    g["w_down"] = _matmul(act, dh2b, "tn", "mm_gw_down", out_dtype=BF16).reshape(4, DFF // 4, D)
    dact = _matmul(dh2b, p["w_down"], "nt", "mm_dact", out_dtype=BF16)
    dup, g["conv_ffn_w"], g["conv_ffn_b"] = _conv_ffn_bwd(up, dact, p["conv_ffn_w"], p["conv_ffn_b"])
    g["w_up"] = _matmul(hn2, dup, "tn", "mm_gw_up", out_dtype=BF16, tn=1408, out_stacked=True)
    tok = hooks["swap_start"](["w_up", "w_down"], g, "s1")
    dhn2 = _matmul(dup, p["w_up"], "nt", "mm_dhn2", tk=2816, b_stacked=True, after=tok)
    tok = hooks["scatter_go"]("s1", dhn2)
    dh1, dh1b, g["norm_ffn_w"] = _rms_bwd(dhn2, h1, p["norm_ffn_w"], dh2, "rms_ffn_bwd", after=tok)
    g["w_out"] = _matmul(merged, dh1b, "tn", "mm_gw_out", out_dtype=BF16).reshape(4, D // 4, D)
    dmerged = _matmul(dh1b, p["w_out"], "nt", "mm_dmerged")
    dga, dgb, dya, dval, dgate = _merge_bwd(dmerged, proj, ya, vg)
    dvg = jnp.concatenate([dval, dgate], axis=1)
    g["w_s5_glu"] = _matmul(g5, dvg, "tn", "mm_gw_glu", out_dtype=BF16, out_stacked=True)
    dg5 = _matmul(dvg, p["w_s5_glu"], "nt", "mm_dg5", b_stacked=True)
    tr = lambda b: b.transpose(0, 2, 1)
    gxr, gxi, dus, gcr, gci, g["s5_d"] = _s5_out_bwd(dg5, ypre, tr(cout_r).astype(BF16), tr(cout_in).astype(BF16),
                                                     proj, p["s5_d"], xre, xim)
    are, aim = _s5_scan(gxr, gxi, _scan_table(lbr, -lbi, True), True, "s5_scan_bwd")
    du, gbr, gbi, glr, gli = _s5_in_bwd(are, aim, tr(bin_r).astype(BF16), tr(bin_i).astype(BF16), proj, dus, xre, xim)
    g["s5_c_re"] = _from_out_blocks(gcr).transpose(0, 2, 1)
    g["s5_c_im"] = _from_out_blocks(gci).transpose(0, 2, 1)
    (g["s5_lam_re"], g["s5_lam_im"], g["s5_log_dt"], g["s5_b_re"], g["s5_b_im"]) = s5_vjp(
        (glr.reshape(64, 64), gli.reshape(64, 64), _from_out_blocks(gbr), _from_out_blocks(gbi)))
    g["w_proj_a"] = _matmul(yn, dya, "tn", "mm_gw_proj", out_dtype=BF16).reshape(4, DI // 4, D)
    tok = hooks["swap_start"](["w_proj_a", "w_s5_glu", "w_out"], g, "s2")
    dyn = _matmul(dya, p["w_proj_a"], "nt", "mm_dyn", after=tok)
    tok = hooks["scatter_go"]("s2", dyn)
    dyssd, dz, g["norm_a_w"] = _gnorm_bwd(dyn, yssd, proj, p["norm_a_w"], tok)
    dxs, dbm, dcm, ds_e, ddt_e, tsum, dsh8, pd = _ssd_bwd(xbc, s8, s_t, dt_e, s_e, dexp, sprev, dyssd)
    dsh = jnp.pad(dsh8.transpose(1, 0, 2).reshape(n, 8 * NG), ((0, 0), (0, HL - 8 * NG)))
    draw, ps = _ssd_post(ds_e, ddt_e, tsum, dsh, dtraw, dt, hp, emat.T)
    g["dt_bias"] = from_lanes(ps[0:1])
    g["a_log"] = from_lanes(ps[1:2])
    g["d_a"] = pd.reshape(NG * HPG, HD).sum(axis=1).reshape(1, NG * HPG)
    ddt = draw.astype(BF16)
    dxbc_parts, gcw, gcb = [], [], []
    for arr, col0, nm in ((dxs, 0, "conv_a_bwd_x"), (dbm, DI, "conv_a_bwd_b"), (dcm, DI + NG * NS, "conv_a_bwd_c")):
        dpart, gw_, gb_ = _conv_a_bwd(proj, arr, p["conv_a_w"], p["conv_a_b"], col0, nm)
        dxbc_parts.append(dpart)
        gcw.append(gw_)
        gcb.append(gb_)
    g["conv_a_w"] = jnp.concatenate(gcw, axis=1)
    g["conv_a_b"] = jnp.concatenate(gcb, axis=1)
    dproj = jnp.concatenate([dz] + dxbc_parts + [du, dga, dgb, ddt], axis=1)
    g_main = _matmul(dproj, hn1, "tn", "mm_gw_in", out_dtype=BF16, tm=896, tn=2048)
    g_dt = g_main[NMAIN:NMAIN + 8 * NG].reshape(NG, 8, D)[:, :HPG].reshape(NG * HPG, D)
    g_sh = _move_rows(g_main, RUNS_TO_SHARDS, 4 * WPAD, MT, MT, "rows_to_shards")
    g["w_in"] = lax.dynamic_update_slice(g_sh, g_dt, (DT_SHARD_ROW, 0)).reshape(4, WPAD, D)
    hooks["swap_start"](["w_in"], g, "s3")
    tok = hooks["scatter_go"]("s3", g_dt)
    dhn1 = _matmul(dproj, p["w_full"], "nn", "mm_dhn1", tk=2688, after=tok)
    gx, _, g["norm_mix_w"] = _rms_bwd(dhn1, x, p["norm_mix_w"], dh1, "rms_mix_bwd")
    return loss_blk, gx, g


BIG = ["w_in", "w_proj_a", "w_s5_glu", "w_out", "w_up", "w_down"]
SMALL = ["norm_mix_w", "conv_a_w", "conv_a_b", "dt_bias", "a_log", "d_a", "norm_a_w", "s5_lam_re", "s5_lam_im",
         "s5_log_dt", "s5_b_re", "s5_b_im", "s5_c_re", "s5_c_im", "s5_d", "norm_ffn_w", "conv_ffn_w", "conv_ffn_b",
         "norm_final_w"]
ORDER = ["norm_mix_w", "w_in", "conv_a_w", "conv_a_b", "dt_bias", "a_log", "d_a", "norm_a_w", "w_proj_a", "s5_lam_re",
         "s5_lam_im", "s5_log_dt", "s5_b_re", "s5_b_im", "s5_c_re", "s5_c_im", "s5_d", "w_s5_glu", "w_out",
         "norm_ffn_w", "w_up", "conv_ffn_w", "conv_ffn_b", "w_down", "norm_final_w"]
CONV_FULL = {"conv_a_w": (KA, CONVD), "conv_ffn_w": (KF, 2 * DFF)}


def _pack(arrs):
    flat = jnp.concatenate([a.reshape(-1).astype(F32) for a in arrs])
    total = flat.shape[0]
    padded = -(-total // 1024) * 1024
    return jnp.pad(flat, (0, padded - total)).reshape(padded // 128, 128)


def _unpack(block, shapes):
    flat = block.reshape(-1)
    out, at = [], 0
    for sh in shapes:
        size = math.prod(sh)
        out.append(flat[at:at + size].reshape(sh))
        at += size
    return out


def _stack_cols(a):
    return a.transpose(1, 0, 2).reshape(a.shape[1], 4 * a.shape[2])


def _unstack_cols(a):
    return a.reshape(a.shape[0], 4, a.shape[1] // 4).transpose(1, 0, 2)


def kernel(x, norm_mix_w, w_in, conv_a_w, conv_a_b, dt_bias, a_log, d_a, norm_a_w, w_proj_a, s5_lam_re, s5_lam_im, s5_log_dt, s5_b_re, s5_b_im, s5_c_re, s5_c_im, s5_d, w_s5_glu, w_out, norm_ffn_w, w_up, conv_ffn_w, conv_ffn_b, w_down, norm_final_w, loss_target, m_norm_mix_w, m_w_in, m_conv_a_w, m_conv_a_b, m_dt_bias, m_a_log, m_d_a, m_norm_a_w, m_w_proj_a, m_s5_lam_re, m_s5_lam_im, m_s5_log_dt, m_s5_b_re, m_s5_b_im, m_s5_c_re, m_s5_c_im, m_s5_d, m_w_s5_glu, m_w_out, m_norm_ffn_w, m_w_up, m_conv_ffn_w, m_conv_ffn_b, m_w_down, m_norm_final_w, v_norm_mix_w, v_w_in, v_conv_a_w, v_conv_a_b, v_dt_bias, v_a_log, v_d_a, v_norm_a_w, v_w_proj_a, v_s5_lam_re, v_s5_lam_im, v_s5_log_dt, v_s5_b_re, v_s5_b_im, v_s5_c_re, v_s5_c_im, v_s5_d, v_w_s5_glu, v_w_out, v_norm_ffn_w, v_w_up, v_conv_ffn_w, v_conv_ffn_b, v_w_down, v_norm_final_w):
    w = dict(norm_mix_w=norm_mix_w, w_in=w_in, conv_a_w=conv_a_w, conv_a_b=conv_a_b, dt_bias=dt_bias, a_log=a_log, d_a=d_a, norm_a_w=norm_a_w, w_proj_a=w_proj_a, s5_lam_re=s5_lam_re, s5_lam_im=s5_lam_im, s5_log_dt=s5_log_dt, s5_b_re=s5_b_re, s5_b_im=s5_b_im, s5_c_re=s5_c_re, s5_c_im=s5_c_im, s5_d=s5_d, w_s5_glu=w_s5_glu, w_out=w_out, norm_ffn_w=norm_ffn_w, w_up=w_up, conv_ffn_w=conv_ffn_w, conv_ffn_b=conv_ffn_b, w_down=w_down, norm_final_w=norm_final_w)
    m = dict(norm_mix_w=m_norm_mix_w, w_in=m_w_in, conv_a_w=m_conv_a_w, conv_a_b=m_conv_a_b, dt_bias=m_dt_bias, a_log=m_a_log, d_a=m_d_a, norm_a_w=m_norm_a_w, w_proj_a=m_w_proj_a, s5_lam_re=m_s5_lam_re, s5_lam_im=m_s5_lam_im, s5_log_dt=m_s5_log_dt, s5_b_re=m_s5_b_re, s5_b_im=m_s5_b_im, s5_c_re=m_s5_c_re, s5_c_im=m_s5_c_im, s5_d=m_s5_d, w_s5_glu=m_w_s5_glu, w_out=m_w_out, norm_ffn_w=m_norm_ffn_w, w_up=m_w_up, conv_ffn_w=m_conv_ffn_w, conv_ffn_b=m_conv_ffn_b, w_down=m_w_down, norm_final_w=m_norm_final_w)
    v = dict(norm_mix_w=v_norm_mix_w, w_in=v_w_in, conv_a_w=v_conv_a_w, conv_a_b=v_conv_a_b, dt_bias=v_dt_bias, a_log=v_a_log, d_a=v_d_a, norm_a_w=v_norm_a_w, w_proj_a=v_w_proj_a, s5_lam_re=v_s5_lam_re, s5_lam_im=v_s5_lam_im, s5_log_dt=v_s5_log_dt, s5_b_re=v_s5_b_re, s5_b_im=v_s5_b_im, s5_c_re=v_s5_c_re, s5_c_im=v_s5_c_im, s5_d=v_s5_d, w_s5_glu=v_w_s5_glu, w_out=v_w_out, norm_ffn_w=v_norm_ffn_w, w_up=v_w_up, conv_ffn_w=v_conv_ffn_w, conv_ffn_b=v_conv_ffn_b, w_down=v_w_down, norm_final_w=v_norm_final_w)
    xi, yi, ci = _place()
    chip = 2 * xi + yi

    cidx = jnp.reshape(ci, (1,)).astype(jnp.int32)
    sidx = jnp.reshape(chip, (1,)).astype(jnp.int32)

    tw = lambda a: jnp.transpose(a[0])[None]
    w["w_in"], m["w_in"], v["w_in"] = tw(w_in), tw(m_w_in), tw(v_w_in)
    shards = [w[k][0].astype(BF16) for k in BIG]
    shards[0] = jnp.pad(shards[0], ((0, WPAD - WSH), (0, 0)))

    late = {}

    def late_start(after):
        srcs, got = _ici_wait("gather", g_sems, g_srcs, g_lands, after, "gather_rest_wait")
        late["sems"], late["got"], late["srcs"], tok = _ici_start("pass", list(got), cidx, "pass_rest_start",
                                                                  lands=list(srcs))
        return tok

    def late_weights(after):
        full, _ = _ici_wait("pass", late["sems"], late["got"], late["srcs"], after, "pass_rest_wait")
        return {"w_proj_a": full[0].reshape(DI, D), "w_s5_glu": full[1], "w_out": full[2].reshape(D, D),
                "w_up": full[3], "w_down": full[4].reshape(DFF, D)}

    swaps, pending = {}, []

    def swap_start(names, g, tag):
        sems, parts, lands, tok = _ici_start("swap", [g[k] for k in names], cidx, "swap_start_" + tag)
        swaps[tag] = (names, sems, parts, lands)
        return tok

    def scatter_go(tag, after):
        names, sems, parts, lands = swaps[tag]
        parts, sib = _ici_wait("swap", sems, parts, lands, after, "swap_wait_" + tag)
        sums = [_chip_sum(parts[t], sib[t], cidx, "chip_sum_" + k) for t, k in enumerate(names)]
        sems, srcs, lands, tok = _ici_start("scatter", sums, cidx, "scatter_start_" + tag)
        pending.append((names, tag, sems, srcs, lands))
        return tok

    hooks = {"late_start": late_start, "late_weights": late_weights, "swap_start": swap_start,
             "scatter_go": scatter_go}
    conv_blocks = []
    for k, (taps, cols) in CONV_FULL.items():
        shard = jnp.where(ci == 0, w[k][0], 0.0)
        conv_blocks.append(lax.dynamic_update_slice_in_dim(jnp.zeros((taps, cols), F32), shard, chip * (cols // 4), 1))
    conv_full = _unpack(_allsum_small(_pack(conv_blocks), "sum_conv_w", cidx), [CONV_FULL[k] for k in CONV_FULL])

    half = D // 2
    sh_a, sh_b = shards[0][:, :half], shards[0][:, half:]
    a_sems, a_srcs, a_lands, a_tok = _ici_start("gather", [sh_a], conv_full[0], "gather_in_a_start")
    b_sems, b_srcs, b_lands, b_tok = _ici_start("gather", [sh_b], a_tok, "gather_in_b_start")
    hn1 = _rms_fwd(x[0], norm_mix_w, "rms_mix", after=b_tok)

    def w_in_part(sems, srcs, lands, after, tag):
        srcs, got = _ici_wait("gather", sems, srcs, lands, after, "gather_in_%s_wait" % tag)
        w_sh = _pass_halves(list(got), list(srcs), "pass_halves_in_" + tag)[0].reshape(4 * WPAD, half)
        w_dt = jnp.pad(w_sh[DT_SHARD_ROW:DT_SHARD_ROW + NG * HPG].reshape(NG, HPG, half),
                       ((0, 0), (0, 8 - HPG), (0, 0)))
        return lax.dynamic_update_slice(_move_rows(w_sh, RUNS_TO_MAIN, NFULL, MT, MT, "rows_to_main_" + tag),
                                        w_dt.reshape(8 * NG, half), (NMAIN, 0))

    w_a = w_in_part(a_sems, a_srcs, a_lands, hn1, "a")
    proj_a = _matmul(hn1[:, :half], w_a, "nt", "mm_in_a", tn=896)
    w_b = w_in_part(b_sems, b_srcs, b_lands, proj_a, "b")
    g_sems, g_srcs, g_lands, token = _ici_start("gather", shards[1:], w_b, "gather_rest_start")
    proj = _matmul(hn1[:, half:], w_b, "nt", "mm_in_b", tn=896, residual=proj_a, after=token)
    w_full = jnp.concatenate([w_a, w_b], axis=1)
    p = {
        "w_full": w_full,
        "conv_a_w": conv_full[0], "conv_ffn_w": conv_full[1],
        "conv_a_b": conv_a_b, "conv_ffn_b": conv_ffn_b,
        "norm_mix_w": norm_mix_w, "norm_a_w": norm_a_w, "norm_ffn_w": norm_ffn_w,
        "norm_final_w": norm_final_w.reshape(1, D),
        "dt_bias": dt_bias, "a_log": a_log, "d_a": d_a, "s5_d": s5_d,
        "s5_lam_re": s5_lam_re[0], "s5_lam_im": s5_lam_im[0], "s5_log_dt": s5_log_dt[0],
        "s5_b_re": s5_b_re[0], "s5_b_im": s5_b_im[0], "s5_c_re": s5_c_re[0], "s5_c_im": s5_c_im[0],
    }
    loss_blk, gx, g = _local_step(x[0], loss_target[0], hn1, proj, p, hooks)

    after, halves = gx, {}
    for names, tag, sems, srcs, lands in pending:
        srcs, got = _ici_wait("scatter", sems, srcs, lands, after, "scatter_wait_" + tag)
        for t, k in enumerate(names):
            halves[k] = _shard_sum(srcs[t], got[t], sidx, "shard_sum_" + k)
        after = halves[names[0]]
    w_sems, g_mine, w_lands, w_tok = _ici_start("whole", [halves[k] for k in BIG], cidx, "whole_start")

    small_shapes = [CONV_FULL.get(k, w[k].shape[1:] if k != "norm_final_w" else w[k].shape) for k in SMALL]
    small = _allsum_small(_pack([g[k] for k in SMALL] + [loss_blk[0:1, 0:1]]), "sum_small_grads", w_tok)
    small_grads = dict(zip(SMALL + ["loss"], _unpack(small, small_shapes + [(1,)])))
    for k, (taps, cols) in CONV_FULL.items():
        small_grads[k] = lax.dynamic_slice_in_dim(small_grads[k], chip * (cols // 4), cols // 4, axis=1)
    loss = small_grads.pop("loss").reshape(())

    grads, delta, new_m, new_v = {}, {}, {}, {}
    for k in SMALL:
        grads[k] = small_grads[k].reshape(w[k].shape)
    pk = lambda t: _pack([t[k] for k in SMALL])
    d_, m_, v_ = _adamw(pk(w), pk(grads), pk(m), pk(v), "adamw_small")
    shapes = [w[k].shape for k in SMALL]
    for k, dd, mm, vv in zip(SMALL, _unpack(d_, shapes), _unpack(m_, shapes), _unpack(v_, shapes)):
        delta[k], new_m[k], new_v[k] = dd, mm, vv
    g_mine, g_other = _ici_wait("whole", w_sems, g_mine, w_lands, d_, "whole_wait")
    for t, k in enumerate(BIG):
        outs = _adamw_halves(w[k], g_mine[t], g_other[t], m[k], v[k], cidx, "adamw_" + k)
        grads[k], delta[k], new_m[k], new_v[k] = [tw(o) for o in outs] if k == "w_in" else outs
    return (loss, gx[None], *[grads[k] for k in ORDER], *[delta[k] for k in ORDER],
            *[new_m[k] for k in ORDER], *[new_v[k] for k in ORDER])
```

```python
import functools
import math

import jax
import jax.numpy as jnp
from jax import lax
from jax.experimental import pallas as pl
from jax.experimental.pallas import tpu as pltpu

F32 = jnp.float32
BF16 = jnp.bfloat16
HI = lax.Precision.HIGHEST
MESH = pl.DeviceIdType.MESH
ANY = pl.BlockSpec(memory_space=pl.ANY)

D = 2048
DI = 3072
HD = 64
NG = 8
HPG = 6
GW = HPG * HD
NS = 128
KA = 4
Q = 256
CONVD = DI + 2 * NG * NS
DS5 = 1024
NCH = 4096
DFF = 5632
KF = 3
EPS = 1e-6
EIG_MAX = -1e-4
NMAIN = 13312
OFF_XBC, OFF_U, OFF_GA, OFF_GB = 3072, 8192, 9216, 11264
WSH = 3340
WPAD = 3360
IN_SPLIT = [DI, DI + CONVD, DI + CONVD + NG * HPG]
NFULL = NMAIN + 128
MT = 336


def _w_in_runs():
    runs = []
    for k in range(4):
        for o_lo, o_hi, m_lo in ((0, IN_SPLIT[1], 0), (IN_SPLIT[2], 4 * WSH, IN_SPLIT[1])):
            lo, hi = max(o_lo, WSH * k), min(o_hi, WSH * (k + 1))
            if lo < hi:
                runs.append((m_lo + lo - o_lo, m_lo + hi - o_lo, WPAD * k + lo - WSH * k))
    return runs


RUNS_TO_MAIN = _w_in_runs()
RUNS_TO_SHARDS = [(s_lo, s_lo + m_hi - m_lo, m_lo) for m_lo, m_hi, s_lo in RUNS_TO_MAIN]
DT_SHARD_ROW = WPAD * (IN_SPLIT[1] // WSH) + IN_SPLIT[1] % WSH
assert IN_SPLIT[1] // WSH == (IN_SPLIT[2] - 1) // WSH
VMEM_LIMIT = 56 * 1024 * 1024

LR, B1, B2, AEPS, WD, STEP = 0.001, 0.9, 0.999, 1e-08, 0.01, 10


def _cp(*sem):
    return pltpu.CompilerParams(dimension_semantics=sem, vmem_limit_bytes=VMEM_LIMIT)


def _sig(x):
    return jax.nn.sigmoid(x)


def _silu(x):
    return x * _sig(x)


def _dsilu(x):
    s = _sig(x)
    return s * (1.0 + x * (1.0 - s))


def _softplus(x):
    return jnp.maximum(x, 0.0) + jnp.log(1.0 + jnp.exp(-jnp.abs(x)))


_GC = math.sqrt(2.0 / math.pi)


def _gelu(x):
    return 0.5 * x * (1.0 + jnp.tanh(_GC * (x + 0.044715 * x * x * x)))


def _dgelu(x):
    t = jnp.tanh(_GC * (x + 0.044715 * x * x * x))
    return 0.5 * (1.0 + t) + 0.5 * x * (1.0 - t * t) * _GC * (1.0 + 3.0 * 0.044715 * x * x)


def _dot(a, b, dims=((1,), (0,)), prec=None):
    return lax.dot_general(a, b, (dims, ((), ())), precision=prec, preferred_element_type=F32)


NT = ((1,), (1,))
TN = ((0,), (0,))


def _pick(n, t):
    for unit in (128, 8):
        for cand in range(min(n, t) // unit * unit, 0, -unit):
            if n % cand == 0:
                return cand
    return n


def _matmul(a, b, mode, name, out_dtype=F32, tm=1024, tn=1024, tk=2048, residual=None, b_stacked=False,
            out_stacked=False, after=None):
    if b_stacked:
        _, brows, bn = b.shape
        bshape = (brows, 4 * bn)
    else:
        bshape = b.shape
    if mode == "nn":
        (m, k), (k2, n) = a.shape, bshape
    elif mode == "nt":
        (m, k), (n, k2) = a.shape, bshape
    else:
        (k, m), (k2, n) = a.shape, bshape
    assert k == k2
    tm = _pick(m, tm)
    tn = _pick(n // 4 if (out_stacked or (b_stacked and mode != "nt")) else n, tn)
    tk = _pick(k // 4 if (b_stacked and mode == "nt") else k, tk)
    nk = k // tk
    dims = {"nn": ((1,), (0,)), "nt": NT, "tn": TN}[mode]
    has_res = residual is not None
    n_in = 2 + has_res + (after is not None)

    def body(*refs):
        a_ref, b_ref = refs[0], refs[1]
        r_ref = refs[2] if has_res else None
        o_ref = refs[n_in]
        p = _dot(a_ref[...], b_ref[...], dims)

        def finish(r):
            if has_res:
                r = r + r_ref[...]
            o_ref[...] = r.astype(out_dtype)

        if nk == 1:
            finish(p)
        else:
            acc = refs[-1]
            kk = pl.program_id(2)

            @pl.when(kk == 0)
            def _():
                acc[...] = p

            @pl.when(kk > 0)
            def _():
                acc[...] += p

            @pl.when(kk == nk - 1)
            def _():
                finish(acc[...])

    if mode == "tn":
        a_spec = pl.BlockSpec((tk, tm), lambda i, j, kk: (kk, i))
    else:
        a_spec = pl.BlockSpec((tm, tk), lambda i, j, kk: (i, kk))
    if mode == "nt":
        if b_stacked:
            per = bn // tk
            b_spec = pl.BlockSpec((None, tn, tk), lambda i, j, kk: (kk // per, j, kk % per))
        else:
            b_spec = pl.BlockSpec((tn, tk), lambda i, j, kk: (j, kk))
    elif b_stacked:
        per = bn // tn
        b_spec = pl.BlockSpec((None, tk, tn), lambda i, j, kk: (j // per, kk, j % per))
    else:
        b_spec = pl.BlockSpec((tk, tn), lambda i, j, kk: (kk, j))
    o_spec = pl.BlockSpec((tm, tn), lambda i, j, kk: (i, j))
    if out_stacked:
        per_o = n // 4 // tn
        out_spec = pl.BlockSpec((None, tm, tn), lambda i, j, kk: (j // per_o, i, j % per_o))
        out_shape = jax.ShapeDtypeStruct((4, m, n // 4), out_dtype)
    else:
        out_spec, out_shape = o_spec, jax.ShapeDtypeStruct((m, n), out_dtype)
    in_specs, args = [a_spec, b_spec], [a, b]
    if has_res:
        in_specs.append(o_spec)
        args.append(residual)
    if after is not None:
        in_specs.append(ANY)
        args.append(after)
    return pl.pallas_call(
        body, name=name, grid=(m // tm, n // tn, nk),
        in_specs=in_specs, out_specs=out_spec, out_shape=out_shape,
        scratch_shapes=[pltpu.VMEM((tm, tn), F32)] if nk > 1 else [],
        compiler_params=_cp("parallel", "parallel", "arbitrary"),
    )(*args)


def _move_rows(src, runs, rows_out, t_out, t_in, name):
    rows_in, cols = src.shape
    nb_out, nb_in = rows_out // t_out, rows_in // t_in
    assert rows_out % t_out == 0 and rows_in % t_in == 0 and t_in >= t_out
    blk, off, lo, hi = ([[0] * nb_out for _ in range(2)] for _ in range(4))
    for i in range(nb_out):
        hits = [r for r in runs if r[0] < (i + 1) * t_out and r[1] > i * t_out]
        assert len(hits) <= 2
        for s, (o_lo, o_hi, s_lo) in enumerate(hits):
            lo[s][i] = max(o_lo, i * t_out) - i * t_out
            hi[s][i] = min(o_hi, (i + 1) * t_out) - i * t_out
            first = i * t_out + lo[s][i] - o_lo + s_lo
            blk[s][i] = min(first // t_in, nb_in - 1)
            off[s][i] = first - lo[s][i] - blk[s][i] * t_in
    table = jnp.asarray([blk[0], off[0], lo[0], hi[0], blk[1], off[1], lo[1], hi[1]], jnp.int32)

    def body(tab, a0, a1, b0, b1, o_ref):
        i = pl.program_id(0)
        o_ref[...] = jnp.zeros_like(o_ref)
        r = lax.broadcasted_iota(jnp.int32, (t_out, t_in), 0)
        k = lax.broadcasted_iota(jnp.int32, (t_out, t_in), 1)
        for s, (first, second) in enumerate(((a0, a1), (b0, b1))):
            off_s, lo_s, hi_s = tab[4 * s + 1, i], tab[4 * s + 2, i], tab[4 * s + 3, i]
            live = (r >= lo_s) & (r < hi_s)

            @pl.when(hi_s > lo_s)
            def _():
                sel = (live & (k == r + off_s)).astype(BF16)
                o_ref[...] += _dot(sel, first[...]).astype(o_ref.dtype)

            @pl.when((hi_s > lo_s) & (off_s + hi_s > t_in))
            def _():
                sel = (live & (k == r + off_s - t_in)).astype(BF16)
                o_ref[...] += _dot(sel, second[...]).astype(o_ref.dtype)

    def in_spec(s, nxt):
        return pl.BlockSpec((t_in, cols), lambda i, tab: (jnp.minimum(tab[4 * s, i] + nxt, nb_in - 1), 0))

    return pl.pallas_call(
        body, name=name,
        grid_spec=pltpu.PrefetchScalarGridSpec(
            num_scalar_prefetch=1, grid=(nb_out,),
            in_specs=[in_spec(0, 0), in_spec(0, 1), in_spec(1, 0), in_spec(1, 1)],
            out_specs=pl.BlockSpec((t_out, cols), lambda i, tab: (i, 0))),
        out_shape=jax.ShapeDtypeStruct((rows_out, cols), src.dtype), compiler_params=_cp("parallel"),
    )(table, src, src, src, src)


TL = 256


def _rms_fwd(x, w, name, after=None):
    n, d = x.shape

    def body(x_ref, w_ref, *rest):
        xv = x_ref[...]
        r = lax.rsqrt(jnp.mean(xv * xv, axis=-1, keepdims=True) + EPS)
        rest[-1][...] = (xv * r * w_ref[...]).astype(BF16)

    extra = [] if after is None else [after]
    return pl.pallas_call(
        body, name=name, grid=(n // TL,),
        in_specs=[pl.BlockSpec((TL, d), lambda i: (i, 0)), pl.BlockSpec((1, d), lambda i: (0, 0))] + [ANY] * len(extra),
        out_specs=pl.BlockSpec((TL, d), lambda i: (i, 0)),
        out_shape=jax.ShapeDtypeStruct((n, d), BF16), compiler_params=_cp("parallel"),
    )(x, w, *extra)


def _rms_bwd(dhn, x, w, dres, name, after=None):
    n, d = x.shape

    def body(g_ref, x_ref, w_ref, r_ref, *rest):
        dx_ref, dxb_ref, gw_ref = rest[-3:]
        xv = x_ref[...]
        r = lax.rsqrt(jnp.mean(xv * xv, axis=-1, keepdims=True) + EPS)
        xh = xv * r
        gv = g_ref[...]
        g = gv * w_ref[...]
        dx = r_ref[...] + r * (g - xh * jnp.mean(g * xh, axis=-1, keepdims=True))
        dx_ref[...] = dx
        dxb_ref[...] = dx.astype(BF16)

        @pl.when(pl.program_id(0) == 0)
        def _():
            gw_ref[...] = jnp.zeros_like(gw_ref)

        gw_ref[...] += jnp.sum(gv * xh, axis=0, keepdims=True)

    extra = [] if after is None else [after]
    row = pl.BlockSpec((TL, d), lambda i: (i, 0))
    vec = pl.BlockSpec((1, d), lambda i: (0, 0))
    return pl.pallas_call(
        body, name=name, grid=(n // TL,),
        in_specs=[row, row, vec, row] + [ANY] * len(extra), out_specs=[row, row, vec],
        out_shape=[jax.ShapeDtypeStruct((n, d), F32), jax.ShapeDtypeStruct((n, d), BF16),
                   jax.ShapeDtypeStruct((1, d), F32)],
        compiler_params=_cp("arbitrary"),
    )(dhn, x, w, dres, *extra)


def _final(h2, w, target):
    n, d = h2.shape

    def body(x_ref, w_ref, t_ref, dx_ref, dxb_ref, gw_ref, loss_ref):
        xv = x_ref[...]
        r = lax.rsqrt(jnp.mean(xv * xv, axis=-1, keepdims=True) + EPS)
        xh = xv * r
        diff = xh * w_ref[...] - t_ref[...]
        gv = diff * (1.0 / d)
        g = gv * w_ref[...]
        dx = r * (g - xh * jnp.mean(g * xh, axis=-1, keepdims=True))
        dx_ref[...] = dx
        dxb_ref[...] = dx.astype(BF16)

        @pl.when(pl.program_id(0) == 0)
        def _():
            gw_ref[...] = jnp.zeros_like(gw_ref)
            loss_ref[...] = jnp.zeros_like(loss_ref)

        gw_ref[...] += jnp.sum(gv * xh, axis=0, keepdims=True)
        part = 0.5 * jnp.sum(jnp.mean(diff * diff, axis=-1, keepdims=True), axis=0, keepdims=True)
        loss_ref[...] += jnp.broadcast_to(part, loss_ref.shape)

    row = pl.BlockSpec((TL, d), lambda i: (i, 0))
    vec = pl.BlockSpec((1, d), lambda i: (0, 0))
    return pl.pallas_call(
        body, name="final_loss", grid=(n // TL,),
        in_specs=[row, vec, row], out_specs=[row, row, vec, pl.BlockSpec((8, 128), lambda i: (0, 0))],
        out_shape=[jax.ShapeDtypeStruct((n, d), F32), jax.ShapeDtypeStruct((n, d), BF16),
                   jax.ShapeDtypeStruct((1, d), F32), jax.ShapeDtypeStruct((8, 128), F32)],
        compiler_params=_cp("arbitrary"),
    )(h2, w, target)


CT = 512
CL = 512


def _lagged(xf, taps, rows):
    return [xf[8:8 + rows]] + [pltpu.roll(xf, s, 0)[8:8 + rows] for s in range(1, taps)]


def _shift_up(x, u, n):
    if u == 0:
        return x[0:n]
    return pltpu.roll(x, x.shape[0] - u, 0)[0:n]


def _conv_pre(lagged, w_ref, b_ref, taps):
    pre = b_ref[...]
    for k in range(taps):
        pre = pre + w_ref[k:k + 1, :] * lagged[taps - 1 - k]
    return pre


def _conv_back(e, w_ref, taps):
    dx = w_ref[taps - 1:taps, :] * e[0:CL]
    for k in range(taps - 1):
        dx = dx + w_ref[k:k + 1, :] * _shift_up(e, taps - 1 - k, CL)
    return dx


def _halo_specs(n, col_of):
    per = CL // 8
    cur = pl.BlockSpec((CL, CT), lambda j, i, *_: (i, col_of(j)))
    prev = pl.BlockSpec((8, CT), lambda j, i, *_: (jnp.maximum(i * per - 1, 0), col_of(j)))
    nxt = pl.BlockSpec((8, CT), lambda j, i, *_: (jnp.minimum((i + 1) * per, n // 8 - 1), col_of(j)))
    return prev, cur, nxt


def _conv_a_fwd(proj, w, b):
    n = proj.shape[0]
    off = OFF_XBC // CT

    def body(p_ref, x_ref, w_ref, b_ref, o_ref):
        p8 = jnp.where(pl.program_id(1) > 0, p_ref[...], 0.0)
        xf = jnp.concatenate([p8, x_ref[...]], axis=0)
        o_ref[...] = _silu(_conv_pre(_lagged(xf, KA, CL), w_ref, b_ref, KA))

    prev, cur, _ = _halo_specs(n, lambda j: j + off)
    return pl.pallas_call(
        body, name="conv_a_fwd", grid=(CONVD // CT, n // CL),
        in_specs=[prev, cur, pl.BlockSpec((KA, CT), lambda j, i: (0, j)), pl.BlockSpec((1, CT), lambda j, i: (0, j))],
        out_specs=pl.BlockSpec((CL, CT), lambda j, i: (i, j)),
        out_shape=jax.ShapeDtypeStruct((n, CONVD), F32), compiler_params=_cp("parallel", "parallel"),
    )(proj, proj, w, b)


def _conv_a_bwd(proj, dout, w, b, col0, name):
    n, width = dout.shape
    off = (OFF_XBC + col0) // CT
    woff = col0 // CT
    nl = n // CL

    def body(p_ref, x_ref, n_ref, d_ref, dn_ref, w_ref, b_ref, dx_ref, dw_ref, db_ref):
        i = pl.program_id(1)
        xf = jnp.concatenate([jnp.where(i > 0, p_ref[...], 0.0), x_ref[...], n_ref[...]], axis=0)
        lag = _lagged(xf, KA, CL + 8)
        de = jnp.concatenate([d_ref[...], jnp.where(i < nl - 1, dn_ref[...], 0.0)], axis=0)
        se = de * _dsilu(_conv_pre(lag, w_ref, b_ref, KA))
        dx_ref[...] = _conv_back(se, w_ref, KA).astype(BF16)

        @pl.when(i == 0)
        def _():
            dw_ref[...] = jnp.zeros_like(dw_ref)
            db_ref[...] = jnp.zeros_like(db_ref)

        sc = se[0:CL]
        for k in range(KA):
            dw_ref[k:k + 1, :] += jnp.sum(sc * lag[KA - 1 - k][0:CL], axis=0, keepdims=True)
        db_ref[...] += jnp.sum(sc, axis=0, keepdims=True)

    prev, cur, nxt = _halo_specs(n, lambda j: j + off)
    _, dcur, dnxt = _halo_specs(n, lambda j: j)
    wspec = pl.BlockSpec((KA, CT), lambda j, i: (0, j + woff))
    bspec = pl.BlockSpec((1, CT), lambda j, i: (0, j + woff))
    return pl.pallas_call(
        body, name=name, grid=(width // CT, nl),
        in_specs=[prev, cur, nxt, dcur, dnxt, wspec, bspec],
        out_specs=[pl.BlockSpec((CL, CT), lambda j, i: (i, j)), pl.BlockSpec((KA, CT), lambda j, i: (0, j)),
                   pl.BlockSpec((1, CT), lambda j, i: (0, j))],
        out_shape=[jax.ShapeDtypeStruct((n, width), BF16), jax.ShapeDtypeStruct((KA, width), F32),
                   jax.ShapeDtypeStruct((1, width), F32)],
        compiler_params=_cp("parallel", "arbitrary"),
    )(proj, proj, proj, dout, dout, w, b)


def _conv_ffn_fwd(up, w, b):
    n = up.shape[0]
    nb = DFF // CT

    def body(pg_ref, g_ref, pv_ref, v_ref, wg_ref, bg_ref, wv_ref, bv_ref, o_ref):
        inner = pl.program_id(1) > 0
        gf = jnp.concatenate([jnp.where(inner, pg_ref[...], 0.0), g_ref[...]], axis=0)
        vf = jnp.concatenate([jnp.where(inner, pv_ref[...], 0.0), v_ref[...]], axis=0)
        gc = _conv_pre(_lagged(gf, KF, CL), wg_ref, bg_ref, KF)
        vc = _conv_pre(_lagged(vf, KF, CL), wv_ref, bv_ref, KF)
        o_ref[...] = (_silu(gc) * vc).astype(BF16)

    gp, gcur, _ = _halo_specs(n, lambda j: j)
    vp, vcur, _ = _halo_specs(n, lambda j: j + nb)
    return pl.pallas_call(
        body, name="conv_ffn_fwd", grid=(nb, n // CL),
        in_specs=[gp, gcur, vp, vcur,
                  pl.BlockSpec((KF, CT), lambda j, i: (0, j)), pl.BlockSpec((1, CT), lambda j, i: (0, j)),
                  pl.BlockSpec((KF, CT), lambda j, i: (0, j + nb)), pl.BlockSpec((1, CT), lambda j, i: (0, j + nb))],
        out_specs=pl.BlockSpec((CL, CT), lambda j, i: (i, j)),
        out_shape=jax.ShapeDtypeStruct((n, DFF), BF16), compiler_params=_cp("parallel", "parallel"),
    )(up, up, up, up, w, b, w, b)


def _conv_ffn_bwd(up, dact, w, b):
    n = up.shape[0]
    nb = DFF // CT
    nl = n // CL

    def body(pg_ref, g_ref, ng_ref, pv_ref, v_ref, nv_ref, d_ref, dn_ref, wg_ref, bg_ref, wv_ref, bv_ref,
             dxg_ref, dxv_ref, dwg_ref, dwv_ref, dbg_ref, dbv_ref):
        i = pl.program_id(1)
        gf = jnp.concatenate([jnp.where(i > 0, pg_ref[...], 0.0), g_ref[...], ng_ref[...]], axis=0)
        vf = jnp.concatenate([jnp.where(i > 0, pv_ref[...], 0.0), v_ref[...], nv_ref[...]], axis=0)
        glag, vlag = _lagged(gf, KF, CL + 8), _lagged(vf, KF, CL + 8)
        de = jnp.concatenate([d_ref[...], jnp.where(i < nl - 1, dn_ref[...], 0.0)], axis=0).astype(F32)
        gc = _conv_pre(glag, wg_ref, bg_ref, KF)
        vc = _conv_pre(vlag, wv_ref, bv_ref, KF)
        sg = _sig(gc)
        dgc = de * vc * (sg * (1.0 + gc * (1.0 - sg)))
        dvc = de * (gc * sg)
        dxg_ref[...] = _conv_back(dgc, wg_ref, KF).astype(BF16)
        dxv_ref[...] = _conv_back(dvc, wv_ref, KF).astype(BF16)

        @pl.when(i == 0)
        def _():
            for r in (dwg_ref, dwv_ref, dbg_ref, dbv_ref):
                r[...] = jnp.zeros_like(r)

        for e, lag, dw_ref, db_ref in ((dgc, glag, dwg_ref, dbg_ref), (dvc, vlag, dwv_ref, dbv_ref)):
            ec = e[0:CL]
            for k in range(KF):
                dw_ref[k:k + 1, :] += jnp.sum(ec * lag[KF - 1 - k][0:CL], axis=0, keepdims=True)
            db_ref[...] += jnp.sum(ec, axis=0, keepdims=True)

    gp, gcur, gnx = _halo_specs(n, lambda j: j)
    vp, vcur, vnx = _halo_specs(n, lambda j: j + nb)
    wcol = lambda o: (pl.BlockSpec((KF, CT), lambda j, i: (0, j + o)), pl.BlockSpec((1, CT), lambda j, i: (0, j + o)))
    wg, bg = wcol(0)
    wv, bv = wcol(nb)
    dxs = pl.BlockSpec((CL, CT), lambda j, i: (i, j))
    outs = pl.pallas_call(
        body, name="conv_ffn_bwd", grid=(nb, nl),
        in_specs=[gp, gcur, gnx, vp, vcur, vnx, gcur, gnx, wg, bg, wv, bv],
        out_specs=[dxs, dxs, wg, wg, bg, bg],
        out_shape=[jax.ShapeDtypeStruct((n, DFF), BF16)] * 2 + [jax.ShapeDtypeStruct((KF, DFF), F32)] * 2
        + [jax.ShapeDtypeStruct((1, DFF), F32)] * 2,
        compiler_params=_cp("parallel", "arbitrary"),
    )(up, up, up, up, up, up, dact, dact, w, b, w, b)
    return [jnp.concatenate(outs[k:k + 2], axis=1) for k in (0, 2, 4)]


HL = 128


def _split3(x):
    hi = x.astype(BF16)
    r1 = x - hi.astype(F32)
    mid = r1.astype(BF16)
    return hi, mid, (r1 - mid.astype(F32)).astype(BF16)


def _dot3(x, m):
    hi, mid, lo = _split3(x)
    return _dot(hi, m) + _dot(mid, m) + _dot(lo, m)


def _tri():
    row = lax.broadcasted_iota(jnp.int32, (Q, Q), 0)
    col = lax.broadcasted_iota(jnp.int32, (Q, Q), 1)
    return row >= col, row <= col


def _ssd_prep(dtraw, hp, emat):
    n = dtraw.shape[0]

    def body(d_ref, hp_ref, e_ref, dt_ref, s_ref, st_ref, dte_ref, se_ref):
        lower, upper = _tri()
        dt = _softplus(d_ref[...] + hp_ref[0:1, :])
        da = dt * (-jnp.exp(hp_ref[1:2, :]))
        s = _dot(lower.astype(F32), da, prec=HI)
        dt_ref[...] = dt
        s_ref[...] = s
        st_ref[...] = _dot(da, upper.astype(F32), TN, prec=HI)
        e = e_ref[...]
        dte_ref[...] = _dot3(dt, e)
        se_ref[...] = _dot3(s, e)

    row = pl.BlockSpec((Q, HL), lambda c: (c, 0))
    wide = pl.BlockSpec((Q, DI), lambda c: (c, 0))
    return pl.pallas_call(
        body, name="ssd_prep", grid=(n // Q,),
        in_specs=[pl.BlockSpec((Q, HL), lambda c: (c, NMAIN // HL)), pl.BlockSpec((8, HL), lambda c: (0, 0)),
                  pl.BlockSpec((HL, DI), lambda c: (0, 0))],
        out_specs=[row, row, pl.BlockSpec((HL, Q), lambda c: (0, c)), wide, wide],
        out_shape=[jax.ShapeDtypeStruct((n, HL), F32)] * 2 + [jax.ShapeDtypeStruct((HL, n), F32)]
        + [jax.ShapeDtypeStruct((n, DI), F32)] * 2,
        compiler_params=_cp("parallel"),
    )(dtraw, hp, emat)


def _ssd_post(ds_e, ddt_e, tsum, dsh, dtraw, dt, hp, emat_t):
    n = dtraw.shape[0]

    def body(dse_ref, dde_ref, ts_ref, dsh_ref, d_ref, dt_ref, hp_ref, et_ref, draw_ref, ps_ref):
        _, upper = _tri()
        et = et_ref[...]
        a = -jnp.exp(hp_ref[1:2, :])
        rows = lax.broadcasted_iota(jnp.int32, (Q, HL), 0)
        ds_t = _dot3(jnp.broadcast_to(ts_ref[...], (8, DI)), et)[0:1, :]
        ds = _dot3(dse_ref[...], et) + dsh_ref[...] + jnp.where(rows == Q - 1, ds_t, 0.0)
        d_a = _dot(upper.astype(F32), ds, prec=HI)
        draw = (_dot3(dde_ref[...], et) + d_a * a) * _sig(d_ref[...] + hp_ref[0:1, :])
        draw_ref[...] = draw

        @pl.when(pl.program_id(0) == 0)
        def _():
            ps_ref[...] = jnp.zeros_like(ps_ref)

        ps_ref[0:1, :] += jnp.sum(draw, axis=0, keepdims=True)
        ps_ref[1:2, :] += jnp.sum(d_a * dt_ref[...], axis=0, keepdims=True) * a

    row = pl.BlockSpec((Q, HL), lambda c: (c, 0))
    wide = pl.BlockSpec((Q, DI), lambda c: (c, 0))
    small = pl.BlockSpec((8, HL), lambda c: (0, 0))
    return pl.pallas_call(
        body, name="ssd_post", grid=(n // Q,),
        in_specs=[wide, wide, pl.BlockSpec((None, 1, DI), lambda c: (c, 0, 0)), row,
                  pl.BlockSpec((Q, HL), lambda c: (c, NMAIN // HL)), row, small,
                  pl.BlockSpec((DI, HL), lambda c: (0, 0))],
        out_specs=[row, small],
        out_shape=[jax.ShapeDtypeStruct((n, HL), F32), jax.ShapeDtypeStruct((8, HL), F32)],
        compiler_params=_cp("arbitrary"),
    )(ds_e, ddt_e, tsum, dsh, dtraw, dt, hp, emat_t)


def _ssd_specs(nc, rev):
    cc = (lambda c: nc - 1 - c) if rev else (lambda c: c)
    return [
        pl.BlockSpec((Q, GW), lambda g, c: (cc(c), g)),
        pl.BlockSpec((Q, NS), lambda g, c: (cc(c), DI // NS + g)),
        pl.BlockSpec((Q, NS), lambda g, c: (cc(c), (DI + NG * NS) // NS + g)),
        pl.BlockSpec((None, Q, 8), lambda g, c: (g, cc(c), 0)),
        pl.BlockSpec((8, Q), lambda g, c: (g, cc(c))),
        pl.BlockSpec((Q, GW), lambda g, c: (cc(c), g)),
        pl.BlockSpec((Q, GW), lambda g, c: (cc(c), g)),
        pl.BlockSpec((1, GW), lambda g, c: (0, g)),
    ]


def _ssd_fwd(xbc, s8, s_t, dt_e, s_e, dexp):
    n = xbc.shape[0]
    nc = n // Q

    def body(xs_ref, b_ref, c_ref, sc_ref, sr_ref, dte_ref, se_ref, dexp_ref, y_ref, sp_ref, st):
        @pl.when(pl.program_id(1) == 0)
        def _():
            st[...] = jnp.zeros_like(st)

        lower, _ = _tri()
        s_c, s_r, dt_e, s_e = sc_ref[...], sr_ref[...], dte_ref[...], se_ref[...]
        xs = xs_ref[...]
        x = xs * dt_e
        xb = x.astype(BF16)
        bb, cb = b_ref[...].astype(BF16), c_ref[...].astype(BF16)
        cbm = _dot(cb, bb, NT)
        st_e = s_e[Q - 1:Q, :]
        sprev = st[...]
        sp_ref[...] = sprev
        yoff = _dot(cb, sprev.astype(BF16)) * jnp.exp(s_e) + dexp_ref[...] * xs
        for h in range(HPG):
            sl = slice(h * HD, (h + 1) * HD)
            lm = jnp.where(lower, jnp.exp(jnp.minimum(s_c[:, h:h + 1] - s_r[h:h + 1, :], 0.0)), 0.0)
            y_ref[:, sl] = _dot((cbm * lm).astype(BF16), xb[:, sl]) + yoff[:, sl]
        w = (x * jnp.exp(st_e - s_e)).astype(BF16)
        st[...] = jnp.exp(st_e) * sprev + _dot(bb, w, TN)

    return pl.pallas_call(
        body, name="ssd_fwd", grid=(NG, nc), in_specs=_ssd_specs(nc, False),
        out_specs=[pl.BlockSpec((Q, GW), lambda g, c: (c, g)),
                   pl.BlockSpec((None, None, NS, GW), lambda g, c: (c, g, 0, 0))],
        out_shape=[jax.ShapeDtypeStruct((n, DI), F32), jax.ShapeDtypeStruct((nc, NG, NS, GW), F32)],
        scratch_shapes=[pltpu.VMEM((NS, GW), F32)],
        compiler_params=_cp("parallel", "arbitrary"),
    )(xbc, xbc, xbc, s8, s_t, dt_e, s_e, dexp)


def _ssd_bwd(xbc, s8, s_t, dt_e, s_e, dexp, sprev_all, dy):
    n = xbc.shape[0]
    nc = n // Q
    rc = lambda c: nc - 1 - c

    def body(xs_ref, b_ref, c_ref, sc_ref, sr_ref, dte_ref, se_ref, dexp_ref, sp_ref, dy_ref,
             dxs_ref, db_ref, dc_ref, dse_ref, dde_ref, ts_ref, dsh_ref, pd_ref, dst, dxbuf):
        @pl.when(pl.program_id(1) == 0)
        def _():
            dst[...] = jnp.zeros_like(dst)
            pd_ref[...] = jnp.zeros_like(pd_ref)

        lower, upper = _tri()
        s_c, s_r, dt_e, s_e = sc_ref[...], sr_ref[...], dte_ref[...], se_ref[...]
        xs = xs_ref[...]
        x = xs * dt_e
        xb = x.astype(BF16)
        bb, cb = b_ref[...].astype(BF16), c_ref[...].astype(BF16)
        cbm = _dot(cb, bb, NT)
        cbt = _dot(bb, cb, NT)
        st_e = s_e[Q - 1:Q, :]
        dec_out, dec_st, e_t = jnp.exp(s_e), jnp.exp(st_e - s_e), jnp.exp(st_e)
        dyv = dy_ref[...]
        dyb = dyv.astype(BF16)
        sprev = sp_ref[...]
        sb = sprev.astype(BF16)
        ds_in = dst[...]
        dsb = ds_in.astype(BF16)

        cs = _dot(cb, sb)
        dcs = (dyv * dec_out).astype(BF16)
        d_c = _dot(dcs, sb, NT)
        wf = x * dec_st
        d_w = _dot(bb, dsb)
        d_b = _dot(wf.astype(BF16), dsb, NT)
        tw = d_w * wf
        dse_ref[...] = dyv * cs * dec_out - tw
        ds_c = jnp.zeros((Q, 8), F32)
        dcb = jnp.zeros((Q, Q), F32)
        dcbt = jnp.zeros((Q, Q), F32)
        lane8 = lax.broadcasted_iota(jnp.int32, (1, 8), 1)
        for h in range(HPG):
            sl = slice(h * HD, (h + 1) * HD)
            sc_h, sr_h = s_c[:, h:h + 1], s_r[h:h + 1, :]
            lm = jnp.where(lower, jnp.exp(jnp.minimum(sc_h - sr_h, 0.0)), 0.0)
            lmt = jnp.where(upper, jnp.exp(jnp.minimum(sr_h - sc_h, 0.0)), 0.0)
            mt = cbt * lmt
            dm = _dot(dyb[:, sl], xb[:, sl], NT)
            dmt = _dot(xb[:, sl], dyb[:, sl], NT)
            dxbuf[:, sl] = _dot(mt.astype(BF16), dyb[:, sl])
            dml = dm * lm
            dmlt = dmt * lmt
            dcb = dcb + dml
            dcbt = dcbt + dmlt
            dsh = jnp.sum(dml * cbm, axis=1, keepdims=True) - jnp.sum(dmlt * cbt, axis=1, keepdims=True)
            ds_c = ds_c + dsh * (lane8 == h).astype(F32)
        d_c = d_c + _dot(dcb.astype(BF16), bb)
        d_b = d_b + _dot(dcbt.astype(BF16), cb)
        dx = d_w * dec_st + dxbuf[...]
        ts_ref[...] = jnp.sum(tw, axis=0, keepdims=True) + jnp.sum(ds_in * sprev, axis=0, keepdims=True) * e_t
        dsh_ref[...] = ds_c
        dde_ref[...] = dx * xs
        pd_ref[...] += jnp.sum(dyv * xs, axis=0, keepdims=True)
        dxs_ref[...] = dx * dt_e + dyv * dexp_ref[...]
        db_ref[...] = d_b
        dc_ref[...] = d_c
        dst[...] = e_t * ds_in + _dot(cb, dcs, TN)

    wide = pl.BlockSpec((Q, GW), lambda g, c: (rc(c), g))
    state = pl.BlockSpec((Q, NS), lambda g, c: (rc(c), g))
    in_specs = _ssd_specs(nc, True) + [pl.BlockSpec((None, None, NS, GW), lambda g, c: (rc(c), g, 0, 0)), wide]
    return pl.pallas_call(
        body, name="ssd_bwd", grid=(NG, nc), in_specs=in_specs,
        out_specs=[wide, state, state, wide, wide,
                   pl.BlockSpec((None, 1, GW), lambda g, c: (rc(c), 0, g)),
                   pl.BlockSpec((None, Q, 8), lambda g, c: (g, rc(c), 0)),
                   pl.BlockSpec((None, 1, GW), lambda g, c: (g, 0, 0))],
        out_shape=[jax.ShapeDtypeStruct((n, DI), F32), jax.ShapeDtypeStruct((n, NG * NS), F32),
                   jax.ShapeDtypeStruct((n, NG * NS), F32), jax.ShapeDtypeStruct((n, DI), F32),
                   jax.ShapeDtypeStruct((n, DI), F32), jax.ShapeDtypeStruct((nc, 1, DI), F32),
                   jax.ShapeDtypeStruct((NG, n, 8), F32), jax.ShapeDtypeStruct((NG, 1, GW), F32)],
        scratch_shapes=[pltpu.VMEM((NS, GW), F32), pltpu.VMEM((Q, GW), F32)],
        compiler_params=_cp("parallel", "arbitrary"),
    )(xbc, xbc, xbc, s8, s_t, dt_e, s_e, dexp, sprev_all, dy)


GL = 128


def _gnorm_fwd(y, proj, w):
    n = y.shape[0]

    def body(y_ref, z_ref, w_ref, o_ref):
        for g in range(NG):
            sl = slice(g * GW, (g + 1) * GW)
            yz = y_ref[:, sl] * _silu(z_ref[:, sl])
            r = lax.rsqrt(jnp.mean(yz * yz, axis=-1, keepdims=True) + EPS)
            o_ref[:, sl] = (yz * r * w_ref[:, sl]).astype(BF16)

    row = pl.BlockSpec((GL, DI), lambda i: (i, 0))
    return pl.pallas_call(
        body, name="gnorm_fwd", grid=(n // GL,),
        in_specs=[row, row, pl.BlockSpec((1, DI), lambda i: (0, 0))], out_specs=row,
        out_shape=jax.ShapeDtypeStruct((n, DI), BF16), compiler_params=_cp("parallel"),
    )(y, proj, w)


def _gnorm_bwd(dyn, y, proj, w, after):
    n = y.shape[0]

    def body(d_ref, y_ref, z_ref, w_ref, _, dy_ref, dz_ref, gw_ref):
        @pl.when(pl.program_id(0) == 0)
        def _():
            gw_ref[...] = jnp.zeros_like(gw_ref)

        for g in range(NG):
            sl = slice(g * GW, (g + 1) * GW)
            yv, zv, dv = y_ref[:, sl], z_ref[:, sl], d_ref[:, sl]
            sz = _silu(zv)
            yz = yv * sz
            r = lax.rsqrt(jnp.mean(yz * yz, axis=-1, keepdims=True) + EPS)
            yh = yz * r
            gg = dv * w_ref[:, sl]
            dyz = r * (gg - yh * jnp.mean(gg * yh, axis=-1, keepdims=True))
            gw_ref[:, sl] += jnp.sum(dv * yh, axis=0, keepdims=True)
            dy_ref[:, sl] = dyz * sz
            dz_ref[:, sl] = (dyz * yv * _dsilu(zv)).astype(BF16)

    row = pl.BlockSpec((GL, DI), lambda i: (i, 0))
    vec = pl.BlockSpec((1, DI), lambda i: (0, 0))
    return pl.pallas_call(
        body, name="gnorm_bwd", grid=(n // GL,),
        in_specs=[row, row, row, vec, ANY], out_specs=[row, row, vec],
        out_shape=[jax.ShapeDtypeStruct((n, DI), F32), jax.ShapeDtypeStruct((n, DI), BF16),
                   jax.ShapeDtypeStruct((1, DI), F32)],
        compiler_params=_cp("arbitrary"),
    )(dyn, y, proj, w, after)


SL = 512
SB = 8
SCB = NCH // SB


def _s5_in(proj, bre, bim, after=None):
    n = proj.shape[0]
    uoff = OFF_U // 128

    def body(u_ref, br_ref, bi_ref, *rest):
        or_ref, oi_ref = rest[-2:]
        u = u_ref[...].astype(BF16)
        or_ref[...] = _dot(u, br_ref[...])
        oi_ref[...] = _dot(u, bi_ref[...])

    extra = [] if after is None else [after]
    blk = pl.BlockSpec((None, 128, SCB), lambda i, j: (j, 0, 0))
    out = pl.BlockSpec((SL, SCB), lambda i, j: (i, j))
    return pl.pallas_call(
        body, name="s5_in", grid=(n // SL, SB),
        in_specs=[pl.BlockSpec((SL, 128), lambda i, j: (i, uoff + j)), blk, blk] + [ANY] * len(extra),
        out_specs=[out, out],
        out_shape=[jax.ShapeDtypeStruct((n, NCH), F32)] * 2, compiler_params=_cp("parallel", "parallel"),
    )(proj, bre, bim, *extra)


SC = 1024


def _s5_scan(vre, vim, tab, reverse, name):
    n = vre.shape[0]
    nl = n // SL
    ng = SL // 8
    ti = (lambda i: nl - 1 - i) if reverse else (lambda i: i)

    def body(re_ref, im_ref, tab_ref, ore_ref, oim_ref, cre, cim):
        @pl.when(pl.program_id(1) == 0)
        def _():
            cre[...] = jnp.zeros_like(cre)
            cim[...] = jnp.zeros_like(cim)

        def step(j, carry):
            cr, ci = carry
            jj = (ng - 1 - j) if reverse else j
            rows = pl.ds(pl.multiple_of(jj * 8, 8), 8)
            vr, vi = re_ref[rows, :], im_ref[rows, :]
            for t, k in enumerate((1, 2, 4)):
                sh = (8 - k) if reverse else k
                rr, ri = pltpu.roll(vr, sh, 0), pltpu.roll(vi, sh, 0)
                pr, pi = tab_ref[2 * t], tab_ref[2 * t + 1]
                vr, vi = vr + pr * rr - pi * ri, vi + pr * ri + pi * rr
            lr, li = tab_ref[6], tab_ref[7]
            vr, vi = vr + lr * cr - li * ci, vi + lr * ci + li * cr
            ore_ref[rows, :] = vr
            oim_ref[rows, :] = vi
            e = 0 if reverse else 7
            return (jnp.broadcast_to(vr[e:e + 1, :], (8, SC)), jnp.broadcast_to(vi[e:e + 1, :], (8, SC)))

        cr, ci = lax.fori_loop(0, ng, step, (cre[...], cim[...]))
        cre[...] = cr
        cim[...] = ci

    blk = pl.BlockSpec((SL, SC), lambda j, i: (ti(i), j))
    return pl.pallas_call(
        body, name=name, grid=(NCH // SC, nl),
        in_specs=[blk, blk, pl.BlockSpec((8, 8, SC), lambda j, i: (0, 0, j))], out_specs=[blk, blk],
        out_shape=[jax.ShapeDtypeStruct((n, NCH), F32)] * 2,
        scratch_shapes=[pltpu.VMEM((8, SC), F32), pltpu.VMEM((8, SC), F32)],
        compiler_params=_cp("parallel", "arbitrary"),
    )(vre, vim, tab)


def _s5_out(xre, xim, cre, cimn, proj, dvec):
    n = xre.shape[0]
    uoff = OFF_U // 128

    def body(xr_ref, xi_ref, cr_ref, ci_ref, u_ref, d_ref, y_ref, g_ref):
        y = (_dot(xr_ref[...].astype(BF16), cr_ref[...]) + _dot(xi_ref[...].astype(BF16), ci_ref[...])
             + d_ref[...] * u_ref[...])
        y_ref[...] = y
        g_ref[...] = _gelu(y).astype(BF16)

    xs = pl.BlockSpec((SL, SCB), lambda i, j: (i, j))
    blk = pl.BlockSpec((None, SCB, 128), lambda i, j: (j, 0, 0))
    out = pl.BlockSpec((SL, 128), lambda i, j: (i, j))
    return pl.pallas_call(
        body, name="s5_out", grid=(n // SL, SB),
        in_specs=[xs, xs, blk, blk, pl.BlockSpec((SL, 128), lambda i, j: (i, uoff + j)),
                  pl.BlockSpec((1, 128), lambda i, j: (0, j))],
        out_specs=[out, out],
        out_shape=[jax.ShapeDtypeStruct((n, DS5), F32), jax.ShapeDtypeStruct((n, DS5), BF16)],
        compiler_params=_cp("parallel", "parallel"),
    )(xre, xim, cre, cimn, proj, dvec)


def _s5_out_bwd(dg, ypre, crt, cimnt, proj, dvec, xre, xim):
    n = dg.shape[0]
    uoff = OFF_U // 128
    nl = n // SL

    def body(dg_ref, y_ref, cr_ref, ci_ref, u_ref, d_ref, xr_ref, xi_ref,
             gr_ref, gi_ref, dus_ref, gcr_ref, gci_ref, gd_ref):
        dy = dg_ref[...] * _dgelu(y_ref[...])
        dyb = dy.astype(BF16)
        gr_ref[...] = _dot(dyb, cr_ref[...])
        gi_ref[...] = _dot(dyb, ci_ref[...])
        dus_ref[...] = dy * d_ref[...]

        @pl.when(pl.program_id(1) == 0)
        def _():
            gcr_ref[...] = jnp.zeros_like(gcr_ref)
            gci_ref[...] = jnp.zeros_like(gci_ref)
            gd_ref[...] = jnp.zeros_like(gd_ref)

        gcr_ref[...] += _dot(xr_ref[...].astype(BF16), dyb, TN)
        gci_ref[...] -= _dot(xi_ref[...].astype(BF16), dyb, TN)
        gd_ref[...] += jnp.sum(dy * u_ref[...], axis=0, keepdims=True)

    u128 = pl.BlockSpec((SL, 128), lambda j, i: (i, j))
    xs = pl.BlockSpec((SL, SCB), lambda j, i: (i, j))
    blk = pl.BlockSpec((None, 128, SCB), lambda j, i: (j, 0, 0))
    gblk = pl.BlockSpec((None, SCB, 128), lambda j, i: (j, 0, 0))
    vec = pl.BlockSpec((1, 128), lambda j, i: (0, j))
    return pl.pallas_call(
        body, name="s5_out_bwd", grid=(SB, nl),
        in_specs=[u128, u128, blk, blk, pl.BlockSpec((SL, 128), lambda j, i: (i, uoff + j)), vec, xs, xs],
        out_specs=[xs, xs, u128, gblk, gblk, vec],
        out_shape=[jax.ShapeDtypeStruct((n, NCH), F32)] * 2 + [jax.ShapeDtypeStruct((n, DS5), F32)]
        + [jax.ShapeDtypeStruct((SB, SCB, 128), F32)] * 2 + [jax.ShapeDtypeStruct((1, DS5), F32)],
        compiler_params=_cp("parallel", "arbitrary"),
    )(dg, ypre, crt, cimnt, proj, dvec, xre, xim)


def _s5_in_bwd(are, aim, brt, bit, proj, dus, xre, xim):
    n = are.shape[0]
    uoff = OFF_U // 128
    per = SL // 8

    def body(ar_ref, ai_ref, br_ref, bi_ref, u_ref, dus_ref, xr_ref, xi_ref, pr_ref, pi_ref,
             du_ref, gbr_ref, gbi_ref, glr_ref, gli_ref):
        i = pl.program_id(1)
        ar, ai = ar_ref[...], ai_ref[...]
        arb, aib = ar.astype(BF16), ai.astype(BF16)
        du_ref[...] = (_dot(arb, br_ref[...]) + _dot(aib, bi_ref[...]) + dus_ref[...]).astype(BF16)

        @pl.when(i == 0)
        def _():
            for r in (gbr_ref, gbi_ref, glr_ref, gli_ref):
                r[...] = jnp.zeros_like(r)

        ub = u_ref[...].astype(BF16)
        gbr_ref[...] += _dot(arb, ub, TN)
        gbi_ref[...] += _dot(aib, ub, TN)
        row0 = lax.broadcasted_iota(jnp.int32, (SL, SCB), 0) == 0
        last_r = jnp.where(i > 0, pr_ref[7:8, :], 0.0)
        last_i = jnp.where(i > 0, pi_ref[7:8, :], 0.0)
        xpr = jnp.where(row0, last_r, pltpu.roll(xr_ref[...], 1, 0))
        xpi = jnp.where(row0, last_i, pltpu.roll(xi_ref[...], 1, 0))
        glr_ref[...] += jnp.sum(ar * xpr + ai * xpi, axis=0, keepdims=True)
        gli_ref[...] += jnp.sum(ai * xpr - ar * xpi, axis=0, keepdims=True)

    xs = pl.BlockSpec((SL, SCB), lambda j, i: (i, j))
    prev = pl.BlockSpec((8, SCB), lambda j, i: (jnp.maximum(i * per - 1, 0), j))
    blk = pl.BlockSpec((None, SCB, 128), lambda j, i: (j, 0, 0))
    u128 = pl.BlockSpec((SL, 128), lambda j, i: (i, j))
    vec = pl.BlockSpec((1, SCB), lambda j, i: (0, j))
    return pl.pallas_call(
        body, name="s5_in_bwd", grid=(SB, n // SL),
        in_specs=[xs, xs, blk, blk, pl.BlockSpec((SL, 128), lambda j, i: (i, uoff + j)), u128, xs, xs, prev, prev],
        out_specs=[u128, blk, blk, vec, vec],
        out_shape=[jax.ShapeDtypeStruct((n, DS5), BF16)] + [jax.ShapeDtypeStruct((SB, SCB, 128), F32)] * 2
        + [jax.ShapeDtypeStruct((1, NCH), F32)] * 2,
        compiler_params=_cp("parallel", "arbitrary"),
    )(are, aim, brt, bit, proj, dus, xre, xim, xre, xim)


MC = 1024


def _merge_specs():
    ga = pl.BlockSpec((TL, MC), lambda i, j: (i, OFF_GA // MC + j))
    gb = pl.BlockSpec((TL, MC), lambda i, j: (i, OFF_GB // MC + j))
    col = pl.BlockSpec((TL, MC), lambda i, j: (i, j))
    gate = pl.BlockSpec((TL, MC), lambda i, j: (i, D // MC + j))
    return ga, gb, col, gate


def _merge_fwd(proj, ya, vg):
    n = ya.shape[0]

    def body(ga_ref, gb_ref, ya_ref, v_ref, g_ref, o_ref):
        yb = v_ref[...] * _sig(g_ref[...])
        o_ref[...] = (_sig(ga_ref[...]) * ya_ref[...] + _sig(gb_ref[...]) * yb).astype(BF16)

    ga, gb, col, gate = _merge_specs()
    return pl.pallas_call(
        body, name="merge_fwd", grid=(n // TL, D // MC), in_specs=[ga, gb, col, col, gate], out_specs=col,
        out_shape=jax.ShapeDtypeStruct((n, D), BF16), compiler_params=_cp("parallel", "parallel"),
    )(proj, proj, ya, vg, vg)


def _merge_bwd(dm, proj, ya, vg):
    n = ya.shape[0]

    def body(dm_ref, ga_ref, gb_ref, ya_ref, v_ref, g_ref, dga_ref, dgb_ref, dya_ref, dv_ref, dg_ref):
        d = dm_ref[...]
        sa, sb, sg = _sig(ga_ref[...]), _sig(gb_ref[...]), _sig(g_ref[...])
        v = v_ref[...]
        yb = v * sg
        dga_ref[...] = (d * ya_ref[...] * sa * (1.0 - sa)).astype(BF16)
        dgb_ref[...] = (d * yb * sb * (1.0 - sb)).astype(BF16)
        dya_ref[...] = (d * sa).astype(BF16)
        dyb = d * sb
        dv_ref[...] = (dyb * sg).astype(BF16)
        dg_ref[...] = (dyb * v * sg * (1.0 - sg)).astype(BF16)

    ga, gb, col, gate = _merge_specs()
    o = jax.ShapeDtypeStruct((n, D), BF16)
    return pl.pallas_call(
        body, name="merge_bwd", grid=(n // TL, D // MC), in_specs=[col, ga, gb, col, col, gate],
        out_specs=[col] * 5, out_shape=[o] * 5, compiler_params=_cp("parallel", "parallel"),
    )(dm, proj, proj, ya, vg, vg)


def _adamw_update(wv, gv, mv, vv):
    nm = B1 * mv + (1.0 - B1) * gv
    nv = B2 * vv + (1.0 - B2) * (gv * gv)
    m_hat = nm / (1.0 - B1 ** STEP)
    v_hat = nv / (1.0 - B2 ** STEP)
    return -LR * (m_hat / (jnp.sqrt(v_hat) + AEPS) + WD * wv), nm, nv


def _adamw(w, g, m, v, name):
    r, c = w.shape
    tr = _pick(r, 128)

    def body(w_ref, g_ref, m_ref, v_ref, d_ref, nm_ref, nv_ref):
        d_ref[...], nm_ref[...], nv_ref[...] = _adamw_update(w_ref[...], g_ref[...], m_ref[...], v_ref[...])

    blk = pl.BlockSpec((tr, c), lambda i: (i, 0))
    o = jax.ShapeDtypeStruct((r, c), F32)
    return pl.pallas_call(
        body, name=name, grid=(r // tr,), in_specs=[blk] * 4, out_specs=[blk] * 3, out_shape=[o] * 3,
        compiler_params=_cp("parallel"),
    )(w, g, m, v)


def _adamw_halves(w, g_mine, g_other, m, v, cidx, name):
    _, r, c = w.shape
    hr, gc = g_mine.shape
    tr = _pick(hr, 128)
    nbh = hr // tr
    assert gc == c and 2 * hr - tr < r <= 2 * hr

    def body(cs, w_ref, gm_ref, go_ref, m_ref, v_ref, g_ref, d_ref, nm_ref, nv_ref):
        mine = pl.program_id(0) // nbh == cs[0]
        gv = jnp.where(mine, gm_ref[...], go_ref[...])
        g_ref[...] = gv
        d_ref[...], nm_ref[...], nv_ref[...] = _adamw_update(w_ref[...], gv, m_ref[...], v_ref[...])

    blk = pl.BlockSpec((None, tr, c), lambda i, cs: (0, i, 0))
    gmine = pl.BlockSpec((tr, gc), lambda i, cs: (jnp.where(i // nbh == cs[0], i % nbh, 0), 0))
    gother = pl.BlockSpec((tr, gc), lambda i, cs: (jnp.where(i // nbh == cs[0], 0, i % nbh), 0))
    o = jax.ShapeDtypeStruct((1, r, c), F32)
    return pl.pallas_call(
        body, name=name,
        grid_spec=pltpu.PrefetchScalarGridSpec(num_scalar_prefetch=1, grid=(2 * nbh,),
                                               in_specs=[blk, gmine, gother, blk, blk], out_specs=[blk] * 4),
        out_shape=[o] * 4, compiler_params=_cp("parallel"),
    )(cidx, w, g_mine, g_other, m, v)


def _chip_sum(part, sib, cidx, name):
    _, r, cc = part.shape
    hr = r // 2
    tr = _pick(hr, 256)

    def body(cs, p_ref, s_ref, o_ref):
        o_ref[...] = (p_ref[...].astype(F32) + s_ref[...].astype(F32)).astype(BF16)

    blk = pl.BlockSpec((None, tr, cc), lambda k, i, cs: (k, i, 0))
    return pl.pallas_call(
        body, name=name,
        grid_spec=pltpu.PrefetchScalarGridSpec(
            num_scalar_prefetch=1, grid=(4, hr // tr),
            in_specs=[pl.BlockSpec((None, None, tr, cc), lambda k, i, cs: (k, cs[0], i, 0)), blk], out_specs=blk),
        out_shape=jax.ShapeDtypeStruct((4, hr, cc), BF16), compiler_params=_cp("parallel", "parallel"),
    )(cidx, part.reshape(4, 2, hr, cc), sib)


def _shard_sum(own, got, sidx, name):
    _, hr, cc = own.shape
    tr = _pick(hr, 256)

    def body(cs, own_ref, g0, g1, g2, g3, o_ref):
        acc = None
        for k, g_ref in enumerate((g0, g1, g2, g3)):
            term = jnp.where(cs[0] == k, own_ref[...], g_ref[...]).astype(F32)
            acc = term if acc is None else acc + term
        o_ref[...] = acc

    def got_spec(k):
        return pl.BlockSpec((None, tr, cc), lambda i, cs: (jnp.where(cs[0] == k, (k + 1) % 4, k), i, 0))

    return pl.pallas_call(
        body, name=name,
        grid_spec=pltpu.PrefetchScalarGridSpec(
            num_scalar_prefetch=1, grid=(hr // tr,),
            in_specs=[pl.BlockSpec((None, tr, cc), lambda i, cs: (cs[0], i, 0))] + [got_spec(k) for k in range(4)],
            out_specs=pl.BlockSpec((tr, cc), lambda i, cs: (i, 0))),
        out_shape=jax.ShapeDtypeStruct((hr, cc), F32), compiler_params=_cp("parallel"),
    )(sidx, own, got, got, got, got)


def _sum_slabs(xs, name, out_dtype=F32):
    r, c = xs[0].shape
    tr = _pick(r, 256)

    def body(*refs):
        acc = refs[0][...].astype(F32)
        for ref in refs[1:-1]:
            acc = acc + ref[...].astype(F32)
        refs[-1][...] = acc.astype(out_dtype)

    blk = pl.BlockSpec((tr, c), lambda i: (i, 0))
    return pl.pallas_call(
        body, name=name, grid=(r // tr,), in_specs=[blk] * len(xs), out_specs=blk,
        out_shape=jax.ShapeDtypeStruct((r, c), out_dtype), compiler_params=_cp("parallel"),
    )(*xs)


def _place():
    return lax.axis_index("x"), lax.axis_index("y"), lax.axis_index("c")


def _gather_small(v, after):
    m_per, n = v.shape

    def body(x_ref, _, out_ref, send_sems, recv_sems, local_sem):
        x, y, c = _place()
        me, sibling = (x, y, c), (x, y, 1 - c)
        chips = [(1 - x, y), (x, 1 - y), (1 - x, 1 - y)]

        def rows(px, py, pc):
            return out_ref.at[pl.ds((4 * px + 2 * py + pc) * m_per, m_per), :]

        def copy(k, block, to, src=None):
            return pltpu.make_async_remote_copy(
                src_ref=rows(*block) if src is None else src, dst_ref=rows(*block),
                send_sem=send_sems.at[k], recv_sem=recv_sems.at[k], device_id=to, device_id_type=MESH)

        mine = pltpu.make_async_copy(x_ref, rows(*me), local_sem)
        mine.start()
        first = [copy(0, me, sibling, src=x_ref)]
        first += [copy(1 + j, me, (*chip, c), src=x_ref) for j, chip in enumerate(chips)]
        for cp in first:
            cp.start()
        passed = [copy(4 + j, (*chip, c), sibling) for j, chip in enumerate(chips)]
        for j, chip in enumerate(chips):
            copy(1 + j, (*chip, c), me).wait_recv()
            passed[j].start()
        copy(0, sibling, me).wait_recv()
        for j, chip in enumerate(chips):
            copy(4 + j, (*chip, 1 - c), me).wait_recv()
        for cp in first + passed:
            cp.wait_send()
        mine.wait()

    return pl.pallas_call(
        body, name="gather_small_%d" % m_per,
        out_shape=jax.ShapeDtypeStruct((8 * m_per, n), v.dtype),
        in_specs=[pl.BlockSpec(memory_space=pltpu.VMEM), ANY], out_specs=pl.BlockSpec(memory_space=pltpu.VMEM),
        scratch_shapes=[pltpu.SemaphoreType.DMA((7,)), pltpu.SemaphoreType.DMA((7,)), pltpu.SemaphoreType.DMA],
        compiler_params=pltpu.CompilerParams(vmem_limit_bytes=VMEM_LIMIT),
    )(v, after)


def _allsum_small(v, name, after):
    r = v.shape[0]
    g = _gather_small(v, after)
    return _sum_slabs([g[k * r:(k + 1) * r] for k in range(8)], name)


def _pass_halves(got, shards, name):
    nt = len(got)

    def body(*refs):
        ins, own, outs = refs[:nt], refs[nt:2 * nt], refs[2 * nt:3 * nt]
        send_sems, recv_sems = refs[3 * nt:]
        x, y, c = _place()
        s = 2 * x + y
        chips = [(1 - x, y), (x, 1 - y), (1 - x, 1 - y)]

        def half(ref, t, slot, h):
            hr = ins[t].shape[1] // 2
            return ref.at[slot, pl.ds(h * hr, hr), :]

        def copy(t, j, h):
            px, py = chips[j]
            return pltpu.make_async_remote_copy(
                src_ref=half(ins[t], t, 2 * px + py, h), dst_ref=half(outs[t], t, 2 * px + py, h),
                send_sem=send_sems.at[4 * t + j], recv_sem=recv_sems.at[4 * t + j],
                device_id=(x, y, 1 - c), device_id_type=MESH)

        def whole(t):
            return pltpu.make_async_remote_copy(
                src_ref=own[t], dst_ref=outs[t].at[s], send_sem=send_sems.at[4 * t + 3],
                recv_sem=recv_sems.at[4 * t + 3], device_id=(x, y, 1 - c), device_id_type=MESH)

        sends = [copy(t, j, c) for t in range(nt) for j in range(3)] + [whole(t) for t in range(nt)]
        for cp in sends:
            cp.start()
        for t in range(nt):
            for j in range(3):
                copy(t, j, 1 - c).wait_recv()
            whole(t).wait_recv()
        for cp in sends:
            cp.wait_send()

    return pl.pallas_call(
        body, name=name,
        out_shape=[jax.ShapeDtypeStruct(a.shape, a.dtype) for a in got],
        in_specs=[ANY] * (2 * nt), out_specs=[ANY] * nt, input_output_aliases={t: t for t in range(nt)},
        scratch_shapes=[pltpu.SemaphoreType.DMA((4 * nt,)), pltpu.SemaphoreType.DMA((4 * nt,))],
    )(*got, *shards)


HBM = pl.BlockSpec(memory_space=pltpu.HBM)
SEM = pl.BlockSpec(memory_space=pltpu.SEMAPHORE)
EFFECT = pltpu.SideEffectType.DATAFLOW_SIDE_EFFECTING


PER_TENSOR = {"gather": 3, "scatter": 3, "swap": 1, "pass": 4, "whole": 1}


def _ici_copies(kind, srcs, lands, send_sems, recv_sems):
    x, y, c = _place()
    s = 2 * x + y
    sib = (x, y, 1 - c)
    chips = [(1 - x, y), (x, 1 - y), (1 - x, 1 - y)]
    cps = []

    def add(src, dst, dev):
        k = len(cps)
        cps.append(pltpu.make_async_remote_copy(src_ref=src, dst_ref=dst, send_sem=send_sems[k], recv_sem=recv_sems[k],
                                                device_id=dev, device_id_type=MESH))

    for t in range(len(srcs)):
        if kind == "gather":
            hr = srcs[t].shape[0] // 2
            for px, py in chips:
                add(srcs[t].at[pl.ds(c * hr, hr), :], lands[t].at[s, pl.ds(c * hr, hr), :], (px, py, c))
        elif kind == "scatter":
            for px, py in chips:
                add(srcs[t].at[2 * px + py], lands[t].at[s], (px, py, c))
        elif kind == "swap":
            hr = srcs[t].shape[1] // 2
            add(srcs[t].at[:, pl.ds((1 - c) * hr, hr), :], lands[t], sib)
        elif kind == "pass":
            hr = srcs[t].shape[1] // 2
            for px, py in chips:
                half = srcs[t].at[2 * px + py, pl.ds(c * hr, hr), :]
                add(half, half, sib)
            add(lands[t], srcs[t].at[s], sib)
        else:
            add(srcs[t], lands[t], sib)
    return cps


def _ici_start(kind, srcs, after, name, lands=None):
    nt = len(srcs)
    nc = PER_TENSOR[kind] * nt
    hbm = lambda a: pltpu.with_memory_space_constraint(a, pltpu.HBM)
    if lands is None:
        shape = {"gather": lambda a: (4,) + a.shape, "scatter": lambda a: a.shape,
                 "swap": lambda a: (4, a.shape[1] // 2, a.shape[2]), "whole": lambda a: a.shape}[kind]
        lands = [lax.empty(shape(a), a.dtype) for a in srcs]

    def body(*refs):
        src, land = refs[:nt], refs[nt:2 * nt]
        outs = refs[2 * nt + 1:]
        for cp in _ici_copies(kind, src, land, outs[:nc], outs[nc:2 * nc]):
            cp.start()
        outs[-1][...] = jnp.zeros_like(outs[-1])

    outs = pl.pallas_call(
        body, name=name,
        out_shape=tuple([pltpu.SemaphoreType.DMA(())] * (2 * nc) + [pltpu.HBM(a.shape, a.dtype) for a in srcs]
                        + [pltpu.HBM(a.shape, a.dtype) for a in lands] + [jax.ShapeDtypeStruct((8, 128), F32)]),
        in_specs=[HBM] * (2 * nt) + [ANY],
        out_specs=tuple([SEM] * (2 * nc) + [HBM] * (2 * nt) + [pl.BlockSpec(memory_space=pltpu.VMEM)]),
        input_output_aliases={i: 2 * nc + i for i in range(2 * nt)},
        compiler_params=pltpu.CompilerParams(has_side_effects=EFFECT),
    )(*[hbm(a) for a in srcs], *[hbm(a) for a in lands], after)
    return outs[:2 * nc], outs[2 * nc:2 * nc + nt], outs[2 * nc + nt:2 * nc + 2 * nt], outs[-1]


def _ici_wait(kind, sems, srcs, lands, after, name):
    nt = len(srcs)
    nc = PER_TENSOR[kind] * nt

    def body(*refs):
        src, land = refs[:nt], refs[nt:2 * nt]
        sem = refs[2 * nt:2 * nt + 2 * nc]
        for cp in _ici_copies(kind, src, land, sem[:nc], sem[nc:]):
            cp.wait_send()
            cp.wait_recv()

    outs = pl.pallas_call(
        body, name=name,
        out_shape=tuple(pltpu.HBM(a.shape, a.dtype) for a in list(srcs) + list(lands)),
        in_specs=[HBM] * (2 * nt) + [SEM] * (2 * nc) + [ANY], out_specs=tuple([HBM] * (2 * nt)),
        input_output_aliases={i: i for i in range(2 * nt)},
        compiler_params=pltpu.CompilerParams(has_side_effects=EFFECT),
    )(*srcs, *lands, *sems, after)
    return outs[:nt], outs[nt:]


def _s5_params(lam_re, lam_im, log_dt, b_re, b_im):
    lr = jnp.minimum(lam_re, EIG_MAX)
    dt = jnp.exp(log_dt)[:, None]
    mag = jnp.exp(lr * dt)
    lbr, lbi = mag * jnp.cos(lam_im * dt), mag * jnp.sin(lam_im * dt)
    den = lr * lr + lam_im * lam_im
    qr = ((lbr - 1.0) * lr + lbi * lam_im) / den
    qi = (lbi * lr - (lbr - 1.0) * lam_im) / den
    bbr = qr[..., None] * b_re - qi[..., None] * b_im
    bbi = qr[..., None] * b_im + qi[..., None] * b_re
    return lbr, lbi, bbr, bbi


def _cmul(a, b):
    return a[0] * b[0] - a[1] * b[1], a[0] * b[1] + a[1] * b[0]


def _scan_table(lr, li, reverse):
    l1 = (lr.reshape(1, NCH), li.reshape(1, NCH))
    pows = [l1]
    for _ in range(7):
        pows.append(_cmul(pows[-1], l1))
    r = jnp.arange(8)[:, None]
    tabs = []
    for k in (1, 2, 4):
        keep = (r < 8 - k) if reverse else (r >= k)
        tabs += [jnp.where(keep, pows[k - 1][0], 0.0), jnp.where(keep, pows[k - 1][1], 0.0)]
    order = range(7, -1, -1) if reverse else range(8)
    tabs += [jnp.concatenate([pows[e][0] for e in order], axis=0), jnp.concatenate([pows[e][1] for e in order], axis=0)]
    return jnp.stack(tabs).astype(F32)


_EYE8 = lambda: jnp.eye(8, dtype=F32)


def _to_in_blocks(b):
    return jnp.einsum("jgpc,gh->jgchp", b.reshape(8, 8, 64, 16), _EYE8()).reshape(8, 128, 512)


def _to_out_blocks(cm):
    return jnp.einsum("jgcp,gh->jgphc", cm.reshape(8, 8, 16, 64), _EYE8()).reshape(8, 512, 128)


def _from_out_blocks(g):
    return jnp.einsum("jgphc,gh->jgpc", g.reshape(8, 8, 64, 8, 16), _EYE8()).reshape(64, 64, 16)


def _local_step(x, target, hn1, proj, p, hooks):
    n = x.shape[0]
    g = {}
    dtraw = proj
    xbc = _conv_a_fwd(proj, p["conv_a_w"], p["conv_a_b"])
    to_lanes = lambda v: jnp.pad(jnp.pad(v.reshape(NG, HPG), ((0, 0), (0, 8 - HPG))).reshape(1, 8 * NG),
                                 ((0, 0), (0, HL - 8 * NG)))
    from_lanes = lambda v: v[:, :8 * NG].reshape(-1, NG, 8)[:, :, :HPG].reshape(-1, NG * HPG)
    hp = jnp.concatenate([to_lanes(p["dt_bias"]), to_lanes(p["a_log"]), jnp.zeros((6, HL), F32)], axis=0)
    lane = jnp.arange(HL)[:, None]
    emat = ((lane < 8 * NG) & (lane % 8 < HPG)
            & (jnp.arange(DI)[None, :] // HD == HPG * (lane // 8) + lane % 8)).astype(BF16)
    dexp = jnp.repeat(p["d_a"].reshape(1, NG * HPG), HD, axis=1)
    dt, s_cum, s_t, dt_e, s_e = _ssd_prep(dtraw, hp, emat)
    s8 = s_cum[:, :8 * NG].reshape(n, NG, 8).transpose(1, 0, 2)
    yssd, sprev = _ssd_fwd(xbc, s8, s_t, dt_e, s_e, dexp)
    yn = _gnorm_fwd(yssd, proj, p["norm_a_w"])
    tok = hooks["late_start"](yn)
    (lbr, lbi, bbr, bbi), s5_vjp = jax.vjp(_s5_params, p["s5_lam_re"], p["s5_lam_im"], p["s5_log_dt"],
                                           p["s5_b_re"], p["s5_b_im"])
    bin_r, bin_i = _to_in_blocks(bbr), _to_in_blocks(bbi)
    cout_r, cout_in = _to_out_blocks(p["s5_c_re"]), _to_out_blocks(-p["s5_c_im"])
    bur, bui = _s5_in(proj, bin_r.astype(BF16), bin_i.astype(BF16), after=tok)
    xre, xim = _s5_scan(bur, bui, _scan_table(lbr, lbi, False), False, "s5_scan_fwd")
    ypre, g5 = _s5_out(xre, xim, cout_r.astype(BF16), cout_in.astype(BF16), proj, p["s5_d"])
    p = {**p, **hooks["late_weights"](ypre)}
    ya = _matmul(yn, p["w_proj_a"], "nn", "mm_proj")
    vg = _matmul(g5, p["w_s5_glu"], "nn", "mm_glu", b_stacked=True)
    merged = _merge_fwd(proj, ya, vg)
    h1 = _matmul(merged, p["w_out"], "nn", "mm_out", residual=x)
    hn2 = _rms_fwd(h1, p["norm_ffn_w"], "rms_ffn")
    up = _matmul(hn2, p["w_up"], "nn", "mm_up", tn=1408, b_stacked=True)
    act = _conv_ffn_fwd(up, p["conv_ffn_w"], p["conv_ffn_b"])
    h2 = _matmul(act, p["w_down"], "nn", "mm_down", tk=DFF // 2, residual=h1)
    dh2, dh2b, g["norm_final_w"], loss_blk = _final(h2, p["norm_final_w"], target)
    g["w_down"] = _matmul(act, dh2b, "tn", "mm_gw_down", out_dtype=BF16, tm=DFF // 4).reshape(4, DFF // 4, D)
    dact = _matmul(dh2b, p["w_down"], "nt", "mm_dact", out_dtype=BF16, tn=DFF // 4)
    dup, g["conv_ffn_w"], g["conv_ffn_b"] = _conv_ffn_bwd(up, dact, p["conv_ffn_w"], p["conv_ffn_b"])
    g["w_up"] = _matmul(hn2, dup, "tn", "mm_gw_up", out_dtype=BF16, tn=1408, out_stacked=True)
    tok = hooks["swap_start"](["w_up", "w_down"], g, "s1")
    dhn2 = _matmul(dup, p["w_up"], "nt", "mm_dhn2", tk=2816, b_stacked=True, after=tok)
    tok = hooks["scatter_go"]("s1", dhn2)
    dh1, dh1b, g["norm_ffn_w"] = _rms_bwd(dhn2, h1, p["norm_ffn_w"], dh2, "rms_ffn_bwd", after=tok)
    g["w_out"] = _matmul(merged, dh1b, "tn", "mm_gw_out", out_dtype=BF16).reshape(4, D // 4, D)
    dmerged = _matmul(dh1b, p["w_out"], "nt", "mm_dmerged")
    dga, dgb, dya, dval, dgate = _merge_bwd(dmerged, proj, ya, vg)
    dvg = jnp.concatenate([dval, dgate], axis=1)
    g["w_s5_glu"] = _matmul(g5, dvg, "tn", "mm_gw_glu", out_dtype=BF16, out_stacked=True)
    dg5 = _matmul(dvg, p["w_s5_glu"], "nt", "mm_dg5", b_stacked=True)
    tr = lambda b: b.transpose(0, 2, 1)
    gxr, gxi, dus, gcr, gci, g["s5_d"] = _s5_out_bwd(dg5, ypre, tr(cout_r).astype(BF16), tr(cout_in).astype(BF16),
                                                     proj, p["s5_d"], xre, xim)
    are, aim = _s5_scan(gxr, gxi, _scan_table(lbr, -lbi, True), True, "s5_scan_bwd")
    du, gbr, gbi, glr, gli = _s5_in_bwd(are, aim, tr(bin_r).astype(BF16), tr(bin_i).astype(BF16), proj, dus, xre, xim)
    g["s5_c_re"] = _from_out_blocks(gcr).transpose(0, 2, 1)
    g["s5_c_im"] = _from_out_blocks(gci).transpose(0, 2, 1)
    (g["s5_lam_re"], g["s5_lam_im"], g["s5_log_dt"], g["s5_b_re"], g["s5_b_im"]) = s5_vjp(
        (glr.reshape(64, 64), gli.reshape(64, 64), _from_out_blocks(gbr), _from_out_blocks(gbi)))
    g["w_proj_a"] = _matmul(yn, dya, "tn", "mm_gw_proj", out_dtype=BF16).reshape(4, DI // 4, D)
    tok = hooks["swap_start"](["w_proj_a", "w_s5_glu", "w_out"], g, "s2")
    dyn = _matmul(dya, p["w_proj_a"], "nt", "mm_dyn", after=tok)
    tok = hooks["scatter_go"]("s2", dyn)
    dyssd, dz, g["norm_a_w"] = _gnorm_bwd(dyn, yssd, proj, p["norm_a_w"], tok)
    dxs, dbm, dcm, ds_e, ddt_e, tsum, dsh8, pd = _ssd_bwd(xbc, s8, s_t, dt_e, s_e, dexp, sprev, dyssd)
    dsh = jnp.pad(dsh8.transpose(1, 0, 2).reshape(n, 8 * NG), ((0, 0), (0, HL - 8 * NG)))
    draw, ps = _ssd_post(ds_e, ddt_e, tsum, dsh, dtraw, dt, hp, emat.T)
    g["dt_bias"] = from_lanes(ps[0:1])
    g["a_log"] = from_lanes(ps[1:2])
    g["d_a"] = pd.reshape(NG * HPG, HD).sum(axis=1).reshape(1, NG * HPG)
    ddt = draw.astype(BF16)
    dxbc_parts, gcw, gcb = [], [], []
    for arr, col0, nm in ((dxs, 0, "conv_a_bwd_x"), (dbm, DI, "conv_a_bwd_b"), (dcm, DI + NG * NS, "conv_a_bwd_c")):
        dpart, gw_, gb_ = _conv_a_bwd(proj, arr, p["conv_a_w"], p["conv_a_b"], col0, nm)
        dxbc_parts.append(dpart)
        gcw.append(gw_)
        gcb.append(gb_)
    g["conv_a_w"] = jnp.concatenate(gcw, axis=1)
    g["conv_a_b"] = jnp.concatenate(gcb, axis=1)
    dproj = jnp.concatenate([dz] + dxbc_parts + [du, dga, dgb, ddt], axis=1)
    g_main = _matmul(dproj, hn1, "tn", "mm_gw_in", out_dtype=BF16, tm=896, tn=2048)
    g_dt = g_main[NMAIN:NMAIN + 8 * NG].reshape(NG, 8, D)[:, :HPG].reshape(NG * HPG, D)
    g_sh = _move_rows(g_main, RUNS_TO_SHARDS, 4 * WPAD, MT, MT, "rows_to_shards")
    g["w_in"] = lax.dynamic_update_slice(g_sh, g_dt, (DT_SHARD_ROW, 0)).reshape(4, WPAD, D)
    hooks["swap_start"](["w_in"], g, "s3")
    tok = hooks["scatter_go"]("s3", g_dt)
    dhn1 = _matmul(dproj, p["w_full"], "nn", "mm_dhn1", tk=2688, after=tok)
    gx, _, g["norm_mix_w"] = _rms_bwd(dhn1, x, p["norm_mix_w"], dh1, "rms_mix_bwd")
    return loss_blk, gx, g


BIG = ["w_in", "w_proj_a", "w_s5_glu", "w_out", "w_up", "w_down"]
SMALL = ["norm_mix_w", "conv_a_w", "conv_a_b", "dt_bias", "a_log", "d_a", "norm_a_w", "s5_lam_re", "s5_lam_im",
         "s5_log_dt", "s5_b_re", "s5_b_im", "s5_c_re", "s5_c_im", "s5_d", "norm_ffn_w", "conv_ffn_w", "conv_ffn_b",
         "norm_final_w"]
ORDER = ["norm_mix_w", "w_in", "conv_a_w", "conv_a_b", "dt_bias", "a_log", "d_a", "norm_a_w", "w_proj_a", "s5_lam_re",
         "s5_lam_im", "s5_log_dt", "s5_b_re", "s5_b_im", "s5_c_re", "s5_c_im", "s5_d", "w_s5_glu", "w_out",
         "norm_ffn_w", "w_up", "conv_ffn_w", "conv_ffn_b", "w_down", "norm_final_w"]
CONV_FULL = {"conv_a_w": (KA, CONVD), "conv_ffn_w": (KF, 2 * DFF)}


def _pack(arrs):
    flat = jnp.concatenate([a.reshape(-1).astype(F32) for a in arrs])
    total = flat.shape[0]
    padded = -(-total // 1024) * 1024
    return jnp.pad(flat, (0, padded - total)).reshape(padded // 128, 128)


def _unpack(block, shapes):
    flat = block.reshape(-1)
    out, at = [], 0
    for sh in shapes:
        size = math.prod(sh)
        out.append(flat[at:at + size].reshape(sh))
        at += size
    return out


def kernel(x, norm_mix_w, w_in, conv_a_w, conv_a_b, dt_bias, a_log, d_a, norm_a_w, w_proj_a, s5_lam_re, s5_lam_im, s5_log_dt, s5_b_re, s5_b_im, s5_c_re, s5_c_im, s5_d, w_s5_glu, w_out, norm_ffn_w, w_up, conv_ffn_w, conv_ffn_b, w_down, norm_final_w, loss_target, m_norm_mix_w, m_w_in, m_conv_a_w, m_conv_a_b, m_dt_bias, m_a_log, m_d_a, m_norm_a_w, m_w_proj_a, m_s5_lam_re, m_s5_lam_im, m_s5_log_dt, m_s5_b_re, m_s5_b_im, m_s5_c_re, m_s5_c_im, m_s5_d, m_w_s5_glu, m_w_out, m_norm_ffn_w, m_w_up, m_conv_ffn_w, m_conv_ffn_b, m_w_down, m_norm_final_w, v_norm_mix_w, v_w_in, v_conv_a_w, v_conv_a_b, v_dt_bias, v_a_log, v_d_a, v_norm_a_w, v_w_proj_a, v_s5_lam_re, v_s5_lam_im, v_s5_log_dt, v_s5_b_re, v_s5_b_im, v_s5_c_re, v_s5_c_im, v_s5_d, v_w_s5_glu, v_w_out, v_norm_ffn_w, v_w_up, v_conv_ffn_w, v_conv_ffn_b, v_w_down, v_norm_final_w):
    w = dict(norm_mix_w=norm_mix_w, w_in=w_in, conv_a_w=conv_a_w, conv_a_b=conv_a_b, dt_bias=dt_bias, a_log=a_log, d_a=d_a, norm_a_w=norm_a_w, w_proj_a=w_proj_a, s5_lam_re=s5_lam_re, s5_lam_im=s5_lam_im, s5_log_dt=s5_log_dt, s5_b_re=s5_b_re, s5_b_im=s5_b_im, s5_c_re=s5_c_re, s5_c_im=s5_c_im, s5_d=s5_d, w_s5_glu=w_s5_glu, w_out=w_out, norm_ffn_w=norm_ffn_w, w_up=w_up, conv_ffn_w=conv_ffn_w, conv_ffn_b=conv_ffn_b, w_down=w_down, norm_final_w=norm_final_w)
    m = dict(norm_mix_w=m_norm_mix_w, w_in=m_w_in, conv_a_w=m_conv_a_w, conv_a_b=m_conv_a_b, dt_bias=m_dt_bias, a_log=m_a_log, d_a=m_d_a, norm_a_w=m_norm_a_w, w_proj_a=m_w_proj_a, s5_lam_re=m_s5_lam_re, s5_lam_im=m_s5_lam_im, s5_log_dt=m_s5_log_dt, s5_b_re=m_s5_b_re, s5_b_im=m_s5_b_im, s5_c_re=m_s5_c_re, s5_c_im=m_s5_c_im, s5_d=m_s5_d, w_s5_glu=m_w_s5_glu, w_out=m_w_out, norm_ffn_w=m_norm_ffn_w, w_up=m_w_up, conv_ffn_w=m_conv_ffn_w, conv_ffn_b=m_conv_ffn_b, w_down=m_w_down, norm_final_w=m_norm_final_w)
    v = dict(norm_mix_w=v_norm_mix_w, w_in=v_w_in, conv_a_w=v_conv_a_w, conv_a_b=v_conv_a_b, dt_bias=v_dt_bias, a_log=v_a_log, d_a=v_d_a, norm_a_w=v_norm_a_w, w_proj_a=v_w_proj_a, s5_lam_re=v_s5_lam_re, s5_lam_im=v_s5_lam_im, s5_log_dt=v_s5_log_dt, s5_b_re=v_s5_b_re, s5_b_im=v_s5_b_im, s5_c_re=v_s5_c_re, s5_c_im=v_s5_c_im, s5_d=v_s5_d, w_s5_glu=v_w_s5_glu, w_out=v_w_out, norm_ffn_w=v_norm_ffn_w, w_up=v_w_up, conv_ffn_w=v_conv_ffn_w, conv_ffn_b=v_conv_ffn_b, w_down=v_w_down, norm_final_w=v_norm_final_w)
    xi, yi, ci = _place()
    chip = 2 * xi + yi

    cidx = jnp.reshape(ci, (1,)).astype(jnp.int32)
    sidx = jnp.reshape(chip, (1,)).astype(jnp.int32)

    tw = lambda a: jnp.transpose(a[0])[None]
    w["w_in"], m["w_in"], v["w_in"] = tw(w_in), tw(m_w_in), tw(v_w_in)
    shards = [w[k][0].astype(BF16) for k in BIG]
    shards[0] = jnp.pad(shards[0], ((0, WPAD - WSH), (0, 0)))

    late = {}

    def late_start(after):
        srcs, got = _ici_wait("gather", g_sems, g_srcs, g_lands, after, "gather_rest_wait")
        late["sems"], late["got"], late["srcs"], tok = _ici_start("pass", list(got), cidx, "pass_rest_start",
                                                                  lands=list(srcs))
        return tok

    def late_weights(after):
        full, _ = _ici_wait("pass", late["sems"], late["got"], late["srcs"], after, "pass_rest_wait")
        return {"w_proj_a": full[0].reshape(DI, D), "w_s5_glu": full[1], "w_out": full[2].reshape(D, D),
                "w_up": full[3], "w_down": full[4].reshape(DFF, D)}

    swaps, pending = {}, []

    def swap_start(names, g, tag):
        sems, parts, lands, tok = _ici_start("swap", [g[k] for k in names], cidx, "swap_start_" + tag)
        swaps[tag] = (names, sems, parts, lands)
        return tok

    def scatter_go(tag, after):
        names, sems, parts, lands = swaps[tag]
        parts, sib = _ici_wait("swap", sems, parts, lands, after, "swap_wait_" + tag)
        sums = [_chip_sum(parts[t], sib[t], cidx, "chip_sum_" + k) for t, k in enumerate(names)]
        sems, srcs, lands, tok = _ici_start("scatter", sums, cidx, "scatter_start_" + tag)
        pending.append((names, tag, sems, srcs, lands))
        return tok

    hooks = {"late_start": late_start, "late_weights": late_weights, "swap_start": swap_start,
             "scatter_go": scatter_go}
    conv_blocks = []
    for k, (taps, cols) in CONV_FULL.items():
        shard = jnp.where(ci == 0, w[k][0], 0.0)
        conv_blocks.append(lax.dynamic_update_slice_in_dim(jnp.zeros((taps, cols), F32), shard, chip * (cols // 4), 1))
    conv_full = _unpack(_allsum_small(_pack(conv_blocks), "sum_conv_w", cidx), [CONV_FULL[k] for k in CONV_FULL])

    half = D // 2
    sh_a, sh_b = shards[0][:, :half], shards[0][:, half:]
    a_sems, a_srcs, a_lands, a_tok = _ici_start("gather", [sh_a], conv_full[0], "gather_in_a_start")
    b_sems, b_srcs, b_lands, b_tok = _ici_start("gather", [sh_b], a_tok, "gather_in_b_start")
    hn1 = _rms_fwd(x[0], norm_mix_w, "rms_mix", after=b_tok)

    def w_in_part(sems, srcs, lands, after, tag):
        srcs, got = _ici_wait("gather", sems, srcs, lands, after, "gather_in_%s_wait" % tag)
        w_sh = _pass_halves(list(got), list(srcs), "pass_halves_in_" + tag)[0].reshape(4 * WPAD, half)
        w_dt = jnp.pad(w_sh[DT_SHARD_ROW:DT_SHARD_ROW + NG * HPG].reshape(NG, HPG, half),
                       ((0, 0), (0, 8 - HPG), (0, 0)))
        return lax.dynamic_update_slice(_move_rows(w_sh, RUNS_TO_MAIN, NFULL, MT, MT, "rows_to_main_" + tag),
                                        w_dt.reshape(8 * NG, half), (NMAIN, 0))

    w_a = w_in_part(a_sems, a_srcs, a_lands, hn1, "a")
    proj_a = _matmul(hn1[:, :half], w_a, "nt", "mm_in_a", tn=1920)
    w_b = w_in_part(b_sems, b_srcs, b_lands, proj_a, "b")
    g_sems, g_srcs, g_lands, token = _ici_start("gather", shards[1:], w_b, "gather_rest_start")
    proj = _matmul(hn1[:, half:], w_b, "nt", "mm_in_b", tn=1920, residual=proj_a, after=token)
    w_full = jnp.concatenate([w_a, w_b], axis=1)
    p = {
        "w_full": w_full,
        "conv_a_w": conv_full[0], "conv_ffn_w": conv_full[1],
        "conv_a_b": conv_a_b, "conv_ffn_b": conv_ffn_b,
        "norm_mix_w": norm_mix_w, "norm_a_w": norm_a_w, "norm_ffn_w": norm_ffn_w,
        "norm_final_w": norm_final_w.reshape(1, D),
        "dt_bias": dt_bias, "a_log": a_log, "d_a": d_a, "s5_d": s5_d,
        "s5_lam_re": s5_lam_re[0], "s5_lam_im": s5_lam_im[0], "s5_log_dt": s5_log_dt[0],
        "s5_b_re": s5_b_re[0], "s5_b_im": s5_b_im[0], "s5_c_re": s5_c_re[0], "s5_c_im": s5_c_im[0],
    }
    loss_blk, gx, g = _local_step(x[0], loss_target[0], hn1, proj, p, hooks)

    after, halves = gx, {}
    for names, tag, sems, srcs, lands in pending:
        srcs, got = _ici_wait("scatter", sems, srcs, lands, after, "scatter_wait_" + tag)
        for t, k in enumerate(names):
            halves[k] = _shard_sum(srcs[t], got[t], sidx, "shard_sum_" + k)
        after = halves[names[0]]
    w_sems, g_mine, w_lands, w_tok = _ici_start("whole", [halves[k] for k in BIG], cidx, "whole_start")

    small_shapes = [CONV_FULL.get(k, w[k].shape[1:] if k != "norm_final_w" else w[k].shape) for k in SMALL]
    small = _allsum_small(_pack([g[k] for k in SMALL] + [loss_blk[0:1, 0:1]]), "sum_small_grads", w_tok)
    small_grads = dict(zip(SMALL + ["loss"], _unpack(small, small_shapes + [(1,)])))
    for k, (taps, cols) in CONV_FULL.items():
        small_grads[k] = lax.dynamic_slice_in_dim(small_grads[k], chip * (cols // 4), cols // 4, axis=1)
    loss = small_grads.pop("loss").reshape(())

    grads, delta, new_m, new_v = {}, {}, {}, {}
    for k in SMALL:
        grads[k] = small_grads[k].reshape(w[k].shape)
    pk = lambda t: _pack([t[k] for k in SMALL])
    d_, m_, v_ = _adamw(pk(w), pk(grads), pk(m), pk(v), "adamw_small")
    shapes = [w[k].shape for k in SMALL]
    for k, dd, mm, vv in zip(SMALL, _unpack(d_, shapes), _unpack(m_, shapes), _unpack(v_, shapes)):
        delta[k], new_m[k], new_v[k] = dd, mm, vv
    g_mine, g_other = _ici_wait("whole", w_sems, g_mine, w_lands, d_, "whole_wait")
    for t, k in enumerate(BIG):
        outs = _adamw_halves(w[k], g_mine[t], g_other[t], m[k], v[k], cidx, "adamw_" + k)
        grads[k], delta[k], new_m[k], new_v[k] = [tw(o) for o in outs] if k == "w_in" else outs
    return (loss, gx[None], *[grads[k] for k in ORDER], *[delta[k] for k in ORDER],
            *[new_m[k] for k in ORDER], *[new_v[k] for k in ORDER])
```

```python
import functools
import math

import jax
import jax.numpy as jnp
from jax import lax
from jax.experimental import pallas as pl
from jax.experimental.pallas import tpu as pltpu

F32 = jnp.float32
BF16 = jnp.bfloat16
HI = lax.Precision.HIGHEST
MESH = pl.DeviceIdType.MESH
ANY = pl.BlockSpec(memory_space=pl.ANY)

D = 2048
DI = 3072
HD = 64
NG = 8
HPG = 6
GW = HPG * HD
NS = 128
KA = 4
Q = 256
CONVD = DI + 2 * NG * NS
DS5 = 1024
NCH = 4096
DFF = 5632
KF = 3
EPS = 1e-6
EIG_MAX = -1e-4
NMAIN = 13312
OFF_XBC, OFF_U, OFF_GA, OFF_GB = 3072, 8192, 9216, 11264
WSH = 3340
WPAD = 3360
IN_SPLIT = [DI, DI + CONVD, DI + CONVD + NG * HPG]
NFULL = NMAIN + 128
MT = 336


def _w_in_runs():
    runs = []
    for k in range(4):
        for o_lo, o_hi, m_lo in ((0, IN_SPLIT[1], 0), (IN_SPLIT[2], 4 * WSH, IN_SPLIT[1])):
            lo, hi = max(o_lo, WSH * k), min(o_hi, WSH * (k + 1))
            if lo < hi:
                runs.append((m_lo + lo - o_lo, m_lo + hi - o_lo, WPAD * k + lo - WSH * k))
    return runs


RUNS_TO_MAIN = _w_in_runs()
RUNS_TO_SHARDS = [(s_lo, s_lo + m_hi - m_lo, m_lo) for m_lo, m_hi, s_lo in RUNS_TO_MAIN]
DT_SHARD_ROW = WPAD * (IN_SPLIT[1] // WSH) + IN_SPLIT[1] % WSH
assert IN_SPLIT[1] // WSH == (IN_SPLIT[2] - 1) // WSH
VMEM_LIMIT = 56 * 1024 * 1024

LR, B1, B2, AEPS, WD, STEP = 0.001, 0.9, 0.999, 1e-08, 0.01, 10


def _cp(*sem):
    return pltpu.CompilerParams(dimension_semantics=sem, vmem_limit_bytes=VMEM_LIMIT)


def _sig(x):
    return jax.nn.sigmoid(x)


def _silu(x):
    return x * _sig(x)


def _dsilu(x):
    s = _sig(x)
    return s * (1.0 + x * (1.0 - s))


def _softplus(x):
    return jnp.maximum(x, 0.0) + jnp.log(1.0 + jnp.exp(-jnp.abs(x)))


_GC = math.sqrt(2.0 / math.pi)


def _gelu(x):
    return 0.5 * x * (1.0 + jnp.tanh(_GC * (x + 0.044715 * x * x * x)))


def _dgelu(x):
    t = jnp.tanh(_GC * (x + 0.044715 * x * x * x))
    return 0.5 * (1.0 + t) + 0.5 * x * (1.0 - t * t) * _GC * (1.0 + 3.0 * 0.044715 * x * x)


def _dot(a, b, dims=((1,), (0,)), prec=None):
    return lax.dot_general(a, b, (dims, ((), ())), precision=prec, preferred_element_type=F32)


NT = ((1,), (1,))
TN = ((0,), (0,))


def _pick(n, t):
    for unit in (128, 8):
        for cand in range(min(n, t) // unit * unit, 0, -unit):
            if n % cand == 0:
                return cand
    return n


def _matmul(a, b, mode, name, out_dtype=F32, tm=1024, tn=1024, tk=2048, residual=None, b_stacked=False,
            out_stacked=False, after=None):
    if b_stacked:
        _, brows, bn = b.shape
        bshape = (brows, 4 * bn)
    else:
        bshape = b.shape
    if mode == "nn":
        (m, k), (k2, n) = a.shape, bshape
    elif mode == "nt":
        (m, k), (n, k2) = a.shape, bshape
    else:
        (k, m), (k2, n) = a.shape, bshape
    assert k == k2
    tm = _pick(m, tm)
    tn = _pick(n // 4 if (out_stacked or (b_stacked and mode != "nt")) else n, tn)
    tk = _pick(k // 4 if (b_stacked and mode == "nt") else k, tk)
    nk = k // tk
    dims = {"nn": ((1,), (0,)), "nt": NT, "tn": TN}[mode]
    has_res = residual is not None
    n_in = 2 + has_res + (after is not None)

    def body(*refs):
        a_ref, b_ref = refs[0], refs[1]
        r_ref = refs[2] if has_res else None
        o_ref = refs[n_in]
        p = _dot(a_ref[...], b_ref[...], dims)

        def finish(r):
            if has_res:
                r = r + r_ref[...]
            o_ref[...] = r.astype(out_dtype)

        if nk == 1:
            finish(p)
        else:
            acc = refs[-1]
            kk = pl.program_id(2)

            @pl.when(kk == 0)
            def _():
                acc[...] = p

            @pl.when(kk > 0)
            def _():
                acc[...] += p

            @pl.when(kk == nk - 1)
            def _():
                finish(acc[...])

    if mode == "tn":
        a_spec = pl.BlockSpec((tk, tm), lambda i, j, kk: (kk, i))
    else:
        a_spec = pl.BlockSpec((tm, tk), lambda i, j, kk: (i, kk))
    if mode == "nt":
        if b_stacked:
            per = bn // tk
            b_spec = pl.BlockSpec((None, tn, tk), lambda i, j, kk: (kk // per, j, kk % per))
        else:
            b_spec = pl.BlockSpec((tn, tk), lambda i, j, kk: (j, kk))
    elif b_stacked:
        per = bn // tn
        b_spec = pl.BlockSpec((None, tk, tn), lambda i, j, kk: (j // per, kk, j % per))
    else:
        b_spec = pl.BlockSpec((tk, tn), lambda i, j, kk: (kk, j))
    o_spec = pl.BlockSpec((tm, tn), lambda i, j, kk: (i, j))
    if out_stacked:
        per_o = n // 4 // tn
        out_spec = pl.BlockSpec((None, tm, tn), lambda i, j, kk: (j // per_o, i, j % per_o))
        out_shape = jax.ShapeDtypeStruct((4, m, n // 4), out_dtype)
    else:
        out_spec, out_shape = o_spec, jax.ShapeDtypeStruct((m, n), out_dtype)
    in_specs, args = [a_spec, b_spec], [a, b]
    if has_res:
        in_specs.append(o_spec)
        args.append(residual)
    if after is not None:
        in_specs.append(ANY)
        args.append(after)
    return pl.pallas_call(
        body, name=name, grid=(m // tm, n // tn, nk),
        in_specs=in_specs, out_specs=out_spec, out_shape=out_shape,
        scratch_shapes=[pltpu.VMEM((tm, tn), F32)] if nk > 1 else [],
        compiler_params=_cp("parallel", "parallel", "arbitrary"),
    )(*args)


def _move_rows(src, runs, rows_out, t_out, t_in, name):
    rows_in, cols = src.shape
    nb_out, nb_in = rows_out // t_out, rows_in // t_in
    assert rows_out % t_out == 0 and rows_in % t_in == 0 and t_in >= t_out
    blk, off, lo, hi = ([[0] * nb_out for _ in range(2)] for _ in range(4))
    for i in range(nb_out):
        hits = [r for r in runs if r[0] < (i + 1) * t_out and r[1] > i * t_out]
        assert len(hits) <= 2
        for s, (o_lo, o_hi, s_lo) in enumerate(hits):
            lo[s][i] = max(o_lo, i * t_out) - i * t_out
            hi[s][i] = min(o_hi, (i + 1) * t_out) - i * t_out
            first = i * t_out + lo[s][i] - o_lo + s_lo
            blk[s][i] = min(first // t_in, nb_in - 1)
            off[s][i] = first - lo[s][i] - blk[s][i] * t_in
    table = jnp.asarray([blk[0], off[0], lo[0], hi[0], blk[1], off[1], lo[1], hi[1]], jnp.int32)

    def body(tab, a0, a1, b0, b1, o_ref):
        i = pl.program_id(0)
        o_ref[...] = jnp.zeros_like(o_ref)
        r = lax.broadcasted_iota(jnp.int32, (t_out, t_in), 0)
        k = lax.broadcasted_iota(jnp.int32, (t_out, t_in), 1)
        for s, (first, second) in enumerate(((a0, a1), (b0, b1))):
            off_s, lo_s, hi_s = tab[4 * s + 1, i], tab[4 * s + 2, i], tab[4 * s + 3, i]
            live = (r >= lo_s) & (r < hi_s)

            @pl.when(hi_s > lo_s)
            def _():
                sel = (live & (k == r + off_s)).astype(BF16)
                o_ref[...] += _dot(sel, first[...]).astype(o_ref.dtype)

            @pl.when((hi_s > lo_s) & (off_s + hi_s > t_in))
            def _():
                sel = (live & (k == r + off_s - t_in)).astype(BF16)
                o_ref[...] += _dot(sel, second[...]).astype(o_ref.dtype)

    def in_spec(s, nxt):
        return pl.BlockSpec((t_in, cols), lambda i, tab: (jnp.minimum(tab[4 * s, i] + nxt, nb_in - 1), 0))

    return pl.pallas_call(
        body, name=name,
        grid_spec=pltpu.PrefetchScalarGridSpec(
            num_scalar_prefetch=1, grid=(nb_out,),
            in_specs=[in_spec(0, 0), in_spec(0, 1), in_spec(1, 0), in_spec(1, 1)],
            out_specs=pl.BlockSpec((t_out, cols), lambda i, tab: (i, 0))),
        out_shape=jax.ShapeDtypeStruct((rows_out, cols), src.dtype), compiler_params=_cp("parallel"),
    )(table, src, src, src, src)


TL = 256


def _rms_fwd(x, w, name, after=None):
    n, d = x.shape

    def body(x_ref, w_ref, *rest):
        xv = x_ref[...]
        r = lax.rsqrt(jnp.mean(xv * xv, axis=-1, keepdims=True) + EPS)
        rest[-1][...] = (xv * r * w_ref[...]).astype(BF16)

    extra = [] if after is None else [after]
    return pl.pallas_call(
        body, name=name, grid=(n // TL,),
        in_specs=[pl.BlockSpec((TL, d), lambda i: (i, 0)), pl.BlockSpec((1, d), lambda i: (0, 0))] + [ANY] * len(extra),
        out_specs=pl.BlockSpec((TL, d), lambda i: (i, 0)),
        out_shape=jax.ShapeDtypeStruct((n, d), BF16), compiler_params=_cp("parallel"),
    )(x, w, *extra)


def _rms_bwd(dhn, x, w, dres, name, after=None):
    n, d = x.shape

    def body(g_ref, x_ref, w_ref, r_ref, *rest):
        dx_ref, dxb_ref, gw_ref = rest[-3:]
        xv = x_ref[...]
        r = lax.rsqrt(jnp.mean(xv * xv, axis=-1, keepdims=True) + EPS)
        xh = xv * r
        gv = g_ref[...]
        g = gv * w_ref[...]
        dx = r_ref[...] + r * (g - xh * jnp.mean(g * xh, axis=-1, keepdims=True))
        dx_ref[...] = dx
        dxb_ref[...] = dx.astype(BF16)

        @pl.when(pl.program_id(0) == 0)
        def _():
            gw_ref[...] = jnp.zeros_like(gw_ref)

        gw_ref[...] += jnp.sum(gv * xh, axis=0, keepdims=True)

    extra = [] if after is None else [after]
    row = pl.BlockSpec((TL, d), lambda i: (i, 0))
    vec = pl.BlockSpec((1, d), lambda i: (0, 0))
    return pl.pallas_call(
        body, name=name, grid=(n // TL,),
        in_specs=[row, row, vec, row] + [ANY] * len(extra), out_specs=[row, row, vec],
        out_shape=[jax.ShapeDtypeStruct((n, d), F32), jax.ShapeDtypeStruct((n, d), BF16),
                   jax.ShapeDtypeStruct((1, d), F32)],
        compiler_params=_cp("arbitrary"),
    )(dhn, x, w, dres, *extra)


def _final(h2, w, target):
    n, d = h2.shape

    def body(x_ref, w_ref, t_ref, dx_ref, dxb_ref, gw_ref, loss_ref):
        xv = x_ref[...]
        r = lax.rsqrt(jnp.mean(xv * xv, axis=-1, keepdims=True) + EPS)
        xh = xv * r
        diff = xh * w_ref[...] - t_ref[...]
        gv = diff * (1.0 / d)
        g = gv * w_ref[...]
        dx = r * (g - xh * jnp.mean(g * xh, axis=-1, keepdims=True))
        dx_ref[...] = dx
        dxb_ref[...] = dx.astype(BF16)

        @pl.when(pl.program_id(0) == 0)
        def _():
            gw_ref[...] = jnp.zeros_like(gw_ref)
            loss_ref[...] = jnp.zeros_like(loss_ref)

        gw_ref[...] += jnp.sum(gv * xh, axis=0, keepdims=True)
        part = 0.5 * jnp.sum(jnp.mean(diff * diff, axis=-1, keepdims=True), axis=0, keepdims=True)
        loss_ref[...] += jnp.broadcast_to(part, loss_ref.shape)

    row = pl.BlockSpec((TL, d), lambda i: (i, 0))
    vec = pl.BlockSpec((1, d), lambda i: (0, 0))
    return pl.pallas_call(
        body, name="final_loss", grid=(n // TL,),
        in_specs=[row, vec, row], out_specs=[row, row, vec, pl.BlockSpec((8, 128), lambda i: (0, 0))],
        out_shape=[jax.ShapeDtypeStruct((n, d), F32), jax.ShapeDtypeStruct((n, d), BF16),
                   jax.ShapeDtypeStruct((1, d), F32), jax.ShapeDtypeStruct((8, 128), F32)],
        compiler_params=_cp("arbitrary"),
    )(h2, w, target)


CT = 512
CL = 512


def _lagged(xf, taps, rows):
    return [xf[8:8 + rows]] + [pltpu.roll(xf, s, 0)[8:8 + rows] for s in range(1, taps)]


def _shift_up(x, u, n):
    if u == 0:
        return x[0:n]
    return pltpu.roll(x, x.shape[0] - u, 0)[0:n]


def _conv_pre(lagged, w_ref, b_ref, taps):
    pre = b_ref[...]
    for k in range(taps):
        pre = pre + w_ref[k:k + 1, :] * lagged[taps - 1 - k]
    return pre


def _conv_back(e, w_ref, taps):
    dx = w_ref[taps - 1:taps, :] * e[0:CL]
    for k in range(taps - 1):
        dx = dx + w_ref[k:k + 1, :] * _shift_up(e, taps - 1 - k, CL)
    return dx


def _halo_specs(n, col_of):
    per = CL // 8
    cur = pl.BlockSpec((CL, CT), lambda j, i, *_: (i, col_of(j)))
    prev = pl.BlockSpec((8, CT), lambda j, i, *_: (jnp.maximum(i * per - 1, 0), col_of(j)))
    nxt = pl.BlockSpec((8, CT), lambda j, i, *_: (jnp.minimum((i + 1) * per, n // 8 - 1), col_of(j)))
    return prev, cur, nxt


def _conv_a_fwd(proj, w, b):
    n = proj.shape[0]
    off = OFF_XBC // CT

    def body(p_ref, x_ref, w_ref, b_ref, o_ref):
        p8 = jnp.where(pl.program_id(1) > 0, p_ref[...], 0.0)
        xf = jnp.concatenate([p8, x_ref[...]], axis=0)
        o_ref[...] = _silu(_conv_pre(_lagged(xf, KA, CL), w_ref, b_ref, KA))

    prev, cur, _ = _halo_specs(n, lambda j: j + off)
    return pl.pallas_call(
        body, name="conv_a_fwd", grid=(CONVD // CT, n // CL),
        in_specs=[prev, cur, pl.BlockSpec((KA, CT), lambda j, i: (0, j)), pl.BlockSpec((1, CT), lambda j, i: (0, j))],
        out_specs=pl.BlockSpec((CL, CT), lambda j, i: (i, j)),
        out_shape=jax.ShapeDtypeStruct((n, CONVD), F32), compiler_params=_cp("parallel", "parallel"),
    )(proj, proj, w, b)


def _conv_a_bwd(proj, dout, w, b, col0, name):
    n, width = dout.shape
    off = (OFF_XBC + col0) // CT
    woff = col0 // CT
    nl = n // CL

    def body(p_ref, x_ref, n_ref, d_ref, dn_ref, w_ref, b_ref, dx_ref, dw_ref, db_ref):
        i = pl.program_id(1)
        xf = jnp.concatenate([jnp.where(i > 0, p_ref[...], 0.0), x_ref[...], n_ref[...]], axis=0)
        lag = _lagged(xf, KA, CL + 8)
        de = jnp.concatenate([d_ref[...], jnp.where(i < nl - 1, dn_ref[...], 0.0)], axis=0)
        se = de * _dsilu(_conv_pre(lag, w_ref, b_ref, KA))
        dx_ref[...] = _conv_back(se, w_ref, KA).astype(BF16)

        @pl.when(i == 0)
        def _():
            dw_ref[...] = jnp.zeros_like(dw_ref)
            db_ref[...] = jnp.zeros_like(db_ref)

        sc = se[0:CL]
        for k in range(KA):
            dw_ref[k:k + 1, :] += jnp.sum(sc * lag[KA - 1 - k][0:CL], axis=0, keepdims=True)
        db_ref[...] += jnp.sum(sc, axis=0, keepdims=True)

    prev, cur, nxt = _halo_specs(n, lambda j: j + off)
    _, dcur, dnxt = _halo_specs(n, lambda j: j)
    wspec = pl.BlockSpec((KA, CT), lambda j, i: (0, j + woff))
    bspec = pl.BlockSpec((1, CT), lambda j, i: (0, j + woff))
    return pl.pallas_call(
        body, name=name, grid=(width // CT, nl),
        in_specs=[prev, cur, nxt, dcur, dnxt, wspec, bspec],
        out_specs=[pl.BlockSpec((CL, CT), lambda j, i: (i, j)), pl.BlockSpec((KA, CT), lambda j, i: (0, j)),
                   pl.BlockSpec((1, CT), lambda j, i: (0, j))],
        out_shape=[jax.ShapeDtypeStruct((n, width), BF16), jax.ShapeDtypeStruct((KA, width), F32),
                   jax.ShapeDtypeStruct((1, width), F32)],
        compiler_params=_cp("parallel", "arbitrary"),
    )(proj, proj, proj, dout, dout, w, b)


def _conv_ffn_fwd(up, w, b):
    n = up.shape[0]
    nb = DFF // CT

    def body(pg_ref, g_ref, pv_ref, v_ref, wg_ref, bg_ref, wv_ref, bv_ref, o_ref):
        inner = pl.program_id(1) > 0
        gf = jnp.concatenate([jnp.where(inner, pg_ref[...], 0.0), g_ref[...]], axis=0)
        vf = jnp.concatenate([jnp.where(inner, pv_ref[...], 0.0), v_ref[...]], axis=0)
        gc = _conv_pre(_lagged(gf, KF, CL), wg_ref, bg_ref, KF)
        vc = _conv_pre(_lagged(vf, KF, CL), wv_ref, bv_ref, KF)
        o_ref[...] = (_silu(gc) * vc).astype(BF16)

    gp, gcur, _ = _halo_specs(n, lambda j: j)
    vp, vcur, _ = _halo_specs(n, lambda j: j + nb)
    return pl.pallas_call(
        body, name="conv_ffn_fwd", grid=(nb, n // CL),
        in_specs=[gp, gcur, vp, vcur,
                  pl.BlockSpec((KF, CT), lambda j, i: (0, j)), pl.BlockSpec((1, CT), lambda j, i: (0, j)),
                  pl.BlockSpec((KF, CT), lambda j, i: (0, j + nb)), pl.BlockSpec((1, CT), lambda j, i: (0, j + nb))],
        out_specs=pl.BlockSpec((CL, CT), lambda j, i: (i, j)),
        out_shape=jax.ShapeDtypeStruct((n, DFF), BF16), compiler_params=_cp("parallel", "parallel"),
    )(up, up, up, up, w, b, w, b)


def _conv_ffn_bwd(up, dact, w, b):
    n = up.shape[0]
    nb = DFF // CT
    nl = n // CL

    def body(pg_ref, g_ref, ng_ref, pv_ref, v_ref, nv_ref, d_ref, dn_ref, wg_ref, bg_ref, wv_ref, bv_ref,
             dxg_ref, dxv_ref, dwg_ref, dwv_ref, dbg_ref, dbv_ref):
        i = pl.program_id(1)
        gf = jnp.concatenate([jnp.where(i > 0, pg_ref[...], 0.0), g_ref[...], ng_ref[...]], axis=0)
        vf = jnp.concatenate([jnp.where(i > 0, pv_ref[...], 0.0), v_ref[...], nv_ref[...]], axis=0)
        glag, vlag = _lagged(gf, KF, CL + 8), _lagged(vf, KF, CL + 8)
        de = jnp.concatenate([d_ref[...], jnp.where(i < nl - 1, dn_ref[...], 0.0)], axis=0).astype(F32)
        gc = _conv_pre(glag, wg_ref, bg_ref, KF)
        vc = _conv_pre(vlag, wv_ref, bv_ref, KF)
        sg = _sig(gc)
        dgc = de * vc * (sg * (1.0 + gc * (1.0 - sg)))
        dvc = de * (gc * sg)
        dxg_ref[...] = _conv_back(dgc, wg_ref, KF).astype(BF16)
        dxv_ref[...] = _conv_back(dvc, wv_ref, KF).astype(BF16)

        @pl.when(i == 0)
        def _():
            for r in (dwg_ref, dwv_ref, dbg_ref, dbv_ref):
                r[...] = jnp.zeros_like(r)

        for e, lag, dw_ref, db_ref in ((dgc, glag, dwg_ref, dbg_ref), (dvc, vlag, dwv_ref, dbv_ref)):
            ec = e[0:CL]
            for k in range(KF):
                dw_ref[k:k + 1, :] += jnp.sum(ec * lag[KF - 1 - k][0:CL], axis=0, keepdims=True)
            db_ref[...] += jnp.sum(ec, axis=0, keepdims=True)

    gp, gcur, gnx = _halo_specs(n, lambda j: j)
    vp, vcur, vnx = _halo_specs(n, lambda j: j + nb)
    wcol = lambda o: (pl.BlockSpec((KF, CT), lambda j, i: (0, j + o)), pl.BlockSpec((1, CT), lambda j, i: (0, j + o)))
    wg, bg = wcol(0)
    wv, bv = wcol(nb)
    dxs = pl.BlockSpec((CL, CT), lambda j, i: (i, j))
    outs = pl.pallas_call(
        body, name="conv_ffn_bwd", grid=(nb, nl),
        in_specs=[gp, gcur, gnx, vp, vcur, vnx, gcur, gnx, wg, bg, wv, bv],
        out_specs=[dxs, dxs, wg, wg, bg, bg],
        out_shape=[jax.ShapeDtypeStruct((n, DFF), BF16)] * 2 + [jax.ShapeDtypeStruct((KF, DFF), F32)] * 2
        + [jax.ShapeDtypeStruct((1, DFF), F32)] * 2,
        compiler_params=_cp("parallel", "arbitrary"),
    )(up, up, up, up, up, up, dact, dact, w, b, w, b)
    return [jnp.concatenate(outs[k:k + 2], axis=1) for k in (0, 2, 4)]


HL = 128


def _split3(x):
    hi = x.astype(BF16)
    r1 = x - hi.astype(F32)
    mid = r1.astype(BF16)
    return hi, mid, (r1 - mid.astype(F32)).astype(BF16)


def _dot3(x, m):
    hi, mid, lo = _split3(x)
    return _dot(hi, m) + _dot(mid, m) + _dot(lo, m)


def _tri():
    row = lax.broadcasted_iota(jnp.int32, (Q, Q), 0)
    col = lax.broadcasted_iota(jnp.int32, (Q, Q), 1)
    return row >= col, row <= col


def _ssd_prep(dtraw, hp, emat):
    n = dtraw.shape[0]

    def body(d_ref, hp_ref, e_ref, dt_ref, s_ref, st_ref, dte_ref, se_ref):
        lower, upper = _tri()
        dt = _softplus(d_ref[...] + hp_ref[0:1, :])
        da = dt * (-jnp.exp(hp_ref[1:2, :]))
        s = _dot(lower.astype(F32), da, prec=HI)
        dt_ref[...] = dt
        s_ref[...] = s
        st_ref[...] = _dot(da, upper.astype(F32), TN, prec=HI)
        e = e_ref[...]
        dte_ref[...] = _dot3(dt, e)
        se_ref[...] = _dot3(s, e)

    row = pl.BlockSpec((Q, HL), lambda c: (c, 0))
    wide = pl.BlockSpec((Q, DI), lambda c: (c, 0))
    return pl.pallas_call(
        body, name="ssd_prep", grid=(n // Q,),
        in_specs=[pl.BlockSpec((Q, HL), lambda c: (c, NMAIN // HL)), pl.BlockSpec((8, HL), lambda c: (0, 0)),
                  pl.BlockSpec((HL, DI), lambda c: (0, 0))],
        out_specs=[row, row, pl.BlockSpec((HL, Q), lambda c: (0, c)), wide, wide],
        out_shape=[jax.ShapeDtypeStruct((n, HL), F32)] * 2 + [jax.ShapeDtypeStruct((HL, n), F32)]
        + [jax.ShapeDtypeStruct((n, DI), F32)] * 2,
        compiler_params=_cp("parallel"),
    )(dtraw, hp, emat)


def _ssd_post(ds_e, ddt_e, tsum, dsh, dtraw, dt, hp, emat_t):
    n = dtraw.shape[0]

    def body(dse_ref, dde_ref, ts_ref, dsh_ref, d_ref, dt_ref, hp_ref, et_ref, draw_ref, ps_ref):
        _, upper = _tri()
        et = et_ref[...]
        a = -jnp.exp(hp_ref[1:2, :])
        rows = lax.broadcasted_iota(jnp.int32, (Q, HL), 0)
        ds_t = _dot3(jnp.broadcast_to(ts_ref[...], (8, DI)), et)[0:1, :]
        ds = _dot3(dse_ref[...], et) + dsh_ref[...] + jnp.where(rows == Q - 1, ds_t, 0.0)
        d_a = _dot(upper.astype(F32), ds, prec=HI)
        draw = (_dot3(dde_ref[...], et) + d_a * a) * _sig(d_ref[...] + hp_ref[0:1, :])
        draw_ref[...] = draw

        @pl.when(pl.program_id(0) == 0)
        def _():
            ps_ref[...] = jnp.zeros_like(ps_ref)

        ps_ref[0:1, :] += jnp.sum(draw, axis=0, keepdims=True)
        ps_ref[1:2, :] += jnp.sum(d_a * dt_ref[...], axis=0, keepdims=True) * a

    row = pl.BlockSpec((Q, HL), lambda c: (c, 0))
    wide = pl.BlockSpec((Q, DI), lambda c: (c, 0))
    small = pl.BlockSpec((8, HL), lambda c: (0, 0))
    return pl.pallas_call(
        body, name="ssd_post", grid=(n // Q,),
        in_specs=[wide, wide, pl.BlockSpec((None, 1, DI), lambda c: (c, 0, 0)), row,
                  pl.BlockSpec((Q, HL), lambda c: (c, NMAIN // HL)), row, small,
                  pl.BlockSpec((DI, HL), lambda c: (0, 0))],
        out_specs=[row, small],
        out_shape=[jax.ShapeDtypeStruct((n, HL), F32), jax.ShapeDtypeStruct((8, HL), F32)],
        compiler_params=_cp("arbitrary"),
    )(ds_e, ddt_e, tsum, dsh, dtraw, dt, hp, emat_t)


def _ssd_specs(nc, rev):
    cc = (lambda c: nc - 1 - c) if rev else (lambda c: c)
    return [
        pl.BlockSpec((Q, GW), lambda g, c: (cc(c), g)),
        pl.BlockSpec((Q, NS), lambda g, c: (cc(c), DI // NS + g)),
        pl.BlockSpec((Q, NS), lambda g, c: (cc(c), (DI + NG * NS) // NS + g)),
        pl.BlockSpec((None, Q, 8), lambda g, c: (g, cc(c), 0)),
        pl.BlockSpec((8, Q), lambda g, c: (g, cc(c))),
        pl.BlockSpec((Q, GW), lambda g, c: (cc(c), g)),
        pl.BlockSpec((Q, GW), lambda g, c: (cc(c), g)),
        pl.BlockSpec((1, GW), lambda g, c: (0, g)),
    ]


def _ssd_fwd(xbc, s8, s_t, dt_e, s_e, dexp):
    n = xbc.shape[0]
    nc = n // Q

    def body(xs_ref, b_ref, c_ref, sc_ref, sr_ref, dte_ref, se_ref, dexp_ref, y_ref, sp_ref, st):
        @pl.when(pl.program_id(1) == 0)
        def _():
            st[...] = jnp.zeros_like(st)

        lower, _ = _tri()
        s_c, s_r, dt_e, s_e = sc_ref[...], sr_ref[...], dte_ref[...], se_ref[...]
        xs = xs_ref[...]
        x = xs * dt_e
        xb = x.astype(BF16)
        bb, cb = b_ref[...].astype(BF16), c_ref[...].astype(BF16)
        cbm = _dot(cb, bb, NT)
        st_e = s_e[Q - 1:Q, :]
        sprev = st[...]
        sp_ref[...] = sprev
        yoff = _dot(cb, sprev.astype(BF16)) * jnp.exp(s_e) + dexp_ref[...] * xs
        for h in range(HPG):
            sl = slice(h * HD, (h + 1) * HD)
            lm = jnp.where(lower, jnp.exp(jnp.minimum(s_c[:, h:h + 1] - s_r[h:h + 1, :], 0.0)), 0.0)
            y_ref[:, sl] = _dot((cbm * lm).astype(BF16), xb[:, sl]) + yoff[:, sl]
        w = (x * jnp.exp(st_e - s_e)).astype(BF16)
        st[...] = jnp.exp(st_e) * sprev + _dot(bb, w, TN)

    return pl.pallas_call(
        body, name="ssd_fwd", grid=(NG, nc), in_specs=_ssd_specs(nc, False),
        out_specs=[pl.BlockSpec((Q, GW), lambda g, c: (c, g)),
                   pl.BlockSpec((None, None, NS, GW), lambda g, c: (c, g, 0, 0))],
        out_shape=[jax.ShapeDtypeStruct((n, DI), F32), jax.ShapeDtypeStruct((nc, NG, NS, GW), F32)],
        scratch_shapes=[pltpu.VMEM((NS, GW), F32)],
        compiler_params=_cp("parallel", "arbitrary"),
    )(xbc, xbc, xbc, s8, s_t, dt_e, s_e, dexp)


def _ssd_bwd(xbc, s8, s_t, dt_e, s_e, dexp, sprev_all, dy):
    n = xbc.shape[0]
    nc = n // Q
    rc = lambda c: nc - 1 - c

    def body(xs_ref, b_ref, c_ref, sc_ref, sr_ref, dte_ref, se_ref, dexp_ref, sp_ref, dy_ref,
             dxs_ref, db_ref, dc_ref, dse_ref, dde_ref, ts_ref, dsh_ref, pd_ref, dst, dxbuf):
        @pl.when(pl.program_id(1) == 0)
        def _():
            dst[...] = jnp.zeros_like(dst)
            pd_ref[...] = jnp.zeros_like(pd_ref)

        lower, upper = _tri()
        s_c, s_r, dt_e, s_e = sc_ref[...], sr_ref[...], dte_ref[...], se_ref[...]
        xs = xs_ref[...]
        x = xs * dt_e
        xb = x.astype(BF16)
        bb, cb = b_ref[...].astype(BF16), c_ref[...].astype(BF16)
        cbm = _dot(cb, bb, NT)
        cbt = _dot(bb, cb, NT)
        st_e = s_e[Q - 1:Q, :]
        dec_out, dec_st, e_t = jnp.exp(s_e), jnp.exp(st_e - s_e), jnp.exp(st_e)
        dyv = dy_ref[...]
        dyb = dyv.astype(BF16)
        sprev = sp_ref[...]
        sb = sprev.astype(BF16)
        ds_in = dst[...]
        dsb = ds_in.astype(BF16)

        cs = _dot(cb, sb)
        dcs = (dyv * dec_out).astype(BF16)
        d_c = _dot(dcs, sb, NT)
        wf = x * dec_st
        d_w = _dot(bb, dsb)
        d_b = _dot(wf.astype(BF16), dsb, NT)
        tw = d_w * wf
        dse_ref[...] = dyv * cs * dec_out - tw
        ds_c = jnp.zeros((Q, 8), F32)
        dcb = jnp.zeros((Q, Q), F32)
        dcbt = jnp.zeros((Q, Q), F32)
        lane8 = lax.broadcasted_iota(jnp.int32, (1, 8), 1)
        for h in range(HPG):
            sl = slice(h * HD, (h + 1) * HD)
            sc_h, sr_h = s_c[:, h:h + 1], s_r[h:h + 1, :]
            lm = jnp.where(lower, jnp.exp(jnp.minimum(sc_h - sr_h, 0.0)), 0.0)
            lmt = jnp.where(upper, jnp.exp(jnp.minimum(sr_h - sc_h, 0.0)), 0.0)
            mt = cbt * lmt
            dm = _dot(dyb[:, sl], xb[:, sl], NT)
            dmt = _dot(xb[:, sl], dyb[:, sl], NT)
            dxbuf[:, sl] = _dot(mt.astype(BF16), dyb[:, sl])
            dml = dm * lm
            dmlt = dmt * lmt
            dcb = dcb + dml
            dcbt = dcbt + dmlt
            dsh = jnp.sum(dml * cbm, axis=1, keepdims=True) - jnp.sum(dmlt * cbt, axis=1, keepdims=True)
            ds_c = ds_c + dsh * (lane8 == h).astype(F32)
        d_c = d_c + _dot(dcb.astype(BF16), bb)
        d_b = d_b + _dot(dcbt.astype(BF16), cb)
        dx = d_w * dec_st + dxbuf[...]
        ts_ref[...] = jnp.sum(tw, axis=0, keepdims=True) + jnp.sum(ds_in * sprev, axis=0, keepdims=True) * e_t
        dsh_ref[...] = ds_c
        dde_ref[...] = dx * xs
        pd_ref[...] += jnp.sum(dyv * xs, axis=0, keepdims=True)
        dxs_ref[...] = dx * dt_e + dyv * dexp_ref[...]
        db_ref[...] = d_b
        dc_ref[...] = d_c
        dst[...] = e_t * ds_in + _dot(cb, dcs, TN)

    wide = pl.BlockSpec((Q, GW), lambda g, c: (rc(c), g))
    state = pl.BlockSpec((Q, NS), lambda g, c: (rc(c), g))
    in_specs = _ssd_specs(nc, True) + [pl.BlockSpec((None, None, NS, GW), lambda g, c: (rc(c), g, 0, 0)), wide]
    return pl.pallas_call(
        body, name="ssd_bwd", grid=(NG, nc), in_specs=in_specs,
        out_specs=[wide, state, state, wide, wide,
                   pl.BlockSpec((None, 1, GW), lambda g, c: (rc(c), 0, g)),
                   pl.BlockSpec((None, Q, 8), lambda g, c: (g, rc(c), 0)),
                   pl.BlockSpec((None, 1, GW), lambda g, c: (g, 0, 0))],
        out_shape=[jax.ShapeDtypeStruct((n, DI), F32), jax.ShapeDtypeStruct((n, NG * NS), F32),
                   jax.ShapeDtypeStruct((n, NG * NS), F32), jax.ShapeDtypeStruct((n, DI), F32),
                   jax.ShapeDtypeStruct((n, DI), F32), jax.ShapeDtypeStruct((nc, 1, DI), F32),
                   jax.ShapeDtypeStruct((NG, n, 8), F32), jax.ShapeDtypeStruct((NG, 1, GW), F32)],
        scratch_shapes=[pltpu.VMEM((NS, GW), F32), pltpu.VMEM((Q, GW), F32)],
        compiler_params=_cp("parallel", "arbitrary"),
    )(xbc, xbc, xbc, s8, s_t, dt_e, s_e, dexp, sprev_all, dy)


GL = 128


def _gnorm_fwd(y, proj, w):
    n = y.shape[0]

    def body(y_ref, z_ref, w_ref, o_ref):
        for g in range(NG):
            sl = slice(g * GW, (g + 1) * GW)
            yz = y_ref[:, sl] * _silu(z_ref[:, sl])
            r = lax.rsqrt(jnp.mean(yz * yz, axis=-1, keepdims=True) + EPS)
            o_ref[:, sl] = (yz * r * w_ref[:, sl]).astype(BF16)

    row = pl.BlockSpec((GL, DI), lambda i: (i, 0))
    return pl.pallas_call(
        body, name="gnorm_fwd", grid=(n // GL,),
        in_specs=[row, row, pl.BlockSpec((1, DI), lambda i: (0, 0))], out_specs=row,
        out_shape=jax.ShapeDtypeStruct((n, DI), BF16), compiler_params=_cp("parallel"),
    )(y, proj, w)


def _gnorm_bwd(dyn, y, proj, w, after):
    n = y.shape[0]

    def body(d_ref, y_ref, z_ref, w_ref, _, dy_ref, dz_ref, gw_ref):
        @pl.when(pl.program_id(0) == 0)
        def _():
            gw_ref[...] = jnp.zeros_like(gw_ref)

        for g in range(NG):
            sl = slice(g * GW, (g + 1) * GW)
            yv, zv, dv = y_ref[:, sl], z_ref[:, sl], d_ref[:, sl]
            sz = _silu(zv)
            yz = yv * sz
            r = lax.rsqrt(jnp.mean(yz * yz, axis=-1, keepdims=True) + EPS)
            yh = yz * r
            gg = dv * w_ref[:, sl]
            dyz = r * (gg - yh * jnp.mean(gg * yh, axis=-1, keepdims=True))
            gw_ref[:, sl] += jnp.sum(dv * yh, axis=0, keepdims=True)
            dy_ref[:, sl] = dyz * sz
            dz_ref[:, sl] = (dyz * yv * _dsilu(zv)).astype(BF16)

    row = pl.BlockSpec((GL, DI), lambda i: (i, 0))
    vec = pl.BlockSpec((1, DI), lambda i: (0, 0))
    return pl.pallas_call(
        body, name="gnorm_bwd", grid=(n // GL,),
        in_specs=[row, row, row, vec, ANY], out_specs=[row, row, vec],
        out_shape=[jax.ShapeDtypeStruct((n, DI), F32), jax.ShapeDtypeStruct((n, DI), BF16),
                   jax.ShapeDtypeStruct((1, DI), F32)],
        compiler_params=_cp("arbitrary"),
    )(dyn, y, proj, w, after)


SL = 1024
SB = 8
SCB = NCH // SB


def _s5_in(proj, bre, bim, after=None):
    n = proj.shape[0]
    uoff = OFF_U // 128

    def body(u_ref, br_ref, bi_ref, *rest):
        or_ref, oi_ref = rest[-2:]
        u = u_ref[...].astype(BF16)
        or_ref[...] = _dot(u, br_ref[...])
        oi_ref[...] = _dot(u, bi_ref[...])

    extra = [] if after is None else [after]
    blk = pl.BlockSpec((None, 128, SCB), lambda i, j: (j, 0, 0))
    out = pl.BlockSpec((SL, SCB), lambda i, j: (i, j))
    return pl.pallas_call(
        body, name="s5_in", grid=(n // SL, SB),
        in_specs=[pl.BlockSpec((SL, 128), lambda i, j: (i, uoff + j)), blk, blk] + [ANY] * len(extra),
        out_specs=[out, out],
        out_shape=[jax.ShapeDtypeStruct((n, NCH), F32)] * 2, compiler_params=_cp("parallel", "parallel"),
    )(proj, bre, bim, *extra)


SC = 1024


def _s5_scan(vre, vim, tab, reverse, name):
    n = vre.shape[0]
    nl = n // SL
    ng = SL // 8
    ti = (lambda i: nl - 1 - i) if reverse else (lambda i: i)

    def body(re_ref, im_ref, tab_ref, ore_ref, oim_ref, cre, cim):
        @pl.when(pl.program_id(1) == 0)
        def _():
            cre[...] = jnp.zeros_like(cre)
            cim[...] = jnp.zeros_like(cim)

        def step(j, carry):
            cr, ci = carry
            jj = (ng - 1 - j) if reverse else j
            rows = pl.ds(pl.multiple_of(jj * 8, 8), 8)
            vr, vi = re_ref[rows, :], im_ref[rows, :]
            for t, k in enumerate((1, 2, 4)):
                sh = (8 - k) if reverse else k
                rr, ri = pltpu.roll(vr, sh, 0), pltpu.roll(vi, sh, 0)
                pr, pi = tab_ref[2 * t], tab_ref[2 * t + 1]
                vr, vi = vr + pr * rr - pi * ri, vi + pr * ri + pi * rr
            lr, li = tab_ref[6], tab_ref[7]
            vr, vi = vr + lr * cr - li * ci, vi + lr * ci + li * cr
            ore_ref[rows, :] = vr
            oim_ref[rows, :] = vi
            e = 0 if reverse else 7
            return (jnp.broadcast_to(vr[e:e + 1, :], (8, SC)), jnp.broadcast_to(vi[e:e + 1, :], (8, SC)))

        cr, ci = lax.fori_loop(0, ng, step, (cre[...], cim[...]))
        cre[...] = cr
        cim[...] = ci

    blk = pl.BlockSpec((SL, SC), lambda j, i: (ti(i), j))
    return pl.pallas_call(
        body, name=name, grid=(NCH // SC, nl),
        in_specs=[blk, blk, pl.BlockSpec((8, 8, SC), lambda j, i: (0, 0, j))], out_specs=[blk, blk],
        out_shape=[jax.ShapeDtypeStruct((n, NCH), F32)] * 2,
        scratch_shapes=[pltpu.VMEM((8, SC), F32), pltpu.VMEM((8, SC), F32)],
        compiler_params=_cp("parallel", "arbitrary"),
    )(vre, vim, tab)


def _s5_out(xre, xim, cre, cimn, proj, dvec):
    n = xre.shape[0]
    uoff = OFF_U // 128

    def body(xr_ref, xi_ref, cr_ref, ci_ref, u_ref, d_ref, y_ref, g_ref):
        y = (_dot(xr_ref[...].astype(BF16), cr_ref[...]) + _dot(xi_ref[...].astype(BF16), ci_ref[...])
             + d_ref[...] * u_ref[...])
        y_ref[...] = y
        g_ref[...] = _gelu(y).astype(BF16)

    xs = pl.BlockSpec((SL, SCB), lambda i, j: (i, j))
    blk = pl.BlockSpec((None, SCB, 128), lambda i, j: (j, 0, 0))
    out = pl.BlockSpec((SL, 128), lambda i, j: (i, j))
    return pl.pallas_call(
        body, name="s5_out", grid=(n // SL, SB),
        in_specs=[xs, xs, blk, blk, pl.BlockSpec((SL, 128), lambda i, j: (i, uoff + j)),
                  pl.BlockSpec((1, 128), lambda i, j: (0, j))],
        out_specs=[out, out],
        out_shape=[jax.ShapeDtypeStruct((n, DS5), F32), jax.ShapeDtypeStruct((n, DS5), BF16)],
        compiler_params=_cp("parallel", "parallel"),
    )(xre, xim, cre, cimn, proj, dvec)


def _s5_out_bwd(dg, ypre, crt, cimnt, proj, dvec, xre, xim):
    n = dg.shape[0]
    uoff = OFF_U // 128
    nl = n // SL

    def body(dg_ref, y_ref, cr_ref, ci_ref, u_ref, d_ref, xr_ref, xi_ref,
             gr_ref, gi_ref, dus_ref, gcr_ref, gci_ref, gd_ref):
        dy = dg_ref[...] * _dgelu(y_ref[...])
        dyb = dy.astype(BF16)
        gr_ref[...] = _dot(dyb, cr_ref[...])
        gi_ref[...] = _dot(dyb, ci_ref[...])
        dus_ref[...] = dy * d_ref[...]

        @pl.when(pl.program_id(1) == 0)
        def _():
            gcr_ref[...] = jnp.zeros_like(gcr_ref)
            gci_ref[...] = jnp.zeros_like(gci_ref)
            gd_ref[...] = jnp.zeros_like(gd_ref)

        gcr_ref[...] += _dot(xr_ref[...].astype(BF16), dyb, TN)
        gci_ref[...] -= _dot(xi_ref[...].astype(BF16), dyb, TN)
        gd_ref[...] += jnp.sum(dy * u_ref[...], axis=0, keepdims=True)

    u128 = pl.BlockSpec((SL, 128), lambda j, i: (i, j))
    xs = pl.BlockSpec((SL, SCB), lambda j, i: (i, j))
    blk = pl.BlockSpec((None, 128, SCB), lambda j, i: (j, 0, 0))
    gblk = pl.BlockSpec((None, SCB, 128), lambda j, i: (j, 0, 0))
    vec = pl.BlockSpec((1, 128), lambda j, i: (0, j))
    return pl.pallas_call(
        body, name="s5_out_bwd", grid=(SB, nl),
        in_specs=[u128, u128, blk, blk, pl.BlockSpec((SL, 128), lambda j, i: (i, uoff + j)), vec, xs, xs],
        out_specs=[xs, xs, u128, gblk, gblk, vec],
        out_shape=[jax.ShapeDtypeStruct((n, NCH), F32)] * 2 + [jax.ShapeDtypeStruct((n, DS5), F32)]
        + [jax.ShapeDtypeStruct((SB, SCB, 128), F32)] * 2 + [jax.ShapeDtypeStruct((1, DS5), F32)],
        compiler_params=_cp("parallel", "arbitrary"),
    )(dg, ypre, crt, cimnt, proj, dvec, xre, xim)


def _s5_in_bwd(are, aim, brt, bit, proj, dus, xre, xim):
    n = are.shape[0]
    uoff = OFF_U // 128
    per = SL // 8

    def body(ar_ref, ai_ref, br_ref, bi_ref, u_ref, dus_ref, xr_ref, xi_ref, pr_ref, pi_ref,
             du_ref, gbr_ref, gbi_ref, glr_ref, gli_ref):
        i = pl.program_id(1)
        ar, ai = ar_ref[...], ai_ref[...]
        arb, aib = ar.astype(BF16), ai.astype(BF16)
        du_ref[...] = (_dot(arb, br_ref[...]) + _dot(aib, bi_ref[...]) + dus_ref[...]).astype(BF16)

        @pl.when(i == 0)
        def _():
            for r in (gbr_ref, gbi_ref, glr_ref, gli_ref):
                r[...] = jnp.zeros_like(r)

        ub = u_ref[...].astype(BF16)
        gbr_ref[...] += _dot(arb, ub, TN)
        gbi_ref[...] += _dot(aib, ub, TN)
        row0 = lax.broadcasted_iota(jnp.int32, (SL, SCB), 0) == 0
        last_r = jnp.where(i > 0, pr_ref[7:8, :], 0.0)
        last_i = jnp.where(i > 0, pi_ref[7:8, :], 0.0)
        xpr = jnp.where(row0, last_r, pltpu.roll(xr_ref[...], 1, 0))
        xpi = jnp.where(row0, last_i, pltpu.roll(xi_ref[...], 1, 0))
        glr_ref[...] += jnp.sum(ar * xpr + ai * xpi, axis=0, keepdims=True)
        gli_ref[...] += jnp.sum(ai * xpr - ar * xpi, axis=0, keepdims=True)

    xs = pl.BlockSpec((SL, SCB), lambda j, i: (i, j))
    prev = pl.BlockSpec((8, SCB), lambda j, i: (jnp.maximum(i * per - 1, 0), j))
    blk = pl.BlockSpec((None, SCB, 128), lambda j, i: (j, 0, 0))
    u128 = pl.BlockSpec((SL, 128), lambda j, i: (i, j))
    vec = pl.BlockSpec((1, SCB), lambda j, i: (0, j))
    return pl.pallas_call(
        body, name="s5_in_bwd", grid=(SB, n // SL),
        in_specs=[xs, xs, blk, blk, pl.BlockSpec((SL, 128), lambda j, i: (i, uoff + j)), u128, xs, xs, prev, prev],
        out_specs=[u128, blk, blk, vec, vec],
        out_shape=[jax.ShapeDtypeStruct((n, DS5), BF16)] + [jax.ShapeDtypeStruct((SB, SCB, 128), F32)] * 2
        + [jax.ShapeDtypeStruct((1, NCH), F32)] * 2,
        compiler_params=_cp("parallel", "arbitrary"),
    )(are, aim, brt, bit, proj, dus, xre, xim, xre, xim)


MC = 1024


def _merge_specs():
    ga = pl.BlockSpec((TL, MC), lambda i, j: (i, OFF_GA // MC + j))
    gb = pl.BlockSpec((TL, MC), lambda i, j: (i, OFF_GB // MC + j))
    col = pl.BlockSpec((TL, MC), lambda i, j: (i, j))
    gate = pl.BlockSpec((TL, MC), lambda i, j: (i, D // MC + j))
    return ga, gb, col, gate


def _merge_fwd(proj, ya, vg):
    n = ya.shape[0]

    def body(ga_ref, gb_ref, ya_ref, v_ref, g_ref, o_ref):
        yb = v_ref[...] * _sig(g_ref[...])
        o_ref[...] = (_sig(ga_ref[...]) * ya_ref[...] + _sig(gb_ref[...]) * yb).astype(BF16)

    ga, gb, col, gate = _merge_specs()
    return pl.pallas_call(
        body, name="merge_fwd", grid=(n // TL, D // MC), in_specs=[ga, gb, col, col, gate], out_specs=col,
        out_shape=jax.ShapeDtypeStruct((n, D), BF16), compiler_params=_cp("parallel", "parallel"),
    )(proj, proj, ya, vg, vg)


def _merge_bwd(dm, proj, ya, vg):
    n = ya.shape[0]

    def body(dm_ref, ga_ref, gb_ref, ya_ref, v_ref, g_ref, dga_ref, dgb_ref, dya_ref, dv_ref, dg_ref):
        d = dm_ref[...]
        sa, sb, sg = _sig(ga_ref[...]), _sig(gb_ref[...]), _sig(g_ref[...])
        v = v_ref[...]
        yb = v * sg
        dga_ref[...] = (d * ya_ref[...] * sa * (1.0 - sa)).astype(BF16)
        dgb_ref[...] = (d * yb * sb * (1.0 - sb)).astype(BF16)
        dya_ref[...] = (d * sa).astype(BF16)
        dyb = d * sb
        dv_ref[...] = (dyb * sg).astype(BF16)
        dg_ref[...] = (dyb * v * sg * (1.0 - sg)).astype(BF16)

    ga, gb, col, gate = _merge_specs()
    o = jax.ShapeDtypeStruct((n, D), BF16)
    return pl.pallas_call(
        body, name="merge_bwd", grid=(n // TL, D // MC), in_specs=[col, ga, gb, col, col, gate],
        out_specs=[col] * 5, out_shape=[o] * 5, compiler_params=_cp("parallel", "parallel"),
    )(dm, proj, proj, ya, vg, vg)


def _adamw_update(wv, gv, mv, vv):
    nm = B1 * mv + (1.0 - B1) * gv
    nv = B2 * vv + (1.0 - B2) * (gv * gv)
    m_hat = nm / (1.0 - B1 ** STEP)
    v_hat = nv / (1.0 - B2 ** STEP)
    return -LR * (m_hat / (jnp.sqrt(v_hat) + AEPS) + WD * wv), nm, nv


def _adamw(w, g, m, v, name):
    r, c = w.shape
    tr = _pick(r, 128)

    def body(w_ref, g_ref, m_ref, v_ref, d_ref, nm_ref, nv_ref):
        d_ref[...], nm_ref[...], nv_ref[...] = _adamw_update(w_ref[...], g_ref[...], m_ref[...], v_ref[...])

    blk = pl.BlockSpec((tr, c), lambda i: (i, 0))
    o = jax.ShapeDtypeStruct((r, c), F32)
    return pl.pallas_call(
        body, name=name, grid=(r // tr,), in_specs=[blk] * 4, out_specs=[blk] * 3, out_shape=[o] * 3,
        compiler_params=_cp("parallel"),
    )(w, g, m, v)


def _adamw_halves(w, g_mine, g_other, m, v, cidx, name):
    _, r, c = w.shape
    hr, gc = g_mine.shape
    tr = _pick(hr, 128)
    nbh = hr // tr
    assert gc == c and 2 * hr - tr < r <= 2 * hr

    def body(cs, w_ref, gm_ref, go_ref, m_ref, v_ref, g_ref, d_ref, nm_ref, nv_ref):
        mine = pl.program_id(0) // nbh == cs[0]
        gv = jnp.where(mine, gm_ref[...], go_ref[...])
        g_ref[...] = gv
        d_ref[...], nm_ref[...], nv_ref[...] = _adamw_update(w_ref[...], gv, m_ref[...], v_ref[...])

    blk = pl.BlockSpec((None, tr, c), lambda i, cs: (0, i, 0))
    gmine = pl.BlockSpec((tr, gc), lambda i, cs: (jnp.where(i // nbh == cs[0], i % nbh, 0), 0))
    gother = pl.BlockSpec((tr, gc), lambda i, cs: (jnp.where(i // nbh == cs[0], 0, i % nbh), 0))
    o = jax.ShapeDtypeStruct((1, r, c), F32)
    return pl.pallas_call(
        body, name=name,
        grid_spec=pltpu.PrefetchScalarGridSpec(num_scalar_prefetch=1, grid=(2 * nbh,),
                                               in_specs=[blk, gmine, gother, blk, blk], out_specs=[blk] * 4),
        out_shape=[o] * 4, compiler_params=_cp("parallel"),
    )(cidx, w, g_mine, g_other, m, v)


def _chip_sum(part, sib, cidx, name):
    _, r, cc = part.shape
    hr = r // 2
    tr = _pick(hr, 256)

    def body(cs, p_ref, s_ref, o_ref):
        o_ref[...] = (p_ref[...].astype(F32) + s_ref[...].astype(F32)).astype(BF16)

    blk = pl.BlockSpec((None, tr, cc), lambda k, i, cs: (k, i, 0))
    return pl.pallas_call(
        body, name=name,
        grid_spec=pltpu.PrefetchScalarGridSpec(
            num_scalar_prefetch=1, grid=(4, hr // tr),
            in_specs=[pl.BlockSpec((None, None, tr, cc), lambda k, i, cs: (k, cs[0], i, 0)), blk], out_specs=blk),
        out_shape=jax.ShapeDtypeStruct((4, hr, cc), BF16), compiler_params=_cp("parallel", "parallel"),
    )(cidx, part.reshape(4, 2, hr, cc), sib)


def _shard_sum(own, got, sidx, name):
    _, hr, cc = own.shape
    tr = _pick(hr, 256)

    def body(cs, own_ref, g0, g1, g2, g3, o_ref):
        acc = None
        for k, g_ref in enumerate((g0, g1, g2, g3)):
            term = jnp.where(cs[0] == k, own_ref[...], g_ref[...]).astype(F32)
            acc = term if acc is None else acc + term
        o_ref[...] = acc

    def got_spec(k):
        return pl.BlockSpec((None, tr, cc), lambda i, cs: (jnp.where(cs[0] == k, (k + 1) % 4, k), i, 0))

    return pl.pallas_call(
        body, name=name,
        grid_spec=pltpu.PrefetchScalarGridSpec(
            num_scalar_prefetch=1, grid=(hr // tr,),
            in_specs=[pl.BlockSpec((None, tr, cc), lambda i, cs: (cs[0], i, 0))] + [got_spec(k) for k in range(4)],
            out_specs=pl.BlockSpec((tr, cc), lambda i, cs: (i, 0))),
        out_shape=jax.ShapeDtypeStruct((hr, cc), F32), compiler_params=_cp("parallel"),
    )(sidx, own, got, got, got, got)


def _sum_slabs(xs, name, out_dtype=F32):
    r, c = xs[0].shape
    tr = _pick(r, 256)

    def body(*refs):
        acc = refs[0][...].astype(F32)
        for ref in refs[1:-1]:
            acc = acc + ref[...].astype(F32)
        refs[-1][...] = acc.astype(out_dtype)

    blk = pl.BlockSpec((tr, c), lambda i: (i, 0))
    return pl.pallas_call(
        body, name=name, grid=(r // tr,), in_specs=[blk] * len(xs), out_specs=blk,
        out_shape=jax.ShapeDtypeStruct((r, c), out_dtype), compiler_params=_cp("parallel"),
    )(*xs)


def _place():
    return lax.axis_index("x"), lax.axis_index("y"), lax.axis_index("c")


def _gather_small(v, after):
    m_per, n = v.shape

    def body(x_ref, _, out_ref, send_sems, recv_sems, local_sem):
        x, y, c = _place()
        me, sibling = (x, y, c), (x, y, 1 - c)
        chips = [(1 - x, y), (x, 1 - y), (1 - x, 1 - y)]

        def rows(px, py, pc):
            return out_ref.at[pl.ds((4 * px + 2 * py + pc) * m_per, m_per), :]

        def copy(k, block, to, src=None):
            return pltpu.make_async_remote_copy(
                src_ref=rows(*block) if src is None else src, dst_ref=rows(*block),
                send_sem=send_sems.at[k], recv_sem=recv_sems.at[k], device_id=to, device_id_type=MESH)

        mine = pltpu.make_async_copy(x_ref, rows(*me), local_sem)
        mine.start()
        first = [copy(0, me, sibling, src=x_ref)]
        first += [copy(1 + j, me, (*chip, c), src=x_ref) for j, chip in enumerate(chips)]
        for cp in first:
            cp.start()
        passed = [copy(4 + j, (*chip, c), sibling) for j, chip in enumerate(chips)]
        for j, chip in enumerate(chips):
            copy(1 + j, (*chip, c), me).wait_recv()
            passed[j].start()
        copy(0, sibling, me).wait_recv()
        for j, chip in enumerate(chips):
            copy(4 + j, (*chip, 1 - c), me).wait_recv()
        for cp in first + passed:
            cp.wait_send()
        mine.wait()

    return pl.pallas_call(
        body, name="gather_small_%d" % m_per,
        out_shape=jax.ShapeDtypeStruct((8 * m_per, n), v.dtype),
        in_specs=[pl.BlockSpec(memory_space=pltpu.VMEM), ANY], out_specs=pl.BlockSpec(memory_space=pltpu.VMEM),
        scratch_shapes=[pltpu.SemaphoreType.DMA((7,)), pltpu.SemaphoreType.DMA((7,)), pltpu.SemaphoreType.DMA],
        compiler_params=pltpu.CompilerParams(vmem_limit_bytes=VMEM_LIMIT),
    )(v, after)


def _allsum_small(v, name, after):
    r = v.shape[0]
    g = _gather_small(v, after)
    return _sum_slabs([g[k * r:(k + 1) * r] for k in range(8)], name)


def _pass_halves(got, shards, name):
    nt = len(got)

    def body(*refs):
        ins, own, outs = refs[:nt], refs[nt:2 * nt], refs[2 * nt:3 * nt]
        send_sems, recv_sems = refs[3 * nt:]
        x, y, c = _place()
        s = 2 * x + y
        chips = [(1 - x, y), (x, 1 - y), (1 - x, 1 - y)]

        def half(ref, t, slot, h):
            hr = ins[t].shape[1] // 2
            return ref.at[slot, pl.ds(h * hr, hr), :]

        def copy(t, j, h):
            px, py = chips[j]
            return pltpu.make_async_remote_copy(
                src_ref=half(ins[t], t, 2 * px + py, h), dst_ref=half(outs[t], t, 2 * px + py, h),
                send_sem=send_sems.at[4 * t + j], recv_sem=recv_sems.at[4 * t + j],
                device_id=(x, y, 1 - c), device_id_type=MESH)

        def whole(t):
            return pltpu.make_async_remote_copy(
                src_ref=own[t], dst_ref=outs[t].at[s], send_sem=send_sems.at[4 * t + 3],
                recv_sem=recv_sems.at[4 * t + 3], device_id=(x, y, 1 - c), device_id_type=MESH)

        sends = [copy(t, j, c) for t in range(nt) for j in range(3)] + [whole(t) for t in range(nt)]
        for cp in sends:
            cp.start()
        for t in range(nt):
            for j in range(3):
                copy(t, j, 1 - c).wait_recv()
            whole(t).wait_recv()
        for cp in sends:
            cp.wait_send()

    return pl.pallas_call(
        body, name=name,
        out_shape=[jax.ShapeDtypeStruct(a.shape, a.dtype) for a in got],
        in_specs=[ANY] * (2 * nt), out_specs=[ANY] * nt, input_output_aliases={t: t for t in range(nt)},
        scratch_shapes=[pltpu.SemaphoreType.DMA((4 * nt,)), pltpu.SemaphoreType.DMA((4 * nt,))],
    )(*got, *shards)


HBM = pl.BlockSpec(memory_space=pltpu.HBM)
SEM = pl.BlockSpec(memory_space=pltpu.SEMAPHORE)
EFFECT = pltpu.SideEffectType.DATAFLOW_SIDE_EFFECTING


PER_TENSOR = {"gather": 3, "scatter": 3, "swap": 1, "pass": 4, "whole": 1}


def _ici_copies(kind, srcs, lands, send_sems, recv_sems):
    x, y, c = _place()
    s = 2 * x + y
    sib = (x, y, 1 - c)
    chips = [(1 - x, y), (x, 1 - y), (1 - x, 1 - y)]
    cps = []

    def add(src, dst, dev):
        k = len(cps)
        cps.append(pltpu.make_async_remote_copy(src_ref=src, dst_ref=dst, send_sem=send_sems[k], recv_sem=recv_sems[k],
                                                device_id=dev, device_id_type=MESH))

    for t in range(len(srcs)):
        if kind == "gather":
            hr = srcs[t].shape[0] // 2
            for px, py in chips:
                add(srcs[t].at[pl.ds(c * hr, hr), :], lands[t].at[s, pl.ds(c * hr, hr), :], (px, py, c))
        elif kind == "scatter":
            for px, py in chips:
                add(srcs[t].at[2 * px + py], lands[t].at[s], (px, py, c))
        elif kind == "swap":
            hr = srcs[t].shape[1] // 2
            add(srcs[t].at[:, pl.ds((1 - c) * hr, hr), :], lands[t], sib)
        elif kind == "pass":
            hr = srcs[t].shape[1] // 2
            for px, py in chips:
                half = srcs[t].at[2 * px + py, pl.ds(c * hr, hr), :]
                add(half, half, sib)
            add(lands[t], srcs[t].at[s], sib)
        else:
            add(srcs[t], lands[t], sib)
    return cps


def _ici_start(kind, srcs, after, name, lands=None):
    nt = len(srcs)
    nc = PER_TENSOR[kind] * nt
    hbm = lambda a: pltpu.with_memory_space_constraint(a, pltpu.HBM)
    if lands is None:
        shape = {"gather": lambda a: (4,) + a.shape, "scatter": lambda a: a.shape,
                 "swap": lambda a: (4, a.shape[1] // 2, a.shape[2]), "whole": lambda a: a.shape}[kind]
        lands = [lax.empty(shape(a), a.dtype) for a in srcs]

    def body(*refs):
        src, land = refs[:nt], refs[nt:2 * nt]
        outs = refs[2 * nt + 1:]
        for cp in _ici_copies(kind, src, land, outs[:nc], outs[nc:2 * nc]):
            cp.start()
        outs[-1][...] = jnp.zeros_like(outs[-1])

    outs = pl.pallas_call(
        body, name=name,
        out_shape=tuple([pltpu.SemaphoreType.DMA(())] * (2 * nc) + [pltpu.HBM(a.shape, a.dtype) for a in srcs]
                        + [pltpu.HBM(a.shape, a.dtype) for a in lands] + [jax.ShapeDtypeStruct((8, 128), F32)]),
        in_specs=[HBM] * (2 * nt) + [ANY],
        out_specs=tuple([SEM] * (2 * nc) + [HBM] * (2 * nt) + [pl.BlockSpec(memory_space=pltpu.VMEM)]),
        input_output_aliases={i: 2 * nc + i for i in range(2 * nt)},
        compiler_params=pltpu.CompilerParams(has_side_effects=EFFECT),
    )(*[hbm(a) for a in srcs], *[hbm(a) for a in lands], after)
    return outs[:2 * nc], outs[2 * nc:2 * nc + nt], outs[2 * nc + nt:2 * nc + 2 * nt], outs[-1]


def _ici_wait(kind, sems, srcs, lands, after, name):
    nt = len(srcs)
    nc = PER_TENSOR[kind] * nt

    def body(*refs):
        src, land = refs[:nt], refs[nt:2 * nt]
        sem = refs[2 * nt:2 * nt + 2 * nc]
        for cp in _ici_copies(kind, src, land, sem[:nc], sem[nc:]):
            cp.wait_send()
            cp.wait_recv()

    outs = pl.pallas_call(
        body, name=name,
        out_shape=tuple(pltpu.HBM(a.shape, a.dtype) for a in list(srcs) + list(lands)),
        in_specs=[HBM] * (2 * nt) + [SEM] * (2 * nc) + [ANY], out_specs=tuple([HBM] * (2 * nt)),
        input_output_aliases={i: i for i in range(2 * nt)},
        compiler_params=pltpu.CompilerParams(has_side_effects=EFFECT),
    )(*srcs, *lands, *sems, after)
    return outs[:nt], outs[nt:]


def _s5_params(lam_re, lam_im, log_dt, b_re, b_im):
    lr = jnp.minimum(lam_re, EIG_MAX)
    dt = jnp.exp(log_dt)[:, None]
    mag = jnp.exp(lr * dt)
    lbr, lbi = mag * jnp.cos(lam_im * dt), mag * jnp.sin(lam_im * dt)
    den = lr * lr + lam_im * lam_im
    qr = ((lbr - 1.0) * lr + lbi * lam_im) / den
    qi = (lbi * lr - (lbr - 1.0) * lam_im) / den
    bbr = qr[..., None] * b_re - qi[..., None] * b_im
    bbi = qr[..., None] * b_im + qi[..., None] * b_re
    return lbr, lbi, bbr, bbi


def _cmul(a, b):
    return a[0] * b[0] - a[1] * b[1], a[0] * b[1] + a[1] * b[0]


def _scan_table(lr, li, reverse):
    l1 = (lr.reshape(1, NCH), li.reshape(1, NCH))
    pows = [l1]
    for _ in range(7):
        pows.append(_cmul(pows[-1], l1))
    r = jnp.arange(8)[:, None]
    tabs = []
    for k in (1, 2, 4):
        keep = (r < 8 - k) if reverse else (r >= k)
        tabs += [jnp.where(keep, pows[k - 1][0], 0.0), jnp.where(keep, pows[k - 1][1], 0.0)]
    order = range(7, -1, -1) if reverse else range(8)
    tabs += [jnp.concatenate([pows[e][0] for e in order], axis=0), jnp.concatenate([pows[e][1] for e in order], axis=0)]
    return jnp.stack(tabs).astype(F32)


_EYE8 = lambda: jnp.eye(8, dtype=F32)


def _to_in_blocks(b):
    return jnp.einsum("jgpc,gh->jgchp", b.reshape(8, 8, 64, 16), _EYE8()).reshape(8, 128, 512)


def _to_out_blocks(cm):
    return jnp.einsum("jgcp,gh->jgphc", cm.reshape(8, 8, 16, 64), _EYE8()).reshape(8, 512, 128)


def _from_out_blocks(g):
    return jnp.einsum("jgphc,gh->jgpc", g.reshape(8, 8, 64, 8, 16), _EYE8()).reshape(64, 64, 16)


def _local_step(x, target, hn1, proj, p, hooks):
    n = x.shape[0]
    g = {}
    dtraw = proj
    xbc = _conv_a_fwd(proj, p["conv_a_w"], p["conv_a_b"])
    to_lanes = lambda v: jnp.pad(jnp.pad(v.reshape(NG, HPG), ((0, 0), (0, 8 - HPG))).reshape(1, 8 * NG),
                                 ((0, 0), (0, HL - 8 * NG)))
    from_lanes = lambda v: v[:, :8 * NG].reshape(-1, NG, 8)[:, :, :HPG].reshape(-1, NG * HPG)
    hp = jnp.concatenate([to_lanes(p["dt_bias"]), to_lanes(p["a_log"]), jnp.zeros((6, HL), F32)], axis=0)
    lane = jnp.arange(HL)[:, None]
    emat = ((lane < 8 * NG) & (lane % 8 < HPG)
            & (jnp.arange(DI)[None, :] // HD == HPG * (lane // 8) + lane % 8)).astype(BF16)
    dexp = jnp.repeat(p["d_a"].reshape(1, NG * HPG), HD, axis=1)
    dt, s_cum, s_t, dt_e, s_e = _ssd_prep(dtraw, hp, emat)
    s8 = s_cum[:, :8 * NG].reshape(n, NG, 8).transpose(1, 0, 2)
    yssd, sprev = _ssd_fwd(xbc, s8, s_t, dt_e, s_e, dexp)
    yn = _gnorm_fwd(yssd, proj, p["norm_a_w"])
    tok = hooks["late_start"](yn)
    (lbr, lbi, bbr, bbi), s5_vjp = jax.vjp(_s5_params, p["s5_lam_re"], p["s5_lam_im"], p["s5_log_dt"],
                                           p["s5_b_re"], p["s5_b_im"])
    bin_r, bin_i = _to_in_blocks(bbr), _to_in_blocks(bbi)
    cout_r, cout_in = _to_out_blocks(p["s5_c_re"]), _to_out_blocks(-p["s5_c_im"])
    bur, bui = _s5_in(proj, bin_r.astype(BF16), bin_i.astype(BF16), after=tok)
    xre, xim = _s5_scan(bur, bui, _scan_table(lbr, lbi, False), False, "s5_scan_fwd")
    ypre, g5 = _s5_out(xre, xim, cout_r.astype(BF16), cout_in.astype(BF16), proj, p["s5_d"])
    p = {**p, **hooks["late_weights"](ypre)}
    ya = _matmul(yn, p["w_proj_a"], "nn", "mm_proj")
    vg = _matmul(g5, p["w_s5_glu"], "nn", "mm_glu", b_stacked=True)
    merged = _merge_fwd(proj, ya, vg)
    h1 = _matmul(merged, p["w_out"], "nn", "mm_out", residual=x)
    hn2 = _rms_fwd(h1, p["norm_ffn_w"], "rms_ffn")
    up = _matmul(hn2, p["w_up"], "nn", "mm_up", tn=1408, b_stacked=True)
    act = _conv_ffn_fwd(up, p["conv_ffn_w"], p["conv_ffn_b"])
    h2 = _matmul(act, p["w_down"], "nn", "mm_down", tk=DFF // 2, residual=h1)
    dh2, dh2b, g["norm_final_w"], loss_blk = _final(h2, p["norm_final_w"], target)
    g["w_down"] = _matmul(act, dh2b, "tn", "mm_gw_down", out_dtype=BF16, tm=DFF // 4).reshape(4, DFF // 4, D)
    dact = _matmul(dh2b, p["w_down"], "nt", "mm_dact", out_dtype=BF16, tn=DFF // 4)
    dup, g["conv_ffn_w"], g["conv_ffn_b"] = _conv_ffn_bwd(up, dact, p["conv_ffn_w"], p["conv_ffn_b"])
    g["w_up"] = _matmul(hn2, dup, "tn", "mm_gw_up", out_dtype=BF16, tn=1408, out_stacked=True)
    tok = hooks["swap_start"](["w_up", "w_down"], g, "s1")
    dhn2 = _matmul(dup, p["w_up"], "nt", "mm_dhn2", tk=2816, b_stacked=True, after=tok)
    tok = hooks["scatter_go"]("s1", dhn2)
    dh1, dh1b, g["norm_ffn_w"] = _rms_bwd(dhn2, h1, p["norm_ffn_w"], dh2, "rms_ffn_bwd", after=tok)
    g["w_out"] = _matmul(merged, dh1b, "tn", "mm_gw_out", out_dtype=BF16).reshape(4, D // 4, D)
    dmerged = _matmul(dh1b, p["w_out"], "nt", "mm_dmerged")
    dga, dgb, dya, dval, dgate = _merge_bwd(dmerged, proj, ya, vg)
    dvg = jnp.concatenate([dval, dgate], axis=1)
    g["w_s5_glu"] = _matmul(g5, dvg, "tn", "mm_gw_glu", out_dtype=BF16, out_stacked=True)
    dg5 = _matmul(dvg, p["w_s5_glu"], "nt", "mm_dg5", b_stacked=True)
    tr = lambda b: b.transpose(0, 2, 1)
    gxr, gxi, dus, gcr, gci, g["s5_d"] = _s5_out_bwd(dg5, ypre, tr(cout_r).astype(BF16), tr(cout_in).astype(BF16),
                                                     proj, p["s5_d"], xre, xim)
    are, aim = _s5_scan(gxr, gxi, _scan_table(lbr, -lbi, True), True, "s5_scan_bwd")
    du, gbr, gbi, glr, gli = _s5_in_bwd(are, aim, tr(bin_r).astype(BF16), tr(bin_i).astype(BF16), proj, dus, xre, xim)
    g["s5_c_re"] = _from_out_blocks(gcr).transpose(0, 2, 1)
    g["s5_c_im"] = _from_out_blocks(gci).transpose(0, 2, 1)
    (g["s5_lam_re"], g["s5_lam_im"], g["s5_log_dt"], g["s5_b_re"], g["s5_b_im"]) = s5_vjp(
        (glr.reshape(64, 64), gli.reshape(64, 64), _from_out_blocks(gbr), _from_out_blocks(gbi)))
    g["w_proj_a"] = _matmul(yn, dya, "tn", "mm_gw_proj", out_dtype=BF16).reshape(4, DI // 4, D)
    tok = hooks["swap_start"](["w_proj_a", "w_s5_glu", "w_out"], g, "s2")
    dyn = _matmul(dya, p["w_proj_a"], "nt", "mm_dyn", after=tok)
    tok = hooks["scatter_go"]("s2", dyn)
    dyssd, dz, g["norm_a_w"] = _gnorm_bwd(dyn, yssd, proj, p["norm_a_w"], tok)
    dxs, dbm, dcm, ds_e, ddt_e, tsum, dsh8, pd = _ssd_bwd(xbc, s8, s_t, dt_e, s_e, dexp, sprev, dyssd)
    dsh = jnp.pad(dsh8.transpose(1, 0, 2).reshape(n, 8 * NG), ((0, 0), (0, HL - 8 * NG)))
    draw, ps = _ssd_post(ds_e, ddt_e, tsum, dsh, dtraw, dt, hp, emat.T)
    g["dt_bias"] = from_lanes(ps[0:1])
    g["a_log"] = from_lanes(ps[1:2])
    g["d_a"] = pd.reshape(NG * HPG, HD).sum(axis=1).reshape(1, NG * HPG)
    ddt = draw.astype(BF16)
    dxbc_parts, gcw, gcb = [], [], []
    for arr, col0, nm in ((dxs, 0, "conv_a_bwd_x"), (dbm, DI, "conv_a_bwd_b"), (dcm, DI + NG * NS, "conv_a_bwd_c")):
        dpart, gw_, gb_ = _conv_a_bwd(proj, arr, p["conv_a_w"], p["conv_a_b"], col0, nm)
        dxbc_parts.append(dpart)
        gcw.append(gw_)
        gcb.append(gb_)
    g["conv_a_w"] = jnp.concatenate(gcw, axis=1)
    g["conv_a_b"] = jnp.concatenate(gcb, axis=1)
    dproj = jnp.concatenate([dz] + dxbc_parts + [du, dga, dgb, ddt], axis=1)
    g_main = _matmul(dproj, hn1, "tn", "mm_gw_in", out_dtype=BF16, tm=896, tn=2048)
    g_dt = g_main[NMAIN:NMAIN + 8 * NG].reshape(NG, 8, D)[:, :HPG].reshape(NG * HPG, D)
    g_sh = _move_rows(g_main, RUNS_TO_SHARDS, 4 * WPAD, MT, MT, "rows_to_shards")
    g["w_in"] = lax.dynamic_update_slice(g_sh, g_dt, (DT_SHARD_ROW, 0)).reshape(4, WPAD, D)
    hooks["swap_start"](["w_in"], g, "s3")
    tok = hooks["scatter_go"]("s3", g_dt)
    dhn1 = _matmul(dproj, p["w_full"], "nn", "mm_dhn1", tk=2688, after=tok)
    gx, _, g["norm_mix_w"] = _rms_bwd(dhn1, x, p["norm_mix_w"], dh1, "rms_mix_bwd")
    return loss_blk, gx, g


BIG = ["w_in", "w_proj_a", "w_s5_glu", "w_out", "w_up", "w_down"]
SMALL = ["norm_mix_w", "conv_a_w", "conv_a_b", "dt_bias", "a_log", "d_a", "norm_a_w", "s5_lam_re", "s5_lam_im",
         "s5_log_dt", "s5_b_re", "s5_b_im", "s5_c_re", "s5_c_im", "s5_d", "norm_ffn_w", "conv_ffn_w", "conv_ffn_b",
         "norm_final_w"]
ORDER = ["norm_mix_w", "w_in", "conv_a_w", "conv_a_b", "dt_bias", "a_log", "d_a", "norm_a_w", "w_proj_a", "s5_lam_re",
         "s5_lam_im", "s5_log_dt", "s5_b_re", "s5_b_im", "s5_c_re", "s5_c_im", "s5_d", "w_s5_glu", "w_out",
         "norm_ffn_w", "w_up", "conv_ffn_w", "conv_ffn_b", "w_down", "norm_final_w"]
CONV_FULL = {"conv_a_w": (KA, CONVD), "conv_ffn_w": (KF, 2 * DFF)}


def _pack(arrs):
    flat = jnp.concatenate([a.reshape(-1).astype(F32) for a in arrs])
    total = flat.shape[0]
    padded = -(-total // 1024) * 1024
    return jnp.pad(flat, (0, padded - total)).reshape(padded // 128, 128)


def _unpack(block, shapes):
    flat = block.reshape(-1)
    out, at = [], 0
    for sh in shapes:
        size = math.prod(sh)
        out.append(flat[at:at + size].reshape(sh))
        at += size
    return out


def kernel(x, norm_mix_w, w_in, conv_a_w, conv_a_b, dt_bias, a_log, d_a, norm_a_w, w_proj_a, s5_lam_re, s5_lam_im, s5_log_dt, s5_b_re, s5_b_im, s5_c_re, s5_c_im, s5_d, w_s5_glu, w_out, norm_ffn_w, w_up, conv_ffn_w, conv_ffn_b, w_down, norm_final_w, loss_target, m_norm_mix_w, m_w_in, m_conv_a_w, m_conv_a_b, m_dt_bias, m_a_log, m_d_a, m_norm_a_w, m_w_proj_a, m_s5_lam_re, m_s5_lam_im, m_s5_log_dt, m_s5_b_re, m_s5_b_im, m_s5_c_re, m_s5_c_im, m_s5_d, m_w_s5_glu, m_w_out, m_norm_ffn_w, m_w_up, m_conv_ffn_w, m_conv_ffn_b, m_w_down, m_norm_final_w, v_norm_mix_w, v_w_in, v_conv_a_w, v_conv_a_b, v_dt_bias, v_a_log, v_d_a, v_norm_a_w, v_w_proj_a, v_s5_lam_re, v_s5_lam_im, v_s5_log_dt, v_s5_b_re, v_s5_b_im, v_s5_c_re, v_s5_c_im, v_s5_d, v_w_s5_glu, v_w_out, v_norm_ffn_w, v_w_up, v_conv_ffn_w, v_conv_ffn_b, v_w_down, v_norm_final_w):
    w = dict(norm_mix_w=norm_mix_w, w_in=w_in, conv_a_w=conv_a_w, conv_a_b=conv_a_b, dt_bias=dt_bias, a_log=a_log, d_a=d_a, norm_a_w=norm_a_w, w_proj_a=w_proj_a, s5_lam_re=s5_lam_re, s5_lam_im=s5_lam_im, s5_log_dt=s5_log_dt, s5_b_re=s5_b_re, s5_b_im=s5_b_im, s5_c_re=s5_c_re, s5_c_im=s5_c_im, s5_d=s5_d, w_s5_glu=w_s5_glu, w_out=w_out, norm_ffn_w=norm_ffn_w, w_up=w_up, conv_ffn_w=conv_ffn_w, conv_ffn_b=conv_ffn_b, w_down=w_down, norm_final_w=norm_final_w)
    m = dict(norm_mix_w=m_norm_mix_w, w_in=m_w_in, conv_a_w=m_conv_a_w, conv_a_b=m_conv_a_b, dt_bias=m_dt_bias, a_log=m_a_log, d_a=m_d_a, norm_a_w=m_norm_a_w, w_proj_a=m_w_proj_a, s5_lam_re=m_s5_lam_re, s5_lam_im=m_s5_lam_im, s5_log_dt=m_s5_log_dt, s5_b_re=m_s5_b_re, s5_b_im=m_s5_b_im, s5_c_re=m_s5_c_re, s5_c_im=m_s5_c_im, s5_d=m_s5_d, w_s5_glu=m_w_s5_glu, w_out=m_w_out, norm_ffn_w=m_norm_ffn_w, w_up=m_w_up, conv_ffn_w=m_conv_ffn_w, conv_ffn_b=m_conv_ffn_b, w_down=m_w_down, norm_final_w=m_norm_final_w)
    v = dict(norm_mix_w=v_norm_mix_w, w_in=v_w_in, conv_a_w=v_conv_a_w, conv_a_b=v_conv_a_b, dt_bias=v_dt_bias, a_log=v_a_log, d_a=v_d_a, norm_a_w=v_norm_a_w, w_proj_a=v_w_proj_a, s5_lam_re=v_s5_lam_re, s5_lam_im=v_s5_lam_im, s5_log_dt=v_s5_log_dt, s5_b_re=v_s5_b_re, s5_b_im=v_s5_b_im, s5_c_re=v_s5_c_re, s5_c_im=v_s5_c_im, s5_d=v_s5_d, w_s5_glu=v_w_s5_glu, w_out=v_w_out, norm_ffn_w=v_norm_ffn_w, w_up=v_w_up, conv_ffn_w=v_conv_ffn_w, conv_ffn_b=v_conv_ffn_b, w_down=v_w_down, norm_final_w=v_norm_final_w)
    xi, yi, ci = _place()
    chip = 2 * xi + yi

    cidx = jnp.reshape(ci, (1,)).astype(jnp.int32)
    sidx = jnp.reshape(chip, (1,)).astype(jnp.int32)

    tw = lambda a: jnp.transpose(a[0])[None]
    w["w_in"], m["w_in"], v["w_in"] = tw(w_in), tw(m_w_in), tw(v_w_in)
    shards = [w[k][0].astype(BF16) for k in BIG]
    shards[0] = jnp.pad(shards[0], ((0, WPAD - WSH), (0, 0)))

    late = {}

    def late_start(after):
        srcs, got = _ici_wait("gather", g_sems, g_srcs, g_lands, after, "gather_rest_wait")
        late["sems"], late["got"], late["srcs"], tok = _ici_start("pass", list(got), cidx, "pass_rest_start",
                                                                  lands=list(srcs))
        return tok

    def late_weights(after):
        full, _ = _ici_wait("pass", late["sems"], late["got"], late["srcs"], after, "pass_rest_wait")
        return {"w_proj_a": full[0].reshape(DI, D), "w_s5_glu": full[1], "w_out": full[2].reshape(D, D),
                "w_up": full[3], "w_down": full[4].reshape(DFF, D)}

    swaps, pending = {}, []

    def swap_start(names, g, tag):
        sems, parts, lands, tok = _ici_start("swap", [g[k] for k in names], cidx, "swap_start_" + tag)
        swaps[tag] = (names, sems, parts, lands)
        return tok

    def scatter_go(tag, after):
        names, sems, parts, lands = swaps[tag]
        parts, sib = _ici_wait("swap", sems, parts, lands, after, "swap_wait_" + tag)
        sums = [_chip_sum(parts[t], sib[t], cidx, "chip_sum_" + k) for t, k in enumerate(names)]
        sems, srcs, lands, tok = _ici_start("scatter", sums, cidx, "scatter_start_" + tag)
        pending.append((names, tag, sems, srcs, lands))
        return tok

    hooks = {"late_start": late_start, "late_weights": late_weights, "swap_start": swap_start,
             "scatter_go": scatter_go}
    conv_blocks = []
    for k, (taps, cols) in CONV_FULL.items():
        shard = jnp.where(ci == 0, w[k][0], 0.0)
        conv_blocks.append(lax.dynamic_update_slice_in_dim(jnp.zeros((taps, cols), F32), shard, chip * (cols // 4), 1))
    conv_full = _unpack(_allsum_small(_pack(conv_blocks), "sum_conv_w", cidx), [CONV_FULL[k] for k in CONV_FULL])

    half = D // 2
    sh_a, sh_b = shards[0][:, :half], shards[0][:, half:]
    a_sems, a_srcs, a_lands, a_tok = _ici_start("gather", [sh_a], conv_full[0], "gather_in_a_start")
    b_sems, b_srcs, b_lands, b_tok = _ici_start("gather", [sh_b], a_tok, "gather_in_b_start")
    hn1 = _rms_fwd(x[0], norm_mix_w, "rms_mix", after=b_tok)

    def w_in_part(sems, srcs, lands, after, tag):
        srcs, got = _ici_wait("gather", sems, srcs, lands, after, "gather_in_%s_wait" % tag)
        w_sh = _pass_halves(list(got), list(srcs), "pass_halves_in_" + tag)[0].reshape(4 * WPAD, half)
        w_dt = jnp.pad(w_sh[DT_SHARD_ROW:DT_SHARD_ROW + NG * HPG].reshape(NG, HPG, half),
                       ((0, 0), (0, 8 - HPG), (0, 0)))
        return lax.dynamic_update_slice(_move_rows(w_sh, RUNS_TO_MAIN, NFULL, MT, MT, "rows_to_main_" + tag),
                                        w_dt.reshape(8 * NG, half), (NMAIN, 0))

    w_a = w_in_part(a_sems, a_srcs, a_lands, hn1, "a")
    proj_a = _matmul(hn1[:, :half], w_a, "nt", "mm_in_a", tn=1920)
    w_b = w_in_part(b_sems, b_srcs, b_lands, proj_a, "b")
    g_sems, g_srcs, g_lands, token = _ici_start("gather", shards[1:], w_b, "gather_rest_start")
    proj = _matmul(hn1[:, half:], w_b, "nt", "mm_in_b", tn=1920, residual=proj_a, after=token)
    w_full = jnp.concatenate([w_a, w_b], axis=1)
    p = {
        "w_full": w_full,
        "conv_a_w": conv_full[0], "conv_ffn_w": conv_full[1],
        "conv_a_b": conv_a_b, "conv_ffn_b": conv_ffn_b,
        "norm_mix_w": norm_mix_w, "norm_a_w": norm_a_w, "norm_ffn_w": norm_ffn_w,
        "norm_final_w": norm_final_w.reshape(1, D),
        "dt_bias": dt_bias, "a_log": a_log, "d_a": d_a, "s5_d": s5_d,
        "s5_lam_re": s5_lam_re[0], "s5_lam_im": s5_lam_im[0], "s5_log_dt": s5_log_dt[0],
        "s5_b_re": s5_b_re[0], "s5_b_im": s5_b_im[0], "s5_c_re": s5_c_re[0], "s5_c_im": s5_c_im[0],
    }
    loss_blk, gx, g = _local_step(x[0], loss_target[0], hn1, proj, p, hooks)

    after, halves = gx, {}
    for names, tag, sems, srcs, lands in pending:
        srcs, got = _ici_wait("scatter", sems, srcs, lands, after, "scatter_wait_" + tag)
        for t, k in enumerate(names):
            halves[k] = _shard_sum(srcs[t], got[t], sidx, "shard_sum_" + k)
        after = halves[names[0]]
    w_sems, g_mine, w_lands, w_tok = _ici_start("whole", [halves[k] for k in BIG], cidx, "whole_start")

    small_shapes = [CONV_FULL.get(k, w[k].shape[1:] if k != "norm_final_w" else w[k].shape) for k in SMALL]
    small = _allsum_small(_pack([g[k] for k in SMALL] + [loss_blk[0:1, 0:1]]), "sum_small_grads", w_tok)
    small_grads = dict(zip(SMALL + ["loss"], _unpack(small, small_shapes + [(1,)])))
    for k, (taps, cols) in CONV_FULL.items():
        small_grads[k] = lax.dynamic_slice_in_dim(small_grads[k], chip * (cols // 4), cols // 4, axis=1)
    loss = small_grads.pop("loss").reshape(())

    grads, delta, new_m, new_v = {}, {}, {}, {}
    for k in SMALL:
        grads[k] = small_grads[k].reshape(w[k].shape)
    pk = lambda t: _pack([t[k] for k in SMALL])
    d_, m_, v_ = _adamw(pk(w), pk(grads), pk(m), pk(v), "adamw_small")
    shapes = [w[k].shape for k in SMALL]
    for k, dd, mm, vv in zip(SMALL, _unpack(d_, shapes), _unpack(m_, shapes), _unpack(v_, shapes)):
        delta[k], new_m[k], new_v[k] = dd, mm, vv
    g_mine, g_other = _ici_wait("whole", w_sems, g_mine, w_lands, d_, "whole_wait")
    for t, k in enumerate(BIG):
        outs = _adamw_halves(w[k], g_mine[t], g_other[t], m[k], v[k], cidx, "adamw_" + k)
        grads[k], delta[k], new_m[k], new_v[k] = [tw(o) for o in outs] if k == "w_in" else outs
    return (loss, gx[None], *[grads[k] for k in ORDER], *[delta[k] for k in ORDER],
            *[new_m[k] for k in ORDER], *[new_v[k] for k in ORDER])
```

```python
import functools
import math

import jax
import jax.numpy as jnp
from jax import lax
from jax.experimental import pallas as pl
from jax.experimental.pallas import tpu as pltpu

F32 = jnp.float32
BF16 = jnp.bfloat16
HI = lax.Precision.HIGHEST
MESH = pl.DeviceIdType.MESH
ANY = pl.BlockSpec(memory_space=pl.ANY)

D = 2048
DI = 3072
HD = 64
NG = 8
HPG = 6
GW = HPG * HD
NS = 128
KA = 4
Q = 256
CONVD = DI + 2 * NG * NS
DS5 = 1024
NCH = 4096
DFF = 5632
KF = 3
EPS = 1e-6
EIG_MAX = -1e-4
NMAIN = 13312
OFF_XBC, OFF_U, OFF_GA, OFF_GB = 3072, 8192, 9216, 11264
WSH = 3340
WPAD = 3360
IN_SPLIT = [DI, DI + CONVD, DI + CONVD + NG * HPG]
NFULL = NMAIN + 128
MT = 336


def _w_in_runs():
    runs = []
    for k in range(4):
        for o_lo, o_hi, m_lo in ((0, IN_SPLIT[1], 0), (IN_SPLIT[2], 4 * WSH, IN_SPLIT[1])):
            lo, hi = max(o_lo, WSH * k), min(o_hi, WSH * (k + 1))
            if lo < hi:
                runs.append((m_lo + lo - o_lo, m_lo + hi - o_lo, WPAD * k + lo - WSH * k))
    return runs


RUNS_TO_MAIN = _w_in_runs()
RUNS_TO_SHARDS = [(s_lo, s_lo + m_hi - m_lo, m_lo) for m_lo, m_hi, s_lo in RUNS_TO_MAIN]
DT_SHARD_ROW = WPAD * (IN_SPLIT[1] // WSH) + IN_SPLIT[1] % WSH
assert IN_SPLIT[1] // WSH == (IN_SPLIT[2] - 1) // WSH
VMEM_LIMIT = 56 * 1024 * 1024

LR, B1, B2, AEPS, WD, STEP = 0.001, 0.9, 0.999, 1e-08, 0.01, 10


def _cp(*sem):
    return pltpu.CompilerParams(dimension_semantics=sem, vmem_limit_bytes=VMEM_LIMIT)


def _sig(x):
    return jax.nn.sigmoid(x)


def _silu(x):
    return x * _sig(x)


def _dsilu(x):
    s = _sig(x)
    return s * (1.0 + x * (1.0 - s))


def _softplus(x):
    return jnp.maximum(x, 0.0) + jnp.log(1.0 + jnp.exp(-jnp.abs(x)))


_GC = math.sqrt(2.0 / math.pi)


def _gelu(x):
    return 0.5 * x * (1.0 + jnp.tanh(_GC * (x + 0.044715 * x * x * x)))


def _dgelu(x):
    t = jnp.tanh(_GC * (x + 0.044715 * x * x * x))
    return 0.5 * (1.0 + t) + 0.5 * x * (1.0 - t * t) * _GC * (1.0 + 3.0 * 0.044715 * x * x)


def _dot(a, b, dims=((1,), (0,)), prec=None):
    return lax.dot_general(a, b, (dims, ((), ())), precision=prec, preferred_element_type=F32)


NT = ((1,), (1,))
TN = ((0,), (0,))


def _pick(n, t):
    for unit in (128, 8):
        for cand in range(min(n, t) // unit * unit, 0, -unit):
            if n % cand == 0:
                return cand
    return n


def _matmul(a, b, mode, name, out_dtype=F32, tm=1024, tn=1024, tk=2048, residual=None, b_stacked=False,
            out_stacked=False, after=None):
    if b_stacked:
        _, brows, bn = b.shape
        bshape = (brows, 4 * bn)
    else:
        bshape = b.shape
    if mode == "nn":
        (m, k), (k2, n) = a.shape, bshape
    elif mode == "nt":
        (m, k), (n, k2) = a.shape, bshape
    else:
        (k, m), (k2, n) = a.shape, bshape
    assert k == k2
    tm = _pick(m, tm)
    tn = _pick(n // 4 if (out_stacked or (b_stacked and mode != "nt")) else n, tn)
    tk = _pick(k // 4 if (b_stacked and mode == "nt") else k, tk)
    nk = k // tk
    dims = {"nn": ((1,), (0,)), "nt": NT, "tn": TN}[mode]
    has_res = residual is not None
    n_in = 2 + has_res + (after is not None)

    def body(*refs):
        a_ref, b_ref = refs[0], refs[1]
        r_ref = refs[2] if has_res else None
        o_ref = refs[n_in]
        p = _dot(a_ref[...], b_ref[...], dims)

        def finish(r):
            if has_res:
                r = r + r_ref[...]
            o_ref[...] = r.astype(out_dtype)

        if nk == 1:
            finish(p)
        else:
            acc = refs[-1]
            kk = pl.program_id(2)

            @pl.when(kk == 0)
            def _():
                acc[...] = p

            @pl.when(kk > 0)
            def _():
                acc[...] += p

            @pl.when(kk == nk - 1)
            def _():
                finish(acc[...])

    if mode == "tn":
        a_spec = pl.BlockSpec((tk, tm), lambda i, j, kk: (kk, i))
    else:
        a_spec = pl.BlockSpec((tm, tk), lambda i, j, kk: (i, kk))
    if mode == "nt":
        if b_stacked:
            per = bn // tk
            b_spec = pl.BlockSpec((None, tn, tk), lambda i, j, kk: (kk // per, j, kk % per))
        else:
            b_spec = pl.BlockSpec((tn, tk), lambda i, j, kk: (j, kk))
    elif b_stacked:
        per = bn // tn
        b_spec = pl.BlockSpec((None, tk, tn), lambda i, j, kk: (j // per, kk, j % per))
    else:
        b_spec = pl.BlockSpec((tk, tn), lambda i, j, kk: (kk, j))
    o_spec = pl.BlockSpec((tm, tn), lambda i, j, kk: (i, j))
    if out_stacked:
        per_o = n // 4 // tn
        out_spec = pl.BlockSpec((None, tm, tn), lambda i, j, kk: (j // per_o, i, j % per_o))
        out_shape = jax.ShapeDtypeStruct((4, m, n // 4), out_dtype)
    else:
        out_spec, out_shape = o_spec, jax.ShapeDtypeStruct((m, n), out_dtype)
    in_specs, args = [a_spec, b_spec], [a, b]
    if has_res:
        in_specs.append(o_spec)
        args.append(residual)
    if after is not None:
        in_specs.append(ANY)
        args.append(after)
    return pl.pallas_call(
        body, name=name, grid=(m // tm, n // tn, nk),
        in_specs=in_specs, out_specs=out_spec, out_shape=out_shape,
        scratch_shapes=[pltpu.VMEM((tm, tn), F32)] if nk > 1 else [],
        compiler_params=_cp("parallel", "parallel", "arbitrary"),
    )(*args)


def _move_rows(src, runs, rows_out, t_out, t_in, name):
    rows_in, cols = src.shape
    nb_out, nb_in = rows_out // t_out, rows_in // t_in
    assert rows_out % t_out == 0 and rows_in % t_in == 0 and t_in >= t_out
    blk, off, lo, hi = ([[0] * nb_out for _ in range(2)] for _ in range(4))
    for i in range(nb_out):
        hits = [r for r in runs if r[0] < (i + 1) * t_out and r[1] > i * t_out]
        assert len(hits) <= 2
        for s, (o_lo, o_hi, s_lo) in enumerate(hits):
            lo[s][i] = max(o_lo, i * t_out) - i * t_out
            hi[s][i] = min(o_hi, (i + 1) * t_out) - i * t_out
            first = i * t_out + lo[s][i] - o_lo + s_lo
            blk[s][i] = min(first // t_in, nb_in - 1)
            off[s][i] = first - lo[s][i] - blk[s][i] * t_in
    table = jnp.asarray([blk[0], off[0], lo[0], hi[0], blk[1], off[1], lo[1], hi[1]], jnp.int32)

    def body(tab, a0, a1, b0, b1, o_ref):
        i = pl.program_id(0)
        o_ref[...] = jnp.zeros_like(o_ref)
        r = lax.broadcasted_iota(jnp.int32, (t_out, t_in), 0)
        k = lax.broadcasted_iota(jnp.int32, (t_out, t_in), 1)
        for s, (first, second) in enumerate(((a0, a1), (b0, b1))):
            off_s, lo_s, hi_s = tab[4 * s + 1, i], tab[4 * s + 2, i], tab[4 * s + 3, i]
            live = (r >= lo_s) & (r < hi_s)

            @pl.when(hi_s > lo_s)
            def _():
                sel = (live & (k == r + off_s)).astype(BF16)
                o_ref[...] += _dot(sel, first[...]).astype(o_ref.dtype)

            @pl.when((hi_s > lo_s) & (off_s + hi_s > t_in))
            def _():
                sel = (live & (k == r + off_s - t_in)).astype(BF16)
                o_ref[...] += _dot(sel, second[...]).astype(o_ref.dtype)

    def in_spec(s, nxt):
        return pl.BlockSpec((t_in, cols), lambda i, tab: (jnp.minimum(tab[4 * s, i] + nxt, nb_in - 1), 0))

    return pl.pallas_call(
        body, name=name,
        grid_spec=pltpu.PrefetchScalarGridSpec(
            num_scalar_prefetch=1, grid=(nb_out,),
            in_specs=[in_spec(0, 0), in_spec(0, 1), in_spec(1, 0), in_spec(1, 1)],
            out_specs=pl.BlockSpec((t_out, cols), lambda i, tab: (i, 0))),
        out_shape=jax.ShapeDtypeStruct((rows_out, cols), src.dtype), compiler_params=_cp("parallel"),
    )(table, src, src, src, src)


TL = 256


def _rms_fwd(x, w, name, after=None):
    n, d = x.shape

    def body(x_ref, w_ref, *rest):
        xv = x_ref[...]
        r = lax.rsqrt(jnp.mean(xv * xv, axis=-1, keepdims=True) + EPS)
        rest[-1][...] = (xv * r * w_ref[...]).astype(BF16)

    extra = [] if after is None else [after]
    return pl.pallas_call(
        body, name=name, grid=(n // TL,),
        in_specs=[pl.BlockSpec((TL, d), lambda i: (i, 0)), pl.BlockSpec((1, d), lambda i: (0, 0))] + [ANY] * len(extra),
        out_specs=pl.BlockSpec((TL, d), lambda i: (i, 0)),
        out_shape=jax.ShapeDtypeStruct((n, d), BF16), compiler_params=_cp("parallel"),
    )(x, w, *extra)


def _rms_bwd(dhn, x, w, dres, name, after=None):
    n, d = x.shape

    def body(g_ref, x_ref, w_ref, r_ref, *rest):
        dx_ref, dxb_ref, gw_ref = rest[-3:]
        xv = x_ref[...]
        r = lax.rsqrt(jnp.mean(xv * xv, axis=-1, keepdims=True) + EPS)
        xh = xv * r
        gv = g_ref[...]
        g = gv * w_ref[...]
        dx = r_ref[...] + r * (g - xh * jnp.mean(g * xh, axis=-1, keepdims=True))
        dx_ref[...] = dx
        dxb_ref[...] = dx.astype(BF16)

        @pl.when(pl.program_id(0) == 0)
        def _():
            gw_ref[...] = jnp.zeros_like(gw_ref)

        gw_ref[...] += jnp.sum(gv * xh, axis=0, keepdims=True)

    extra = [] if after is None else [after]
    row = pl.BlockSpec((TL, d), lambda i: (i, 0))
    vec = pl.BlockSpec((1, d), lambda i: (0, 0))
    return pl.pallas_call(
        body, name=name, grid=(n // TL,),
        in_specs=[row, row, vec, row] + [ANY] * len(extra), out_specs=[row, row, vec],
        out_shape=[jax.ShapeDtypeStruct((n, d), F32), jax.ShapeDtypeStruct((n, d), BF16),
                   jax.ShapeDtypeStruct((1, d), F32)],
        compiler_params=_cp("arbitrary"),
    )(dhn, x, w, dres, *extra)


def _final(h2, w, target):
    n, d = h2.shape

    def body(x_ref, w_ref, t_ref, dx_ref, dxb_ref, gw_ref, loss_ref):
        xv = x_ref[...]
        r = lax.rsqrt(jnp.mean(xv * xv, axis=-1, keepdims=True) + EPS)
        xh = xv * r
        diff = xh * w_ref[...] - t_ref[...]
        gv = diff * (1.0 / d)
        g = gv * w_ref[...]
        dx = r * (g - xh * jnp.mean(g * xh, axis=-1, keepdims=True))
        dx_ref[...] = dx
        dxb_ref[...] = dx.astype(BF16)

        @pl.when(pl.program_id(0) == 0)
        def _():
            gw_ref[...] = jnp.zeros_like(gw_ref)
            loss_ref[...] = jnp.zeros_like(loss_ref)

        gw_ref[...] += jnp.sum(gv * xh, axis=0, keepdims=True)
        part = 0.5 * jnp.sum(jnp.mean(diff * diff, axis=-1, keepdims=True), axis=0, keepdims=True)
        loss_ref[...] += jnp.broadcast_to(part, loss_ref.shape)

    row = pl.BlockSpec((TL, d), lambda i: (i, 0))
    vec = pl.BlockSpec((1, d), lambda i: (0, 0))
    return pl.pallas_call(
        body, name="final_loss", grid=(n // TL,),
        in_specs=[row, vec, row], out_specs=[row, row, vec, pl.BlockSpec((8, 128), lambda i: (0, 0))],
        out_shape=[jax.ShapeDtypeStruct((n, d), F32), jax.ShapeDtypeStruct((n, d), BF16),
                   jax.ShapeDtypeStruct((1, d), F32), jax.ShapeDtypeStruct((8, 128), F32)],
        compiler_params=_cp("arbitrary"),
    )(h2, w, target)


CT = 512
CL = 512


def _lagged(xf, taps, rows):
    return [xf[8:8 + rows]] + [pltpu.roll(xf, s, 0)[8:8 + rows] for s in range(1, taps)]


def _shift_up(x, u, n):
    if u == 0:
        return x[0:n]
    return pltpu.roll(x, x.shape[0] - u, 0)[0:n]


def _conv_pre(lagged, w_ref, b_ref, taps):
    pre = b_ref[...]
    for k in range(taps):
        pre = pre + w_ref[k:k + 1, :] * lagged[taps - 1 - k]
    return pre


def _conv_back(e, w_ref, taps):
    dx = w_ref[taps - 1:taps, :] * e[0:CL]
    for k in range(taps - 1):
        dx = dx + w_ref[k:k + 1, :] * _shift_up(e, taps - 1 - k, CL)
    return dx


def _halo_specs(n, col_of):
    per = CL // 8
    cur = pl.BlockSpec((CL, CT), lambda j, i, *_: (i, col_of(j)))
    prev = pl.BlockSpec((8, CT), lambda j, i, *_: (jnp.maximum(i * per - 1, 0), col_of(j)))
    nxt = pl.BlockSpec((8, CT), lambda j, i, *_: (jnp.minimum((i + 1) * per, n // 8 - 1), col_of(j)))
    return prev, cur, nxt


def _conv_a_fwd(proj, w, b):
    n = proj.shape[0]
    off = OFF_XBC // CT

    def body(p_ref, x_ref, w_ref, b_ref, o_ref):
        p8 = jnp.where(pl.program_id(1) > 0, p_ref[...], 0.0)
        xf = jnp.concatenate([p8, x_ref[...]], axis=0)
        o_ref[...] = _silu(_conv_pre(_lagged(xf, KA, CL), w_ref, b_ref, KA))

    prev, cur, _ = _halo_specs(n, lambda j: j + off)
    return pl.pallas_call(
        body, name="conv_a_fwd", grid=(CONVD // CT, n // CL),
        in_specs=[prev, cur, pl.BlockSpec((KA, CT), lambda j, i: (0, j)), pl.BlockSpec((1, CT), lambda j, i: (0, j))],
        out_specs=pl.BlockSpec((CL, CT), lambda j, i: (i, j)),
        out_shape=jax.ShapeDtypeStruct((n, CONVD), F32), compiler_params=_cp("parallel", "parallel"),
    )(proj, proj, w, b)


def _conv_a_bwd(proj, dout, w, b, col0, name):
    n, width = dout.shape
    off = (OFF_XBC + col0) // CT
    woff = col0 // CT
    nl = n // CL

    def body(p_ref, x_ref, n_ref, d_ref, dn_ref, w_ref, b_ref, dx_ref, dw_ref, db_ref):
        i = pl.program_id(1)
        xf = jnp.concatenate([jnp.where(i > 0, p_ref[...], 0.0), x_ref[...], n_ref[...]], axis=0)
        lag = _lagged(xf, KA, CL + 8)
        de = jnp.concatenate([d_ref[...], jnp.where(i < nl - 1, dn_ref[...], 0.0)], axis=0)
        se = de * _dsilu(_conv_pre(lag, w_ref, b_ref, KA))
        dx_ref[...] = _conv_back(se, w_ref, KA).astype(BF16)

        @pl.when(i == 0)
        def _():
            dw_ref[...] = jnp.zeros_like(dw_ref)
            db_ref[...] = jnp.zeros_like(db_ref)

        sc = se[0:CL]
        for k in range(KA):
            dw_ref[k:k + 1, :] += jnp.sum(sc * lag[KA - 1 - k][0:CL], axis=0, keepdims=True)
        db_ref[...] += jnp.sum(sc, axis=0, keepdims=True)

    prev, cur, nxt = _halo_specs(n, lambda j: j + off)
    _, dcur, dnxt = _halo_specs(n, lambda j: j)
    wspec = pl.BlockSpec((KA, CT), lambda j, i: (0, j + woff))
    bspec = pl.BlockSpec((1, CT), lambda j, i: (0, j + woff))
    return pl.pallas_call(
        body, name=name, grid=(width // CT, nl),
        in_specs=[prev, cur, nxt, dcur, dnxt, wspec, bspec],
        out_specs=[pl.BlockSpec((CL, CT), lambda j, i: (i, j)), pl.BlockSpec((KA, CT), lambda j, i: (0, j)),
                   pl.BlockSpec((1, CT), lambda j, i: (0, j))],
        out_shape=[jax.ShapeDtypeStruct((n, width), BF16), jax.ShapeDtypeStruct((KA, width), F32),
                   jax.ShapeDtypeStruct((1, width), F32)],
        compiler_params=_cp("parallel", "arbitrary"),
    )(proj, proj, proj, dout, dout, w, b)


def _conv_ffn_fwd(up, w, b):
    n = up.shape[0]
    nb = DFF // CT

    def body(pg_ref, g_ref, pv_ref, v_ref, wg_ref, bg_ref, wv_ref, bv_ref, o_ref):
        inner = pl.program_id(1) > 0
        gf = jnp.concatenate([jnp.where(inner, pg_ref[...], 0.0), g_ref[...]], axis=0)
        vf = jnp.concatenate([jnp.where(inner, pv_ref[...], 0.0), v_ref[...]], axis=0)
        gc = _conv_pre(_lagged(gf, KF, CL), wg_ref, bg_ref, KF)
        vc = _conv_pre(_lagged(vf, KF, CL), wv_ref, bv_ref, KF)
        o_ref[...] = (_silu(gc) * vc).astype(BF16)

    gp, gcur, _ = _halo_specs(n, lambda j: j)
    vp, vcur, _ = _halo_specs(n, lambda j: j + nb)
    return pl.pallas_call(
        body, name="conv_ffn_fwd", grid=(nb, n // CL),
        in_specs=[gp, gcur, vp, vcur,
                  pl.BlockSpec((KF, CT), lambda j, i: (0, j)), pl.BlockSpec((1, CT), lambda j, i: (0, j)),
                  pl.BlockSpec((KF, CT), lambda j, i: (0, j + nb)), pl.BlockSpec((1, CT), lambda j, i: (0, j + nb))],
        out_specs=pl.BlockSpec((CL, CT), lambda j, i: (i, j)),
        out_shape=jax.ShapeDtypeStruct((n, DFF), BF16), compiler_params=_cp("parallel", "parallel"),
    )(up, up, up, up, w, b, w, b)


def _conv_ffn_bwd(up, dact, w, b):
    n = up.shape[0]
    nb = DFF // CT
    nl = n // CL

    def body(pg_ref, g_ref, ng_ref, pv_ref, v_ref, nv_ref, d_ref, dn_ref, wg_ref, bg_ref, wv_ref, bv_ref,
             dxg_ref, dxv_ref, dwg_ref, dwv_ref, dbg_ref, dbv_ref):
        i = pl.program_id(1)
        gf = jnp.concatenate([jnp.where(i > 0, pg_ref[...], 0.0), g_ref[...], ng_ref[...]], axis=0)
        vf = jnp.concatenate([jnp.where(i > 0, pv_ref[...], 0.0), v_ref[...], nv_ref[...]], axis=0)
        glag, vlag = _lagged(gf, KF, CL + 8), _lagged(vf, KF, CL + 8)
        de = jnp.concatenate([d_ref[...], jnp.where(i < nl - 1, dn_ref[...], 0.0)], axis=0).astype(F32)
        gc = _conv_pre(glag, wg_ref, bg_ref, KF)
        vc = _conv_pre(vlag, wv_ref, bv_ref, KF)
        sg = _sig(gc)
        dgc = de * vc * (sg * (1.0 + gc * (1.0 - sg)))
        dvc = de * (gc * sg)
        dxg_ref[...] = _conv_back(dgc, wg_ref, KF).astype(BF16)
        dxv_ref[...] = _conv_back(dvc, wv_ref, KF).astype(BF16)

        @pl.when(i == 0)
        def _():
            for r in (dwg_ref, dwv_ref, dbg_ref, dbv_ref):
                r[...] = jnp.zeros_like(r)

        for e, lag, dw_ref, db_ref in ((dgc, glag, dwg_ref, dbg_ref), (dvc, vlag, dwv_ref, dbv_ref)):
            ec = e[0:CL]
            for k in range(KF):
                dw_ref[k:k + 1, :] += jnp.sum(ec * lag[KF - 1 - k][0:CL], axis=0, keepdims=True)
            db_ref[...] += jnp.sum(ec, axis=0, keepdims=True)

    gp, gcur, gnx = _halo_specs(n, lambda j: j)
    vp, vcur, vnx = _halo_specs(n, lambda j: j + nb)
    wcol = lambda o: (pl.BlockSpec((KF, CT), lambda j, i: (0, j + o)), pl.BlockSpec((1, CT), lambda j, i: (0, j + o)))
    wg, bg = wcol(0)
    wv, bv = wcol(nb)
    dxs = pl.BlockSpec((CL, CT), lambda j, i: (i, j))
    outs = pl.pallas_call(
        body, name="conv_ffn_bwd", grid=(nb, nl),
        in_specs=[gp, gcur, gnx, vp, vcur, vnx, gcur, gnx, wg, bg, wv, bv],
        out_specs=[dxs, dxs, wg, wg, bg, bg],
        out_shape=[jax.ShapeDtypeStruct((n, DFF), BF16)] * 2 + [jax.ShapeDtypeStruct((KF, DFF), F32)] * 2
        + [jax.ShapeDtypeStruct((1, DFF), F32)] * 2,
        compiler_params=_cp("parallel", "arbitrary"),
    )(up, up, up, up, up, up, dact, dact, w, b, w, b)
    return [jnp.concatenate(outs[k:k + 2], axis=1) for k in (0, 2, 4)]


HL = 128


def _split3(x):
    hi = x.astype(BF16)
    r1 = x - hi.astype(F32)
    mid = r1.astype(BF16)
    return hi, mid, (r1 - mid.astype(F32)).astype(BF16)


def _dot3(x, m):
    hi, mid, lo = _split3(x)
    return _dot(hi, m) + _dot(mid, m) + _dot(lo, m)


def _tri():
    row = lax.broadcasted_iota(jnp.int32, (Q, Q), 0)
    col = lax.broadcasted_iota(jnp.int32, (Q, Q), 1)
    return row >= col, row <= col


def _ssd_prep(dtraw, hp, emat):
    n = dtraw.shape[0]

    def body(d_ref, hp_ref, e_ref, dt_ref, s_ref, st_ref, dte_ref, se_ref):
        lower, upper = _tri()
        dt = _softplus(d_ref[...] + hp_ref[0:1, :])
        da = dt * (-jnp.exp(hp_ref[1:2, :]))
        s = _dot(lower.astype(F32), da, prec=HI)
        dt_ref[...] = dt
        s_ref[...] = s
        st_ref[...] = _dot(da, upper.astype(F32), TN, prec=HI)
        e = e_ref[...]
        dte_ref[...] = _dot3(dt, e)
        se_ref[...] = _dot3(s, e)

    row = pl.BlockSpec((Q, HL), lambda c: (c, 0))
    wide = pl.BlockSpec((Q, DI), lambda c: (c, 0))
    return pl.pallas_call(
        body, name="ssd_prep", grid=(n // Q,),
        in_specs=[pl.BlockSpec((Q, HL), lambda c: (c, NMAIN // HL)), pl.BlockSpec((8, HL), lambda c: (0, 0)),
                  pl.BlockSpec((HL, DI), lambda c: (0, 0))],
        out_specs=[row, row, pl.BlockSpec((HL, Q), lambda c: (0, c)), wide, wide],
        out_shape=[jax.ShapeDtypeStruct((n, HL), F32)] * 2 + [jax.ShapeDtypeStruct((HL, n), F32)]
        + [jax.ShapeDtypeStruct((n, DI), F32)] * 2,
        compiler_params=_cp("parallel"),
    )(dtraw, hp, emat)


def _ssd_post(ds_e, ddt_e, tsum, dsh, dtraw, dt, hp, emat_t):
    n = dtraw.shape[0]

    def body(dse_ref, dde_ref, ts_ref, dsh_ref, d_ref, dt_ref, hp_ref, et_ref, draw_ref, ps_ref):
        _, upper = _tri()
        et = et_ref[...]
        a = -jnp.exp(hp_ref[1:2, :])
        rows = lax.broadcasted_iota(jnp.int32, (Q, HL), 0)
        ds_t = _dot3(jnp.broadcast_to(ts_ref[...], (8, DI)), et)[0:1, :]
        ds = _dot3(dse_ref[...], et) + dsh_ref[...] + jnp.where(rows == Q - 1, ds_t, 0.0)
        d_a = _dot(upper.astype(F32), ds, prec=HI)
        draw = (_dot3(dde_ref[...], et) + d_a * a) * _sig(d_ref[...] + hp_ref[0:1, :])
        draw_ref[...] = draw

        @pl.when(pl.program_id(0) == 0)
        def _():
            ps_ref[...] = jnp.zeros_like(ps_ref)

        ps_ref[0:1, :] += jnp.sum(draw, axis=0, keepdims=True)
        ps_ref[1:2, :] += jnp.sum(d_a * dt_ref[...], axis=0, keepdims=True) * a

    row = pl.BlockSpec((Q, HL), lambda c: (c, 0))
    wide = pl.BlockSpec((Q, DI), lambda c: (c, 0))
    small = pl.BlockSpec((8, HL), lambda c: (0, 0))
    return pl.pallas_call(
        body, name="ssd_post", grid=(n // Q,),
        in_specs=[wide, wide, pl.BlockSpec((None, 1, DI), lambda c: (c, 0, 0)), row,
                  pl.BlockSpec((Q, HL), lambda c: (c, NMAIN // HL)), row, small,
                  pl.BlockSpec((DI, HL), lambda c: (0, 0))],
        out_specs=[row, small],
        out_shape=[jax.ShapeDtypeStruct((n, HL), F32), jax.ShapeDtypeStruct((8, HL), F32)],
        compiler_params=_cp("arbitrary"),
    )(ds_e, ddt_e, tsum, dsh, dtraw, dt, hp, emat_t)


def _ssd_specs(nc, rev):
    cc = (lambda c: nc - 1 - c) if rev else (lambda c: c)
    return [
        pl.BlockSpec((Q, GW), lambda g, c: (cc(c), g)),
        pl.BlockSpec((Q, NS), lambda g, c: (cc(c), DI // NS + g)),
        pl.BlockSpec((Q, NS), lambda g, c: (cc(c), (DI + NG * NS) // NS + g)),
        pl.BlockSpec((None, Q, 8), lambda g, c: (g, cc(c), 0)),
        pl.BlockSpec((8, Q), lambda g, c: (g, cc(c))),
        pl.BlockSpec((Q, GW), lambda g, c: (cc(c), g)),
        pl.BlockSpec((Q, GW), lambda g, c: (cc(c), g)),
        pl.BlockSpec((1, GW), lambda g, c: (0, g)),
    ]


def _ssd_fwd(xbc, s8, s_t, dt_e, s_e, dexp):
    n = xbc.shape[0]
    nc = n // Q

    def body(xs_ref, b_ref, c_ref, sc_ref, sr_ref, dte_ref, se_ref, dexp_ref, y_ref, sp_ref, st):
        @pl.when(pl.program_id(1) == 0)
        def _():
            st[...] = jnp.zeros_like(st)

        lower, _ = _tri()
        s_c, s_r, dt_e, s_e = sc_ref[...], sr_ref[...], dte_ref[...], se_ref[...]
        xs = xs_ref[...]
        x = xs * dt_e
        xb = x.astype(BF16)
        bb, cb = b_ref[...].astype(BF16), c_ref[...].astype(BF16)
        cbm = _dot(cb, bb, NT)
        st_e = s_e[Q - 1:Q, :]
        sprev = st[...]
        sp_ref[...] = sprev
        yoff = _dot(cb, sprev.astype(BF16)) * jnp.exp(s_e) + dexp_ref[...] * xs
        for h in range(HPG):
            sl = slice(h * HD, (h + 1) * HD)
            lm = jnp.where(lower, jnp.exp(jnp.minimum(s_c[:, h:h + 1] - s_r[h:h + 1, :], 0.0)), 0.0)
            y_ref[:, sl] = _dot((cbm * lm).astype(BF16), xb[:, sl]) + yoff[:, sl]
        w = (x * jnp.exp(st_e - s_e)).astype(BF16)
        st[...] = jnp.exp(st_e) * sprev + _dot(bb, w, TN)

    return pl.pallas_call(
        body, name="ssd_fwd", grid=(NG, nc), in_specs=_ssd_specs(nc, False),
        out_specs=[pl.BlockSpec((Q, GW), lambda g, c: (c, g)),
                   pl.BlockSpec((None, None, NS, GW), lambda g, c: (c, g, 0, 0))],
        out_shape=[jax.ShapeDtypeStruct((n, DI), F32), jax.ShapeDtypeStruct((nc, NG, NS, GW), F32)],
        scratch_shapes=[pltpu.VMEM((NS, GW), F32)],
        compiler_params=_cp("parallel", "arbitrary"),
    )(xbc, xbc, xbc, s8, s_t, dt_e, s_e, dexp)


def _ssd_bwd(xbc, s8, s_t, dt_e, s_e, dexp, sprev_all, dy):
    n = xbc.shape[0]
    nc = n // Q
    rc = lambda c: nc - 1 - c

    def body(xs_ref, b_ref, c_ref, sc_ref, sr_ref, dte_ref, se_ref, dexp_ref, sp_ref, dy_ref,
             dxs_ref, db_ref, dc_ref, dse_ref, dde_ref, ts_ref, dsh_ref, pd_ref, dst, dxbuf):
        @pl.when(pl.program_id(1) == 0)
        def _():
            dst[...] = jnp.zeros_like(dst)
            pd_ref[...] = jnp.zeros_like(pd_ref)

        lower, upper = _tri()
        s_c, s_r, dt_e, s_e = sc_ref[...], sr_ref[...], dte_ref[...], se_ref[...]
        xs = xs_ref[...]
        x = xs * dt_e
        xb = x.astype(BF16)
        bb, cb = b_ref[...].astype(BF16), c_ref[...].astype(BF16)
        cbm = _dot(cb, bb, NT)
        cbt = _dot(bb, cb, NT)
        st_e = s_e[Q - 1:Q, :]
        dec_out, dec_st, e_t = jnp.exp(s_e), jnp.exp(st_e - s_e), jnp.exp(st_e)
        dyv = dy_ref[...]
        dyb = dyv.astype(BF16)
        sprev = sp_ref[...]
        sb = sprev.astype(BF16)
        ds_in = dst[...]
        dsb = ds_in.astype(BF16)

        cs = _dot(cb, sb)
        dcs = (dyv * dec_out).astype(BF16)
        d_c = _dot(dcs, sb, NT)
        wf = x * dec_st
        d_w = _dot(bb, dsb)
        d_b = _dot(wf.astype(BF16), dsb, NT)
        tw = d_w * wf
        dse_ref[...] = dyv * cs * dec_out - tw
        ds_c = jnp.zeros((Q, 8), F32)
        dcb = jnp.zeros((Q, Q), F32)
        dcbt = jnp.zeros((Q, Q), F32)
        lane8 = lax.broadcasted_iota(jnp.int32, (1, 8), 1)
        for h in range(HPG):
            sl = slice(h * HD, (h + 1) * HD)
            sc_h, sr_h = s_c[:, h:h + 1], s_r[h:h + 1, :]
            lm = jnp.where(lower, jnp.exp(jnp.minimum(sc_h - sr_h, 0.0)), 0.0)
            lmt = jnp.where(upper, jnp.exp(jnp.minimum(sr_h - sc_h, 0.0)), 0.0)
            mt = cbt * lmt
            dm = _dot(dyb[:, sl], xb[:, sl], NT)
            dmt = _dot(xb[:, sl], dyb[:, sl], NT)
            dxbuf[:, sl] = _dot(mt.astype(BF16), dyb[:, sl])
            dml = dm * lm
            dmlt = dmt * lmt
            dcb = dcb + dml
            dcbt = dcbt + dmlt
            dsh = jnp.sum(dml * cbm, axis=1, keepdims=True) - jnp.sum(dmlt * cbt, axis=1, keepdims=True)
            ds_c = ds_c + dsh * (lane8 == h).astype(F32)
        d_c = d_c + _dot(dcb.astype(BF16), bb)
        d_b = d_b + _dot(dcbt.astype(BF16), cb)
        dx = d_w * dec_st + dxbuf[...]
        ts_ref[...] = jnp.sum(tw, axis=0, keepdims=True) + jnp.sum(ds_in * sprev, axis=0, keepdims=True) * e_t
        dsh_ref[...] = ds_c
        dde_ref[...] = dx * xs
        pd_ref[...] += jnp.sum(dyv * xs, axis=0, keepdims=True)
        dxs_ref[...] = dx * dt_e + dyv * dexp_ref[...]
        db_ref[...] = d_b
        dc_ref[...] = d_c
        dst[...] = e_t * ds_in + _dot(cb, dcs, TN)

    wide = pl.BlockSpec((Q, GW), lambda g, c: (rc(c), g))
    state = pl.BlockSpec((Q, NS), lambda g, c: (rc(c), g))
    in_specs = _ssd_specs(nc, True) + [pl.BlockSpec((None, None, NS, GW), lambda g, c: (rc(c), g, 0, 0)), wide]
    return pl.pallas_call(
        body, name="ssd_bwd", grid=(NG, nc), in_specs=in_specs,
        out_specs=[wide, state, state, wide, wide,
                   pl.BlockSpec((None, 1, GW), lambda g, c: (rc(c), 0, g)),
                   pl.BlockSpec((None, Q, 8), lambda g, c: (g, rc(c), 0)),
                   pl.BlockSpec((None, 1, GW), lambda g, c: (g, 0, 0))],
        out_shape=[jax.ShapeDtypeStruct((n, DI), F32), jax.ShapeDtypeStruct((n, NG * NS), F32),
                   jax.ShapeDtypeStruct((n, NG * NS), F32), jax.ShapeDtypeStruct((n, DI), F32),
                   jax.ShapeDtypeStruct((n, DI), F32), jax.ShapeDtypeStruct((nc, 1, DI), F32),
                   jax.ShapeDtypeStruct((NG, n, 8), F32), jax.ShapeDtypeStruct((NG, 1, GW), F32)],
        scratch_shapes=[pltpu.VMEM((NS, GW), F32), pltpu.VMEM((Q, GW), F32)],
        compiler_params=_cp("parallel", "arbitrary"),
    )(xbc, xbc, xbc, s8, s_t, dt_e, s_e, dexp, sprev_all, dy)


GL = 256


def _gnorm_fwd(y, proj, w):
    n = y.shape[0]

    def body(y_ref, z_ref, w_ref, o_ref):
        for g in range(NG):
            sl = slice(g * GW, (g + 1) * GW)
            yz = y_ref[:, sl] * _silu(z_ref[:, sl])
            r = lax.rsqrt(jnp.mean(yz * yz, axis=-1, keepdims=True) + EPS)
            o_ref[:, sl] = (yz * r * w_ref[:, sl]).astype(BF16)

    row = pl.BlockSpec((GL, DI), lambda i: (i, 0))
    return pl.pallas_call(
        body, name="gnorm_fwd", grid=(n // GL,),
        in_specs=[row, row, pl.BlockSpec((1, DI), lambda i: (0, 0))], out_specs=row,
        out_shape=jax.ShapeDtypeStruct((n, DI), BF16), compiler_params=_cp("parallel"),
    )(y, proj, w)


def _gnorm_bwd(dyn, y, proj, w, after):
    n = y.shape[0]

    def body(d_ref, y_ref, z_ref, w_ref, _, dy_ref, dz_ref, gw_ref):
        @pl.when(pl.program_id(0) == 0)
        def _():
            gw_ref[...] = jnp.zeros_like(gw_ref)

        for g in range(NG):
            sl = slice(g * GW, (g + 1) * GW)
            yv, zv, dv = y_ref[:, sl], z_ref[:, sl], d_ref[:, sl]
            sz = _silu(zv)
            yz = yv * sz
            r = lax.rsqrt(jnp.mean(yz * yz, axis=-1, keepdims=True) + EPS)
            yh = yz * r
            gg = dv * w_ref[:, sl]
            dyz = r * (gg - yh * jnp.mean(gg * yh, axis=-1, keepdims=True))
            gw_ref[:, sl] += jnp.sum(dv * yh, axis=0, keepdims=True)
            dy_ref[:, sl] = dyz * sz
            dz_ref[:, sl] = (dyz * yv * _dsilu(zv)).astype(BF16)

    row = pl.BlockSpec((GL, DI), lambda i: (i, 0))
    vec = pl.BlockSpec((1, DI), lambda i: (0, 0))
    return pl.pallas_call(
        body, name="gnorm_bwd", grid=(n // GL,),
        in_specs=[row, row, row, vec, ANY], out_specs=[row, row, vec],
        out_shape=[jax.ShapeDtypeStruct((n, DI), F32), jax.ShapeDtypeStruct((n, DI), BF16),
                   jax.ShapeDtypeStruct((1, DI), F32)],
        compiler_params=_cp("arbitrary"),
    )(dyn, y, proj, w, after)


SL = 1024
SB = 8
SCB = NCH // SB


def _s5_in(proj, bre, bim, after=None):
    n = proj.shape[0]
    uoff = OFF_U // 128

    def body(u_ref, br_ref, bi_ref, *rest):
        or_ref, oi_ref = rest[-2:]
        u = u_ref[...].astype(BF16)
        or_ref[...] = _dot(u, br_ref[...])
        oi_ref[...] = _dot(u, bi_ref[...])

    extra = [] if after is None else [after]
    blk = pl.BlockSpec((None, 128, SCB), lambda i, j: (j, 0, 0))
    out = pl.BlockSpec((SL, SCB), lambda i, j: (i, j))
    return pl.pallas_call(
        body, name="s5_in", grid=(n // SL, SB),
        in_specs=[pl.BlockSpec((SL, 128), lambda i, j: (i, uoff + j)), blk, blk] + [ANY] * len(extra),
        out_specs=[out, out],
        out_shape=[jax.ShapeDtypeStruct((n, NCH), F32)] * 2, compiler_params=_cp("parallel", "parallel"),
    )(proj, bre, bim, *extra)


SC = 1024


def _s5_scan(vre, vim, tab, reverse, name):
    n = vre.shape[0]
    nl = n // SL
    ng = SL // 8
    ti = (lambda i: nl - 1 - i) if reverse else (lambda i: i)

    def body(re_ref, im_ref, tab_ref, ore_ref, oim_ref, cre, cim):
        @pl.when(pl.program_id(1) == 0)
        def _():
            cre[...] = jnp.zeros_like(cre)
            cim[...] = jnp.zeros_like(cim)

        def step(j, carry):
            cr, ci = carry
            jj = (ng - 1 - j) if reverse else j
            rows = pl.ds(pl.multiple_of(jj * 8, 8), 8)
            vr, vi = re_ref[rows, :], im_ref[rows, :]
            for t, k in enumerate((1, 2, 4)):
                sh = (8 - k) if reverse else k
                rr, ri = pltpu.roll(vr, sh, 0), pltpu.roll(vi, sh, 0)
                pr, pi = tab_ref[2 * t], tab_ref[2 * t + 1]
                vr, vi = vr + pr * rr - pi * ri, vi + pr * ri + pi * rr
            lr, li = tab_ref[6], tab_ref[7]
            vr, vi = vr + lr * cr - li * ci, vi + lr * ci + li * cr
            ore_ref[rows, :] = vr
            oim_ref[rows, :] = vi
            e = 0 if reverse else 7
            return (jnp.broadcast_to(vr[e:e + 1, :], (8, SC)), jnp.broadcast_to(vi[e:e + 1, :], (8, SC)))

        cr, ci = lax.fori_loop(0, ng, step, (cre[...], cim[...]))
        cre[...] = cr
        cim[...] = ci

    blk = pl.BlockSpec((SL, SC), lambda j, i: (ti(i), j))
    return pl.pallas_call(
        body, name=name, grid=(NCH // SC, nl),
        in_specs=[blk, blk, pl.BlockSpec((8, 8, SC), lambda j, i: (0, 0, j))], out_specs=[blk, blk],
        out_shape=[jax.ShapeDtypeStruct((n, NCH), F32)] * 2,
        scratch_shapes=[pltpu.VMEM((8, SC), F32), pltpu.VMEM((8, SC), F32)],
        compiler_params=_cp("parallel", "arbitrary"),
    )(vre, vim, tab)


def _s5_out(xre, xim, cre, cimn, proj, dvec):
    n = xre.shape[0]
    uoff = OFF_U // 128

    def body(xr_ref, xi_ref, cr_ref, ci_ref, u_ref, d_ref, y_ref, g_ref):
        y = (_dot(xr_ref[...].astype(BF16), cr_ref[...]) + _dot(xi_ref[...].astype(BF16), ci_ref[...])
             + d_ref[...] * u_ref[...])
        y_ref[...] = y
        g_ref[...] = _gelu(y).astype(BF16)

    xs = pl.BlockSpec((SL, SCB), lambda i, j: (i, j))
    blk = pl.BlockSpec((None, SCB, 128), lambda i, j: (j, 0, 0))
    out = pl.BlockSpec((SL, 128), lambda i, j: (i, j))
    return pl.pallas_call(
        body, name="s5_out", grid=(n // SL, SB),
        in_specs=[xs, xs, blk, blk, pl.BlockSpec((SL, 128), lambda i, j: (i, uoff + j)),
                  pl.BlockSpec((1, 128), lambda i, j: (0, j))],
        out_specs=[out, out],
        out_shape=[jax.ShapeDtypeStruct((n, DS5), F32), jax.ShapeDtypeStruct((n, DS5), BF16)],
        compiler_params=_cp("parallel", "parallel"),
    )(xre, xim, cre, cimn, proj, dvec)


def _s5_out_bwd(dg, ypre, crt, cimnt, proj, dvec, xre, xim):
    n = dg.shape[0]
    uoff = OFF_U // 128
    nl = n // SL

    def body(dg_ref, y_ref, cr_ref, ci_ref, u_ref, d_ref, xr_ref, xi_ref,
             gr_ref, gi_ref, dus_ref, gcr_ref, gci_ref, gd_ref):
        dy = dg_ref[...] * _dgelu(y_ref[...])
        dyb = dy.astype(BF16)
        gr_ref[...] = _dot(dyb, cr_ref[...])
        gi_ref[...] = _dot(dyb, ci_ref[...])
        dus_ref[...] = dy * d_ref[...]

        @pl.when(pl.program_id(1) == 0)
        def _():
            gcr_ref[...] = jnp.zeros_like(gcr_ref)
            gci_ref[...] = jnp.zeros_like(gci_ref)
            gd_ref[...] = jnp.zeros_like(gd_ref)

        gcr_ref[...] += _dot(xr_ref[...].astype(BF16), dyb, TN)
        gci_ref[...] -= _dot(xi_ref[...].astype(BF16), dyb, TN)
        gd_ref[...] += jnp.sum(dy * u_ref[...], axis=0, keepdims=True)

    u128 = pl.BlockSpec((SL, 128), lambda j, i: (i, j))
    xs = pl.BlockSpec((SL, SCB), lambda j, i: (i, j))
    blk = pl.BlockSpec((None, 128, SCB), lambda j, i: (j, 0, 0))
    gblk = pl.BlockSpec((None, SCB, 128), lambda j, i: (j, 0, 0))
    vec = pl.BlockSpec((1, 128), lambda j, i: (0, j))
    return pl.pallas_call(
        body, name="s5_out_bwd", grid=(SB, nl),
        in_specs=[u128, u128, blk, blk, pl.BlockSpec((SL, 128), lambda j, i: (i, uoff + j)), vec, xs, xs],
        out_specs=[xs, xs, u128, gblk, gblk, vec],
        out_shape=[jax.ShapeDtypeStruct((n, NCH), F32)] * 2 + [jax.ShapeDtypeStruct((n, DS5), F32)]
        + [jax.ShapeDtypeStruct((SB, SCB, 128), F32)] * 2 + [jax.ShapeDtypeStruct((1, DS5), F32)],
        compiler_params=_cp("parallel", "arbitrary"),
    )(dg, ypre, crt, cimnt, proj, dvec, xre, xim)


def _s5_in_bwd(are, aim, brt, bit, proj, dus, xre, xim):
    n = are.shape[0]
    uoff = OFF_U // 128
    per = SL // 8

    def body(ar_ref, ai_ref, br_ref, bi_ref, u_ref, dus_ref, xr_ref, xi_ref, pr_ref, pi_ref,
             du_ref, gbr_ref, gbi_ref, glr_ref, gli_ref):
        i = pl.program_id(1)
        ar, ai = ar_ref[...], ai_ref[...]
        arb, aib = ar.astype(BF16), ai.astype(BF16)
        du_ref[...] = (_dot(arb, br_ref[...]) + _dot(aib, bi_ref[...]) + dus_ref[...]).astype(BF16)

        @pl.when(i == 0)
        def _():
            for r in (gbr_ref, gbi_ref, glr_ref, gli_ref):
                r[...] = jnp.zeros_like(r)

        ub = u_ref[...].astype(BF16)
        gbr_ref[...] += _dot(arb, ub, TN)
        gbi_ref[...] += _dot(aib, ub, TN)
        row0 = lax.broadcasted_iota(jnp.int32, (SL, SCB), 0) == 0
        last_r = jnp.where(i > 0, pr_ref[7:8, :], 0.0)
        last_i = jnp.where(i > 0, pi_ref[7:8, :], 0.0)
        xpr = jnp.where(row0, last_r, pltpu.roll(xr_ref[...], 1, 0))
        xpi = jnp.where(row0, last_i, pltpu.roll(xi_ref[...], 1, 0))
        glr_ref[...] += jnp.sum(ar * xpr + ai * xpi, axis=0, keepdims=True)
        gli_ref[...] += jnp.sum(ai * xpr - ar * xpi, axis=0, keepdims=True)

    xs = pl.BlockSpec((SL, SCB), lambda j, i: (i, j))
    prev = pl.BlockSpec((8, SCB), lambda j, i: (jnp.maximum(i * per - 1, 0), j))
    blk = pl.BlockSpec((None, SCB, 128), lambda j, i: (j, 0, 0))
    u128 = pl.BlockSpec((SL, 128), lambda j, i: (i, j))
    vec = pl.BlockSpec((1, SCB), lambda j, i: (0, j))
    return pl.pallas_call(
        body, name="s5_in_bwd", grid=(SB, n // SL),
        in_specs=[xs, xs, blk, blk, pl.BlockSpec((SL, 128), lambda j, i: (i, uoff + j)), u128, xs, xs, prev, prev],
        out_specs=[u128, blk, blk, vec, vec],
        out_shape=[jax.ShapeDtypeStruct((n, DS5), BF16)] + [jax.ShapeDtypeStruct((SB, SCB, 128), F32)] * 2
        + [jax.ShapeDtypeStruct((1, NCH), F32)] * 2,
        compiler_params=_cp("parallel", "arbitrary"),
    )(are, aim, brt, bit, proj, dus, xre, xim, xre, xim)


MC = 1024


def _merge_specs():
    ga = pl.BlockSpec((TL, MC), lambda i, j: (i, OFF_GA // MC + j))
    gb = pl.BlockSpec((TL, MC), lambda i, j: (i, OFF_GB // MC + j))
    col = pl.BlockSpec((TL, MC), lambda i, j: (i, j))
    gate = pl.BlockSpec((TL, MC), lambda i, j: (i, D // MC + j))
    return ga, gb, col, gate


def _merge_fwd(proj, ya, vg):
    n = ya.shape[0]

    def body(ga_ref, gb_ref, ya_ref, v_ref, g_ref, o_ref):
        yb = v_ref[...] * _sig(g_ref[...])
        o_ref[...] = (_sig(ga_ref[...]) * ya_ref[...] + _sig(gb_ref[...]) * yb).astype(BF16)

    ga, gb, col, gate = _merge_specs()
    return pl.pallas_call(
        body, name="merge_fwd", grid=(n // TL, D // MC), in_specs=[ga, gb, col, col, gate], out_specs=col,
        out_shape=jax.ShapeDtypeStruct((n, D), BF16), compiler_params=_cp("parallel", "parallel"),
    )(proj, proj, ya, vg, vg)


def _merge_bwd(dm, proj, ya, vg):
    n = ya.shape[0]

    def body(dm_ref, ga_ref, gb_ref, ya_ref, v_ref, g_ref, dga_ref, dgb_ref, dya_ref, dv_ref, dg_ref):
        d = dm_ref[...]
        sa, sb, sg = _sig(ga_ref[...]), _sig(gb_ref[...]), _sig(g_ref[...])
        v = v_ref[...]
        yb = v * sg
        dga_ref[...] = (d * ya_ref[...] * sa * (1.0 - sa)).astype(BF16)
        dgb_ref[...] = (d * yb * sb * (1.0 - sb)).astype(BF16)
        dya_ref[...] = (d * sa).astype(BF16)
        dyb = d * sb
        dv_ref[...] = (dyb * sg).astype(BF16)
        dg_ref[...] = (dyb * v * sg * (1.0 - sg)).astype(BF16)

    ga, gb, col, gate = _merge_specs()
    o = jax.ShapeDtypeStruct((n, D), BF16)
    return pl.pallas_call(
        body, name="merge_bwd", grid=(n // TL, D // MC), in_specs=[col, ga, gb, col, col, gate],
        out_specs=[col] * 5, out_shape=[o] * 5, compiler_params=_cp("parallel", "parallel"),
    )(dm, proj, proj, ya, vg, vg)


def _adamw_update(wv, gv, mv, vv):
    nm = B1 * mv + (1.0 - B1) * gv
    nv = B2 * vv + (1.0 - B2) * (gv * gv)
    m_hat = nm / (1.0 - B1 ** STEP)
    v_hat = nv / (1.0 - B2 ** STEP)
    return -LR * (m_hat / (jnp.sqrt(v_hat) + AEPS) + WD * wv), nm, nv


def _adamw(w, g, m, v, name):
    r, c = w.shape
    tr = _pick(r, 128)

    def body(w_ref, g_ref, m_ref, v_ref, d_ref, nm_ref, nv_ref):
        d_ref[...], nm_ref[...], nv_ref[...] = _adamw_update(w_ref[...], g_ref[...], m_ref[...], v_ref[...])

    blk = pl.BlockSpec((tr, c), lambda i: (i, 0))
    o = jax.ShapeDtypeStruct((r, c), F32)
    return pl.pallas_call(
        body, name=name, grid=(r // tr,), in_specs=[blk] * 4, out_specs=[blk] * 3, out_shape=[o] * 3,
        compiler_params=_cp("parallel"),
    )(w, g, m, v)


def _adamw_halves(w, g_mine, g_other, m, v, cidx, name):
    _, r, c = w.shape
    hr, gc = g_mine.shape
    tr = _pick(hr, 128)
    nbh = hr // tr
    assert gc == c and 2 * hr - tr < r <= 2 * hr

    def body(cs, w_ref, gm_ref, go_ref, m_ref, v_ref, g_ref, d_ref, nm_ref, nv_ref):
        mine = pl.program_id(0) // nbh == cs[0]
        gv = jnp.where(mine, gm_ref[...], go_ref[...])
        g_ref[...] = gv
        d_ref[...], nm_ref[...], nv_ref[...] = _adamw_update(w_ref[...], gv, m_ref[...], v_ref[...])

    blk = pl.BlockSpec((None, tr, c), lambda i, cs: (0, i, 0))
    gmine = pl.BlockSpec((tr, gc), lambda i, cs: (jnp.where(i // nbh == cs[0], i % nbh, 0), 0))
    gother = pl.BlockSpec((tr, gc), lambda i, cs: (jnp.where(i // nbh == cs[0], 0, i % nbh), 0))
    o = jax.ShapeDtypeStruct((1, r, c), F32)
    return pl.pallas_call(
        body, name=name,
        grid_spec=pltpu.PrefetchScalarGridSpec(num_scalar_prefetch=1, grid=(2 * nbh,),
                                               in_specs=[blk, gmine, gother, blk, blk], out_specs=[blk] * 4),
        out_shape=[o] * 4, compiler_params=_cp("parallel"),
    )(cidx, w, g_mine, g_other, m, v)


def _chip_sum(part, sib, cidx, name):
    _, r, cc = part.shape
    hr = r // 2
    tr = _pick(hr, 256)

    def body(cs, p_ref, s_ref, o_ref):
        o_ref[...] = (p_ref[...].astype(F32) + s_ref[...].astype(F32)).astype(BF16)

    blk = pl.BlockSpec((None, tr, cc), lambda k, i, cs: (k, i, 0))
    return pl.pallas_call(
        body, name=name,
        grid_spec=pltpu.PrefetchScalarGridSpec(
            num_scalar_prefetch=1, grid=(4, hr // tr),
            in_specs=[pl.BlockSpec((None, None, tr, cc), lambda k, i, cs: (k, cs[0], i, 0)), blk], out_specs=blk),
        out_shape=jax.ShapeDtypeStruct((4, hr, cc), BF16), compiler_params=_cp("parallel", "parallel"),
    )(cidx, part.reshape(4, 2, hr, cc), sib)


def _shard_sum(own, got, sidx, name):
    _, hr, cc = own.shape
    tr = _pick(hr, 256)

    def body(cs, own_ref, g0, g1, g2, g3, o_ref):
        acc = None
        for k, g_ref in enumerate((g0, g1, g2, g3)):
            term = jnp.where(cs[0] == k, own_ref[...], g_ref[...]).astype(F32)
            acc = term if acc is None else acc + term
        o_ref[...] = acc

    def got_spec(k):
        return pl.BlockSpec((None, tr, cc), lambda i, cs: (jnp.where(cs[0] == k, (k + 1) % 4, k), i, 0))

    return pl.pallas_call(
        body, name=name,
        grid_spec=pltpu.PrefetchScalarGridSpec(
            num_scalar_prefetch=1, grid=(hr // tr,),
            in_specs=[pl.BlockSpec((None, tr, cc), lambda i, cs: (cs[0], i, 0))] + [got_spec(k) for k in range(4)],
            out_specs=pl.BlockSpec((tr, cc), lambda i, cs: (i, 0))),
        out_shape=jax.ShapeDtypeStruct((hr, cc), F32), compiler_params=_cp("parallel"),
    )(sidx, own, got, got, got, got)


def _sum_slabs(xs, name, out_dtype=F32):
    r, c = xs[0].shape
    tr = _pick(r, 256)

    def body(*refs):
        acc = refs[0][...].astype(F32)
        for ref in refs[1:-1]:
            acc = acc + ref[...].astype(F32)
        refs[-1][...] = acc.astype(out_dtype)

    blk = pl.BlockSpec((tr, c), lambda i: (i, 0))
    return pl.pallas_call(
        body, name=name, grid=(r // tr,), in_specs=[blk] * len(xs), out_specs=blk,
        out_shape=jax.ShapeDtypeStruct((r, c), out_dtype), compiler_params=_cp("parallel"),
    )(*xs)


def _place():
    return lax.axis_index("x"), lax.axis_index("y"), lax.axis_index("c")


def _gather_small(v, after):
    m_per, n = v.shape

    def body(x_ref, _, out_ref, send_sems, recv_sems, local_sem):
        x, y, c = _place()
        me, sibling = (x, y, c), (x, y, 1 - c)
        chips = [(1 - x, y), (x, 1 - y), (1 - x, 1 - y)]

        def rows(px, py, pc):
            return out_ref.at[pl.ds((4 * px + 2 * py + pc) * m_per, m_per), :]

        def copy(k, block, to, src=None):
            return pltpu.make_async_remote_copy(
                src_ref=rows(*block) if src is None else src, dst_ref=rows(*block),
                send_sem=send_sems.at[k], recv_sem=recv_sems.at[k], device_id=to, device_id_type=MESH)

        mine = pltpu.make_async_copy(x_ref, rows(*me), local_sem)
        mine.start()
        first = [copy(0, me, sibling, src=x_ref)]
        first += [copy(1 + j, me, (*chip, c), src=x_ref) for j, chip in enumerate(chips)]
        for cp in first:
            cp.start()
        passed = [copy(4 + j, (*chip, c), sibling) for j, chip in enumerate(chips)]
        for j, chip in enumerate(chips):
            copy(1 + j, (*chip, c), me).wait_recv()
            passed[j].start()
        copy(0, sibling, me).wait_recv()
        for j, chip in enumerate(chips):
            copy(4 + j, (*chip, 1 - c), me).wait_recv()
        for cp in first + passed:
            cp.wait_send()
        mine.wait()

    return pl.pallas_call(
        body, name="gather_small_%d" % m_per,
        out_shape=jax.ShapeDtypeStruct((8 * m_per, n), v.dtype),
        in_specs=[pl.BlockSpec(memory_space=pltpu.VMEM), ANY], out_specs=pl.BlockSpec(memory_space=pltpu.VMEM),
        scratch_shapes=[pltpu.SemaphoreType.DMA((7,)), pltpu.SemaphoreType.DMA((7,)), pltpu.SemaphoreType.DMA],
        compiler_params=pltpu.CompilerParams(vmem_limit_bytes=VMEM_LIMIT),
    )(v, after)


def _allsum_small(v, name, after):
    r = v.shape[0]
    g = _gather_small(v, after)
    return _sum_slabs([g[k * r:(k + 1) * r] for k in range(8)], name)


def _pass_halves(got, shards, name):
    nt = len(got)

    def body(*refs):
        ins, own, outs = refs[:nt], refs[nt:2 * nt], refs[2 * nt:3 * nt]
        send_sems, recv_sems = refs[3 * nt:]
        x, y, c = _place()
        s = 2 * x + y
        chips = [(1 - x, y), (x, 1 - y), (1 - x, 1 - y)]

        def half(ref, t, slot, h):
            hr = ins[t].shape[1] // 2
            return ref.at[slot, pl.ds(h * hr, hr), :]

        def copy(t, j, h):
            px, py = chips[j]
            return pltpu.make_async_remote_copy(
                src_ref=half(ins[t], t, 2 * px + py, h), dst_ref=half(outs[t], t, 2 * px + py, h),
                send_sem=send_sems.at[4 * t + j], recv_sem=recv_sems.at[4 * t + j],
                device_id=(x, y, 1 - c), device_id_type=MESH)

        def whole(t):
            return pltpu.make_async_remote_copy(
                src_ref=own[t], dst_ref=outs[t].at[s], send_sem=send_sems.at[4 * t + 3],
                recv_sem=recv_sems.at[4 * t + 3], device_id=(x, y, 1 - c), device_id_type=MESH)

        sends = [copy(t, j, c) for t in range(nt) for j in range(3)] + [whole(t) for t in range(nt)]
        for cp in sends:
            cp.start()
        for t in range(nt):
            for j in range(3):
                copy(t, j, 1 - c).wait_recv()
            whole(t).wait_recv()
        for cp in sends:
            cp.wait_send()

    return pl.pallas_call(
        body, name=name,
        out_shape=[jax.ShapeDtypeStruct(a.shape, a.dtype) for a in got],
        in_specs=[ANY] * (2 * nt), out_specs=[ANY] * nt, input_output_aliases={t: t for t in range(nt)},
        scratch_shapes=[pltpu.SemaphoreType.DMA((4 * nt,)), pltpu.SemaphoreType.DMA((4 * nt,))],
    )(*got, *shards)


HBM = pl.BlockSpec(memory_space=pltpu.HBM)
SEM = pl.BlockSpec(memory_space=pltpu.SEMAPHORE)
EFFECT = pltpu.SideEffectType.DATAFLOW_SIDE_EFFECTING


PER_TENSOR = {"gather": 3, "scatter": 3, "swap": 1, "pass": 4, "whole": 1}


def _ici_copies(kind, srcs, lands, send_sems, recv_sems):
    x, y, c = _place()
    s = 2 * x + y
    sib = (x, y, 1 - c)
    chips = [(1 - x, y), (x, 1 - y), (1 - x, 1 - y)]
    cps = []

    def add(src, dst, dev):
        k = len(cps)
        cps.append(pltpu.make_async_remote_copy(src_ref=src, dst_ref=dst, send_sem=send_sems[k], recv_sem=recv_sems[k],
                                                device_id=dev, device_id_type=MESH))

    for t in range(len(srcs)):
        if kind == "gather":
            hr = srcs[t].shape[0] // 2
            for px, py in chips:
                add(srcs[t].at[pl.ds(c * hr, hr), :], lands[t].at[s, pl.ds(c * hr, hr), :], (px, py, c))
        elif kind == "scatter":
            for px, py in chips:
                add(srcs[t].at[2 * px + py], lands[t].at[s], (px, py, c))
        elif kind == "swap":
            hr = srcs[t].shape[1] // 2
            add(srcs[t].at[:, pl.ds((1 - c) * hr, hr), :], lands[t], sib)
        elif kind == "pass":
            hr = srcs[t].shape[1] // 2
            for px, py in chips:
                half = srcs[t].at[2 * px + py, pl.ds(c * hr, hr), :]
                add(half, half, sib)
            add(lands[t], srcs[t].at[s], sib)
        else:
            add(srcs[t], lands[t], sib)
    return cps


def _ici_start(kind, srcs, after, name, lands=None):
    nt = len(srcs)
    nc = PER_TENSOR[kind] * nt
    hbm = lambda a: pltpu.with_memory_space_constraint(a, pltpu.HBM)
    if lands is None:
        shape = {"gather": lambda a: (4,) + a.shape, "scatter": lambda a: a.shape,
                 "swap": lambda a: (4, a.shape[1] // 2, a.shape[2]), "whole": lambda a: a.shape}[kind]
        lands = [lax.empty(shape(a), a.dtype) for a in srcs]

    def body(*refs):
        src, land = refs[:nt], refs[nt:2 * nt]
        outs = refs[2 * nt + 1:]
        for cp in _ici_copies(kind, src, land, outs[:nc], outs[nc:2 * nc]):
            cp.start()
        outs[-1][...] = jnp.zeros_like(outs[-1])

    outs = pl.pallas_call(
        body, name=name,
        out_shape=tuple([pltpu.SemaphoreType.DMA(())] * (2 * nc) + [pltpu.HBM(a.shape, a.dtype) for a in srcs]
                        + [pltpu.HBM(a.shape, a.dtype) for a in lands] + [jax.ShapeDtypeStruct((8, 128), F32)]),
        in_specs=[HBM] * (2 * nt) + [ANY],
        out_specs=tuple([SEM] * (2 * nc) + [HBM] * (2 * nt) + [pl.BlockSpec(memory_space=pltpu.VMEM)]),
        input_output_aliases={i: 2 * nc + i for i in range(2 * nt)},
        compiler_params=pltpu.CompilerParams(has_side_effects=EFFECT),
    )(*[hbm(a) for a in srcs], *[hbm(a) for a in lands], after)
    return outs[:2 * nc], outs[2 * nc:2 * nc + nt], outs[2 * nc + nt:2 * nc + 2 * nt], outs[-1]


def _ici_wait(kind, sems, srcs, lands, after, name):
    nt = len(srcs)
    nc = PER_TENSOR[kind] * nt

    def body(*refs):
        src, land = refs[:nt], refs[nt:2 * nt]
        sem = refs[2 * nt:2 * nt + 2 * nc]
        for cp in _ici_copies(kind, src, land, sem[:nc], sem[nc:]):
            cp.wait_send()
            cp.wait_recv()

    outs = pl.pallas_call(
        body, name=name,
        out_shape=tuple(pltpu.HBM(a.shape, a.dtype) for a in list(srcs) + list(lands)),
        in_specs=[HBM] * (2 * nt) + [SEM] * (2 * nc) + [ANY], out_specs=tuple([HBM] * (2 * nt)),
        input_output_aliases={i: i for i in range(2 * nt)},
        compiler_params=pltpu.CompilerParams(has_side_effects=EFFECT),
    )(*srcs, *lands, *sems, after)
    return outs[:nt], outs[nt:]


def _s5_params(lam_re, lam_im, log_dt, b_re, b_im):
    lr = jnp.minimum(lam_re, EIG_MAX)
    dt = jnp.exp(log_dt)[:, None]
    mag = jnp.exp(lr * dt)
    lbr, lbi = mag * jnp.cos(lam_im * dt), mag * jnp.sin(lam_im * dt)
    den = lr * lr + lam_im * lam_im
    qr = ((lbr - 1.0) * lr + lbi * lam_im) / den
    qi = (lbi * lr - (lbr - 1.0) * lam_im) / den
    bbr = qr[..., None] * b_re - qi[..., None] * b_im
    bbi = qr[..., None] * b_im + qi[..., None] * b_re
    return lbr, lbi, bbr, bbi


def _cmul(a, b):
    return a[0] * b[0] - a[1] * b[1], a[0] * b[1] + a[1] * b[0]


def _scan_table(lr, li, reverse):
    l1 = (lr.reshape(1, NCH), li.reshape(1, NCH))
    pows = [l1]
    for _ in range(7):
        pows.append(_cmul(pows[-1], l1))
    r = jnp.arange(8)[:, None]
    tabs = []
    for k in (1, 2, 4):
        keep = (r < 8 - k) if reverse else (r >= k)
        tabs += [jnp.where(keep, pows[k - 1][0], 0.0), jnp.where(keep, pows[k - 1][1], 0.0)]
    order = range(7, -1, -1) if reverse else range(8)
    tabs += [jnp.concatenate([pows[e][0] for e in order], axis=0), jnp.concatenate([pows[e][1] for e in order], axis=0)]
    return jnp.stack(tabs).astype(F32)


_EYE8 = lambda: jnp.eye(8, dtype=F32)


def _to_in_blocks(b):
    return jnp.einsum("jgpc,gh->jgchp", b.reshape(8, 8, 64, 16), _EYE8()).reshape(8, 128, 512)


def _to_out_blocks(cm):
    return jnp.einsum("jgcp,gh->jgphc", cm.reshape(8, 8, 16, 64), _EYE8()).reshape(8, 512, 128)


def _from_out_blocks(g):
    return jnp.einsum("jgphc,gh->jgpc", g.reshape(8, 8, 64, 8, 16), _EYE8()).reshape(64, 64, 16)


def _local_step(x, target, hn1, proj, p, hooks):
    n = x.shape[0]
    g = {}
    dtraw = proj
    xbc = _conv_a_fwd(proj, p["conv_a_w"], p["conv_a_b"])
    to_lanes = lambda v: jnp.pad(jnp.pad(v.reshape(NG, HPG), ((0, 0), (0, 8 - HPG))).reshape(1, 8 * NG),
                                 ((0, 0), (0, HL - 8 * NG)))
    from_lanes = lambda v: v[:, :8 * NG].reshape(-1, NG, 8)[:, :, :HPG].reshape(-1, NG * HPG)
    hp = jnp.concatenate([to_lanes(p["dt_bias"]), to_lanes(p["a_log"]), jnp.zeros((6, HL), F32)], axis=0)
    lane = jnp.arange(HL)[:, None]
    emat = ((lane < 8 * NG) & (lane % 8 < HPG)
            & (jnp.arange(DI)[None, :] // HD == HPG * (lane // 8) + lane % 8)).astype(BF16)
    dexp = jnp.repeat(p["d_a"].reshape(1, NG * HPG), HD, axis=1)
    dt, s_cum, s_t, dt_e, s_e = _ssd_prep(dtraw, hp, emat)
    s8 = s_cum[:, :8 * NG].reshape(n, NG, 8).transpose(1, 0, 2)
    yssd, sprev = _ssd_fwd(xbc, s8, s_t, dt_e, s_e, dexp)
    yn = _gnorm_fwd(yssd, proj, p["norm_a_w"])
    tok = hooks["late_start"](yn)
    (lbr, lbi, bbr, bbi), s5_vjp = jax.vjp(_s5_params, p["s5_lam_re"], p["s5_lam_im"], p["s5_log_dt"],
                                           p["s5_b_re"], p["s5_b_im"])
    bin_r, bin_i = _to_in_blocks(bbr), _to_in_blocks(bbi)
    cout_r, cout_in = _to_out_blocks(p["s5_c_re"]), _to_out_blocks(-p["s5_c_im"])
    bur, bui = _s5_in(proj, bin_r.astype(BF16), bin_i.astype(BF16), after=tok)
    xre, xim = _s5_scan(bur, bui, _scan_table(lbr, lbi, False), False, "s5_scan_fwd")
    ypre, g5 = _s5_out(xre, xim, cout_r.astype(BF16), cout_in.astype(BF16), proj, p["s5_d"])
    p = {**p, **hooks["late_weights"](ypre)}
    ya = _matmul(yn, p["w_proj_a"], "nn", "mm_proj")
    vg = _matmul(g5, p["w_s5_glu"], "nn", "mm_glu", b_stacked=True)
    merged = _merge_fwd(proj, ya, vg)
    h1 = _matmul(merged, p["w_out"], "nn", "mm_out", residual=x)
    hn2 = _rms_fwd(h1, p["norm_ffn_w"], "rms_ffn")
    up = _matmul(hn2, p["w_up"], "nn", "mm_up", tn=1408, b_stacked=True)
    act = _conv_ffn_fwd(up, p["conv_ffn_w"], p["conv_ffn_b"])
    h2 = _matmul(act, p["w_down"], "nn", "mm_down", tk=DFF // 2, residual=h1)
    dh2, dh2b, g["norm_final_w"], loss_blk = _final(h2, p["norm_final_w"], target)
    g["w_down"] = _matmul(act, dh2b, "tn", "mm_gw_down", out_dtype=BF16, tm=DFF // 4).reshape(4, DFF // 4, D)
    dact = _matmul(dh2b, p["w_down"], "nt", "mm_dact", out_dtype=BF16, tn=DFF // 4)
    dup, g["conv_ffn_w"], g["conv_ffn_b"] = _conv_ffn_bwd(up, dact, p["conv_ffn_w"], p["conv_ffn_b"])
    g["w_up"] = _matmul(hn2, dup, "tn", "mm_gw_up", out_dtype=BF16, tn=1408, out_stacked=True)
    tok = hooks["swap_start"](["w_up", "w_down"], g, "s1")
    dhn2 = _matmul(dup, p["w_up"], "nt", "mm_dhn2", tk=2816, b_stacked=True, after=tok)
    tok = hooks["scatter_go"]("s1", dhn2)
    dh1, dh1b, g["norm_ffn_w"] = _rms_bwd(dhn2, h1, p["norm_ffn_w"], dh2, "rms_ffn_bwd", after=tok)
    g["w_out"] = _matmul(merged, dh1b, "tn", "mm_gw_out", out_dtype=BF16).reshape(4, D // 4, D)
    dmerged = _matmul(dh1b, p["w_out"], "nt", "mm_dmerged")
    dga, dgb, dya, dval, dgate = _merge_bwd(dmerged, proj, ya, vg)
    dvg = jnp.concatenate([dval, dgate], axis=1)
    g["w_s5_glu"] = _matmul(g5, dvg, "tn", "mm_gw_glu", out_dtype=BF16, out_stacked=True)
    dg5 = _matmul(dvg, p["w_s5_glu"], "nt", "mm_dg5", b_stacked=True)
    tr = lambda b: b.transpose(0, 2, 1)
    gxr, gxi, dus, gcr, gci, g["s5_d"] = _s5_out_bwd(dg5, ypre, tr(cout_r).astype(BF16), tr(cout_in).astype(BF16),
                                                     proj, p["s5_d"], xre, xim)
    are, aim = _s5_scan(gxr, gxi, _scan_table(lbr, -lbi, True), True, "s5_scan_bwd")
    du, gbr, gbi, glr, gli = _s5_in_bwd(are, aim, tr(bin_r).astype(BF16), tr(bin_i).astype(BF16), proj, dus, xre, xim)
    g["s5_c_re"] = _from_out_blocks(gcr).transpose(0, 2, 1)
    g["s5_c_im"] = _from_out_blocks(gci).transpose(0, 2, 1)
    (g["s5_lam_re"], g["s5_lam_im"], g["s5_log_dt"], g["s5_b_re"], g["s5_b_im"]) = s5_vjp(
        (glr.reshape(64, 64), gli.reshape(64, 64), _from_out_blocks(gbr), _from_out_blocks(gbi)))
    g["w_proj_a"] = _matmul(yn, dya, "tn", "mm_gw_proj", out_dtype=BF16).reshape(4, DI // 4, D)
    tok = hooks["swap_start"](["w_proj_a", "w_s5_glu", "w_out"], g, "s2")
    dyn = _matmul(dya, p["w_proj_a"], "nt", "mm_dyn", after=tok)
    tok = hooks["scatter_go"]("s2", dyn)
    dyssd, dz, g["norm_a_w"] = _gnorm_bwd(dyn, yssd, proj, p["norm_a_w"], tok)
    dxs, dbm, dcm, ds_e, ddt_e, tsum, dsh8, pd = _ssd_bwd(xbc, s8, s_t, dt_e, s_e, dexp, sprev, dyssd)
    dsh = jnp.pad(dsh8.transpose(1, 0, 2).reshape(n, 8 * NG), ((0, 0), (0, HL - 8 * NG)))
    draw, ps = _ssd_post(ds_e, ddt_e, tsum, dsh, dtraw, dt, hp, emat.T)
    g["dt_bias"] = from_lanes(ps[0:1])
    g["a_log"] = from_lanes(ps[1:2])
    g["d_a"] = pd.reshape(NG * HPG, HD).sum(axis=1).reshape(1, NG * HPG)
    ddt = draw.astype(BF16)
    dxbc_parts, gcw, gcb = [], [], []
    for arr, col0, nm in ((dxs, 0, "conv_a_bwd_x"), (dbm, DI, "conv_a_bwd_b"), (dcm, DI + NG * NS, "conv_a_bwd_c")):
        dpart, gw_, gb_ = _conv_a_bwd(proj, arr, p["conv_a_w"], p["conv_a_b"], col0, nm)
        dxbc_parts.append(dpart)
        gcw.append(gw_)
        gcb.append(gb_)
    g["conv_a_w"] = jnp.concatenate(gcw, axis=1)
    g["conv_a_b"] = jnp.concatenate(gcb, axis=1)
    dproj = jnp.concatenate([dz] + dxbc_parts + [du, dga, dgb, ddt], axis=1)
    g_main = _matmul(dproj, hn1, "tn", "mm_gw_in", out_dtype=BF16, tm=896, tn=2048)
    g_dt = g_main[NMAIN:NMAIN + 8 * NG].reshape(NG, 8, D)[:, :HPG].reshape(NG * HPG, D)
    g_sh = _move_rows(g_main, RUNS_TO_SHARDS, 4 * WPAD, MT, MT, "rows_to_shards")
    g["w_in"] = lax.dynamic_update_slice(g_sh, g_dt, (DT_SHARD_ROW, 0)).reshape(4, WPAD, D)
    hooks["swap_start"](["w_in"], g, "s3")
    tok = hooks["scatter_go"]("s3", g_dt)
    dhn1 = _matmul(dproj, p["w_full"], "nn", "mm_dhn1", tk=4480, after=tok)
    gx, _, g["norm_mix_w"] = _rms_bwd(dhn1, x, p["norm_mix_w"], dh1, "rms_mix_bwd")
    return loss_blk, gx, g


BIG = ["w_in", "w_proj_a", "w_s5_glu", "w_out", "w_up", "w_down"]
SMALL = ["norm_mix_w", "conv_a_w", "conv_a_b", "dt_bias", "a_log", "d_a", "norm_a_w", "s5_lam_re", "s5_lam_im",
         "s5_log_dt", "s5_b_re", "s5_b_im", "s5_c_re", "s5_c_im", "s5_d", "norm_ffn_w", "conv_ffn_w", "conv_ffn_b",
         "norm_final_w"]
ORDER = ["norm_mix_w", "w_in", "conv_a_w", "conv_a_b", "dt_bias", "a_log", "d_a", "norm_a_w", "w_proj_a", "s5_lam_re",
         "s5_lam_im", "s5_log_dt", "s5_b_re", "s5_b_im", "s5_c_re", "s5_c_im", "s5_d", "w_s5_glu", "w_out",
         "norm_ffn_w", "w_up", "conv_ffn_w", "conv_ffn_b", "w_down", "norm_final_w"]
CONV_FULL = {"conv_a_w": (KA, CONVD), "conv_ffn_w": (KF, 2 * DFF)}


def _pack(arrs):
    flat = jnp.concatenate([a.reshape(-1).astype(F32) for a in arrs])
    total = flat.shape[0]
    padded = -(-total // 1024) * 1024
    return jnp.pad(flat, (0, padded - total)).reshape(padded // 128, 128)


def _unpack(block, shapes):
    flat = block.reshape(-1)
    out, at = [], 0
    for sh in shapes:
        size = math.prod(sh)
        out.append(flat[at:at + size].reshape(sh))
        at += size
    return out


def kernel(x, norm_mix_w, w_in, conv_a_w, conv_a_b, dt_bias, a_log, d_a, norm_a_w, w_proj_a, s5_lam_re, s5_lam_im, s5_log_dt, s5_b_re, s5_b_im, s5_c_re, s5_c_im, s5_d, w_s5_glu, w_out, norm_ffn_w, w_up, conv_ffn_w, conv_ffn_b, w_down, norm_final_w, loss_target, m_norm_mix_w, m_w_in, m_conv_a_w, m_conv_a_b, m_dt_bias, m_a_log, m_d_a, m_norm_a_w, m_w_proj_a, m_s5_lam_re, m_s5_lam_im, m_s5_log_dt, m_s5_b_re, m_s5_b_im, m_s5_c_re, m_s5_c_im, m_s5_d, m_w_s5_glu, m_w_out, m_norm_ffn_w, m_w_up, m_conv_ffn_w, m_conv_ffn_b, m_w_down, m_norm_final_w, v_norm_mix_w, v_w_in, v_conv_a_w, v_conv_a_b, v_dt_bias, v_a_log, v_d_a, v_norm_a_w, v_w_proj_a, v_s5_lam_re, v_s5_lam_im, v_s5_log_dt, v_s5_b_re, v_s5_b_im, v_s5_c_re, v_s5_c_im, v_s5_d, v_w_s5_glu, v_w_out, v_norm_ffn_w, v_w_up, v_conv_ffn_w, v_conv_ffn_b, v_w_down, v_norm_final_w):
    w = dict(norm_mix_w=norm_mix_w, w_in=w_in, conv_a_w=conv_a_w, conv_a_b=conv_a_b, dt_bias=dt_bias, a_log=a_log, d_a=d_a, norm_a_w=norm_a_w, w_proj_a=w_proj_a, s5_lam_re=s5_lam_re, s5_lam_im=s5_lam_im, s5_log_dt=s5_log_dt, s5_b_re=s5_b_re, s5_b_im=s5_b_im, s5_c_re=s5_c_re, s5_c_im=s5_c_im, s5_d=s5_d, w_s5_glu=w_s5_glu, w_out=w_out, norm_ffn_w=norm_ffn_w, w_up=w_up, conv_ffn_w=conv_ffn_w, conv_ffn_b=conv_ffn_b, w_down=w_down, norm_final_w=norm_final_w)
    m = dict(norm_mix_w=m_norm_mix_w, w_in=m_w_in, conv_a_w=m_conv_a_w, conv_a_b=m_conv_a_b, dt_bias=m_dt_bias, a_log=m_a_log, d_a=m_d_a, norm_a_w=m_norm_a_w, w_proj_a=m_w_proj_a, s5_lam_re=m_s5_lam_re, s5_lam_im=m_s5_lam_im, s5_log_dt=m_s5_log_dt, s5_b_re=m_s5_b_re, s5_b_im=m_s5_b_im, s5_c_re=m_s5_c_re, s5_c_im=m_s5_c_im, s5_d=m_s5_d, w_s5_glu=m_w_s5_glu, w_out=m_w_out, norm_ffn_w=m_norm_ffn_w, w_up=m_w_up, conv_ffn_w=m_conv_ffn_w, conv_ffn_b=m_conv_ffn_b, w_down=m_w_down, norm_final_w=m_norm_final_w)
    v = dict(norm_mix_w=v_norm_mix_w, w_in=v_w_in, conv_a_w=v_conv_a_w, conv_a_b=v_conv_a_b, dt_bias=v_dt_bias, a_log=v_a_log, d_a=v_d_a, norm_a_w=v_norm_a_w, w_proj_a=v_w_proj_a, s5_lam_re=v_s5_lam_re, s5_lam_im=v_s5_lam_im, s5_log_dt=v_s5_log_dt, s5_b_re=v_s5_b_re, s5_b_im=v_s5_b_im, s5_c_re=v_s5_c_re, s5_c_im=v_s5_c_im, s5_d=v_s5_d, w_s5_glu=v_w_s5_glu, w_out=v_w_out, norm_ffn_w=v_norm_ffn_w, w_up=v_w_up, conv_ffn_w=v_conv_ffn_w, conv_ffn_b=v_conv_ffn_b, w_down=v_w_down, norm_final_w=v_norm_final_w)
    xi, yi, ci = _place()
    chip = 2 * xi + yi

    cidx = jnp.reshape(ci, (1,)).astype(jnp.int32)
    sidx = jnp.reshape(chip, (1,)).astype(jnp.int32)

    tw = lambda a: jnp.transpose(a[0])[None]
    w["w_in"], m["w_in"], v["w_in"] = tw(w_in), tw(m_w_in), tw(v_w_in)
    shards = [w[k][0].astype(BF16) for k in BIG]
    shards[0] = jnp.pad(shards[0], ((0, WPAD - WSH), (0, 0)))

    late = {}

    def late_start(after):
        srcs, got = _ici_wait("gather", g_sems, g_srcs, g_lands, after, "gather_rest_wait")
        late["sems"], late["got"], late["srcs"], tok = _ici_start("pass", list(got), cidx, "pass_rest_start",
                                                                  lands=list(srcs))
        return tok

    def late_weights(after):
        full, _ = _ici_wait("pass", late["sems"], late["got"], late["srcs"], after, "pass_rest_wait")
        return {"w_proj_a": full[0].reshape(DI, D), "w_s5_glu": full[1], "w_out": full[2].reshape(D, D),
                "w_up": full[3], "w_down": full[4].reshape(DFF, D)}

    swaps, pending = {}, []

    def swap_start(names, g, tag):
        sems, parts, lands, tok = _ici_start("swap", [g[k] for k in names], cidx, "swap_start_" + tag)
        swaps[tag] = (names, sems, parts, lands)
        return tok

    def scatter_go(tag, after):
        names, sems, parts, lands = swaps[tag]
        parts, sib = _ici_wait("swap", sems, parts, lands, after, "swap_wait_" + tag)
        sums = [_chip_sum(parts[t], sib[t], cidx, "chip_sum_" + k) for t, k in enumerate(names)]
        sems, srcs, lands, tok = _ici_start("scatter", sums, cidx, "scatter_start_" + tag)
        pending.append((names, tag, sems, srcs, lands))
        return tok

    hooks = {"late_start": late_start, "late_weights": late_weights, "swap_start": swap_start,
             "scatter_go": scatter_go}
    conv_blocks = []
    for k, (taps, cols) in CONV_FULL.items():
        shard = jnp.where(ci == 0, w[k][0], 0.0)
        conv_blocks.append(lax.dynamic_update_slice_in_dim(jnp.zeros((taps, cols), F32), shard, chip * (cols // 4), 1))
    conv_full = _unpack(_allsum_small(_pack(conv_blocks), "sum_conv_w", cidx), [CONV_FULL[k] for k in CONV_FULL])

    half = D // 2
    sh_a, sh_b = shards[0][:, :half], shards[0][:, half:]
    a_sems, a_srcs, a_lands, a_tok = _ici_start("gather", [sh_a], conv_full[0], "gather_in_a_start")
    b_sems, b_srcs, b_lands, b_tok = _ici_start("gather", [sh_b], a_tok, "gather_in_b_start")
    hn1 = _rms_fwd(x[0], norm_mix_w, "rms_mix", after=b_tok)

    def w_in_part(sems, srcs, lands, after, tag):
        srcs, got = _ici_wait("gather", sems, srcs, lands, after, "gather_in_%s_wait" % tag)
        w_sh = _pass_halves(list(got), list(srcs), "pass_halves_in_" + tag)[0].reshape(4 * WPAD, half)
        w_dt = jnp.pad(w_sh[DT_SHARD_ROW:DT_SHARD_ROW + NG * HPG].reshape(NG, HPG, half),
                       ((0, 0), (0, 8 - HPG), (0, 0)))
        return lax.dynamic_update_slice(_move_rows(w_sh, RUNS_TO_MAIN, NFULL, MT, MT, "rows_to_main_" + tag),
                                        w_dt.reshape(8 * NG, half), (NMAIN, 0))

    w_a = w_in_part(a_sems, a_srcs, a_lands, hn1, "a")
    proj_a = _matmul(hn1[:, :half], w_a, "nt", "mm_in_a", tn=1920)
    w_b = w_in_part(b_sems, b_srcs, b_lands, proj_a, "b")
    g_sems, g_srcs, g_lands, token = _ici_start("gather", shards[1:], w_b, "gather_rest_start")
    proj = _matmul(hn1[:, half:], w_b, "nt", "mm_in_b", tn=1920, residual=proj_a, after=token)
    w_full = jnp.concatenate([w_a, w_b], axis=1)
    p = {
        "w_full": w_full,
        "conv_a_w": conv_full[0], "conv_ffn_w": conv_full[1],
        "conv_a_b": conv_a_b, "conv_ffn_b": conv_ffn_b,
        "norm_mix_w": norm_mix_w, "norm_a_w": norm_a_w, "norm_ffn_w": norm_ffn_w,
        "norm_final_w": norm_final_w.reshape(1, D),
        "dt_bias": dt_bias, "a_log": a_log, "d_a": d_a, "s5_d": s5_d,
        "s5_lam_re": s5_lam_re[0], "s5_lam_im": s5_lam_im[0], "s5_log_dt": s5_log_dt[0],
        "s5_b_re": s5_b_re[0], "s5_b_im": s5_b_im[0], "s5_c_re": s5_c_re[0], "s5_c_im": s5_c_im[0],
    }
    loss_blk, gx, g = _local_step(x[0], loss_target[0], hn1, proj, p, hooks)

    after, halves = gx, {}
    for names, tag, sems, srcs, lands in pending:
        srcs, got = _ici_wait("scatter", sems, srcs, lands, after, "scatter_wait_" + tag)
        for t, k in enumerate(names):
            halves[k] = _shard_sum(srcs[t], got[t], sidx, "shard_sum_" + k)
        after = halves[names[0]]
    w_sems, g_mine, w_lands, w_tok = _ici_start("whole", [halves[k] for k in BIG], cidx, "whole_start")

    small_shapes = [CONV_FULL.get(k, w[k].shape[1:] if k != "norm_final_w" else w[k].shape) for k in SMALL]
    small = _allsum_small(_pack([g[k] for k in SMALL] + [loss_blk[0:1, 0:1]]), "sum_small_grads", w_tok)
    small_grads = dict(zip(SMALL + ["loss"], _unpack(small, small_shapes + [(1,)])))
    for k, (taps, cols) in CONV_FULL.items():
        small_grads[k] = lax.dynamic_slice_in_dim(small_grads[k], chip * (cols // 4), cols // 4, axis=1)
    loss = small_grads.pop("loss").reshape(())

    grads, delta, new_m, new_v = {}, {}, {}, {}
    for k in SMALL:
        grads[k] = small_grads[k].reshape(w[k].shape)
    pk = lambda t: _pack([t[k] for k in SMALL])
    d_, m_, v_ = _adamw(pk(w), pk(grads), pk(m), pk(v), "adamw_small")
    shapes = [w[k].shape for k in SMALL]
    for k, dd, mm, vv in zip(SMALL, _unpack(d_, shapes), _unpack(m_, shapes), _unpack(v_, shapes)):
        delta[k], new_m[k], new_v[k] = dd, mm, vv
    g_mine, g_other = _ici_wait("whole", w_sems, g_mine, w_lands, d_, "whole_wait")
    for t, k in enumerate(BIG):
        outs = _adamw_halves(w[k], g_mine[t], g_other[t], m[k], v[k], cidx, "adamw_" + k)
        grads[k], delta[k], new_m[k], new_v[k] = [tw(o) for o in outs] if k == "w_in" else outs
    return (loss, gx[None], *[grads[k] for k in ORDER], *[delta[k] for k in ORDER],
            *[new_m[k] for k in ORDER], *[new_v[k] for k in ORDER])
```

```python
import functools
import math

import jax
import jax.numpy as jnp
from jax import lax
from jax.experimental import pallas as pl
from jax.experimental.pallas import tpu as pltpu

F32 = jnp.float32
BF16 = jnp.bfloat16
HI = lax.Precision.HIGHEST
MESH = pl.DeviceIdType.MESH
ANY = pl.BlockSpec(memory_space=pl.ANY)

D = 2048
DI = 3072
HD = 64
NG = 8
HPG = 6
GW = HPG * HD
NS = 128
KA = 4
Q = 256
CONVD = DI + 2 * NG * NS
DS5 = 1024
NCH = 4096
DFF = 5632
KF = 3
EPS = 1e-6
EIG_MAX = -1e-4
NMAIN = 13312
OFF_XBC, OFF_U, OFF_GA, OFF_GB = 3072, 8192, 9216, 11264
WSH = 3340
WPAD = 3360
IN_SPLIT = [DI, DI + CONVD, DI + CONVD + NG * HPG]
NFULL = NMAIN + 128
MT = 336


def _w_in_runs():
    runs = []
    for k in range(4):
        for o_lo, o_hi, m_lo in ((0, IN_SPLIT[1], 0), (IN_SPLIT[2], 4 * WSH, IN_SPLIT[1])):
            lo, hi = max(o_lo, WSH * k), min(o_hi, WSH * (k + 1))
            if lo < hi:
                runs.append((m_lo + lo - o_lo, m_lo + hi - o_lo, WPAD * k + lo - WSH * k))
    return runs


RUNS_TO_MAIN = _w_in_runs()
RUNS_TO_SHARDS = [(s_lo, s_lo + m_hi - m_lo, m_lo) for m_lo, m_hi, s_lo in RUNS_TO_MAIN]
DT_SHARD_ROW = WPAD * (IN_SPLIT[1] // WSH) + IN_SPLIT[1] % WSH
assert IN_SPLIT[1] // WSH == (IN_SPLIT[2] - 1) // WSH
VMEM_LIMIT = 56 * 1024 * 1024

LR, B1, B2, AEPS, WD, STEP = 0.001, 0.9, 0.999, 1e-08, 0.01, 10


def _cp(*sem):
    return pltpu.CompilerParams(dimension_semantics=sem, vmem_limit_bytes=VMEM_LIMIT)


def _sig(x):
    return jax.nn.sigmoid(x)


def _silu(x):
    return x * _sig(x)


def _dsilu(x):
    s = _sig(x)
    return s * (1.0 + x * (1.0 - s))


def _softplus(x):
    return jnp.maximum(x, 0.0) + jnp.log(1.0 + jnp.exp(-jnp.abs(x)))


_GC = math.sqrt(2.0 / math.pi)


def _gelu(x):
    return 0.5 * x * (1.0 + jnp.tanh(_GC * (x + 0.044715 * x * x * x)))


def _dgelu(x):
    t = jnp.tanh(_GC * (x + 0.044715 * x * x * x))
    return 0.5 * (1.0 + t) + 0.5 * x * (1.0 - t * t) * _GC * (1.0 + 3.0 * 0.044715 * x * x)


def _dot(a, b, dims=((1,), (0,)), prec=None):
    return lax.dot_general(a, b, (dims, ((), ())), precision=prec, preferred_element_type=F32)


NT = ((1,), (1,))
TN = ((0,), (0,))


def _pick(n, t):
    for unit in (128, 8):
        for cand in range(min(n, t) // unit * unit, 0, -unit):
            if n % cand == 0:
                return cand
    return n


def _matmul(a, b, mode, name, out_dtype=F32, tm=1024, tn=1024, tk=2048, residual=None, b_stacked=False,
            out_stacked=False, after=None):
    if b_stacked:
        _, brows, bn = b.shape
        bshape = (brows, 4 * bn)
    else:
        bshape = b.shape
    if mode == "nn":
        (m, k), (k2, n) = a.shape, bshape
    elif mode == "nt":
        (m, k), (n, k2) = a.shape, bshape
    else:
        (k, m), (k2, n) = a.shape, bshape
    assert k == k2
    tm = _pick(m, tm)
    tn = _pick(n // 4 if (out_stacked or (b_stacked and mode != "nt")) else n, tn)
    tk = _pick(k // 4 if (b_stacked and mode == "nt") else k, tk)
    nk = k // tk
    dims = {"nn": ((1,), (0,)), "nt": NT, "tn": TN}[mode]
    has_res = residual is not None
    n_in = 2 + has_res + (after is not None)

    def body(*refs):
        a_ref, b_ref = refs[0], refs[1]
        r_ref = refs[2] if has_res else None
        o_ref = refs[n_in]
        p = _dot(a_ref[...], b_ref[...], dims)

        def finish(r):
            if has_res:
                r = r + r_ref[...]
            o_ref[...] = r.astype(out_dtype)

        if nk == 1:
            finish(p)
        else:
            acc = refs[-1]
            kk = pl.program_id(2)

            @pl.when(kk == 0)
            def _():
                acc[...] = p

            @pl.when(kk > 0)
            def _():
                acc[...] += p

            @pl.when(kk == nk - 1)
            def _():
                finish(acc[...])

    if mode == "tn":
        a_spec = pl.BlockSpec((tk, tm), lambda i, j, kk: (kk, i))
    else:
        a_spec = pl.BlockSpec((tm, tk), lambda i, j, kk: (i, kk))
    if mode == "nt":
        if b_stacked:
            per = bn // tk
            b_spec = pl.BlockSpec((None, tn, tk), lambda i, j, kk: (kk // per, j, kk % per))
        else:
            b_spec = pl.BlockSpec((tn, tk), lambda i, j, kk: (j, kk))
    elif b_stacked:
        per = bn // tn
        b_spec = pl.BlockSpec((None, tk, tn), lambda i, j, kk: (j // per, kk, j % per))
    else:
        b_spec = pl.BlockSpec((tk, tn), lambda i, j, kk: (kk, j))
    o_spec = pl.BlockSpec((tm, tn), lambda i, j, kk: (i, j))
    if out_stacked:
        per_o = n // 4 // tn
        out_spec = pl.BlockSpec((None, tm, tn), lambda i, j, kk: (j // per_o, i, j % per_o))
        out_shape = jax.ShapeDtypeStruct((4, m, n // 4), out_dtype)
    else:
        out_spec, out_shape = o_spec, jax.ShapeDtypeStruct((m, n), out_dtype)
    in_specs, args = [a_spec, b_spec], [a, b]
    if has_res:
        in_specs.append(o_spec)
        args.append(residual)
    if after is not None:
        in_specs.append(ANY)
        args.append(after)
    return pl.pallas_call(
        body, name=name, grid=(m // tm, n // tn, nk),
        in_specs=in_specs, out_specs=out_spec, out_shape=out_shape,
        scratch_shapes=[pltpu.VMEM((tm, tn), F32)] if nk > 1 else [],
        compiler_params=_cp("parallel", "parallel", "arbitrary"),
    )(*args)


def _move_rows(src, runs, rows_out, t_out, t_in, name):
    rows_in, cols = src.shape
    nb_out, nb_in = rows_out // t_out, rows_in // t_in
    assert rows_out % t_out == 0 and rows_in % t_in == 0 and t_in >= t_out
    blk, off, lo, hi = ([[0] * nb_out for _ in range(2)] for _ in range(4))
    for i in range(nb_out):
        hits = [r for r in runs if r[0] < (i + 1) * t_out and r[1] > i * t_out]
        assert len(hits) <= 2
        for s, (o_lo, o_hi, s_lo) in enumerate(hits):
            lo[s][i] = max(o_lo, i * t_out) - i * t_out
            hi[s][i] = min(o_hi, (i + 1) * t_out) - i * t_out
            first = i * t_out + lo[s][i] - o_lo + s_lo
            blk[s][i] = min(first // t_in, nb_in - 1)
            off[s][i] = first - lo[s][i] - blk[s][i] * t_in
    table = jnp.asarray([blk[0], off[0], lo[0], hi[0], blk[1], off[1], lo[1], hi[1]], jnp.int32)

    def body(tab, a0, a1, b0, b1, o_ref):
        i = pl.program_id(0)
        o_ref[...] = jnp.zeros_like(o_ref)
        r = lax.broadcasted_iota(jnp.int32, (t_out, t_in), 0)
        k = lax.broadcasted_iota(jnp.int32, (t_out, t_in), 1)
        for s, (first, second) in enumerate(((a0, a1), (b0, b1))):
            off_s, lo_s, hi_s = tab[4 * s + 1, i], tab[4 * s + 2, i], tab[4 * s + 3, i]
            live = (r >= lo_s) & (r < hi_s)

            @pl.when(hi_s > lo_s)
            def _():
                sel = (live & (k == r + off_s)).astype(BF16)
                o_ref[...] += _dot(sel, first[...]).astype(o_ref.dtype)

            @pl.when((hi_s > lo_s) & (off_s + hi_s > t_in))
            def _():
                sel = (live & (k == r + off_s - t_in)).astype(BF16)
                o_ref[...] += _dot(sel, second[...]).astype(o_ref.dtype)

    def in_spec(s, nxt):
        return pl.BlockSpec((t_in, cols), lambda i, tab: (jnp.minimum(tab[4 * s, i] + nxt, nb_in - 1), 0))

    return pl.pallas_call(
        body, name=name,
        grid_spec=pltpu.PrefetchScalarGridSpec(
            num_scalar_prefetch=1, grid=(nb_out,),
            in_specs=[in_spec(0, 0), in_spec(0, 1), in_spec(1, 0), in_spec(1, 1)],
            out_specs=pl.BlockSpec((t_out, cols), lambda i, tab: (i, 0))),
        out_shape=jax.ShapeDtypeStruct((rows_out, cols), src.dtype), compiler_params=_cp("parallel"),
    )(table, src, src, src, src)


TL = 256


def _rms_fwd(x, w, name, after=None):
    n, d = x.shape

    def body(x_ref, w_ref, *rest):
        xv = x_ref[...]
        r = lax.rsqrt(jnp.mean(xv * xv, axis=-1, keepdims=True) + EPS)
        rest[-1][...] = (xv * r * w_ref[...]).astype(BF16)

    extra = [] if after is None else [after]
    return pl.pallas_call(
        body, name=name, grid=(n // TL,),
        in_specs=[pl.BlockSpec((TL, d), lambda i: (i, 0)), pl.BlockSpec((1, d), lambda i: (0, 0))] + [ANY] * len(extra),
        out_specs=pl.BlockSpec((TL, d), lambda i: (i, 0)),
        out_shape=jax.ShapeDtypeStruct((n, d), BF16), compiler_params=_cp("parallel"),
    )(x, w, *extra)


def _rms_bwd(dhn, x, w, dres, name, after=None):
    n, d = x.shape

    def body(g_ref, x_ref, w_ref, r_ref, *rest):
        dx_ref, dxb_ref, gw_ref = rest[-3:]
        xv = x_ref[...]
        r = lax.rsqrt(jnp.mean(xv * xv, axis=-1, keepdims=True) + EPS)
        xh = xv * r
        gv = g_ref[...]
        g = gv * w_ref[...]
        dx = r_ref[...] + r * (g - xh * jnp.mean(g * xh, axis=-1, keepdims=True))
        dx_ref[...] = dx
        dxb_ref[...] = dx.astype(BF16)

        @pl.when(pl.program_id(0) == 0)
        def _():
            gw_ref[...] = jnp.zeros_like(gw_ref)

        gw_ref[...] += jnp.sum(gv * xh, axis=0, keepdims=True)

    extra = [] if after is None else [after]
    row = pl.BlockSpec((TL, d), lambda i: (i, 0))
    vec = pl.BlockSpec((1, d), lambda i: (0, 0))
    return pl.pallas_call(
        body, name=name, grid=(n // TL,),
        in_specs=[row, row, vec, row] + [ANY] * len(extra), out_specs=[row, row, vec],
        out_shape=[jax.ShapeDtypeStruct((n, d), F32), jax.ShapeDtypeStruct((n, d), BF16),
                   jax.ShapeDtypeStruct((1, d), F32)],
        compiler_params=_cp("arbitrary"),
    )(dhn, x, w, dres, *extra)


def _final(h2, w, target):
    n, d = h2.shape

    def body(x_ref, w_ref, t_ref, dx_ref, dxb_ref, gw_ref, loss_ref):
        xv = x_ref[...]
        r = lax.rsqrt(jnp.mean(xv * xv, axis=-1, keepdims=True) + EPS)
        xh = xv * r
        diff = xh * w_ref[...] - t_ref[...]
        gv = diff * (1.0 / d)
        g = gv * w_ref[...]
        dx = r * (g - xh * jnp.mean(g * xh, axis=-1, keepdims=True))
        dx_ref[...] = dx
        dxb_ref[...] = dx.astype(BF16)

        @pl.when(pl.program_id(0) == 0)
        def _():
            gw_ref[...] = jnp.zeros_like(gw_ref)
            loss_ref[...] = jnp.zeros_like(loss_ref)

        gw_ref[...] += jnp.sum(gv * xh, axis=0, keepdims=True)
        part = 0.5 * jnp.sum(jnp.mean(diff * diff, axis=-1, keepdims=True), axis=0, keepdims=True)
        loss_ref[...] += jnp.broadcast_to(part, loss_ref.shape)

    row = pl.BlockSpec((TL, d), lambda i: (i, 0))
    vec = pl.BlockSpec((1, d), lambda i: (0, 0))
    return pl.pallas_call(
        body, name="final_loss", grid=(n // TL,),
        in_specs=[row, vec, row], out_specs=[row, row, vec, pl.BlockSpec((8, 128), lambda i: (0, 0))],
        out_shape=[jax.ShapeDtypeStruct((n, d), F32), jax.ShapeDtypeStruct((n, d), BF16),
                   jax.ShapeDtypeStruct((1, d), F32), jax.ShapeDtypeStruct((8, 128), F32)],
        compiler_params=_cp("arbitrary"),
    )(h2, w, target)


CT = 512
CL = 512


def _lagged(xf, taps, rows):
    return [xf[8:8 + rows]] + [pltpu.roll(xf, s, 0)[8:8 + rows] for s in range(1, taps)]


def _shift_up(x, u, n):
    if u == 0:
        return x[0:n]
    return pltpu.roll(x, x.shape[0] - u, 0)[0:n]


def _conv_pre(lagged, w_ref, b_ref, taps):
    pre = b_ref[...]
    for k in range(taps):
        pre = pre + w_ref[k:k + 1, :] * lagged[taps - 1 - k]
    return pre


def _conv_back(e, w_ref, taps):
    dx = w_ref[taps - 1:taps, :] * e[0:CL]
    for k in range(taps - 1):
        dx = dx + w_ref[k:k + 1, :] * _shift_up(e, taps - 1 - k, CL)
    return dx


def _halo_specs(n, col_of):
    per = CL // 8
    cur = pl.BlockSpec((CL, CT), lambda j, i, *_: (i, col_of(j)))
    prev = pl.BlockSpec((8, CT), lambda j, i, *_: (jnp.maximum(i * per - 1, 0), col_of(j)))
    nxt = pl.BlockSpec((8, CT), lambda j, i, *_: (jnp.minimum((i + 1) * per, n // 8 - 1), col_of(j)))
    return prev, cur, nxt


def _conv_a_fwd(proj, w, b):
    n = proj.shape[0]
    off = OFF_XBC // CT

    def body(p_ref, x_ref, w_ref, b_ref, o_ref):
        p8 = jnp.where(pl.program_id(1) > 0, p_ref[...], 0.0)
        xf = jnp.concatenate([p8, x_ref[...]], axis=0)
        o_ref[...] = _silu(_conv_pre(_lagged(xf, KA, CL), w_ref, b_ref, KA))

    prev, cur, _ = _halo_specs(n, lambda j: j + off)
    return pl.pallas_call(
        body, name="conv_a_fwd", grid=(CONVD // CT, n // CL),
        in_specs=[prev, cur, pl.BlockSpec((KA, CT), lambda j, i: (0, j)), pl.BlockSpec((1, CT), lambda j, i: (0, j))],
        out_specs=pl.BlockSpec((CL, CT), lambda j, i: (i, j)),
        out_shape=jax.ShapeDtypeStruct((n, CONVD), F32), compiler_params=_cp("parallel", "parallel"),
    )(proj, proj, w, b)


def _conv_a_bwd(proj, dout, w, b, col0, name):
    n, width = dout.shape
    off = (OFF_XBC + col0) // CT
    woff = col0 // CT
    nl = n // CL

    def body(p_ref, x_ref, n_ref, d_ref, dn_ref, w_ref, b_ref, dx_ref, dw_ref, db_ref):
        i = pl.program_id(1)
        xf = jnp.concatenate([jnp.where(i > 0, p_ref[...], 0.0), x_ref[...], n_ref[...]], axis=0)
        lag = _lagged(xf, KA, CL + 8)
        de = jnp.concatenate([d_ref[...], jnp.where(i < nl - 1, dn_ref[...], 0.0)], axis=0)
        se = de * _dsilu(_conv_pre(lag, w_ref, b_ref, KA))
        dx_ref[...] = _conv_back(se, w_ref, KA).astype(BF16)

        @pl.when(i == 0)
        def _():
            dw_ref[...] = jnp.zeros_like(dw_ref)
            db_ref[...] = jnp.zeros_like(db_ref)

        sc = se[0:CL]
        for k in range(KA):
            dw_ref[k:k + 1, :] += jnp.sum(sc * lag[KA - 1 - k][0:CL], axis=0, keepdims=True)
        db_ref[...] += jnp.sum(sc, axis=0, keepdims=True)

    prev, cur, nxt = _halo_specs(n, lambda j: j + off)
    _, dcur, dnxt = _halo_specs(n, lambda j: j)
    wspec = pl.BlockSpec((KA, CT), lambda j, i: (0, j + woff))
    bspec = pl.BlockSpec((1, CT), lambda j, i: (0, j + woff))
    return pl.pallas_call(
        body, name=name, grid=(width // CT, nl),
        in_specs=[prev, cur, nxt, dcur, dnxt, wspec, bspec],
        out_specs=[pl.BlockSpec((CL, CT), lambda j, i: (i, j)), pl.BlockSpec((KA, CT), lambda j, i: (0, j)),
                   pl.BlockSpec((1, CT), lambda j, i: (0, j))],
        out_shape=[jax.ShapeDtypeStruct((n, width), BF16), jax.ShapeDtypeStruct((KA, width), F32),
                   jax.ShapeDtypeStruct((1, width), F32)],
        compiler_params=_cp("parallel", "arbitrary"),
    )(proj, proj, proj, dout, dout, w, b)


def _conv_ffn_fwd(up, w, b):
    n = up.shape[0]
    nb = DFF // CT

    def body(pg_ref, g_ref, pv_ref, v_ref, wg_ref, bg_ref, wv_ref, bv_ref, o_ref):
        inner = pl.program_id(1) > 0
        gf = jnp.concatenate([jnp.where(inner, pg_ref[...], 0.0), g_ref[...]], axis=0)
        vf = jnp.concatenate([jnp.where(inner, pv_ref[...], 0.0), v_ref[...]], axis=0)
        gc = _conv_pre(_lagged(gf, KF, CL), wg_ref, bg_ref, KF)
        vc = _conv_pre(_lagged(vf, KF, CL), wv_ref, bv_ref, KF)
        o_ref[...] = (_silu(gc) * vc).astype(BF16)

    gp, gcur, _ = _halo_specs(n, lambda j: j)
    vp, vcur, _ = _halo_specs(n, lambda j: j + nb)
    return pl.pallas_call(
        body, name="conv_ffn_fwd", grid=(nb, n // CL),
        in_specs=[gp, gcur, vp, vcur,
                  pl.BlockSpec((KF, CT), lambda j, i: (0, j)), pl.BlockSpec((1, CT), lambda j, i: (0, j)),
                  pl.BlockSpec((KF, CT), lambda j, i: (0, j + nb)), pl.BlockSpec((1, CT), lambda j, i: (0, j + nb))],
        out_specs=pl.BlockSpec((CL, CT), lambda j, i: (i, j)),
        out_shape=jax.ShapeDtypeStruct((n, DFF), BF16), compiler_params=_cp("parallel", "parallel"),
    )(up, up, up, up, w, b, w, b)


def _conv_ffn_bwd(up, dact, w, b):
    n = up.shape[0]
    nb = DFF // CT
    nl = n // CL

    def body(pg_ref, g_ref, ng_ref, pv_ref, v_ref, nv_ref, d_ref, dn_ref, wg_ref, bg_ref, wv_ref, bv_ref,
             dxg_ref, dxv_ref, dwg_ref, dwv_ref, dbg_ref, dbv_ref):
        i = pl.program_id(1)
        gf = jnp.concatenate([jnp.where(i > 0, pg_ref[...], 0.0), g_ref[...], ng_ref[...]], axis=0)
        vf = jnp.concatenate([jnp.where(i > 0, pv_ref[...], 0.0), v_ref[...], nv_ref[...]], axis=0)
        glag, vlag = _lagged(gf, KF, CL + 8), _lagged(vf, KF, CL + 8)
        de = jnp.concatenate([d_ref[...], jnp.where(i < nl - 1, dn_ref[...], 0.0)], axis=0).astype(F32)
        gc = _conv_pre(glag, wg_ref, bg_ref, KF)
        vc = _conv_pre(vlag, wv_ref, bv_ref, KF)
        sg = _sig(gc)
        dgc = de * vc * (sg * (1.0 + gc * (1.0 - sg)))
        dvc = de * (gc * sg)
        dxg_ref[...] = _conv_back(dgc, wg_ref, KF).astype(BF16)
        dxv_ref[...] = _conv_back(dvc, wv_ref, KF).astype(BF16)

        @pl.when(i == 0)
        def _():
            for r in (dwg_ref, dwv_ref, dbg_ref, dbv_ref):
                r[...] = jnp.zeros_like(r)

        for e, lag, dw_ref, db_ref in ((dgc, glag, dwg_ref, dbg_ref), (dvc, vlag, dwv_ref, dbv_ref)):
            ec = e[0:CL]
            for k in range(KF):
                dw_ref[k:k + 1, :] += jnp.sum(ec * lag[KF - 1 - k][0:CL], axis=0, keepdims=True)
            db_ref[...] += jnp.sum(ec, axis=0, keepdims=True)

    gp, gcur, gnx = _halo_specs(n, lambda j: j)
    vp, vcur, vnx = _halo_specs(n, lambda j: j + nb)
    wcol = lambda o: (pl.BlockSpec((KF, CT), lambda j, i: (0, j + o)), pl.BlockSpec((1, CT), lambda j, i: (0, j + o)))
    wg, bg = wcol(0)
    wv, bv = wcol(nb)
    dxs = pl.BlockSpec((CL, CT), lambda j, i: (i, j))
    outs = pl.pallas_call(
        body, name="conv_ffn_bwd", grid=(nb, nl),
        in_specs=[gp, gcur, gnx, vp, vcur, vnx, gcur, gnx, wg, bg, wv, bv],
        out_specs=[dxs, dxs, wg, wg, bg, bg],
        out_shape=[jax.ShapeDtypeStruct((n, DFF), BF16)] * 2 + [jax.ShapeDtypeStruct((KF, DFF), F32)] * 2
        + [jax.ShapeDtypeStruct((1, DFF), F32)] * 2,
        compiler_params=_cp("parallel", "arbitrary"),
    )(up, up, up, up, up, up, dact, dact, w, b, w, b)
    return [jnp.concatenate(outs[k:k + 2], axis=1) for k in (0, 2, 4)]


HL = 128
HS = 128


def _split3(x):
    hi = x.astype(BF16)
    r1 = x - hi.astype(F32)
    mid = r1.astype(BF16)
    return hi, mid, (r1 - mid.astype(F32)).astype(BF16)


def _dot3(x, m):
    hi, mid, lo = _split3(x)
    return _dot(hi, m) + _dot(mid, m) + _dot(lo, m)


def _tri():
    row = lax.broadcasted_iota(jnp.int32, (Q, Q), 0)
    col = lax.broadcasted_iota(jnp.int32, (Q, Q), 1)
    return row >= col, row <= col


def _ssd_prep(dtraw, hp, emat):
    n = dtraw.shape[0]

    def body(d_ref, hp_ref, e_ref, dt_ref, s_ref, st_ref, dte_ref, se_ref):
        lower, upper = _tri()
        dt = _softplus(d_ref[...] + hp_ref[0:1, :])
        da = dt * (-jnp.exp(hp_ref[1:2, :]))
        s = _dot(lower.astype(F32), da, prec=HI)
        dt_ref[...] = dt
        s_ref[...] = s
        st_ref[...] = _dot(da, upper.astype(F32), TN, prec=HI)
        e = e_ref[...]
        dte_ref[...] = _dot3(dt, e)
        se_ref[...] = _dot3(s, e)

    row = pl.BlockSpec((Q, HL), lambda c: (c, 0))
    wide = pl.BlockSpec((Q, DI), lambda c: (c, 0))
    return pl.pallas_call(
        body, name="ssd_prep", grid=(n // Q,),
        in_specs=[pl.BlockSpec((Q, HL), lambda c: (c, NMAIN // HL)), pl.BlockSpec((8, HL), lambda c: (0, 0)),
                  pl.BlockSpec((HL, DI), lambda c: (0, 0))],
        out_specs=[row, row, pl.BlockSpec((HL, Q), lambda c: (0, c)), wide, wide],
        out_shape=[jax.ShapeDtypeStruct((n, HL), F32)] * 2 + [jax.ShapeDtypeStruct((HL, n), F32)]
        + [jax.ShapeDtypeStruct((n, DI), F32)] * 2,
        compiler_params=_cp("parallel"),
    )(dtraw, hp, emat)


def _ssd_post(ds_e, ddt_e, tsum, dsh, dtraw, dt, hp, emat_t):
    n = dtraw.shape[0]

    def body(dse_ref, dde_ref, ts_ref, dsh_ref, d_ref, dt_ref, hp_ref, et_ref, draw_ref, ps_ref):
        _, upper = _tri()
        et = et_ref[...]
        a = -jnp.exp(hp_ref[1:2, :])
        rows = lax.broadcasted_iota(jnp.int32, (Q, HL), 0)
        ds_t = _dot3(jnp.broadcast_to(ts_ref[...], (8, DI)), et)[0:1, :]
        ds = _dot3(dse_ref[...], et) + dsh_ref[...] + jnp.where(rows == Q - 1, ds_t, 0.0)
        d_a = _dot(upper.astype(F32), ds, prec=HI)
        draw = (_dot3(dde_ref[...], et) + d_a * a) * _sig(d_ref[...] + hp_ref[0:1, :])
        draw_ref[...] = draw

        @pl.when(pl.program_id(0) == 0)
        def _():
            ps_ref[...] = jnp.zeros_like(ps_ref)

        ps_ref[0:1, :] += jnp.sum(draw, axis=0, keepdims=True)
        ps_ref[1:2, :] += jnp.sum(d_a * dt_ref[...], axis=0, keepdims=True) * a

    row = pl.BlockSpec((Q, HL), lambda c: (c, 0))
    wide = pl.BlockSpec((Q, DI), lambda c: (c, 0))
    small = pl.BlockSpec((8, HL), lambda c: (0, 0))
    return pl.pallas_call(
        body, name="ssd_post", grid=(n // Q,),
        in_specs=[wide, wide, pl.BlockSpec((None, 1, DI), lambda c: (c, 0, 0)), row,
                  pl.BlockSpec((Q, HL), lambda c: (c, NMAIN // HL)), row, small,
                  pl.BlockSpec((DI, HL), lambda c: (0, 0))],
        out_specs=[row, small],
        out_shape=[jax.ShapeDtypeStruct((n, HL), F32), jax.ShapeDtypeStruct((8, HL), F32)],
        compiler_params=_cp("arbitrary"),
    )(ds_e, ddt_e, tsum, dsh, dtraw, dt, hp, emat_t)


def _ssd_specs(nc, rev):
    cc = (lambda c: nc - 1 - c) if rev else (lambda c: c)
    return [
        pl.BlockSpec((Q, GW), lambda g, c: (cc(c), g)),
        pl.BlockSpec((Q, NS), lambda g, c: (cc(c), DI // NS + g)),
        pl.BlockSpec((Q, NS), lambda g, c: (cc(c), (DI + NG * NS) // NS + g)),
        pl.BlockSpec((None, Q, 8), lambda g, c: (g, cc(c), 0)),
        pl.BlockSpec((8, Q), lambda g, c: (g, cc(c))),
        pl.BlockSpec((Q, GW), lambda g, c: (cc(c), g)),
        pl.BlockSpec((Q, GW), lambda g, c: (cc(c), g)),
        pl.BlockSpec((1, GW), lambda g, c: (0, g)),
    ]


def _ssd_fwd(xbc, s8, s_t, dt_e, s_e, dexp):
    n = xbc.shape[0]
    nc = n // Q

    def body(xs_ref, b_ref, c_ref, sc_ref, sr_ref, dte_ref, se_ref, dexp_ref, y_ref, sp_ref, st):
        @pl.when(pl.program_id(1) == 0)
        def _():
            st[...] = jnp.zeros_like(st)

        lower, _ = _tri()
        s_c, s_r, dt_e, s_e = sc_ref[...], sr_ref[...], dte_ref[...], se_ref[...]
        xs = xs_ref[...]
        x = xs * dt_e
        xb = x.astype(BF16)
        bb, cb = b_ref[...].astype(BF16), c_ref[...].astype(BF16)
        cbm = _dot(cb, bb, NT)
        st_e = s_e[Q - 1:Q, :]
        sprev = st[...]
        sp_ref[...] = sprev
        yoff = _dot(cb, sprev.astype(BF16)) * jnp.exp(s_e) + dexp_ref[...] * xs
        for h in range(HPG):
            sl = slice(h * HD, (h + 1) * HD)
            lm = jnp.where(lower, jnp.exp(jnp.minimum(s_c[:, h:h + 1] - s_r[h:h + 1, :], 0.0)), 0.0)
            y_ref[:, sl] = _dot((cbm * lm).astype(BF16), xb[:, sl]) + yoff[:, sl]
        w = (x * jnp.exp(st_e - s_e)).astype(BF16)
        st[...] = jnp.exp(st_e) * sprev + _dot(bb, w, TN)

    return pl.pallas_call(
        body, name="ssd_fwd", grid=(NG, nc), in_specs=_ssd_specs(nc, False),
        out_specs=[pl.BlockSpec((Q, GW), lambda g, c: (c, g)),
                   pl.BlockSpec((None, None, NS, GW), lambda g, c: (c, g, 0, 0))],
        out_shape=[jax.ShapeDtypeStruct((n, DI), F32), jax.ShapeDtypeStruct((nc, NG, NS, GW), F32)],
        scratch_shapes=[pltpu.VMEM((NS, GW), F32)],
        compiler_params=_cp("parallel", "arbitrary"),
    )(xbc, xbc, xbc, s8, s_t, dt_e, s_e, dexp)


def _ssd_bwd(xbc, s8, s_t, dt_e, s_e, dexp, sprev_all, dy):
    n = xbc.shape[0]
    nc = n // Q
    rc = lambda c: nc - 1 - c

    def body(xs_ref, b_ref, c_ref, sc_ref, sr_ref, dte_ref, se_ref, dexp_ref, sp_ref, dy_ref,
             dxs_ref, db_ref, dc_ref, dse_ref, dde_ref, ts_ref, dsh_ref, pd_ref, dst, dxbuf, dcbuf, dcbtbuf):
        @pl.when(pl.program_id(1) == 0)
        def _():
            dst[...] = jnp.zeros_like(dst)
            pd_ref[...] = jnp.zeros_like(pd_ref)

        lower, upper = _tri()
        s_c, s_r, dt_e, s_e = sc_ref[...], sr_ref[...], dte_ref[...], se_ref[...]
        xs = xs_ref[...]
        x = xs * dt_e
        xb = x.astype(BF16)
        bb, cb = b_ref[...].astype(BF16), c_ref[...].astype(BF16)
        cbm = _dot(cb, bb, NT)
        cbt = _dot(bb, cb, NT)
        st_e = s_e[Q - 1:Q, :]
        dec_out, dec_st, e_t = jnp.exp(s_e), jnp.exp(st_e - s_e), jnp.exp(st_e)
        dyv = dy_ref[...]
        dyb = dyv.astype(BF16)
        sprev = sp_ref[...]
        sb = sprev.astype(BF16)
        ds_in = dst[...]
        dsb = ds_in.astype(BF16)

        cs = _dot(cb, sb)
        dcs = (dyv * dec_out).astype(BF16)
        d_c = _dot(dcs, sb, NT)
        wf = x * dec_st
        d_w = _dot(bb, dsb)
        d_b = _dot(wf.astype(BF16), dsb, NT)
        tw = d_w * wf
        dse_ref[...] = dyv * cs * dec_out - tw
        dsh_ref[...] = jnp.zeros((Q, 8), F32)
        dcbuf[...] = jnp.zeros((Q, Q), F32)
        dcbtbuf[...] = jnp.zeros((Q, Q), F32)
        lane8 = lax.broadcasted_iota(jnp.int32, (1, 8), 1)
        for h in range(HPG):
            sl = slice(h * HD, (h + 1) * HD)
            sr_h = s_r[h:h + 1, :]
            for r0 in range(0, Q, HS):
                rows = slice(r0, r0 + HS)
                sc_s, cbm_s, cbt_s = s_c[rows, h:h + 1], cbm[rows], cbt[rows]
                lm = jnp.where(lower[rows], jnp.exp(jnp.minimum(sc_s - sr_h, 0.0)), 0.0)
                lmt = jnp.where(upper[rows], jnp.exp(jnp.minimum(sr_h - sc_s, 0.0)), 0.0)
                dxbuf[rows, sl] = _dot((cbt_s * lmt).astype(BF16), dyb[:, sl])
                dml = _dot(dyb[rows, sl], xb[:, sl], NT) * lm
                dmlt = _dot(xb[rows, sl], dyb[:, sl], NT) * lmt
                dcbuf[rows, :] += dml
                dcbtbuf[rows, :] += dmlt
                dsh = jnp.sum(dml * cbm_s, axis=1, keepdims=True) - jnp.sum(dmlt * cbt_s, axis=1, keepdims=True)
                dsh_ref[rows, :] += dsh * (lane8 == h).astype(F32)
        d_c = d_c + _dot(dcbuf[...].astype(BF16), bb)
        d_b = d_b + _dot(dcbtbuf[...].astype(BF16), cb)
        dx = d_w * dec_st + dxbuf[...]
        ts_ref[...] = jnp.sum(tw, axis=0, keepdims=True) + jnp.sum(ds_in * sprev, axis=0, keepdims=True) * e_t
        dde_ref[...] = dx * xs
        pd_ref[...] += jnp.sum(dyv * xs, axis=0, keepdims=True)
        dxs_ref[...] = dx * dt_e + dyv * dexp_ref[...]
        db_ref[...] = d_b
        dc_ref[...] = d_c
        dst[...] = e_t * ds_in + _dot(cb, dcs, TN)

    wide = pl.BlockSpec((Q, GW), lambda g, c: (rc(c), g))
    state = pl.BlockSpec((Q, NS), lambda g, c: (rc(c), g))
    in_specs = _ssd_specs(nc, True) + [pl.BlockSpec((None, None, NS, GW), lambda g, c: (rc(c), g, 0, 0)), wide]
    return pl.pallas_call(
        body, name="ssd_bwd", grid=(NG, nc), in_specs=in_specs,
        out_specs=[wide, state, state, wide, wide,
                   pl.BlockSpec((None, 1, GW), lambda g, c: (rc(c), 0, g)),
                   pl.BlockSpec((None, Q, 8), lambda g, c: (g, rc(c), 0)),
                   pl.BlockSpec((None, 1, GW), lambda g, c: (g, 0, 0))],
        out_shape=[jax.ShapeDtypeStruct((n, DI), F32), jax.ShapeDtypeStruct((n, NG * NS), F32),
                   jax.ShapeDtypeStruct((n, NG * NS), F32), jax.ShapeDtypeStruct((n, DI), F32),
                   jax.ShapeDtypeStruct((n, DI), F32), jax.ShapeDtypeStruct((nc, 1, DI), F32),
                   jax.ShapeDtypeStruct((NG, n, 8), F32), jax.ShapeDtypeStruct((NG, 1, GW), F32)],
        scratch_shapes=[pltpu.VMEM((NS, GW), F32), pltpu.VMEM((Q, GW), F32), pltpu.VMEM((Q, Q), F32),
                        pltpu.VMEM((Q, Q), F32)],
        compiler_params=_cp("parallel", "arbitrary"),
    )(xbc, xbc, xbc, s8, s_t, dt_e, s_e, dexp, sprev_all, dy)


GL = 256


def _gnorm_fwd(y, proj, w):
    n = y.shape[0]

    def body(y_ref, z_ref, w_ref, o_ref):
        for g in range(NG):
            sl = slice(g * GW, (g + 1) * GW)
            yz = y_ref[:, sl] * _silu(z_ref[:, sl])
            r = lax.rsqrt(jnp.mean(yz * yz, axis=-1, keepdims=True) + EPS)
            o_ref[:, sl] = (yz * r * w_ref[:, sl]).astype(BF16)

    row = pl.BlockSpec((GL, DI), lambda i: (i, 0))
    return pl.pallas_call(
        body, name="gnorm_fwd", grid=(n // GL,),
        in_specs=[row, row, pl.BlockSpec((1, DI), lambda i: (0, 0))], out_specs=row,
        out_shape=jax.ShapeDtypeStruct((n, DI), BF16), compiler_params=_cp("parallel"),
    )(y, proj, w)


def _gnorm_bwd(dyn, y, proj, w, after):
    n = y.shape[0]

    def body(d_ref, y_ref, z_ref, w_ref, _, dy_ref, dz_ref, gw_ref):
        @pl.when(pl.program_id(0) == 0)
        def _():
            gw_ref[...] = jnp.zeros_like(gw_ref)

        for g in range(NG):
            sl = slice(g * GW, (g + 1) * GW)
            yv, zv, dv = y_ref[:, sl], z_ref[:, sl], d_ref[:, sl]
            sz = _silu(zv)
            yz = yv * sz
            r = lax.rsqrt(jnp.mean(yz * yz, axis=-1, keepdims=True) + EPS)
            yh = yz * r
            gg = dv * w_ref[:, sl]
            dyz = r * (gg - yh * jnp.mean(gg * yh, axis=-1, keepdims=True))
            gw_ref[:, sl] += jnp.sum(dv * yh, axis=0, keepdims=True)
            dy_ref[:, sl] = dyz * sz
            dz_ref[:, sl] = (dyz * yv * _dsilu(zv)).astype(BF16)

    row = pl.BlockSpec((GL, DI), lambda i: (i, 0))
    vec = pl.BlockSpec((1, DI), lambda i: (0, 0))
    return pl.pallas_call(
        body, name="gnorm_bwd", grid=(n // GL,),
        in_specs=[row, row, row, vec, ANY], out_specs=[row, row, vec],
        out_shape=[jax.ShapeDtypeStruct((n, DI), F32), jax.ShapeDtypeStruct((n, DI), BF16),
                   jax.ShapeDtypeStruct((1, DI), F32)],
        compiler_params=_cp("arbitrary"),
    )(dyn, y, proj, w, after)


SL = 1024
SB = 8
SCB = NCH // SB


def _s5_in(proj, bre, bim, after=None):
    n = proj.shape[0]
    uoff = OFF_U // 128

    def body(u_ref, br_ref, bi_ref, *rest):
        or_ref, oi_ref = rest[-2:]
        u = u_ref[...].astype(BF16)
        or_ref[...] = _dot(u, br_ref[...])
        oi_ref[...] = _dot(u, bi_ref[...])

    extra = [] if after is None else [after]
    blk = pl.BlockSpec((None, 128, SCB), lambda i, j: (j, 0, 0))
    out = pl.BlockSpec((SL, SCB), lambda i, j: (i, j))
    return pl.pallas_call(
        body, name="s5_in", grid=(n // SL, SB),
        in_specs=[pl.BlockSpec((SL, 128), lambda i, j: (i, uoff + j)), blk, blk] + [ANY] * len(extra),
        out_specs=[out, out],
        out_shape=[jax.ShapeDtypeStruct((n, NCH), F32)] * 2, compiler_params=_cp("parallel", "parallel"),
    )(proj, bre, bim, *extra)


SC = 1024


def _s5_scan(vre, vim, tab, reverse, name):
    n = vre.shape[0]
    nl = n // SL
    ng = SL // 8
    ti = (lambda i: nl - 1 - i) if reverse else (lambda i: i)

    def body(re_ref, im_ref, tab_ref, ore_ref, oim_ref, cre, cim):
        @pl.when(pl.program_id(1) == 0)
        def _():
            cre[...] = jnp.zeros_like(cre)
            cim[...] = jnp.zeros_like(cim)

        def step(j, carry):
            cr, ci = carry
            jj = (ng - 1 - j) if reverse else j
            rows = pl.ds(pl.multiple_of(jj * 8, 8), 8)
            vr, vi = re_ref[rows, :], im_ref[rows, :]
            for t, k in enumerate((1, 2, 4)):
                sh = (8 - k) if reverse else k
                rr, ri = pltpu.roll(vr, sh, 0), pltpu.roll(vi, sh, 0)
                pr, pi = tab_ref[2 * t], tab_ref[2 * t + 1]
                vr, vi = vr + pr * rr - pi * ri, vi + pr * ri + pi * rr
            lr, li = tab_ref[6], tab_ref[7]
            vr, vi = vr + lr * cr - li * ci, vi + lr * ci + li * cr
            ore_ref[rows, :] = vr
            oim_ref[rows, :] = vi
            e = 0 if reverse else 7
            return (jnp.broadcast_to(vr[e:e + 1, :], (8, SC)), jnp.broadcast_to(vi[e:e + 1, :], (8, SC)))

        cr, ci = lax.fori_loop(0, ng, step, (cre[...], cim[...]))
        cre[...] = cr
        cim[...] = ci

    blk = pl.BlockSpec((SL, SC), lambda j, i: (ti(i), j))
    return pl.pallas_call(
        body, name=name, grid=(NCH // SC, nl),
        in_specs=[blk, blk, pl.BlockSpec((8, 8, SC), lambda j, i: (0, 0, j))], out_specs=[blk, blk],
        out_shape=[jax.ShapeDtypeStruct((n, NCH), F32)] * 2,
        scratch_shapes=[pltpu.VMEM((8, SC), F32), pltpu.VMEM((8, SC), F32)],
        compiler_params=_cp("parallel", "arbitrary"),
    )(vre, vim, tab)


def _s5_out(xre, xim, cre, cimn, proj, dvec):
    n = xre.shape[0]
    uoff = OFF_U // 128

    def body(xr_ref, xi_ref, cr_ref, ci_ref, u_ref, d_ref, y_ref, g_ref):
        y = (_dot(xr_ref[...].astype(BF16), cr_ref[...]) + _dot(xi_ref[...].astype(BF16), ci_ref[...])
             + d_ref[...] * u_ref[...])
        y_ref[...] = y
        g_ref[...] = _gelu(y).astype(BF16)

    xs = pl.BlockSpec((SL, SCB), lambda i, j: (i, j))
    blk = pl.BlockSpec((None, SCB, 128), lambda i, j: (j, 0, 0))
    out = pl.BlockSpec((SL, 128), lambda i, j: (i, j))
    return pl.pallas_call(
        body, name="s5_out", grid=(n // SL, SB),
        in_specs=[xs, xs, blk, blk, pl.BlockSpec((SL, 128), lambda i, j: (i, uoff + j)),
                  pl.BlockSpec((1, 128), lambda i, j: (0, j))],
        out_specs=[out, out],
        out_shape=[jax.ShapeDtypeStruct((n, DS5), F32), jax.ShapeDtypeStruct((n, DS5), BF16)],
        compiler_params=_cp("parallel", "parallel"),
    )(xre, xim, cre, cimn, proj, dvec)


def _s5_out_bwd(dg, ypre, crt, cimnt, proj, dvec, xre, xim):
    n = dg.shape[0]
    uoff = OFF_U // 128
    nl = n // SL

    def body(dg_ref, y_ref, cr_ref, ci_ref, u_ref, d_ref, xr_ref, xi_ref,
             gr_ref, gi_ref, dus_ref, gcr_ref, gci_ref, gd_ref):
        dy = dg_ref[...] * _dgelu(y_ref[...])
        dyb = dy.astype(BF16)
        gr_ref[...] = _dot(dyb, cr_ref[...])
        gi_ref[...] = _dot(dyb, ci_ref[...])
        dus_ref[...] = dy * d_ref[...]

        @pl.when(pl.program_id(1) == 0)
        def _():
            gcr_ref[...] = jnp.zeros_like(gcr_ref)
            gci_ref[...] = jnp.zeros_like(gci_ref)
            gd_ref[...] = jnp.zeros_like(gd_ref)

        gcr_ref[...] += _dot(xr_ref[...].astype(BF16), dyb, TN)
        gci_ref[...] -= _dot(xi_ref[...].astype(BF16), dyb, TN)
        gd_ref[...] += jnp.sum(dy * u_ref[...], axis=0, keepdims=True)

    u128 = pl.BlockSpec((SL, 128), lambda j, i: (i, j))
    xs = pl.BlockSpec((SL, SCB), lambda j, i: (i, j))
    blk = pl.BlockSpec((None, 128, SCB), lambda j, i: (j, 0, 0))
    gblk = pl.BlockSpec((None, SCB, 128), lambda j, i: (j, 0, 0))
    vec = pl.BlockSpec((1, 128), lambda j, i: (0, j))
    return pl.pallas_call(
        body, name="s5_out_bwd", grid=(SB, nl),
        in_specs=[u128, u128, blk, blk, pl.BlockSpec((SL, 128), lambda j, i: (i, uoff + j)), vec, xs, xs],
        out_specs=[xs, xs, u128, gblk, gblk, vec],
        out_shape=[jax.ShapeDtypeStruct((n, NCH), F32)] * 2 + [jax.ShapeDtypeStruct((n, DS5), F32)]
        + [jax.ShapeDtypeStruct((SB, SCB, 128), F32)] * 2 + [jax.ShapeDtypeStruct((1, DS5), F32)],
        compiler_params=_cp("parallel", "arbitrary"),
    )(dg, ypre, crt, cimnt, proj, dvec, xre, xim)


def _s5_in_bwd(are, aim, brt, bit, proj, dus, xre, xim):
    n = are.shape[0]
    uoff = OFF_U // 128
    per = SL // 8

    def body(ar_ref, ai_ref, br_ref, bi_ref, u_ref, dus_ref, xr_ref, xi_ref, pr_ref, pi_ref,
             du_ref, gbr_ref, gbi_ref, glr_ref, gli_ref):
        i = pl.program_id(1)
        ar, ai = ar_ref[...], ai_ref[...]
        arb, aib = ar.astype(BF16), ai.astype(BF16)
        du_ref[...] = (_dot(arb, br_ref[...]) + _dot(aib, bi_ref[...]) + dus_ref[...]).astype(BF16)

        @pl.when(i == 0)
        def _():
            for r in (gbr_ref, gbi_ref, glr_ref, gli_ref):
                r[...] = jnp.zeros_like(r)

        ub = u_ref[...].astype(BF16)
        gbr_ref[...] += _dot(arb, ub, TN)
        gbi_ref[...] += _dot(aib, ub, TN)
        row0 = lax.broadcasted_iota(jnp.int32, (SL, SCB), 0) == 0
        last_r = jnp.where(i > 0, pr_ref[7:8, :], 0.0)
        last_i = jnp.where(i > 0, pi_ref[7:8, :], 0.0)
        xpr = jnp.where(row0, last_r, pltpu.roll(xr_ref[...], 1, 0))
        xpi = jnp.where(row0, last_i, pltpu.roll(xi_ref[...], 1, 0))
        glr_ref[...] += jnp.sum(ar * xpr + ai * xpi, axis=0, keepdims=True)
        gli_ref[...] += jnp.sum(ai * xpr - ar * xpi, axis=0, keepdims=True)

    xs = pl.BlockSpec((SL, SCB), lambda j, i: (i, j))
    prev = pl.BlockSpec((8, SCB), lambda j, i: (jnp.maximum(i * per - 1, 0), j))
    blk = pl.BlockSpec((None, SCB, 128), lambda j, i: (j, 0, 0))
    u128 = pl.BlockSpec((SL, 128), lambda j, i: (i, j))
    vec = pl.BlockSpec((1, SCB), lambda j, i: (0, j))
    return pl.pallas_call(
        body, name="s5_in_bwd", grid=(SB, n // SL),
        in_specs=[xs, xs, blk, blk, pl.BlockSpec((SL, 128), lambda j, i: (i, uoff + j)), u128, xs, xs, prev, prev],
        out_specs=[u128, blk, blk, vec, vec],
        out_shape=[jax.ShapeDtypeStruct((n, DS5), BF16)] + [jax.ShapeDtypeStruct((SB, SCB, 128), F32)] * 2
        + [jax.ShapeDtypeStruct((1, NCH), F32)] * 2,
        compiler_params=_cp("parallel", "arbitrary"),
    )(are, aim, brt, bit, proj, dus, xre, xim, xre, xim)


MC = 1024


def _merge_specs():
    ga = pl.BlockSpec((TL, MC), lambda i, j: (i, OFF_GA // MC + j))
    gb = pl.BlockSpec((TL, MC), lambda i, j: (i, OFF_GB // MC + j))
    col = pl.BlockSpec((TL, MC), lambda i, j: (i, j))
    gate = pl.BlockSpec((TL, MC), lambda i, j: (i, D // MC + j))
    return ga, gb, col, gate


def _merge_fwd(proj, ya, vg):
    n = ya.shape[0]

    def body(ga_ref, gb_ref, ya_ref, v_ref, g_ref, o_ref):
        yb = v_ref[...] * _sig(g_ref[...])
        o_ref[...] = (_sig(ga_ref[...]) * ya_ref[...] + _sig(gb_ref[...]) * yb).astype(BF16)

    ga, gb, col, gate = _merge_specs()
    return pl.pallas_call(
        body, name="merge_fwd", grid=(n // TL, D // MC), in_specs=[ga, gb, col, col, gate], out_specs=col,
        out_shape=jax.ShapeDtypeStruct((n, D), BF16), compiler_params=_cp("parallel", "parallel"),
    )(proj, proj, ya, vg, vg)


def _merge_bwd(dm, proj, ya, vg):
    n = ya.shape[0]

    def body(dm_ref, ga_ref, gb_ref, ya_ref, v_ref, g_ref, dga_ref, dgb_ref, dya_ref, dv_ref, dg_ref):
        d = dm_ref[...]
        sa, sb, sg = _sig(ga_ref[...]), _sig(gb_ref[...]), _sig(g_ref[...])
        v = v_ref[...]
        yb = v * sg
        dga_ref[...] = (d * ya_ref[...] * sa * (1.0 - sa)).astype(BF16)
        dgb_ref[...] = (d * yb * sb * (1.0 - sb)).astype(BF16)
        dya_ref[...] = (d * sa).astype(BF16)
        dyb = d * sb
        dv_ref[...] = (dyb * sg).astype(BF16)
        dg_ref[...] = (dyb * v * sg * (1.0 - sg)).astype(BF16)

    ga, gb, col, gate = _merge_specs()
    o = jax.ShapeDtypeStruct((n, D), BF16)
    return pl.pallas_call(
        body, name="merge_bwd", grid=(n // TL, D // MC), in_specs=[col, ga, gb, col, col, gate],
        out_specs=[col] * 5, out_shape=[o] * 5, compiler_params=_cp("parallel", "parallel"),
    )(dm, proj, proj, ya, vg, vg)


def _adamw_update(wv, gv, mv, vv):
    nm = B1 * mv + (1.0 - B1) * gv
    nv = B2 * vv + (1.0 - B2) * (gv * gv)
    m_hat = nm / (1.0 - B1 ** STEP)
    v_hat = nv / (1.0 - B2 ** STEP)
    return -LR * (m_hat / (jnp.sqrt(v_hat) + AEPS) + WD * wv), nm, nv


def _adamw(w, g, m, v, name):
    r, c = w.shape
    tr = _pick(r, 128)

    def body(w_ref, g_ref, m_ref, v_ref, d_ref, nm_ref, nv_ref):
        d_ref[...], nm_ref[...], nv_ref[...] = _adamw_update(w_ref[...], g_ref[...], m_ref[...], v_ref[...])

    blk = pl.BlockSpec((tr, c), lambda i: (i, 0))
    o = jax.ShapeDtypeStruct((r, c), F32)
    return pl.pallas_call(
        body, name=name, grid=(r // tr,), in_specs=[blk] * 4, out_specs=[blk] * 3, out_shape=[o] * 3,
        compiler_params=_cp("parallel"),
    )(w, g, m, v)


def _adamw_halves(w, g_mine, g_other, m, v, cidx, name):
    _, r, c = w.shape
    hr, gc = g_mine.shape
    tr = _pick(hr, 128)
    nbh = hr // tr
    assert gc == c and 2 * hr - tr < r <= 2 * hr

    def body(cs, w_ref, gm_ref, go_ref, m_ref, v_ref, g_ref, d_ref, nm_ref, nv_ref):
        mine = pl.program_id(0) // nbh == cs[0]
        gv = jnp.where(mine, gm_ref[...], go_ref[...])
        g_ref[...] = gv
        d_ref[...], nm_ref[...], nv_ref[...] = _adamw_update(w_ref[...], gv, m_ref[...], v_ref[...])

    blk = pl.BlockSpec((None, tr, c), lambda i, cs: (0, i, 0))
    gmine = pl.BlockSpec((tr, gc), lambda i, cs: (jnp.where(i // nbh == cs[0], i % nbh, 0), 0))
    gother = pl.BlockSpec((tr, gc), lambda i, cs: (jnp.where(i // nbh == cs[0], 0, i % nbh), 0))
    o = jax.ShapeDtypeStruct((1, r, c), F32)
    return pl.pallas_call(
        body, name=name,
        grid_spec=pltpu.PrefetchScalarGridSpec(num_scalar_prefetch=1, grid=(2 * nbh,),
                                               in_specs=[blk, gmine, gother, blk, blk], out_specs=[blk] * 4),
        out_shape=[o] * 4, compiler_params=_cp("parallel"),
    )(cidx, w, g_mine, g_other, m, v)


def _chip_sum(part, sib, cidx, name):
    _, r, cc = part.shape
    hr = r // 2
    tr = _pick(hr, 256)

    def body(cs, p_ref, s_ref, o_ref):
        o_ref[...] = (p_ref[...].astype(F32) + s_ref[...].astype(F32)).astype(BF16)

    blk = pl.BlockSpec((None, tr, cc), lambda k, i, cs: (k, i, 0))
    return pl.pallas_call(
        body, name=name,
        grid_spec=pltpu.PrefetchScalarGridSpec(
            num_scalar_prefetch=1, grid=(4, hr // tr),
            in_specs=[pl.BlockSpec((None, None, tr, cc), lambda k, i, cs: (k, cs[0], i, 0)), blk], out_specs=blk),
        out_shape=jax.ShapeDtypeStruct((4, hr, cc), BF16), compiler_params=_cp("parallel", "parallel"),
    )(cidx, part.reshape(4, 2, hr, cc), sib)


def _shard_sum(own, got, sidx, name):
    _, hr, cc = own.shape
    tr = _pick(hr, 256)

    def body(cs, own_ref, g0, g1, g2, g3, o_ref):
        acc = None
        for k, g_ref in enumerate((g0, g1, g2, g3)):
            term = jnp.where(cs[0] == k, own_ref[...], g_ref[...]).astype(F32)
            acc = term if acc is None else acc + term
        o_ref[...] = acc

    def got_spec(k):
        return pl.BlockSpec((None, tr, cc), lambda i, cs: (jnp.where(cs[0] == k, (k + 1) % 4, k), i, 0))

    return pl.pallas_call(
        body, name=name,
        grid_spec=pltpu.PrefetchScalarGridSpec(
            num_scalar_prefetch=1, grid=(hr // tr,),
            in_specs=[pl.BlockSpec((None, tr, cc), lambda i, cs: (cs[0], i, 0))] + [got_spec(k) for k in range(4)],
            out_specs=pl.BlockSpec((tr, cc), lambda i, cs: (i, 0))),
        out_shape=jax.ShapeDtypeStruct((hr, cc), F32), compiler_params=_cp("parallel"),
    )(sidx, own, got, got, got, got)


def _sum_slabs(xs, name, out_dtype=F32):
    r, c = xs[0].shape
    tr = _pick(r, 256)

    def body(*refs):
        acc = refs[0][...].astype(F32)
        for ref in refs[1:-1]:
            acc = acc + ref[...].astype(F32)
        refs[-1][...] = acc.astype(out_dtype)

    blk = pl.BlockSpec((tr, c), lambda i: (i, 0))
    return pl.pallas_call(
        body, name=name, grid=(r // tr,), in_specs=[blk] * len(xs), out_specs=blk,
        out_shape=jax.ShapeDtypeStruct((r, c), out_dtype), compiler_params=_cp("parallel"),
    )(*xs)


def _place():
    return lax.axis_index("x"), lax.axis_index("y"), lax.axis_index("c")


def _gather_small(v, after):
    m_per, n = v.shape

    def body(x_ref, _, out_ref, send_sems, recv_sems, local_sem):
        x, y, c = _place()
        me, sibling = (x, y, c), (x, y, 1 - c)
        chips = [(1 - x, y), (x, 1 - y), (1 - x, 1 - y)]

        def rows(px, py, pc):
            return out_ref.at[pl.ds((4 * px + 2 * py + pc) * m_per, m_per), :]

        def copy(k, block, to, src=None):
            return pltpu.make_async_remote_copy(
                src_ref=rows(*block) if src is None else src, dst_ref=rows(*block),
                send_sem=send_sems.at[k], recv_sem=recv_sems.at[k], device_id=to, device_id_type=MESH)

        mine = pltpu.make_async_copy(x_ref, rows(*me), local_sem)
        mine.start()
        first = [copy(0, me, sibling, src=x_ref)]
        first += [copy(1 + j, me, (*chip, c), src=x_ref) for j, chip in enumerate(chips)]
        for cp in first:
            cp.start()
        passed = [copy(4 + j, (*chip, c), sibling) for j, chip in enumerate(chips)]
        for j, chip in enumerate(chips):
            copy(1 + j, (*chip, c), me).wait_recv()
            passed[j].start()
        copy(0, sibling, me).wait_recv()
        for j, chip in enumerate(chips):
            copy(4 + j, (*chip, 1 - c), me).wait_recv()
        for cp in first + passed:
            cp.wait_send()
        mine.wait()

    return pl.pallas_call(
        body, name="gather_small_%d" % m_per,
        out_shape=jax.ShapeDtypeStruct((8 * m_per, n), v.dtype),
        in_specs=[pl.BlockSpec(memory_space=pltpu.VMEM), ANY], out_specs=pl.BlockSpec(memory_space=pltpu.VMEM),
        scratch_shapes=[pltpu.SemaphoreType.DMA((7,)), pltpu.SemaphoreType.DMA((7,)), pltpu.SemaphoreType.DMA],
        compiler_params=pltpu.CompilerParams(vmem_limit_bytes=VMEM_LIMIT),
    )(v, after)


def _allsum_small(v, name, after):
    r = v.shape[0]
    g = _gather_small(v, after)
    return _sum_slabs([g[k * r:(k + 1) * r] for k in range(8)], name)


def _pass_halves(got, shards, name):
    nt = len(got)

    def body(*refs):
        ins, own, outs = refs[:nt], refs[nt:2 * nt], refs[2 * nt:3 * nt]
        send_sems, recv_sems = refs[3 * nt:]
        x, y, c = _place()
        s = 2 * x + y
        chips = [(1 - x, y), (x, 1 - y), (1 - x, 1 - y)]

        def half(ref, t, slot, h):
            hr = ins[t].shape[1] // 2
            return ref.at[slot, pl.ds(h * hr, hr), :]

        def copy(t, j, h):
            px, py = chips[j]
            return pltpu.make_async_remote_copy(
                src_ref=half(ins[t], t, 2 * px + py, h), dst_ref=half(outs[t], t, 2 * px + py, h),
                send_sem=send_sems.at[4 * t + j], recv_sem=recv_sems.at[4 * t + j],
                device_id=(x, y, 1 - c), device_id_type=MESH)

        def whole(t):
            return pltpu.make_async_remote_copy(
                src_ref=own[t], dst_ref=outs[t].at[s], send_sem=send_sems.at[4 * t + 3],
                recv_sem=recv_sems.at[4 * t + 3], device_id=(x, y, 1 - c), device_id_type=MESH)

        sends = [copy(t, j, c) for t in range(nt) for j in range(3)] + [whole(t) for t in range(nt)]
        for cp in sends:
            cp.start()
        for t in range(nt):
            for j in range(3):
                copy(t, j, 1 - c).wait_recv()
            whole(t).wait_recv()
        for cp in sends:
            cp.wait_send()

    return pl.pallas_call(
        body, name=name,
        out_shape=[jax.ShapeDtypeStruct(a.shape, a.dtype) for a in got],
        in_specs=[ANY] * (2 * nt), out_specs=[ANY] * nt, input_output_aliases={t: t for t in range(nt)},
        scratch_shapes=[pltpu.SemaphoreType.DMA((4 * nt,)), pltpu.SemaphoreType.DMA((4 * nt,))],
    )(*got, *shards)


HBM = pl.BlockSpec(memory_space=pltpu.HBM)
SEM = pl.BlockSpec(memory_space=pltpu.SEMAPHORE)
EFFECT = pltpu.SideEffectType.DATAFLOW_SIDE_EFFECTING


PER_TENSOR = {"gather": 3, "scatter": 3, "swap": 1, "pass": 4, "whole": 1}


def _ici_copies(kind, srcs, lands, send_sems, recv_sems):
    x, y, c = _place()
    s = 2 * x + y
    sib = (x, y, 1 - c)
    chips = [(1 - x, y), (x, 1 - y), (1 - x, 1 - y)]
    cps = []

    def add(src, dst, dev):
        k = len(cps)
        cps.append(pltpu.make_async_remote_copy(src_ref=src, dst_ref=dst, send_sem=send_sems[k], recv_sem=recv_sems[k],
                                                device_id=dev, device_id_type=MESH))

    for t in range(len(srcs)):
        if kind == "gather":
            hr = srcs[t].shape[0] // 2
            for px, py in chips:
                add(srcs[t].at[pl.ds(c * hr, hr), :], lands[t].at[s, pl.ds(c * hr, hr), :], (px, py, c))
        elif kind == "scatter":
            for px, py in chips:
                add(srcs[t].at[2 * px + py], lands[t].at[s], (px, py, c))
        elif kind == "swap":
            hr = srcs[t].shape[1] // 2
            add(srcs[t].at[:, pl.ds((1 - c) * hr, hr), :], lands[t], sib)
        elif kind == "pass":
            hr = srcs[t].shape[1] // 2
            for px, py in chips:
                half = srcs[t].at[2 * px + py, pl.ds(c * hr, hr), :]
                add(half, half, sib)
            add(lands[t], srcs[t].at[s], sib)
        else:
            add(srcs[t], lands[t], sib)
    return cps


def _ici_start(kind, srcs, after, name, lands=None):
    nt = len(srcs)
    nc = PER_TENSOR[kind] * nt
    hbm = lambda a: pltpu.with_memory_space_constraint(a, pltpu.HBM)
    if lands is None:
        shape = {"gather": lambda a: (4,) + a.shape, "scatter": lambda a: a.shape,
                 "swap": lambda a: (4, a.shape[1] // 2, a.shape[2]), "whole": lambda a: a.shape}[kind]
        lands = [lax.empty(shape(a), a.dtype) for a in srcs]

    def body(*refs):
        src, land = refs[:nt], refs[nt:2 * nt]
        outs = refs[2 * nt + 1:]
        for cp in _ici_copies(kind, src, land, outs[:nc], outs[nc:2 * nc]):
            cp.start()
        outs[-1][...] = jnp.zeros_like(outs[-1])

    outs = pl.pallas_call(
        body, name=name,
        out_shape=tuple([pltpu.SemaphoreType.DMA(())] * (2 * nc) + [pltpu.HBM(a.shape, a.dtype) for a in srcs]
                        + [pltpu.HBM(a.shape, a.dtype) for a in lands] + [jax.ShapeDtypeStruct((8, 128), F32)]),
        in_specs=[HBM] * (2 * nt) + [ANY],
        out_specs=tuple([SEM] * (2 * nc) + [HBM] * (2 * nt) + [pl.BlockSpec(memory_space=pltpu.VMEM)]),
        input_output_aliases={i: 2 * nc + i for i in range(2 * nt)},
        compiler_params=pltpu.CompilerParams(has_side_effects=EFFECT),
    )(*[hbm(a) for a in srcs], *[hbm(a) for a in lands], after)
    return outs[:2 * nc], outs[2 * nc:2 * nc + nt], outs[2 * nc + nt:2 * nc + 2 * nt], outs[-1]


def _ici_wait(kind, sems, srcs, lands, after, name):
    nt = len(srcs)
    nc = PER_TENSOR[kind] * nt

    def body(*refs):
        src, land = refs[:nt], refs[nt:2 * nt]
        sem = refs[2 * nt:2 * nt + 2 * nc]
        for cp in _ici_copies(kind, src, land, sem[:nc], sem[nc:]):
            cp.wait_send()
            cp.wait_recv()

    outs = pl.pallas_call(
        body, name=name,
        out_shape=tuple(pltpu.HBM(a.shape, a.dtype) for a in list(srcs) + list(lands)),
        in_specs=[HBM] * (2 * nt) + [SEM] * (2 * nc) + [ANY], out_specs=tuple([HBM] * (2 * nt)),
        input_output_aliases={i: i for i in range(2 * nt)},
        compiler_params=pltpu.CompilerParams(has_side_effects=EFFECT),
    )(*srcs, *lands, *sems, after)
    return outs[:nt], outs[nt:]


def _s5_params(lam_re, lam_im, log_dt, b_re, b_im):
    lr = jnp.minimum(lam_re, EIG_MAX)
    dt = jnp.exp(log_dt)[:, None]
    mag = jnp.exp(lr * dt)
    lbr, lbi = mag * jnp.cos(lam_im * dt), mag * jnp.sin(lam_im * dt)
    den = lr * lr + lam_im * lam_im
    qr = ((lbr - 1.0) * lr + lbi * lam_im) / den
    qi = (lbi * lr - (lbr - 1.0) * lam_im) / den
    bbr = qr[..., None] * b_re - qi[..., None] * b_im
    bbi = qr[..., None] * b_im + qi[..., None] * b_re
    return lbr, lbi, bbr, bbi


def _cmul(a, b):
    return a[0] * b[0] - a[1] * b[1], a[0] * b[1] + a[1] * b[0]


def _scan_table(lr, li, reverse):
    l1 = (lr.reshape(1, NCH), li.reshape(1, NCH))
    pows = [l1]
    for _ in range(7):
        pows.append(_cmul(pows[-1], l1))
    r = jnp.arange(8)[:, None]
    tabs = []
    for k in (1, 2, 4):
        keep = (r < 8 - k) if reverse else (r >= k)
        tabs += [jnp.where(keep, pows[k - 1][0], 0.0), jnp.where(keep, pows[k - 1][1], 0.0)]
    order = range(7, -1, -1) if reverse else range(8)
    tabs += [jnp.concatenate([pows[e][0] for e in order], axis=0), jnp.concatenate([pows[e][1] for e in order], axis=0)]
    return jnp.stack(tabs).astype(F32)


_EYE8 = lambda: jnp.eye(8, dtype=F32)


def _to_in_blocks(b):
    return jnp.einsum("jgpc,gh->jgchp", b.reshape(8, 8, 64, 16), _EYE8()).reshape(8, 128, 512)


def _to_out_blocks(cm):
    return jnp.einsum("jgcp,gh->jgphc", cm.reshape(8, 8, 16, 64), _EYE8()).reshape(8, 512, 128)


def _from_out_blocks(g):
    return jnp.einsum("jgphc,gh->jgpc", g.reshape(8, 8, 64, 8, 16), _EYE8()).reshape(64, 64, 16)


def _local_step(x, target, hn1, proj, p, hooks):
    n = x.shape[0]
    g = {}
    dtraw = proj
    xbc = _conv_a_fwd(proj, p["conv_a_w"], p["conv_a_b"])
    to_lanes = lambda v: jnp.pad(jnp.pad(v.reshape(NG, HPG), ((0, 0), (0, 8 - HPG))).reshape(1, 8 * NG),
                                 ((0, 0), (0, HL - 8 * NG)))
    from_lanes = lambda v: v[:, :8 * NG].reshape(-1, NG, 8)[:, :, :HPG].reshape(-1, NG * HPG)
    hp = jnp.concatenate([to_lanes(p["dt_bias"]), to_lanes(p["a_log"]), jnp.zeros((6, HL), F32)], axis=0)
    lane = jnp.arange(HL)[:, None]
    emat = ((lane < 8 * NG) & (lane % 8 < HPG)
            & (jnp.arange(DI)[None, :] // HD == HPG * (lane // 8) + lane % 8)).astype(BF16)
    dexp = jnp.repeat(p["d_a"].reshape(1, NG * HPG), HD, axis=1)
    dt, s_cum, s_t, dt_e, s_e = _ssd_prep(dtraw, hp, emat)
    s8 = s_cum[:, :8 * NG].reshape(n, NG, 8).transpose(1, 0, 2)
    yssd, sprev = _ssd_fwd(xbc, s8, s_t, dt_e, s_e, dexp)
    yn = _gnorm_fwd(yssd, proj, p["norm_a_w"])
    tok = hooks["late_start"](yn)
    (lbr, lbi, bbr, bbi), s5_vjp = jax.vjp(_s5_params, p["s5_lam_re"], p["s5_lam_im"], p["s5_log_dt"],
                                           p["s5_b_re"], p["s5_b_im"])
    bin_r, bin_i = _to_in_blocks(bbr), _to_in_blocks(bbi)
    cout_r, cout_in = _to_out_blocks(p["s5_c_re"]), _to_out_blocks(-p["s5_c_im"])
    bur, bui = _s5_in(proj, bin_r.astype(BF16), bin_i.astype(BF16), after=tok)
    xre, xim = _s5_scan(bur, bui, _scan_table(lbr, lbi, False), False, "s5_scan_fwd")
    ypre, g5 = _s5_out(xre, xim, cout_r.astype(BF16), cout_in.astype(BF16), proj, p["s5_d"])
    p = {**p, **hooks["late_weights"](ypre)}
    ya = _matmul(yn, p["w_proj_a"], "nn", "mm_proj")
    vg = _matmul(g5, p["w_s5_glu"], "nn", "mm_glu", b_stacked=True)
    merged = _merge_fwd(proj, ya, vg)
    h1 = _matmul(merged, p["w_out"], "nn", "mm_out", residual=x)
    hn2 = _rms_fwd(h1, p["norm_ffn_w"], "rms_ffn")
    up = _matmul(hn2, p["w_up"], "nn", "mm_up", tn=1408, b_stacked=True)
    act = _conv_ffn_fwd(up, p["conv_ffn_w"], p["conv_ffn_b"])
    h2 = _matmul(act, p["w_down"], "nn", "mm_down", tk=DFF // 2, residual=h1)
    dh2, dh2b, g["norm_final_w"], loss_blk = _final(h2, p["norm_final_w"], target)
    g["w_down"] = _matmul(act, dh2b, "tn", "mm_gw_down", out_dtype=BF16, tm=DFF // 4).reshape(4, DFF // 4, D)
    dact = _matmul(dh2b, p["w_down"], "nt", "mm_dact", out_dtype=BF16, tn=DFF // 4)
    dup, g["conv_ffn_w"], g["conv_ffn_b"] = _conv_ffn_bwd(up, dact, p["conv_ffn_w"], p["conv_ffn_b"])
    g["w_up"] = _matmul(hn2, dup, "tn", "mm_gw_up", out_dtype=BF16, tn=1408, out_stacked=True)
    tok = hooks["swap_start"](["w_up", "w_down"], g, "s1")
    dhn2 = _matmul(dup, p["w_up"], "nt", "mm_dhn2", tk=2816, b_stacked=True, after=tok)
    tok = hooks["scatter_go"]("s1", dhn2)
    dh1, dh1b, g["norm_ffn_w"] = _rms_bwd(dhn2, h1, p["norm_ffn_w"], dh2, "rms_ffn_bwd", after=tok)
    g["w_out"] = _matmul(merged, dh1b, "tn", "mm_gw_out", out_dtype=BF16).reshape(4, D // 4, D)
    dmerged = _matmul(dh1b, p["w_out"], "nt", "mm_dmerged")
    dga, dgb, dya, dval, dgate = _merge_bwd(dmerged, proj, ya, vg)
    dvg = jnp.concatenate([dval, dgate], axis=1)
    g["w_s5_glu"] = _matmul(g5, dvg, "tn", "mm_gw_glu", out_dtype=BF16, out_stacked=True)
    dg5 = _matmul(dvg, p["w_s5_glu"], "nt", "mm_dg5", b_stacked=True)
    tr = lambda b: b.transpose(0, 2, 1)
    gxr, gxi, dus, gcr, gci, g["s5_d"] = _s5_out_bwd(dg5, ypre, tr(cout_r).astype(BF16), tr(cout_in).astype(BF16),
                                                     proj, p["s5_d"], xre, xim)
    are, aim = _s5_scan(gxr, gxi, _scan_table(lbr, -lbi, True), True, "s5_scan_bwd")
    du, gbr, gbi, glr, gli = _s5_in_bwd(are, aim, tr(bin_r).astype(BF16), tr(bin_i).astype(BF16), proj, dus, xre, xim)
    g["s5_c_re"] = _from_out_blocks(gcr).transpose(0, 2, 1)
    g["s5_c_im"] = _from_out_blocks(gci).transpose(0, 2, 1)
    (g["s5_lam_re"], g["s5_lam_im"], g["s5_log_dt"], g["s5_b_re"], g["s5_b_im"]) = s5_vjp(
        (glr.reshape(64, 64), gli.reshape(64, 64), _from_out_blocks(gbr), _from_out_blocks(gbi)))
    g["w_proj_a"] = _matmul(yn, dya, "tn", "mm_gw_proj", out_dtype=BF16).reshape(4, DI // 4, D)
    tok = hooks["swap_start"](["w_proj_a", "w_s5_glu", "w_out"], g, "s2")
    dyn = _matmul(dya, p["w_proj_a"], "nt", "mm_dyn", after=tok)
    tok = hooks["scatter_go"]("s2", dyn)
    dyssd, dz, g["norm_a_w"] = _gnorm_bwd(dyn, yssd, proj, p["norm_a_w"], tok)
    dxs, dbm, dcm, ds_e, ddt_e, tsum, dsh8, pd = _ssd_bwd(xbc, s8, s_t, dt_e, s_e, dexp, sprev, dyssd)
    dsh = jnp.pad(dsh8.transpose(1, 0, 2).reshape(n, 8 * NG), ((0, 0), (0, HL - 8 * NG)))
    draw, ps = _ssd_post(ds_e, ddt_e, tsum, dsh, dtraw, dt, hp, emat.T)
    g["dt_bias"] = from_lanes(ps[0:1])
    g["a_log"] = from_lanes(ps[1:2])
    g["d_a"] = pd.reshape(NG * HPG, HD).sum(axis=1).reshape(1, NG * HPG)
    ddt = draw.astype(BF16)
    dxbc_parts, gcw, gcb = [], [], []
    for arr, col0, nm in ((dxs, 0, "conv_a_bwd_x"), (dbm, DI, "conv_a_bwd_b"), (dcm, DI + NG * NS, "conv_a_bwd_c")):
        dpart, gw_, gb_ = _conv_a_bwd(proj, arr, p["conv_a_w"], p["conv_a_b"], col0, nm)
        dxbc_parts.append(dpart)
        gcw.append(gw_)
        gcb.append(gb_)
    g["conv_a_w"] = jnp.concatenate(gcw, axis=1)
    g["conv_a_b"] = jnp.concatenate(gcb, axis=1)
    dproj = jnp.concatenate([dz] + dxbc_parts + [du, dga, dgb, ddt], axis=1)
    g_main = _matmul(dproj, hn1, "tn", "mm_gw_in", out_dtype=BF16, tm=896, tn=2048)
    g_dt = g_main[NMAIN:NMAIN + 8 * NG].reshape(NG, 8, D)[:, :HPG].reshape(NG * HPG, D)
    g_sh = _move_rows(g_main, RUNS_TO_SHARDS, 4 * WPAD, MT, MT, "rows_to_shards")
    g["w_in"] = lax.dynamic_update_slice(g_sh, g_dt, (DT_SHARD_ROW, 0)).reshape(4, WPAD, D)
    hooks["swap_start"](["w_in"], g, "s3")
    tok = hooks["scatter_go"]("s3", g_dt)
    dhn1 = _matmul(dproj, p["w_full"], "nn", "mm_dhn1", tk=4480, after=tok)
    gx, _, g["norm_mix_w"] = _rms_bwd(dhn1, x, p["norm_mix_w"], dh1, "rms_mix_bwd")
    return loss_blk, gx, g


BIG = ["w_in", "w_proj_a", "w_s5_glu", "w_out", "w_up", "w_down"]
SMALL = ["norm_mix_w", "conv_a_w", "conv_a_b", "dt_bias", "a_log", "d_a", "norm_a_w", "s5_lam_re", "s5_lam_im",
         "s5_log_dt", "s5_b_re", "s5_b_im", "s5_c_re", "s5_c_im", "s5_d", "norm_ffn_w", "conv_ffn_w", "conv_ffn_b",
         "norm_final_w"]
ORDER = ["norm_mix_w", "w_in", "conv_a_w", "conv_a_b", "dt_bias", "a_log", "d_a", "norm_a_w", "w_proj_a", "s5_lam_re",
         "s5_lam_im", "s5_log_dt", "s5_b_re", "s5_b_im", "s5_c_re", "s5_c_im", "s5_d", "w_s5_glu", "w_out",
         "norm_ffn_w", "w_up", "conv_ffn_w", "conv_ffn_b", "w_down", "norm_final_w"]
CONV_FULL = {"conv_a_w": (KA, CONVD), "conv_ffn_w": (KF, 2 * DFF)}


def _pack(arrs):
    flat = jnp.concatenate([a.reshape(-1).astype(F32) for a in arrs])
    total = flat.shape[0]
    padded = -(-total // 1024) * 1024
    return jnp.pad(flat, (0, padded - total)).reshape(padded // 128, 128)


def _unpack(block, shapes):
    flat = block.reshape(-1)
    out, at = [], 0
    for sh in shapes:
        size = math.prod(sh)
        out.append(flat[at:at + size].reshape(sh))
        at += size
    return out


def kernel(x, norm_mix_w, w_in, conv_a_w, conv_a_b, dt_bias, a_log, d_a, norm_a_w, w_proj_a, s5_lam_re, s5_lam_im, s5_log_dt, s5_b_re, s5_b_im, s5_c_re, s5_c_im, s5_d, w_s5_glu, w_out, norm_ffn_w, w_up, conv_ffn_w, conv_ffn_b, w_down, norm_final_w, loss_target, m_norm_mix_w, m_w_in, m_conv_a_w, m_conv_a_b, m_dt_bias, m_a_log, m_d_a, m_norm_a_w, m_w_proj_a, m_s5_lam_re, m_s5_lam_im, m_s5_log_dt, m_s5_b_re, m_s5_b_im, m_s5_c_re, m_s5_c_im, m_s5_d, m_w_s5_glu, m_w_out, m_norm_ffn_w, m_w_up, m_conv_ffn_w, m_conv_ffn_b, m_w_down, m_norm_final_w, v_norm_mix_w, v_w_in, v_conv_a_w, v_conv_a_b, v_dt_bias, v_a_log, v_d_a, v_norm_a_w, v_w_proj_a, v_s5_lam_re, v_s5_lam_im, v_s5_log_dt, v_s5_b_re, v_s5_b_im, v_s5_c_re, v_s5_c_im, v_s5_d, v_w_s5_glu, v_w_out, v_norm_ffn_w, v_w_up, v_conv_ffn_w, v_conv_ffn_b, v_w_down, v_norm_final_w):
    w = dict(norm_mix_w=norm_mix_w, w_in=w_in, conv_a_w=conv_a_w, conv_a_b=conv_a_b, dt_bias=dt_bias, a_log=a_log, d_a=d_a, norm_a_w=norm_a_w, w_proj_a=w_proj_a, s5_lam_re=s5_lam_re, s5_lam_im=s5_lam_im, s5_log_dt=s5_log_dt, s5_b_re=s5_b_re, s5_b_im=s5_b_im, s5_c_re=s5_c_re, s5_c_im=s5_c_im, s5_d=s5_d, w_s5_glu=w_s5_glu, w_out=w_out, norm_ffn_w=norm_ffn_w, w_up=w_up, conv_ffn_w=conv_ffn_w, conv_ffn_b=conv_ffn_b, w_down=w_down, norm_final_w=norm_final_w)
    m = dict(norm_mix_w=m_norm_mix_w, w_in=m_w_in, conv_a_w=m_conv_a_w, conv_a_b=m_conv_a_b, dt_bias=m_dt_bias, a_log=m_a_log, d_a=m_d_a, norm_a_w=m_norm_a_w, w_proj_a=m_w_proj_a, s5_lam_re=m_s5_lam_re, s5_lam_im=m_s5_lam_im, s5_log_dt=m_s5_log_dt, s5_b_re=m_s5_b_re, s5_b_im=m_s5_b_im, s5_c_re=m_s5_c_re, s5_c_im=m_s5_c_im, s5_d=m_s5_d, w_s5_glu=m_w_s5_glu, w_out=m_w_out, norm_ffn_w=m_norm_ffn_w, w_up=m_w_up, conv_ffn_w=m_conv_ffn_w, conv_ffn_b=m_conv_ffn_b, w_down=m_w_down, norm_final_w=m_norm_final_w)
    v = dict(norm_mix_w=v_norm_mix_w, w_in=v_w_in, conv_a_w=v_conv_a_w, conv_a_b=v_conv_a_b, dt_bias=v_dt_bias, a_log=v_a_log, d_a=v_d_a, norm_a_w=v_norm_a_w, w_proj_a=v_w_proj_a, s5_lam_re=v_s5_lam_re, s5_lam_im=v_s5_lam_im, s5_log_dt=v_s5_log_dt, s5_b_re=v_s5_b_re, s5_b_im=v_s5_b_im, s5_c_re=v_s5_c_re, s5_c_im=v_s5_c_im, s5_d=v_s5_d, w_s5_glu=v_w_s5_glu, w_out=v_w_out, norm_ffn_w=v_norm_ffn_w, w_up=v_w_up, conv_ffn_w=v_conv_ffn_w, conv_ffn_b=v_conv_ffn_b, w_down=v_w_down, norm_final_w=v_norm_final_w)
    xi, yi, ci = _place()
    chip = 2 * xi + yi

    cidx = jnp.reshape(ci, (1,)).astype(jnp.int32)
    sidx = jnp.reshape(chip, (1,)).astype(jnp.int32)

    tw = lambda a: jnp.transpose(a[0])[None]
    w["w_in"], m["w_in"], v["w_in"] = tw(w_in), tw(m_w_in), tw(v_w_in)
    shards = [w[k][0].astype(BF16) for k in BIG]
    shards[0] = jnp.pad(shards[0], ((0, WPAD - WSH), (0, 0)))

    late = {}

    def late_start(after):
        srcs, got = _ici_wait("gather", g_sems, g_srcs, g_lands, after, "gather_rest_wait")
        late["sems"], late["got"], late["srcs"], tok = _ici_start("pass", list(got), cidx, "pass_rest_start",
                                                                  lands=list(srcs))
        return tok

    def late_weights(after):
        full, _ = _ici_wait("pass", late["sems"], late["got"], late["srcs"], after, "pass_rest_wait")
        return {"w_proj_a": full[0].reshape(DI, D), "w_s5_glu": full[1], "w_out": full[2].reshape(D, D),
                "w_up": full[3], "w_down": full[4].reshape(DFF, D)}

    swaps, pending = {}, []

    def swap_start(names, g, tag):
        sems, parts, lands, tok = _ici_start("swap", [g[k] for k in names], cidx, "swap_start_" + tag)
        swaps[tag] = (names, sems, parts, lands)
        return tok

    def scatter_go(tag, after):
        names, sems, parts, lands = swaps[tag]
        parts, sib = _ici_wait("swap", sems, parts, lands, after, "swap_wait_" + tag)
        sums = [_chip_sum(parts[t], sib[t], cidx, "chip_sum_" + k) for t, k in enumerate(names)]
        sems, srcs, lands, tok = _ici_start("scatter", sums, cidx, "scatter_start_" + tag)
        pending.append((names, tag, sems, srcs, lands))
        return tok

    hooks = {"late_start": late_start, "late_weights": late_weights, "swap_start": swap_start,
             "scatter_go": scatter_go}
    conv_blocks = []
    for k, (taps, cols) in CONV_FULL.items():
        shard = jnp.where(ci == 0, w[k][0], 0.0)
        conv_blocks.append(lax.dynamic_update_slice_in_dim(jnp.zeros((taps, cols), F32), shard, chip * (cols // 4), 1))
    conv_full = _unpack(_allsum_small(_pack(conv_blocks), "sum_conv_w", cidx), [CONV_FULL[k] for k in CONV_FULL])

    half = D // 2
    sh_a, sh_b = shards[0][:, :half], shards[0][:, half:]
    a_sems, a_srcs, a_lands, a_tok = _ici_start("gather", [sh_a], conv_full[0], "gather_in_a_start")
    b_sems, b_srcs, b_lands, b_tok = _ici_start("gather", [sh_b], a_tok, "gather_in_b_start")
    hn1 = _rms_fwd(x[0], norm_mix_w, "rms_mix", after=b_tok)

    def w_in_part(sems, srcs, lands, after, tag):
        srcs, got = _ici_wait("gather", sems, srcs, lands, after, "gather_in_%s_wait" % tag)
        w_sh = _pass_halves(list(got), list(srcs), "pass_halves_in_" + tag)[0].reshape(4 * WPAD, half)
        w_dt = jnp.pad(w_sh[DT_SHARD_ROW:DT_SHARD_ROW + NG * HPG].reshape(NG, HPG, half),
                       ((0, 0), (0, 8 - HPG), (0, 0)))
        return lax.dynamic_update_slice(_move_rows(w_sh, RUNS_TO_MAIN, NFULL, MT, MT, "rows_to_main_" + tag),
                                        w_dt.reshape(8 * NG, half), (NMAIN, 0))

    w_a = w_in_part(a_sems, a_srcs, a_lands, hn1, "a")
    proj_a = _matmul(hn1[:, :half], w_a, "nt", "mm_in_a", tn=1920)
    w_b = w_in_part(b_sems, b_srcs, b_lands, proj_a, "b")
    g_sems, g_srcs, g_lands, token = _ici_start("gather", shards[1:], w_b, "gather_rest_start")
    proj = _matmul(hn1[:, half:], w_b, "nt", "mm_in_b", tn=1920, residual=proj_a, after=token)
    w_full = jnp.concatenate([w_a, w_b], axis=1)
    p = {
        "w_full": w_full,
        "conv_a_w": conv_full[0], "conv_ffn_w": conv_full[1],
        "conv_a_b": conv_a_b, "conv_ffn_b": conv_ffn_b,
        "norm_mix_w": norm_mix_w, "norm_a_w": norm_a_w, "norm_ffn_w": norm_ffn_w,
        "norm_final_w": norm_final_w.reshape(1, D),
        "dt_bias": dt_bias, "a_log": a_log, "d_a": d_a, "s5_d": s5_d,
        "s5_lam_re": s5_lam_re[0], "s5_lam_im": s5_lam_im[0], "s5_log_dt": s5_log_dt[0],
        "s5_b_re": s5_b_re[0], "s5_b_im": s5_b_im[0], "s5_c_re": s5_c_re[0], "s5_c_im": s5_c_im[0],
    }
    loss_blk, gx, g = _local_step(x[0], loss_target[0], hn1, proj, p, hooks)

    after, halves = gx, {}
    for names, tag, sems, srcs, lands in pending:
        srcs, got = _ici_wait("scatter", sems, srcs, lands, after, "scatter_wait_" + tag)
        for t, k in enumerate(names):
            halves[k] = _shard_sum(srcs[t], got[t], sidx, "shard_sum_" + k)
        after = halves[names[0]]
    w_sems, g_mine, w_lands, w_tok = _ici_start("whole", [halves[k] for k in BIG], cidx, "whole_start")

    small_shapes = [CONV_FULL.get(k, w[k].shape[1:] if k != "norm_final_w" else w[k].shape) for k in SMALL]
    small = _allsum_small(_pack([g[k] for k in SMALL] + [loss_blk[0:1, 0:1]]), "sum_small_grads", w_tok)
    small_grads = dict(zip(SMALL + ["loss"], _unpack(small, small_shapes + [(1,)])))
    for k, (taps, cols) in CONV_FULL.items():
        small_grads[k] = lax.dynamic_slice_in_dim(small_grads[k], chip * (cols // 4), cols // 4, axis=1)
    loss = small_grads.pop("loss").reshape(())

    grads, delta, new_m, new_v = {}, {}, {}, {}
    for k in SMALL:
        grads[k] = small_grads[k].reshape(w[k].shape)
    pk = lambda t: _pack([t[k] for k in SMALL])
    d_, m_, v_ = _adamw(pk(w), pk(grads), pk(m), pk(v), "adamw_small")
    shapes = [w[k].shape for k in SMALL]
    for k, dd, mm, vv in zip(SMALL, _unpack(d_, shapes), _unpack(m_, shapes), _unpack(v_, shapes)):
        delta[k], new_m[k], new_v[k] = dd, mm, vv
    g_mine, g_other = _ici_wait("whole", w_sems, g_mine, w_lands, d_, "whole_wait")
    for t, k in enumerate(BIG):
        outs = _adamw_halves(w[k], g_mine[t], g_other[t], m[k], v[k], cidx, "adamw_" + k)
        grads[k], delta[k], new_m[k], new_v[k] = [tw(o) for o in outs] if k == "w_in" else outs
    return (loss, gx[None], *[grads[k] for k in ORDER], *[delta[k] for k in ORDER],
            *[new_m[k] for k in ORDER], *[new_v[k] for k in ORDER])
```
